```python
import math
import jax, jax.numpy as jnp
from jax import lax
import numpy as np

D_MODEL = 1024
BATCH = 16
SEQ = 4096
DEPTH = 2

N_A_LAYERS = DEPTH // 2
N_B_LAYERS = DEPTH - N_A_LAYERS
HEAD_DIM = 64
MEM_LEN = 256
MEM_HEADS = 4
MEM_WIDTH = MEM_HEADS * HEAD_DIM
MIX_WIDTH = D_MODEL
TOK_WIDTH = MIX_WIDTH - MEM_WIDTH
POOL_WINDOWS = (2, 4, 8, 16)
POOL_GROUP = TOK_WIDTH // len(POOL_WINDOWS)
FOX_HEADS = TOK_WIDTH // HEAD_DIM
Q_BLOCK = 128
D_FF = ((8 * D_MODEL // 3 + 63) // 64) * 64
CONV_WIDTH = 3
DN_ALPHA = (2.0 * DEPTH) ** 0.25
DN_BETA = (8.0 * DEPTH) ** -0.25
LN_EPS = 1e-5

kernel_name = "yoco_pool_fox_memory_convffn"


def layer_norm(x, g, b):
    xf = x.astype(jnp.float32)
    mu = jnp.mean(xf, axis=-1, keepdims=True)
    var = jnp.mean(jnp.square(xf - mu), axis=-1, keepdims=True)
    y = (xf - mu) * lax.rsqrt(var + LN_EPS) * g.astype(jnp.float32) + b.astype(jnp.float32)
    return y.astype(x.dtype)


def causal_mean_minus_self(u, w):
    S = u.shape[1]
    uf = u.astype(jnp.float32)
    csum = jnp.cumsum(uf, axis=1)
    lag = jnp.pad(csum, ((0, 0), (w, 0), (0, 0)))[:, :S]
    count = jnp.minimum(jnp.arange(1, S + 1), w).astype(jnp.float32)[None, :, None]
    return ((csum - lag) / count - uf).astype(u.dtype)


def multiscale_pool(u, pool_w, pool_scale):
    B, S, _ = u.shape
    ug = u.reshape(B, S, len(POOL_WINDOWS), POOL_GROUP)
    pooled = jnp.stack([causal_mean_minus_self(ug[:, :, i], w)
                        for i, w in enumerate(POOL_WINDOWS)], axis=2)
    mixed = jnp.einsum('bsgc,gcd->bsgd', pooled, pool_w)
    return mixed.reshape(B, S, TOK_WIDTH) * pool_scale


def memory_attention(q_mem, mem_k, mem_v):
    B, S = q_mem.shape[:2]
    logits = jnp.einsum('bshd,bmhd->bhsm', q_mem, mem_k).astype(jnp.float32) * (HEAD_DIM ** -0.5)
    p = jax.nn.softmax(logits, axis=-1)
    out = jnp.einsum('bhsm,bmhd->bshd', p.astype(mem_v.dtype), mem_v)
    return out.reshape(B, S, MEM_WIDTH)


def forgetting_attention(q, k, v, F):
    B, S, H, Dh = q.shape
    nb = S // Q_BLOCK
    scale = Dh ** -0.5
    qb = q.reshape(B, nb, Q_BLOCK, H, Dh).transpose(1, 0, 2, 3, 4)
    Fqb = F.reshape(B, nb, Q_BLOCK, H).transpose(1, 0, 3, 2)
    Fk = F.transpose(0, 2, 1)[:, :, None, :]
    k_pos = jnp.arange(S)

    def block(args):
        qi, Fqi, start = args
        logits = jnp.einsum('bqhd,bkhd->bhqk', qi, k).astype(jnp.float32) * scale
        logits = logits + Fqi[..., None] - Fk
        q_pos = start + jnp.arange(Q_BLOCK)
        mask = q_pos[:, None] >= k_pos[None, :]
        logits = jnp.where(mask, logits, -jnp.inf)
        p = jax.nn.softmax(logits, axis=-1)
        return jnp.einsum('bhqk,bkhd->bqhd', p.astype(v.dtype), v)

    starts = jnp.arange(nb) * Q_BLOCK
    out = lax.map(block, (qb, Fqb, starts))
    return out.transpose(1, 0, 2, 3, 4).reshape(B, S, H * Dh)


def conv_ffn(x, w_up, conv_w, conv_b, w_down):
    h = x @ w_up
    C = h.shape[-1]
    h = lax.conv_general_dilated(h, conv_w[:, None, :].astype(h.dtype), window_strides=(1,),
                                 padding=[(CONV_WIDTH - 1, 0)],
                                 dimension_numbers=('NWC', 'WIO', 'NWC'),
                                 feature_group_count=C) + conv_b
    u, g = jnp.split(h, 2, axis=-1)
    return (jax.nn.silu(g) * u) @ w_down


def _fwd_setup_inputs(seed: int = 0) -> dict:
    key = jax.random.key(seed)
    ks = jax.random.split(key, 24)
    f32 = jnp.float32

    def nrm(k, shape, fan_in, gain=1.0):
        return jax.random.normal(k, shape, f32) * (gain * fan_in ** -0.5)

    x = jax.random.normal(ks[0], (BATCH, SEQ, D_MODEL), f32)
    mem = jax.random.normal(ks[1], (BATCH, MEM_LEN, D_MODEL), f32)
    a_w_in = nrm(ks[2], (N_A_LAYERS, D_MODEL, MIX_WIDTH), D_MODEL)
    a_pool_w = nrm(ks[3], (N_A_LAYERS, len(POOL_WINDOWS), POOL_GROUP, POOL_GROUP), POOL_GROUP)
    a_pool_scale = 1.0 + 0.1 * jax.random.normal(ks[4], (N_A_LAYERS, TOK_WIDTH), f32)
    a_w_out = nrm(ks[5], (N_A_LAYERS, MIX_WIDTH, D_MODEL), MIX_WIDTH, DN_BETA)
    b_w_q = nrm(ks[6], (N_B_LAYERS, D_MODEL, MIX_WIDTH), D_MODEL)
    b_w_out = nrm(ks[7], (N_B_LAYERS, MIX_WIDTH, D_MODEL), MIX_WIDTH, DN_BETA)
    kv_w = jnp.concatenate([nrm(ks[8], (D_MODEL, 2 * TOK_WIDTH), D_MODEL),
                            nrm(ks[9], (D_MODEL, FOX_HEADS), D_MODEL, 0.5)], axis=-1)
    f_b = jax.random.uniform(ks[10], (FOX_HEADS,), f32, 1.0, 4.0)
    mem_w_kv = nrm(ks[11], (DEPTH, D_MODEL, 2 * MEM_WIDTH), D_MODEL)
    ln1_g = 1.0 + 0.05 * jax.random.normal(ks[12], (DEPTH, D_MODEL), f32)
    ln1_b = 0.02 * jax.random.normal(ks[13], (DEPTH, D_MODEL), f32)
    ln2_g = 1.0 + 0.05 * jax.random.normal(ks[14], (DEPTH, D_MODEL), f32)
    ln2_b = 0.02 * jax.random.normal(ks[15], (DEPTH, D_MODEL), f32)
    ffn_w_up = nrm(ks[16], (DEPTH, D_MODEL, 2 * D_FF), D_MODEL)
    ffn_conv_w = nrm(ks[17], (DEPTH, CONV_WIDTH, 2 * D_FF), CONV_WIDTH)
    ffn_conv_b = 0.02 * jax.random.normal(ks[18], (DEPTH, 2 * D_FF), f32)
    ffn_w_down = nrm(ks[19], (DEPTH, D_FF, D_MODEL), D_FF, DN_BETA)
    return {"x": x, "mem": mem, "a_w_in": a_w_in, "a_pool_w": a_pool_w,
            "a_pool_scale": a_pool_scale, "a_w_out": a_w_out, "b_w_q": b_w_q,
            "b_w_out": b_w_out, "kv_w": kv_w, "f_b": f_b, "mem_w_kv": mem_w_kv,
            "ln1_g": ln1_g, "ln1_b": ln1_b, "ln2_g": ln2_g, "ln2_b": ln2_b,
            "ffn_w_up": ffn_w_up, "ffn_conv_w": ffn_conv_w, "ffn_conv_b": ffn_conv_b,
            "ffn_w_down": ffn_w_down}


def _fwd_reference(x, mem, a_w_in, a_pool_w, a_pool_scale, a_w_out, b_w_q, b_w_out, kv_w, f_b,
              mem_w_kv, ln1_g, ln1_b, ln2_g, ln2_b, ffn_w_up, ffn_conv_w, ffn_conv_b,
              ffn_w_down):
    B, S, _ = x.shape
    M = mem.shape[1]
    k_sh = v_sh = F_sh = None
    for l in range(DEPTH):
        mem_kv = mem @ mem_w_kv[l]
        mem_k = mem_kv[..., :MEM_WIDTH].reshape(B, M, MEM_HEADS, HEAD_DIM)
        mem_v = mem_kv[..., MEM_WIDTH:].reshape(B, M, MEM_HEADS, HEAD_DIM)

        if l < N_A_LAYERS:
            proj = x @ a_w_in[l]
            tok = multiscale_pool(proj[..., :TOK_WIDTH], a_pool_w[l], a_pool_scale[l])
            w_out = a_w_out[l]
        else:
            if l == N_A_LAYERS:
                kvf = x @ kv_w
                k_sh = kvf[..., :TOK_WIDTH].reshape(B, S, FOX_HEADS, HEAD_DIM)
                v_sh = kvf[..., TOK_WIDTH:2 * TOK_WIDTH].reshape(B, S, FOX_HEADS, HEAD_DIM)
                log_f = jax.nn.log_sigmoid(kvf[..., 2 * TOK_WIDTH:].astype(jnp.float32)
                                           + f_b.astype(jnp.float32))
                F_sh = jnp.cumsum(log_f, axis=1)
            j = l - N_A_LAYERS
            proj = x @ b_w_q[j]
            q = proj[..., :TOK_WIDTH].reshape(B, S, FOX_HEADS, HEAD_DIM)
            tok = forgetting_attention(q, k_sh, v_sh, F_sh)
            w_out = b_w_out[j]

        q_mem = proj[..., TOK_WIDTH:].reshape(B, S, MEM_HEADS, HEAD_DIM)
        mem_out = memory_attention(q_mem, mem_k, mem_v)
        mix = jnp.concatenate([tok, mem_out], axis=-1) @ w_out
        x = layer_norm(DN_ALPHA * x + mix, ln1_g[l], ln1_b[l])

        ffn = conv_ffn(x, ffn_w_up[l], ffn_conv_w[l], ffn_conv_b[l], ffn_w_down[l])
        x = layer_norm(DN_ALPHA * x + ffn, ln2_g[l], ln2_b[l])
    return x


import jax as _jax
import jax.numpy as _jnp

TWIN_FORMAT = 'train_step'
FWD_PARAMS = ['x', 'mem', 'a_w_in', 'a_pool_w', 'a_pool_scale', 'a_w_out', 'b_w_q', 'b_w_out', 'kv_w', 'f_b', 'mem_w_kv', 'ln1_g', 'ln1_b', 'ln2_g', 'ln2_b', 'ffn_w_up', 'ffn_conv_w', 'ffn_conv_b', 'ffn_w_down']
TWIN_WEIGHTS = ['a_w_in', 'a_pool_w', 'a_pool_scale', 'a_w_out', 'b_w_q', 'b_w_out', 'kv_w', 'f_b', 'mem_w_kv', 'ln1_g', 'ln1_b', 'ln2_g', 'ln2_b', 'ffn_w_up', 'ffn_conv_w', 'ffn_conv_b', 'ffn_w_down']
TWIN_DIFF_INPUT = 'x'
TWIN_INPUTS = ['x', 'mem', 'a_w_in', 'a_pool_w', 'a_pool_scale', 'a_w_out', 'b_w_q', 'b_w_out', 'kv_w', 'f_b', 'mem_w_kv', 'ln1_g', 'ln1_b', 'ln2_g', 'ln2_b', 'ffn_w_up', 'ffn_conv_w', 'ffn_conv_b', 'ffn_w_down', 'loss_target', 'm_a_w_in', 'm_a_pool_w', 'm_a_pool_scale', 'm_a_w_out', 'm_b_w_q', 'm_b_w_out', 'm_kv_w', 'm_f_b', 'm_mem_w_kv', 'm_ln1_g', 'm_ln1_b', 'm_ln2_g', 'm_ln2_b', 'm_ffn_w_up', 'm_ffn_conv_w', 'm_ffn_conv_b', 'm_ffn_w_down', 'v_a_w_in', 'v_a_pool_w', 'v_a_pool_scale', 'v_a_w_out', 'v_b_w_q', 'v_b_w_out', 'v_kv_w', 'v_f_b', 'v_mem_w_kv', 'v_ln1_g', 'v_ln1_b', 'v_ln2_g', 'v_ln2_b', 'v_ffn_w_up', 'v_ffn_conv_w', 'v_ffn_conv_b', 'v_ffn_w_down']
TWIN_OUTPUTS = ['loss', 'grad_x', 'grad_a_w_in', 'grad_a_pool_w', 'grad_a_pool_scale', 'grad_a_w_out', 'grad_b_w_q', 'grad_b_w_out', 'grad_kv_w', 'grad_f_b', 'grad_mem_w_kv', 'grad_ln1_g', 'grad_ln1_b', 'grad_ln2_g', 'grad_ln2_b', 'grad_ffn_w_up', 'grad_ffn_conv_w', 'grad_ffn_conv_b', 'grad_ffn_w_down', 'delta_a_w_in', 'delta_a_pool_w', 'delta_a_pool_scale', 'delta_a_w_out', 'delta_b_w_q', 'delta_b_w_out', 'delta_kv_w', 'delta_f_b', 'delta_mem_w_kv', 'delta_ln1_g', 'delta_ln1_b', 'delta_ln2_g', 'delta_ln2_b', 'delta_ffn_w_up', 'delta_ffn_conv_w', 'delta_ffn_conv_b', 'delta_ffn_w_down', 'new_m_a_w_in', 'new_m_a_pool_w', 'new_m_a_pool_scale', 'new_m_a_w_out', 'new_m_b_w_q', 'new_m_b_w_out', 'new_m_kv_w', 'new_m_f_b', 'new_m_mem_w_kv', 'new_m_ln1_g', 'new_m_ln1_b', 'new_m_ln2_g', 'new_m_ln2_b', 'new_m_ffn_w_up', 'new_m_ffn_conv_w', 'new_m_ffn_conv_b', 'new_m_ffn_w_down', 'new_v_a_w_in', 'new_v_a_pool_w', 'new_v_a_pool_scale', 'new_v_a_w_out', 'new_v_b_w_q', 'new_v_b_w_out', 'new_v_kv_w', 'new_v_f_b', 'new_v_mem_w_kv', 'new_v_ln1_g', 'new_v_ln1_b', 'new_v_ln2_g', 'new_v_ln2_b', 'new_v_ffn_w_up', 'new_v_ffn_conv_w', 'new_v_ffn_conv_b', 'new_v_ffn_w_down']
TWIN_LEAF_KINDS = {'loss': 'loss', 'grad_x': 'grad_x', 'grad_a_w_in': 'grad_w', 'grad_a_pool_w': 'grad_w', 'grad_a_pool_scale': 'grad_w', 'grad_a_w_out': 'grad_w', 'grad_b_w_q': 'grad_w', 'grad_b_w_out': 'grad_w', 'grad_kv_w': 'grad_w', 'grad_f_b': 'grad_w', 'grad_mem_w_kv': 'grad_w', 'grad_ln1_g': 'grad_w', 'grad_ln1_b': 'grad_w', 'grad_ln2_g': 'grad_w', 'grad_ln2_b': 'grad_w', 'grad_ffn_w_up': 'grad_w', 'grad_ffn_conv_w': 'grad_w', 'grad_ffn_conv_b': 'grad_w', 'grad_ffn_w_down': 'grad_w', 'delta_a_w_in': 'delta_w', 'delta_a_pool_w': 'delta_w', 'delta_a_pool_scale': 'delta_w', 'delta_a_w_out': 'delta_w', 'delta_b_w_q': 'delta_w', 'delta_b_w_out': 'delta_w', 'delta_kv_w': 'delta_w', 'delta_f_b': 'delta_w', 'delta_mem_w_kv': 'delta_w', 'delta_ln1_g': 'delta_w', 'delta_ln1_b': 'delta_w', 'delta_ln2_g': 'delta_w', 'delta_ln2_b': 'delta_w', 'delta_ffn_w_up': 'delta_w', 'delta_ffn_conv_w': 'delta_w', 'delta_ffn_conv_b': 'delta_w', 'delta_ffn_w_down': 'delta_w', 'new_m_a_w_in': 'new_m', 'new_m_a_pool_w': 'new_m', 'new_m_a_pool_scale': 'new_m', 'new_m_a_w_out': 'new_m', 'new_m_b_w_q': 'new_m', 'new_m_b_w_out': 'new_m', 'new_m_kv_w': 'new_m', 'new_m_f_b': 'new_m', 'new_m_mem_w_kv': 'new_m', 'new_m_ln1_g': 'new_m', 'new_m_ln1_b': 'new_m', 'new_m_ln2_g': 'new_m', 'new_m_ln2_b': 'new_m', 'new_m_ffn_w_up': 'new_m', 'new_m_ffn_conv_w': 'new_m', 'new_m_ffn_conv_b': 'new_m', 'new_m_ffn_w_down': 'new_m', 'new_v_a_w_in': 'new_v', 'new_v_a_pool_w': 'new_v', 'new_v_a_pool_scale': 'new_v', 'new_v_a_w_out': 'new_v', 'new_v_b_w_q': 'new_v', 'new_v_b_w_out': 'new_v', 'new_v_kv_w': 'new_v', 'new_v_f_b': 'new_v', 'new_v_mem_w_kv': 'new_v', 'new_v_ln1_g': 'new_v', 'new_v_ln1_b': 'new_v', 'new_v_ln2_g': 'new_v', 'new_v_ln2_b': 'new_v', 'new_v_ffn_w_up': 'new_v', 'new_v_ffn_conv_w': 'new_v', 'new_v_ffn_conv_b': 'new_v', 'new_v_ffn_w_down': 'new_v'}


def _forward(args):
    return _fwd_reference(*[args[k] for k in FWD_PARAMS])


def _output_shape():
    out = _jax.eval_shape(lambda: _forward(_fwd_setup_inputs(0)))
    return out.shape, out.dtype

N_MICROBATCH = 1
ADAM_LR = 0.001
ADAM_B1 = 0.9
ADAM_B2 = 0.999
ADAM_EPS = 1e-08
ADAM_WD = 0.01
ADAM_STEP = 10
PER_EXAMPLE_BATCH_AXIS = {'x': 0, 'mem': 0, 'loss_target': 0}
SHARED_INPUTS = []
_WEIGHT_DTYPES = {'a_w_in': _jnp.float32, 'a_pool_w': _jnp.float32, 'a_pool_scale': _jnp.float32, 'a_w_out': _jnp.float32, 'b_w_q': _jnp.float32, 'b_w_out': _jnp.float32, 'kv_w': _jnp.float32, 'f_b': _jnp.float32, 'mem_w_kv': _jnp.float32, 'ln1_g': _jnp.float32, 'ln1_b': _jnp.float32, 'ln2_g': _jnp.float32, 'ln2_b': _jnp.float32, 'ffn_w_up': _jnp.float32, 'ffn_conv_w': _jnp.float32, 'ffn_conv_b': _jnp.float32, 'ffn_w_down': _jnp.float32}
MOMENT_SCALE = {'a_w_in': 7.051897e-02, 'a_pool_w': 8.087543e-02, 'a_pool_scale': 7.989262e-02, 'a_w_out': 1.416387e-01, 'b_w_q': 2.216255e-02, 'b_w_out': 5.544739e-02, 'kv_w': 2.856637e-02, 'f_b': 1.089869e-01, 'mem_w_kv': 1.061359e-02, 'ln1_g': 5.623938e+00, 'ln1_b': 7.813650e-01, 'ln2_g': 4.583741e+01, 'ln2_b': 1.398515e+00, 'ffn_w_up': 3.369840e-02, 'ffn_conv_w': 3.399012e-02, 'ffn_conv_b': 3.920115e-02, 'ffn_w_down': 1.085257e-01}


def _to_microbatches(a, axis):
    t = _jnp.moveaxis(a, axis, 0)
    t = t.reshape((N_MICROBATCH, t.shape[0] // N_MICROBATCH) + t.shape[1:])
    return _jnp.moveaxis(t, 1, axis + 1)


def setup_inputs(seed: int = 0) -> dict:
    inp = _fwd_setup_inputs(seed)
    key = _jax.random.fold_in(_jax.random.key(seed), 7919)
    shape, _ = _output_shape()
    out = dict(inp)
    out["loss_target"] = _jax.random.normal(_jax.random.fold_in(key, 0), shape, _jnp.float32)
    for i, name in enumerate(TWIN_WEIGHTS):
        w = inp[name].astype(_jnp.float32)
        if MOMENT_SCALE is None:
            s = _jnp.sqrt(_jnp.mean(_jnp.square(w)) + 1e-30)
        else:
            s = MOMENT_SCALE[name]
        km, kv = _jax.random.split(_jax.random.fold_in(key, i + 1))
        out[name] = w
        out["m_" + name] = s * _jax.random.normal(km, w.shape, _jnp.float32)
        out["v_" + name] = (s * s) * _jax.random.uniform(kv, w.shape, _jnp.float32, 0.5, 1.5)
    if N_MICROBATCH > 1:
        for name, axis in PER_EXAMPLE_BATCH_AXIS.items():
            out[name] = _to_microbatches(out[name], axis)
    return {'x': out['x'], 'mem': out['mem'], 'a_w_in': out['a_w_in'], 'a_pool_w': out['a_pool_w'], 'a_pool_scale': out['a_pool_scale'], 'a_w_out': out['a_w_out'], 'b_w_q': out['b_w_q'], 'b_w_out': out['b_w_out'], 'kv_w': out['kv_w'], 'f_b': out['f_b'], 'mem_w_kv': out['mem_w_kv'], 'ln1_g': out['ln1_g'], 'ln1_b': out['ln1_b'], 'ln2_g': out['ln2_g'], 'ln2_b': out['ln2_b'], 'ffn_w_up': out['ffn_w_up'], 'ffn_conv_w': out['ffn_conv_w'], 'ffn_conv_b': out['ffn_conv_b'], 'ffn_w_down': out['ffn_w_down'], 'loss_target': out['loss_target'], 'm_a_w_in': out['m_a_w_in'], 'm_a_pool_w': out['m_a_pool_w'], 'm_a_pool_scale': out['m_a_pool_scale'], 'm_a_w_out': out['m_a_w_out'], 'm_b_w_q': out['m_b_w_q'], 'm_b_w_out': out['m_b_w_out'], 'm_kv_w': out['m_kv_w'], 'm_f_b': out['m_f_b'], 'm_mem_w_kv': out['m_mem_w_kv'], 'm_ln1_g': out['m_ln1_g'], 'm_ln1_b': out['m_ln1_b'], 'm_ln2_g': out['m_ln2_g'], 'm_ln2_b': out['m_ln2_b'], 'm_ffn_w_up': out['m_ffn_w_up'], 'm_ffn_conv_w': out['m_ffn_conv_w'], 'm_ffn_conv_b': out['m_ffn_conv_b'], 'm_ffn_w_down': out['m_ffn_w_down'], 'v_a_w_in': out['v_a_w_in'], 'v_a_pool_w': out['v_a_pool_w'], 'v_a_pool_scale': out['v_a_pool_scale'], 'v_a_w_out': out['v_a_w_out'], 'v_b_w_q': out['v_b_w_q'], 'v_b_w_out': out['v_b_w_out'], 'v_kv_w': out['v_kv_w'], 'v_f_b': out['v_f_b'], 'v_mem_w_kv': out['v_mem_w_kv'], 'v_ln1_g': out['v_ln1_g'], 'v_ln1_b': out['v_ln1_b'], 'v_ln2_g': out['v_ln2_g'], 'v_ln2_b': out['v_ln2_b'], 'v_ffn_w_up': out['v_ffn_w_up'], 'v_ffn_conv_w': out['v_ffn_conv_w'], 'v_ffn_conv_b': out['v_ffn_conv_b'], 'v_ffn_w_down': out['v_ffn_w_down']}


def _loss(weights, diff, rest, loss_target):
    with _jax.named_scope("forward"):
        args = {**rest, TWIN_DIFF_INPUT: diff, **{k: w.astype(_WEIGHT_DTYPES[k]) for k, w in weights.items()}}
        y = _forward(args)
    with _jax.named_scope("loss_head"):
        err = _jnp.square(y.astype(_jnp.float32) - loss_target)
        return 0.5 * _jnp.sum(_jnp.mean(err, axis=-1)) if err.ndim else 0.5 * err


def _adamw(w, g, m, v):
    m = ADAM_B1 * m + (1.0 - ADAM_B1) * g
    v = ADAM_B2 * v + (1.0 - ADAM_B2) * _jnp.square(g)
    m_hat = m / (1.0 - ADAM_B1 ** ADAM_STEP)
    v_hat = v / (1.0 - ADAM_B2 ** ADAM_STEP)
    delta = -ADAM_LR * (m_hat / (_jnp.sqrt(v_hat) + ADAM_EPS) + ADAM_WD * w)
    return delta, m, v


def reference(x, mem, a_w_in, a_pool_w, a_pool_scale, a_w_out, b_w_q, b_w_out, kv_w, f_b, mem_w_kv, ln1_g, ln1_b, ln2_g, ln2_b, ffn_w_up, ffn_conv_w, ffn_conv_b, ffn_w_down, loss_target, m_a_w_in, m_a_pool_w, m_a_pool_scale, m_a_w_out, m_b_w_q, m_b_w_out, m_kv_w, m_f_b, m_mem_w_kv, m_ln1_g, m_ln1_b, m_ln2_g, m_ln2_b, m_ffn_w_up, m_ffn_conv_w, m_ffn_conv_b, m_ffn_w_down, v_a_w_in, v_a_pool_w, v_a_pool_scale, v_a_w_out, v_b_w_q, v_b_w_out, v_kv_w, v_f_b, v_mem_w_kv, v_ln1_g, v_ln1_b, v_ln2_g, v_ln2_b, v_ffn_w_up, v_ffn_conv_w, v_ffn_conv_b, v_ffn_w_down):
    given = dict(x=x, mem=mem, a_w_in=a_w_in, a_pool_w=a_pool_w, a_pool_scale=a_pool_scale, a_w_out=a_w_out, b_w_q=b_w_q, b_w_out=b_w_out, kv_w=kv_w, f_b=f_b, mem_w_kv=mem_w_kv, ln1_g=ln1_g, ln1_b=ln1_b, ln2_g=ln2_g, ln2_b=ln2_b, ffn_w_up=ffn_w_up, ffn_conv_w=ffn_conv_w, ffn_conv_b=ffn_conv_b, ffn_w_down=ffn_w_down, loss_target=loss_target, m_a_w_in=m_a_w_in, m_a_pool_w=m_a_pool_w, m_a_pool_scale=m_a_pool_scale, m_a_w_out=m_a_w_out, m_b_w_q=m_b_w_q, m_b_w_out=m_b_w_out, m_kv_w=m_kv_w, m_f_b=m_f_b, m_mem_w_kv=m_mem_w_kv, m_ln1_g=m_ln1_g, m_ln1_b=m_ln1_b, m_ln2_g=m_ln2_g, m_ln2_b=m_ln2_b, m_ffn_w_up=m_ffn_w_up, m_ffn_conv_w=m_ffn_conv_w, m_ffn_conv_b=m_ffn_conv_b, m_ffn_w_down=m_ffn_w_down, v_a_w_in=v_a_w_in, v_a_pool_w=v_a_pool_w, v_a_pool_scale=v_a_pool_scale, v_a_w_out=v_a_w_out, v_b_w_q=v_b_w_q, v_b_w_out=v_b_w_out, v_kv_w=v_kv_w, v_f_b=v_f_b, v_mem_w_kv=v_mem_w_kv, v_ln1_g=v_ln1_g, v_ln1_b=v_ln1_b, v_ln2_g=v_ln2_g, v_ln2_b=v_ln2_b, v_ffn_w_up=v_ffn_w_up, v_ffn_conv_w=v_ffn_conv_w, v_ffn_conv_b=v_ffn_conv_b, v_ffn_w_down=v_ffn_w_down)
    weights = {n: given[n] for n in TWIN_WEIGHTS}
    shared = {n: given[n] for n in SHARED_INPUTS}
    per_example = {n: given[n] for n in ['x', 'mem']}
    grad_fn = _jax.value_and_grad(_loss, argnums=(0, 1))

    def one_microbatch(ex, loss_target):
        ex = dict(ex)
        diff = ex.pop(TWIN_DIFF_INPUT)
        return grad_fn(weights, diff, {**shared, **ex}, loss_target)

    if N_MICROBATCH == 1:
        loss, (grad_w, grad_x) = one_microbatch(per_example, given["loss_target"])
    else:
        def body(carry, xs):
            loss_sum, grad_sum = carry
            l_k, (gw_k, gx_k) = one_microbatch(xs[0], xs[1])
            with _jax.named_scope("update"):
                return (loss_sum + l_k, _jax.tree.map(_jnp.add, grad_sum, gw_k)), gx_k

        init = (_jnp.zeros((), _jnp.float32), _jax.tree.map(_jnp.zeros_like, weights))
        (loss, grad_w), grad_x = _jax.lax.scan(body, init, (per_example, given["loss_target"]))
    with _jax.named_scope("update"):
        delta_w, new_m, new_v = {}, {}, {}
        for n in TWIN_WEIGHTS:
            delta_w[n], new_m[n], new_v[n] = _adamw(weights[n], grad_w[n], given["m_" + n], given["v_" + n])
    return (loss, grad_x, *[grad_w[n] for n in TWIN_WEIGHTS], *[delta_w[n] for n in TWIN_WEIGHTS],
            *[new_m[n] for n in TWIN_WEIGHTS], *[new_v[n] for n in TWIN_WEIGHTS])
```

```python
import functools
import math

import jax
import jax.numpy as jnp
from jax import lax
from jax.experimental import pallas as pl
from jax.experimental.pallas import tpu as pltpu

F32 = jnp.float32
BF16 = jnp.bfloat16

HEAD_DIM = 64
MEM_HEADS = 4
MEM_WIDTH = MEM_HEADS * HEAD_DIM
POOL_WINDOWS = (2, 4, 8, 16)
MAX_WINDOW = 16
CONV_WIDTH = 3
DEPTH = 2
DN_ALPHA = (2.0 * DEPTH) ** 0.25
LN_EPS = 1e-5
ATT_SCALE = HEAD_DIM ** -0.5
NEG_BIG = -1e30

ADAM_LR = 0.001
ADAM_B1 = 0.9
ADAM_B2 = 0.999
ADAM_EPS = 1e-08
ADAM_WD = 0.01
ADAM_STEP = 10

LANE = 128
SUBLANE = 8
PACK_COLS = 1024
VMEM_LIMIT = 56 * 1024 * 1024
N_DEV = 8
N_CHIP = 4
MESH_T = pl.DeviceIdType.MESH

SHARDED = ("a_w_in", "a_pool_scale", "a_w_out", "b_w_q", "b_w_out", "kv_w", "mem_w_kv", "ffn_w_up",
           "ffn_conv_w", "ffn_w_down")
SHARD_AXIS = {"a_w_in": 1, "a_pool_scale": 1, "a_w_out": 1, "b_w_q": 1, "b_w_out": 1, "kv_w": 1, "mem_w_kv": 1,
              "ffn_w_up": 2, "ffn_conv_w": 2, "ffn_w_down": 1}
GATHER_F32 = ("a_pool_scale", "ffn_conv_w")
REPLICATED = ("a_pool_w", "f_b", "ln1_g", "ln1_b", "ln2_g", "ln2_b", "ffn_conv_b")
WEIGHTS = ("a_w_in", "a_pool_w", "a_pool_scale", "a_w_out", "b_w_q", "b_w_out", "kv_w", "f_b", "mem_w_kv",
           "ln1_g", "ln1_b", "ln2_g", "ln2_b", "ffn_w_up", "ffn_conv_w", "ffn_conv_b", "ffn_w_down")


def _round_up(n, m):
    return -(-n // m) * m


def _pick(dim, pref, unit=LANE):
    if dim <= pref:
        return dim
    t = (pref // unit) * unit
    while t >= unit:
        if dim % t == 0:
            return t
        t -= unit
    raise ValueError(f"no tile for {dim} <= {pref}")


def _params():
    return pltpu.CompilerParams(vmem_limit_bytes=VMEM_LIMIT)


_DIMS = {"nn": ((1,), (0,)), "nt": ((1,), (1,)), "tn": ((0,), (0,))}


def _bdot(a, b, mode):
    return lax.dot_general(a.astype(BF16), b.astype(BF16), (_DIMS[mode], ((), ())), preferred_element_type=F32)


def _mm(a, b, mode, *, name, tm=512, tn=1024, tk=2048, adds=()):
    if mode == "nn":
        (M, K), (K2, N) = a.shape, b.shape
    elif mode == "nt":
        (M, K), (N, K2) = a.shape, b.shape
    else:
        (K, M), (K2, N) = a.shape, b.shape
    assert K == K2, (name, a.shape, b.shape)
    tm, tn = _pick(M, tm, SUBLANE if mode != "tn" else LANE), _pick(N, tn)
    tk = _pick(K, tk, LANE if mode != "tn" else SUBLANE)
    nk = K // tk
    n_add = len(adds)

    def body(*refs):
        a_ref, b_ref = refs[0], refs[1]
        add_refs = refs[2:2 + n_add]
        o_ref, acc_ref = refs[2 + n_add], refs[3 + n_add]
        part = _bdot(a_ref[...], b_ref[...], mode)

        def finish(r):
            for ar in add_refs:
                r = r + ar[...]
            o_ref[...] = r

        if nk == 1:
            finish(part)
        else:
            k = pl.program_id(2)

            @pl.when(k == 0)
            def _():
                acc_ref[...] = part

            @pl.when(k > 0)
            def _():
                acc_ref[...] += part

            @pl.when(k == nk - 1)
            def _():
                finish(acc_ref[...])

    if mode == "nn":
        a_spec = pl.BlockSpec((tm, tk), lambda i, j, k: (i, k))
        b_spec = pl.BlockSpec((tk, tn), lambda i, j, k: (k, j))
    elif mode == "nt":
        a_spec = pl.BlockSpec((tm, tk), lambda i, j, k: (i, k))
        b_spec = pl.BlockSpec((tn, tk), lambda i, j, k: (j, k))
    else:
        a_spec = pl.BlockSpec((tk, tm), lambda i, j, k: (k, i))
        b_spec = pl.BlockSpec((tk, tn), lambda i, j, k: (k, j))
    o_spec = pl.BlockSpec((tm, tn), lambda i, j, k: (i, j))
    acc_shape = (tm, tn) if nk > 1 else (SUBLANE, LANE)
    return pl.pallas_call(
        body, name=name, grid=(M // tm, N // tn, nk),
        in_specs=[a_spec, b_spec] + [o_spec] * n_add, out_specs=o_spec,
        out_shape=jax.ShapeDtypeStruct((M, N), F32),
        scratch_shapes=[pltpu.VMEM(acc_shape, F32)],
        compiler_params=_params(),
    )(a, b, *adds)


def _rowwise(fn, tiled, full, outs_tiled, outs_acc, *, rows, tile, name, acc_period=None):
    n_tiles = rows // tile
    period = n_tiles if acc_period is None else acc_period
    arrays, in_specs = [], []
    for t in tiled:
        arr, width, cb = t if isinstance(t, tuple) else (t, t.shape[1], 0)
        arrays.append(arr)
        in_specs.append(pl.BlockSpec((tile, width), lambda i, cb=cb: (i, cb)))
    for f in full:
        arr, spec = f if isinstance(f, tuple) else (f, None)
        arrays.append(arr)
        in_specs.append(spec if spec is not None else pl.BlockSpec(arr.shape, lambda i, nd=arr.ndim: (0,) * nd))
    out_shape, out_specs = [], []
    for width, dt in outs_tiled:
        out_shape.append(jax.ShapeDtypeStruct((rows, width), dt))
        out_specs.append(pl.BlockSpec((tile, width), lambda i: (i, 0)))
    for acc in outs_acc:
        shape, dt = acc[0], acc[1]
        out_shape.append(jax.ShapeDtypeStruct(shape, dt))
        out_specs.append(acc[2] if len(acc) > 2 else pl.BlockSpec(shape, lambda i, nd=len(shape): (0,) * nd))
    n_in, n_t, n_a = len(arrays), len(outs_tiled), len(outs_acc)

    def body(*refs):
        vals = [r[...] for r in refs[:n_in]]
        o_t, o_a = fn(*vals)
        for r, v in zip(refs[n_in:n_in + n_t], o_t):
            r[...] = v.astype(r.dtype)
        first = pl.program_id(0) % period == 0
        for r, v in zip(refs[n_in + n_t:n_in + n_t + n_a], o_a):
            v = v.reshape(r.shape)

            @pl.when(first)
            def _(r=r, v=v):
                r[...] = v

            @pl.when(jnp.logical_not(first))
            def _(r=r, v=v):
                r[...] += v

    return pl.pallas_call(
        body, name=name, grid=(n_tiles,), in_specs=in_specs, out_specs=out_specs, out_shape=out_shape,
        compiler_params=_params(),
    )(*arrays)


def _ln_stats(h):
    mu = jnp.mean(h, axis=-1, keepdims=True)
    d = h - mu
    var = jnp.mean(d * d, axis=-1, keepdims=True)
    rstd = lax.rsqrt(var + LN_EPS)
    return d * rstd, rstd


def _ln_bwd_math(h, g, dy):
    xhat, rstd = _ln_stats(h)
    dxhat = dy * g
    dh = rstd * (dxhat - jnp.mean(dxhat, axis=-1, keepdims=True)
                 - xhat * jnp.mean(dxhat * xhat, axis=-1, keepdims=True))
    return dh, jnp.sum(dy * xhat, axis=0, keepdims=True), jnp.sum(dy, axis=0, keepdims=True)


def _ln_fwd(x, r, g, b, *, name):
    n, d = x.shape

    def fn(x, r, g, b):
        xhat, _ = _ln_stats(DN_ALPHA * x + r)
        return (xhat * g + b,), ()

    return _rowwise(fn, [x, r], [g, b], [(d, F32)], [], rows=n, tile=_pick(n, 512, SUBLANE), name=name)[0]


def _ln_bwd(x, r, g, dys, *, name):
    n, d = x.shape
    n_dy = len(dys)

    def fn(x, r, *rest):
        dy = rest[0]
        for e in rest[1:n_dy]:
            dy = dy + e
        dh, dg, db = _ln_bwd_math(DN_ALPHA * x + r, rest[n_dy], dy)
        return (DN_ALPHA * dh, dh), (dg, db)

    return _rowwise(fn, [x, r, *dys], [g], [(d, F32), (d, F32)], [((1, d), F32), ((1, d), F32)],
                    rows=n, tile=_pick(n, 256, SUBLANE), name=name)


def _final_ln_loss(x, r, target, g, b, *, name):
    n, d = x.shape

    def fn(x, r, t, g, b):
        h = DN_ALPHA * x + r
        xhat, _ = _ln_stats(h)
        err = xhat * g + b - t
        loss = jnp.full((1, LANE), 0.5 * jnp.sum(err * err) / d, F32)
        dh, dg, db = _ln_bwd_math(h, g, err / d)
        return (DN_ALPHA * dh, dh), (loss, dg, db)

    return _rowwise(fn, [x, r, target], [g, b], [(d, F32), (d, F32)],
                    [((1, LANE), F32), ((1, d), F32), ((1, d), F32)],
                    rows=n, tile=_pick(n, 256, SUBLANE), name=name)


def _mem_heads(qm):
    lane = lax.broadcasted_iota(jnp.int32, (1, MEM_WIDTH), 1)
    for h in range(MEM_HEADS):
        msk = (lane >= h * HEAD_DIM) & (lane < (h + 1) * HEAD_DIM)
        yield msk, jnp.where(msk, qm, 0.0).astype(BF16)


def _mem_softmax(qh, k):
    s = _bdot(qh, k, "nt") * ATT_SCALE
    p = jnp.exp(s - jnp.max(s, axis=-1, keepdims=True))
    return p / jnp.sum(p, axis=-1, keepdims=True)


def _memattn_fwd(tok, proj, memkv, scale, *, seq, name):
    n, tokw = tok.shape
    d = tokw + MEM_WIDTH
    tile = _pick(seq, 512, SUBLANE)

    def fn(tok, qm, kv, scale):
        k, v = kv[:, :MEM_WIDTH].astype(BF16), kv[:, MEM_WIDTH:].astype(BF16)
        out = jnp.zeros(qm.shape, F32)
        for msk, qh in _mem_heads(qm):
            out = jnp.where(msk, _bdot(_mem_softmax(qh, k), v, "nn"), out)
        return (jnp.concatenate([tok * scale, out], axis=1),), ()

    kv_spec = pl.BlockSpec((None,) + memkv.shape[1:], lambda i: (i // (seq // tile), 0, 0))
    return _rowwise(fn, [tok, (proj, MEM_WIDTH, tokw // MEM_WIDTH)], [(memkv, kv_spec), scale], [(d, F32)], [],
                    rows=n, tile=tile, name=name)[0]


def _memattn_bwd(dmixin, dtok, proj, memkv, *, seq, name):
    n, tokw = dtok.shape
    d = tokw + MEM_WIDTH
    tile = _pick(seq, 512, SUBLANE)

    def fn(dmo, dtok, qm, kv):
        k, v = kv[:, :MEM_WIDTH].astype(BF16), kv[:, MEM_WIDTH:].astype(BF16)
        dq = jnp.zeros(qm.shape, F32)
        dk = jnp.zeros(k.shape, F32)
        dv = jnp.zeros(v.shape, F32)
        for msk, qh in _mem_heads(qm):
            p = _mem_softmax(qh, k)
            doh = jnp.where(msk, dmo, 0.0).astype(BF16)
            dv = dv + _bdot(p, doh, "tn")
            dp = _bdot(doh, v, "nt")
            ds = (p * (dp - jnp.sum(dp * p, axis=-1, keepdims=True))).astype(BF16)
            dq = jnp.where(msk, _bdot(ds, k, "nn") * ATT_SCALE, dq)
            dk = dk + _bdot(ds, qh, "tn") * ATT_SCALE
        return (jnp.concatenate([dtok, dq], axis=1),), (jnp.concatenate([dk, dv], axis=1),)

    tpe = seq // tile
    kv_spec = pl.BlockSpec((None,) + memkv.shape[1:], lambda i: (i // tpe, 0, 0))
    return _rowwise(fn, [(dmixin, MEM_WIDTH, tokw // MEM_WIDTH), dtok, (proj, MEM_WIDTH, tokw // MEM_WIDTH)],
                    [(memkv, kv_spec)], [(d, F32)], [(memkv.shape, F32, kv_spec)],
                    rows=n, tile=tile, name=name, acc_period=tpe)


def _scale_bwd(dmixin, mixed, scale, *, name):
    n, tokw = mixed.shape

    def fn(dt, mixed, scale):
        return (dt * scale,), (jnp.sum(dt * mixed, axis=0, keepdims=True),)

    return _rowwise(fn, [(dmixin, tokw, 0), mixed], [scale], [(tokw, F32)], [((1, tokw), F32)],
                    rows=n, tile=_pick(n, 512, SUBLANE), name=name)


def _chunk_rows(seq):
    return _pick(seq, 512, SUBLANE)


def _load_ext(ref, c, rows, before, after, seq):
    lo, hi = c * rows - before, (c + 1) * rows + after
    parts = []
    if lo < 0:
        parts.append(jnp.zeros((-lo, ref.shape[1]), F32))
    parts.append(ref[max(lo, 0):min(hi, seq), :])
    if hi > seq:
        parts.append(jnp.zeros((hi - seq, ref.shape[1]), F32))
    return parts[0] if len(parts) == 1 else jnp.concatenate(parts, axis=0)


def _down(x, k):
    return pltpu.roll(x, k, 0)


def _up(x, k):
    return pltpu.roll(x, x.shape[0] - k, 0)


def _window_sums(ext, shift, col0, group):
    lane = col0 + lax.broadcasted_iota(jnp.int32, (1, ext.shape[1]), 1)
    gidx = lane // group
    s = ext
    out = None
    k = 1
    for gi, w in enumerate(POOL_WINDOWS):
        while k < w:
            s = s + shift(s, k)
            k *= 2
        out = s if out is None else jnp.where(gidx >= gi, s, out)
    return out, jnp.left_shift(2, jnp.minimum(gidx, len(POOL_WINDOWS) - 1))


def _pool_fwd(proj3, tokw, *, name):
    nb, seq, _ = proj3.shape
    rows = _chunk_rows(seq)
    group = tokw // len(POOL_WINDOWS)

    def body(u_ref, o_ref):
        col0 = pl.program_id(1) * LANE
        for c in range(seq // rows):
            ext = _load_ext(u_ref, c, rows, MAX_WINDOW, 0, seq)
            sums, win = _window_sums(ext, _down, col0, group)
            t = c * rows + lax.broadcasted_iota(jnp.int32, (rows, 1), 0)
            count = jnp.minimum(t + 1, win).astype(F32)
            o_ref[c * rows:(c + 1) * rows, :] = sums[MAX_WINDOW:, :] / count - ext[MAX_WINDOW:, :]

    spec = pl.BlockSpec((None, seq, LANE), lambda b, j: (b, 0, j))
    return pl.pallas_call(
        body, name=name, grid=(nb, tokw // LANE), in_specs=[spec], out_specs=spec,
        out_shape=jax.ShapeDtypeStruct((nb, seq, tokw), F32), compiler_params=_params(),
    )(proj3)


def _pool_bwd(dp3, *, name):
    nb, seq, tokw = dp3.shape
    rows = _chunk_rows(seq)
    group = tokw // len(POOL_WINDOWS)

    def body(d_ref, o_ref):
        col0 = pl.program_id(1) * LANE
        for c in range(seq // rows):
            ext = _load_ext(d_ref, c, rows, 0, MAX_WINDOW, seq)
            lane = col0 + lax.broadcasted_iota(jnp.int32, (1, LANE), 1)
            win = jnp.left_shift(2, jnp.minimum(lane // group, len(POOL_WINDOWS) - 1))
            t = c * rows + lax.broadcasted_iota(jnp.int32, (rows + MAX_WINDOW, 1), 0)
            scaled = ext / jnp.minimum(t + 1, win).astype(F32)
            sums, _ = _window_sums(scaled, _up, col0, group)
            o_ref[c * rows:(c + 1) * rows, :] = sums[:rows, :] - ext[:rows, :]

    spec = pl.BlockSpec((None, seq, LANE), lambda b, j: (b, 0, j))
    return pl.pallas_call(
        body, name=name, grid=(nb, tokw // LANE), in_specs=[spec], out_specs=spec,
        out_shape=jax.ShapeDtypeStruct((nb, seq, tokw), F32), compiler_params=_params(),
    )(dp3)


def _conv3(ext, w_ref, b_ref):
    x1, x2 = _down(ext, 1), _down(ext, 2)
    return w_ref[0:1, :] * x2 + w_ref[1:2, :] * x1 + w_ref[2:3, :] * ext + b_ref[...], x1, x2


def _convgate_fwd(up3, cw, cb, *, name):
    nb, seq, c2 = up3.shape
    fp = c2 // 2
    nblk = fp // LANE
    rows = _chunk_rows(seq)

    def body(u_ref, g_ref, wu_ref, wg_ref, bu_ref, bg_ref, o_ref):
        for c in range(seq // rows):
            hu, _, _ = _conv3(_load_ext(u_ref, c, rows, SUBLANE, 0, seq), wu_ref, bu_ref)
            hg, _, _ = _conv3(_load_ext(g_ref, c, rows, SUBLANE, 0, seq), wg_ref, bg_ref)
            o_ref[c * rows:(c + 1) * rows, :] = (hg * jax.nn.sigmoid(hg) * hu)[SUBLANE:, :]

    def col(off, r):
        return pl.BlockSpec((r, LANE), lambda b, j: (0, j + off))

    def act(off):
        return pl.BlockSpec((None, seq, LANE), lambda b, j: (b, 0, j + off))

    return pl.pallas_call(
        body, name=name, grid=(nb, nblk),
        in_specs=[act(0), act(nblk), col(0, SUBLANE), col(nblk, SUBLANE), col(0, 1), col(nblk, 1)],
        out_specs=act(0), out_shape=jax.ShapeDtypeStruct((nb, seq, fp), F32), compiler_params=_params(),
    )(up3, up3, cw, cw, cb, cb)


def _convgate_bwd(up3, dact3, cw, cb, *, name):
    nb, seq, c2 = up3.shape
    fp = c2 // 2
    nblk = fp // LANE
    rows = _chunk_rows(seq)
    h = SUBLANE

    def body(u_ref, g_ref, da_ref, wu_ref, wg_ref, bu_ref, bg_ref, du_ref, dg_ref, dwu_ref, dwg_ref, dbu_ref,
             dbg_ref):
        @pl.when(pl.program_id(1) == 0)
        def _():
            for r in (dwu_ref, dwg_ref, dbu_ref, dbg_ref):
                r[...] = jnp.zeros(r.shape, F32)

        for c in range(seq // rows):
            eu = _load_ext(u_ref, c, rows, h, h, seq)
            eg = _load_ext(g_ref, c, rows, h, h, seq)
            da = _load_ext(da_ref, c, rows, h, h, seq)
            hu, u1, u2 = _conv3(eu, wu_ref, bu_ref)
            hg, g1, g2 = _conv3(eg, wg_ref, bg_ref)
            sig = jax.nn.sigmoid(hg)
            dhu = da * hg * sig
            dhg = da * hu * sig * (1.0 + hg * (1.0 - sig))
            for dh, w_ref, x0, x1, x2, dx_ref, dw_ref, db_ref in (
                    (dhu, wu_ref, eu, u1, u2, du_ref, dwu_ref, dbu_ref),
                    (dhg, wg_ref, eg, g1, g2, dg_ref, dwg_ref, dbg_ref)):
                dx = w_ref[2:3, :] * dh + w_ref[1:2, :] * _up(dh, 1) + w_ref[0:1, :] * _up(dh, 2)
                dx_ref[c * rows:(c + 1) * rows, :] = dx[h:h + rows, :]
                core = dh[h:h + rows, :]
                for k, xk in ((0, x2), (1, x1), (2, x0)):
                    dw_ref[k:k + 1, :] += jnp.sum(core * xk[h:h + rows, :], axis=0, keepdims=True)
                db_ref[...] += jnp.sum(core, axis=0, keepdims=True)

    def col(off, r):
        return pl.BlockSpec((r, LANE), lambda j, b: (0, j + off))

    def act(off):
        return pl.BlockSpec((None, seq, LANE), lambda j, b: (b, 0, j + off))

    du, dg, dwu, dwg, dbu, dbg = pl.pallas_call(
        body, name=name, grid=(nblk, nb),
        in_specs=[act(0), act(nblk), act(0), col(0, SUBLANE), col(nblk, SUBLANE), col(0, 1), col(nblk, 1)],
        out_specs=[act(0), act(0), col(0, SUBLANE), col(0, SUBLANE), col(0, 1), col(0, 1)],
        out_shape=[jax.ShapeDtypeStruct((nb, seq, fp), F32), jax.ShapeDtypeStruct((nb, seq, fp), F32),
                   jax.ShapeDtypeStruct((SUBLANE, fp), F32), jax.ShapeDtypeStruct((SUBLANE, fp), F32),
                   jax.ShapeDtypeStruct((1, fp), F32), jax.ShapeDtypeStruct((1, fp), F32)],
        compiler_params=_params(),
    )(up3, up3, dact3, cw, cw, cb, cb)
    return (jnp.concatenate([du, dg], axis=2), jnp.concatenate([dwu, dwg], axis=1),
            jnp.concatenate([dbu, dbg], axis=1))


def _scan_rows(x, shift, valid):
    row = lax.broadcasted_iota(jnp.int32, (x.shape[0], 1), 0)
    k = 1
    while k < x.shape[0]:
        x = x + jnp.where(valid(row, k), shift(x, k), 0.0)
        k *= 2
    return x


def _pick_row(x, r):
    row = lax.broadcasted_iota(jnp.int32, (x.shape[0], 1), 0)
    return jnp.sum(jnp.where(row == r, x, 0.0), axis=0, keepdims=True)


def _log_sigmoid(z):
    return jnp.minimum(z, 0.0) - jnp.log(1.0 + jnp.exp(-jnp.abs(z)))


def _gate_fwd(kvf3, fb, col_block, *, name):
    nb, seq, _ = kvf3.shape
    rows = _chunk_rows(seq)

    def body(f_ref, fb_ref, o_ref):
        carry = jnp.zeros((1, LANE), F32)
        for c in range(seq // rows):
            logf = _log_sigmoid(f_ref[c * rows:(c + 1) * rows, :] + fb_ref[...])
            run = _scan_rows(logf, _down, lambda row, k: row >= k) + carry
            o_ref[c * rows:(c + 1) * rows, :] = run
            carry = _pick_row(run, rows - 1)

    return pl.pallas_call(
        body, name=name, grid=(nb,),
        in_specs=[pl.BlockSpec((None, seq, LANE), lambda b: (b, 0, col_block)),
                  pl.BlockSpec((1, LANE), lambda b: (0, 0))],
        out_specs=pl.BlockSpec((None, seq, LANE), lambda b: (b, 0, 0)),
        out_shape=jax.ShapeDtypeStruct((nb, seq, LANE), F32), compiler_params=_params(),
    )(kvf3, fb)


def _gate_bwd(kvf3, fb, dF3, col_block, heads, *, name):
    nb, seq, _ = kvf3.shape
    rows = _chunk_rows(seq)

    def body(f_ref, fb_ref, d_ref, o_ref, dfb_ref):
        @pl.when(pl.program_id(0) == 0)
        def _():
            dfb_ref[...] = jnp.zeros(dfb_ref.shape, F32)

        lane = lax.broadcasted_iota(jnp.int32, (1, LANE), 1)
        carry = jnp.zeros((1, LANE), F32)
        for c in reversed(range(seq // rows)):
            run = _scan_rows(d_ref[c * rows:(c + 1) * rows, :], _up, lambda row, k: row < rows - k) + carry
            carry = _pick_row(run, 0)
            z = f_ref[c * rows:(c + 1) * rows, :] + fb_ref[...]
            df = jnp.where(lane < heads, run * jax.nn.sigmoid(-z), 0.0)
            o_ref[c * rows:(c + 1) * rows, :] = df
            dfb_ref[...] += jnp.sum(df, axis=0, keepdims=True)

    return pl.pallas_call(
        body, name=name, grid=(nb,),
        in_specs=[pl.BlockSpec((None, seq, LANE), lambda b: (b, 0, col_block)),
                  pl.BlockSpec((1, LANE), lambda b: (0, 0)),
                  pl.BlockSpec((None, seq, LANE), lambda b: (b, 0, 0))],
        out_specs=[pl.BlockSpec((None, seq, LANE), lambda b: (b, 0, 0)), pl.BlockSpec((1, LANE), lambda b: (0, 0))],
        out_shape=[jax.ShapeDtypeStruct((nb, seq, LANE), F32), jax.ShapeDtypeStruct((1, LANE), F32)],
        compiler_params=_params(),
    )(kvf3, fb, dF3)


def _head_masks():
    lane = lax.broadcasted_iota(jnp.int32, (1, LANE), 1)
    return (lane < HEAD_DIM, lane >= HEAD_DIM)


def _fox_logits(qh, k, fq, fk, h, causal):
    s = _bdot(qh, k, "nt") * ATT_SCALE + fq[:, h * HEAD_DIM:h * HEAD_DIM + 1] - fk[h:h + 1, :]
    return jnp.where(causal, s, NEG_BIG)


def _causal(qi, ki, tq, tk):
    r = qi * tq + lax.broadcasted_iota(jnp.int32, (tq, tk), 0)
    c = ki * tk + lax.broadcasted_iota(jnp.int32, (tq, tk), 1)
    return r >= c


def _fox_specs(tokw, t, order):
    hp0 = tokw // LANE

    def q_spec(off=0):
        if order == "qk":
            return pl.BlockSpec((None, t, LANE), lambda b, p, qi, ki: (b, qi, p + off))
        return pl.BlockSpec((None, t, LANE), lambda b, p, ki, qi: (b, jnp.maximum(qi, ki), p + off))

    def k_spec(off=0):
        if order == "qk":
            return pl.BlockSpec((None, t, LANE), lambda b, p, qi, ki: (b, jnp.minimum(qi, ki), p + off))
        return pl.BlockSpec((None, t, LANE), lambda b, p, ki, qi: (b, ki, p + off))

    def fk_spec():
        if order == "qk":
            return pl.BlockSpec((None, None, SUBLANE, t), lambda b, p, qi, ki: (b, p, 0, jnp.minimum(qi, ki)))
        return pl.BlockSpec((None, None, SUBLANE, t), lambda b, p, ki, qi: (b, p, 0, ki))

    return q_spec, k_spec, fk_spec, hp0


def _fox_fwd(proj3, kvf3, fq3, fkt, tokw, *, name):
    nb, seq, _ = proj3.shape
    t = _pick(seq, 512, LANE)
    nblk = seq // t
    q_spec, k_spec, fk_spec, hp0 = _fox_specs(tokw, t, "qk")

    def body(q_ref, k_ref, v_ref, fq_ref, fk_ref, o_ref, lse_ref, m_s, l_s, acc_s):
        qi, ki = pl.program_id(2), pl.program_id(3)

        @pl.when(ki == 0)
        def _():
            m_s[...] = jnp.full(m_s.shape, NEG_BIG, F32)
            l_s[...] = jnp.zeros(l_s.shape, F32)
            acc_s[...] = jnp.zeros(acc_s.shape, F32)

        @pl.when(ki <= qi)
        def _():
            q, k, v = q_ref[...], k_ref[...].astype(BF16), v_ref[...].astype(BF16)
            fq, fk = fq_ref[...], fk_ref[...]
            causal = _causal(qi, ki, t, t)
            for h, msk in enumerate(_head_masks()):
                s = _fox_logits(jnp.where(msk, q, 0.0).astype(BF16), k, fq, fk, h, causal)
                m_new = jnp.maximum(m_s[h], jnp.max(s, axis=-1, keepdims=True))
                alpha = jnp.exp(m_s[h] - m_new)
                p = jnp.exp(s - m_new)
                l_s[h] = alpha * l_s[h] + jnp.sum(p, axis=-1, keepdims=True)
                acc_s[h] = alpha * acc_s[h] + _bdot(p, v, "nn")
                m_s[h] = m_new

        @pl.when(ki == nblk - 1)
        def _():
            first = _head_masks()[0]
            o_ref[...] = jnp.where(first, acc_s[0] / l_s[0], acc_s[1] / l_s[1])
            lse_ref[...] = jnp.where(first, m_s[0] + jnp.log(l_s[0]), m_s[1] + jnp.log(l_s[1]))

    out = jax.ShapeDtypeStruct((nb, seq, tokw), F32)
    return pl.pallas_call(
        body, name=name, grid=(nb, hp0, nblk, nblk),
        in_specs=[q_spec(), k_spec(), k_spec(hp0), q_spec(), fk_spec()],
        out_specs=[q_spec(), q_spec()], out_shape=[out, out],
        scratch_shapes=[pltpu.VMEM((2, t, 1), F32), pltpu.VMEM((2, t, 1), F32), pltpu.VMEM((2, t, LANE), F32)],
        compiler_params=_params(),
    )(proj3, kvf3, kvf3, fq3, fkt)


def _fox_bwd_common(q, k, v, fq, fk, o, do, lse, h, msk, causal):
    qh = jnp.where(msk, q, 0.0).astype(BF16)
    doh = jnp.where(msk, do, 0.0)
    s = _fox_logits(qh, k, fq, fk, h, causal)
    p = jnp.exp(s - lse[:, h * HEAD_DIM:h * HEAD_DIM + 1])
    dp = _bdot(doh, v, "nt")
    delta = jnp.sum(doh * o, axis=-1, keepdims=True)
    return qh, doh.astype(BF16), p, p * (dp - delta)


def _fox_bwd_dq(proj3, kvf3, fq3, fkt, o3, dmixin3, lse3, tokw, *, name):
    nb, seq, _ = proj3.shape
    t = _pick(seq, 512, LANE)
    nblk = seq // t
    q_spec, k_spec, fk_spec, hp0 = _fox_specs(tokw, t, "qk")

    def body(q_ref, k_ref, v_ref, fq_ref, fk_ref, o_ref, do_ref, lse_ref, dq_ref, dfq_ref, acc_s, row_s):
        qi, ki = pl.program_id(2), pl.program_id(3)

        @pl.when(ki == 0)
        def _():
            acc_s[...] = jnp.zeros(acc_s.shape, F32)
            row_s[...] = jnp.zeros(row_s.shape, F32)

        @pl.when(ki <= qi)
        def _():
            k, v = k_ref[...].astype(BF16), v_ref[...].astype(BF16)
            causal = _causal(qi, ki, t, t)
            for h, msk in enumerate(_head_masks()):
                _, _, _, ds = _fox_bwd_common(q_ref[...], k, v, fq_ref[...], fk_ref[...], o_ref[...], do_ref[...],
                                              lse_ref[...], h, msk, causal)
                acc_s[h] += _bdot(ds, k, "nn")
                row_s[h] += jnp.sum(ds, axis=-1, keepdims=True)

        @pl.when(ki == nblk - 1)
        def _():
            first = _head_masks()[0]
            dq_ref[...] = jnp.where(first, acc_s[0], acc_s[1]) * ATT_SCALE
            dfq_ref[...] = jnp.where(first, row_s[0], row_s[1])

    out = jax.ShapeDtypeStruct((nb, seq, tokw), F32)
    return pl.pallas_call(
        body, name=name, grid=(nb, hp0, nblk, nblk),
        in_specs=[q_spec(), k_spec(), k_spec(hp0), q_spec(), fk_spec(), q_spec(), q_spec(), q_spec()],
        out_specs=[q_spec(), q_spec()], out_shape=[out, out],
        scratch_shapes=[pltpu.VMEM((2, t, LANE), F32), pltpu.VMEM((2, t, 1), F32)], compiler_params=_params(),
    )(proj3, kvf3, kvf3, fq3, fkt, o3, dmixin3, lse3)


def _fox_bwd_dkv(proj3, kvf3, fq3, fkt, o3, dmixin3, lse3, tokw, *, name):
    nb, seq, _ = proj3.shape
    t = _pick(seq, 512, LANE)
    nblk = seq // t
    q_spec, k_spec, fk_spec, hp0 = _fox_specs(tokw, t, "kq")

    def body(q_ref, k_ref, v_ref, fq_ref, fk_ref, o_ref, do_ref, lse_ref, dk_ref, dv_ref, dfk_ref, dk_s, dv_s,
             dfk_s):
        ki, qi = pl.program_id(2), pl.program_id(3)

        @pl.when(qi == 0)
        def _():
            dk_s[...] = jnp.zeros(dk_s.shape, F32)
            dv_s[...] = jnp.zeros(dv_s.shape, F32)
            dfk_s[...] = jnp.zeros(dfk_s.shape, F32)

        @pl.when(qi >= ki)
        def _():
            k, v = k_ref[...].astype(BF16), v_ref[...].astype(BF16)
            causal = _causal(qi, ki, t, t)
            for h, msk in enumerate(_head_masks()):
                qh, doh, p, ds = _fox_bwd_common(q_ref[...], k, v, fq_ref[...], fk_ref[...], o_ref[...],
                                                 do_ref[...], lse_ref[...], h, msk, causal)
                dv_s[...] += _bdot(p, doh, "tn")
                dk_s[...] += _bdot(ds, qh, "tn")
                dfk_s[h:h + 1, :] -= jnp.sum(ds, axis=0, keepdims=True)

        @pl.when(qi == nblk - 1)
        def _():
            dk_ref[...] = dk_s[...] * ATT_SCALE
            dv_ref[...] = dv_s[...]
            dfk_ref[...] = dfk_s[...]

    out = jax.ShapeDtypeStruct((nb, seq, tokw), F32)
    return pl.pallas_call(
        body, name=name, grid=(nb, hp0, nblk, nblk),
        in_specs=[q_spec(), k_spec(), k_spec(hp0), q_spec(), fk_spec(), q_spec(), q_spec(), q_spec()],
        out_specs=[k_spec(), k_spec(), fk_spec()],
        out_shape=[out, out, jax.ShapeDtypeStruct((nb, hp0, SUBLANE, seq), F32)],
        scratch_shapes=[pltpu.VMEM((t, LANE), F32), pltpu.VMEM((t, LANE), F32), pltpu.VMEM((SUBLANE, t), F32)],
        compiler_params=_params(),
    )(proj3, kvf3, kvf3, fq3, fkt, o3, dmixin3, lse3)


def _peer(k):
    x, y, c = lax.axis_index("x"), lax.axis_index("y"), lax.axis_index("c")
    return (1 - x if k & 4 else x, 1 - y if k & 2 else y, 1 - c if k & 1 else c)


def _dev_index(p):
    return 4 * p[0] + 2 * p[1] + p[2]


def _exchange(send, *, gather, name):
    block = send.shape[-2:]

    def body(s_ref, o_ref, send_sems, recv_sems, local_sem):
        me = _peer(0)
        mine = pltpu.make_async_copy(s_ref if gather else s_ref.at[_dev_index(me)], o_ref.at[_dev_index(me)],
                                     local_sem)
        mine.start()
        sends, recvs = [], []
        for k in range(1, N_DEV):
            peer = _peer(k)
            src = s_ref if gather else s_ref.at[_dev_index(peer)]
            sends.append(pltpu.make_async_remote_copy(
                src_ref=src, dst_ref=o_ref.at[_dev_index(me)], send_sem=send_sems.at[k - 1],
                recv_sem=recv_sems.at[k - 1], device_id=peer, device_id_type=MESH_T))
            recvs.append(pltpu.make_async_remote_copy(
                src_ref=src, dst_ref=o_ref.at[_dev_index(peer)], send_sem=send_sems.at[k - 1],
                recv_sem=recv_sems.at[k - 1], device_id=peer, device_id_type=MESH_T))
        for cp in sends:
            cp.start()
        for cp in recvs:
            cp.wait_recv()
        for cp in sends:
            cp.wait_send()
        mine.wait()

    return pl.pallas_call(
        body, name=name,
        in_specs=[pl.BlockSpec(memory_space=pltpu.HBM)], out_specs=pl.BlockSpec(memory_space=pltpu.HBM),
        out_shape=jax.ShapeDtypeStruct((N_DEV,) + block, send.dtype),
        scratch_shapes=[pltpu.SemaphoreType.DMA((N_DEV - 1,)), pltpu.SemaphoreType.DMA((N_DEV - 1,)),
                        pltpu.SemaphoreType.DMA],
    )(send)


def _swap_halves(res, half_rows, *, name):
    n_arr, _, cols = res.shape

    def body(r_ref, o_ref, send_sems, recv_sems, local_sems):
        c = lax.axis_index("c")
        sib = _peer(1)
        ops = []
        for a in range(n_arr):
            src = r_ref.at[a, pl.ds(0, half_rows)]
            ops.append(pltpu.make_async_copy(src, o_ref.at[a, c], local_sems.at[a]))
            ops.append(pltpu.make_async_remote_copy(
                src_ref=src, dst_ref=o_ref.at[a, c], send_sem=send_sems.at[a], recv_sem=recv_sems.at[a],
                device_id=sib, device_id_type=MESH_T))
        for op in ops:
            op.start()
        for a in range(n_arr):
            pltpu.make_async_remote_copy(
                src_ref=r_ref.at[a, pl.ds(0, half_rows)], dst_ref=o_ref.at[a, 1 - c], send_sem=send_sems.at[a],
                recv_sem=recv_sems.at[a], device_id=sib, device_id_type=MESH_T).wait_recv()
        for a in range(n_arr):
            ops[2 * a].wait()
            ops[2 * a + 1].wait_send()

    return pl.pallas_call(
        body, name=name,
        in_specs=[pl.BlockSpec(memory_space=pltpu.HBM)], out_specs=pl.BlockSpec(memory_space=pltpu.HBM),
        out_shape=jax.ShapeDtypeStruct((n_arr, 2, half_rows, cols), res.dtype),
        scratch_shapes=[pltpu.SemaphoreType.DMA((n_arr,)), pltpu.SemaphoreType.DMA((n_arr,)),
                        pltpu.SemaphoreType.DMA((n_arr,))],
    )(res)


def _adamw(parts, w, m, v, *, name):
    _, rows, cols = parts.shape
    tile = _pick(rows, 256, SUBLANE)
    bc1 = 1.0 - ADAM_B1 ** ADAM_STEP
    bc2 = 1.0 - ADAM_B2 ** ADAM_STEP

    def body(p_ref, w_ref, m_ref, v_ref, o_ref):
        g = p_ref[0]
        for i in range(1, N_DEV):
            g = g + p_ref[i]
        m_new = ADAM_B1 * m_ref[...] + (1.0 - ADAM_B1) * g
        v_new = ADAM_B2 * v_ref[...] + (1.0 - ADAM_B2) * (g * g)
        delta = -ADAM_LR * ((m_new / bc1) / (jnp.sqrt(v_new / bc2) + ADAM_EPS) + ADAM_WD * w_ref[...])
        o_ref[0] = g
        o_ref[1] = delta
        o_ref[2] = m_new
        o_ref[3] = v_new

    spec = pl.BlockSpec((tile, cols), lambda i: (i, 0))
    return pl.pallas_call(
        body, name=name, grid=(rows // tile,),
        in_specs=[pl.BlockSpec((N_DEV, tile, cols), lambda i: (0, i, 0)), spec, spec, spec],
        out_specs=pl.BlockSpec((4, tile, cols), lambda i: (0, i, 0)),
        out_shape=jax.ShapeDtypeStruct((4, rows, cols), F32), compiler_params=_params(),
    )(parts, w, m, v)


def _layout(shapes, names, align):
    out, off = [], 0
    for n in names:
        size = math.prod(shapes[n])
        out.append((n, tuple(shapes[n]), off, size))
        off += _round_up(size, align)
    return out, off


def _pack(arrays, layout, total, lead=()):
    parts = []
    for i, (n, _, off, size) in enumerate(layout):
        end = layout[i + 1][2] if i + 1 < len(layout) else total
        flat = arrays[n].reshape(lead + (size,))
        if end - off > size:
            flat = jnp.pad(flat, [(0, 0)] * len(lead) + [(0, end - off - size)])
        parts.append(flat)
    return jnp.concatenate(parts, axis=len(lead))


def _unpack(flat, layout, lead=()):
    return {n: flat[..., off:off + size].reshape(lead + shape) for n, shape, off, size in layout}


def _to_shards(full, axis):
    shp = full.shape
    return jnp.moveaxis(full.reshape(shp[:axis] + (N_CHIP, shp[axis] // N_CHIP) + shp[axis + 1:]), axis, 0)


def _from_shards(shards, axis):
    x = jnp.moveaxis(shards, 0, axis)
    shp = x.shape
    return x.reshape(shp[:axis] + (shp[axis] * shp[axis + 1],) + shp[axis + 2:])


def _pad_cols(w, per, padded):
    lead = w.shape[:-1]
    x = w.reshape(lead + (-1, per))
    x = jnp.pad(x, [(0, 0)] * len(lead) + [(0, 0), (0, padded - per)])
    return x.reshape(lead + (-1,))


def _unpad_cols(w, per, padded):
    lead = w.shape[:-1]
    return w.reshape(lead + (-1, padded))[..., :per].reshape(lead + (-1,))


def _local_step(x, mem, target, W):
    nb, seq, d = x.shape
    n = nb * seq
    tokw = d - MEM_WIDTH
    heads = tokw // HEAD_DIM
    mlen = mem.shape[1]
    dff2 = W["ffn_w_up"].shape[-1]
    per = dff2 // N_CHIP
    per_p = _round_up(per, LANE)
    fp = 2 * per_p
    kvw = 2 * tokw + heads
    kvp = 2 * tokw + LANE
    gate_block = 2 * tokw // LANE

    x2d = x.reshape(n, d)
    mem2d = mem.reshape(nb * mlen, d)
    t2d = target.reshape(n, d)
    row = lambda a: a.reshape(1, -1)
    ones_tok = jnp.ones((1, tokw), F32)

    pool_bd = jax.scipy.linalg.block_diag(*[W["a_pool_w"][0, i] for i in range(len(POOL_WINDOWS))]).astype(BF16)
    kv_w = jnp.pad(W["kv_w"], ((0, 0), (0, kvp - kvw)))
    fb = jnp.pad(W["f_b"], (0, LANE - heads)).reshape(1, LANE)
    w_up = [_pad_cols(W["ffn_w_up"][l], per, per_p) for l in range(DEPTH)]
    w_down = [jnp.pad(W["ffn_w_down"][l].reshape(2, per, d), ((0, 0), (0, per_p - per), (0, 0))).reshape(fp, d)
              for l in range(DEPTH)]
    conv_w = [jnp.pad(_pad_cols(W["ffn_conv_w"][l], per, per_p), ((0, SUBLANE - CONV_WIDTH), (0, 0)))
              for l in range(DEPTH)]
    conv_b = [_pad_cols(W["ffn_conv_b"][l], per, per_p).reshape(1, 2 * fp) for l in range(DEPTH)]
    w_in = [W["a_w_in"][0], W["b_w_q"][0]]
    w_out = [W["a_w_out"][0], W["b_w_out"][0]]

    saved = []
    cur = x2d
    for l in range(DEPTH):
        s = {"x_in": cur}
        memkv = _mm(mem2d, W["mem_w_kv"][l], "nn", name=f"memkv{l}").reshape(nb, mlen, 2 * MEM_WIDTH)
        if l == 0:
            proj = _mm(cur, w_in[l], "nn", name="proj0")
            pooled = _pool_fwd(proj.reshape(nb, seq, d), tokw, name="pool_fwd").reshape(n, tokw)
            tok = _mm(pooled, pool_bd, "nn", name="pool_mix")
            scale = W["a_pool_scale"].reshape(1, tokw)
            s.update(pooled=pooled, mixed=tok, scale=scale)
        else:
            kvf = _mm(cur, kv_w, "nn", tn=kvp, name="kvf")
            kvf3 = kvf.reshape(nb, seq, kvp)
            gsum = _gate_fwd(kvf3, fb, gate_block, name="gate_fwd")[:, :, :heads]
            fq3 = jnp.repeat(gsum, HEAD_DIM, axis=2)
            fkt = jnp.pad(jnp.swapaxes(gsum, 1, 2).reshape(nb, heads // 2, 2, seq),
                          ((0, 0), (0, 0), (0, SUBLANE - 2), (0, 0)))
            proj = _mm(cur, w_in[l], "nn", name="proj1")
            o3, lse3 = _fox_fwd(proj.reshape(nb, seq, d), kvf3, fq3, fkt, tokw, name="fox_fwd")
            tok = o3.reshape(n, tokw)
            scale = ones_tok
            s.update(kvf3=kvf3, fq3=fq3, fkt=fkt, o3=o3, lse3=lse3)
        mixin = _memattn_fwd(tok, proj, memkv, scale, seq=seq, name=f"memattn_fwd{l}")
        mix = _mm(mixin, w_out[l], "nn", name=f"mix{l}")
        x1 = _ln_fwd(cur, mix, row(W["ln1_g"][l]), row(W["ln1_b"][l]), name=f"ln1_fwd{l}")
        up = _mm(x1, w_up[l], "nn", tn=per_p, name=f"ffn_up{l}")
        act = _convgate_fwd(up.reshape(nb, seq, 2 * fp), conv_w[l], conv_b[l], name=f"convgate_fwd{l}")
        act = act.reshape(n, fp)
        ffn = _mm(act, w_down[l], "nn", tk=fp, name=f"ffn_down{l}")
        s.update(proj=proj, memkv=memkv, mixin=mixin, mix=mix, x1=x1, up=up, act=act, ffn=ffn)
        saved.append(s)
        if l + 1 < DEPTH:
            cur = _ln_fwd(x1, ffn, row(W["ln2_g"][l]), row(W["ln2_b"][l]), name=f"ln2_fwd{l}")

    G = {}
    ln_g = {k: [None] * DEPTH for k in ("ln1_g", "ln1_b", "ln2_g", "ln2_b")}
    stack = {k: [None] * DEPTH for k in ("mem_w_kv", "ffn_w_up", "ffn_conv_w", "ffn_conv_b", "ffn_w_down")}
    dx_terms = None
    loss = None
    for l in reversed(range(DEPTH)):
        s = saved[l]
        g2 = row(W["ln2_g"][l])
        if l == DEPTH - 1:
            dres, dffn, loss, dg, db = _final_ln_loss(s["x1"], s["ffn"], t2d, g2, row(W["ln2_b"][l]),
                                                      name="final_ln_loss")
        else:
            dres, dffn, dg, db = _ln_bwd(s["x1"], s["ffn"], g2, dx_terms, name=f"ln2_bwd{l}")
        ln_g["ln2_g"][l], ln_g["ln2_b"][l] = dg[0], db[0]
        dact = _mm(dffn, w_down[l], "nt", tn=fp, name=f"ffn_down_dx{l}")
        stack["ffn_w_down"][l] = _mm(s["act"], dffn, "tn", tm=per_p, tk=512, name=f"ffn_down_dw{l}")
        dup3, dcw, dcb = _convgate_bwd(s["up"].reshape(nb, seq, 2 * fp), dact.reshape(nb, seq, fp), conv_w[l],
                                       conv_b[l], name=f"convgate_bwd{l}")
        dup = dup3.reshape(n, 2 * fp)
        dx1_ffn = _mm(dup, w_up[l], "nt", tk=per_p, name=f"ffn_up_dx{l}")
        stack["ffn_w_up"][l] = _mm(s["x1"], dup, "tn", tm=d, tn=per_p, tk=512, name=f"ffn_up_dw{l}")
        stack["ffn_conv_w"][l] = dcw[:CONV_WIDTH]
        stack["ffn_conv_b"][l] = dcb[0]
        dres1, dmix, dg, db = _ln_bwd(s["x_in"], s["mix"], row(W["ln1_g"][l]), [dres, dx1_ffn], name=f"ln1_bwd{l}")
        ln_g["ln1_g"][l], ln_g["ln1_b"][l] = dg[0], db[0]
        dmixin = _mm(dmix, w_out[l], "nt", name=f"mix_dx{l}")
        d_w_out = _mm(s["mixin"], dmix, "tn", tm=d, tk=512, name=f"mix_dw{l}")
        if l == 0:
            G["a_w_out"] = d_w_out[None]
            dmixed, dscale = _scale_bwd(dmixin, s["mixed"], s["scale"], name="scale_bwd")
            G["a_pool_scale"] = dscale
            dpooled = _mm(dmixed, pool_bd, "nt", name="pool_mix_dx")
            dpw = _mm(s["pooled"], dmixed, "tn", tm=tokw, tk=512, name="pool_mix_dw")
            grp = tokw // len(POOL_WINDOWS)
            G["a_pool_w"] = jnp.stack([dpw[i * grp:(i + 1) * grp, i * grp:(i + 1) * grp]
                                       for i in range(len(POOL_WINDOWS))])[None]
            dtok = _pool_bwd(dpooled.reshape(nb, seq, tokw), name="pool_bwd").reshape(n, tokw)
            extra = []
        else:
            G["b_w_out"] = d_w_out[None]
            p3 = s["proj"].reshape(nb, seq, d)
            dm3 = dmixin.reshape(nb, seq, d)
            args = (p3, s["kvf3"], s["fq3"], s["fkt"], s["o3"], dm3, s["lse3"], tokw)
            dq3, dfq3 = _fox_bwd_dq(*args, name="fox_bwd_dq")
            dtok = dq3.reshape(n, tokw)
            dk3, dv3, dfk = _fox_bwd_dkv(*args, name="fox_bwd_dkv")
            dgsum = jnp.swapaxes(dfk[:, :, :2, :].reshape(nb, heads, seq), 1, 2) + dfq3[:, :, ::HEAD_DIM]
            dgsum = jnp.pad(dgsum, ((0, 0), (0, 0), (0, LANE - heads)))
            df3, dfb = _gate_bwd(s["kvf3"], fb, dgsum, gate_block, heads, name="gate_bwd")
            G["f_b"] = dfb[0, :heads]
            dkvf = jnp.concatenate([dk3, dv3, df3], axis=2).reshape(n, kvp)
            extra = [_mm(dkvf, kv_w, "nt", tk=kvp, name="kvf_dx")]
            G["kv_w"] = _mm(s["x_in"], dkvf, "tn", tm=d, tn=kvp, tk=512, name="kvf_dw")[:, :kvw]
        dproj, dmemkv = _memattn_bwd(dmixin, dtok, s["proj"], s["memkv"], seq=seq, name=f"memattn_bwd{l}")
        stack["mem_w_kv"][l] = _mm(mem2d, dmemkv.reshape(nb * mlen, 2 * MEM_WIDTH), "tn", tm=d, tk=512,
                                   name=f"memkv_dw{l}")
        G["a_w_in" if l == 0 else "b_w_q"] = _mm(s["x_in"], dproj, "tn", tm=d, tk=512, name=f"proj_dw{l}")[None]
        if l == 0:
            grad_x = _mm(dproj, w_in[l], "nt", adds=[dres1], name="proj_dx0")
        else:
            dx_terms = [dres1, _mm(dproj, w_in[l], "nt", name="proj_dx1")] + extra
    for k, v in ln_g.items():
        G[k] = jnp.stack(v)
    G["mem_w_kv"] = jnp.stack(stack["mem_w_kv"])
    G["ffn_w_up"] = jnp.stack([_unpad_cols(g, per, per_p) for g in stack["ffn_w_up"]])
    G["ffn_conv_w"] = jnp.stack([_unpad_cols(g, per, per_p) for g in stack["ffn_conv_w"]])
    G["ffn_conv_b"] = jnp.stack([_unpad_cols(g, per, per_p) for g in stack["ffn_conv_b"]])
    G["ffn_w_down"] = jnp.stack([g.reshape(2, per_p, d)[:, :per].reshape(2 * per, d) for g in stack["ffn_w_down"]])
    return loss[0, 0], grad_x.reshape(nb, seq, d), G


def kernel(x, mem, a_w_in, a_pool_w, a_pool_scale, a_w_out, b_w_q, b_w_out, kv_w, f_b, mem_w_kv, ln1_g, ln1_b, ln2_g, ln2_b, ffn_w_up, ffn_conv_w, ffn_conv_b, ffn_w_down, loss_target, m_a_w_in, m_a_pool_w, m_a_pool_scale, m_a_w_out, m_b_w_q, m_b_w_out, m_kv_w, m_f_b, m_mem_w_kv, m_ln1_g, m_ln1_b, m_ln2_g, m_ln2_b, m_ffn_w_up, m_ffn_conv_w, m_ffn_conv_b, m_ffn_w_down, v_a_w_in, v_a_pool_w, v_a_pool_scale, v_a_w_out, v_b_w_q, v_b_w_out, v_kv_w, v_f_b, v_mem_w_kv, v_ln1_g, v_ln1_b, v_ln2_g, v_ln2_b, v_ffn_w_up, v_ffn_conv_w, v_ffn_conv_b, v_ffn_w_down):
    w_loc = dict(a_w_in=a_w_in, a_pool_w=a_pool_w, a_pool_scale=a_pool_scale, a_w_out=a_w_out, b_w_q=b_w_q,
                 b_w_out=b_w_out, kv_w=kv_w, f_b=f_b, mem_w_kv=mem_w_kv, ln1_g=ln1_g, ln1_b=ln1_b, ln2_g=ln2_g,
                 ln2_b=ln2_b, ffn_w_up=ffn_w_up, ffn_conv_w=ffn_conv_w, ffn_conv_b=ffn_conv_b, ffn_w_down=ffn_w_down)
    m_loc = dict(a_w_in=m_a_w_in, a_pool_w=m_a_pool_w, a_pool_scale=m_a_pool_scale, a_w_out=m_a_w_out,
                 b_w_q=m_b_w_q, b_w_out=m_b_w_out, kv_w=m_kv_w, f_b=m_f_b, mem_w_kv=m_mem_w_kv, ln1_g=m_ln1_g,
                 ln1_b=m_ln1_b, ln2_g=m_ln2_g, ln2_b=m_ln2_b, ffn_w_up=m_ffn_w_up, ffn_conv_w=m_ffn_conv_w,
                 ffn_conv_b=m_ffn_conv_b, ffn_w_down=m_ffn_w_down)
    v_loc = dict(a_w_in=v_a_w_in, a_pool_w=v_a_pool_w, a_pool_scale=v_a_pool_scale, a_w_out=v_a_w_out,
                 b_w_q=v_b_w_q, b_w_out=v_b_w_out, kv_w=v_kv_w, f_b=v_f_b, mem_w_kv=v_mem_w_kv, ln1_g=v_ln1_g,
                 ln1_b=v_ln1_b, ln2_g=v_ln2_g, ln2_b=v_ln2_b, ffn_w_up=v_ffn_w_up, ffn_conv_w=v_ffn_conv_w,
                 ffn_conv_b=v_ffn_conv_b, ffn_w_down=v_ffn_w_down)
    c = lax.axis_index("c")
    shapes = {k: v.shape for k, v in w_loc.items()}
    lay_s, tot_s = _layout(shapes, SHARDED, 2 * PACK_COLS)
    lay_r, tot_r = _layout(shapes, REPLICATED, PACK_COLS)
    half_rows = tot_s // (2 * PACK_COLS)
    rep_rows = tot_r // PACK_COLS
    rows = _round_up(half_rows + rep_rows, 256)

    def mine(packed_shard, packed_rep):
        half = lax.dynamic_slice_in_dim(packed_shard.reshape(2 * half_rows, PACK_COLS), c * half_rows, half_rows)
        return jnp.concatenate([half, packed_rep.reshape(rep_rows, PACK_COLS),
                                jnp.zeros((rows - half_rows - rep_rows, PACK_COLS), F32)], axis=0)

    w_mine = mine(_pack(w_loc, lay_s, tot_s), _pack(w_loc, lay_r, tot_r))
    gathered = _exchange(w_mine[:half_rows].astype(BF16), gather=True, name="gather_weights")
    by_chip = _unpack(gathered.reshape(N_CHIP, tot_s), lay_s, lead=(N_CHIP,))
    W = {k: _from_shards(by_chip[k], SHARD_AXIS[k]) for k in SHARDED if k not in GATHER_F32}
    lay_f, tot_f = _layout(shapes, GATHER_F32, PACK_COLS)
    small = _exchange(_pack(w_loc, lay_f, tot_f).reshape(tot_f // PACK_COLS, PACK_COLS), gather=True,
                      name="gather_small")
    by_chip = _unpack(small[::2].reshape(N_CHIP, tot_f), lay_f, lead=(N_CHIP,))
    for k in GATHER_F32:
        W[k] = _from_shards(by_chip[k], SHARD_AXIS[k])
    for k in REPLICATED:
        W[k] = w_loc[k]

    loss_part, grad_x, G = _local_step(x, mem, loss_target, W)
    loss = lax.psum(loss_part, ("x", "y", "c"))

    g_sh = _pack({k: _to_shards(G[k], SHARD_AXIS[k]) for k in SHARDED}, lay_s, tot_s, lead=(N_CHIP,))
    g_rep = _pack({k: G[k].reshape(shapes[k]) for k in REPLICATED}, lay_r, tot_r).reshape(rep_rows, PACK_COLS)
    send = jnp.concatenate([
        g_sh.reshape(N_DEV, half_rows, PACK_COLS),
        jnp.broadcast_to(g_rep, (N_DEV, rep_rows, PACK_COLS)),
        jnp.zeros((N_DEV, rows - half_rows - rep_rows, PACK_COLS), F32)], axis=1)
    parts = _exchange(send, gather=False, name="scatter_grads")
    m_mine = mine(_pack(m_loc, lay_s, tot_s), _pack(m_loc, lay_r, tot_r))
    v_mine = mine(_pack(v_loc, lay_s, tot_s), _pack(v_loc, lay_r, tot_r))
    res = _adamw(parts, w_mine, m_mine, v_mine, name="adamw")
    both = _swap_halves(res, half_rows, name="swap_halves")

    out_s = _unpack(both.reshape(4, tot_s), lay_s, lead=(4,))
    out_r = _unpack(res[:, half_rows:half_rows + rep_rows].reshape(4, tot_r), lay_r, lead=(4,))
    outs = [loss, grad_x]
    for a in range(4):
        for k in WEIGHTS:
            outs.append((out_s if k in SHARDED else out_r)[k][a])
    return tuple(outs)
```

```python
import functools
import math

import jax
import jax.numpy as jnp
from jax import lax
from jax.experimental import pallas as pl
from jax.experimental.pallas import tpu as pltpu

F32 = jnp.float32
BF16 = jnp.bfloat16

HEAD_DIM = 64
MEM_HEADS = 4
MEM_WIDTH = MEM_HEADS * HEAD_DIM
POOL_WINDOWS = (2, 4, 8, 16)
MAX_WINDOW = 16
CONV_WIDTH = 3
DEPTH = 2
DN_ALPHA = (2.0 * DEPTH) ** 0.25
LN_EPS = 1e-5
ATT_SCALE = HEAD_DIM ** -0.5
NEG_BIG = -1e30

ADAM_LR = 0.001
ADAM_B1 = 0.9
ADAM_B2 = 0.999
ADAM_EPS = 1e-08
ADAM_WD = 0.01
ADAM_STEP = 10

LANE = 128
SUBLANE = 8
PACK_COLS = 1024
VMEM_LIMIT = 56 * 1024 * 1024
N_DEV = 8
N_CHIP = 4
MESH_T = pl.DeviceIdType.MESH

SHARDED = ("a_w_in", "a_pool_scale", "a_w_out", "b_w_q", "b_w_out", "kv_w", "mem_w_kv", "ffn_w_up",
           "ffn_conv_w", "ffn_w_down")
SHARD_AXIS = {"a_w_in": 1, "a_pool_scale": 1, "a_w_out": 1, "b_w_q": 1, "b_w_out": 1, "kv_w": 1, "mem_w_kv": 1,
              "ffn_w_up": 2, "ffn_conv_w": 2, "ffn_w_down": 1}
GATHER_F32 = ("a_pool_scale", "ffn_conv_w")
BIG = tuple(k for k in SHARDED if k != "a_pool_scale")
REPLICATED = ("a_pool_w", "f_b", "ln1_g", "ln1_b", "ln2_g", "ln2_b", "ffn_conv_b")
WEIGHTS = ("a_w_in", "a_pool_w", "a_pool_scale", "a_w_out", "b_w_q", "b_w_out", "kv_w", "f_b", "mem_w_kv",
           "ln1_g", "ln1_b", "ln2_g", "ln2_b", "ffn_w_up", "ffn_conv_w", "ffn_conv_b", "ffn_w_down")


def _round_up(n, m):
    return -(-n // m) * m


def _pick(dim, pref, unit=LANE):
    if dim <= pref:
        return dim
    t = (pref // unit) * unit
    while t >= unit:
        if dim % t == 0:
            return t
        t -= unit
    raise ValueError(f"no tile for {dim} <= {pref}")


def _params():
    return pltpu.CompilerParams(vmem_limit_bytes=VMEM_LIMIT)


_DIMS = {"nn": ((1,), (0,)), "nt": ((1,), (1,)), "tn": ((0,), (0,))}


def _bdot(a, b, mode):
    return lax.dot_general(a.astype(BF16), b.astype(BF16), (_DIMS[mode], ((), ())), preferred_element_type=F32)


def _mm(a, b, mode, *, name, tm=512, tn=1024, tk=2048, adds=(), b_col0=0):
    if mode == "nn":
        (M, K), (K2, N) = a.shape, b.shape
    elif mode == "nt":
        (M, K), (N, K2) = a.shape, b.shape
        K2 = K if b_col0 + K <= K2 else -1
    else:
        (K, M), (K2, N) = a.shape, b.shape
    assert K == K2 and (mode == "nt" or b_col0 == 0), (name, a.shape, b.shape)
    tm, tn = _pick(M, tm, SUBLANE if mode != "tn" else LANE), _pick(N, tn)
    tk = _pick(K, tk, LANE if mode != "tn" else SUBLANE)
    nk = K // tk
    assert b_col0 % tk == 0, (name, b_col0, tk)
    koff = b_col0 // tk
    n_add = len(adds)

    def body(*refs):
        a_ref, b_ref = refs[0], refs[1]
        add_refs = refs[2:2 + n_add]
        o_ref, acc_ref = refs[2 + n_add], refs[3 + n_add]
        part = _bdot(a_ref[...], b_ref[...], mode)

        def finish(r):
            for ar in add_refs:
                r = r + ar[...]
            o_ref[...] = r

        if nk == 1:
            finish(part)
        else:
            k = pl.program_id(2)

            @pl.when(k == 0)
            def _():
                acc_ref[...] = part

            @pl.when(k > 0)
            def _():
                acc_ref[...] += part

            @pl.when(k == nk - 1)
            def _():
                finish(acc_ref[...])

    if mode == "nn":
        a_spec = pl.BlockSpec((tm, tk), lambda i, j, k: (i, k))
        b_spec = pl.BlockSpec((tk, tn), lambda i, j, k: (k, j))
    elif mode == "nt":
        a_spec = pl.BlockSpec((tm, tk), lambda i, j, k: (i, k))
        b_spec = pl.BlockSpec((tn, tk), lambda i, j, k: (j, k + koff))
    else:
        a_spec = pl.BlockSpec((tk, tm), lambda i, j, k: (k, i))
        b_spec = pl.BlockSpec((tk, tn), lambda i, j, k: (k, j))
    o_spec = pl.BlockSpec((tm, tn), lambda i, j, k: (i, j))
    acc_shape = (tm, tn) if nk > 1 else (SUBLANE, LANE)
    return pl.pallas_call(
        body, name=name, grid=(M // tm, N // tn, nk),
        in_specs=[a_spec, b_spec] + [o_spec] * n_add, out_specs=o_spec,
        out_shape=jax.ShapeDtypeStruct((M, N), F32),
        scratch_shapes=[pltpu.VMEM(acc_shape, F32)],
        compiler_params=_params(),
    )(a, b, *adds)


def _rowwise(fn, tiled, full, outs_tiled, outs_acc, *, rows, tile, name, acc_period=None):
    n_tiles = rows // tile
    period = n_tiles if acc_period is None else acc_period
    arrays, in_specs = [], []
    for t in tiled:
        arr, width, cb = t if isinstance(t, tuple) else (t, t.shape[1], 0)
        arrays.append(arr)
        in_specs.append(pl.BlockSpec((tile, width), lambda i, cb=cb: (i, cb)))
    for f in full:
        arr, spec = f if isinstance(f, tuple) else (f, None)
        arrays.append(arr)
        in_specs.append(spec if spec is not None else pl.BlockSpec(arr.shape, lambda i, nd=arr.ndim: (0,) * nd))
    out_shape, out_specs = [], []
    for width, dt in outs_tiled:
        out_shape.append(jax.ShapeDtypeStruct((rows, width), dt))
        out_specs.append(pl.BlockSpec((tile, width), lambda i: (i, 0)))
    for acc in outs_acc:
        shape, dt = acc[0], acc[1]
        out_shape.append(jax.ShapeDtypeStruct(shape, dt))
        out_specs.append(acc[2] if len(acc) > 2 else pl.BlockSpec(shape, lambda i, nd=len(shape): (0,) * nd))
    n_in, n_t, n_a = len(arrays), len(outs_tiled), len(outs_acc)

    def body(*refs):
        vals = [r[...] for r in refs[:n_in]]
        o_t, o_a = fn(*vals)
        for r, v in zip(refs[n_in:n_in + n_t], o_t):
            r[...] = v.astype(r.dtype)
        first = pl.program_id(0) % period == 0
        for r, v in zip(refs[n_in + n_t:n_in + n_t + n_a], o_a):
            v = v.reshape(r.shape)

            @pl.when(first)
            def _(r=r, v=v):
                r[...] = v

            @pl.when(jnp.logical_not(first))
            def _(r=r, v=v):
                r[...] += v

    return pl.pallas_call(
        body, name=name, grid=(n_tiles,), in_specs=in_specs, out_specs=out_specs, out_shape=out_shape,
        compiler_params=_params(),
    )(*arrays)


def _ln_stats(h):
    mu = jnp.mean(h, axis=-1, keepdims=True)
    d = h - mu
    var = jnp.mean(d * d, axis=-1, keepdims=True)
    rstd = lax.rsqrt(var + LN_EPS)
    return d * rstd, rstd


def _ln_bwd_math(h, g, dy):
    xhat, rstd = _ln_stats(h)
    dxhat = dy * g
    dh = rstd * (dxhat - jnp.mean(dxhat, axis=-1, keepdims=True)
                 - xhat * jnp.mean(dxhat * xhat, axis=-1, keepdims=True))
    return dh, jnp.sum(dy * xhat, axis=0, keepdims=True), jnp.sum(dy, axis=0, keepdims=True)


def _ln_fwd(x, r, g, b, *, name):
    n, d = x.shape

    def fn(x, r, g, b):
        xhat, _ = _ln_stats(DN_ALPHA * x + r)
        return (xhat * g + b,), ()

    return _rowwise(fn, [x, r], [g, b], [(d, F32)], [], rows=n, tile=_pick(n, 512, SUBLANE), name=name)[0]


def _ln_bwd(x, r, g, dys, *, name):
    n, d = x.shape
    n_dy = len(dys)

    def fn(x, r, *rest):
        dy = rest[0]
        for e in rest[1:n_dy]:
            dy = dy + e
        dh, dg, db = _ln_bwd_math(DN_ALPHA * x + r, rest[n_dy], dy)
        return (DN_ALPHA * dh, dh), (dg, db)

    return _rowwise(fn, [x, r, *dys], [g], [(d, F32), (d, F32)], [((1, d), F32), ((1, d), F32)],
                    rows=n, tile=_pick(n, 256, SUBLANE), name=name)


def _final_ln_loss(x, r, target, g, b, *, name):
    n, d = x.shape

    def fn(x, r, t, g, b):
        h = DN_ALPHA * x + r
        xhat, _ = _ln_stats(h)
        err = xhat * g + b - t
        loss = jnp.full((1, LANE), 0.5 * jnp.sum(err * err) / d, F32)
        dh, dg, db = _ln_bwd_math(h, g, err / d)
        return (DN_ALPHA * dh, dh), (loss, dg, db)

    return _rowwise(fn, [x, r, target], [g, b], [(d, F32), (d, F32)],
                    [((1, LANE), F32), ((1, d), F32), ((1, d), F32)],
                    rows=n, tile=_pick(n, 256, SUBLANE), name=name)


def _mem_heads(qm):
    lane = lax.broadcasted_iota(jnp.int32, (1, MEM_WIDTH), 1)
    for h in range(MEM_HEADS):
        msk = (lane >= h * HEAD_DIM) & (lane < (h + 1) * HEAD_DIM)
        yield msk, jnp.where(msk, qm, 0.0).astype(BF16)


def _mem_softmax(qh, k):
    s = _bdot(qh, k, "nt") * ATT_SCALE
    p = jnp.exp(s - jnp.max(s, axis=-1, keepdims=True))
    return p / jnp.sum(p, axis=-1, keepdims=True)


def _memattn_fwd(tok, proj, memkv, scale, *, seq, name):
    n, tokw = tok.shape
    d = tokw + MEM_WIDTH
    tile = _pick(seq, 512, SUBLANE)

    def fn(tok, qm, kv, scale):
        k, v = kv[:, :MEM_WIDTH].astype(BF16), kv[:, MEM_WIDTH:].astype(BF16)
        out = jnp.zeros(qm.shape, F32)
        for msk, qh in _mem_heads(qm):
            out = jnp.where(msk, _bdot(_mem_softmax(qh, k), v, "nn"), out)
        return (jnp.concatenate([tok * scale, out], axis=1),), ()

    kv_spec = pl.BlockSpec((None,) + memkv.shape[1:], lambda i: (i // (seq // tile), 0, 0))
    return _rowwise(fn, [tok, (proj, MEM_WIDTH, tokw // MEM_WIDTH)], [(memkv, kv_spec), scale], [(d, F32)], [],
                    rows=n, tile=tile, name=name)[0]


def _memattn_bwd(dmixin, dtok, proj, memkv, *, seq, name):
    n, tokw = dtok.shape
    d = tokw + MEM_WIDTH
    tile = _pick(seq, 512, SUBLANE)

    def fn(dmo, dtok, qm, kv):
        k, v = kv[:, :MEM_WIDTH].astype(BF16), kv[:, MEM_WIDTH:].astype(BF16)
        dq = jnp.zeros(qm.shape, F32)
        dk = jnp.zeros(k.shape, F32)
        dv = jnp.zeros(v.shape, F32)
        for msk, qh in _mem_heads(qm):
            p = _mem_softmax(qh, k)
            doh = jnp.where(msk, dmo, 0.0).astype(BF16)
            dv = dv + _bdot(p, doh, "tn")
            dp = _bdot(doh, v, "nt")
            ds = (p * (dp - jnp.sum(dp * p, axis=-1, keepdims=True))).astype(BF16)
            dq = jnp.where(msk, _bdot(ds, k, "nn") * ATT_SCALE, dq)
            dk = dk + _bdot(ds, qh, "tn") * ATT_SCALE
        return (jnp.concatenate([dtok, dq], axis=1),), (jnp.concatenate([dk, dv], axis=1),)

    tpe = seq // tile
    kv_spec = pl.BlockSpec((None,) + memkv.shape[1:], lambda i: (i // tpe, 0, 0))
    return _rowwise(fn, [(dmixin, MEM_WIDTH, tokw // MEM_WIDTH), dtok, (proj, MEM_WIDTH, tokw // MEM_WIDTH)],
                    [(memkv, kv_spec)], [(d, F32)], [(memkv.shape, F32, kv_spec)],
                    rows=n, tile=tile, name=name, acc_period=tpe)


def _scale_bwd(dmixin, mixed, scale, *, name):
    n, tokw = mixed.shape

    def fn(dt, mixed, scale):
        return (dt * scale,), (jnp.sum(dt * mixed, axis=0, keepdims=True),)

    return _rowwise(fn, [(dmixin, tokw, 0), mixed], [scale], [(tokw, F32)], [((1, tokw), F32)],
                    rows=n, tile=_pick(n, 512, SUBLANE), name=name)


def _chunk_rows(seq):
    return _pick(seq, 512, SUBLANE)


def _load_ext(ref, c, rows, before, after, seq):
    lo, hi = c * rows - before, (c + 1) * rows + after
    parts = []
    if lo < 0:
        parts.append(jnp.zeros((-lo, ref.shape[1]), F32))
    parts.append(ref[max(lo, 0):min(hi, seq), :])
    if hi > seq:
        parts.append(jnp.zeros((hi - seq, ref.shape[1]), F32))
    return parts[0] if len(parts) == 1 else jnp.concatenate(parts, axis=0)


def _down(x, k):
    return pltpu.roll(x, k, 0)


def _up(x, k):
    return pltpu.roll(x, x.shape[0] - k, 0)


def _window_sums(ext, shift, col0, group):
    lane = col0 + lax.broadcasted_iota(jnp.int32, (1, ext.shape[1]), 1)
    gidx = lane // group
    s = ext
    out = None
    k = 1
    for gi, w in enumerate(POOL_WINDOWS):
        while k < w:
            s = s + shift(s, k)
            k *= 2
        out = s if out is None else jnp.where(gidx >= gi, s, out)
    return out, jnp.left_shift(2, jnp.minimum(gidx, len(POOL_WINDOWS) - 1))


def _pool_fwd(proj3, tokw, *, name):
    nb, seq, _ = proj3.shape
    rows = _chunk_rows(seq)
    group = tokw // len(POOL_WINDOWS)

    def body(u_ref, o_ref):
        col0 = pl.program_id(1) * LANE
        for c in range(seq // rows):
            ext = _load_ext(u_ref, c, rows, MAX_WINDOW, 0, seq)
            sums, win = _window_sums(ext, _down, col0, group)
            t = c * rows + lax.broadcasted_iota(jnp.int32, (rows, 1), 0)
            count = jnp.minimum(t + 1, win).astype(F32)
            o_ref[c * rows:(c + 1) * rows, :] = sums[MAX_WINDOW:, :] / count - ext[MAX_WINDOW:, :]

    spec = pl.BlockSpec((None, seq, LANE), lambda b, j: (b, 0, j))
    return pl.pallas_call(
        body, name=name, grid=(nb, tokw // LANE), in_specs=[spec], out_specs=spec,
        out_shape=jax.ShapeDtypeStruct((nb, seq, tokw), F32), compiler_params=_params(),
    )(proj3)


def _pool_bwd(dp3, *, name):
    nb, seq, tokw = dp3.shape
    rows = _chunk_rows(seq)
    group = tokw // len(POOL_WINDOWS)

    def body(d_ref, o_ref):
        col0 = pl.program_id(1) * LANE
        for c in range(seq // rows):
            ext = _load_ext(d_ref, c, rows, 0, MAX_WINDOW, seq)
            lane = col0 + lax.broadcasted_iota(jnp.int32, (1, LANE), 1)
            win = jnp.left_shift(2, jnp.minimum(lane // group, len(POOL_WINDOWS) - 1))
            t = c * rows + lax.broadcasted_iota(jnp.int32, (rows + MAX_WINDOW, 1), 0)
            scaled = ext / jnp.minimum(t + 1, win).astype(F32)
            sums, _ = _window_sums(scaled, _up, col0, group)
            o_ref[c * rows:(c + 1) * rows, :] = sums[:rows, :] - ext[:rows, :]

    spec = pl.BlockSpec((None, seq, LANE), lambda b, j: (b, 0, j))
    return pl.pallas_call(
        body, name=name, grid=(nb, tokw // LANE), in_specs=[spec], out_specs=spec,
        out_shape=jax.ShapeDtypeStruct((nb, seq, tokw), F32), compiler_params=_params(),
    )(dp3)


def _conv3(ext, w_ref, b_ref):
    x1, x2 = _down(ext, 1), _down(ext, 2)
    return w_ref[0:1, :] * x2 + w_ref[1:2, :] * x1 + w_ref[2:3, :] * ext + b_ref[...], x1, x2


def _convgate_fwd(up3, cw, cb, *, name):
    nb, seq, c2 = up3.shape
    fp = c2 // 2
    nblk = fp // LANE
    rows = _chunk_rows(seq)

    def body(u_ref, g_ref, wu_ref, wg_ref, bu_ref, bg_ref, o_ref):
        for c in range(seq // rows):
            hu, _, _ = _conv3(_load_ext(u_ref, c, rows, SUBLANE, 0, seq), wu_ref, bu_ref)
            hg, _, _ = _conv3(_load_ext(g_ref, c, rows, SUBLANE, 0, seq), wg_ref, bg_ref)
            o_ref[c * rows:(c + 1) * rows, :] = (hg * jax.nn.sigmoid(hg) * hu)[SUBLANE:, :]

    def col(off, r):
        return pl.BlockSpec((r, LANE), lambda b, j: (0, j + off))

    def act(off):
        return pl.BlockSpec((None, seq, LANE), lambda b, j: (b, 0, j + off))

    return pl.pallas_call(
        body, name=name, grid=(nb, nblk),
        in_specs=[act(0), act(nblk), col(0, SUBLANE), col(nblk, SUBLANE), col(0, 1), col(nblk, 1)],
        out_specs=act(0), out_shape=jax.ShapeDtypeStruct((nb, seq, fp), F32), compiler_params=_params(),
    )(up3, up3, cw, cw, cb, cb)


def _convgate_bwd(up3, dact3, cw, cb, *, name):
    nb, seq, c2 = up3.shape
    fp = c2 // 2
    nblk = fp // LANE
    rows = _chunk_rows(seq)
    h = SUBLANE

    def body(u_ref, g_ref, da_ref, wu_ref, wg_ref, bu_ref, bg_ref, du_ref, dg_ref, dwu_ref, dwg_ref, dbu_ref,
             dbg_ref):
        @pl.when(pl.program_id(1) == 0)
        def _():
            for r in (dwu_ref, dwg_ref, dbu_ref, dbg_ref):
                r[...] = jnp.zeros(r.shape, F32)

        for c in range(seq // rows):
            eu = _load_ext(u_ref, c, rows, h, h, seq)
            eg = _load_ext(g_ref, c, rows, h, h, seq)
            da = _load_ext(da_ref, c, rows, h, h, seq)
            hu, u1, u2 = _conv3(eu, wu_ref, bu_ref)
            hg, g1, g2 = _conv3(eg, wg_ref, bg_ref)
            sig = jax.nn.sigmoid(hg)
            dhu = da * hg * sig
            dhg = da * hu * sig * (1.0 + hg * (1.0 - sig))
            for dh, w_ref, x0, x1, x2, dx_ref, dw_ref, db_ref in (
                    (dhu, wu_ref, eu, u1, u2, du_ref, dwu_ref, dbu_ref),
                    (dhg, wg_ref, eg, g1, g2, dg_ref, dwg_ref, dbg_ref)):
                dx = w_ref[2:3, :] * dh + w_ref[1:2, :] * _up(dh, 1) + w_ref[0:1, :] * _up(dh, 2)
                dx_ref[c * rows:(c + 1) * rows, :] = dx[h:h + rows, :]
                core = dh[h:h + rows, :]
                for k, xk in ((0, x2), (1, x1), (2, x0)):
                    dw_ref[k:k + 1, :] += jnp.sum(core * xk[h:h + rows, :], axis=0, keepdims=True)
                db_ref[...] += jnp.sum(core, axis=0, keepdims=True)

    def col(off, r):
        return pl.BlockSpec((r, LANE), lambda j, b: (0, j + off))

    def act(off):
        return pl.BlockSpec((None, seq, LANE), lambda j, b: (b, 0, j + off))

    du, dg, dwu, dwg, dbu, dbg = pl.pallas_call(
        body, name=name, grid=(nblk, nb),
        in_specs=[act(0), act(nblk), act(0), col(0, SUBLANE), col(nblk, SUBLANE), col(0, 1), col(nblk, 1)],
        out_specs=[act(0), act(0), col(0, SUBLANE), col(0, SUBLANE), col(0, 1), col(0, 1)],
        out_shape=[jax.ShapeDtypeStruct((nb, seq, fp), F32), jax.ShapeDtypeStruct((nb, seq, fp), F32),
                   jax.ShapeDtypeStruct((SUBLANE, fp), F32), jax.ShapeDtypeStruct((SUBLANE, fp), F32),
                   jax.ShapeDtypeStruct((1, fp), F32), jax.ShapeDtypeStruct((1, fp), F32)],
        compiler_params=_params(),
    )(up3, up3, dact3, cw, cw, cb, cb)
    return du, dg, jnp.concatenate([dwu, dwg], axis=1), jnp.concatenate([dbu, dbg], axis=1)


def _scan_rows(x, shift, valid):
    row = lax.broadcasted_iota(jnp.int32, (x.shape[0], 1), 0)
    k = 1
    while k < x.shape[0]:
        x = x + jnp.where(valid(row, k), shift(x, k), 0.0)
        k *= 2
    return x


def _pick_row(x, r):
    row = lax.broadcasted_iota(jnp.int32, (x.shape[0], 1), 0)
    return jnp.sum(jnp.where(row == r, x, 0.0), axis=0, keepdims=True)


def _log_sigmoid(z):
    return jnp.minimum(z, 0.0) - jnp.log(1.0 + jnp.exp(-jnp.abs(z)))


def _gate_fwd(kvf3, fb, col_block, *, name):
    nb, seq, _ = kvf3.shape
    rows = _chunk_rows(seq)

    def body(f_ref, fb_ref, o_ref):
        carry = jnp.zeros((1, LANE), F32)
        for c in range(seq // rows):
            logf = _log_sigmoid(f_ref[c * rows:(c + 1) * rows, :] + fb_ref[...])
            run = _scan_rows(logf, _down, lambda row, k: row >= k) + carry
            o_ref[c * rows:(c + 1) * rows, :] = run
            carry = _pick_row(run, rows - 1)

    return pl.pallas_call(
        body, name=name, grid=(nb,),
        in_specs=[pl.BlockSpec((None, seq, LANE), lambda b: (b, 0, col_block)),
                  pl.BlockSpec((1, LANE), lambda b: (0, 0))],
        out_specs=pl.BlockSpec((None, seq, LANE), lambda b: (b, 0, 0)),
        out_shape=jax.ShapeDtypeStruct((nb, seq, LANE), F32), compiler_params=_params(),
    )(kvf3, fb)


def _gate_bwd(kvf3, fb, dF3, col_block, heads, *, name):
    nb, seq, _ = kvf3.shape
    rows = _chunk_rows(seq)

    def body(f_ref, fb_ref, d_ref, o_ref, dfb_ref):
        @pl.when(pl.program_id(0) == 0)
        def _():
            dfb_ref[...] = jnp.zeros(dfb_ref.shape, F32)

        lane = lax.broadcasted_iota(jnp.int32, (1, LANE), 1)
        carry = jnp.zeros((1, LANE), F32)
        for c in reversed(range(seq // rows)):
            run = _scan_rows(d_ref[c * rows:(c + 1) * rows, :], _up, lambda row, k: row < rows - k) + carry
            carry = _pick_row(run, 0)
            z = f_ref[c * rows:(c + 1) * rows, :] + fb_ref[...]
            df = jnp.where(lane < heads, run * jax.nn.sigmoid(-z), 0.0)
            o_ref[c * rows:(c + 1) * rows, :] = df
            dfb_ref[...] += jnp.sum(df, axis=0, keepdims=True)

    return pl.pallas_call(
        body, name=name, grid=(nb,),
        in_specs=[pl.BlockSpec((None, seq, LANE), lambda b: (b, 0, col_block)),
                  pl.BlockSpec((1, LANE), lambda b: (0, 0)),
                  pl.BlockSpec((None, seq, LANE), lambda b: (b, 0, 0))],
        out_specs=[pl.BlockSpec((None, seq, LANE), lambda b: (b, 0, 0)), pl.BlockSpec((1, LANE), lambda b: (0, 0))],
        out_shape=[jax.ShapeDtypeStruct((nb, seq, LANE), F32), jax.ShapeDtypeStruct((1, LANE), F32)],
        compiler_params=_params(),
    )(kvf3, fb, dF3)


def _head_masks():
    lane = lax.broadcasted_iota(jnp.int32, (1, LANE), 1)
    return (lane < HEAD_DIM, lane >= HEAD_DIM)


def _fox_logits(qh, k, fq, fk, h, causal):
    s = _bdot(qh, k, "nt") * ATT_SCALE + fq[:, h * HEAD_DIM:h * HEAD_DIM + 1] - fk[h:h + 1, :]
    return jnp.where(causal, s, NEG_BIG)


def _causal(qi, ki, tq, tk):
    r = qi * tq + lax.broadcasted_iota(jnp.int32, (tq, tk), 0)
    c = ki * tk + lax.broadcasted_iota(jnp.int32, (tq, tk), 1)
    return r >= c


def _fox_specs(tokw, t, order):
    hp0 = tokw // LANE

    def q_spec(off=0):
        if order == "qk":
            return pl.BlockSpec((None, t, LANE), lambda b, p, qi, ki: (b, qi, p + off))
        return pl.BlockSpec((None, t, LANE), lambda b, p, ki, qi: (b, jnp.maximum(qi, ki), p + off))

    def k_spec(off=0):
        if order == "qk":
            return pl.BlockSpec((None, t, LANE), lambda b, p, qi, ki: (b, jnp.minimum(qi, ki), p + off))
        return pl.BlockSpec((None, t, LANE), lambda b, p, ki, qi: (b, ki, p + off))

    def fk_spec():
        if order == "qk":
            return pl.BlockSpec((None, None, SUBLANE, t), lambda b, p, qi, ki: (b, p, 0, jnp.minimum(qi, ki)))
        return pl.BlockSpec((None, None, SUBLANE, t), lambda b, p, ki, qi: (b, p, 0, ki))

    return q_spec, k_spec, fk_spec, hp0


def _fox_fwd(proj3, kvf3, fq3, fkt, tokw, *, name):
    nb, seq, _ = proj3.shape
    t = _pick(seq, 512, LANE)
    nblk = seq // t
    q_spec, k_spec, fk_spec, hp0 = _fox_specs(tokw, t, "qk")

    def body(q_ref, k_ref, v_ref, fq_ref, fk_ref, o_ref, lse_ref, m_s, l_s, acc_s):
        qi, ki = pl.program_id(2), pl.program_id(3)

        @pl.when(ki == 0)
        def _():
            m_s[...] = jnp.full(m_s.shape, NEG_BIG, F32)
            l_s[...] = jnp.zeros(l_s.shape, F32)
            acc_s[...] = jnp.zeros(acc_s.shape, F32)

        @pl.when(ki <= qi)
        def _():
            q, k, v = q_ref[...], k_ref[...].astype(BF16), v_ref[...].astype(BF16)
            fq, fk = fq_ref[...], fk_ref[...]
            causal = _causal(qi, ki, t, t)
            for h, msk in enumerate(_head_masks()):
                s = _fox_logits(jnp.where(msk, q, 0.0).astype(BF16), k, fq, fk, h, causal)
                m_new = jnp.maximum(m_s[h], jnp.max(s, axis=-1, keepdims=True))
                alpha = jnp.exp(m_s[h] - m_new)
                p = jnp.exp(s - m_new)
                l_s[h] = alpha * l_s[h] + jnp.sum(p, axis=-1, keepdims=True)
                acc_s[h] = alpha * acc_s[h] + _bdot(p, v, "nn")
                m_s[h] = m_new

        @pl.when(ki == nblk - 1)
        def _():
            first = _head_masks()[0]
            o_ref[...] = jnp.where(first, acc_s[0] / l_s[0], acc_s[1] / l_s[1])
            lse_ref[...] = jnp.where(first, m_s[0] + jnp.log(l_s[0]), m_s[1] + jnp.log(l_s[1]))

    out = jax.ShapeDtypeStruct((nb, seq, tokw), F32)
    return pl.pallas_call(
        body, name=name, grid=(nb, hp0, nblk, nblk),
        in_specs=[q_spec(), k_spec(), k_spec(hp0), q_spec(), fk_spec()],
        out_specs=[q_spec(), q_spec()], out_shape=[out, out],
        scratch_shapes=[pltpu.VMEM((2, t, 1), F32), pltpu.VMEM((2, t, 1), F32), pltpu.VMEM((2, t, LANE), F32)],
        compiler_params=_params(),
    )(proj3, kvf3, kvf3, fq3, fkt)


def _fox_bwd_common(q, k, v, fq, fk, o, do, lse, h, msk, causal):
    qh = jnp.where(msk, q, 0.0).astype(BF16)
    doh = jnp.where(msk, do, 0.0)
    s = _fox_logits(qh, k, fq, fk, h, causal)
    p = jnp.exp(s - lse[:, h * HEAD_DIM:h * HEAD_DIM + 1])
    dp = _bdot(doh, v, "nt")
    delta = jnp.sum(doh * o, axis=-1, keepdims=True)
    return qh, doh.astype(BF16), p, p * (dp - delta)


def _fox_bwd_dq(proj3, kvf3, fq3, fkt, o3, dmixin3, lse3, tokw, *, name):
    nb, seq, _ = proj3.shape
    t = _pick(seq, 512, LANE)
    nblk = seq // t
    q_spec, k_spec, fk_spec, hp0 = _fox_specs(tokw, t, "qk")

    def body(q_ref, k_ref, v_ref, fq_ref, fk_ref, o_ref, do_ref, lse_ref, dq_ref, dfq_ref, acc_s, row_s):
        qi, ki = pl.program_id(2), pl.program_id(3)

        @pl.when(ki == 0)
        def _():
            acc_s[...] = jnp.zeros(acc_s.shape, F32)
            row_s[...] = jnp.zeros(row_s.shape, F32)

        @pl.when(ki <= qi)
        def _():
            k, v = k_ref[...].astype(BF16), v_ref[...].astype(BF16)
            causal = _causal(qi, ki, t, t)
            for h, msk in enumerate(_head_masks()):
                _, _, _, ds = _fox_bwd_common(q_ref[...], k, v, fq_ref[...], fk_ref[...], o_ref[...], do_ref[...],
                                              lse_ref[...], h, msk, causal)
                acc_s[h] += _bdot(ds, k, "nn")
                row_s[h] += jnp.sum(ds, axis=-1, keepdims=True)

        @pl.when(ki == nblk - 1)
        def _():
            first = _head_masks()[0]
            dq_ref[...] = jnp.where(first, acc_s[0], acc_s[1]) * ATT_SCALE
            dfq_ref[...] = jnp.where(first, row_s[0], row_s[1])

    out = jax.ShapeDtypeStruct((nb, seq, tokw), F32)
    return pl.pallas_call(
        body, name=name, grid=(nb, hp0, nblk, nblk),
        in_specs=[q_spec(), k_spec(), k_spec(hp0), q_spec(), fk_spec(), q_spec(), q_spec(), q_spec()],
        out_specs=[q_spec(), q_spec()], out_shape=[out, out],
        scratch_shapes=[pltpu.VMEM((2, t, LANE), F32), pltpu.VMEM((2, t, 1), F32)], compiler_params=_params(),
    )(proj3, kvf3, kvf3, fq3, fkt, o3, dmixin3, lse3)


def _fox_bwd_dkv(proj3, kvf3, fq3, fkt, o3, dmixin3, lse3, tokw, *, name):
    nb, seq, _ = proj3.shape
    t = _pick(seq, 512, LANE)
    nblk = seq // t
    q_spec, k_spec, fk_spec, hp0 = _fox_specs(tokw, t, "kq")

    def body(q_ref, k_ref, v_ref, fq_ref, fk_ref, o_ref, do_ref, lse_ref, dk_ref, dv_ref, dfk_ref, dk_s, dv_s,
             dfk_s):
        ki, qi = pl.program_id(2), pl.program_id(3)

        @pl.when(qi == 0)
        def _():
            dk_s[...] = jnp.zeros(dk_s.shape, F32)
            dv_s[...] = jnp.zeros(dv_s.shape, F32)
            dfk_s[...] = jnp.zeros(dfk_s.shape, F32)

        @pl.when(qi >= ki)
        def _():
            k, v = k_ref[...].astype(BF16), v_ref[...].astype(BF16)
            causal = _causal(qi, ki, t, t)
            for h, msk in enumerate(_head_masks()):
                qh, doh, p, ds = _fox_bwd_common(q_ref[...], k, v, fq_ref[...], fk_ref[...], o_ref[...],
                                                 do_ref[...], lse_ref[...], h, msk, causal)
                dv_s[...] += _bdot(p, doh, "tn")
                dk_s[...] += _bdot(ds, qh, "tn")
                dfk_s[h:h + 1, :] -= jnp.sum(ds, axis=0, keepdims=True)

        @pl.when(qi == nblk - 1)
        def _():
            dk_ref[...] = dk_s[...] * ATT_SCALE
            dv_ref[...] = dv_s[...]
            dfk_ref[...] = dfk_s[...]

    out = jax.ShapeDtypeStruct((nb, seq, tokw), F32)
    return pl.pallas_call(
        body, name=name, grid=(nb, hp0, nblk, nblk),
        in_specs=[q_spec(), k_spec(), k_spec(hp0), q_spec(), fk_spec(), q_spec(), q_spec(), q_spec()],
        out_specs=[k_spec(), k_spec(), fk_spec()],
        out_shape=[out, out, jax.ShapeDtypeStruct((nb, hp0, SUBLANE, seq), F32)],
        scratch_shapes=[pltpu.VMEM((t, LANE), F32), pltpu.VMEM((t, LANE), F32), pltpu.VMEM((SUBLANE, t), F32)],
        compiler_params=_params(),
    )(proj3, kvf3, kvf3, fq3, fkt, o3, dmixin3, lse3)


def _peer(k):
    x, y, c = lax.axis_index("x"), lax.axis_index("y"), lax.axis_index("c")
    return (1 - x if k & 4 else x, 1 - y if k & 2 else y, 1 - c if k & 1 else c)


def _dev_index(p):
    return 4 * p[0] + 2 * p[1] + p[2]


def _exchange(send, *, gather, name):
    block = send.shape[-2:]

    def body(s_ref, o_ref, send_sems, recv_sems, local_sem):
        me = _peer(0)
        mine = pltpu.make_async_copy(s_ref if gather else s_ref.at[_dev_index(me)], o_ref.at[_dev_index(me)],
                                     local_sem)
        mine.start()
        sends, recvs = [], []
        for k in range(1, N_DEV):
            peer = _peer(k)
            src = s_ref if gather else s_ref.at[_dev_index(peer)]
            sends.append(pltpu.make_async_remote_copy(
                src_ref=src, dst_ref=o_ref.at[_dev_index(me)], send_sem=send_sems.at[k - 1],
                recv_sem=recv_sems.at[k - 1], device_id=peer, device_id_type=MESH_T))
            recvs.append(pltpu.make_async_remote_copy(
                src_ref=src, dst_ref=o_ref.at[_dev_index(peer)], send_sem=send_sems.at[k - 1],
                recv_sem=recv_sems.at[k - 1], device_id=peer, device_id_type=MESH_T))
        for cp in sends:
            cp.start()
        for cp in recvs:
            cp.wait_recv()
        for cp in sends:
            cp.wait_send()
        mine.wait()

    return pl.pallas_call(
        body, name=name,
        in_specs=[pl.BlockSpec(memory_space=pltpu.HBM)], out_specs=pl.BlockSpec(memory_space=pltpu.HBM),
        out_shape=jax.ShapeDtypeStruct((N_DEV,) + block, send.dtype),
        scratch_shapes=[pltpu.SemaphoreType.DMA((N_DEV - 1,)), pltpu.SemaphoreType.DMA((N_DEV - 1,)),
                        pltpu.SemaphoreType.DMA],
    )(send)


_HBM = pl.BlockSpec(memory_space=pltpu.HBM)
CHIP_RELATIONS = (2, 4, 6)


def _chip_index(p):
    return 2 * p[0] + p[1]


def _run_copies(sends, recvs):
    for cp in sends:
        cp.start()
    for cp in recvs:
        cp.wait_recv()
    for cp in sends:
        cp.wait_send()


def _gather_shards(shards, *, name):
    n = len(shards)

    def body(*refs):
        ins, outs, send_sems, recv_sems = refs[:n], refs[n:2 * n], refs[2 * n], refs[2 * n + 1]
        me = _peer(0)
        sends, recvs = [], []
        for j, k in enumerate(CHIP_RELATIONS):
            peer = _peer(k)
            for i in range(n):
                sem = dict(send_sem=send_sems.at[3 * i + j], recv_sem=recv_sems.at[3 * i + j], device_id=peer,
                           device_id_type=MESH_T)
                sends.append(pltpu.make_async_remote_copy(src_ref=ins[i], dst_ref=outs[i].at[_chip_index(me)], **sem))
                recvs.append(pltpu.make_async_remote_copy(src_ref=ins[i], dst_ref=outs[i].at[_chip_index(peer)], **sem))
        _run_copies(sends, recvs)

    return pl.pallas_call(
        body, name=name, in_specs=[_HBM] * n, out_specs=[_HBM] * n,
        out_shape=[jax.ShapeDtypeStruct((N_CHIP,) + a.shape, a.dtype) for a in shards],
        scratch_shapes=[pltpu.SemaphoreType.DMA((3 * n,)), pltpu.SemaphoreType.DMA((3 * n,))],
    )(*shards)


def _scatter_halves(grads, *, name):
    n = len(grads)

    def body(*refs):
        ins, outs, send_sems, recv_sems = refs[:n], refs[n:2 * n], refs[2 * n], refs[2 * n + 1]
        sends, recvs = [], []
        for k in range(1, N_DEV):
            peer = _peer(k)
            for i in range(n):
                sem = dict(send_sem=send_sems.at[7 * i + k - 1], recv_sem=recv_sems.at[7 * i + k - 1],
                           device_id=peer, device_id_type=MESH_T)
                src = ins[i].at[_chip_index(peer), peer[2]]
                sends.append(pltpu.make_async_remote_copy(src_ref=src, dst_ref=outs[i].at[k - 1], **sem))
                recvs.append(pltpu.make_async_remote_copy(src_ref=src, dst_ref=outs[i].at[k - 1], **sem))
        _run_copies(sends, recvs)

    return pl.pallas_call(
        body, name=name, in_specs=[_HBM] * n, out_specs=[_HBM] * n,
        out_shape=[jax.ShapeDtypeStruct((N_DEV - 1,) + g.shape[2:], g.dtype) for g in grads],
        scratch_shapes=[pltpu.SemaphoreType.DMA((7 * n,)), pltpu.SemaphoreType.DMA((7 * n,))],
    )(*grads)


def _swap_halves(arrays, *, name):
    n = len(arrays)

    def body(*refs):
        outs, send_sems, recv_sems = refs[n:2 * n], refs[2 * n], refs[2 * n + 1]
        c = lax.axis_index("c")
        sib = _peer(1)
        sends, recvs = [], []
        for i in range(n):
            sem = dict(send_sem=send_sems.at[i], recv_sem=recv_sems.at[i], device_id=sib, device_id_type=MESH_T)
            sends.append(pltpu.make_async_remote_copy(src_ref=outs[i].at[c], dst_ref=outs[i].at[c], **sem))
            recvs.append(pltpu.make_async_remote_copy(src_ref=outs[i].at[c], dst_ref=outs[i].at[1 - c], **sem))
        _run_copies(sends, recvs)

    return pl.pallas_call(
        body, name=name, in_specs=[_HBM] * n, out_specs=[_HBM] * n,
        out_shape=[jax.ShapeDtypeStruct(a.shape, a.dtype) for a in arrays],
        input_output_aliases={i: i for i in range(n)},
        scratch_shapes=[pltpu.SemaphoreType.DMA((n,)), pltpu.SemaphoreType.DMA((n,))],
    )(*arrays)


def _adam_math(g, w, m, v):
    bc1 = 1.0 - ADAM_B1 ** ADAM_STEP
    bc2 = 1.0 - ADAM_B2 ** ADAM_STEP
    m_new = ADAM_B1 * m + (1.0 - ADAM_B1) * g
    v_new = ADAM_B2 * v + (1.0 - ADAM_B2) * (g * g)
    delta = -ADAM_LR * ((m_new / bc1) / (jnp.sqrt(v_new / bc2) + ADAM_EPS) + ADAM_WD * w)
    return delta, m_new, v_new


def _adamw_half(grads, parts, w, m, v, qc, *, name):
    _, _, rows, cols = grads.shape
    tile = _pick(rows, 64, SUBLANE)

    def body(qc_ref, g_ref, p_ref, w_ref, m_ref, v_ref, go_ref, do_ref, mo_ref, vo_ref):
        del qc_ref
        g = g_ref[...]
        for k in range(N_DEV - 1):
            g = g + p_ref[k]
        delta, m_new, v_new = _adam_math(g, w_ref[...], m_ref[...], v_ref[...])
        go_ref[...] = g
        do_ref[...] = delta
        mo_ref[...] = m_new
        vo_ref[...] = v_new

    half = pl.BlockSpec((None, tile, cols), lambda i, qc: (qc[1], i, 0))
    shape = jax.ShapeDtypeStruct((2, rows, cols), F32)
    return pl.pallas_call(
        body, name=name,
        grid_spec=pltpu.PrefetchScalarGridSpec(
            num_scalar_prefetch=1, grid=(rows // tile,),
            in_specs=[pl.BlockSpec((None, None, tile, cols), lambda i, qc: (qc[0], qc[1], i, 0)),
                      pl.BlockSpec((N_DEV - 1, tile, cols), lambda i, qc: (0, i, 0)), half, half, half],
            out_specs=[half] * 4),
        out_shape=[shape] * 4, compiler_params=_params(),
    )(qc, grads, parts, w, m, v)


def _adamw(parts, w, m, v, *, name):
    _, rows, cols = parts.shape
    tile = _pick(rows, 256, SUBLANE)

    def body(p_ref, w_ref, m_ref, v_ref, o_ref):
        g = p_ref[0]
        for i in range(1, N_DEV):
            g = g + p_ref[i]
        delta, m_new, v_new = _adam_math(g, w_ref[...], m_ref[...], v_ref[...])
        o_ref[0] = g
        o_ref[1] = delta
        o_ref[2] = m_new
        o_ref[3] = v_new

    spec = pl.BlockSpec((tile, cols), lambda i: (i, 0))
    return pl.pallas_call(
        body, name=name, grid=(rows // tile,),
        in_specs=[pl.BlockSpec((N_DEV, tile, cols), lambda i: (0, i, 0)), spec, spec, spec],
        out_specs=pl.BlockSpec((4, tile, cols), lambda i: (0, i, 0)),
        out_shape=jax.ShapeDtypeStruct((4, rows, cols), F32), compiler_params=_params(),
    )(parts, w, m, v)


def _layout(shapes, names, align):
    out, off = [], 0
    for n in names:
        size = math.prod(shapes[n])
        out.append((n, tuple(shapes[n]), off, size))
        off += _round_up(size, align)
    return out, off


def _pack(arrays, layout, total, lead=()):
    parts = []
    for i, (n, _, off, size) in enumerate(layout):
        end = layout[i + 1][2] if i + 1 < len(layout) else total
        flat = arrays[n].reshape(lead + (size,))
        if end - off > size:
            flat = jnp.pad(flat, [(0, 0)] * len(lead) + [(0, end - off - size)])
        parts.append(flat)
    return jnp.concatenate(parts, axis=len(lead))


def _unpack(flat, layout, lead=()):
    return {n: flat[..., off:off + size].reshape(lead + shape) for n, shape, off, size in layout}


def _to_shards(full, axis):
    shp = full.shape
    return jnp.moveaxis(full.reshape(shp[:axis] + (N_CHIP, shp[axis] // N_CHIP) + shp[axis + 1:]), axis, 0)


def _from_shards(shards, axis):
    x = jnp.moveaxis(shards, 0, axis)
    shp = x.shape
    return x.reshape(shp[:axis] + (shp[axis] * shp[axis + 1],) + shp[axis + 2:])


def _pad_cols(w, per, padded):
    lead = w.shape[:-1]
    x = w.reshape(lead + (-1, per))
    x = jnp.pad(x, [(0, 0)] * len(lead) + [(0, 0), (0, padded - per)])
    return x.reshape(lead + (-1,))


def _unpad_cols(w, per, padded):
    lead = w.shape[:-1]
    return w.reshape(lead + (-1, padded))[..., :per].reshape(lead + (-1,))


def _local_step(x, mem, target, W):
    nb, seq, d = x.shape
    n = nb * seq
    tokw = d - MEM_WIDTH
    heads = tokw // HEAD_DIM
    mlen = mem.shape[1]
    dff2 = W["ffn_w_up"].shape[-1]
    per = dff2 // N_CHIP
    per_p = _round_up(per, LANE)
    fp = 2 * per_p
    kvw = 2 * tokw + heads
    kvp = 2 * tokw + LANE
    gate_block = 2 * tokw // LANE

    x2d = x.reshape(n, d)
    mem2d = mem.reshape(nb * mlen, d)
    t2d = target.reshape(n, d)
    row = lambda a: a.reshape(1, -1)
    ones_tok = jnp.ones((1, tokw), F32)

    pool_bd = jax.scipy.linalg.block_diag(*[W["a_pool_w"][0, i] for i in range(len(POOL_WINDOWS))]).astype(BF16)
    kv_w = jnp.pad(W["kv_w"], ((0, 0), (0, kvp - kvw)))
    fb = jnp.pad(W["f_b"], (0, LANE - heads)).reshape(1, LANE)
    w_up = [_pad_cols(W["ffn_w_up"][l], per, per_p) for l in range(DEPTH)]
    w_down = [jnp.pad(W["ffn_w_down"][l].reshape(2, per, d), ((0, 0), (0, per_p - per), (0, 0))).reshape(fp, d)
              for l in range(DEPTH)]
    conv_w = [jnp.pad(_pad_cols(W["ffn_conv_w"][l], per, per_p), ((0, SUBLANE - CONV_WIDTH), (0, 0)))
              for l in range(DEPTH)]
    conv_b = [_pad_cols(W["ffn_conv_b"][l], per, per_p).reshape(1, 2 * fp) for l in range(DEPTH)]
    w_in = [W["a_w_in"][0], W["b_w_q"][0]]
    w_out = [W["a_w_out"][0], W["b_w_out"][0]]

    saved = []
    cur = x2d
    for l in range(DEPTH):
        s = {"x_in": cur}
        memkv = _mm(mem2d, W["mem_w_kv"][l], "nn", name=f"memkv{l}").reshape(nb, mlen, 2 * MEM_WIDTH)
        if l == 0:
            proj = _mm(cur, w_in[l], "nn", name="proj0")
            pooled = _pool_fwd(proj.reshape(nb, seq, d), tokw, name="pool_fwd").reshape(n, tokw)
            tok = _mm(pooled, pool_bd, "nn", name="pool_mix")
            scale = W["a_pool_scale"].reshape(1, tokw)
            s.update(pooled=pooled, mixed=tok, scale=scale)
        else:
            kvf = _mm(cur, kv_w, "nn", tn=kvp, name="kvf")
            kvf3 = kvf.reshape(nb, seq, kvp)
            gsum = _gate_fwd(kvf3, fb, gate_block, name="gate_fwd")[:, :, :heads]
            fq3 = jnp.repeat(gsum, HEAD_DIM, axis=2)
            fkt = jnp.pad(jnp.swapaxes(gsum, 1, 2).reshape(nb, heads // 2, 2, seq),
                          ((0, 0), (0, 0), (0, SUBLANE - 2), (0, 0)))
            proj = _mm(cur, w_in[l], "nn", name="proj1")
            o3, lse3 = _fox_fwd(proj.reshape(nb, seq, d), kvf3, fq3, fkt, tokw, name="fox_fwd")
            tok = o3.reshape(n, tokw)
            scale = ones_tok
            s.update(kvf3=kvf3, fq3=fq3, fkt=fkt, o3=o3, lse3=lse3)
        mixin = _memattn_fwd(tok, proj, memkv, scale, seq=seq, name=f"memattn_fwd{l}")
        mix = _mm(mixin, w_out[l], "nn", name=f"mix{l}")
        x1 = _ln_fwd(cur, mix, row(W["ln1_g"][l]), row(W["ln1_b"][l]), name=f"ln1_fwd{l}")
        up = _mm(x1, w_up[l], "nn", tn=per_p, name=f"ffn_up{l}")
        act = _convgate_fwd(up.reshape(nb, seq, 2 * fp), conv_w[l], conv_b[l], name=f"convgate_fwd{l}")
        act = act.reshape(n, fp)
        ffn = _mm(act, w_down[l], "nn", tk=fp, name=f"ffn_down{l}")
        s.update(proj=proj, memkv=memkv, mixin=mixin, mix=mix, x1=x1, up=up, act=act, ffn=ffn)
        saved.append(s)
        if l + 1 < DEPTH:
            cur = _ln_fwd(x1, ffn, row(W["ln2_g"][l]), row(W["ln2_b"][l]), name=f"ln2_fwd{l}")

    G = {}
    ln_g = {k: [None] * DEPTH for k in ("ln1_g", "ln1_b", "ln2_g", "ln2_b")}
    stack = {k: [None] * DEPTH for k in ("mem_w_kv", "ffn_w_up", "ffn_conv_w", "ffn_conv_b", "ffn_w_down")}
    dx_terms = None
    loss = None
    for l in reversed(range(DEPTH)):
        s = saved[l]
        g2 = row(W["ln2_g"][l])
        if l == DEPTH - 1:
            dres, dffn, loss, dg, db = _final_ln_loss(s["x1"], s["ffn"], t2d, g2, row(W["ln2_b"][l]),
                                                      name="final_ln_loss")
        else:
            dres, dffn, dg, db = _ln_bwd(s["x1"], s["ffn"], g2, dx_terms, name=f"ln2_bwd{l}")
        ln_g["ln2_g"][l], ln_g["ln2_b"][l] = dg[0], db[0]
        dact = _mm(dffn, w_down[l], "nt", tn=fp, name=f"ffn_down_dx{l}")
        stack["ffn_w_down"][l] = _mm(s["act"], dffn, "tn", tm=per_p, tk=512, name=f"ffn_down_dw{l}")
        du3, dg3, dcw, dcb = _convgate_bwd(s["up"].reshape(nb, seq, 2 * fp), dact.reshape(nb, seq, fp), conv_w[l],
                                           conv_b[l], name=f"convgate_bwd{l}")
        du, dgt = du3.reshape(n, fp), dg3.reshape(n, fp)
        dx1_u = _mm(du, w_up[l], "nt", tk=per_p, name=f"ffn_up_dx_u{l}")
        dx1_ffn = _mm(dgt, w_up[l], "nt", tk=per_p, b_col0=fp, adds=[dx1_u], name=f"ffn_up_dx_g{l}")
        stack["ffn_w_up"][l] = [_mm(s["x1"], part, "tn", tm=d, tn=per_p, tk=512, name=f"ffn_up_dw_{nm}{l}")
                                for nm, part in (("u", du), ("g", dgt))]
        stack["ffn_conv_w"][l] = dcw[:CONV_WIDTH]
        stack["ffn_conv_b"][l] = dcb[0]
        dres1, dmix, dg, db = _ln_bwd(s["x_in"], s["mix"], row(W["ln1_g"][l]), [dres, dx1_ffn], name=f"ln1_bwd{l}")
        ln_g["ln1_g"][l], ln_g["ln1_b"][l] = dg[0], db[0]
        dmixin = _mm(dmix, w_out[l], "nt", name=f"mix_dx{l}")
        d_w_out = _mm(s["mixin"], dmix, "tn", tm=d, tk=512, name=f"mix_dw{l}")
        if l == 0:
            G["a_w_out"] = d_w_out[None]
            dmixed, dscale = _scale_bwd(dmixin, s["mixed"], s["scale"], name="scale_bwd")
            G["a_pool_scale"] = dscale
            dpooled = _mm(dmixed, pool_bd, "nt", name="pool_mix_dx")
            dpw = _mm(s["pooled"], dmixed, "tn", tm=tokw, tk=512, name="pool_mix_dw")
            grp = tokw // len(POOL_WINDOWS)
            G["a_pool_w"] = jnp.stack([dpw[i * grp:(i + 1) * grp, i * grp:(i + 1) * grp]
                                       for i in range(len(POOL_WINDOWS))])[None]
            dtok = _pool_bwd(dpooled.reshape(nb, seq, tokw), name="pool_bwd").reshape(n, tokw)
            extra = []
        else:
            G["b_w_out"] = d_w_out[None]
            p3 = s["proj"].reshape(nb, seq, d)
            dm3 = dmixin.reshape(nb, seq, d)
            args = (p3, s["kvf3"], s["fq3"], s["fkt"], s["o3"], dm3, s["lse3"], tokw)
            dq3, dfq3 = _fox_bwd_dq(*args, name="fox_bwd_dq")
            dtok = dq3.reshape(n, tokw)
            dk3, dv3, dfk = _fox_bwd_dkv(*args, name="fox_bwd_dkv")
            dgsum = jnp.swapaxes(dfk[:, :, :2, :].reshape(nb, heads, seq), 1, 2) + dfq3[:, :, ::HEAD_DIM]
            dgsum = jnp.pad(dgsum, ((0, 0), (0, 0), (0, LANE - heads)))
            df3, dfb = _gate_bwd(s["kvf3"], fb, dgsum, gate_block, heads, name="gate_bwd")
            G["f_b"] = dfb[0, :heads]
            dkvf = [(dk3.reshape(n, tokw), 0, "k"), (dv3.reshape(n, tokw), tokw, "v"),
                    (df3.reshape(n, LANE), 2 * tokw, "f")]
            dx_kv = []
            for part, col0, nm in dkvf:
                dx_kv = [_mm(part, kv_w, "nt", b_col0=col0, adds=dx_kv, name=f"kvf_dx_{nm}")]
            extra = dx_kv
            G["kv_w"] = jnp.concatenate([_mm(s["x_in"], part, "tn", tm=d, tk=512, name=f"kvf_dw_{nm}")
                                         for part, _, nm in dkvf], axis=1)[:, :kvw]
        dproj, dmemkv = _memattn_bwd(dmixin, dtok, s["proj"], s["memkv"], seq=seq, name=f"memattn_bwd{l}")
        stack["mem_w_kv"][l] = _mm(mem2d, dmemkv.reshape(nb * mlen, 2 * MEM_WIDTH), "tn", tm=d, tk=512,
                                   name=f"memkv_dw{l}")
        G["a_w_in" if l == 0 else "b_w_q"] = _mm(s["x_in"], dproj, "tn", tm=d, tk=512, name=f"proj_dw{l}")[None]
        if l == 0:
            grad_x = _mm(dproj, w_in[l], "nt", adds=[dres1], name="proj_dx0")
        else:
            dx_terms = [dres1, _mm(dproj, w_in[l], "nt", name="proj_dx1")] + extra
    for k, v in ln_g.items():
        G[k] = jnp.stack(v)
    G["mem_w_kv"] = jnp.stack(stack["mem_w_kv"])
    G["ffn_w_up"] = jnp.stack([jnp.concatenate([_unpad_cols(g, per, per_p) for g in halves], axis=1)
                               for halves in stack["ffn_w_up"]])
    G["ffn_conv_w"] = jnp.stack([_unpad_cols(g, per, per_p) for g in stack["ffn_conv_w"]])
    G["ffn_conv_b"] = jnp.stack([_unpad_cols(g, per, per_p) for g in stack["ffn_conv_b"]])
    G["ffn_w_down"] = jnp.stack([g.reshape(2, per_p, d)[:, :per].reshape(2 * per, d) for g in stack["ffn_w_down"]])
    return loss[0, 0], grad_x.reshape(nb, seq, d), G


def kernel(x, mem, a_w_in, a_pool_w, a_pool_scale, a_w_out, b_w_q, b_w_out, kv_w, f_b, mem_w_kv, ln1_g, ln1_b, ln2_g, ln2_b, ffn_w_up, ffn_conv_w, ffn_conv_b, ffn_w_down, loss_target, m_a_w_in, m_a_pool_w, m_a_pool_scale, m_a_w_out, m_b_w_q, m_b_w_out, m_kv_w, m_f_b, m_mem_w_kv, m_ln1_g, m_ln1_b, m_ln2_g, m_ln2_b, m_ffn_w_up, m_ffn_conv_w, m_ffn_conv_b, m_ffn_w_down, v_a_w_in, v_a_pool_w, v_a_pool_scale, v_a_w_out, v_b_w_q, v_b_w_out, v_kv_w, v_f_b, v_mem_w_kv, v_ln1_g, v_ln1_b, v_ln2_g, v_ln2_b, v_ffn_w_up, v_ffn_conv_w, v_ffn_conv_b, v_ffn_w_down):
    w_loc = dict(a_w_in=a_w_in, a_pool_w=a_pool_w, a_pool_scale=a_pool_scale, a_w_out=a_w_out, b_w_q=b_w_q,
                 b_w_out=b_w_out, kv_w=kv_w, f_b=f_b, mem_w_kv=mem_w_kv, ln1_g=ln1_g, ln1_b=ln1_b, ln2_g=ln2_g,
                 ln2_b=ln2_b, ffn_w_up=ffn_w_up, ffn_conv_w=ffn_conv_w, ffn_conv_b=ffn_conv_b, ffn_w_down=ffn_w_down)
    m_loc = dict(a_w_in=m_a_w_in, a_pool_w=m_a_pool_w, a_pool_scale=m_a_pool_scale, a_w_out=m_a_w_out,
                 b_w_q=m_b_w_q, b_w_out=m_b_w_out, kv_w=m_kv_w, f_b=m_f_b, mem_w_kv=m_mem_w_kv, ln1_g=m_ln1_g,
                 ln1_b=m_ln1_b, ln2_g=m_ln2_g, ln2_b=m_ln2_b, ffn_w_up=m_ffn_w_up, ffn_conv_w=m_ffn_conv_w,
                 ffn_conv_b=m_ffn_conv_b, ffn_w_down=m_ffn_w_down)
    v_loc = dict(a_w_in=v_a_w_in, a_pool_w=v_a_pool_w, a_pool_scale=v_a_pool_scale, a_w_out=v_a_w_out,
                 b_w_q=v_b_w_q, b_w_out=v_b_w_out, kv_w=v_kv_w, f_b=v_f_b, mem_w_kv=v_mem_w_kv, ln1_g=v_ln1_g,
                 ln1_b=v_ln1_b, ln2_g=v_ln2_g, ln2_b=v_ln2_b, ffn_w_up=v_ffn_w_up, ffn_conv_w=v_ffn_conv_w,
                 ffn_conv_b=v_ffn_conv_b, ffn_w_down=v_ffn_w_down)
    x_i, y_i, c = lax.axis_index("x"), lax.axis_index("y"), lax.axis_index("c")
    q = 2 * x_i + y_i
    qc = jnp.stack([q, c]).astype(jnp.int32)
    shapes = {k: v.shape for k, v in w_loc.items()}

    own = {k: (w_loc[k] if k in GATHER_F32 else w_loc[k].astype(BF16)) for k in SHARDED}
    gathered = _gather_shards([own[k] for k in SHARDED], name="gather_weights")
    W = {k: _from_shards(lax.dynamic_update_slice_in_dim(g, own[k][None], q, axis=0), SHARD_AXIS[k])
         for k, g in zip(SHARDED, gathered)}
    for k in REPLICATED:
        W[k] = w_loc[k]

    loss_part, grad_x, G = _local_step(x, mem, loss_target, W)
    loss = lax.psum(loss_part, ("x", "y", "c"))

    def halves(a):
        if a.ndim == 3 and a.shape[0] == 2:
            return a
        rows = math.prod(a.shape[:-1])
        return a.reshape(2, rows // 2, a.shape[-1])

    g_chip = [_to_shards(G[k], SHARD_AXIS[k]) for k in BIG]
    g_chip = [g.reshape((N_CHIP,) + halves(w_loc[k]).shape) for k, g in zip(BIG, g_chip)]
    parts = _scatter_halves(g_chip, name="scatter_grads")
    res = []
    for k, g, p in zip(BIG, g_chip, parts):
        res.extend(_adamw_half(g, p, halves(w_loc[k]), halves(m_loc[k]), halves(v_loc[k]), qc, name=f"adamw_{k}"))
    res = _swap_halves(res, name="swap_halves")
    out = {k: [r.reshape(shapes[k]) for r in res[4 * i:4 * i + 4]] for i, k in enumerate(BIG)}

    lay_r, tot_r = _layout(shapes, REPLICATED, PACK_COLS)
    rep_rows = tot_r // PACK_COLS
    rows = _round_up(rep_rows + 1, SUBLANE)
    scale_w = shapes["a_pool_scale"][-1]

    def small(rep, scale_row):
        lead = scale_row.shape[:-2]
        rep = jnp.broadcast_to(_pack(rep, lay_r, tot_r).reshape(rep_rows, PACK_COLS), lead + (rep_rows, PACK_COLS))
        pad = [(0, 0)] * len(lead)
        return jnp.concatenate([rep, jnp.pad(scale_row, pad + [(0, rows - rep_rows - 1), (0, PACK_COLS - scale_w)])],
                               axis=-2)

    g_scale = jnp.repeat(_to_shards(G["a_pool_scale"], 1), 2, axis=0)
    sm_parts = _exchange(small(G, g_scale), gather=False, name="scatter_small")
    sm = _adamw(sm_parts, *[small(d, d["a_pool_scale"]) for d in (w_loc, m_loc, v_loc)], name="adamw_small")
    out_r = _unpack(sm[:, :rep_rows].reshape(4, tot_r), lay_r, lead=(4,))
    for k in REPLICATED:
        out[k] = [out_r[k][a] for a in range(4)]
    out["a_pool_scale"] = [sm[a, rep_rows:rep_rows + 1, :scale_w] for a in range(4)]

    outs = [loss, grad_x]
    for a in range(4):
        for k in WEIGHTS:
            outs.append(out[k][a])
    return tuple(outs)
```

```python
import functools
import math

import jax
import jax.numpy as jnp
from jax import lax
from jax.experimental import pallas as pl
from jax.experimental.pallas import tpu as pltpu

F32 = jnp.float32
BF16 = jnp.bfloat16

HEAD_DIM = 64
MEM_HEADS = 4
MEM_WIDTH = MEM_HEADS * HEAD_DIM
POOL_WINDOWS = (2, 4, 8, 16)
MAX_WINDOW = 16
CONV_WIDTH = 3
DEPTH = 2
DN_ALPHA = (2.0 * DEPTH) ** 0.25
LN_EPS = 1e-5
ATT_SCALE = HEAD_DIM ** -0.5
NEG_BIG = -1e30

ADAM_LR = 0.001
ADAM_B1 = 0.9
ADAM_B2 = 0.999
ADAM_EPS = 1e-08
ADAM_WD = 0.01
ADAM_STEP = 10

LANE = 128
SUBLANE = 8
PACK_COLS = 1024
VMEM_LIMIT = 56 * 1024 * 1024
N_DEV = 8
N_CHIP = 4
MESH_T = pl.DeviceIdType.MESH

SHARDED = ("a_w_in", "a_pool_scale", "a_w_out", "b_w_q", "b_w_out", "kv_w", "mem_w_kv", "ffn_w_up",
           "ffn_conv_w", "ffn_w_down")
SHARD_AXIS = {"a_w_in": 1, "a_pool_scale": 1, "a_w_out": 1, "b_w_q": 1, "b_w_out": 1, "kv_w": 1, "mem_w_kv": 1,
              "ffn_w_up": 2, "ffn_conv_w": 2, "ffn_w_down": 1}
GATHER_F32 = ("a_pool_scale", "ffn_conv_w")
BIG = tuple(k for k in SHARDED if k != "a_pool_scale")
REPLICATED = ("a_pool_w", "f_b", "ln1_g", "ln1_b", "ln2_g", "ln2_b", "ffn_conv_b")
WEIGHTS = ("a_w_in", "a_pool_w", "a_pool_scale", "a_w_out", "b_w_q", "b_w_out", "kv_w", "f_b", "mem_w_kv",
           "ln1_g", "ln1_b", "ln2_g", "ln2_b", "ffn_w_up", "ffn_conv_w", "ffn_conv_b", "ffn_w_down")


def _round_up(n, m):
    return -(-n // m) * m


def _pick(dim, pref, unit=LANE):
    if dim <= pref:
        return dim
    t = (pref // unit) * unit
    while t >= unit:
        if dim % t == 0:
            return t
        t -= unit
    raise ValueError(f"no tile for {dim} <= {pref}")


def _params():
    return pltpu.CompilerParams(vmem_limit_bytes=VMEM_LIMIT)


_DIMS = {"nn": ((1,), (0,)), "nt": ((1,), (1,)), "tn": ((0,), (0,))}


def _bdot(a, b, mode):
    return lax.dot_general(a.astype(BF16), b.astype(BF16), (_DIMS[mode], ((), ())), preferred_element_type=F32)


def _mm(a, b, mode, *, name, tm=512, tn=1024, tk=2048, adds=(), b_col0=0):
    if mode == "nn":
        (M, K), (K2, N) = a.shape, b.shape
    elif mode == "nt":
        (M, K), (N, K2) = a.shape, b.shape
        K2 = K if b_col0 + K <= K2 else -1
    else:
        (K, M), (K2, N) = a.shape, b.shape
    assert K == K2 and (mode == "nt" or b_col0 == 0), (name, a.shape, b.shape)
    tm, tn = _pick(M, tm, SUBLANE if mode != "tn" else LANE), _pick(N, tn)
    tk = _pick(K, tk, LANE if mode != "tn" else SUBLANE)
    nk = K // tk
    assert b_col0 % tk == 0, (name, b_col0, tk)
    koff = b_col0 // tk
    n_add = len(adds)

    def body(*refs):
        a_ref, b_ref = refs[0], refs[1]
        add_refs = refs[2:2 + n_add]
        o_ref, acc_ref = refs[2 + n_add], refs[3 + n_add]
        part = _bdot(a_ref[...], b_ref[...], mode)

        def finish(r):
            for ar in add_refs:
                r = r + ar[...]
            o_ref[...] = r

        if nk == 1:
            finish(part)
        else:
            k = pl.program_id(2)

            @pl.when(k == 0)
            def _():
                acc_ref[...] = part

            @pl.when(k > 0)
            def _():
                acc_ref[...] += part

            @pl.when(k == nk - 1)
            def _():
                finish(acc_ref[...])

    if mode == "nn":
        a_spec = pl.BlockSpec((tm, tk), lambda i, j, k: (i, k))
        b_spec = pl.BlockSpec((tk, tn), lambda i, j, k: (k, j))
    elif mode == "nt":
        a_spec = pl.BlockSpec((tm, tk), lambda i, j, k: (i, k))
        b_spec = pl.BlockSpec((tn, tk), lambda i, j, k: (j, k + koff))
    else:
        a_spec = pl.BlockSpec((tk, tm), lambda i, j, k: (k, i))
        b_spec = pl.BlockSpec((tk, tn), lambda i, j, k: (k, j))
    o_spec = pl.BlockSpec((tm, tn), lambda i, j, k: (i, j))
    acc_shape = (tm, tn) if nk > 1 else (SUBLANE, LANE)
    return pl.pallas_call(
        body, name=name, grid=(M // tm, N // tn, nk),
        in_specs=[a_spec, b_spec] + [o_spec] * n_add, out_specs=o_spec,
        out_shape=jax.ShapeDtypeStruct((M, N), F32),
        scratch_shapes=[pltpu.VMEM(acc_shape, F32)],
        compiler_params=_params(),
    )(a, b, *adds)


def _rowwise(fn, tiled, full, outs_tiled, outs_acc, *, rows, tile, name, acc_period=None):
    n_tiles = rows // tile
    period = n_tiles if acc_period is None else acc_period
    arrays, in_specs = [], []
    for t in tiled:
        arr, width, cb = t if isinstance(t, tuple) else (t, t.shape[1], 0)
        arrays.append(arr)
        in_specs.append(pl.BlockSpec((tile, width), lambda i, cb=cb: (i, cb)))
    for f in full:
        arr, spec = f if isinstance(f, tuple) else (f, None)
        arrays.append(arr)
        in_specs.append(spec if spec is not None else pl.BlockSpec(arr.shape, lambda i, nd=arr.ndim: (0,) * nd))
    out_shape, out_specs = [], []
    for width, dt in outs_tiled:
        out_shape.append(jax.ShapeDtypeStruct((rows, width), dt))
        out_specs.append(pl.BlockSpec((tile, width), lambda i: (i, 0)))
    for acc in outs_acc:
        shape, dt = acc[0], acc[1]
        out_shape.append(jax.ShapeDtypeStruct(shape, dt))
        out_specs.append(acc[2] if len(acc) > 2 else pl.BlockSpec(shape, lambda i, nd=len(shape): (0,) * nd))
    n_in, n_t, n_a = len(arrays), len(outs_tiled), len(outs_acc)

    def body(*refs):
        vals = [r[...] for r in refs[:n_in]]
        o_t, o_a = fn(*vals)
        for r, v in zip(refs[n_in:n_in + n_t], o_t):
            r[...] = v.astype(r.dtype)
        first = pl.program_id(0) % period == 0
        for r, v in zip(refs[n_in + n_t:n_in + n_t + n_a], o_a):
            v = v.reshape(r.shape)

            @pl.when(first)
            def _(r=r, v=v):
                r[...] = v

            @pl.when(jnp.logical_not(first))
            def _(r=r, v=v):
                r[...] += v

    return pl.pallas_call(
        body, name=name, grid=(n_tiles,), in_specs=in_specs, out_specs=out_specs, out_shape=out_shape,
        compiler_params=_params(),
    )(*arrays)


def _ln_stats(h):
    mu = jnp.mean(h, axis=-1, keepdims=True)
    d = h - mu
    var = jnp.mean(d * d, axis=-1, keepdims=True)
    rstd = lax.rsqrt(var + LN_EPS)
    return d * rstd, rstd


def _ln_bwd_math(h, g, dy):
    xhat, rstd = _ln_stats(h)
    dxhat = dy * g
    dh = rstd * (dxhat - jnp.mean(dxhat, axis=-1, keepdims=True)
                 - xhat * jnp.mean(dxhat * xhat, axis=-1, keepdims=True))
    return dh, jnp.sum(dy * xhat, axis=0, keepdims=True), jnp.sum(dy, axis=0, keepdims=True)


def _ln_fwd(x, r, g, b, *, name):
    n, d = x.shape

    def fn(x, r, g, b):
        xhat, _ = _ln_stats(DN_ALPHA * x + r)
        return (xhat * g + b,), ()

    return _rowwise(fn, [x, r], [g, b], [(d, F32)], [], rows=n, tile=_pick(n, 512, SUBLANE), name=name)[0]


def _ln_bwd(x, r, g, dys, *, name):
    n, d = x.shape
    n_dy = len(dys)

    def fn(x, r, *rest):
        dy = rest[0]
        for e in rest[1:n_dy]:
            dy = dy + e
        dh, dg, db = _ln_bwd_math(DN_ALPHA * x + r, rest[n_dy], dy)
        return (DN_ALPHA * dh, dh), (dg, db)

    return _rowwise(fn, [x, r, *dys], [g], [(d, F32), (d, F32)], [((1, d), F32), ((1, d), F32)],
                    rows=n, tile=_pick(n, 256, SUBLANE), name=name)


def _final_ln_loss(x, r, target, g, b, *, name):
    n, d = x.shape

    def fn(x, r, t, g, b):
        h = DN_ALPHA * x + r
        xhat, _ = _ln_stats(h)
        err = xhat * g + b - t
        loss = jnp.full((1, LANE), 0.5 * jnp.sum(err * err) / d, F32)
        dh, dg, db = _ln_bwd_math(h, g, err / d)
        return (DN_ALPHA * dh, dh), (loss, dg, db)

    return _rowwise(fn, [x, r, target], [g, b], [(d, F32), (d, F32)],
                    [((1, LANE), F32), ((1, d), F32), ((1, d), F32)],
                    rows=n, tile=_pick(n, 256, SUBLANE), name=name)


def _mem_heads(qm):
    lane = lax.broadcasted_iota(jnp.int32, (1, MEM_WIDTH), 1)
    for h in range(MEM_HEADS):
        msk = (lane >= h * HEAD_DIM) & (lane < (h + 1) * HEAD_DIM)
        yield msk, jnp.where(msk, qm, 0.0).astype(BF16)


def _mem_softmax(qh, k):
    s = _bdot(qh, k, "nt") * ATT_SCALE
    p = jnp.exp(s - jnp.max(s, axis=-1, keepdims=True))
    return p / jnp.sum(p, axis=-1, keepdims=True)


def _memattn_fwd(tok, proj, memkv, scale, *, seq, name):
    n, tokw = tok.shape
    d = tokw + MEM_WIDTH
    tile = _pick(seq, 512, SUBLANE)

    def fn(tok, qm, kv, scale):
        k, v = kv[:, :MEM_WIDTH].astype(BF16), kv[:, MEM_WIDTH:].astype(BF16)
        out = jnp.zeros(qm.shape, F32)
        for msk, qh in _mem_heads(qm):
            out = jnp.where(msk, _bdot(_mem_softmax(qh, k), v, "nn"), out)
        return (jnp.concatenate([tok * scale, out], axis=1),), ()

    kv_spec = pl.BlockSpec((None,) + memkv.shape[1:], lambda i: (i // (seq // tile), 0, 0))
    return _rowwise(fn, [tok, (proj, MEM_WIDTH, tokw // MEM_WIDTH)], [(memkv, kv_spec), scale], [(d, F32)], [],
                    rows=n, tile=tile, name=name)[0]


def _memattn_bwd(dmixin, dtok, proj, memkv, *, seq, name):
    n, tokw = dtok.shape
    d = tokw + MEM_WIDTH
    tile = _pick(seq, 512, SUBLANE)

    def fn(dmo, dtok, qm, kv):
        k, v = kv[:, :MEM_WIDTH].astype(BF16), kv[:, MEM_WIDTH:].astype(BF16)
        dq = jnp.zeros(qm.shape, F32)
        dk = jnp.zeros(k.shape, F32)
        dv = jnp.zeros(v.shape, F32)
        for msk, qh in _mem_heads(qm):
            p = _mem_softmax(qh, k)
            doh = jnp.where(msk, dmo, 0.0).astype(BF16)
            dv = dv + _bdot(p, doh, "tn")
            dp = _bdot(doh, v, "nt")
            ds = (p * (dp - jnp.sum(dp * p, axis=-1, keepdims=True))).astype(BF16)
            dq = jnp.where(msk, _bdot(ds, k, "nn") * ATT_SCALE, dq)
            dk = dk + _bdot(ds, qh, "tn") * ATT_SCALE
        return (jnp.concatenate([dtok, dq], axis=1),), (jnp.concatenate([dk, dv], axis=1),)

    tpe = seq // tile
    kv_spec = pl.BlockSpec((None,) + memkv.shape[1:], lambda i: (i // tpe, 0, 0))
    return _rowwise(fn, [(dmixin, MEM_WIDTH, tokw // MEM_WIDTH), dtok, (proj, MEM_WIDTH, tokw // MEM_WIDTH)],
                    [(memkv, kv_spec)], [(d, F32)], [(memkv.shape, F32, kv_spec)],
                    rows=n, tile=tile, name=name, acc_period=tpe)


def _scale_bwd(dmixin, mixed, scale, *, name):
    n, tokw = mixed.shape

    def fn(dt, mixed, scale):
        return (dt * scale,), (jnp.sum(dt * mixed, axis=0, keepdims=True),)

    return _rowwise(fn, [(dmixin, tokw, 0), mixed], [scale], [(tokw, F32)], [((1, tokw), F32)],
                    rows=n, tile=_pick(n, 512, SUBLANE), name=name)


def _chunk_rows(seq):
    return _pick(seq, 512, SUBLANE)


def _load_ext(ref, c, rows, before, after, seq):
    lo, hi = c * rows - before, (c + 1) * rows + after
    parts = []
    if lo < 0:
        parts.append(jnp.zeros((-lo, ref.shape[1]), F32))
    parts.append(ref[max(lo, 0):min(hi, seq), :])
    if hi > seq:
        parts.append(jnp.zeros((hi - seq, ref.shape[1]), F32))
    return parts[0] if len(parts) == 1 else jnp.concatenate(parts, axis=0)


def _down(x, k):
    return pltpu.roll(x, k, 0)


def _up(x, k):
    return pltpu.roll(x, x.shape[0] - k, 0)


def _window_sums(ext, shift, col0, group):
    lane = col0 + lax.broadcasted_iota(jnp.int32, (1, ext.shape[1]), 1)
    gidx = lane // group
    s = ext
    out = None
    k = 1
    for gi, w in enumerate(POOL_WINDOWS):
        while k < w:
            s = s + shift(s, k)
            k *= 2
        out = s if out is None else jnp.where(gidx >= gi, s, out)
    return out, jnp.left_shift(2, jnp.minimum(gidx, len(POOL_WINDOWS) - 1))


def _pool_fwd(proj3, tokw, *, name):
    nb, seq, _ = proj3.shape
    rows = _chunk_rows(seq)
    group = tokw // len(POOL_WINDOWS)

    def body(u_ref, o_ref):
        col0 = pl.program_id(1) * LANE
        for c in range(seq // rows):
            ext = _load_ext(u_ref, c, rows, MAX_WINDOW, 0, seq)
            sums, win = _window_sums(ext, _down, col0, group)
            t = c * rows + lax.broadcasted_iota(jnp.int32, (rows, 1), 0)
            count = jnp.minimum(t + 1, win).astype(F32)
            o_ref[c * rows:(c + 1) * rows, :] = sums[MAX_WINDOW:, :] / count - ext[MAX_WINDOW:, :]

    spec = pl.BlockSpec((None, seq, LANE), lambda b, j: (b, 0, j))
    return pl.pallas_call(
        body, name=name, grid=(nb, tokw // LANE), in_specs=[spec], out_specs=spec,
        out_shape=jax.ShapeDtypeStruct((nb, seq, tokw), F32), compiler_params=_params(),
    )(proj3)


def _pool_bwd(dp3, *, name):
    nb, seq, tokw = dp3.shape
    rows = _chunk_rows(seq)
    group = tokw // len(POOL_WINDOWS)

    def body(d_ref, o_ref):
        col0 = pl.program_id(1) * LANE
        for c in range(seq // rows):
            ext = _load_ext(d_ref, c, rows, 0, MAX_WINDOW, seq)
            lane = col0 + lax.broadcasted_iota(jnp.int32, (1, LANE), 1)
            win = jnp.left_shift(2, jnp.minimum(lane // group, len(POOL_WINDOWS) - 1))
            t = c * rows + lax.broadcasted_iota(jnp.int32, (rows + MAX_WINDOW, 1), 0)
            scaled = ext / jnp.minimum(t + 1, win).astype(F32)
            sums, _ = _window_sums(scaled, _up, col0, group)
            o_ref[c * rows:(c + 1) * rows, :] = sums[:rows, :] - ext[:rows, :]

    spec = pl.BlockSpec((None, seq, LANE), lambda b, j: (b, 0, j))
    return pl.pallas_call(
        body, name=name, grid=(nb, tokw // LANE), in_specs=[spec], out_specs=spec,
        out_shape=jax.ShapeDtypeStruct((nb, seq, tokw), F32), compiler_params=_params(),
    )(dp3)


def _conv3(ext, w_ref, b_ref):
    x1, x2 = _down(ext, 1), _down(ext, 2)
    return w_ref[0:1, :] * x2 + w_ref[1:2, :] * x1 + w_ref[2:3, :] * ext + b_ref[...], x1, x2


def _convgate_fwd(up3, cw, cb, *, name):
    nb, seq, c2 = up3.shape
    fp = c2 // 2
    nblk = fp // LANE
    rows = _chunk_rows(seq)

    def body(u_ref, g_ref, wu_ref, wg_ref, bu_ref, bg_ref, o_ref):
        for c in range(seq // rows):
            hu, _, _ = _conv3(_load_ext(u_ref, c, rows, SUBLANE, 0, seq), wu_ref, bu_ref)
            hg, _, _ = _conv3(_load_ext(g_ref, c, rows, SUBLANE, 0, seq), wg_ref, bg_ref)
            o_ref[c * rows:(c + 1) * rows, :] = (hg * jax.nn.sigmoid(hg) * hu)[SUBLANE:, :]

    def col(off, r):
        return pl.BlockSpec((r, LANE), lambda b, j: (0, j + off))

    def act(off):
        return pl.BlockSpec((None, seq, LANE), lambda b, j: (b, 0, j + off))

    return pl.pallas_call(
        body, name=name, grid=(nb, nblk),
        in_specs=[act(0), act(nblk), col(0, SUBLANE), col(nblk, SUBLANE), col(0, 1), col(nblk, 1)],
        out_specs=act(0), out_shape=jax.ShapeDtypeStruct((nb, seq, fp), F32), compiler_params=_params(),
    )(up3, up3, cw, cw, cb, cb)


def _convgate_bwd(up3, dact3, cw, cb, *, name):
    nb, seq, c2 = up3.shape
    fp = c2 // 2
    nblk = fp // LANE
    rows = _chunk_rows(seq)
    h = SUBLANE

    def body(u_ref, g_ref, da_ref, wu_ref, wg_ref, bu_ref, bg_ref, du_ref, dg_ref, dwu_ref, dwg_ref, dbu_ref,
             dbg_ref):
        @pl.when(pl.program_id(1) == 0)
        def _():
            for r in (dwu_ref, dwg_ref, dbu_ref, dbg_ref):
                r[...] = jnp.zeros(r.shape, F32)

        for c in range(seq // rows):
            eu = _load_ext(u_ref, c, rows, h, h, seq)
            eg = _load_ext(g_ref, c, rows, h, h, seq)
            da = _load_ext(da_ref, c, rows, h, h, seq)
            hu, u1, u2 = _conv3(eu, wu_ref, bu_ref)
            hg, g1, g2 = _conv3(eg, wg_ref, bg_ref)
            sig = jax.nn.sigmoid(hg)
            dhu = da * hg * sig
            dhg = da * hu * sig * (1.0 + hg * (1.0 - sig))
            for dh, w_ref, x0, x1, x2, dx_ref, dw_ref, db_ref in (
                    (dhu, wu_ref, eu, u1, u2, du_ref, dwu_ref, dbu_ref),
                    (dhg, wg_ref, eg, g1, g2, dg_ref, dwg_ref, dbg_ref)):
                dx = w_ref[2:3, :] * dh + w_ref[1:2, :] * _up(dh, 1) + w_ref[0:1, :] * _up(dh, 2)
                dx_ref[c * rows:(c + 1) * rows, :] = dx[h:h + rows, :]
                core = dh[h:h + rows, :]
                for k, xk in ((0, x2), (1, x1), (2, x0)):
                    dw_ref[k:k + 1, :] += jnp.sum(core * xk[h:h + rows, :], axis=0, keepdims=True)
                db_ref[...] += jnp.sum(core, axis=0, keepdims=True)

    def col(off, r):
        return pl.BlockSpec((r, LANE), lambda j, b: (0, j + off))

    def act(off):
        return pl.BlockSpec((None, seq, LANE), lambda j, b: (b, 0, j + off))

    du, dg, dwu, dwg, dbu, dbg = pl.pallas_call(
        body, name=name, grid=(nblk, nb),
        in_specs=[act(0), act(nblk), act(0), col(0, SUBLANE), col(nblk, SUBLANE), col(0, 1), col(nblk, 1)],
        out_specs=[act(0), act(0), col(0, SUBLANE), col(0, SUBLANE), col(0, 1), col(0, 1)],
        out_shape=[jax.ShapeDtypeStruct((nb, seq, fp), F32), jax.ShapeDtypeStruct((nb, seq, fp), F32),
                   jax.ShapeDtypeStruct((SUBLANE, fp), F32), jax.ShapeDtypeStruct((SUBLANE, fp), F32),
                   jax.ShapeDtypeStruct((1, fp), F32), jax.ShapeDtypeStruct((1, fp), F32)],
        compiler_params=_params(),
    )(up3, up3, dact3, cw, cw, cb, cb)
    return du, dg, jnp.concatenate([dwu, dwg], axis=1), jnp.concatenate([dbu, dbg], axis=1)


def _scan_rows(x, shift, valid):
    row = lax.broadcasted_iota(jnp.int32, (x.shape[0], 1), 0)
    k = 1
    while k < x.shape[0]:
        x = x + jnp.where(valid(row, k), shift(x, k), 0.0)
        k *= 2
    return x


def _pick_row(x, r):
    row = lax.broadcasted_iota(jnp.int32, (x.shape[0], 1), 0)
    return jnp.sum(jnp.where(row == r, x, 0.0), axis=0, keepdims=True)


def _log_sigmoid(z):
    return jnp.minimum(z, 0.0) - jnp.log(1.0 + jnp.exp(-jnp.abs(z)))


def _gate_fwd(kvf3, fb, col_block, *, name):
    nb, seq, _ = kvf3.shape
    rows = _chunk_rows(seq)

    def body(f_ref, fb_ref, o_ref):
        carry = jnp.zeros((1, LANE), F32)
        for c in range(seq // rows):
            logf = _log_sigmoid(f_ref[c * rows:(c + 1) * rows, :] + fb_ref[...])
            run = _scan_rows(logf, _down, lambda row, k: row >= k) + carry
            o_ref[c * rows:(c + 1) * rows, :] = run
            carry = _pick_row(run, rows - 1)

    return pl.pallas_call(
        body, name=name, grid=(nb,),
        in_specs=[pl.BlockSpec((None, seq, LANE), lambda b: (b, 0, col_block)),
                  pl.BlockSpec((1, LANE), lambda b: (0, 0))],
        out_specs=pl.BlockSpec((None, seq, LANE), lambda b: (b, 0, 0)),
        out_shape=jax.ShapeDtypeStruct((nb, seq, LANE), F32), compiler_params=_params(),
    )(kvf3, fb)


def _gate_bwd(kvf3, fb, dF3, col_block, heads, *, name):
    nb, seq, _ = kvf3.shape
    rows = _chunk_rows(seq)

    def body(f_ref, fb_ref, d_ref, o_ref, dfb_ref):
        @pl.when(pl.program_id(0) == 0)
        def _():
            dfb_ref[...] = jnp.zeros(dfb_ref.shape, F32)

        lane = lax.broadcasted_iota(jnp.int32, (1, LANE), 1)
        carry = jnp.zeros((1, LANE), F32)
        for c in reversed(range(seq // rows)):
            run = _scan_rows(d_ref[c * rows:(c + 1) * rows, :], _up, lambda row, k: row < rows - k) + carry
            carry = _pick_row(run, 0)
            z = f_ref[c * rows:(c + 1) * rows, :] + fb_ref[...]
            df = jnp.where(lane < heads, run * jax.nn.sigmoid(-z), 0.0)
            o_ref[c * rows:(c + 1) * rows, :] = df
            dfb_ref[...] += jnp.sum(df, axis=0, keepdims=True)

    return pl.pallas_call(
        body, name=name, grid=(nb,),
        in_specs=[pl.BlockSpec((None, seq, LANE), lambda b: (b, 0, col_block)),
                  pl.BlockSpec((1, LANE), lambda b: (0, 0)),
                  pl.BlockSpec((None, seq, LANE), lambda b: (b, 0, 0))],
        out_specs=[pl.BlockSpec((None, seq, LANE), lambda b: (b, 0, 0)), pl.BlockSpec((1, LANE), lambda b: (0, 0))],
        out_shape=[jax.ShapeDtypeStruct((nb, seq, LANE), F32), jax.ShapeDtypeStruct((1, LANE), F32)],
        compiler_params=_params(),
    )(kvf3, fb, dF3)


def _head_masks():
    lane = lax.broadcasted_iota(jnp.int32, (1, LANE), 1)
    return (lane < HEAD_DIM, lane >= HEAD_DIM)


BIAS_TERMS = 3


def _bias_lanes(gsum):
    nb, seq, heads = gsum.shape
    terms, rest = [], gsum
    for _ in range(BIAS_TERMS):
        t = lax.reduce_precision(rest, exponent_bits=8, mantissa_bits=7)
        terms.append(t)
        rest = rest - t
    ones = [jnp.ones_like(gsum)] * BIAS_TERMS

    def lanes(parts):
        z = jnp.stack(parts, axis=-1)
        z = jnp.pad(z, ((0, 0), (0, 0), (0, 0), (0, HEAD_DIM - 2 * BIAS_TERMS)))
        z = z.reshape(nb, seq, heads // 2, 2, HEAD_DIM)[:, :, :, ::-1]
        return z.reshape(nb, seq, heads * HEAD_DIM).astype(BF16)

    return lanes(terms + ones), lanes(ones + [-t for t in terms])


def _fox_scores(q, k, aq, ak, masked):
    qs = (q * ATT_SCALE).astype(BF16)
    qts = [jnp.where(msk, qs, aq) for msk in _head_masks()]
    ss = [_bdot(qt, jnp.where(msk, k, ak), "nt") for qt, msk in zip(qts, _head_masks())]
    if masked:
        t = q.shape[0]
        keep = lax.broadcasted_iota(jnp.int32, (t, t), 0) >= lax.broadcasted_iota(jnp.int32, (t, t), 1)
        ss = [jnp.where(keep, s, NEG_BIG) for s in ss]
    return ss, qts


def _on_blocks(qi, ki, step):
    @pl.when(ki < qi)
    def _():
        step(False)

    @pl.when(ki == qi)
    def _():
        step(True)


def _fox_grid(nblk, tokw, t, q_major):
    if q_major:
        pairs = [(qi, ki) for qi in range(nblk) for ki in range(qi + 1)]
    else:
        pairs = [(qi, ki) for ki in range(nblk) for qi in range(ki, nblk)]
    tables = [jnp.array([p[i] for p in pairs], jnp.int32) for i in (0, 1)]

    def q_spec(off=0, wide=False):
        width = 2 * LANE if wide else LANE
        return pl.BlockSpec((None, t, width), lambda b, p, i, qt, kt: (b, qt[i], p + off))

    def k_spec(off=0):
        return pl.BlockSpec((None, t, LANE), lambda b, p, i, qt, kt: (b, kt[i], p + off))

    return tables, len(pairs), q_spec, k_spec, tokw // LANE


def _lanes(col):
    return jnp.broadcast_to(col, (col.shape[0], LANE))


def _across(stat, width):
    return jnp.tile(stat, (1, width // LANE))


def _fox_fwd(proj3, kvf3, aq3, ak3, tokw, *, name):
    nb, seq, _ = proj3.shape
    t = _pick(seq, 512, LANE)
    tables, n_pairs, q_spec, k_spec, hp0 = _fox_grid(seq // t, tokw, t, True)

    def body(qt_ref, kt_ref, q_ref, k_ref, v_ref, aq_ref, ak_ref, o_ref, lse_ref, m_s, l_s, acc_s):
        i = pl.program_id(2)
        qi, ki = qt_ref[i], kt_ref[i]

        @pl.when(ki == 0)
        def _():
            m_s[...] = jnp.full(m_s.shape, NEG_BIG, F32)
            l_s[...] = jnp.zeros(l_s.shape, F32)
            acc_s[...] = jnp.zeros(acc_s.shape, F32)

        def step(masked):
            v = v_ref[...].astype(BF16)
            ss, _ = _fox_scores(q_ref[...], k_ref[...].astype(BF16), aq_ref[...], ak_ref[...], masked)
            for h, s in enumerate(ss):
                m_old = m_s[h]
                m_new = jnp.maximum(m_old, _lanes(jnp.max(s, axis=-1, keepdims=True)))
                alpha = jnp.exp(m_old - m_new)
                p = jnp.exp(s - _across(m_new, t))
                l_s[h] = alpha * l_s[h] + _lanes(jnp.sum(p, axis=-1, keepdims=True))
                acc_s[h] = alpha * acc_s[h] + _bdot(p, v, "nn")
                m_s[h] = m_new

        _on_blocks(qi, ki, step)

        @pl.when(ki == qi)
        def _():
            o_ref[...] = jnp.where(_head_masks()[0], acc_s[0] / l_s[0], acc_s[1] / l_s[1])
            lse_ref[...] = jnp.concatenate([m_s[0] + jnp.log(l_s[0]), m_s[1] + jnp.log(l_s[1])], axis=1)

    stat = pltpu.VMEM((2, t, LANE), F32)
    return pl.pallas_call(
        body, name=name,
        grid_spec=pltpu.PrefetchScalarGridSpec(
            num_scalar_prefetch=2, grid=(nb, hp0, n_pairs),
            in_specs=[q_spec(), k_spec(), k_spec(hp0), q_spec(), k_spec()],
            out_specs=[q_spec(), q_spec(wide=True)], scratch_shapes=[stat, stat, stat]),
        out_shape=[jax.ShapeDtypeStruct((nb, seq, tokw), F32), jax.ShapeDtypeStruct((nb, seq, 2 * tokw), F32)],
        compiler_params=_params(),
    )(*tables, proj3, kvf3, kvf3, aq3, ak3)


def _fox_bwd_common(q_ref, k_ref, v_ref, aq_ref, ak_ref, do_ref, lse_ref, delta_ref, masked):
    k, v = k_ref[...].astype(BF16), v_ref[...].astype(BF16)
    ss, qts = _fox_scores(q_ref[...], k, aq_ref[...], ak_ref[...], masked)
    do = do_ref[...]
    t = do.shape[0]
    out = []
    for h, (s, qt, msk) in enumerate(zip(ss, qts, _head_masks())):
        doh = jnp.where(msk, do, 0.0).astype(BF16)
        p = jnp.exp(s - _across(lse_ref[:, h * LANE:(h + 1) * LANE], t))
        ds = p * (_bdot(doh, v, "nt") - _across(delta_ref[:, h * LANE:(h + 1) * LANE], t))
        out.append((qt, doh, p, ds))
    return out, k


def _fox_bwd_dq(proj3, kvf3, aq3, ak3, o3, dmixin3, lse3, tokw, *, name):
    nb, seq, _ = proj3.shape
    t = _pick(seq, 512, LANE)
    tables, n_pairs, q_spec, k_spec, hp0 = _fox_grid(seq // t, tokw, t, True)

    def body(qt_ref, kt_ref, q_ref, k_ref, v_ref, aq_ref, ak_ref, o_ref, do_ref, lse_ref, dq_ref, delta_ref,
             dfq_ref, acc_s, row_s):
        i = pl.program_id(2)
        qi, ki = qt_ref[i], kt_ref[i]

        @pl.when(ki == 0)
        def _():
            acc_s[...] = jnp.zeros(acc_s.shape, F32)
            row_s[...] = jnp.zeros(row_s.shape, F32)
            prod = do_ref[...] * o_ref[...]
            delta_ref[...] = jnp.concatenate(
                [_lanes(jnp.sum(jnp.where(msk, prod, 0.0), axis=-1, keepdims=True)) for msk in _head_masks()],
                axis=1)

        def step(masked):
            heads, k = _fox_bwd_common(q_ref, k_ref, v_ref, aq_ref, ak_ref, do_ref, lse_ref, delta_ref, masked)
            for h, (_, _, _, ds) in enumerate(heads):
                acc_s[h] += _bdot(ds, k, "nn")
                row_s[h] += _lanes(jnp.sum(ds, axis=-1, keepdims=True))

        _on_blocks(qi, ki, step)

        @pl.when(ki == qi)
        def _():
            dq_ref[...] = jnp.where(_head_masks()[0], acc_s[0], acc_s[1]) * ATT_SCALE
            dfq_ref[...] = jnp.concatenate([row_s[0], row_s[1]], axis=1)

    wide = jax.ShapeDtypeStruct((nb, seq, 2 * tokw), F32)
    stat = pltpu.VMEM((2, t, LANE), F32)
    return pl.pallas_call(
        body, name=name,
        grid_spec=pltpu.PrefetchScalarGridSpec(
            num_scalar_prefetch=2, grid=(nb, hp0, n_pairs),
            in_specs=[q_spec(), k_spec(), k_spec(hp0), q_spec(), k_spec(), q_spec(), q_spec(), q_spec(wide=True)],
            out_specs=[q_spec(), q_spec(wide=True), q_spec(wide=True)], scratch_shapes=[stat, stat]),
        out_shape=[jax.ShapeDtypeStruct((nb, seq, tokw), F32), wide, wide], compiler_params=_params(),
    )(*tables, proj3, kvf3, kvf3, aq3, ak3, o3, dmixin3, lse3)


def _fox_bwd_dkv(proj3, kvf3, aq3, ak3, dmixin3, lse3, delta3, tokw, *, name):
    nb, seq, _ = proj3.shape
    t = _pick(seq, 512, LANE)
    nblk = seq // t
    tables, n_pairs, q_spec, k_spec, hp0 = _fox_grid(nblk, tokw, t, False)
    dfk_spec = pl.BlockSpec((None, None, SUBLANE, t), lambda b, p, i, qt, kt: (b, p, 0, kt[i]))

    def body(qt_ref, kt_ref, q_ref, k_ref, v_ref, aq_ref, ak_ref, do_ref, lse_ref, delta_ref, dk_ref, dv_ref,
             dfk_ref, dk_s, dv_s, dfk_s):
        i = pl.program_id(2)
        qi, ki = qt_ref[i], kt_ref[i]

        @pl.when(qi == ki)
        def _():
            dk_s[...] = jnp.zeros(dk_s.shape, F32)
            dv_s[...] = jnp.zeros(dv_s.shape, F32)
            dfk_s[...] = jnp.zeros(dfk_s.shape, F32)

        def step(masked):
            heads, _ = _fox_bwd_common(q_ref, k_ref, v_ref, aq_ref, ak_ref, do_ref, lse_ref, delta_ref, masked)
            for h, ((qt, doh, p, ds), msk) in enumerate(zip(heads, _head_masks())):
                dv_s[...] += _bdot(p, doh, "tn")
                dk_s[...] += jnp.where(msk, _bdot(ds, qt, "tn"), 0.0)
                dfk_s[h:h + 1, :] -= jnp.sum(ds, axis=0, keepdims=True)

        _on_blocks(qi, ki, step)

        @pl.when(qi == nblk - 1)
        def _():
            dk_ref[...] = dk_s[...]
            dv_ref[...] = dv_s[...]
            dfk_ref[...] = dfk_s[...]

    out = jax.ShapeDtypeStruct((nb, seq, tokw), F32)
    return pl.pallas_call(
        body, name=name,
        grid_spec=pltpu.PrefetchScalarGridSpec(
            num_scalar_prefetch=2, grid=(nb, hp0, n_pairs),
            in_specs=[q_spec(), k_spec(), k_spec(hp0), q_spec(), k_spec(), q_spec(), q_spec(wide=True),
                      q_spec(wide=True)],
            out_specs=[k_spec(), k_spec(), dfk_spec],
            scratch_shapes=[pltpu.VMEM((t, LANE), F32), pltpu.VMEM((t, LANE), F32), pltpu.VMEM((SUBLANE, t), F32)]),
        out_shape=[out, out, jax.ShapeDtypeStruct((nb, hp0, SUBLANE, seq), F32)], compiler_params=_params(),
    )(*tables, proj3, kvf3, kvf3, aq3, ak3, dmixin3, lse3, delta3)


def _peer(k):
    x, y, c = lax.axis_index("x"), lax.axis_index("y"), lax.axis_index("c")
    return (1 - x if k & 4 else x, 1 - y if k & 2 else y, 1 - c if k & 1 else c)


def _dev_index(p):
    return 4 * p[0] + 2 * p[1] + p[2]


def _exchange(send, *, gather, name):
    block = send.shape[-2:]

    def body(s_ref, o_ref, send_sems, recv_sems, local_sem):
        me = _peer(0)
        mine = pltpu.make_async_copy(s_ref if gather else s_ref.at[_dev_index(me)], o_ref.at[_dev_index(me)],
                                     local_sem)
        mine.start()
        sends, recvs = [], []
        for k in range(1, N_DEV):
            peer = _peer(k)
            src = s_ref if gather else s_ref.at[_dev_index(peer)]
            sends.append(pltpu.make_async_remote_copy(
                src_ref=src, dst_ref=o_ref.at[_dev_index(me)], send_sem=send_sems.at[k - 1],
                recv_sem=recv_sems.at[k - 1], device_id=peer, device_id_type=MESH_T))
            recvs.append(pltpu.make_async_remote_copy(
                src_ref=src, dst_ref=o_ref.at[_dev_index(peer)], send_sem=send_sems.at[k - 1],
                recv_sem=recv_sems.at[k - 1], device_id=peer, device_id_type=MESH_T))
        for cp in sends:
            cp.start()
        for cp in recvs:
            cp.wait_recv()
        for cp in sends:
            cp.wait_send()
        mine.wait()

    return pl.pallas_call(
        body, name=name,
        in_specs=[pl.BlockSpec(memory_space=pltpu.HBM)], out_specs=pl.BlockSpec(memory_space=pltpu.HBM),
        out_shape=jax.ShapeDtypeStruct((N_DEV,) + block, send.dtype),
        scratch_shapes=[pltpu.SemaphoreType.DMA((N_DEV - 1,)), pltpu.SemaphoreType.DMA((N_DEV - 1,)),
                        pltpu.SemaphoreType.DMA],
    )(send)


_HBM = pl.BlockSpec(memory_space=pltpu.HBM)
CHIP_RELATIONS = (2, 4, 6)


def _chip_index(p):
    return 2 * p[0] + p[1]


def _run_copies(sends, recvs):
    for cp in sends:
        cp.start()
    for cp in recvs:
        cp.wait_recv()
    for cp in sends:
        cp.wait_send()


def _gather_shards(shards, *, name):
    n = len(shards)

    def body(*refs):
        ins, outs, send_sems, recv_sems = refs[:n], refs[n:2 * n], refs[2 * n], refs[2 * n + 1]
        me = _peer(0)
        sends, recvs = [], []
        for j, k in enumerate(CHIP_RELATIONS):
            peer = _peer(k)
            for i in range(n):
                sem = dict(send_sem=send_sems.at[3 * i + j], recv_sem=recv_sems.at[3 * i + j], device_id=peer,
                           device_id_type=MESH_T)
                sends.append(pltpu.make_async_remote_copy(src_ref=ins[i], dst_ref=outs[i].at[_chip_index(me)], **sem))
                recvs.append(pltpu.make_async_remote_copy(src_ref=ins[i], dst_ref=outs[i].at[_chip_index(peer)], **sem))
        _run_copies(sends, recvs)

    return pl.pallas_call(
        body, name=name, in_specs=[_HBM] * n, out_specs=[_HBM] * n,
        out_shape=[jax.ShapeDtypeStruct((N_CHIP,) + a.shape, a.dtype) for a in shards],
        scratch_shapes=[pltpu.SemaphoreType.DMA((3 * n,)), pltpu.SemaphoreType.DMA((3 * n,))],
    )(*shards)


def _scatter_halves(grads, *, name):
    n = len(grads)

    def body(*refs):
        ins, outs, send_sems, recv_sems = refs[:n], refs[n:2 * n], refs[2 * n], refs[2 * n + 1]
        sends, recvs = [], []
        for k in range(1, N_DEV):
            peer = _peer(k)
            for i in range(n):
                sem = dict(send_sem=send_sems.at[7 * i + k - 1], recv_sem=recv_sems.at[7 * i + k - 1],
                           device_id=peer, device_id_type=MESH_T)
                src = ins[i].at[_chip_index(peer), peer[2]]
                sends.append(pltpu.make_async_remote_copy(src_ref=src, dst_ref=outs[i].at[k - 1], **sem))
                recvs.append(pltpu.make_async_remote_copy(src_ref=src, dst_ref=outs[i].at[k - 1], **sem))
        _run_copies(sends, recvs)

    return pl.pallas_call(
        body, name=name, in_specs=[_HBM] * n, out_specs=[_HBM] * n,
        out_shape=[jax.ShapeDtypeStruct((N_DEV - 1,) + g.shape[2:], g.dtype) for g in grads],
        scratch_shapes=[pltpu.SemaphoreType.DMA((7 * n,)), pltpu.SemaphoreType.DMA((7 * n,))],
    )(*grads)


def _swap_halves(arrays, *, name):
    n = len(arrays)

    def body(*refs):
        outs, send_sems, recv_sems = refs[n:2 * n], refs[2 * n], refs[2 * n + 1]
        c = lax.axis_index("c")
        sib = _peer(1)
        sends, recvs = [], []
        for i in range(n):
            sem = dict(send_sem=send_sems.at[i], recv_sem=recv_sems.at[i], device_id=sib, device_id_type=MESH_T)
            sends.append(pltpu.make_async_remote_copy(src_ref=outs[i].at[c], dst_ref=outs[i].at[c], **sem))
            recvs.append(pltpu.make_async_remote_copy(src_ref=outs[i].at[c], dst_ref=outs[i].at[1 - c], **sem))
        _run_copies(sends, recvs)

    return pl.pallas_call(
        body, name=name, in_specs=[_HBM] * n, out_specs=[_HBM] * n,
        out_shape=[jax.ShapeDtypeStruct(a.shape, a.dtype) for a in arrays],
        input_output_aliases={i: i for i in range(n)},
        scratch_shapes=[pltpu.SemaphoreType.DMA((n,)), pltpu.SemaphoreType.DMA((n,))],
    )(*arrays)


def _adam_math(g, w, m, v):
    bc1 = 1.0 - ADAM_B1 ** ADAM_STEP
    bc2 = 1.0 - ADAM_B2 ** ADAM_STEP
    m_new = ADAM_B1 * m + (1.0 - ADAM_B1) * g
    v_new = ADAM_B2 * v + (1.0 - ADAM_B2) * (g * g)
    delta = -ADAM_LR * ((m_new / bc1) / (jnp.sqrt(v_new / bc2) + ADAM_EPS) + ADAM_WD * w)
    return delta, m_new, v_new


def _adamw_half(grads, parts, w, m, v, qc, *, name):
    _, _, rows, cols = grads.shape
    tile = _pick(rows, 64, SUBLANE)

    def body(qc_ref, g_ref, p_ref, w_ref, m_ref, v_ref, go_ref, do_ref, mo_ref, vo_ref):
        del qc_ref
        g = g_ref[...]
        for k in range(N_DEV - 1):
            g = g + p_ref[k]
        delta, m_new, v_new = _adam_math(g, w_ref[...], m_ref[...], v_ref[...])
        go_ref[...] = g
        do_ref[...] = delta
        mo_ref[...] = m_new
        vo_ref[...] = v_new

    half = pl.BlockSpec((None, tile, cols), lambda i, qc: (qc[1], i, 0))
    shape = jax.ShapeDtypeStruct((2, rows, cols), F32)
    return pl.pallas_call(
        body, name=name,
        grid_spec=pltpu.PrefetchScalarGridSpec(
            num_scalar_prefetch=1, grid=(rows // tile,),
            in_specs=[pl.BlockSpec((None, None, tile, cols), lambda i, qc: (qc[0], qc[1], i, 0)),
                      pl.BlockSpec((N_DEV - 1, tile, cols), lambda i, qc: (0, i, 0)), half, half, half],
            out_specs=[half] * 4),
        out_shape=[shape] * 4, compiler_params=_params(),
    )(qc, grads, parts, w, m, v)


def _adamw(parts, w, m, v, *, name):
    _, rows, cols = parts.shape
    tile = _pick(rows, 256, SUBLANE)

    def body(p_ref, w_ref, m_ref, v_ref, o_ref):
        g = p_ref[0]
        for i in range(1, N_DEV):
            g = g + p_ref[i]
        delta, m_new, v_new = _adam_math(g, w_ref[...], m_ref[...], v_ref[...])
        o_ref[0] = g
        o_ref[1] = delta
        o_ref[2] = m_new
        o_ref[3] = v_new

    spec = pl.BlockSpec((tile, cols), lambda i: (i, 0))
    return pl.pallas_call(
        body, name=name, grid=(rows // tile,),
        in_specs=[pl.BlockSpec((N_DEV, tile, cols), lambda i: (0, i, 0)), spec, spec, spec],
        out_specs=pl.BlockSpec((4, tile, cols), lambda i: (0, i, 0)),
        out_shape=jax.ShapeDtypeStruct((4, rows, cols), F32), compiler_params=_params(),
    )(parts, w, m, v)


def _layout(shapes, names, align):
    out, off = [], 0
    for n in names:
        size = math.prod(shapes[n])
        out.append((n, tuple(shapes[n]), off, size))
        off += _round_up(size, align)
    return out, off


def _pack(arrays, layout, total, lead=()):
    parts = []
    for i, (n, _, off, size) in enumerate(layout):
        end = layout[i + 1][2] if i + 1 < len(layout) else total
        flat = arrays[n].reshape(lead + (size,))
        if end - off > size:
            flat = jnp.pad(flat, [(0, 0)] * len(lead) + [(0, end - off - size)])
        parts.append(flat)
    return jnp.concatenate(parts, axis=len(lead))


def _unpack(flat, layout, lead=()):
    return {n: flat[..., off:off + size].reshape(lead + shape) for n, shape, off, size in layout}


def _to_shards(full, axis):
    shp = full.shape
    return jnp.moveaxis(full.reshape(shp[:axis] + (N_CHIP, shp[axis] // N_CHIP) + shp[axis + 1:]), axis, 0)


def _from_shards(shards, axis):
    x = jnp.moveaxis(shards, 0, axis)
    shp = x.shape
    return x.reshape(shp[:axis] + (shp[axis] * shp[axis + 1],) + shp[axis + 2:])


def _pad_cols(w, per, padded):
    lead = w.shape[:-1]
    x = w.reshape(lead + (-1, per))
    x = jnp.pad(x, [(0, 0)] * len(lead) + [(0, 0), (0, padded - per)])
    return x.reshape(lead + (-1,))


def _unpad_cols(w, per, padded):
    lead = w.shape[:-1]
    return w.reshape(lead + (-1, padded))[..., :per].reshape(lead + (-1,))


def _local_step(x, mem, target, W):
    nb, seq, d = x.shape
    n = nb * seq
    tokw = d - MEM_WIDTH
    heads = tokw // HEAD_DIM
    mlen = mem.shape[1]
    dff2 = W["ffn_w_up"].shape[-1]
    per = dff2 // N_CHIP
    per_p = _round_up(per, LANE)
    fp = 2 * per_p
    kvw = 2 * tokw + heads
    kvp = 2 * tokw + LANE
    gate_block = 2 * tokw // LANE

    x2d = x.reshape(n, d)
    mem2d = mem.reshape(nb * mlen, d)
    t2d = target.reshape(n, d)
    row = lambda a: a.reshape(1, -1)
    ones_tok = jnp.ones((1, tokw), F32)

    pool_bd = jax.scipy.linalg.block_diag(*[W["a_pool_w"][0, i] for i in range(len(POOL_WINDOWS))]).astype(BF16)
    kv_w = jnp.pad(W["kv_w"], ((0, 0), (0, kvp - kvw)))
    fb = jnp.pad(W["f_b"], (0, LANE - heads)).reshape(1, LANE)
    w_up = [_pad_cols(W["ffn_w_up"][l], per, per_p) for l in range(DEPTH)]
    w_down = [jnp.pad(W["ffn_w_down"][l].reshape(2, per, d), ((0, 0), (0, per_p - per), (0, 0))).reshape(fp, d)
              for l in range(DEPTH)]
    conv_w = [jnp.pad(_pad_cols(W["ffn_conv_w"][l], per, per_p), ((0, SUBLANE - CONV_WIDTH), (0, 0)))
              for l in range(DEPTH)]
    conv_b = [_pad_cols(W["ffn_conv_b"][l], per, per_p).reshape(1, 2 * fp) for l in range(DEPTH)]
    w_in = [W["a_w_in"][0], W["b_w_q"][0]]
    w_out = [W["a_w_out"][0], W["b_w_out"][0]]

    saved = []
    cur = x2d
    for l in range(DEPTH):
        s = {"x_in": cur}
        memkv = _mm(mem2d, W["mem_w_kv"][l], "nn", name=f"memkv{l}").reshape(nb, mlen, 2 * MEM_WIDTH)
        if l == 0:
            proj = _mm(cur, w_in[l], "nn", name="proj0")
            pooled = _pool_fwd(proj.reshape(nb, seq, d), tokw, name="pool_fwd").reshape(n, tokw)
            tok = _mm(pooled, pool_bd, "nn", name="pool_mix")
            scale = W["a_pool_scale"].reshape(1, tokw)
            s.update(pooled=pooled, mixed=tok, scale=scale)
        else:
            kvf = _mm(cur, kv_w, "nn", tn=kvp, name="kvf")
            kvf3 = kvf.reshape(nb, seq, kvp)
            gsum = _gate_fwd(kvf3, fb, gate_block, name="gate_fwd")[:, :, :heads]
            aq3, ak3 = _bias_lanes(gsum)
            proj = _mm(cur, w_in[l], "nn", name="proj1")
            o3, lse3 = _fox_fwd(proj.reshape(nb, seq, d), kvf3, aq3, ak3, tokw, name="fox_fwd")
            tok = o3.reshape(n, tokw)
            scale = ones_tok
            s.update(kvf3=kvf3, aq3=aq3, ak3=ak3, o3=o3, lse3=lse3)
        mixin = _memattn_fwd(tok, proj, memkv, scale, seq=seq, name=f"memattn_fwd{l}")
        mix = _mm(mixin, w_out[l], "nn", name=f"mix{l}")
        x1 = _ln_fwd(cur, mix, row(W["ln1_g"][l]), row(W["ln1_b"][l]), name=f"ln1_fwd{l}")
        up = _mm(x1, w_up[l], "nn", tn=per_p, name=f"ffn_up{l}")
        act = _convgate_fwd(up.reshape(nb, seq, 2 * fp), conv_w[l], conv_b[l], name=f"convgate_fwd{l}")
        act = act.reshape(n, fp)
        ffn = _mm(act, w_down[l], "nn", tk=fp, name=f"ffn_down{l}")
        s.update(proj=proj, memkv=memkv, mixin=mixin, mix=mix, x1=x1, up=up, act=act, ffn=ffn)
        saved.append(s)
        if l + 1 < DEPTH:
            cur = _ln_fwd(x1, ffn, row(W["ln2_g"][l]), row(W["ln2_b"][l]), name=f"ln2_fwd{l}")

    G = {}
    ln_g = {k: [None] * DEPTH for k in ("ln1_g", "ln1_b", "ln2_g", "ln2_b")}
    stack = {k: [None] * DEPTH for k in ("mem_w_kv", "ffn_w_up", "ffn_conv_w", "ffn_conv_b", "ffn_w_down")}
    dx_terms = None
    loss = None
    for l in reversed(range(DEPTH)):
        s = saved[l]
        g2 = row(W["ln2_g"][l])
        if l == DEPTH - 1:
            dres, dffn, loss, dg, db = _final_ln_loss(s["x1"], s["ffn"], t2d, g2, row(W["ln2_b"][l]),
                                                      name="final_ln_loss")
        else:
            dres, dffn, dg, db = _ln_bwd(s["x1"], s["ffn"], g2, dx_terms, name=f"ln2_bwd{l}")
        ln_g["ln2_g"][l], ln_g["ln2_b"][l] = dg[0], db[0]
        dact = _mm(dffn, w_down[l], "nt", tn=fp, name=f"ffn_down_dx{l}")
        stack["ffn_w_down"][l] = _mm(s["act"], dffn, "tn", tm=per_p, tk=512, name=f"ffn_down_dw{l}")
        du3, dg3, dcw, dcb = _convgate_bwd(s["up"].reshape(nb, seq, 2 * fp), dact.reshape(nb, seq, fp), conv_w[l],
                                           conv_b[l], name=f"convgate_bwd{l}")
        du, dgt = du3.reshape(n, fp), dg3.reshape(n, fp)
        dx1_u = _mm(du, w_up[l], "nt", tk=per_p, name=f"ffn_up_dx_u{l}")
        dx1_ffn = _mm(dgt, w_up[l], "nt", tk=per_p, b_col0=fp, adds=[dx1_u], name=f"ffn_up_dx_g{l}")
        stack["ffn_w_up"][l] = [_mm(s["x1"], part, "tn", tm=d, tn=per_p, tk=512, name=f"ffn_up_dw_{nm}{l}")
                                for nm, part in (("u", du), ("g", dgt))]
        stack["ffn_conv_w"][l] = dcw[:CONV_WIDTH]
        stack["ffn_conv_b"][l] = dcb[0]
        dres1, dmix, dg, db = _ln_bwd(s["x_in"], s["mix"], row(W["ln1_g"][l]), [dres, dx1_ffn], name=f"ln1_bwd{l}")
        ln_g["ln1_g"][l], ln_g["ln1_b"][l] = dg[0], db[0]
        dmixin = _mm(dmix, w_out[l], "nt", name=f"mix_dx{l}")
        d_w_out = _mm(s["mixin"], dmix, "tn", tm=d, tk=512, name=f"mix_dw{l}")
        if l == 0:
            G["a_w_out"] = d_w_out[None]
            dmixed, dscale = _scale_bwd(dmixin, s["mixed"], s["scale"], name="scale_bwd")
            G["a_pool_scale"] = dscale
            dpooled = _mm(dmixed, pool_bd, "nt", name="pool_mix_dx")
            dpw = _mm(s["pooled"], dmixed, "tn", tm=tokw, tk=512, name="pool_mix_dw")
            grp = tokw // len(POOL_WINDOWS)
            G["a_pool_w"] = jnp.stack([dpw[i * grp:(i + 1) * grp, i * grp:(i + 1) * grp]
                                       for i in range(len(POOL_WINDOWS))])[None]
            dtok = _pool_bwd(dpooled.reshape(nb, seq, tokw), name="pool_bwd").reshape(n, tokw)
            extra = []
        else:
            G["b_w_out"] = d_w_out[None]
            p3 = s["proj"].reshape(nb, seq, d)
            dm3 = dmixin.reshape(nb, seq, d)
            dq3, delta3, dfq3 = _fox_bwd_dq(p3, s["kvf3"], s["aq3"], s["ak3"], s["o3"], dm3, s["lse3"], tokw,
                                            name="fox_bwd_dq")
            dtok = dq3.reshape(n, tokw)
            dk3, dv3, dfk = _fox_bwd_dkv(p3, s["kvf3"], s["aq3"], s["ak3"], dm3, s["lse3"], delta3, tokw,
                                         name="fox_bwd_dkv")
            dgsum = jnp.swapaxes(dfk[:, :, :2, :].reshape(nb, heads, seq), 1, 2) + dfq3[:, :, ::LANE]
            dgsum = jnp.pad(dgsum, ((0, 0), (0, 0), (0, LANE - heads)))
            df3, dfb = _gate_bwd(s["kvf3"], fb, dgsum, gate_block, heads, name="gate_bwd")
            G["f_b"] = dfb[0, :heads]
            dkvf = [(dk3.reshape(n, tokw), 0, "k"), (dv3.reshape(n, tokw), tokw, "v"),
                    (df3.reshape(n, LANE), 2 * tokw, "f")]
            dx_kv = []
            for part, col0, nm in dkvf:
                dx_kv = [_mm(part, kv_w, "nt", b_col0=col0, adds=dx_kv, name=f"kvf_dx_{nm}")]
            extra = dx_kv
            G["kv_w"] = jnp.concatenate([_mm(s["x_in"], part, "tn", tm=d, tk=512, name=f"kvf_dw_{nm}")
                                         for part, _, nm in dkvf], axis=1)[:, :kvw]
        dproj, dmemkv = _memattn_bwd(dmixin, dtok, s["proj"], s["memkv"], seq=seq, name=f"memattn_bwd{l}")
        stack["mem_w_kv"][l] = _mm(mem2d, dmemkv.reshape(nb * mlen, 2 * MEM_WIDTH), "tn", tm=d, tk=512,
                                   name=f"memkv_dw{l}")
        G["a_w_in" if l == 0 else "b_w_q"] = _mm(s["x_in"], dproj, "tn", tm=d, tk=512, name=f"proj_dw{l}")[None]
        if l == 0:
            grad_x = _mm(dproj, w_in[l], "nt", adds=[dres1], name="proj_dx0")
        else:
            dx_terms = [dres1, _mm(dproj, w_in[l], "nt", name="proj_dx1")] + extra
    for k, v in ln_g.items():
        G[k] = jnp.stack(v)
    G["mem_w_kv"] = jnp.stack(stack["mem_w_kv"])
    G["ffn_w_up"] = jnp.stack([jnp.concatenate([_unpad_cols(g, per, per_p) for g in halves], axis=1)
                               for halves in stack["ffn_w_up"]])
    G["ffn_conv_w"] = jnp.stack([_unpad_cols(g, per, per_p) for g in stack["ffn_conv_w"]])
    G["ffn_conv_b"] = jnp.stack([_unpad_cols(g, per, per_p) for g in stack["ffn_conv_b"]])
    G["ffn_w_down"] = jnp.stack([g.reshape(2, per_p, d)[:, :per].reshape(2 * per, d) for g in stack["ffn_w_down"]])
    return loss[0, 0], grad_x.reshape(nb, seq, d), G


def kernel(x, mem, a_w_in, a_pool_w, a_pool_scale, a_w_out, b_w_q, b_w_out, kv_w, f_b, mem_w_kv, ln1_g, ln1_b, ln2_g, ln2_b, ffn_w_up, ffn_conv_w, ffn_conv_b, ffn_w_down, loss_target, m_a_w_in, m_a_pool_w, m_a_pool_scale, m_a_w_out, m_b_w_q, m_b_w_out, m_kv_w, m_f_b, m_mem_w_kv, m_ln1_g, m_ln1_b, m_ln2_g, m_ln2_b, m_ffn_w_up, m_ffn_conv_w, m_ffn_conv_b, m_ffn_w_down, v_a_w_in, v_a_pool_w, v_a_pool_scale, v_a_w_out, v_b_w_q, v_b_w_out, v_kv_w, v_f_b, v_mem_w_kv, v_ln1_g, v_ln1_b, v_ln2_g, v_ln2_b, v_ffn_w_up, v_ffn_conv_w, v_ffn_conv_b, v_ffn_w_down):
    w_loc = dict(a_w_in=a_w_in, a_pool_w=a_pool_w, a_pool_scale=a_pool_scale, a_w_out=a_w_out, b_w_q=b_w_q,
                 b_w_out=b_w_out, kv_w=kv_w, f_b=f_b, mem_w_kv=mem_w_kv, ln1_g=ln1_g, ln1_b=ln1_b, ln2_g=ln2_g,
                 ln2_b=ln2_b, ffn_w_up=ffn_w_up, ffn_conv_w=ffn_conv_w, ffn_conv_b=ffn_conv_b, ffn_w_down=ffn_w_down)
    m_loc = dict(a_w_in=m_a_w_in, a_pool_w=m_a_pool_w, a_pool_scale=m_a_pool_scale, a_w_out=m_a_w_out,
                 b_w_q=m_b_w_q, b_w_out=m_b_w_out, kv_w=m_kv_w, f_b=m_f_b, mem_w_kv=m_mem_w_kv, ln1_g=m_ln1_g,
                 ln1_b=m_ln1_b, ln2_g=m_ln2_g, ln2_b=m_ln2_b, ffn_w_up=m_ffn_w_up, ffn_conv_w=m_ffn_conv_w,
                 ffn_conv_b=m_ffn_conv_b, ffn_w_down=m_ffn_w_down)
    v_loc = dict(a_w_in=v_a_w_in, a_pool_w=v_a_pool_w, a_pool_scale=v_a_pool_scale, a_w_out=v_a_w_out,
                 b_w_q=v_b_w_q, b_w_out=v_b_w_out, kv_w=v_kv_w, f_b=v_f_b, mem_w_kv=v_mem_w_kv, ln1_g=v_ln1_g,
                 ln1_b=v_ln1_b, ln2_g=v_ln2_g, ln2_b=v_ln2_b, ffn_w_up=v_ffn_w_up, ffn_conv_w=v_ffn_conv_w,
                 ffn_conv_b=v_ffn_conv_b, ffn_w_down=v_ffn_w_down)
    x_i, y_i, c = lax.axis_index("x"), lax.axis_index("y"), lax.axis_index("c")
    q = 2 * x_i + y_i
    qc = jnp.stack([q, c]).astype(jnp.int32)
    shapes = {k: v.shape for k, v in w_loc.items()}

    own = {k: (w_loc[k] if k in GATHER_F32 else w_loc[k].astype(BF16)) for k in SHARDED}
    gathered = _gather_shards([own[k] for k in SHARDED], name="gather_weights")
    W = {k: _from_shards(lax.dynamic_update_slice_in_dim(g, own[k][None], q, axis=0), SHARD_AXIS[k])
         for k, g in zip(SHARDED, gathered)}
    for k in REPLICATED:
        W[k] = w_loc[k]

    loss_part, grad_x, G = _local_step(x, mem, loss_target, W)
    loss = lax.psum(loss_part, ("x", "y", "c"))

    def halves(a):
        if a.ndim == 3 and a.shape[0] == 2:
            return a
        rows = math.prod(a.shape[:-1])
        return a.reshape(2, rows // 2, a.shape[-1])

    g_chip = [_to_shards(G[k], SHARD_AXIS[k]) for k in BIG]
    g_chip = [g.reshape((N_CHIP,) + halves(w_loc[k]).shape) for k, g in zip(BIG, g_chip)]
    parts = _scatter_halves(g_chip, name="scatter_grads")
    res = []
    for k, g, p in zip(BIG, g_chip, parts):
        res.extend(_adamw_half(g, p, halves(w_loc[k]), halves(m_loc[k]), halves(v_loc[k]), qc, name=f"adamw_{k}"))
    res = _swap_halves(res, name="swap_halves")
    out = {k: [r.reshape(shapes[k]) for r in res[4 * i:4 * i + 4]] for i, k in enumerate(BIG)}

    lay_r, tot_r = _layout(shapes, REPLICATED, PACK_COLS)
    rep_rows = tot_r // PACK_COLS
    rows = _round_up(rep_rows + 1, SUBLANE)
    scale_w = shapes["a_pool_scale"][-1]

    def small(rep, scale_row):
        lead = scale_row.shape[:-2]
        rep = jnp.broadcast_to(_pack(rep, lay_r, tot_r).reshape(rep_rows, PACK_COLS), lead + (rep_rows, PACK_COLS))
        pad = [(0, 0)] * len(lead)
        return jnp.concatenate([rep, jnp.pad(scale_row, pad + [(0, rows - rep_rows - 1), (0, PACK_COLS - scale_w)])],
                               axis=-2)

    g_scale = jnp.repeat(_to_shards(G["a_pool_scale"], 1), 2, axis=0)
    sm_parts = _exchange(small(G, g_scale), gather=False, name="scatter_small")
    sm = _adamw(sm_parts, *[small(d, d["a_pool_scale"]) for d in (w_loc, m_loc, v_loc)], name="adamw_small")
    out_r = _unpack(sm[:, :rep_rows].reshape(4, tot_r), lay_r, lead=(4,))
    for k in REPLICATED:
        out[k] = [out_r[k][a] for a in range(4)]
    out["a_pool_scale"] = [sm[a, rep_rows:rep_rows + 1, :scale_w] for a in range(4)]

    outs = [loss, grad_x]
    for a in range(4):
        for k in WEIGHTS:
            outs.append(out[k][a])
    return tuple(outs)
```

```python
import functools
import math

import jax
import jax.numpy as jnp
from jax import lax
from jax.experimental import pallas as pl
from jax.experimental.pallas import tpu as pltpu

F32 = jnp.float32
BF16 = jnp.bfloat16

HEAD_DIM = 64
MEM_HEADS = 4
MEM_WIDTH = MEM_HEADS * HEAD_DIM
POOL_WINDOWS = (2, 4, 8, 16)
MAX_WINDOW = 16
CONV_WIDTH = 3
DEPTH = 2
DN_ALPHA = (2.0 * DEPTH) ** 0.25
LN_EPS = 1e-5
ATT_SCALE = HEAD_DIM ** -0.5
NEG_BIG = -1e30

ADAM_LR = 0.001
ADAM_B1 = 0.9
ADAM_B2 = 0.999
ADAM_EPS = 1e-08
ADAM_WD = 0.01
ADAM_STEP = 10

LANE = 128
SUBLANE = 8
PACK_COLS = 1024
VMEM_LIMIT = 56 * 1024 * 1024
N_DEV = 8
N_CHIP = 4
MESH_T = pl.DeviceIdType.MESH

SHARDED = ("a_w_in", "a_pool_scale", "a_w_out", "b_w_q", "b_w_out", "kv_w", "mem_w_kv", "ffn_w_up",
           "ffn_conv_w", "ffn_w_down")
SHARD_AXIS = {"a_w_in": 1, "a_pool_scale": 1, "a_w_out": 1, "b_w_q": 1, "b_w_out": 1, "kv_w": 1, "mem_w_kv": 1,
              "ffn_w_up": 2, "ffn_conv_w": 2, "ffn_w_down": 1}
GATHER_F32 = ("a_pool_scale", "ffn_conv_w")
BIG = tuple(k for k in SHARDED if k != "a_pool_scale")
REPLICATED = ("a_pool_w", "f_b", "ln1_g", "ln1_b", "ln2_g", "ln2_b", "ffn_conv_b")
WEIGHTS = ("a_w_in", "a_pool_w", "a_pool_scale", "a_w_out", "b_w_q", "b_w_out", "kv_w", "f_b", "mem_w_kv",
           "ln1_g", "ln1_b", "ln2_g", "ln2_b", "ffn_w_up", "ffn_conv_w", "ffn_conv_b", "ffn_w_down")


def _round_up(n, m):
    return -(-n // m) * m


def _pick(dim, pref, unit=LANE):
    if dim <= pref:
        return dim
    t = (pref // unit) * unit
    while t >= unit:
        if dim % t == 0:
            return t
        t -= unit
    raise ValueError(f"no tile for {dim} <= {pref}")


def _params():
    return pltpu.CompilerParams(vmem_limit_bytes=VMEM_LIMIT)


_DIMS = {"nn": ((1,), (0,)), "nt": ((1,), (1,)), "tn": ((0,), (0,))}


def _bdot(a, b, mode):
    return lax.dot_general(a.astype(BF16), b.astype(BF16), (_DIMS[mode], ((), ())), preferred_element_type=F32)


def _mm(a, b, mode, *, name, tm=512, tn=1024, tk=2048, adds=(), b_col0=0):
    if mode == "nn":
        (M, K), (K2, N) = a.shape, b.shape
    elif mode == "nt":
        (M, K), (N, K2) = a.shape, b.shape
        K2 = K if b_col0 + K <= K2 else -1
    else:
        (K, M), (K2, N) = a.shape, b.shape
    assert K == K2 and (mode == "nt" or b_col0 == 0), (name, a.shape, b.shape)
    tm, tn = _pick(M, tm, SUBLANE if mode != "tn" else LANE), _pick(N, tn)
    tk = _pick(K, tk, LANE if mode != "tn" else SUBLANE)
    nk = K // tk
    assert b_col0 % tk == 0, (name, b_col0, tk)
    koff = b_col0 // tk
    n_add = len(adds)

    def body(*refs):
        a_ref, b_ref = refs[0], refs[1]
        add_refs = refs[2:2 + n_add]
        o_ref, acc_ref = refs[2 + n_add], refs[3 + n_add]
        part = _bdot(a_ref[...], b_ref[...], mode)

        def finish(r):
            for ar in add_refs:
                r = r + ar[...]
            o_ref[...] = r

        if nk == 1:
            finish(part)
        else:
            k = pl.program_id(2)

            @pl.when(k == 0)
            def _():
                acc_ref[...] = part

            @pl.when(k > 0)
            def _():
                acc_ref[...] += part

            @pl.when(k == nk - 1)
            def _():
                finish(acc_ref[...])

    if mode == "nn":
        a_spec = pl.BlockSpec((tm, tk), lambda i, j, k: (i, k))
        b_spec = pl.BlockSpec((tk, tn), lambda i, j, k: (k, j))
    elif mode == "nt":
        a_spec = pl.BlockSpec((tm, tk), lambda i, j, k: (i, k))
        b_spec = pl.BlockSpec((tn, tk), lambda i, j, k: (j, k + koff))
    else:
        a_spec = pl.BlockSpec((tk, tm), lambda i, j, k: (k, i))
        b_spec = pl.BlockSpec((tk, tn), lambda i, j, k: (k, j))
    o_spec = pl.BlockSpec((tm, tn), lambda i, j, k: (i, j))
    acc_shape = (tm, tn) if nk > 1 else (SUBLANE, LANE)
    return pl.pallas_call(
        body, name=name, grid=(M // tm, N // tn, nk),
        in_specs=[a_spec, b_spec] + [o_spec] * n_add, out_specs=o_spec,
        out_shape=jax.ShapeDtypeStruct((M, N), F32),
        scratch_shapes=[pltpu.VMEM(acc_shape, F32)],
        compiler_params=_params(),
    )(a, b, *adds)


def _rowwise(fn, tiled, full, outs_tiled, outs_acc, *, rows, tile, name, acc_period=None):
    n_tiles = rows // tile
    period = n_tiles if acc_period is None else acc_period
    arrays, in_specs = [], []
    for t in tiled:
        arr, width, cb = t if isinstance(t, tuple) else (t, t.shape[1], 0)
        arrays.append(arr)
        in_specs.append(pl.BlockSpec((tile, width), lambda i, cb=cb: (i, cb)))
    for f in full:
        arr, spec = f if isinstance(f, tuple) else (f, None)
        arrays.append(arr)
        in_specs.append(spec if spec is not None else pl.BlockSpec(arr.shape, lambda i, nd=arr.ndim: (0,) * nd))
    out_shape, out_specs = [], []
    for width, dt in outs_tiled:
        out_shape.append(jax.ShapeDtypeStruct((rows, width), dt))
        out_specs.append(pl.BlockSpec((tile, width), lambda i: (i, 0)))
    for acc in outs_acc:
        shape, dt = acc[0], acc[1]
        out_shape.append(jax.ShapeDtypeStruct(shape, dt))
        out_specs.append(acc[2] if len(acc) > 2 else pl.BlockSpec(shape, lambda i, nd=len(shape): (0,) * nd))
    n_in, n_t, n_a = len(arrays), len(outs_tiled), len(outs_acc)

    def body(*refs):
        vals = [r[...] for r in refs[:n_in]]
        o_t, o_a = fn(*vals)
        for r, v in zip(refs[n_in:n_in + n_t], o_t):
            r[...] = v.astype(r.dtype)
        first = pl.program_id(0) % period == 0
        for r, v in zip(refs[n_in + n_t:n_in + n_t + n_a], o_a):
            v = v.reshape(r.shape)

            @pl.when(first)
            def _(r=r, v=v):
                r[...] = v

            @pl.when(jnp.logical_not(first))
            def _(r=r, v=v):
                r[...] += v

    return pl.pallas_call(
        body, name=name, grid=(n_tiles,), in_specs=in_specs, out_specs=out_specs, out_shape=out_shape,
        compiler_params=_params(),
    )(*arrays)


def _ln_stats(h):
    mu = jnp.mean(h, axis=-1, keepdims=True)
    d = h - mu
    var = jnp.mean(d * d, axis=-1, keepdims=True)
    rstd = lax.rsqrt(var + LN_EPS)
    return d * rstd, rstd


def _ln_bwd_math(h, g, dy):
    xhat, rstd = _ln_stats(h)
    dxhat = dy * g
    dh = rstd * (dxhat - jnp.mean(dxhat, axis=-1, keepdims=True)
                 - xhat * jnp.mean(dxhat * xhat, axis=-1, keepdims=True))
    return dh, jnp.sum(dy * xhat, axis=0, keepdims=True), jnp.sum(dy, axis=0, keepdims=True)


def _ln_fwd(x, r, g, b, *, name):
    n, d = x.shape

    def fn(x, r, g, b):
        xhat, _ = _ln_stats(DN_ALPHA * x + r)
        return (xhat * g + b,), ()

    return _rowwise(fn, [x, r], [g, b], [(d, F32)], [], rows=n, tile=_pick(n, 512, SUBLANE), name=name)[0]


def _ln_bwd(x, r, g, dys, *, name):
    n, d = x.shape
    n_dy = len(dys)

    def fn(x, r, *rest):
        dy = rest[0]
        for e in rest[1:n_dy]:
            dy = dy + e
        dh, dg, db = _ln_bwd_math(DN_ALPHA * x + r, rest[n_dy], dy)
        return (DN_ALPHA * dh, dh), (dg, db)

    return _rowwise(fn, [x, r, *dys], [g], [(d, F32), (d, F32)], [((1, d), F32), ((1, d), F32)],
                    rows=n, tile=_pick(n, 256, SUBLANE), name=name)


def _final_ln_loss(x, r, target, g, b, *, name):
    n, d = x.shape

    def fn(x, r, t, g, b):
        h = DN_ALPHA * x + r
        xhat, _ = _ln_stats(h)
        err = xhat * g + b - t
        loss = jnp.full((1, LANE), 0.5 * jnp.sum(err * err) / d, F32)
        dh, dg, db = _ln_bwd_math(h, g, err / d)
        return (DN_ALPHA * dh, dh), (loss, dg, db)

    return _rowwise(fn, [x, r, target], [g, b], [(d, F32), (d, F32)],
                    [((1, LANE), F32), ((1, d), F32), ((1, d), F32)],
                    rows=n, tile=_pick(n, 256, SUBLANE), name=name)


def _mem_heads(qm):
    lane = lax.broadcasted_iota(jnp.int32, (1, MEM_WIDTH), 1)
    for h in range(MEM_HEADS):
        msk = (lane >= h * HEAD_DIM) & (lane < (h + 1) * HEAD_DIM)
        yield msk, jnp.where(msk, qm, 0.0).astype(BF16)


def _mem_softmax(qh, k):
    s = _bdot(qh, k, "nt") * ATT_SCALE
    p = jnp.exp(s - jnp.max(s, axis=-1, keepdims=True))
    return p / jnp.sum(p, axis=-1, keepdims=True)


def _memattn_fwd(tok, proj, memkv, scale, *, seq, name):
    n, tokw = tok.shape
    d = tokw + MEM_WIDTH
    tile = _pick(seq, 512, SUBLANE)

    def fn(tok, qm, kv, scale):
        k, v = kv[:, :MEM_WIDTH].astype(BF16), kv[:, MEM_WIDTH:].astype(BF16)
        out = jnp.zeros(qm.shape, F32)
        for msk, qh in _mem_heads(qm):
            out = jnp.where(msk, _bdot(_mem_softmax(qh, k), v, "nn"), out)
        return (jnp.concatenate([tok * scale, out], axis=1),), ()

    kv_spec = pl.BlockSpec((None,) + memkv.shape[1:], lambda i: (i // (seq // tile), 0, 0))
    return _rowwise(fn, [tok, (proj, MEM_WIDTH, tokw // MEM_WIDTH)], [(memkv, kv_spec), scale], [(d, F32)], [],
                    rows=n, tile=tile, name=name)[0]


def _memattn_bwd(dmixin, dtok, proj, memkv, *, seq, name):
    n, tokw = dtok.shape
    d = tokw + MEM_WIDTH
    tile = _pick(seq, 512, SUBLANE)

    def fn(dmo, dtok, qm, kv):
        k, v = kv[:, :MEM_WIDTH].astype(BF16), kv[:, MEM_WIDTH:].astype(BF16)
        dq = jnp.zeros(qm.shape, F32)
        dk = jnp.zeros(k.shape, F32)
        dv = jnp.zeros(v.shape, F32)
        for msk, qh in _mem_heads(qm):
            p = _mem_softmax(qh, k)
            doh = jnp.where(msk, dmo, 0.0).astype(BF16)
            dv = dv + _bdot(p, doh, "tn")
            dp = _bdot(doh, v, "nt")
            ds = (p * (dp - jnp.sum(dp * p, axis=-1, keepdims=True))).astype(BF16)
            dq = jnp.where(msk, _bdot(ds, k, "nn") * ATT_SCALE, dq)
            dk = dk + _bdot(ds, qh, "tn") * ATT_SCALE
        return (jnp.concatenate([dtok, dq], axis=1),), (jnp.concatenate([dk, dv], axis=1),)

    tpe = seq // tile
    kv_spec = pl.BlockSpec((None,) + memkv.shape[1:], lambda i: (i // tpe, 0, 0))
    return _rowwise(fn, [(dmixin, MEM_WIDTH, tokw // MEM_WIDTH), dtok, (proj, MEM_WIDTH, tokw // MEM_WIDTH)],
                    [(memkv, kv_spec)], [(d, F32)], [(memkv.shape, F32, kv_spec)],
                    rows=n, tile=tile, name=name, acc_period=tpe)


def _scale_bwd(dmixin, mixed, scale, *, name):
    n, tokw = mixed.shape

    def fn(dt, mixed, scale):
        return (dt * scale,), (jnp.sum(dt * mixed, axis=0, keepdims=True),)

    return _rowwise(fn, [(dmixin, tokw, 0), mixed], [scale], [(tokw, F32)], [((1, tokw), F32)],
                    rows=n, tile=_pick(n, 512, SUBLANE), name=name)


def _chunk_rows(seq):
    return _pick(seq, 512, SUBLANE)


def _load_ext(ref, c, rows, before, after, seq):
    lo, hi = c * rows - before, (c + 1) * rows + after
    parts = []
    if lo < 0:
        parts.append(jnp.zeros((-lo, ref.shape[1]), F32))
    parts.append(ref[max(lo, 0):min(hi, seq), :])
    if hi > seq:
        parts.append(jnp.zeros((hi - seq, ref.shape[1]), F32))
    return parts[0] if len(parts) == 1 else jnp.concatenate(parts, axis=0)


def _down(x, k):
    return pltpu.roll(x, k, 0)


def _up(x, k):
    return pltpu.roll(x, x.shape[0] - k, 0)


def _window_sums(ext, shift, col0, group):
    lane = col0 + lax.broadcasted_iota(jnp.int32, (1, ext.shape[1]), 1)
    gidx = lane // group
    s = ext
    out = None
    k = 1
    for gi, w in enumerate(POOL_WINDOWS):
        while k < w:
            s = s + shift(s, k)
            k *= 2
        out = s if out is None else jnp.where(gidx >= gi, s, out)
    return out, jnp.left_shift(2, jnp.minimum(gidx, len(POOL_WINDOWS) - 1))


def _pool_fwd(proj3, tokw, *, name):
    nb, seq, _ = proj3.shape
    rows = _chunk_rows(seq)
    group = tokw // len(POOL_WINDOWS)

    def body(u_ref, o_ref):
        col0 = pl.program_id(1) * LANE
        for c in range(seq // rows):
            ext = _load_ext(u_ref, c, rows, MAX_WINDOW, 0, seq)
            sums, win = _window_sums(ext, _down, col0, group)
            t = c * rows + lax.broadcasted_iota(jnp.int32, (rows, 1), 0)
            count = jnp.minimum(t + 1, win).astype(F32)
            o_ref[c * rows:(c + 1) * rows, :] = sums[MAX_WINDOW:, :] / count - ext[MAX_WINDOW:, :]

    spec = pl.BlockSpec((None, seq, LANE), lambda b, j: (b, 0, j))
    return pl.pallas_call(
        body, name=name, grid=(nb, tokw // LANE), in_specs=[spec], out_specs=spec,
        out_shape=jax.ShapeDtypeStruct((nb, seq, tokw), F32), compiler_params=_params(),
    )(proj3)


def _pool_bwd(dp3, *, name):
    nb, seq, tokw = dp3.shape
    rows = _chunk_rows(seq)
    group = tokw // len(POOL_WINDOWS)

    def body(d_ref, o_ref):
        col0 = pl.program_id(1) * LANE
        for c in range(seq // rows):
            ext = _load_ext(d_ref, c, rows, 0, MAX_WINDOW, seq)
            lane = col0 + lax.broadcasted_iota(jnp.int32, (1, LANE), 1)
            win = jnp.left_shift(2, jnp.minimum(lane // group, len(POOL_WINDOWS) - 1))
            t = c * rows + lax.broadcasted_iota(jnp.int32, (rows + MAX_WINDOW, 1), 0)
            scaled = ext / jnp.minimum(t + 1, win).astype(F32)
            sums, _ = _window_sums(scaled, _up, col0, group)
            o_ref[c * rows:(c + 1) * rows, :] = sums[:rows, :] - ext[:rows, :]

    spec = pl.BlockSpec((None, seq, LANE), lambda b, j: (b, 0, j))
    return pl.pallas_call(
        body, name=name, grid=(nb, tokw // LANE), in_specs=[spec], out_specs=spec,
        out_shape=jax.ShapeDtypeStruct((nb, seq, tokw), F32), compiler_params=_params(),
    )(dp3)


def _conv3(ext, w_ref, b_ref):
    x1, x2 = _down(ext, 1), _down(ext, 2)
    return w_ref[0:1, :] * x2 + w_ref[1:2, :] * x1 + w_ref[2:3, :] * ext + b_ref[...], x1, x2


def _convgate_fwd(up3, cw, cb, *, name):
    nb, seq, c2 = up3.shape
    fp = c2 // 2
    nblk = fp // LANE
    rows = _chunk_rows(seq)

    def body(u_ref, g_ref, wu_ref, wg_ref, bu_ref, bg_ref, o_ref):
        for c in range(seq // rows):
            hu, _, _ = _conv3(_load_ext(u_ref, c, rows, SUBLANE, 0, seq), wu_ref, bu_ref)
            hg, _, _ = _conv3(_load_ext(g_ref, c, rows, SUBLANE, 0, seq), wg_ref, bg_ref)
            o_ref[c * rows:(c + 1) * rows, :] = (hg * jax.nn.sigmoid(hg) * hu)[SUBLANE:, :]

    def col(off, r):
        return pl.BlockSpec((r, LANE), lambda b, j: (0, j + off))

    def act(off):
        return pl.BlockSpec((None, seq, LANE), lambda b, j: (b, 0, j + off))

    return pl.pallas_call(
        body, name=name, grid=(nb, nblk),
        in_specs=[act(0), act(nblk), col(0, SUBLANE), col(nblk, SUBLANE), col(0, 1), col(nblk, 1)],
        out_specs=act(0), out_shape=jax.ShapeDtypeStruct((nb, seq, fp), F32), compiler_params=_params(),
    )(up3, up3, cw, cw, cb, cb)


def _convgate_bwd(up3, dact3, cw, cb, *, name):
    nb, seq, c2 = up3.shape
    fp = c2 // 2
    nblk = fp // LANE
    rows = _chunk_rows(seq)
    h = SUBLANE

    def body(u_ref, g_ref, da_ref, wu_ref, wg_ref, bu_ref, bg_ref, du_ref, dg_ref, dwu_ref, dwg_ref, dbu_ref,
             dbg_ref):
        @pl.when(pl.program_id(1) == 0)
        def _():
            for r in (dwu_ref, dwg_ref, dbu_ref, dbg_ref):
                r[...] = jnp.zeros(r.shape, F32)

        for c in range(seq // rows):
            eu = _load_ext(u_ref, c, rows, h, h, seq)
            eg = _load_ext(g_ref, c, rows, h, h, seq)
            da = _load_ext(da_ref, c, rows, h, h, seq)
            hu, u1, u2 = _conv3(eu, wu_ref, bu_ref)
            hg, g1, g2 = _conv3(eg, wg_ref, bg_ref)
            sig = jax.nn.sigmoid(hg)
            dhu = da * hg * sig
            dhg = da * hu * sig * (1.0 + hg * (1.0 - sig))
            for dh, w_ref, x0, x1, x2, dx_ref, dw_ref, db_ref in (
                    (dhu, wu_ref, eu, u1, u2, du_ref, dwu_ref, dbu_ref),
                    (dhg, wg_ref, eg, g1, g2, dg_ref, dwg_ref, dbg_ref)):
                dx = w_ref[2:3, :] * dh + w_ref[1:2, :] * _up(dh, 1) + w_ref[0:1, :] * _up(dh, 2)
                dx_ref[c * rows:(c + 1) * rows, :] = dx[h:h + rows, :]
                core = dh[h:h + rows, :]
                for k, xk in ((0, x2), (1, x1), (2, x0)):
                    dw_ref[k:k + 1, :] += jnp.sum(core * xk[h:h + rows, :], axis=0, keepdims=True)
                db_ref[...] += jnp.sum(core, axis=0, keepdims=True)

    def col(off, r):
        return pl.BlockSpec((r, LANE), lambda j, b: (0, j + off))

    def act(off):
        return pl.BlockSpec((None, seq, LANE), lambda j, b: (b, 0, j + off))

    du, dg, dwu, dwg, dbu, dbg = pl.pallas_call(
        body, name=name, grid=(nblk, nb),
        in_specs=[act(0), act(nblk), act(0), col(0, SUBLANE), col(nblk, SUBLANE), col(0, 1), col(nblk, 1)],
        out_specs=[act(0), act(0), col(0, SUBLANE), col(0, SUBLANE), col(0, 1), col(0, 1)],
        out_shape=[jax.ShapeDtypeStruct((nb, seq, fp), F32), jax.ShapeDtypeStruct((nb, seq, fp), F32),
                   jax.ShapeDtypeStruct((SUBLANE, fp), F32), jax.ShapeDtypeStruct((SUBLANE, fp), F32),
                   jax.ShapeDtypeStruct((1, fp), F32), jax.ShapeDtypeStruct((1, fp), F32)],
        compiler_params=_params(),
    )(up3, up3, dact3, cw, cw, cb, cb)
    return du, dg, jnp.concatenate([dwu, dwg], axis=1), jnp.concatenate([dbu, dbg], axis=1)


def _scan_rows(x, shift, valid):
    row = lax.broadcasted_iota(jnp.int32, (x.shape[0], 1), 0)
    k = 1
    while k < x.shape[0]:
        x = x + jnp.where(valid(row, k), shift(x, k), 0.0)
        k *= 2
    return x


def _pick_row(x, r):
    row = lax.broadcasted_iota(jnp.int32, (x.shape[0], 1), 0)
    return jnp.sum(jnp.where(row == r, x, 0.0), axis=0, keepdims=True)


def _log_sigmoid(z):
    return jnp.minimum(z, 0.0) - jnp.log(1.0 + jnp.exp(-jnp.abs(z)))


def _gate_fwd(kvf3, fb, col_block, *, name):
    nb, seq, _ = kvf3.shape
    rows = _chunk_rows(seq)

    def body(f_ref, fb_ref, o_ref):
        carry = jnp.zeros((1, LANE), F32)
        for c in range(seq // rows):
            logf = _log_sigmoid(f_ref[c * rows:(c + 1) * rows, :] + fb_ref[...])
            run = _scan_rows(logf, _down, lambda row, k: row >= k) + carry
            o_ref[c * rows:(c + 1) * rows, :] = run
            carry = _pick_row(run, rows - 1)

    return pl.pallas_call(
        body, name=name, grid=(nb,),
        in_specs=[pl.BlockSpec((None, seq, LANE), lambda b: (b, 0, col_block)),
                  pl.BlockSpec((1, LANE), lambda b: (0, 0))],
        out_specs=pl.BlockSpec((None, seq, LANE), lambda b: (b, 0, 0)),
        out_shape=jax.ShapeDtypeStruct((nb, seq, LANE), F32), compiler_params=_params(),
    )(kvf3, fb)


def _gate_bwd(kvf3, fb, dF3, col_block, heads, *, name):
    nb, seq, _ = kvf3.shape
    rows = _chunk_rows(seq)

    def body(f_ref, fb_ref, d_ref, o_ref, dfb_ref):
        @pl.when(pl.program_id(0) == 0)
        def _():
            dfb_ref[...] = jnp.zeros(dfb_ref.shape, F32)

        lane = lax.broadcasted_iota(jnp.int32, (1, LANE), 1)
        carry = jnp.zeros((1, LANE), F32)
        for c in reversed(range(seq // rows)):
            run = _scan_rows(d_ref[c * rows:(c + 1) * rows, :], _up, lambda row, k: row < rows - k) + carry
            carry = _pick_row(run, 0)
            z = f_ref[c * rows:(c + 1) * rows, :] + fb_ref[...]
            df = jnp.where(lane < heads, run * jax.nn.sigmoid(-z), 0.0)
            o_ref[c * rows:(c + 1) * rows, :] = df
            dfb_ref[...] += jnp.sum(df, axis=0, keepdims=True)

    return pl.pallas_call(
        body, name=name, grid=(nb,),
        in_specs=[pl.BlockSpec((None, seq, LANE), lambda b: (b, 0, col_block)),
                  pl.BlockSpec((1, LANE), lambda b: (0, 0)),
                  pl.BlockSpec((None, seq, LANE), lambda b: (b, 0, 0))],
        out_specs=[pl.BlockSpec((None, seq, LANE), lambda b: (b, 0, 0)), pl.BlockSpec((1, LANE), lambda b: (0, 0))],
        out_shape=[jax.ShapeDtypeStruct((nb, seq, LANE), F32), jax.ShapeDtypeStruct((1, LANE), F32)],
        compiler_params=_params(),
    )(kvf3, fb, dF3)


def _head_masks():
    lane = lax.broadcasted_iota(jnp.int32, (1, LANE), 1)
    return (lane < HEAD_DIM, lane >= HEAD_DIM)


BIAS_TERMS = 3


def _bias_lanes(gsum):
    nb, seq, heads = gsum.shape
    terms, rest = [], gsum
    for _ in range(BIAS_TERMS):
        t = lax.reduce_precision(rest, exponent_bits=8, mantissa_bits=7)
        terms.append(t)
        rest = rest - t
    ones = [jnp.ones_like(gsum)] * BIAS_TERMS

    def lanes(parts):
        z = jnp.stack(parts, axis=-1)
        z = jnp.pad(z, ((0, 0), (0, 0), (0, 0), (0, HEAD_DIM - 2 * BIAS_TERMS)))
        z = z.reshape(nb, seq, heads // 2, 2, HEAD_DIM)[:, :, :, ::-1]
        return z.reshape(nb, seq, heads * HEAD_DIM).astype(BF16)

    return lanes(terms + ones), lanes(ones + [-t for t in terms])


def _fox_scores(q, k, aq, ak, masked):
    qs = (q * ATT_SCALE).astype(BF16)
    qts = [jnp.where(msk, qs, aq) for msk in _head_masks()]
    ss = [_bdot(qt, jnp.where(msk, k, ak), "nt") for qt, msk in zip(qts, _head_masks())]
    if masked:
        t = q.shape[0]
        keep = lax.broadcasted_iota(jnp.int32, (t, t), 0) >= lax.broadcasted_iota(jnp.int32, (t, t), 1)
        ss = [jnp.where(keep, s, NEG_BIG) for s in ss]
    return ss, qts


def _on_blocks(qi, ki, step):
    @pl.when(ki < qi)
    def _():
        step(False)

    @pl.when(ki == qi)
    def _():
        step(True)


def _fox_grid(nblk, tokw, t, q_major):
    if q_major:
        pairs = [(qi, ki) for qi in range(nblk) for ki in range(qi + 1)]
    else:
        pairs = [(qi, ki) for ki in range(nblk) for qi in range(ki, nblk)]
    tables = [jnp.array([p[i] for p in pairs], jnp.int32) for i in (0, 1)]

    def q_spec(off=0, wide=False):
        width = 2 * LANE if wide else LANE
        return pl.BlockSpec((None, t, width), lambda b, p, i, qt, kt: (b, qt[i], p + off))

    def k_spec(off=0):
        return pl.BlockSpec((None, t, LANE), lambda b, p, i, qt, kt: (b, kt[i], p + off))

    return tables, len(pairs), q_spec, k_spec, tokw // LANE


def _lanes(col):
    return jnp.broadcast_to(col, (col.shape[0], LANE))


def _across(stat, width):
    return jnp.tile(stat, (1, width // LANE))


def _fox_fwd(proj3, kvf3, aq3, ak3, tokw, *, name):
    nb, seq, _ = proj3.shape
    t = _pick(seq, 512, LANE)
    tables, n_pairs, q_spec, k_spec, hp0 = _fox_grid(seq // t, tokw, t, True)

    def body(qt_ref, kt_ref, q_ref, k_ref, v_ref, aq_ref, ak_ref, o_ref, lse_ref, m_s, l_s, acc_s):
        i = pl.program_id(2)
        qi, ki = qt_ref[i], kt_ref[i]

        @pl.when(ki == 0)
        def _():
            m_s[...] = jnp.full(m_s.shape, NEG_BIG, F32)
            l_s[...] = jnp.zeros(l_s.shape, F32)
            acc_s[...] = jnp.zeros(acc_s.shape, F32)

        def step(masked):
            v = v_ref[...].astype(BF16)
            ss, _ = _fox_scores(q_ref[...], k_ref[...].astype(BF16), aq_ref[...], ak_ref[...], masked)
            for h, s in enumerate(ss):
                m_old = m_s[h]
                m_new = jnp.maximum(m_old, _lanes(jnp.max(s, axis=-1, keepdims=True)))
                alpha = jnp.exp(m_old - m_new)
                p = jnp.exp(s - _across(m_new, t))
                l_s[h] = alpha * l_s[h] + _lanes(jnp.sum(p, axis=-1, keepdims=True))
                acc_s[h] = alpha * acc_s[h] + _bdot(p, v, "nn")
                m_s[h] = m_new

        _on_blocks(qi, ki, step)

        @pl.when(ki == qi)
        def _():
            o_ref[...] = jnp.where(_head_masks()[0], acc_s[0] / l_s[0], acc_s[1] / l_s[1])
            lse_ref[...] = jnp.concatenate([m_s[0] + jnp.log(l_s[0]), m_s[1] + jnp.log(l_s[1])], axis=1)

    stat = pltpu.VMEM((2, t, LANE), F32)
    return pl.pallas_call(
        body, name=name,
        grid_spec=pltpu.PrefetchScalarGridSpec(
            num_scalar_prefetch=2, grid=(nb, hp0, n_pairs),
            in_specs=[q_spec(), k_spec(), k_spec(hp0), q_spec(), k_spec()],
            out_specs=[q_spec(), q_spec(wide=True)], scratch_shapes=[stat, stat, stat]),
        out_shape=[jax.ShapeDtypeStruct((nb, seq, tokw), F32), jax.ShapeDtypeStruct((nb, seq, 2 * tokw), F32)],
        compiler_params=_params(),
    )(*tables, proj3, kvf3, kvf3, aq3, ak3)


def _fox_bwd_common(q_ref, k_ref, v_ref, aq_ref, ak_ref, do_ref, lse_ref, delta_ref, masked):
    k, v = k_ref[...].astype(BF16), v_ref[...].astype(BF16)
    ss, qts = _fox_scores(q_ref[...], k, aq_ref[...], ak_ref[...], masked)
    do = do_ref[...]
    t = do.shape[0]
    out = []
    for h, (s, qt, msk) in enumerate(zip(ss, qts, _head_masks())):
        doh = jnp.where(msk, do, 0.0).astype(BF16)
        p = jnp.exp(s - _across(lse_ref[:, h * LANE:(h + 1) * LANE], t))
        ds = p * (_bdot(doh, v, "nt") - _across(delta_ref[:, h * LANE:(h + 1) * LANE], t))
        out.append((qt, doh, p, ds))
    return out, k


def _fox_bwd_dq(proj3, kvf3, aq3, ak3, o3, dmixin3, lse3, tokw, *, name):
    nb, seq, _ = proj3.shape
    t = _pick(seq, 512, LANE)
    tables, n_pairs, q_spec, k_spec, hp0 = _fox_grid(seq // t, tokw, t, True)

    def body(qt_ref, kt_ref, q_ref, k_ref, v_ref, aq_ref, ak_ref, o_ref, do_ref, lse_ref, dq_ref, delta_ref,
             dfq_ref, acc_s, row_s):
        i = pl.program_id(2)
        qi, ki = qt_ref[i], kt_ref[i]

        @pl.when(ki == 0)
        def _():
            acc_s[...] = jnp.zeros(acc_s.shape, F32)
            row_s[...] = jnp.zeros(row_s.shape, F32)
            prod = do_ref[...] * o_ref[...]
            delta_ref[...] = jnp.concatenate(
                [_lanes(jnp.sum(jnp.where(msk, prod, 0.0), axis=-1, keepdims=True)) for msk in _head_masks()],
                axis=1)

        def step(masked):
            heads, k = _fox_bwd_common(q_ref, k_ref, v_ref, aq_ref, ak_ref, do_ref, lse_ref, delta_ref, masked)
            for h, (_, _, _, ds) in enumerate(heads):
                acc_s[h] += _bdot(ds, k, "nn")
                row_s[h] += _lanes(jnp.sum(ds, axis=-1, keepdims=True))

        _on_blocks(qi, ki, step)

        @pl.when(ki == qi)
        def _():
            dq_ref[...] = jnp.where(_head_masks()[0], acc_s[0], acc_s[1]) * ATT_SCALE
            dfq_ref[...] = jnp.concatenate([row_s[0], row_s[1]], axis=1)

    wide = jax.ShapeDtypeStruct((nb, seq, 2 * tokw), F32)
    stat = pltpu.VMEM((2, t, LANE), F32)
    return pl.pallas_call(
        body, name=name,
        grid_spec=pltpu.PrefetchScalarGridSpec(
            num_scalar_prefetch=2, grid=(nb, hp0, n_pairs),
            in_specs=[q_spec(), k_spec(), k_spec(hp0), q_spec(), k_spec(), q_spec(), q_spec(), q_spec(wide=True)],
            out_specs=[q_spec(), q_spec(wide=True), q_spec(wide=True)], scratch_shapes=[stat, stat]),
        out_shape=[jax.ShapeDtypeStruct((nb, seq, tokw), F32), wide, wide], compiler_params=_params(),
    )(*tables, proj3, kvf3, kvf3, aq3, ak3, o3, dmixin3, lse3)


def _fox_bwd_dkv(proj3, kvf3, aq3, ak3, dmixin3, lse3, delta3, tokw, *, name):
    nb, seq, _ = proj3.shape
    t = _pick(seq, 512, LANE)
    nblk = seq // t
    tables, n_pairs, q_spec, k_spec, hp0 = _fox_grid(nblk, tokw, t, False)
    dfk_spec = pl.BlockSpec((None, None, SUBLANE, t), lambda b, p, i, qt, kt: (b, p, 0, kt[i]))

    def body(qt_ref, kt_ref, q_ref, k_ref, v_ref, aq_ref, ak_ref, do_ref, lse_ref, delta_ref, dk_ref, dv_ref,
             dfk_ref, dk_s, dv_s, dfk_s):
        i = pl.program_id(2)
        qi, ki = qt_ref[i], kt_ref[i]

        @pl.when(qi == ki)
        def _():
            dk_s[...] = jnp.zeros(dk_s.shape, F32)
            dv_s[...] = jnp.zeros(dv_s.shape, F32)
            dfk_s[...] = jnp.zeros(dfk_s.shape, F32)

        def step(masked):
            heads, _ = _fox_bwd_common(q_ref, k_ref, v_ref, aq_ref, ak_ref, do_ref, lse_ref, delta_ref, masked)
            for h, ((qt, doh, p, ds), msk) in enumerate(zip(heads, _head_masks())):
                dv_s[...] += _bdot(p, doh, "tn")
                dk_s[...] += jnp.where(msk, _bdot(ds, qt, "tn"), 0.0)
                dfk_s[h:h + 1, :] -= jnp.sum(ds, axis=0, keepdims=True)

        _on_blocks(qi, ki, step)

        @pl.when(qi == nblk - 1)
        def _():
            dk_ref[...] = dk_s[...]
            dv_ref[...] = dv_s[...]
            dfk_ref[...] = dfk_s[...]

    out = jax.ShapeDtypeStruct((nb, seq, tokw), F32)
    return pl.pallas_call(
        body, name=name,
        grid_spec=pltpu.PrefetchScalarGridSpec(
            num_scalar_prefetch=2, grid=(nb, hp0, n_pairs),
            in_specs=[q_spec(), k_spec(), k_spec(hp0), q_spec(), k_spec(), q_spec(), q_spec(wide=True),
                      q_spec(wide=True)],
            out_specs=[k_spec(), k_spec(), dfk_spec],
            scratch_shapes=[pltpu.VMEM((t, LANE), F32), pltpu.VMEM((t, LANE), F32), pltpu.VMEM((SUBLANE, t), F32)]),
        out_shape=[out, out, jax.ShapeDtypeStruct((nb, hp0, SUBLANE, seq), F32)], compiler_params=_params(),
    )(*tables, proj3, kvf3, kvf3, aq3, ak3, dmixin3, lse3, delta3)


def _peer(k):
    x, y, c = lax.axis_index("x"), lax.axis_index("y"), lax.axis_index("c")
    return (1 - x if k & 4 else x, 1 - y if k & 2 else y, 1 - c if k & 1 else c)


def _dev_index(p):
    return 4 * p[0] + 2 * p[1] + p[2]


def _exchange(send, *, gather, name):
    block = send.shape[-2:]

    def body(s_ref, o_ref, send_sems, recv_sems, local_sem):
        me = _peer(0)
        mine = pltpu.make_async_copy(s_ref if gather else s_ref.at[_dev_index(me)], o_ref.at[_dev_index(me)],
                                     local_sem)
        mine.start()
        sends, recvs = [], []
        for k in range(1, N_DEV):
            peer = _peer(k)
            src = s_ref if gather else s_ref.at[_dev_index(peer)]
            sends.append(pltpu.make_async_remote_copy(
                src_ref=src, dst_ref=o_ref.at[_dev_index(me)], send_sem=send_sems.at[k - 1],
                recv_sem=recv_sems.at[k - 1], device_id=peer, device_id_type=MESH_T))
            recvs.append(pltpu.make_async_remote_copy(
                src_ref=src, dst_ref=o_ref.at[_dev_index(peer)], send_sem=send_sems.at[k - 1],
                recv_sem=recv_sems.at[k - 1], device_id=peer, device_id_type=MESH_T))
        for cp in sends:
            cp.start()
        for cp in recvs:
            cp.wait_recv()
        for cp in sends:
            cp.wait_send()
        mine.wait()

    return pl.pallas_call(
        body, name=name,
        in_specs=[pl.BlockSpec(memory_space=pltpu.HBM)], out_specs=pl.BlockSpec(memory_space=pltpu.HBM),
        out_shape=jax.ShapeDtypeStruct((N_DEV,) + block, send.dtype),
        scratch_shapes=[pltpu.SemaphoreType.DMA((N_DEV - 1,)), pltpu.SemaphoreType.DMA((N_DEV - 1,)),
                        pltpu.SemaphoreType.DMA],
    )(send)


_HBM = pl.BlockSpec(memory_space=pltpu.HBM)
CHIP_RELATIONS = (2, 4, 6)


def _chip_index(p):
    return 2 * p[0] + p[1]


def _run_copies(sends, recvs):
    for cp in sends:
        cp.start()
    for cp in recvs:
        cp.wait_recv()
    for cp in sends:
        cp.wait_send()


def _gather_shards(whole, halved, *, name):
    nw, nh = len(whole), len(halved)
    n = nw + nh
    n_sem = 3 * nw + 6 * nh

    def body(*refs):
        ins, outs, send_sems, recv_sems = refs[:n], refs[n:2 * n], refs[2 * n], refs[2 * n + 1]
        me, sib = _peer(0), _peer(1)
        q, c = _chip_index(me), me[2]

        def copy(src, dst, s, to):
            return pltpu.make_async_remote_copy(src_ref=src, dst_ref=dst, send_sem=send_sems.at[s],
                                                recv_sem=recv_sems.at[s], device_id=to, device_id_type=MESH_T)

        sends, recvs, passes = [], [], []
        for j, k in enumerate(CHIP_RELATIONS):
            peer = _peer(k)
            pq = _chip_index(peer)
            for i in range(nw):
                sends.append(copy(ins[i], outs[i].at[q], 3 * i + j, peer))
                recvs.append(copy(ins[i], outs[i].at[pq], 3 * i + j, peer))
            for i in range(nh):
                src, out, s = ins[nw + i], outs[nw + i], 3 * nw + 6 * i + j
                sends.append(copy(src.at[c], out.at[q, c], s, peer))
                passes.append((copy(src.at[c], out.at[pq, c], s, peer), copy(out.at[pq, c], out.at[pq, c], s + 3, sib),
                               copy(out.at[pq, c], out.at[pq, 1 - c], s + 3, sib)))
        for cp in sends:
            cp.start()
        for arrival, hand_over, _ in passes:
            arrival.wait_recv()
            hand_over.start()
        for cp in recvs:
            cp.wait_recv()
        for _, _, from_sibling in passes:
            from_sibling.wait_recv()
        for cp in sends + [hand_over for _, hand_over, _ in passes]:
            cp.wait_send()

    arrays = list(whole) + list(halved)
    return pl.pallas_call(
        body, name=name, in_specs=[_HBM] * n, out_specs=[_HBM] * n,
        out_shape=[jax.ShapeDtypeStruct((N_CHIP,) + a.shape, a.dtype) for a in arrays],
        scratch_shapes=[pltpu.SemaphoreType.DMA((n_sem,)), pltpu.SemaphoreType.DMA((n_sem,))],
    )(*arrays)


def _to_sibling(grads, *, name):
    n = len(grads)

    def body(*refs):
        ins, outs, send_sems, recv_sems = refs[:n], refs[n:2 * n], refs[2 * n], refs[2 * n + 1]
        c = lax.axis_index("c")
        sends = [pltpu.make_async_remote_copy(src_ref=ins[i].at[:, 1 - c], dst_ref=outs[i], send_sem=send_sems.at[i],
                                              recv_sem=recv_sems.at[i], device_id=_peer(1), device_id_type=MESH_T)
                 for i in range(n)]
        _run_copies(sends, sends)

    return pl.pallas_call(
        body, name=name, in_specs=[_HBM] * n, out_specs=[_HBM] * n,
        out_shape=[jax.ShapeDtypeStruct(g.shape[:1] + g.shape[2:], g.dtype) for g in grads],
        scratch_shapes=[pltpu.SemaphoreType.DMA((n,)), pltpu.SemaphoreType.DMA((n,))],
    )(*grads)


def _pair_add(grads, from_sibling, qc, *, name):
    _, _, rows, cols = grads.shape
    tile = _pick(rows, 128, SUBLANE)

    def body(qc_ref, g_ref, s_ref, o_ref):
        del qc_ref
        o_ref[...] = g_ref[...] + s_ref[...]

    spec = pl.BlockSpec((None, tile, cols), lambda j, i, qc: (j, i, 0))
    return pl.pallas_call(
        body, name=name,
        grid_spec=pltpu.PrefetchScalarGridSpec(
            num_scalar_prefetch=1, grid=(N_CHIP, rows // tile),
            in_specs=[pl.BlockSpec((None, None, tile, cols), lambda j, i, qc: (j, qc[1], i, 0)), spec],
            out_specs=spec),
        out_shape=jax.ShapeDtypeStruct((N_CHIP, rows, cols), F32), compiler_params=_params(),
    )(qc, grads, from_sibling)


def _to_chips(sums, *, name):
    n = len(sums)

    def body(*refs):
        ins, outs, send_sems, recv_sems = refs[:n], refs[n:2 * n], refs[2 * n], refs[2 * n + 1]
        sends, recvs = [], []
        for j, k in enumerate(CHIP_RELATIONS):
            peer = _peer(k)
            for i in range(n):
                sem = dict(send_sem=send_sems.at[3 * i + j], recv_sem=recv_sems.at[3 * i + j], device_id=peer,
                           device_id_type=MESH_T)
                src = ins[i].at[_chip_index(peer)]
                sends.append(pltpu.make_async_remote_copy(src_ref=src, dst_ref=outs[i].at[j], **sem))
                recvs.append(pltpu.make_async_remote_copy(src_ref=src, dst_ref=outs[i].at[j], **sem))
        _run_copies(sends, recvs)

    return pl.pallas_call(
        body, name=name, in_specs=[_HBM] * n, out_specs=[_HBM] * n,
        out_shape=[jax.ShapeDtypeStruct((3,) + g.shape[1:], g.dtype) for g in sums],
        scratch_shapes=[pltpu.SemaphoreType.DMA((3 * n,)), pltpu.SemaphoreType.DMA((3 * n,))],
    )(*sums)


def _swap_halves(arrays, *, name):
    n = len(arrays)

    def body(*refs):
        outs, send_sems, recv_sems = refs[n:2 * n], refs[2 * n], refs[2 * n + 1]
        c = lax.axis_index("c")
        sib = _peer(1)
        sends, recvs = [], []
        for i in range(n):
            sem = dict(send_sem=send_sems.at[i], recv_sem=recv_sems.at[i], device_id=sib, device_id_type=MESH_T)
            sends.append(pltpu.make_async_remote_copy(src_ref=outs[i].at[c], dst_ref=outs[i].at[c], **sem))
            recvs.append(pltpu.make_async_remote_copy(src_ref=outs[i].at[c], dst_ref=outs[i].at[1 - c], **sem))
        _run_copies(sends, recvs)

    return pl.pallas_call(
        body, name=name, in_specs=[_HBM] * n, out_specs=[_HBM] * n,
        out_shape=[jax.ShapeDtypeStruct(a.shape, a.dtype) for a in arrays],
        input_output_aliases={i: i for i in range(n)},
        scratch_shapes=[pltpu.SemaphoreType.DMA((n,)), pltpu.SemaphoreType.DMA((n,))],
    )(*arrays)


def _adam_math(g, w, m, v):
    bc1 = 1.0 - ADAM_B1 ** ADAM_STEP
    bc2 = 1.0 - ADAM_B2 ** ADAM_STEP
    m_new = ADAM_B1 * m + (1.0 - ADAM_B1) * g
    v_new = ADAM_B2 * v + (1.0 - ADAM_B2) * (g * g)
    delta = -ADAM_LR * ((m_new / bc1) / (jnp.sqrt(v_new / bc2) + ADAM_EPS) + ADAM_WD * w)
    return delta, m_new, v_new


def _adamw_half(sums, parts, w, m, v, qc, *, name):
    _, rows, cols = sums.shape
    tile = _pick(rows, 64, SUBLANE)
    n_parts = parts.shape[0]

    def body(qc_ref, g_ref, p_ref, w_ref, m_ref, v_ref, go_ref, do_ref, mo_ref, vo_ref):
        del qc_ref
        g = g_ref[...]
        for k in range(n_parts):
            g = g + p_ref[k]
        delta, m_new, v_new = _adam_math(g, w_ref[...], m_ref[...], v_ref[...])
        go_ref[...] = g
        do_ref[...] = delta
        mo_ref[...] = m_new
        vo_ref[...] = v_new

    half = pl.BlockSpec((None, tile, cols), lambda i, qc: (qc[1], i, 0))
    shape = jax.ShapeDtypeStruct((2, rows, cols), F32)
    return pl.pallas_call(
        body, name=name,
        grid_spec=pltpu.PrefetchScalarGridSpec(
            num_scalar_prefetch=1, grid=(rows // tile,),
            in_specs=[pl.BlockSpec((None, tile, cols), lambda i, qc: (qc[0], i, 0)),
                      pl.BlockSpec((n_parts, tile, cols), lambda i, qc: (0, i, 0)), half, half, half],
            out_specs=[half] * 4),
        out_shape=[shape] * 4, compiler_params=_params(),
    )(qc, sums, parts, w, m, v)


def _adamw(parts, w, m, v, *, name):
    _, rows, cols = parts.shape
    tile = _pick(rows, 256, SUBLANE)

    def body(p_ref, w_ref, m_ref, v_ref, o_ref):
        g = p_ref[0]
        for i in range(1, N_DEV):
            g = g + p_ref[i]
        delta, m_new, v_new = _adam_math(g, w_ref[...], m_ref[...], v_ref[...])
        o_ref[0] = g
        o_ref[1] = delta
        o_ref[2] = m_new
        o_ref[3] = v_new

    spec = pl.BlockSpec((tile, cols), lambda i: (i, 0))
    return pl.pallas_call(
        body, name=name, grid=(rows // tile,),
        in_specs=[pl.BlockSpec((N_DEV, tile, cols), lambda i: (0, i, 0)), spec, spec, spec],
        out_specs=pl.BlockSpec((4, tile, cols), lambda i: (0, i, 0)),
        out_shape=jax.ShapeDtypeStruct((4, rows, cols), F32), compiler_params=_params(),
    )(parts, w, m, v)


def _layout(shapes, names, align):
    out, off = [], 0
    for n in names:
        size = math.prod(shapes[n])
        out.append((n, tuple(shapes[n]), off, size))
        off += _round_up(size, align)
    return out, off


def _pack(arrays, layout, total, lead=()):
    parts = []
    for i, (n, _, off, size) in enumerate(layout):
        end = layout[i + 1][2] if i + 1 < len(layout) else total
        flat = arrays[n].reshape(lead + (size,))
        if end - off > size:
            flat = jnp.pad(flat, [(0, 0)] * len(lead) + [(0, end - off - size)])
        parts.append(flat)
    return jnp.concatenate(parts, axis=len(lead))


def _unpack(flat, layout, lead=()):
    return {n: flat[..., off:off + size].reshape(lead + shape) for n, shape, off, size in layout}


def _to_shards(full, axis):
    shp = full.shape
    return jnp.moveaxis(full.reshape(shp[:axis] + (N_CHIP, shp[axis] // N_CHIP) + shp[axis + 1:]), axis, 0)


def _from_shards(shards, axis):
    x = jnp.moveaxis(shards, 0, axis)
    shp = x.shape
    return x.reshape(shp[:axis] + (shp[axis] * shp[axis + 1],) + shp[axis + 2:])


def _pad_cols(w, per, padded):
    lead = w.shape[:-1]
    x = w.reshape(lead + (-1, per))
    x = jnp.pad(x, [(0, 0)] * len(lead) + [(0, 0), (0, padded - per)])
    return x.reshape(lead + (-1,))


def _unpad_cols(w, per, padded):
    lead = w.shape[:-1]
    return w.reshape(lead + (-1, padded))[..., :per].reshape(lead + (-1,))


def _local_step(x, mem, target, W):
    nb, seq, d = x.shape
    n = nb * seq
    tokw = d - MEM_WIDTH
    heads = tokw // HEAD_DIM
    mlen = mem.shape[1]
    dff2 = W["ffn_w_up"].shape[-1]
    per = dff2 // N_CHIP
    per_p = _round_up(per, LANE)
    fp = 2 * per_p
    kvw = 2 * tokw + heads
    kvp = 2 * tokw + LANE
    gate_block = 2 * tokw // LANE

    x2d = x.reshape(n, d)
    mem2d = mem.reshape(nb * mlen, d)
    t2d = target.reshape(n, d)
    row = lambda a: a.reshape(1, -1)
    ones_tok = jnp.ones((1, tokw), F32)

    pool_bd = jax.scipy.linalg.block_diag(*[W["a_pool_w"][0, i] for i in range(len(POOL_WINDOWS))]).astype(BF16)
    kv_w = jnp.pad(W["kv_w"], ((0, 0), (0, kvp - kvw)))
    fb = jnp.pad(W["f_b"], (0, LANE - heads)).reshape(1, LANE)
    w_up = [_pad_cols(W["ffn_w_up"][l], per, per_p) for l in range(DEPTH)]
    w_down = [jnp.pad(W["ffn_w_down"][l].reshape(2, per, d), ((0, 0), (0, per_p - per), (0, 0))).reshape(fp, d)
              for l in range(DEPTH)]
    conv_w = [jnp.pad(_pad_cols(W["ffn_conv_w"][l], per, per_p), ((0, SUBLANE - CONV_WIDTH), (0, 0)))
              for l in range(DEPTH)]
    conv_b = [_pad_cols(W["ffn_conv_b"][l], per, per_p).reshape(1, 2 * fp) for l in range(DEPTH)]
    w_in = [W["a_w_in"][0], W["b_w_q"][0]]
    w_out = [W["a_w_out"][0], W["b_w_out"][0]]

    saved = []
    cur = x2d
    for l in range(DEPTH):
        s = {"x_in": cur}
        memkv = _mm(mem2d, W["mem_w_kv"][l], "nn", name=f"memkv{l}").reshape(nb, mlen, 2 * MEM_WIDTH)
        if l == 0:
            proj = _mm(cur, w_in[l], "nn", name="proj0")
            pooled = _pool_fwd(proj.reshape(nb, seq, d), tokw, name="pool_fwd").reshape(n, tokw)
            tok = _mm(pooled, pool_bd, "nn", name="pool_mix")
            scale = W["a_pool_scale"].reshape(1, tokw)
            s.update(pooled=pooled, mixed=tok, scale=scale)
        else:
            kvf = _mm(cur, kv_w, "nn", tn=kvp, name="kvf")
            kvf3 = kvf.reshape(nb, seq, kvp)
            gsum = _gate_fwd(kvf3, fb, gate_block, name="gate_fwd")[:, :, :heads]
            aq3, ak3 = _bias_lanes(gsum)
            proj = _mm(cur, w_in[l], "nn", name="proj1")
            o3, lse3 = _fox_fwd(proj.reshape(nb, seq, d), kvf3, aq3, ak3, tokw, name="fox_fwd")
            tok = o3.reshape(n, tokw)
            scale = ones_tok
            s.update(kvf3=kvf3, aq3=aq3, ak3=ak3, o3=o3, lse3=lse3)
        mixin = _memattn_fwd(tok, proj, memkv, scale, seq=seq, name=f"memattn_fwd{l}")
        mix = _mm(mixin, w_out[l], "nn", name=f"mix{l}")
        x1 = _ln_fwd(cur, mix, row(W["ln1_g"][l]), row(W["ln1_b"][l]), name=f"ln1_fwd{l}")
        up = _mm(x1, w_up[l], "nn", tn=per_p, name=f"ffn_up{l}")
        act = _convgate_fwd(up.reshape(nb, seq, 2 * fp), conv_w[l], conv_b[l], name=f"convgate_fwd{l}")
        act = act.reshape(n, fp)
        ffn = _mm(act, w_down[l], "nn", tk=fp, name=f"ffn_down{l}")
        s.update(proj=proj, memkv=memkv, mixin=mixin, mix=mix, x1=x1, up=up, act=act, ffn=ffn)
        saved.append(s)
        if l + 1 < DEPTH:
            cur = _ln_fwd(x1, ffn, row(W["ln2_g"][l]), row(W["ln2_b"][l]), name=f"ln2_fwd{l}")

    G = {}
    ln_g = {k: [None] * DEPTH for k in ("ln1_g", "ln1_b", "ln2_g", "ln2_b")}
    stack = {k: [None] * DEPTH for k in ("mem_w_kv", "ffn_w_up", "ffn_conv_w", "ffn_conv_b", "ffn_w_down")}
    dx_terms = None
    loss = None
    for l in reversed(range(DEPTH)):
        s = saved[l]
        g2 = row(W["ln2_g"][l])
        if l == DEPTH - 1:
            dres, dffn, loss, dg, db = _final_ln_loss(s["x1"], s["ffn"], t2d, g2, row(W["ln2_b"][l]),
                                                      name="final_ln_loss")
        else:
            dres, dffn, dg, db = _ln_bwd(s["x1"], s["ffn"], g2, dx_terms, name=f"ln2_bwd{l}")
        ln_g["ln2_g"][l], ln_g["ln2_b"][l] = dg[0], db[0]
        dact = _mm(dffn, w_down[l], "nt", tn=fp, name=f"ffn_down_dx{l}")
        stack["ffn_w_down"][l] = _mm(s["act"], dffn, "tn", tm=per_p, tk=512, name=f"ffn_down_dw{l}")
        du3, dg3, dcw, dcb = _convgate_bwd(s["up"].reshape(nb, seq, 2 * fp), dact.reshape(nb, seq, fp), conv_w[l],
                                           conv_b[l], name=f"convgate_bwd{l}")
        du, dgt = du3.reshape(n, fp), dg3.reshape(n, fp)
        dx1_u = _mm(du, w_up[l], "nt", tk=per_p, name=f"ffn_up_dx_u{l}")
        dx1_ffn = _mm(dgt, w_up[l], "nt", tk=per_p, b_col0=fp, adds=[dx1_u], name=f"ffn_up_dx_g{l}")
        stack["ffn_w_up"][l] = [_mm(s["x1"], part, "tn", tm=d, tn=per_p, tk=512, name=f"ffn_up_dw_{nm}{l}")
                                for nm, part in (("u", du), ("g", dgt))]
        stack["ffn_conv_w"][l] = dcw[:CONV_WIDTH]
        stack["ffn_conv_b"][l] = dcb[0]
        dres1, dmix, dg, db = _ln_bwd(s["x_in"], s["mix"], row(W["ln1_g"][l]), [dres, dx1_ffn], name=f"ln1_bwd{l}")
        ln_g["ln1_g"][l], ln_g["ln1_b"][l] = dg[0], db[0]
        dmixin = _mm(dmix, w_out[l], "nt", name=f"mix_dx{l}")
        d_w_out = _mm(s["mixin"], dmix, "tn", tm=d, tk=512, name=f"mix_dw{l}")
        if l == 0:
            G["a_w_out"] = d_w_out[None]
            dmixed, dscale = _scale_bwd(dmixin, s["mixed"], s["scale"], name="scale_bwd")
            G["a_pool_scale"] = dscale
            dpooled = _mm(dmixed, pool_bd, "nt", name="pool_mix_dx")
            dpw = _mm(s["pooled"], dmixed, "tn", tm=tokw, tk=512, name="pool_mix_dw")
            grp = tokw // len(POOL_WINDOWS)
            G["a_pool_w"] = jnp.stack([dpw[i * grp:(i + 1) * grp, i * grp:(i + 1) * grp]
                                       for i in range(len(POOL_WINDOWS))])[None]
            dtok = _pool_bwd(dpooled.reshape(nb, seq, tokw), name="pool_bwd").reshape(n, tokw)
            extra = []
        else:
            G["b_w_out"] = d_w_out[None]
            p3 = s["proj"].reshape(nb, seq, d)
            dm3 = dmixin.reshape(nb, seq, d)
            dq3, delta3, dfq3 = _fox_bwd_dq(p3, s["kvf3"], s["aq3"], s["ak3"], s["o3"], dm3, s["lse3"], tokw,
                                            name="fox_bwd_dq")
            dtok = dq3.reshape(n, tokw)
            dk3, dv3, dfk = _fox_bwd_dkv(p3, s["kvf3"], s["aq3"], s["ak3"], dm3, s["lse3"], delta3, tokw,
                                         name="fox_bwd_dkv")
            dgsum = jnp.swapaxes(dfk[:, :, :2, :].reshape(nb, heads, seq), 1, 2) + dfq3[:, :, ::LANE]
            dgsum = jnp.pad(dgsum, ((0, 0), (0, 0), (0, LANE - heads)))
            df3, dfb = _gate_bwd(s["kvf3"], fb, dgsum, gate_block, heads, name="gate_bwd")
            G["f_b"] = dfb[0, :heads]
            dkvf = [(dk3.reshape(n, tokw), 0, "k"), (dv3.reshape(n, tokw), tokw, "v"),
                    (df3.reshape(n, LANE), 2 * tokw, "f")]
            dx_kv = []
            for part, col0, nm in dkvf:
                dx_kv = [_mm(part, kv_w, "nt", b_col0=col0, adds=dx_kv, name=f"kvf_dx_{nm}")]
            extra = dx_kv
            G["kv_w"] = jnp.concatenate([_mm(s["x_in"], part, "tn", tm=d, tk=512, name=f"kvf_dw_{nm}")
                                         for part, _, nm in dkvf], axis=1)[:, :kvw]
        dproj, dmemkv = _memattn_bwd(dmixin, dtok, s["proj"], s["memkv"], seq=seq, name=f"memattn_bwd{l}")
        stack["mem_w_kv"][l] = _mm(mem2d, dmemkv.reshape(nb * mlen, 2 * MEM_WIDTH), "tn", tm=d, tk=512,
                                   name=f"memkv_dw{l}")
        G["a_w_in" if l == 0 else "b_w_q"] = _mm(s["x_in"], dproj, "tn", tm=d, tk=512, name=f"proj_dw{l}")[None]
        if l == 0:
            grad_x = _mm(dproj, w_in[l], "nt", adds=[dres1], name="proj_dx0")
        else:
            dx_terms = [dres1, _mm(dproj, w_in[l], "nt", name="proj_dx1")] + extra
    for k, v in ln_g.items():
        G[k] = jnp.stack(v)
    G["mem_w_kv"] = jnp.stack(stack["mem_w_kv"])
    G["ffn_w_up"] = jnp.stack([jnp.concatenate([_unpad_cols(g, per, per_p) for g in halves], axis=1)
                               for halves in stack["ffn_w_up"]])
    G["ffn_conv_w"] = jnp.stack([_unpad_cols(g, per, per_p) for g in stack["ffn_conv_w"]])
    G["ffn_conv_b"] = jnp.stack([_unpad_cols(g, per, per_p) for g in stack["ffn_conv_b"]])
    G["ffn_w_down"] = jnp.stack([g.reshape(2, per_p, d)[:, :per].reshape(2 * per, d) for g in stack["ffn_w_down"]])
    return loss[0, 0], grad_x.reshape(nb, seq, d), G


def kernel(x, mem, a_w_in, a_pool_w, a_pool_scale, a_w_out, b_w_q, b_w_out, kv_w, f_b, mem_w_kv, ln1_g, ln1_b, ln2_g, ln2_b, ffn_w_up, ffn_conv_w, ffn_conv_b, ffn_w_down, loss_target, m_a_w_in, m_a_pool_w, m_a_pool_scale, m_a_w_out, m_b_w_q, m_b_w_out, m_kv_w, m_f_b, m_mem_w_kv, m_ln1_g, m_ln1_b, m_ln2_g, m_ln2_b, m_ffn_w_up, m_ffn_conv_w, m_ffn_conv_b, m_ffn_w_down, v_a_w_in, v_a_pool_w, v_a_pool_scale, v_a_w_out, v_b_w_q, v_b_w_out, v_kv_w, v_f_b, v_mem_w_kv, v_ln1_g, v_ln1_b, v_ln2_g, v_ln2_b, v_ffn_w_up, v_ffn_conv_w, v_ffn_conv_b, v_ffn_w_down):
    w_loc = dict(a_w_in=a_w_in, a_pool_w=a_pool_w, a_pool_scale=a_pool_scale, a_w_out=a_w_out, b_w_q=b_w_q,
                 b_w_out=b_w_out, kv_w=kv_w, f_b=f_b, mem_w_kv=mem_w_kv, ln1_g=ln1_g, ln1_b=ln1_b, ln2_g=ln2_g,
                 ln2_b=ln2_b, ffn_w_up=ffn_w_up, ffn_conv_w=ffn_conv_w, ffn_conv_b=ffn_conv_b, ffn_w_down=ffn_w_down)
    m_loc = dict(a_w_in=m_a_w_in, a_pool_w=m_a_pool_w, a_pool_scale=m_a_pool_scale, a_w_out=m_a_w_out,
                 b_w_q=m_b_w_q, b_w_out=m_b_w_out, kv_w=m_kv_w, f_b=m_f_b, mem_w_kv=m_mem_w_kv, ln1_g=m_ln1_g,
                 ln1_b=m_ln1_b, ln2_g=m_ln2_g, ln2_b=m_ln2_b, ffn_w_up=m_ffn_w_up, ffn_conv_w=m_ffn_conv_w,
                 ffn_conv_b=m_ffn_conv_b, ffn_w_down=m_ffn_w_down)
    v_loc = dict(a_w_in=v_a_w_in, a_pool_w=v_a_pool_w, a_pool_scale=v_a_pool_scale, a_w_out=v_a_w_out,
                 b_w_q=v_b_w_q, b_w_out=v_b_w_out, kv_w=v_kv_w, f_b=v_f_b, mem_w_kv=v_mem_w_kv, ln1_g=v_ln1_g,
                 ln1_b=v_ln1_b, ln2_g=v_ln2_g, ln2_b=v_ln2_b, ffn_w_up=v_ffn_w_up, ffn_conv_w=v_ffn_conv_w,
                 ffn_conv_b=v_ffn_conv_b, ffn_w_down=v_ffn_w_down)
    x_i, y_i, c = lax.axis_index("x"), lax.axis_index("y"), lax.axis_index("c")
    q = 2 * x_i + y_i
    qc = jnp.stack([q, c]).astype(jnp.int32)
    shapes = {k: v.shape for k, v in w_loc.items()}

    def halves(a):
        if a.ndim == 3 and a.shape[0] == 2:
            return a
        rows = math.prod(a.shape[:-1])
        return a.reshape(2, rows // 2, a.shape[-1])

    own = {k: (w_loc[k] if k in GATHER_F32 else w_loc[k].astype(BF16)) for k in SHARDED}
    gathered = _gather_shards([own["a_pool_scale"]], [halves(own[k]) for k in BIG], name="gather_weights")
    W = {k: _from_shards(lax.dynamic_update_slice_in_dim(g.reshape((N_CHIP,) + shapes[k]), own[k][None], q, axis=0),
                         SHARD_AXIS[k])
         for k, g in zip(("a_pool_scale",) + BIG, gathered)}
    for k in REPLICATED:
        W[k] = w_loc[k]

    loss_part, grad_x, G = _local_step(x, mem, loss_target, W)
    loss = lax.psum(loss_part, ("x", "y", "c"))

    g_chip = [_to_shards(G[k], SHARD_AXIS[k]) for k in BIG]
    g_chip = [g.reshape((N_CHIP,) + halves(w_loc[k]).shape) for k, g in zip(BIG, g_chip)]
    from_sib = _to_sibling(g_chip, name="grads_to_sibling")
    sums = [_pair_add(g, s, qc, name=f"pair_add_{k}") for k, g, s in zip(BIG, g_chip, from_sib)]
    parts = _to_chips(sums, name="scatter_grads")
    res = []
    for k, g, p in zip(BIG, sums, parts):
        res.extend(_adamw_half(g, p, halves(w_loc[k]), halves(m_loc[k]), halves(v_loc[k]), qc, name=f"adamw_{k}"))
    res = _swap_halves(res, name="swap_halves")
    out = {k: [r.reshape(shapes[k]) for r in res[4 * i:4 * i + 4]] for i, k in enumerate(BIG)}

    lay_r, tot_r = _layout(shapes, REPLICATED, PACK_COLS)
    rep_rows = tot_r // PACK_COLS
    rows = _round_up(rep_rows + 1, SUBLANE)
    scale_w = shapes["a_pool_scale"][-1]

    def small(rep, scale_row):
        lead = scale_row.shape[:-2]
        rep = jnp.broadcast_to(_pack(rep, lay_r, tot_r).reshape(rep_rows, PACK_COLS), lead + (rep_rows, PACK_COLS))
        pad = [(0, 0)] * len(lead)
        return jnp.concatenate([rep, jnp.pad(scale_row, pad + [(0, rows - rep_rows - 1), (0, PACK_COLS - scale_w)])],
                               axis=-2)

    g_scale = jnp.repeat(_to_shards(G["a_pool_scale"], 1), 2, axis=0)
    sm_parts = _exchange(small(G, g_scale), gather=False, name="scatter_small")
    sm = _adamw(sm_parts, *[small(d, d["a_pool_scale"]) for d in (w_loc, m_loc, v_loc)], name="adamw_small")
    out_r = _unpack(sm[:, :rep_rows].reshape(4, tot_r), lay_r, lead=(4,))
    for k in REPLICATED:
        out[k] = [out_r[k][a] for a in range(4)]
    out["a_pool_scale"] = [sm[a, rep_rows:rep_rows + 1, :scale_w] for a in range(4)]

    outs = [loss, grad_x]
    for a in range(4):
        for k in WEIGHTS:
            outs.append(out[k][a])
    return tuple(outs)
```

```python
import functools
import math

import jax
import jax.numpy as jnp
from jax import lax
from jax.experimental import pallas as pl
from jax.experimental.pallas import tpu as pltpu

F32 = jnp.float32
BF16 = jnp.bfloat16

HEAD_DIM = 64
MEM_HEADS = 4
MEM_WIDTH = MEM_HEADS * HEAD_DIM
POOL_WINDOWS = (2, 4, 8, 16)
MAX_WINDOW = 16
CONV_WIDTH = 3
DEPTH = 2
DN_ALPHA = (2.0 * DEPTH) ** 0.25
LN_EPS = 1e-5
ATT_SCALE = HEAD_DIM ** -0.5
NEG_BIG = -1e30

ADAM_LR = 0.001
ADAM_B1 = 0.9
ADAM_B2 = 0.999
ADAM_EPS = 1e-08
ADAM_WD = 0.01
ADAM_STEP = 10

LANE = 128
SUBLANE = 8
PACK_COLS = 1024
VMEM_LIMIT = 56 * 1024 * 1024
N_DEV = 8
N_CHIP = 4
MESH_T = pl.DeviceIdType.MESH

SHARDED = ("a_w_in", "a_pool_scale", "a_w_out", "b_w_q", "b_w_out", "kv_w", "mem_w_kv", "ffn_w_up",
           "ffn_conv_w", "ffn_w_down")
SHARD_AXIS = {"a_w_in": 1, "a_pool_scale": 1, "a_w_out": 1, "b_w_q": 1, "b_w_out": 1, "kv_w": 1, "mem_w_kv": 1,
              "ffn_w_up": 2, "ffn_conv_w": 2, "ffn_w_down": 1}
GATHER_F32 = ("a_pool_scale", "ffn_conv_w")
BIG = tuple(k for k in SHARDED if k != "a_pool_scale")
GRADS_BY_CHIP = ("ffn_w_up", "ffn_w_down")
REPLICATED = ("a_pool_w", "f_b", "ln1_g", "ln1_b", "ln2_g", "ln2_b", "ffn_conv_b")
WEIGHTS = ("a_w_in", "a_pool_w", "a_pool_scale", "a_w_out", "b_w_q", "b_w_out", "kv_w", "f_b", "mem_w_kv",
           "ln1_g", "ln1_b", "ln2_g", "ln2_b", "ffn_w_up", "ffn_conv_w", "ffn_conv_b", "ffn_w_down")


def _round_up(n, m):
    return -(-n // m) * m


def _pick(dim, pref, unit=LANE):
    if dim <= pref:
        return dim
    t = (pref // unit) * unit
    while t >= unit:
        if dim % t == 0:
            return t
        t -= unit
    raise ValueError(f"no tile for {dim} <= {pref}")


def _params():
    return pltpu.CompilerParams(vmem_limit_bytes=VMEM_LIMIT)


_DIMS = {"nn": ((1,), (0,)), "nt": ((1,), (1,)), "tn": ((0,), (0,))}


def _bdot(a, b, mode):
    return lax.dot_general(a.astype(BF16), b.astype(BF16), (_DIMS[mode], ((), ())), preferred_element_type=F32)


def _mm(a, b, mode, *, name, tm=512, tn=1024, tk=2048, adds=(), b_col0=0, trim=None):
    if mode == "nn":
        (M, K), (K2, N) = a.shape, b.shape
    elif mode == "nt":
        (M, K), (N, K2) = a.shape, b.shape
        K2 = K if b_col0 + K <= K2 else -1
    else:
        (K, M), (K2, N) = a.shape, b.shape
    assert K == K2 and (mode == "nt" or b_col0 == 0), (name, a.shape, b.shape)
    tm, tn = _pick(M, tm, SUBLANE if mode != "tn" else LANE), _pick(N, tn)
    tk = _pick(K, tk, LANE if mode != "tn" else SUBLANE)
    nk = K // tk
    assert b_col0 % tk == 0, (name, b_col0, tk)
    koff = b_col0 // tk
    n_add = len(adds)

    def body(*refs):
        a_ref, b_ref = refs[0], refs[1]
        add_refs = refs[2:2 + n_add]
        o_ref, acc_ref = refs[2 + n_add], refs[3 + n_add]
        part = _bdot(a_ref[...], b_ref[...], mode)

        def finish(r):
            for ar in add_refs:
                r = r + ar[...]
            if trim is not None:
                r = r[:, :trim[1]] if trim[0] == "cols" else r[:trim[1], :]
            o_ref[...] = r

        if nk == 1:
            finish(part)
        else:
            k = pl.program_id(2)

            @pl.when(k == 0)
            def _():
                acc_ref[...] = part

            @pl.when(k > 0)
            def _():
                acc_ref[...] += part

            @pl.when(k == nk - 1)
            def _():
                finish(acc_ref[...])

    if mode == "nn":
        a_spec = pl.BlockSpec((tm, tk), lambda i, j, k: (i, k))
        b_spec = pl.BlockSpec((tk, tn), lambda i, j, k: (k, j))
    elif mode == "nt":
        a_spec = pl.BlockSpec((tm, tk), lambda i, j, k: (i, k))
        b_spec = pl.BlockSpec((tn, tk), lambda i, j, k: (j, k + koff))
    else:
        a_spec = pl.BlockSpec((tk, tm), lambda i, j, k: (k, i))
        b_spec = pl.BlockSpec((tk, tn), lambda i, j, k: (k, j))
    o_spec = pl.BlockSpec((tm, tn), lambda i, j, k: (i, j))
    out_spec, out_shape = o_spec, (M, N)
    if trim is not None and trim[0] == "cols":
        out_spec, out_shape = pl.BlockSpec((None, tm, trim[1]), lambda i, j, k: (j, i, 0)), (N // tn, M, trim[1])
    elif trim is not None:
        out_spec, out_shape = pl.BlockSpec((None, trim[1], tn), lambda i, j, k: (i, 0, j)), (M // tm, trim[1], N)
    acc_shape = (tm, tn) if nk > 1 else (SUBLANE, LANE)
    return pl.pallas_call(
        body, name=name, grid=(M // tm, N // tn, nk),
        in_specs=[a_spec, b_spec] + [o_spec] * n_add, out_specs=out_spec,
        out_shape=jax.ShapeDtypeStruct(out_shape, F32),
        scratch_shapes=[pltpu.VMEM(acc_shape, F32)],
        compiler_params=_params(),
    )(a, b, *adds)


def _rowwise(fn, tiled, full, outs_tiled, outs_acc, *, rows, tile, name, acc_period=None):
    n_tiles = rows // tile
    period = n_tiles if acc_period is None else acc_period
    arrays, in_specs = [], []
    for t in tiled:
        arr, width, cb = t if isinstance(t, tuple) else (t, t.shape[1], 0)
        arrays.append(arr)
        in_specs.append(pl.BlockSpec((tile, width), lambda i, cb=cb: (i, cb)))
    for f in full:
        arr, spec = f if isinstance(f, tuple) else (f, None)
        arrays.append(arr)
        in_specs.append(spec if spec is not None else pl.BlockSpec(arr.shape, lambda i, nd=arr.ndim: (0,) * nd))
    out_shape, out_specs = [], []
    for width, dt in outs_tiled:
        out_shape.append(jax.ShapeDtypeStruct((rows, width), dt))
        out_specs.append(pl.BlockSpec((tile, width), lambda i: (i, 0)))
    for acc in outs_acc:
        shape, dt = acc[0], acc[1]
        out_shape.append(jax.ShapeDtypeStruct(shape, dt))
        out_specs.append(acc[2] if len(acc) > 2 else pl.BlockSpec(shape, lambda i, nd=len(shape): (0,) * nd))
    n_in, n_t, n_a = len(arrays), len(outs_tiled), len(outs_acc)

    def body(*refs):
        vals = [r[...] for r in refs[:n_in]]
        o_t, o_a = fn(*vals)
        for r, v in zip(refs[n_in:n_in + n_t], o_t):
            r[...] = v.astype(r.dtype)
        first = pl.program_id(0) % period == 0
        for r, v in zip(refs[n_in + n_t:n_in + n_t + n_a], o_a):
            v = v.reshape(r.shape)

            @pl.when(first)
            def _(r=r, v=v):
                r[...] = v

            @pl.when(jnp.logical_not(first))
            def _(r=r, v=v):
                r[...] += v

    return pl.pallas_call(
        body, name=name, grid=(n_tiles,), in_specs=in_specs, out_specs=out_specs, out_shape=out_shape,
        compiler_params=_params(),
    )(*arrays)


def _ln_stats(h):
    mu = jnp.mean(h, axis=-1, keepdims=True)
    d = h - mu
    var = jnp.mean(d * d, axis=-1, keepdims=True)
    rstd = lax.rsqrt(var + LN_EPS)
    return d * rstd, rstd


def _ln_bwd_math(h, g, dy):
    xhat, rstd = _ln_stats(h)
    dxhat = dy * g
    dh = rstd * (dxhat - jnp.mean(dxhat, axis=-1, keepdims=True)
                 - xhat * jnp.mean(dxhat * xhat, axis=-1, keepdims=True))
    return dh, jnp.sum(dy * xhat, axis=0, keepdims=True), jnp.sum(dy, axis=0, keepdims=True)


def _ln_fwd(x, r, g, b, *, name):
    n, d = x.shape

    def fn(x, r, g, b):
        xhat, _ = _ln_stats(DN_ALPHA * x + r)
        return (xhat * g + b,), ()

    return _rowwise(fn, [x, r], [g, b], [(d, F32)], [], rows=n, tile=_pick(n, 512, SUBLANE), name=name)[0]


def _ln_bwd(x, r, g, dys, *, name):
    n, d = x.shape
    n_dy = len(dys)

    def fn(x, r, *rest):
        dy = rest[0]
        for e in rest[1:n_dy]:
            dy = dy + e
        dh, dg, db = _ln_bwd_math(DN_ALPHA * x + r, rest[n_dy], dy)
        return (DN_ALPHA * dh, dh), (dg, db)

    return _rowwise(fn, [x, r, *dys], [g], [(d, F32), (d, F32)], [((1, d), F32), ((1, d), F32)],
                    rows=n, tile=_pick(n, 256, SUBLANE), name=name)


def _final_ln_loss(x, r, target, g, b, *, name):
    n, d = x.shape

    def fn(x, r, t, g, b):
        h = DN_ALPHA * x + r
        xhat, _ = _ln_stats(h)
        err = xhat * g + b - t
        loss = jnp.full((1, LANE), 0.5 * jnp.sum(err * err) / d, F32)
        dh, dg, db = _ln_bwd_math(h, g, err / d)
        return (DN_ALPHA * dh, dh), (loss, dg, db)

    return _rowwise(fn, [x, r, target], [g, b], [(d, F32), (d, F32)],
                    [((1, LANE), F32), ((1, d), F32), ((1, d), F32)],
                    rows=n, tile=_pick(n, 256, SUBLANE), name=name)


def _mem_heads(qm):
    lane = lax.broadcasted_iota(jnp.int32, (1, MEM_WIDTH), 1)
    for h in range(MEM_HEADS):
        msk = (lane >= h * HEAD_DIM) & (lane < (h + 1) * HEAD_DIM)
        yield msk, jnp.where(msk, qm, 0.0).astype(BF16)


def _mem_softmax(qh, k):
    s = _bdot(qh, k, "nt") * ATT_SCALE
    p = jnp.exp(s - jnp.max(s, axis=-1, keepdims=True))
    return p / jnp.sum(p, axis=-1, keepdims=True)


def _memattn_fwd(tok, proj, memkv, scale, *, seq, name):
    n, tokw = tok.shape
    d = tokw + MEM_WIDTH
    tile = _pick(seq, 512, SUBLANE)

    def fn(tok, qm, kv, scale):
        k, v = kv[:, :MEM_WIDTH].astype(BF16), kv[:, MEM_WIDTH:].astype(BF16)
        out = jnp.zeros(qm.shape, F32)
        for msk, qh in _mem_heads(qm):
            out = jnp.where(msk, _bdot(_mem_softmax(qh, k), v, "nn"), out)
        return (jnp.concatenate([tok * scale, out], axis=1),), ()

    kv_spec = pl.BlockSpec((None,) + memkv.shape[1:], lambda i: (i // (seq // tile), 0, 0))
    return _rowwise(fn, [tok, (proj, MEM_WIDTH, tokw // MEM_WIDTH)], [(memkv, kv_spec), scale], [(d, F32)], [],
                    rows=n, tile=tile, name=name)[0]


def _memattn_bwd(dmixin, dtok, proj, memkv, *, seq, name):
    n, tokw = dtok.shape
    d = tokw + MEM_WIDTH
    tile = _pick(seq, 512, SUBLANE)

    def fn(dmo, dtok, qm, kv):
        k, v = kv[:, :MEM_WIDTH].astype(BF16), kv[:, MEM_WIDTH:].astype(BF16)
        dq = jnp.zeros(qm.shape, F32)
        dk = jnp.zeros(k.shape, F32)
        dv = jnp.zeros(v.shape, F32)
        for msk, qh in _mem_heads(qm):
            p = _mem_softmax(qh, k)
            doh = jnp.where(msk, dmo, 0.0).astype(BF16)
            dv = dv + _bdot(p, doh, "tn")
            dp = _bdot(doh, v, "nt")
            ds = (p * (dp - jnp.sum(dp * p, axis=-1, keepdims=True))).astype(BF16)
            dq = jnp.where(msk, _bdot(ds, k, "nn") * ATT_SCALE, dq)
            dk = dk + _bdot(ds, qh, "tn") * ATT_SCALE
        return (jnp.concatenate([dtok, dq], axis=1),), (jnp.concatenate([dk, dv], axis=1),)

    tpe = seq // tile
    kv_spec = pl.BlockSpec((None,) + memkv.shape[1:], lambda i: (i // tpe, 0, 0))
    return _rowwise(fn, [(dmixin, MEM_WIDTH, tokw // MEM_WIDTH), dtok, (proj, MEM_WIDTH, tokw // MEM_WIDTH)],
                    [(memkv, kv_spec)], [(d, F32)], [(memkv.shape, F32, kv_spec)],
                    rows=n, tile=tile, name=name, acc_period=tpe)


def _scale_bwd(dmixin, mixed, scale, *, name):
    n, tokw = mixed.shape

    def fn(dt, mixed, scale):
        return (dt * scale,), (jnp.sum(dt * mixed, axis=0, keepdims=True),)

    return _rowwise(fn, [(dmixin, tokw, 0), mixed], [scale], [(tokw, F32)], [((1, tokw), F32)],
                    rows=n, tile=_pick(n, 512, SUBLANE), name=name)


def _chunk_rows(seq):
    return _pick(seq, 512, SUBLANE)


def _load_ext(ref, c, rows, before, after, seq):
    lo, hi = c * rows - before, (c + 1) * rows + after
    parts = []
    if lo < 0:
        parts.append(jnp.zeros((-lo, ref.shape[1]), F32))
    parts.append(ref[max(lo, 0):min(hi, seq), :])
    if hi > seq:
        parts.append(jnp.zeros((hi - seq, ref.shape[1]), F32))
    return parts[0] if len(parts) == 1 else jnp.concatenate(parts, axis=0)


def _down(x, k):
    return pltpu.roll(x, k, 0)


def _up(x, k):
    return pltpu.roll(x, x.shape[0] - k, 0)


def _window_sums(ext, shift, col0, group):
    lane = col0 + lax.broadcasted_iota(jnp.int32, (1, ext.shape[1]), 1)
    gidx = lane // group
    s = ext
    out = None
    k = 1
    for gi, w in enumerate(POOL_WINDOWS):
        while k < w:
            s = s + shift(s, k)
            k *= 2
        out = s if out is None else jnp.where(gidx >= gi, s, out)
    return out, jnp.left_shift(2, jnp.minimum(gidx, len(POOL_WINDOWS) - 1))


def _pool_fwd(proj3, tokw, *, name):
    nb, seq, _ = proj3.shape
    rows = _chunk_rows(seq)
    group = tokw // len(POOL_WINDOWS)

    def body(u_ref, o_ref):
        col0 = pl.program_id(1) * LANE
        for c in range(seq // rows):
            ext = _load_ext(u_ref, c, rows, MAX_WINDOW, 0, seq)
            sums, win = _window_sums(ext, _down, col0, group)
            t = c * rows + lax.broadcasted_iota(jnp.int32, (rows, 1), 0)
            count = jnp.minimum(t + 1, win).astype(F32)
            o_ref[c * rows:(c + 1) * rows, :] = sums[MAX_WINDOW:, :] / count - ext[MAX_WINDOW:, :]

    spec = pl.BlockSpec((None, seq, LANE), lambda b, j: (b, 0, j))
    return pl.pallas_call(
        body, name=name, grid=(nb, tokw // LANE), in_specs=[spec], out_specs=spec,
        out_shape=jax.ShapeDtypeStruct((nb, seq, tokw), F32), compiler_params=_params(),
    )(proj3)


def _pool_bwd(dp3, *, name):
    nb, seq, tokw = dp3.shape
    rows = _chunk_rows(seq)
    group = tokw // len(POOL_WINDOWS)

    def body(d_ref, o_ref):
        col0 = pl.program_id(1) * LANE
        for c in range(seq // rows):
            ext = _load_ext(d_ref, c, rows, 0, MAX_WINDOW, seq)
            lane = col0 + lax.broadcasted_iota(jnp.int32, (1, LANE), 1)
            win = jnp.left_shift(2, jnp.minimum(lane // group, len(POOL_WINDOWS) - 1))
            t = c * rows + lax.broadcasted_iota(jnp.int32, (rows + MAX_WINDOW, 1), 0)
            scaled = ext / jnp.minimum(t + 1, win).astype(F32)
            sums, _ = _window_sums(scaled, _up, col0, group)
            o_ref[c * rows:(c + 1) * rows, :] = sums[:rows, :] - ext[:rows, :]

    spec = pl.BlockSpec((None, seq, LANE), lambda b, j: (b, 0, j))
    return pl.pallas_call(
        body, name=name, grid=(nb, tokw // LANE), in_specs=[spec], out_specs=spec,
        out_shape=jax.ShapeDtypeStruct((nb, seq, tokw), F32), compiler_params=_params(),
    )(dp3)


def _conv3(ext, w_ref, b_ref):
    x1, x2 = _down(ext, 1), _down(ext, 2)
    return w_ref[0:1, :] * x2 + w_ref[1:2, :] * x1 + w_ref[2:3, :] * ext + b_ref[...], x1, x2


def _convgate_fwd(up3, cw, cb, *, name):
    nb, seq, c2 = up3.shape
    fp = c2 // 2
    nblk = fp // LANE
    rows = _chunk_rows(seq)

    def body(u_ref, g_ref, wu_ref, wg_ref, bu_ref, bg_ref, o_ref):
        for c in range(seq // rows):
            hu, _, _ = _conv3(_load_ext(u_ref, c, rows, SUBLANE, 0, seq), wu_ref, bu_ref)
            hg, _, _ = _conv3(_load_ext(g_ref, c, rows, SUBLANE, 0, seq), wg_ref, bg_ref)
            o_ref[c * rows:(c + 1) * rows, :] = (hg * jax.nn.sigmoid(hg) * hu)[SUBLANE:, :]

    def col(off, r):
        return pl.BlockSpec((r, LANE), lambda b, j: (0, j + off))

    def act(off):
        return pl.BlockSpec((None, seq, LANE), lambda b, j: (b, 0, j + off))

    return pl.pallas_call(
        body, name=name, grid=(nb, nblk),
        in_specs=[act(0), act(nblk), col(0, SUBLANE), col(nblk, SUBLANE), col(0, 1), col(nblk, 1)],
        out_specs=act(0), out_shape=jax.ShapeDtypeStruct((nb, seq, fp), F32), compiler_params=_params(),
    )(up3, up3, cw, cw, cb, cb)


def _convgate_bwd(up3, dact3, cw, cb, *, name):
    nb, seq, c2 = up3.shape
    fp = c2 // 2
    nblk = fp // LANE
    rows = _chunk_rows(seq)
    h = SUBLANE

    def body(u_ref, g_ref, da_ref, wu_ref, wg_ref, bu_ref, bg_ref, du_ref, dg_ref, dwu_ref, dwg_ref, dbu_ref,
             dbg_ref):
        @pl.when(pl.program_id(1) == 0)
        def _():
            for r in (dwu_ref, dwg_ref, dbu_ref, dbg_ref):
                r[...] = jnp.zeros(r.shape, F32)

        for c in range(seq // rows):
            eu = _load_ext(u_ref, c, rows, h, h, seq)
            eg = _load_ext(g_ref, c, rows, h, h, seq)
            da = _load_ext(da_ref, c, rows, h, h, seq)
            hu, u1, u2 = _conv3(eu, wu_ref, bu_ref)
            hg, g1, g2 = _conv3(eg, wg_ref, bg_ref)
            sig = jax.nn.sigmoid(hg)
            dhu = da * hg * sig
            dhg = da * hu * sig * (1.0 + hg * (1.0 - sig))
            for dh, w_ref, x0, x1, x2, dx_ref, dw_ref, db_ref in (
                    (dhu, wu_ref, eu, u1, u2, du_ref, dwu_ref, dbu_ref),
                    (dhg, wg_ref, eg, g1, g2, dg_ref, dwg_ref, dbg_ref)):
                dx = w_ref[2:3, :] * dh + w_ref[1:2, :] * _up(dh, 1) + w_ref[0:1, :] * _up(dh, 2)
                dx_ref[c * rows:(c + 1) * rows, :] = dx[h:h + rows, :]
                core = dh[h:h + rows, :]
                for k, xk in ((0, x2), (1, x1), (2, x0)):
                    dw_ref[k:k + 1, :] += jnp.sum(core * xk[h:h + rows, :], axis=0, keepdims=True)
                db_ref[...] += jnp.sum(core, axis=0, keepdims=True)

    def col(off, r):
        return pl.BlockSpec((r, LANE), lambda j, b: (0, j + off))

    def act(off):
        return pl.BlockSpec((None, seq, LANE), lambda j, b: (b, 0, j + off))

    du, dg, dwu, dwg, dbu, dbg = pl.pallas_call(
        body, name=name, grid=(nblk, nb),
        in_specs=[act(0), act(nblk), act(0), col(0, SUBLANE), col(nblk, SUBLANE), col(0, 1), col(nblk, 1)],
        out_specs=[act(0), act(0), col(0, SUBLANE), col(0, SUBLANE), col(0, 1), col(0, 1)],
        out_shape=[jax.ShapeDtypeStruct((nb, seq, fp), F32), jax.ShapeDtypeStruct((nb, seq, fp), F32),
                   jax.ShapeDtypeStruct((SUBLANE, fp), F32), jax.ShapeDtypeStruct((SUBLANE, fp), F32),
                   jax.ShapeDtypeStruct((1, fp), F32), jax.ShapeDtypeStruct((1, fp), F32)],
        compiler_params=_params(),
    )(up3, up3, dact3, cw, cw, cb, cb)
    return du, dg, jnp.concatenate([dwu, dwg], axis=1), jnp.concatenate([dbu, dbg], axis=1)


def _scan_rows(x, shift, valid):
    row = lax.broadcasted_iota(jnp.int32, (x.shape[0], 1), 0)
    k = 1
    while k < x.shape[0]:
        x = x + jnp.where(valid(row, k), shift(x, k), 0.0)
        k *= 2
    return x


def _pick_row(x, r):
    row = lax.broadcasted_iota(jnp.int32, (x.shape[0], 1), 0)
    return jnp.sum(jnp.where(row == r, x, 0.0), axis=0, keepdims=True)


def _log_sigmoid(z):
    return jnp.minimum(z, 0.0) - jnp.log(1.0 + jnp.exp(-jnp.abs(z)))


def _gate_fwd(kvf3, fb, col_block, *, name):
    nb, seq, _ = kvf3.shape
    rows = _chunk_rows(seq)

    def body(f_ref, fb_ref, o_ref):
        carry = jnp.zeros((1, LANE), F32)
        for c in range(seq // rows):
            logf = _log_sigmoid(f_ref[c * rows:(c + 1) * rows, :] + fb_ref[...])
            run = _scan_rows(logf, _down, lambda row, k: row >= k) + carry
            o_ref[c * rows:(c + 1) * rows, :] = run
            carry = _pick_row(run, rows - 1)

    return pl.pallas_call(
        body, name=name, grid=(nb,),
        in_specs=[pl.BlockSpec((None, seq, LANE), lambda b: (b, 0, col_block)),
                  pl.BlockSpec((1, LANE), lambda b: (0, 0))],
        out_specs=pl.BlockSpec((None, seq, LANE), lambda b: (b, 0, 0)),
        out_shape=jax.ShapeDtypeStruct((nb, seq, LANE), F32), compiler_params=_params(),
    )(kvf3, fb)


def _gate_bwd(kvf3, fb, dF3, col_block, heads, *, name):
    nb, seq, _ = kvf3.shape
    rows = _chunk_rows(seq)

    def body(f_ref, fb_ref, d_ref, o_ref, dfb_ref):
        @pl.when(pl.program_id(0) == 0)
        def _():
            dfb_ref[...] = jnp.zeros(dfb_ref.shape, F32)

        lane = lax.broadcasted_iota(jnp.int32, (1, LANE), 1)
        carry = jnp.zeros((1, LANE), F32)
        for c in reversed(range(seq // rows)):
            run = _scan_rows(d_ref[c * rows:(c + 1) * rows, :], _up, lambda row, k: row < rows - k) + carry
            carry = _pick_row(run, 0)
            z = f_ref[c * rows:(c + 1) * rows, :] + fb_ref[...]
            df = jnp.where(lane < heads, run * jax.nn.sigmoid(-z), 0.0)
            o_ref[c * rows:(c + 1) * rows, :] = df
            dfb_ref[...] += jnp.sum(df, axis=0, keepdims=True)

    return pl.pallas_call(
        body, name=name, grid=(nb,),
        in_specs=[pl.BlockSpec((None, seq, LANE), lambda b: (b, 0, col_block)),
                  pl.BlockSpec((1, LANE), lambda b: (0, 0)),
                  pl.BlockSpec((None, seq, LANE), lambda b: (b, 0, 0))],
        out_specs=[pl.BlockSpec((None, seq, LANE), lambda b: (b, 0, 0)), pl.BlockSpec((1, LANE), lambda b: (0, 0))],
        out_shape=[jax.ShapeDtypeStruct((nb, seq, LANE), F32), jax.ShapeDtypeStruct((1, LANE), F32)],
        compiler_params=_params(),
    )(kvf3, fb, dF3)


def _head_masks():
    lane = lax.broadcasted_iota(jnp.int32, (1, LANE), 1)
    return (lane < HEAD_DIM, lane >= HEAD_DIM)


BIAS_TERMS = 3


def _bias_lanes(gsum):
    nb, seq, heads = gsum.shape
    terms, rest = [], gsum
    for _ in range(BIAS_TERMS):
        t = lax.reduce_precision(rest, exponent_bits=8, mantissa_bits=7)
        terms.append(t)
        rest = rest - t
    ones = [jnp.ones_like(gsum)] * BIAS_TERMS

    def lanes(parts):
        z = jnp.stack(parts, axis=-1)
        z = jnp.pad(z, ((0, 0), (0, 0), (0, 0), (0, HEAD_DIM - 2 * BIAS_TERMS)))
        z = z.reshape(nb, seq, heads // 2, 2, HEAD_DIM)[:, :, :, ::-1]
        return z.reshape(nb, seq, heads * HEAD_DIM).astype(BF16)

    return lanes(terms + ones), lanes(ones + [-t for t in terms])


def _fox_scores(q, k, aq, ak, masked):
    qs = (q * ATT_SCALE).astype(BF16)
    qts = [jnp.where(msk, qs, aq) for msk in _head_masks()]
    ss = [_bdot(qt, jnp.where(msk, k, ak), "nt") for qt, msk in zip(qts, _head_masks())]
    if masked:
        t = q.shape[0]
        keep = lax.broadcasted_iota(jnp.int32, (t, t), 0) >= lax.broadcasted_iota(jnp.int32, (t, t), 1)
        ss = [jnp.where(keep, s, NEG_BIG) for s in ss]
    return ss, qts


def _on_blocks(qi, ki, step):
    @pl.when(ki < qi)
    def _():
        step(False)

    @pl.when(ki == qi)
    def _():
        step(True)


def _fox_grid(nblk, tokw, t, q_major):
    if q_major:
        pairs = [(qi, ki) for qi in range(nblk) for ki in range(qi + 1)]
    else:
        pairs = [(qi, ki) for ki in range(nblk) for qi in range(ki, nblk)]
    tables = [jnp.array([p[i] for p in pairs], jnp.int32) for i in (0, 1)]

    def q_spec(off=0, wide=False):
        width = 2 * LANE if wide else LANE
        return pl.BlockSpec((None, t, width), lambda b, p, i, qt, kt: (b, qt[i], p + off))

    def k_spec(off=0):
        return pl.BlockSpec((None, t, LANE), lambda b, p, i, qt, kt: (b, kt[i], p + off))

    return tables, len(pairs), q_spec, k_spec, tokw // LANE


def _lanes(col):
    return jnp.broadcast_to(col, (col.shape[0], LANE))


def _across(stat, width):
    return jnp.tile(stat, (1, width // LANE))


def _fox_fwd(proj3, kvf3, aq3, ak3, tokw, *, name):
    nb, seq, _ = proj3.shape
    t = _pick(seq, 512, LANE)
    tables, n_pairs, q_spec, k_spec, hp0 = _fox_grid(seq // t, tokw, t, True)

    def body(qt_ref, kt_ref, q_ref, k_ref, v_ref, aq_ref, ak_ref, o_ref, lse_ref, m_s, l_s, acc_s):
        i = pl.program_id(2)
        qi, ki = qt_ref[i], kt_ref[i]

        @pl.when(ki == 0)
        def _():
            m_s[...] = jnp.full(m_s.shape, NEG_BIG, F32)
            l_s[...] = jnp.zeros(l_s.shape, F32)
            acc_s[...] = jnp.zeros(acc_s.shape, F32)

        def step(masked):
            v = v_ref[...].astype(BF16)
            ss, _ = _fox_scores(q_ref[...], k_ref[...].astype(BF16), aq_ref[...], ak_ref[...], masked)
            for h, s in enumerate(ss):
                m_old = m_s[h]
                m_new = jnp.maximum(m_old, _lanes(jnp.max(s, axis=-1, keepdims=True)))
                alpha = jnp.exp(m_old - m_new)
                p = jnp.exp(s - _across(m_new, t))
                l_s[h] = alpha * l_s[h] + _lanes(jnp.sum(p, axis=-1, keepdims=True))
                acc_s[h] = alpha * acc_s[h] + _bdot(p, v, "nn")
                m_s[h] = m_new

        _on_blocks(qi, ki, step)

        @pl.when(ki == qi)
        def _():
            o_ref[...] = jnp.where(_head_masks()[0], acc_s[0] / l_s[0], acc_s[1] / l_s[1])
            lse_ref[...] = jnp.concatenate([m_s[0] + jnp.log(l_s[0]), m_s[1] + jnp.log(l_s[1])], axis=1)

    stat = pltpu.VMEM((2, t, LANE), F32)
    return pl.pallas_call(
        body, name=name,
        grid_spec=pltpu.PrefetchScalarGridSpec(
            num_scalar_prefetch=2, grid=(nb, hp0, n_pairs),
            in_specs=[q_spec(), k_spec(), k_spec(hp0), q_spec(), k_spec()],
            out_specs=[q_spec(), q_spec(wide=True)], scratch_shapes=[stat, stat, stat]),
        out_shape=[jax.ShapeDtypeStruct((nb, seq, tokw), F32), jax.ShapeDtypeStruct((nb, seq, 2 * tokw), F32)],
        compiler_params=_params(),
    )(*tables, proj3, kvf3, kvf3, aq3, ak3)


def _fox_bwd_common(q_ref, k_ref, v_ref, aq_ref, ak_ref, do_ref, lse_ref, delta_ref, masked):
    k, v = k_ref[...].astype(BF16), v_ref[...].astype(BF16)
    ss, qts = _fox_scores(q_ref[...], k, aq_ref[...], ak_ref[...], masked)
    do = do_ref[...]
    t = do.shape[0]
    out = []
    for h, (s, qt, msk) in enumerate(zip(ss, qts, _head_masks())):
        doh = jnp.where(msk, do, 0.0).astype(BF16)
        p = jnp.exp(s - _across(lse_ref[:, h * LANE:(h + 1) * LANE], t))
        ds = p * (_bdot(doh, v, "nt") - _across(delta_ref[:, h * LANE:(h + 1) * LANE], t))
        out.append((qt, doh, p, ds))
    return out, k


def _fox_bwd(proj3, kvf3, aq3, ak3, o3, dmixin3, lse3, tokw, *, name):
    nb, seq, _ = proj3.shape
    t = _pick(seq, 512, LANE)
    nblk = seq // t
    tables, n_pairs, q_spec, k_spec, hp0 = _fox_grid(nblk, tokw, t, True)
    whole = pl.BlockSpec((None, seq, LANE), lambda b, p, i, qt, kt: (b, 0, p))
    dfk_spec = pl.BlockSpec((None, None, nblk, SUBLANE, t), lambda b, p, i, qt, kt: (b, p, 0, 0, 0))

    def body(qt_ref, kt_ref, q_ref, k_ref, v_ref, aq_ref, ak_ref, o_ref, do_ref, lse_ref, dq_ref, dk_ref, dv_ref,
             dfq_ref, dfk_ref, acc_s, row_s, delta_s):
        i = pl.program_id(2)
        qi, ki = qt_ref[i], kt_ref[i]

        @pl.when(i == 0)
        def _():
            dk_ref[...] = jnp.zeros(dk_ref.shape, F32)
            dv_ref[...] = jnp.zeros(dv_ref.shape, F32)
            dfk_ref[...] = jnp.zeros(dfk_ref.shape, F32)

        @pl.when(ki == 0)
        def _():
            acc_s[...] = jnp.zeros(acc_s.shape, F32)
            row_s[...] = jnp.zeros(row_s.shape, F32)
            prod = do_ref[...] * o_ref[...]
            delta_s[...] = jnp.concatenate(
                [_lanes(jnp.sum(jnp.where(msk, prod, 0.0), axis=-1, keepdims=True)) for msk in _head_masks()],
                axis=1)

        def step(masked):
            heads, k = _fox_bwd_common(q_ref, k_ref, v_ref, aq_ref, ak_ref, do_ref, lse_ref, delta_s, masked)
            rows = pl.ds(pl.multiple_of(ki * t, t), t)
            for h, ((qt, doh, p, ds), msk) in enumerate(zip(heads, _head_masks())):
                acc_s[h] += _bdot(ds, k, "nn")
                row_s[h] += _lanes(jnp.sum(ds, axis=-1, keepdims=True))
                dv_ref[rows, :] += _bdot(p, doh, "tn")
                dk_ref[rows, :] += jnp.where(msk, _bdot(ds, qt, "tn"), 0.0)
                dfk_ref[ki, h:h + 1, :] -= jnp.sum(ds, axis=0, keepdims=True)

        _on_blocks(qi, ki, step)

        @pl.when(ki == qi)
        def _():
            dq_ref[...] = jnp.where(_head_masks()[0], acc_s[0], acc_s[1]) * ATT_SCALE
            dfq_ref[...] = jnp.concatenate([row_s[0], row_s[1]], axis=1)

    out = jax.ShapeDtypeStruct((nb, seq, tokw), F32)
    stat = pltpu.VMEM((2, t, LANE), F32)
    return pl.pallas_call(
        body, name=name,
        grid_spec=pltpu.PrefetchScalarGridSpec(
            num_scalar_prefetch=2, grid=(nb, hp0, n_pairs),
            in_specs=[q_spec(), k_spec(), k_spec(hp0), q_spec(), k_spec(), q_spec(), q_spec(), q_spec(wide=True)],
            out_specs=[q_spec(), whole, whole, q_spec(wide=True), dfk_spec],
            scratch_shapes=[stat, stat, pltpu.VMEM((t, 2 * LANE), F32)]),
        out_shape=[out, out, out, jax.ShapeDtypeStruct((nb, seq, 2 * tokw), F32),
                   jax.ShapeDtypeStruct((nb, hp0, nblk, SUBLANE, t), F32)],
        compiler_params=_params(),
    )(*tables, proj3, kvf3, kvf3, aq3, ak3, o3, dmixin3, lse3)


def _peer(k):
    x, y, c = lax.axis_index("x"), lax.axis_index("y"), lax.axis_index("c")
    return (1 - x if k & 4 else x, 1 - y if k & 2 else y, 1 - c if k & 1 else c)


def _dev_index(p):
    return 4 * p[0] + 2 * p[1] + p[2]


def _exchange(send, *, gather, name):
    block = send.shape[-2:]

    def body(s_ref, o_ref, send_sems, recv_sems, local_sem):
        me = _peer(0)
        mine = pltpu.make_async_copy(s_ref if gather else s_ref.at[_dev_index(me)], o_ref.at[_dev_index(me)],
                                     local_sem)
        mine.start()
        sends, recvs = [], []
        for k in range(1, N_DEV):
            peer = _peer(k)
            src = s_ref if gather else s_ref.at[_dev_index(peer)]
            sends.append(pltpu.make_async_remote_copy(
                src_ref=src, dst_ref=o_ref.at[_dev_index(me)], send_sem=send_sems.at[k - 1],
                recv_sem=recv_sems.at[k - 1], device_id=peer, device_id_type=MESH_T))
            recvs.append(pltpu.make_async_remote_copy(
                src_ref=src, dst_ref=o_ref.at[_dev_index(peer)], send_sem=send_sems.at[k - 1],
                recv_sem=recv_sems.at[k - 1], device_id=peer, device_id_type=MESH_T))
        for cp in sends:
            cp.start()
        for cp in recvs:
            cp.wait_recv()
        for cp in sends:
            cp.wait_send()
        mine.wait()

    return pl.pallas_call(
        body, name=name,
        in_specs=[pl.BlockSpec(memory_space=pltpu.HBM)], out_specs=pl.BlockSpec(memory_space=pltpu.HBM),
        out_shape=jax.ShapeDtypeStruct((N_DEV,) + block, send.dtype),
        scratch_shapes=[pltpu.SemaphoreType.DMA((N_DEV - 1,)), pltpu.SemaphoreType.DMA((N_DEV - 1,)),
                        pltpu.SemaphoreType.DMA],
    )(send)


_HBM = pl.BlockSpec(memory_space=pltpu.HBM)
CHIP_RELATIONS = (2, 4, 6)


def _chip_index(p):
    return 2 * p[0] + p[1]


def _run_copies(sends, recvs):
    for cp in sends:
        cp.start()
    for cp in recvs:
        cp.wait_recv()
    for cp in sends:
        cp.wait_send()


def _gather_shards(whole, halved, *, name):
    nw, nh = len(whole), len(halved)
    n = nw + nh
    n_sem = 3 * nw + 6 * nh

    def body(*refs):
        ins, outs, send_sems, recv_sems = refs[:n], refs[n:2 * n], refs[2 * n], refs[2 * n + 1]
        me, sib = _peer(0), _peer(1)
        q, c = _chip_index(me), me[2]

        def copy(src, dst, s, to):
            return pltpu.make_async_remote_copy(src_ref=src, dst_ref=dst, send_sem=send_sems.at[s],
                                                recv_sem=recv_sems.at[s], device_id=to, device_id_type=MESH_T)

        sends, recvs, passes = [], [], []
        for j, k in enumerate(CHIP_RELATIONS):
            peer = _peer(k)
            pq = _chip_index(peer)
            for i in range(nw):
                sends.append(copy(ins[i], outs[i].at[q], 3 * i + j, peer))
                recvs.append(copy(ins[i], outs[i].at[pq], 3 * i + j, peer))
            for i in range(nh):
                src, out, s = ins[nw + i], outs[nw + i], 3 * nw + 6 * i + j
                sends.append(copy(src.at[c], out.at[q, c], s, peer))
                passes.append((copy(src.at[c], out.at[pq, c], s, peer), copy(out.at[pq, c], out.at[pq, c], s + 3, sib),
                               copy(out.at[pq, c], out.at[pq, 1 - c], s + 3, sib)))
        for cp in sends:
            cp.start()
        for arrival, hand_over, _ in passes:
            arrival.wait_recv()
            hand_over.start()
        for cp in recvs:
            cp.wait_recv()
        for _, _, from_sibling in passes:
            from_sibling.wait_recv()
        for cp in sends + [hand_over for _, hand_over, _ in passes]:
            cp.wait_send()

    arrays = list(whole) + list(halved)
    return pl.pallas_call(
        body, name=name, in_specs=[_HBM] * n, out_specs=[_HBM] * n,
        out_shape=[jax.ShapeDtypeStruct((N_CHIP,) + a.shape, a.dtype) for a in arrays],
        scratch_shapes=[pltpu.SemaphoreType.DMA((n_sem,)), pltpu.SemaphoreType.DMA((n_sem,))],
    )(*arrays)


def _to_sibling(grads, *, name):
    n = len(grads)

    def body(*refs):
        ins, outs, send_sems, recv_sems = refs[:n], refs[n:2 * n], refs[2 * n], refs[2 * n + 1]
        c = lax.axis_index("c")
        sends = [pltpu.make_async_remote_copy(src_ref=ins[i].at[:, 1 - c], dst_ref=outs[i], send_sem=send_sems.at[i],
                                              recv_sem=recv_sems.at[i], device_id=_peer(1), device_id_type=MESH_T)
                 for i in range(n)]
        _run_copies(sends, sends)

    return pl.pallas_call(
        body, name=name, in_specs=[_HBM] * n, out_specs=[_HBM] * n,
        out_shape=[jax.ShapeDtypeStruct(g.shape[:1] + g.shape[2:], g.dtype) for g in grads],
        scratch_shapes=[pltpu.SemaphoreType.DMA((n,)), pltpu.SemaphoreType.DMA((n,))],
    )(*grads)


def _pair_add(grads, from_sibling, qc, *, name):
    _, _, rows, cols = grads.shape
    tile = _pick(rows, 1024, SUBLANE)

    def body(qc_ref, g_ref, s_ref, o_ref):
        del qc_ref
        o_ref[...] = g_ref[...] + s_ref[...]

    spec = pl.BlockSpec((None, tile, cols), lambda j, i, qc: (j, i, 0))
    return pl.pallas_call(
        body, name=name,
        grid_spec=pltpu.PrefetchScalarGridSpec(
            num_scalar_prefetch=1, grid=(N_CHIP, rows // tile),
            in_specs=[pl.BlockSpec((None, None, tile, cols), lambda j, i, qc: (j, qc[1], i, 0)), spec],
            out_specs=spec),
        out_shape=jax.ShapeDtypeStruct((N_CHIP, rows, cols), F32), compiler_params=_params(),
    )(qc, grads, from_sibling)


def _to_chips(sums, *, name):
    n = len(sums)

    def body(*refs):
        ins, outs, send_sems, recv_sems = refs[:n], refs[n:2 * n], refs[2 * n], refs[2 * n + 1]
        sends, recvs = [], []
        for j, k in enumerate(CHIP_RELATIONS):
            peer = _peer(k)
            for i in range(n):
                sem = dict(send_sem=send_sems.at[3 * i + j], recv_sem=recv_sems.at[3 * i + j], device_id=peer,
                           device_id_type=MESH_T)
                src = ins[i].at[_chip_index(peer)]
                sends.append(pltpu.make_async_remote_copy(src_ref=src, dst_ref=outs[i].at[j], **sem))
                recvs.append(pltpu.make_async_remote_copy(src_ref=src, dst_ref=outs[i].at[j], **sem))
        _run_copies(sends, recvs)

    return pl.pallas_call(
        body, name=name, in_specs=[_HBM] * n, out_specs=[_HBM] * n,
        out_shape=[jax.ShapeDtypeStruct((3,) + g.shape[1:], g.dtype) for g in sums],
        scratch_shapes=[pltpu.SemaphoreType.DMA((3 * n,)), pltpu.SemaphoreType.DMA((3 * n,))],
    )(*sums)


def _swap_halves(arrays, *, name):
    n = len(arrays)

    def body(*refs):
        outs, send_sems, recv_sems = refs[n:2 * n], refs[2 * n], refs[2 * n + 1]
        c = lax.axis_index("c")
        sib = _peer(1)
        sends, recvs = [], []
        for i in range(n):
            sem = dict(send_sem=send_sems.at[i], recv_sem=recv_sems.at[i], device_id=sib, device_id_type=MESH_T)
            sends.append(pltpu.make_async_remote_copy(src_ref=outs[i].at[c], dst_ref=outs[i].at[c], **sem))
            recvs.append(pltpu.make_async_remote_copy(src_ref=outs[i].at[c], dst_ref=outs[i].at[1 - c], **sem))
        _run_copies(sends, recvs)

    return pl.pallas_call(
        body, name=name, in_specs=[_HBM] * n, out_specs=[_HBM] * n,
        out_shape=[jax.ShapeDtypeStruct(a.shape, a.dtype) for a in arrays],
        input_output_aliases={i: i for i in range(n)},
        scratch_shapes=[pltpu.SemaphoreType.DMA((n,)), pltpu.SemaphoreType.DMA((n,))],
    )(*arrays)


def _adam_math(g, w, m, v):
    bc1 = 1.0 - ADAM_B1 ** ADAM_STEP
    bc2 = 1.0 - ADAM_B2 ** ADAM_STEP
    m_new = ADAM_B1 * m + (1.0 - ADAM_B1) * g
    v_new = ADAM_B2 * v + (1.0 - ADAM_B2) * (g * g)
    delta = -ADAM_LR * ((m_new / bc1) / (jnp.sqrt(v_new / bc2) + ADAM_EPS) + ADAM_WD * w)
    return delta, m_new, v_new


def _adamw_half(sums, parts, w, m, v, qc, *, name):
    _, rows, cols = sums.shape
    tile = _pick(rows, 384, SUBLANE)
    n_parts = parts.shape[0]

    def body(qc_ref, g_ref, p_ref, w_ref, m_ref, v_ref, go_ref, do_ref, mo_ref, vo_ref):
        del qc_ref
        g = g_ref[...]
        for k in range(n_parts):
            g = g + p_ref[k]
        delta, m_new, v_new = _adam_math(g, w_ref[...], m_ref[...], v_ref[...])
        go_ref[...] = g
        do_ref[...] = delta
        mo_ref[...] = m_new
        vo_ref[...] = v_new

    half = pl.BlockSpec((None, tile, cols), lambda i, qc: (qc[1], i, 0))
    shape = jax.ShapeDtypeStruct((2, rows, cols), F32)
    return pl.pallas_call(
        body, name=name,
        grid_spec=pltpu.PrefetchScalarGridSpec(
            num_scalar_prefetch=1, grid=(rows // tile,),
            in_specs=[pl.BlockSpec((None, tile, cols), lambda i, qc: (qc[0], i, 0)),
                      pl.BlockSpec((n_parts, tile, cols), lambda i, qc: (0, i, 0)), half, half, half],
            out_specs=[half] * 4),
        out_shape=[shape] * 4, compiler_params=_params(),
    )(qc, sums, parts, w, m, v)


def _adamw(parts, w, m, v, *, name):
    _, rows, cols = parts.shape
    tile = _pick(rows, 256, SUBLANE)

    def body(p_ref, w_ref, m_ref, v_ref, o_ref):
        g = p_ref[0]
        for i in range(1, N_DEV):
            g = g + p_ref[i]
        delta, m_new, v_new = _adam_math(g, w_ref[...], m_ref[...], v_ref[...])
        o_ref[0] = g
        o_ref[1] = delta
        o_ref[2] = m_new
        o_ref[3] = v_new

    spec = pl.BlockSpec((tile, cols), lambda i: (i, 0))
    return pl.pallas_call(
        body, name=name, grid=(rows // tile,),
        in_specs=[pl.BlockSpec((N_DEV, tile, cols), lambda i: (0, i, 0)), spec, spec, spec],
        out_specs=pl.BlockSpec((4, tile, cols), lambda i: (0, i, 0)),
        out_shape=jax.ShapeDtypeStruct((4, rows, cols), F32), compiler_params=_params(),
    )(parts, w, m, v)


def _layout(shapes, names, align):
    out, off = [], 0
    for n in names:
        size = math.prod(shapes[n])
        out.append((n, tuple(shapes[n]), off, size))
        off += _round_up(size, align)
    return out, off


def _pack(arrays, layout, total, lead=()):
    parts = []
    for i, (n, _, off, size) in enumerate(layout):
        end = layout[i + 1][2] if i + 1 < len(layout) else total
        flat = arrays[n].reshape(lead + (size,))
        if end - off > size:
            flat = jnp.pad(flat, [(0, 0)] * len(lead) + [(0, end - off - size)])
        parts.append(flat)
    return jnp.concatenate(parts, axis=len(lead))


def _unpack(flat, layout, lead=()):
    return {n: flat[..., off:off + size].reshape(lead + shape) for n, shape, off, size in layout}


def _to_shards(full, axis):
    shp = full.shape
    return jnp.moveaxis(full.reshape(shp[:axis] + (N_CHIP, shp[axis] // N_CHIP) + shp[axis + 1:]), axis, 0)


def _from_shards(shards, axis):
    x = jnp.moveaxis(shards, 0, axis)
    shp = x.shape
    return x.reshape(shp[:axis] + (shp[axis] * shp[axis + 1],) + shp[axis + 2:])


def _pad_cols(w, per, padded):
    lead = w.shape[:-1]
    x = w.reshape(lead + (-1, per))
    x = jnp.pad(x, [(0, 0)] * len(lead) + [(0, 0), (0, padded - per)])
    return x.reshape(lead + (-1,))


def _unpad_cols(w, per, padded):
    lead = w.shape[:-1]
    return w.reshape(lead + (-1, padded))[..., :per].reshape(lead + (-1,))


def _local_step(x, mem, target, W):
    nb, seq, d = x.shape
    n = nb * seq
    tokw = d - MEM_WIDTH
    heads = tokw // HEAD_DIM
    mlen = mem.shape[1]
    dff2 = W["ffn_w_up"].shape[-1]
    per = dff2 // N_CHIP
    per_p = _round_up(per, LANE)
    fp = 2 * per_p
    kvw = 2 * tokw + heads
    kvp = 2 * tokw + LANE
    gate_block = 2 * tokw // LANE

    x2d = x.reshape(n, d)
    mem2d = mem.reshape(nb * mlen, d)
    t2d = target.reshape(n, d)
    row = lambda a: a.reshape(1, -1)
    ones_tok = jnp.ones((1, tokw), F32)

    pool_bd = jax.scipy.linalg.block_diag(*[W["a_pool_w"][0, i] for i in range(len(POOL_WINDOWS))]).astype(BF16)
    kv_w = jnp.pad(W["kv_w"], ((0, 0), (0, kvp - kvw)))
    fb = jnp.pad(W["f_b"], (0, LANE - heads)).reshape(1, LANE)
    w_up = [_pad_cols(W["ffn_w_up"][l], per, per_p) for l in range(DEPTH)]
    w_down = [jnp.pad(W["ffn_w_down"][l].reshape(2, per, d), ((0, 0), (0, per_p - per), (0, 0))).reshape(fp, d)
              for l in range(DEPTH)]
    conv_w = [jnp.pad(_pad_cols(W["ffn_conv_w"][l], per, per_p), ((0, SUBLANE - CONV_WIDTH), (0, 0)))
              for l in range(DEPTH)]
    conv_b = [_pad_cols(W["ffn_conv_b"][l], per, per_p).reshape(1, 2 * fp) for l in range(DEPTH)]
    w_in = [W["a_w_in"][0], W["b_w_q"][0]]
    w_out = [W["a_w_out"][0], W["b_w_out"][0]]

    saved = []
    cur = x2d
    for l in range(DEPTH):
        s = {"x_in": cur}
        memkv = _mm(mem2d, W["mem_w_kv"][l], "nn", name=f"memkv{l}").reshape(nb, mlen, 2 * MEM_WIDTH)
        if l == 0:
            proj = _mm(cur, w_in[l], "nn", name="proj0")
            pooled = _pool_fwd(proj.reshape(nb, seq, d), tokw, name="pool_fwd").reshape(n, tokw)
            tok = _mm(pooled, pool_bd, "nn", name="pool_mix")
            scale = W["a_pool_scale"].reshape(1, tokw)
            s.update(pooled=pooled, mixed=tok, scale=scale)
        else:
            kvf = _mm(cur, kv_w, "nn", tn=kvp, name="kvf")
            kvf3 = kvf.reshape(nb, seq, kvp)
            gsum = _gate_fwd(kvf3, fb, gate_block, name="gate_fwd")[:, :, :heads]
            aq3, ak3 = _bias_lanes(gsum)
            proj = _mm(cur, w_in[l], "nn", name="proj1")
            o3, lse3 = _fox_fwd(proj.reshape(nb, seq, d), kvf3, aq3, ak3, tokw, name="fox_fwd")
            tok = o3.reshape(n, tokw)
            scale = ones_tok
            s.update(kvf3=kvf3, aq3=aq3, ak3=ak3, o3=o3, lse3=lse3)
        mixin = _memattn_fwd(tok, proj, memkv, scale, seq=seq, name=f"memattn_fwd{l}")
        mix = _mm(mixin, w_out[l], "nn", name=f"mix{l}")
        x1 = _ln_fwd(cur, mix, row(W["ln1_g"][l]), row(W["ln1_b"][l]), name=f"ln1_fwd{l}")
        up = _mm(x1, w_up[l], "nn", tn=per_p, name=f"ffn_up{l}")
        act = _convgate_fwd(up.reshape(nb, seq, 2 * fp), conv_w[l], conv_b[l], name=f"convgate_fwd{l}")
        act = act.reshape(n, fp)
        ffn = _mm(act, w_down[l], "nn", tk=fp, name=f"ffn_down{l}")
        s.update(proj=proj, memkv=memkv, mixin=mixin, mix=mix, x1=x1, up=up, act=act, ffn=ffn)
        saved.append(s)
        if l + 1 < DEPTH:
            cur = _ln_fwd(x1, ffn, row(W["ln2_g"][l]), row(W["ln2_b"][l]), name=f"ln2_fwd{l}")

    G = {}
    ln_g = {k: [None] * DEPTH for k in ("ln1_g", "ln1_b", "ln2_g", "ln2_b")}
    stack = {k: [None] * DEPTH for k in ("mem_w_kv", "ffn_w_up", "ffn_conv_w", "ffn_conv_b", "ffn_w_down")}
    dx_terms = None
    loss = None
    for l in reversed(range(DEPTH)):
        s = saved[l]
        g2 = row(W["ln2_g"][l])
        if l == DEPTH - 1:
            dres, dffn, loss, dg, db = _final_ln_loss(s["x1"], s["ffn"], t2d, g2, row(W["ln2_b"][l]),
                                                      name="final_ln_loss")
        else:
            dres, dffn, dg, db = _ln_bwd(s["x1"], s["ffn"], g2, dx_terms, name=f"ln2_bwd{l}")
        ln_g["ln2_g"][l], ln_g["ln2_b"][l] = dg[0], db[0]
        dact = _mm(dffn, w_down[l], "nt", tn=fp, name=f"ffn_down_dx{l}")
        stack["ffn_w_down"][l] = _mm(s["act"], dffn, "tn", tm=per_p, tk=512, trim=("rows", per),
                                     name=f"ffn_down_dw{l}")
        du3, dg3, dcw, dcb = _convgate_bwd(s["up"].reshape(nb, seq, 2 * fp), dact.reshape(nb, seq, fp), conv_w[l],
                                           conv_b[l], name=f"convgate_bwd{l}")
        du, dgt = du3.reshape(n, fp), dg3.reshape(n, fp)
        dx1_u = _mm(du, w_up[l], "nt", tk=per_p, name=f"ffn_up_dx_u{l}")
        dx1_ffn = _mm(dgt, w_up[l], "nt", tk=per_p, b_col0=fp, adds=[dx1_u], name=f"ffn_up_dx_g{l}")
        stack["ffn_w_up"][l] = [_mm(s["x1"], part, "tn", tm=d, tn=per_p, tk=512, trim=("cols", per),
                                    name=f"ffn_up_dw_{nm}{l}") for nm, part in (("u", du), ("g", dgt))]
        stack["ffn_conv_w"][l] = dcw[:CONV_WIDTH]
        stack["ffn_conv_b"][l] = dcb[0]
        dres1, dmix, dg, db = _ln_bwd(s["x_in"], s["mix"], row(W["ln1_g"][l]), [dres, dx1_ffn], name=f"ln1_bwd{l}")
        ln_g["ln1_g"][l], ln_g["ln1_b"][l] = dg[0], db[0]
        dmixin = _mm(dmix, w_out[l], "nt", name=f"mix_dx{l}")
        d_w_out = _mm(s["mixin"], dmix, "tn", tm=d, tk=512, name=f"mix_dw{l}")
        if l == 0:
            G["a_w_out"] = d_w_out[None]
            dmixed, dscale = _scale_bwd(dmixin, s["mixed"], s["scale"], name="scale_bwd")
            G["a_pool_scale"] = dscale
            dpooled = _mm(dmixed, pool_bd, "nt", name="pool_mix_dx")
            dpw = _mm(s["pooled"], dmixed, "tn", tm=tokw, tk=512, name="pool_mix_dw")
            grp = tokw // len(POOL_WINDOWS)
            G["a_pool_w"] = jnp.stack([dpw[i * grp:(i + 1) * grp, i * grp:(i + 1) * grp]
                                       for i in range(len(POOL_WINDOWS))])[None]
            dtok = _pool_bwd(dpooled.reshape(nb, seq, tokw), name="pool_bwd").reshape(n, tokw)
            extra = []
        else:
            G["b_w_out"] = d_w_out[None]
            p3 = s["proj"].reshape(nb, seq, d)
            dm3 = dmixin.reshape(nb, seq, d)
            dq3, dk3, dv3, dfq3, dfk = _fox_bwd(p3, s["kvf3"], s["aq3"], s["ak3"], s["o3"], dm3, s["lse3"], tokw,
                                                name="fox_bwd")
            dtok = dq3.reshape(n, tokw)
            dfk = jnp.swapaxes(dfk[:, :, :, :2, :], 2, 3).reshape(nb, heads, seq)
            dgsum = jnp.swapaxes(dfk, 1, 2) + dfq3[:, :, ::LANE]
            dgsum = jnp.pad(dgsum, ((0, 0), (0, 0), (0, LANE - heads)))
            df3, dfb = _gate_bwd(s["kvf3"], fb, dgsum, gate_block, heads, name="gate_bwd")
            G["f_b"] = dfb[0, :heads]
            dkvf = [(dk3.reshape(n, tokw), 0, "k"), (dv3.reshape(n, tokw), tokw, "v"),
                    (df3.reshape(n, LANE), 2 * tokw, "f")]
            dx_kv = []
            for part, col0, nm in dkvf:
                dx_kv = [_mm(part, kv_w, "nt", b_col0=col0, adds=dx_kv, name=f"kvf_dx_{nm}")]
            extra = dx_kv
            G["kv_w"] = jnp.concatenate([_mm(s["x_in"], part, "tn", tm=d, tk=512, name=f"kvf_dw_{nm}")
                                         for part, _, nm in dkvf], axis=1)[:, :kvw]
        dproj, dmemkv = _memattn_bwd(dmixin, dtok, s["proj"], s["memkv"], seq=seq, name=f"memattn_bwd{l}")
        stack["mem_w_kv"][l] = _mm(mem2d, dmemkv.reshape(nb * mlen, 2 * MEM_WIDTH), "tn", tm=d, tk=512,
                                   name=f"memkv_dw{l}")
        G["a_w_in" if l == 0 else "b_w_q"] = _mm(s["x_in"], dproj, "tn", tm=d, tk=512, name=f"proj_dw{l}")[None]
        if l == 0:
            grad_x = _mm(dproj, w_in[l], "nt", adds=[dres1], name="proj_dx0")
        else:
            dx_terms = [dres1, _mm(dproj, w_in[l], "nt", name="proj_dx1")] + extra
    for k, v in ln_g.items():
        G[k] = jnp.stack(v)
    G["mem_w_kv"] = jnp.stack(stack["mem_w_kv"])
    G["ffn_w_up"] = jnp.stack([jnp.concatenate(ug, axis=0) for ug in stack["ffn_w_up"]], axis=1)
    G["ffn_conv_w"] = jnp.stack([_unpad_cols(g, per, per_p) for g in stack["ffn_conv_w"]])
    G["ffn_conv_b"] = jnp.stack([_unpad_cols(g, per, per_p) for g in stack["ffn_conv_b"]])
    G["ffn_w_down"] = jnp.stack([g.reshape(N_CHIP, per // 2, d) for g in stack["ffn_w_down"]], axis=1)
    return loss[0, 0], grad_x.reshape(nb, seq, d), G


def kernel(x, mem, a_w_in, a_pool_w, a_pool_scale, a_w_out, b_w_q, b_w_out, kv_w, f_b, mem_w_kv, ln1_g, ln1_b, ln2_g, ln2_b, ffn_w_up, ffn_conv_w, ffn_conv_b, ffn_w_down, loss_target, m_a_w_in, m_a_pool_w, m_a_pool_scale, m_a_w_out, m_b_w_q, m_b_w_out, m_kv_w, m_f_b, m_mem_w_kv, m_ln1_g, m_ln1_b, m_ln2_g, m_ln2_b, m_ffn_w_up, m_ffn_conv_w, m_ffn_conv_b, m_ffn_w_down, v_a_w_in, v_a_pool_w, v_a_pool_scale, v_a_w_out, v_b_w_q, v_b_w_out, v_kv_w, v_f_b, v_mem_w_kv, v_ln1_g, v_ln1_b, v_ln2_g, v_ln2_b, v_ffn_w_up, v_ffn_conv_w, v_ffn_conv_b, v_ffn_w_down):
    w_loc = dict(a_w_in=a_w_in, a_pool_w=a_pool_w, a_pool_scale=a_pool_scale, a_w_out=a_w_out, b_w_q=b_w_q,
                 b_w_out=b_w_out, kv_w=kv_w, f_b=f_b, mem_w_kv=mem_w_kv, ln1_g=ln1_g, ln1_b=ln1_b, ln2_g=ln2_g,
                 ln2_b=ln2_b, ffn_w_up=ffn_w_up, ffn_conv_w=ffn_conv_w, ffn_conv_b=ffn_conv_b, ffn_w_down=ffn_w_down)
    m_loc = dict(a_w_in=m_a_w_in, a_pool_w=m_a_pool_w, a_pool_scale=m_a_pool_scale, a_w_out=m_a_w_out,
                 b_w_q=m_b_w_q, b_w_out=m_b_w_out, kv_w=m_kv_w, f_b=m_f_b, mem_w_kv=m_mem_w_kv, ln1_g=m_ln1_g,
                 ln1_b=m_ln1_b, ln2_g=m_ln2_g, ln2_b=m_ln2_b, ffn_w_up=m_ffn_w_up, ffn_conv_w=m_ffn_conv_w,
                 ffn_conv_b=m_ffn_conv_b, ffn_w_down=m_ffn_w_down)
    v_loc = dict(a_w_in=v_a_w_in, a_pool_w=v_a_pool_w, a_pool_scale=v_a_pool_scale, a_w_out=v_a_w_out,
                 b_w_q=v_b_w_q, b_w_out=v_b_w_out, kv_w=v_kv_w, f_b=v_f_b, mem_w_kv=v_mem_w_kv, ln1_g=v_ln1_g,
                 ln1_b=v_ln1_b, ln2_g=v_ln2_g, ln2_b=v_ln2_b, ffn_w_up=v_ffn_w_up, ffn_conv_w=v_ffn_conv_w,
                 ffn_conv_b=v_ffn_conv_b, ffn_w_down=v_ffn_w_down)
    x_i, y_i, c = lax.axis_index("x"), lax.axis_index("y"), lax.axis_index("c")
    q = 2 * x_i + y_i
    qc = jnp.stack([q, c]).astype(jnp.int32)
    shapes = {k: v.shape for k, v in w_loc.items()}

    def halves(a):
        if a.ndim == 3 and a.shape[0] == 2:
            return a
        rows = math.prod(a.shape[:-1])
        return a.reshape(2, rows // 2, a.shape[-1])

    own = {k: (w_loc[k] if k in GATHER_F32 else w_loc[k].astype(BF16)) for k in SHARDED}
    gathered = _gather_shards([own["a_pool_scale"]], [halves(own[k]) for k in BIG], name="gather_weights")
    W = {k: _from_shards(lax.dynamic_update_slice_in_dim(g.reshape((N_CHIP,) + shapes[k]), own[k][None], q, axis=0),
                         SHARD_AXIS[k])
         for k, g in zip(("a_pool_scale",) + BIG, gathered)}
    for k in REPLICATED:
        W[k] = w_loc[k]

    loss_part, grad_x, G = _local_step(x, mem, loss_target, W)
    loss = lax.psum(loss_part, ("x", "y", "c"))

    g_chip = [G[k] if k in GRADS_BY_CHIP else _to_shards(G[k], SHARD_AXIS[k]) for k in BIG]
    g_chip = [g.reshape((N_CHIP,) + halves(w_loc[k]).shape) for k, g in zip(BIG, g_chip)]
    from_sib = _to_sibling(g_chip, name="grads_to_sibling")
    sums = [_pair_add(g, s, qc, name=f"pair_add_{k}") for k, g, s in zip(BIG, g_chip, from_sib)]
    parts = _to_chips(sums, name="scatter_grads")
    res = []
    for k, g, p in zip(BIG, sums, parts):
        res.extend(_adamw_half(g, p, halves(w_loc[k]), halves(m_loc[k]), halves(v_loc[k]), qc, name=f"adamw_{k}"))
    res = _swap_halves(res, name="swap_halves")
    out = {k: [r.reshape(shapes[k]) for r in res[4 * i:4 * i + 4]] for i, k in enumerate(BIG)}

    lay_r, tot_r = _layout(shapes, REPLICATED, PACK_COLS)
    rep_rows = tot_r // PACK_COLS
    rows = _round_up(rep_rows + 1, SUBLANE)
    scale_w = shapes["a_pool_scale"][-1]

    def small(rep, scale_row):
        lead = scale_row.shape[:-2]
        rep = jnp.broadcast_to(_pack(rep, lay_r, tot_r).reshape(rep_rows, PACK_COLS), lead + (rep_rows, PACK_COLS))
        pad = [(0, 0)] * len(lead)
        return jnp.concatenate([rep, jnp.pad(scale_row, pad + [(0, rows - rep_rows - 1), (0, PACK_COLS - scale_w)])],
                               axis=-2)

    g_scale = jnp.repeat(_to_shards(G["a_pool_scale"], 1), 2, axis=0)
    sm_parts = _exchange(small(G, g_scale), gather=False, name="scatter_small")
    sm = _adamw(sm_parts, *[small(d, d["a_pool_scale"]) for d in (w_loc, m_loc, v_loc)], name="adamw_small")
    out_r = _unpack(sm[:, :rep_rows].reshape(4, tot_r), lay_r, lead=(4,))
    for k in REPLICATED:
        out[k] = [out_r[k][a] for a in range(4)]
    out["a_pool_scale"] = [sm[a, rep_rows:rep_rows + 1, :scale_w] for a in range(4)]

    outs = [loss, grad_x]
    for a in range(4):
        for k in WEIGHTS:
            outs.append(out[k][a])
    return tuple(outs)
```

```python
import functools
import math

import jax
import jax.numpy as jnp
from jax import lax
from jax.experimental import pallas as pl
from jax.experimental.pallas import tpu as pltpu

F32 = jnp.float32
BF16 = jnp.bfloat16

HEAD_DIM = 64
MEM_HEADS = 4
MEM_WIDTH = MEM_HEADS * HEAD_DIM
POOL_WINDOWS = (2, 4, 8, 16)
MAX_WINDOW = 16
CONV_WIDTH = 3
DEPTH = 2
DN_ALPHA = (2.0 * DEPTH) ** 0.25
LN_EPS = 1e-5
ATT_SCALE = HEAD_DIM ** -0.5
NEG_BIG = -1e30

ADAM_LR = 0.001
ADAM_B1 = 0.9
ADAM_B2 = 0.999
ADAM_EPS = 1e-08
ADAM_WD = 0.01
ADAM_STEP = 10

LANE = 128
SUBLANE = 8
PACK_COLS = 1024
DW_ROWS = 1024
VMEM_LIMIT = 56 * 1024 * 1024
N_DEV = 8
N_CHIP = 4
MESH_T = pl.DeviceIdType.MESH

SHARDED = ("a_w_in", "a_pool_scale", "a_w_out", "b_w_q", "b_w_out", "kv_w", "mem_w_kv", "ffn_w_up",
           "ffn_conv_w", "ffn_w_down")
SHARD_AXIS = {"a_w_in": 1, "a_pool_scale": 1, "a_w_out": 1, "b_w_q": 1, "b_w_out": 1, "kv_w": 1, "mem_w_kv": 1,
              "ffn_w_up": 2, "ffn_conv_w": 2, "ffn_w_down": 1}
GATHER_F32 = ("a_pool_scale", "ffn_conv_w")
BIG = tuple(k for k in SHARDED if k != "a_pool_scale")
GRADS_BY_CHIP = ("ffn_w_up", "ffn_w_down")
REPLICATED = ("a_pool_w", "f_b", "ln1_g", "ln1_b", "ln2_g", "ln2_b", "ffn_conv_b")
WEIGHTS = ("a_w_in", "a_pool_w", "a_pool_scale", "a_w_out", "b_w_q", "b_w_out", "kv_w", "f_b", "mem_w_kv",
           "ln1_g", "ln1_b", "ln2_g", "ln2_b", "ffn_w_up", "ffn_conv_w", "ffn_conv_b", "ffn_w_down")


def _round_up(n, m):
    return -(-n // m) * m


def _pick(dim, pref, unit=LANE):
    if dim <= pref:
        return dim
    t = (pref // unit) * unit
    while t >= unit:
        if dim % t == 0:
            return t
        t -= unit
    raise ValueError(f"no tile for {dim} <= {pref}")


def _params():
    return pltpu.CompilerParams(vmem_limit_bytes=VMEM_LIMIT)


_DIMS = {"nn": ((1,), (0,)), "nt": ((1,), (1,)), "tn": ((0,), (0,))}


def _bdot(a, b, mode):
    return lax.dot_general(a.astype(BF16), b.astype(BF16), (_DIMS[mode], ((), ())), preferred_element_type=F32)


def _mm(a, b, mode, *, name, tm=512, tn=1024, tk=2048, adds=(), b_col0=0, trim=None, col_major=False):
    if mode == "nn":
        (M, K), (K2, N) = a.shape, b.shape
    elif mode == "nt":
        (M, K), (N, K2) = a.shape, b.shape
        K2 = K if b_col0 + K <= K2 else -1
    else:
        (K, M), (K2, N) = a.shape, b.shape
    assert K == K2 and (mode == "nt" or b_col0 == 0), (name, a.shape, b.shape)
    tm, tn = _pick(M, tm, SUBLANE if mode != "tn" else LANE), _pick(N, tn)
    tk = _pick(K, tk, LANE if mode != "tn" else SUBLANE)
    nk = K // tk
    assert b_col0 % tk == 0, (name, b_col0, tk)
    koff = b_col0 // tk
    n_add = len(adds)

    def body(*refs):
        a_ref, b_ref = refs[0], refs[1]
        add_refs = refs[2:2 + n_add]
        o_ref, acc_ref = refs[2 + n_add], refs[3 + n_add]
        part = _bdot(a_ref[...], b_ref[...], mode)

        def finish(r):
            for ar in add_refs:
                r = r + ar[...]
            if trim is not None:
                r = r[:, :trim[1]] if trim[0] == "cols" else r[:trim[1], :]
            o_ref[...] = r

        if nk == 1:
            finish(part)
        else:
            k = pl.program_id(2)

            @pl.when(k == 0)
            def _():
                acc_ref[...] = part

            @pl.when(k > 0)
            def _():
                acc_ref[...] += part

            @pl.when(k == nk - 1)
            def _():
                finish(acc_ref[...])

    def spec(block, index):
        if col_major:
            return pl.BlockSpec(block, lambda j, i, k: index(i, j, k))
        return pl.BlockSpec(block, index)

    if mode == "nn":
        a_spec = spec((tm, tk), lambda i, j, k: (i, k))
        b_spec = spec((tk, tn), lambda i, j, k: (k, j))
    elif mode == "nt":
        a_spec = spec((tm, tk), lambda i, j, k: (i, k))
        b_spec = spec((tn, tk), lambda i, j, k: (j, k + koff))
    else:
        a_spec = spec((tk, tm), lambda i, j, k: (k, i))
        b_spec = spec((tk, tn), lambda i, j, k: (k, j))
    o_spec = spec((tm, tn), lambda i, j, k: (i, j))
    out_spec, out_shape = o_spec, (M, N)
    if trim is not None and trim[0] == "cols":
        out_spec, out_shape = spec((None, tm, trim[1]), lambda i, j, k: (j, i, 0)), (N // tn, M, trim[1])
    elif trim is not None:
        out_spec, out_shape = spec((None, trim[1], tn), lambda i, j, k: (i, 0, j)), (M // tm, trim[1], N)
    acc_shape = (tm, tn) if nk > 1 else (SUBLANE, LANE)
    return pl.pallas_call(
        body, name=name, grid=(N // tn, M // tm, nk) if col_major else (M // tm, N // tn, nk),
        in_specs=[a_spec, b_spec] + [o_spec] * n_add, out_specs=out_spec,
        out_shape=jax.ShapeDtypeStruct(out_shape, F32),
        scratch_shapes=[pltpu.VMEM(acc_shape, F32)],
        compiler_params=_params(),
    )(a, b, *adds)


def _rowwise(fn, tiled, full, outs_tiled, outs_acc, *, rows, tile, name, acc_period=None):
    n_tiles = rows // tile
    period = n_tiles if acc_period is None else acc_period
    arrays, in_specs = [], []
    for t in tiled:
        arr, width, cb = t if isinstance(t, tuple) else (t, t.shape[1], 0)
        arrays.append(arr)
        in_specs.append(pl.BlockSpec((tile, width), lambda i, cb=cb: (i, cb)))
    for f in full:
        arr, spec = f if isinstance(f, tuple) else (f, None)
        arrays.append(arr)
        in_specs.append(spec if spec is not None else pl.BlockSpec(arr.shape, lambda i, nd=arr.ndim: (0,) * nd))
    out_shape, out_specs = [], []
    for width, dt in outs_tiled:
        out_shape.append(jax.ShapeDtypeStruct((rows, width), dt))
        out_specs.append(pl.BlockSpec((tile, width), lambda i: (i, 0)))
    for acc in outs_acc:
        shape, dt = acc[0], acc[1]
        out_shape.append(jax.ShapeDtypeStruct(shape, dt))
        out_specs.append(acc[2] if len(acc) > 2 else pl.BlockSpec(shape, lambda i, nd=len(shape): (0,) * nd))
    n_in, n_t, n_a = len(arrays), len(outs_tiled), len(outs_acc)

    def body(*refs):
        vals = [r[...] for r in refs[:n_in]]
        o_t, o_a = fn(*vals)
        for r, v in zip(refs[n_in:n_in + n_t], o_t):
            r[...] = v.astype(r.dtype)
        first = pl.program_id(0) % period == 0
        for r, v in zip(refs[n_in + n_t:n_in + n_t + n_a], o_a):
            v = v.reshape(r.shape)

            @pl.when(first)
            def _(r=r, v=v):
                r[...] = v

            @pl.when(jnp.logical_not(first))
            def _(r=r, v=v):
                r[...] += v

    return pl.pallas_call(
        body, name=name, grid=(n_tiles,), in_specs=in_specs, out_specs=out_specs, out_shape=out_shape,
        compiler_params=_params(),
    )(*arrays)


def _ln_stats(h):
    mu = jnp.mean(h, axis=-1, keepdims=True)
    d = h - mu
    var = jnp.mean(d * d, axis=-1, keepdims=True)
    rstd = lax.rsqrt(var + LN_EPS)
    return d * rstd, rstd


def _ln_bwd_math(h, g, dy):
    xhat, rstd = _ln_stats(h)
    dxhat = dy * g
    dh = rstd * (dxhat - jnp.mean(dxhat, axis=-1, keepdims=True)
                 - xhat * jnp.mean(dxhat * xhat, axis=-1, keepdims=True))
    return dh, jnp.sum(dy * xhat, axis=0, keepdims=True), jnp.sum(dy, axis=0, keepdims=True)


def _ln_fwd(x, r, g, b, *, name):
    n, d = x.shape

    def fn(x, r, g, b):
        xhat, _ = _ln_stats(DN_ALPHA * x + r)
        return (xhat * g + b,), ()

    return _rowwise(fn, [x, r], [g, b], [(d, F32)], [], rows=n, tile=_pick(n, 512, SUBLANE), name=name)[0]


def _ln_bwd(x, r, g, dys, *, name):
    n, d = x.shape
    n_dy = len(dys)

    def fn(x, r, *rest):
        dy = rest[0]
        for e in rest[1:n_dy]:
            dy = dy + e
        dh, dg, db = _ln_bwd_math(DN_ALPHA * x + r, rest[n_dy], dy)
        return (DN_ALPHA * dh, dh), (dg, db)

    return _rowwise(fn, [x, r, *dys], [g], [(d, F32), (d, F32)], [((1, d), F32), ((1, d), F32)],
                    rows=n, tile=_pick(n, 256, SUBLANE), name=name)


def _final_ln_loss(x, r, target, g, b, *, name):
    n, d = x.shape

    def fn(x, r, t, g, b):
        h = DN_ALPHA * x + r
        xhat, _ = _ln_stats(h)
        err = xhat * g + b - t
        loss = jnp.full((1, LANE), 0.5 * jnp.sum(err * err) / d, F32)
        dh, dg, db = _ln_bwd_math(h, g, err / d)
        return (DN_ALPHA * dh, dh), (loss, dg, db)

    return _rowwise(fn, [x, r, target], [g, b], [(d, F32), (d, F32)],
                    [((1, LANE), F32), ((1, d), F32), ((1, d), F32)],
                    rows=n, tile=_pick(n, 256, SUBLANE), name=name)


def _mem_heads(qm):
    lane = lax.broadcasted_iota(jnp.int32, (1, MEM_WIDTH), 1)
    for h in range(MEM_HEADS):
        msk = (lane >= h * HEAD_DIM) & (lane < (h + 1) * HEAD_DIM)
        yield msk, jnp.where(msk, qm, 0.0).astype(BF16)


def _mem_softmax(qh, k):
    s = _bdot(qh, k, "nt") * ATT_SCALE
    p = jnp.exp(s - jnp.max(s, axis=-1, keepdims=True))
    return p / jnp.sum(p, axis=-1, keepdims=True)


def _memattn_fwd(tok, proj, memkv, scale, *, seq, name):
    n, tokw = tok.shape
    d = tokw + MEM_WIDTH
    tile = _pick(seq, 512, SUBLANE)

    def fn(tok, qm, kv, scale):
        k, v = kv[:, :MEM_WIDTH].astype(BF16), kv[:, MEM_WIDTH:].astype(BF16)
        out = jnp.zeros(qm.shape, F32)
        for msk, qh in _mem_heads(qm):
            out = jnp.where(msk, _bdot(_mem_softmax(qh, k), v, "nn"), out)
        return (jnp.concatenate([tok * scale, out], axis=1),), ()

    kv_spec = pl.BlockSpec((None,) + memkv.shape[1:], lambda i: (i // (seq // tile), 0, 0))
    return _rowwise(fn, [tok, (proj, MEM_WIDTH, tokw // MEM_WIDTH)], [(memkv, kv_spec), scale], [(d, F32)], [],
                    rows=n, tile=tile, name=name)[0]


def _memattn_bwd(dmixin, dtok, proj, memkv, *, seq, name):
    n, tokw = dtok.shape
    d = tokw + MEM_WIDTH
    tile = _pick(seq, 512, SUBLANE)

    def fn(dmo, dtok, qm, kv):
        k, v = kv[:, :MEM_WIDTH].astype(BF16), kv[:, MEM_WIDTH:].astype(BF16)
        dq = jnp.zeros(qm.shape, F32)
        dk = jnp.zeros(k.shape, F32)
        dv = jnp.zeros(v.shape, F32)
        for msk, qh in _mem_heads(qm):
            p = _mem_softmax(qh, k)
            doh = jnp.where(msk, dmo, 0.0).astype(BF16)
            dv = dv + _bdot(p, doh, "tn")
            dp = _bdot(doh, v, "nt")
            ds = (p * (dp - jnp.sum(dp * p, axis=-1, keepdims=True))).astype(BF16)
            dq = jnp.where(msk, _bdot(ds, k, "nn") * ATT_SCALE, dq)
            dk = dk + _bdot(ds, qh, "tn") * ATT_SCALE
        return (jnp.concatenate([dtok, dq], axis=1),), (jnp.concatenate([dk, dv], axis=1),)

    tpe = seq // tile
    kv_spec = pl.BlockSpec((None,) + memkv.shape[1:], lambda i: (i // tpe, 0, 0))
    return _rowwise(fn, [(dmixin, MEM_WIDTH, tokw // MEM_WIDTH), dtok, (proj, MEM_WIDTH, tokw // MEM_WIDTH)],
                    [(memkv, kv_spec)], [(d, F32)], [(memkv.shape, F32, kv_spec)],
                    rows=n, tile=tile, name=name, acc_period=tpe)


def _scale_bwd(dmixin, mixed, scale, *, name):
    n, tokw = mixed.shape

    def fn(dt, mixed, scale):
        return (dt * scale,), (jnp.sum(dt * mixed, axis=0, keepdims=True),)

    return _rowwise(fn, [(dmixin, tokw, 0), mixed], [scale], [(tokw, F32)], [((1, tokw), F32)],
                    rows=n, tile=_pick(n, 512, SUBLANE), name=name)


def _chunk_rows(seq):
    return _pick(seq, 512, SUBLANE)


def _load_ext(ref, c, rows, before, after, seq):
    lo, hi = c * rows - before, (c + 1) * rows + after
    parts = []
    if lo < 0:
        parts.append(jnp.zeros((-lo, ref.shape[1]), F32))
    parts.append(ref[max(lo, 0):min(hi, seq), :])
    if hi > seq:
        parts.append(jnp.zeros((hi - seq, ref.shape[1]), F32))
    return parts[0] if len(parts) == 1 else jnp.concatenate(parts, axis=0)


def _down(x, k):
    return pltpu.roll(x, k, 0)


def _up(x, k):
    return pltpu.roll(x, x.shape[0] - k, 0)


def _window_sums(ext, shift, col0, group):
    lane = col0 + lax.broadcasted_iota(jnp.int32, (1, ext.shape[1]), 1)
    gidx = lane // group
    s = ext
    out = None
    k = 1
    for gi, w in enumerate(POOL_WINDOWS):
        while k < w:
            s = s + shift(s, k)
            k *= 2
        out = s if out is None else jnp.where(gidx >= gi, s, out)
    return out, jnp.left_shift(2, jnp.minimum(gidx, len(POOL_WINDOWS) - 1))


def _pool_fwd(proj3, tokw, *, name):
    nb, seq, _ = proj3.shape
    rows = _chunk_rows(seq)
    group = tokw // len(POOL_WINDOWS)

    def body(u_ref, o_ref):
        col0 = pl.program_id(1) * LANE
        for c in range(seq // rows):
            ext = _load_ext(u_ref, c, rows, MAX_WINDOW, 0, seq)
            sums, win = _window_sums(ext, _down, col0, group)
            t = c * rows + lax.broadcasted_iota(jnp.int32, (rows, 1), 0)
            count = jnp.minimum(t + 1, win).astype(F32)
            o_ref[c * rows:(c + 1) * rows, :] = sums[MAX_WINDOW:, :] / count - ext[MAX_WINDOW:, :]

    spec = pl.BlockSpec((None, seq, LANE), lambda b, j: (b, 0, j))
    return pl.pallas_call(
        body, name=name, grid=(nb, tokw // LANE), in_specs=[spec], out_specs=spec,
        out_shape=jax.ShapeDtypeStruct((nb, seq, tokw), F32), compiler_params=_params(),
    )(proj3)


def _pool_bwd(dp3, *, name):
    nb, seq, tokw = dp3.shape
    rows = _chunk_rows(seq)
    group = tokw // len(POOL_WINDOWS)

    def body(d_ref, o_ref):
        col0 = pl.program_id(1) * LANE
        for c in range(seq // rows):
            ext = _load_ext(d_ref, c, rows, 0, MAX_WINDOW, seq)
            lane = col0 + lax.broadcasted_iota(jnp.int32, (1, LANE), 1)
            win = jnp.left_shift(2, jnp.minimum(lane // group, len(POOL_WINDOWS) - 1))
            t = c * rows + lax.broadcasted_iota(jnp.int32, (rows + MAX_WINDOW, 1), 0)
            scaled = ext / jnp.minimum(t + 1, win).astype(F32)
            sums, _ = _window_sums(scaled, _up, col0, group)
            o_ref[c * rows:(c + 1) * rows, :] = sums[:rows, :] - ext[:rows, :]

    spec = pl.BlockSpec((None, seq, LANE), lambda b, j: (b, 0, j))
    return pl.pallas_call(
        body, name=name, grid=(nb, tokw // LANE), in_specs=[spec], out_specs=spec,
        out_shape=jax.ShapeDtypeStruct((nb, seq, tokw), F32), compiler_params=_params(),
    )(dp3)


def _conv3(ext, w_ref, b_ref):
    x1, x2 = _down(ext, 1), _down(ext, 2)
    return w_ref[0:1, :] * x2 + w_ref[1:2, :] * x1 + w_ref[2:3, :] * ext + b_ref[...], x1, x2


def _convgate_fwd(up3, cw, cb, *, name):
    nb, seq, c2 = up3.shape
    fp = c2 // 2
    nblk = fp // LANE
    rows = _chunk_rows(seq)

    def body(u_ref, g_ref, wu_ref, wg_ref, bu_ref, bg_ref, o_ref):
        for c in range(seq // rows):
            hu, _, _ = _conv3(_load_ext(u_ref, c, rows, SUBLANE, 0, seq), wu_ref, bu_ref)
            hg, _, _ = _conv3(_load_ext(g_ref, c, rows, SUBLANE, 0, seq), wg_ref, bg_ref)
            o_ref[c * rows:(c + 1) * rows, :] = (hg * jax.nn.sigmoid(hg) * hu)[SUBLANE:, :].astype(BF16)

    def col(off, r):
        return pl.BlockSpec((r, LANE), lambda b, j: (0, j + off))

    def act(off):
        return pl.BlockSpec((None, seq, LANE), lambda b, j: (b, 0, j + off))

    return pl.pallas_call(
        body, name=name, grid=(nb, nblk),
        in_specs=[act(0), act(nblk), col(0, SUBLANE), col(nblk, SUBLANE), col(0, 1), col(nblk, 1)],
        out_specs=act(0), out_shape=jax.ShapeDtypeStruct((nb, seq, fp), BF16), compiler_params=_params(),
    )(up3, up3, cw, cw, cb, cb)


def _convgate_bwd(up3, dact3, cw, cb, *, name):
    nb, seq, c2 = up3.shape
    fp = c2 // 2
    nblk = fp // LANE
    rows = _chunk_rows(seq)
    h = SUBLANE

    def body(u_ref, g_ref, da_ref, wu_ref, wg_ref, bu_ref, bg_ref, du_ref, dg_ref, dwu_ref, dwg_ref, dbu_ref,
             dbg_ref):
        @pl.when(pl.program_id(1) == 0)
        def _():
            for r in (dwu_ref, dwg_ref, dbu_ref, dbg_ref):
                r[...] = jnp.zeros(r.shape, F32)

        for c in range(seq // rows):
            eu = _load_ext(u_ref, c, rows, h, h, seq)
            eg = _load_ext(g_ref, c, rows, h, h, seq)
            da = _load_ext(da_ref, c, rows, h, h, seq)
            hu, u1, u2 = _conv3(eu, wu_ref, bu_ref)
            hg, g1, g2 = _conv3(eg, wg_ref, bg_ref)
            sig = jax.nn.sigmoid(hg)
            dhu = da * hg * sig
            dhg = da * hu * sig * (1.0 + hg * (1.0 - sig))
            for dh, w_ref, x0, x1, x2, dx_ref, dw_ref, db_ref in (
                    (dhu, wu_ref, eu, u1, u2, du_ref, dwu_ref, dbu_ref),
                    (dhg, wg_ref, eg, g1, g2, dg_ref, dwg_ref, dbg_ref)):
                dx = w_ref[2:3, :] * dh + w_ref[1:2, :] * _up(dh, 1) + w_ref[0:1, :] * _up(dh, 2)
                dx_ref[c * rows:(c + 1) * rows, :] = dx[h:h + rows, :].astype(BF16)
                core = dh[h:h + rows, :]
                for k, xk in ((0, x2), (1, x1), (2, x0)):
                    dw_ref[k:k + 1, :] += jnp.sum(core * xk[h:h + rows, :], axis=0, keepdims=True)
                db_ref[...] += jnp.sum(core, axis=0, keepdims=True)

    def col(off, r):
        return pl.BlockSpec((r, LANE), lambda j, b: (0, j + off))

    def act(off):
        return pl.BlockSpec((None, seq, LANE), lambda j, b: (b, 0, j + off))

    du, dg, dwu, dwg, dbu, dbg = pl.pallas_call(
        body, name=name, grid=(nblk, nb),
        in_specs=[act(0), act(nblk), act(0), col(0, SUBLANE), col(nblk, SUBLANE), col(0, 1), col(nblk, 1)],
        out_specs=[act(0), act(0), col(0, SUBLANE), col(0, SUBLANE), col(0, 1), col(0, 1)],
        out_shape=[jax.ShapeDtypeStruct((nb, seq, fp), BF16), jax.ShapeDtypeStruct((nb, seq, fp), BF16),
                   jax.ShapeDtypeStruct((SUBLANE, fp), F32), jax.ShapeDtypeStruct((SUBLANE, fp), F32),
                   jax.ShapeDtypeStruct((1, fp), F32), jax.ShapeDtypeStruct((1, fp), F32)],
        compiler_params=_params(),
    )(up3, up3, dact3, cw, cw, cb, cb)
    return du, dg, jnp.concatenate([dwu, dwg], axis=1), jnp.concatenate([dbu, dbg], axis=1)


def _scan_rows(x, shift, valid):
    row = lax.broadcasted_iota(jnp.int32, (x.shape[0], 1), 0)
    k = 1
    while k < x.shape[0]:
        x = x + jnp.where(valid(row, k), shift(x, k), 0.0)
        k *= 2
    return x


def _pick_row(x, r):
    row = lax.broadcasted_iota(jnp.int32, (x.shape[0], 1), 0)
    return jnp.sum(jnp.where(row == r, x, 0.0), axis=0, keepdims=True)


def _log_sigmoid(z):
    return jnp.minimum(z, 0.0) - jnp.log(1.0 + jnp.exp(-jnp.abs(z)))


def _gate_fwd(kvf3, fb, col_block, *, name):
    nb, seq, _ = kvf3.shape
    rows = _chunk_rows(seq)

    def body(f_ref, fb_ref, o_ref):
        carry = jnp.zeros((1, LANE), F32)
        for c in range(seq // rows):
            logf = _log_sigmoid(f_ref[c * rows:(c + 1) * rows, :] + fb_ref[...])
            run = _scan_rows(logf, _down, lambda row, k: row >= k) + carry
            o_ref[c * rows:(c + 1) * rows, :] = run
            carry = _pick_row(run, rows - 1)

    return pl.pallas_call(
        body, name=name, grid=(nb,),
        in_specs=[pl.BlockSpec((None, seq, LANE), lambda b: (b, 0, col_block)),
                  pl.BlockSpec((1, LANE), lambda b: (0, 0))],
        out_specs=pl.BlockSpec((None, seq, LANE), lambda b: (b, 0, 0)),
        out_shape=jax.ShapeDtypeStruct((nb, seq, LANE), F32), compiler_params=_params(),
    )(kvf3, fb)


def _gate_bwd(kvf3, fb, dF3, col_block, heads, *, name):
    nb, seq, _ = kvf3.shape
    rows = _chunk_rows(seq)

    def body(f_ref, fb_ref, d_ref, o_ref, dfb_ref):
        @pl.when(pl.program_id(0) == 0)
        def _():
            dfb_ref[...] = jnp.zeros(dfb_ref.shape, F32)

        lane = lax.broadcasted_iota(jnp.int32, (1, LANE), 1)
        carry = jnp.zeros((1, LANE), F32)
        for c in reversed(range(seq // rows)):
            run = _scan_rows(d_ref[c * rows:(c + 1) * rows, :], _up, lambda row, k: row < rows - k) + carry
            carry = _pick_row(run, 0)
            z = f_ref[c * rows:(c + 1) * rows, :] + fb_ref[...]
            df = jnp.where(lane < heads, run * jax.nn.sigmoid(-z), 0.0)
            o_ref[c * rows:(c + 1) * rows, :] = df
            dfb_ref[...] += jnp.sum(df, axis=0, keepdims=True)

    return pl.pallas_call(
        body, name=name, grid=(nb,),
        in_specs=[pl.BlockSpec((None, seq, LANE), lambda b: (b, 0, col_block)),
                  pl.BlockSpec((1, LANE), lambda b: (0, 0)),
                  pl.BlockSpec((None, seq, LANE), lambda b: (b, 0, 0))],
        out_specs=[pl.BlockSpec((None, seq, LANE), lambda b: (b, 0, 0)), pl.BlockSpec((1, LANE), lambda b: (0, 0))],
        out_shape=[jax.ShapeDtypeStruct((nb, seq, LANE), F32), jax.ShapeDtypeStruct((1, LANE), F32)],
        compiler_params=_params(),
    )(kvf3, fb, dF3)


def _head_masks():
    lane = lax.broadcasted_iota(jnp.int32, (1, LANE), 1)
    return (lane < HEAD_DIM, lane >= HEAD_DIM)


BIAS_TERMS = 3


def _bias_lanes(gsum):
    nb, seq, heads = gsum.shape
    terms, rest = [], gsum
    for _ in range(BIAS_TERMS):
        t = lax.reduce_precision(rest, exponent_bits=8, mantissa_bits=7)
        terms.append(t)
        rest = rest - t
    ones = [jnp.ones_like(gsum)] * BIAS_TERMS

    def lanes(parts):
        z = jnp.stack(parts, axis=-1)
        z = jnp.pad(z, ((0, 0), (0, 0), (0, 0), (0, HEAD_DIM - 2 * BIAS_TERMS)))
        z = z.reshape(nb, seq, heads // 2, 2, HEAD_DIM)[:, :, :, ::-1]
        return z.reshape(nb, seq, heads * HEAD_DIM).astype(BF16)

    return lanes(terms + ones), lanes(ones + [-t for t in terms])


def _fox_scores(q, k, aq, ak, masked):
    qs = (q * ATT_SCALE).astype(BF16)
    qts = [jnp.where(msk, qs, aq) for msk in _head_masks()]
    ss = [_bdot(qt, jnp.where(msk, k, ak), "nt") for qt, msk in zip(qts, _head_masks())]
    if masked:
        t = q.shape[0]
        keep = lax.broadcasted_iota(jnp.int32, (t, t), 0) >= lax.broadcasted_iota(jnp.int32, (t, t), 1)
        ss = [jnp.where(keep, s, NEG_BIG) for s in ss]
    return ss, qts


def _on_blocks(qi, ki, step):
    @pl.when(ki < qi)
    def _():
        step(False)

    @pl.when(ki == qi)
    def _():
        step(True)


def _fox_grid(nblk, tokw, t, q_major):
    if q_major:
        pairs = [(qi, ki) for qi in range(nblk) for ki in range(qi + 1)]
    else:
        pairs = [(qi, ki) for ki in range(nblk) for qi in range(ki, nblk)]
    tables = [jnp.array([p[i] for p in pairs], jnp.int32) for i in (0, 1)]

    def q_spec(off=0, wide=False):
        width = 2 * LANE if wide else LANE
        return pl.BlockSpec((None, t, width), lambda b, p, i, qt, kt: (b, qt[i], p + off))

    def k_spec(off=0):
        return pl.BlockSpec((None, t, LANE), lambda b, p, i, qt, kt: (b, kt[i], p + off))

    return tables, len(pairs), q_spec, k_spec, tokw // LANE


def _lanes(col):
    return jnp.broadcast_to(col, (col.shape[0], LANE))


def _across(stat, width):
    return jnp.tile(stat, (1, width // LANE))


def _fox_fwd(proj3, kvf3, aq3, ak3, tokw, *, name):
    nb, seq, _ = proj3.shape
    t = _pick(seq, 512, LANE)
    tables, n_pairs, q_spec, k_spec, hp0 = _fox_grid(seq // t, tokw, t, True)

    def body(qt_ref, kt_ref, q_ref, k_ref, v_ref, aq_ref, ak_ref, o_ref, lse_ref, m_s, l_s, acc_s):
        i = pl.program_id(2)
        qi, ki = qt_ref[i], kt_ref[i]

        @pl.when(ki == 0)
        def _():
            m_s[...] = jnp.full(m_s.shape, NEG_BIG, F32)
            l_s[...] = jnp.zeros(l_s.shape, F32)
            acc_s[...] = jnp.zeros(acc_s.shape, F32)

        def step(masked):
            v = v_ref[...].astype(BF16)
            ss, _ = _fox_scores(q_ref[...], k_ref[...].astype(BF16), aq_ref[...], ak_ref[...], masked)
            for h, s in enumerate(ss):
                m_old = m_s[h]
                m_new = jnp.maximum(m_old, _lanes(jnp.max(s, axis=-1, keepdims=True)))
                alpha = jnp.exp(m_old - m_new)
                p = jnp.exp(s - _across(m_new, t))
                l_s[h] = alpha * l_s[h] + _lanes(jnp.sum(p, axis=-1, keepdims=True))
                acc_s[h] = alpha * acc_s[h] + _bdot(p, v, "nn")
                m_s[h] = m_new

        _on_blocks(qi, ki, step)

        @pl.when(ki == qi)
        def _():
            o_ref[...] = jnp.where(_head_masks()[0], acc_s[0] / l_s[0], acc_s[1] / l_s[1])
            lse_ref[...] = jnp.concatenate([m_s[0] + jnp.log(l_s[0]), m_s[1] + jnp.log(l_s[1])], axis=1)

    stat = pltpu.VMEM((2, t, LANE), F32)
    return pl.pallas_call(
        body, name=name,
        grid_spec=pltpu.PrefetchScalarGridSpec(
            num_scalar_prefetch=2, grid=(nb, hp0, n_pairs),
            in_specs=[q_spec(), k_spec(), k_spec(hp0), q_spec(), k_spec()],
            out_specs=[q_spec(), q_spec(wide=True)], scratch_shapes=[stat, stat, stat]),
        out_shape=[jax.ShapeDtypeStruct((nb, seq, tokw), F32), jax.ShapeDtypeStruct((nb, seq, 2 * tokw), F32)],
        compiler_params=_params(),
    )(*tables, proj3, kvf3, kvf3, aq3, ak3)


def _fox_bwd_common(q_ref, k_ref, v_ref, aq_ref, ak_ref, do_ref, lse_ref, delta_ref, masked):
    k, v = k_ref[...].astype(BF16), v_ref[...].astype(BF16)
    ss, qts = _fox_scores(q_ref[...], k, aq_ref[...], ak_ref[...], masked)
    do = do_ref[...]
    t = do.shape[0]
    out = []
    for h, (s, qt, msk) in enumerate(zip(ss, qts, _head_masks())):
        doh = jnp.where(msk, do, 0.0).astype(BF16)
        p = jnp.exp(s - _across(lse_ref[:, h * LANE:(h + 1) * LANE], t))
        ds = p * (_bdot(doh, v, "nt") - _across(delta_ref[:, h * LANE:(h + 1) * LANE], t))
        out.append((qt, doh, p, ds))
    return out, k


def _fox_bwd(proj3, kvf3, aq3, ak3, o3, dmixin3, lse3, tokw, *, name):
    nb, seq, _ = proj3.shape
    t = _pick(seq, 512, LANE)
    nblk = seq // t
    tables, n_pairs, q_spec, k_spec, hp0 = _fox_grid(nblk, tokw, t, True)
    whole = pl.BlockSpec((None, seq, LANE), lambda b, p, i, qt, kt: (b, 0, p))
    dfk_spec = pl.BlockSpec((None, None, nblk, SUBLANE, t), lambda b, p, i, qt, kt: (b, p, 0, 0, 0))

    def body(qt_ref, kt_ref, q_ref, k_ref, v_ref, aq_ref, ak_ref, o_ref, do_ref, lse_ref, dq_ref, dk_ref, dv_ref,
             dfq_ref, dfk_ref, acc_s, row_s, delta_s):
        i = pl.program_id(2)
        qi, ki = qt_ref[i], kt_ref[i]

        @pl.when(i == 0)
        def _():
            dk_ref[...] = jnp.zeros(dk_ref.shape, F32)
            dv_ref[...] = jnp.zeros(dv_ref.shape, F32)
            dfk_ref[...] = jnp.zeros(dfk_ref.shape, F32)

        @pl.when(ki == 0)
        def _():
            acc_s[...] = jnp.zeros(acc_s.shape, F32)
            row_s[...] = jnp.zeros(row_s.shape, F32)
            prod = do_ref[...] * o_ref[...]
            delta_s[...] = jnp.concatenate(
                [_lanes(jnp.sum(jnp.where(msk, prod, 0.0), axis=-1, keepdims=True)) for msk in _head_masks()],
                axis=1)

        def step(masked):
            heads, k = _fox_bwd_common(q_ref, k_ref, v_ref, aq_ref, ak_ref, do_ref, lse_ref, delta_s, masked)
            rows = pl.ds(pl.multiple_of(ki * t, t), t)
            for h, ((qt, doh, p, ds), msk) in enumerate(zip(heads, _head_masks())):
                acc_s[h] += _bdot(ds, k, "nn")
                row_s[h] += _lanes(jnp.sum(ds, axis=-1, keepdims=True))
                dv_ref[rows, :] += _bdot(p, doh, "tn")
                dk_ref[rows, :] += jnp.where(msk, _bdot(ds, qt, "tn"), 0.0)
                dfk_ref[ki, h:h + 1, :] -= jnp.sum(ds, axis=0, keepdims=True)

        _on_blocks(qi, ki, step)

        @pl.when(ki == qi)
        def _():
            dq_ref[...] = jnp.where(_head_masks()[0], acc_s[0], acc_s[1]) * ATT_SCALE
            dfq_ref[...] = jnp.concatenate([row_s[0], row_s[1]], axis=1)

    out = jax.ShapeDtypeStruct((nb, seq, tokw), F32)
    stat = pltpu.VMEM((2, t, LANE), F32)
    return pl.pallas_call(
        body, name=name,
        grid_spec=pltpu.PrefetchScalarGridSpec(
            num_scalar_prefetch=2, grid=(nb, hp0, n_pairs),
            in_specs=[q_spec(), k_spec(), k_spec(hp0), q_spec(), k_spec(), q_spec(), q_spec(), q_spec(wide=True)],
            out_specs=[q_spec(), whole, whole, q_spec(wide=True), dfk_spec],
            scratch_shapes=[stat, stat, pltpu.VMEM((t, 2 * LANE), F32)]),
        out_shape=[out, out, out, jax.ShapeDtypeStruct((nb, seq, 2 * tokw), F32),
                   jax.ShapeDtypeStruct((nb, hp0, nblk, SUBLANE, t), F32)],
        compiler_params=_params(),
    )(*tables, proj3, kvf3, kvf3, aq3, ak3, o3, dmixin3, lse3)


def _peer(k):
    x, y, c = lax.axis_index("x"), lax.axis_index("y"), lax.axis_index("c")
    return (1 - x if k & 4 else x, 1 - y if k & 2 else y, 1 - c if k & 1 else c)


def _dev_index(p):
    return 4 * p[0] + 2 * p[1] + p[2]


def _exchange(send, *, gather, name):
    block = send.shape[-2:]

    def body(s_ref, o_ref, send_sems, recv_sems, local_sem):
        me = _peer(0)
        mine = pltpu.make_async_copy(s_ref if gather else s_ref.at[_dev_index(me)], o_ref.at[_dev_index(me)],
                                     local_sem)
        mine.start()
        sends, recvs = [], []
        for k in range(1, N_DEV):
            peer = _peer(k)
            src = s_ref if gather else s_ref.at[_dev_index(peer)]
            sends.append(pltpu.make_async_remote_copy(
                src_ref=src, dst_ref=o_ref.at[_dev_index(me)], send_sem=send_sems.at[k - 1],
                recv_sem=recv_sems.at[k - 1], device_id=peer, device_id_type=MESH_T))
            recvs.append(pltpu.make_async_remote_copy(
                src_ref=src, dst_ref=o_ref.at[_dev_index(peer)], send_sem=send_sems.at[k - 1],
                recv_sem=recv_sems.at[k - 1], device_id=peer, device_id_type=MESH_T))
        for cp in sends:
            cp.start()
        for cp in recvs:
            cp.wait_recv()
        for cp in sends:
            cp.wait_send()
        mine.wait()

    return pl.pallas_call(
        body, name=name,
        in_specs=[pl.BlockSpec(memory_space=pltpu.HBM)], out_specs=pl.BlockSpec(memory_space=pltpu.HBM),
        out_shape=jax.ShapeDtypeStruct((N_DEV,) + block, send.dtype),
        scratch_shapes=[pltpu.SemaphoreType.DMA((N_DEV - 1,)), pltpu.SemaphoreType.DMA((N_DEV - 1,)),
                        pltpu.SemaphoreType.DMA],
    )(send)


_HBM = pl.BlockSpec(memory_space=pltpu.HBM)
CHIP_RELATIONS = (2, 4, 6)


def _chip_index(p):
    return 2 * p[0] + p[1]


def _run_copies(sends, recvs):
    for cp in sends:
        cp.start()
    for cp in recvs:
        cp.wait_recv()
    for cp in sends:
        cp.wait_send()


def _gather_shards(whole, halved, *, name):
    nw, nh = len(whole), len(halved)
    n = nw + nh
    n_sem = 3 * nw + 6 * nh

    def body(*refs):
        ins, outs, send_sems, recv_sems = refs[:n], refs[n:2 * n], refs[2 * n], refs[2 * n + 1]
        me, sib = _peer(0), _peer(1)
        q, c = _chip_index(me), me[2]

        def copy(src, dst, s, to):
            return pltpu.make_async_remote_copy(src_ref=src, dst_ref=dst, send_sem=send_sems.at[s],
                                                recv_sem=recv_sems.at[s], device_id=to, device_id_type=MESH_T)

        sends, recvs, passes = [], [], []
        for j, k in enumerate(CHIP_RELATIONS):
            peer = _peer(k)
            pq = _chip_index(peer)
            for i in range(nw):
                sends.append(copy(ins[i], outs[i].at[q], 3 * i + j, peer))
                recvs.append(copy(ins[i], outs[i].at[pq], 3 * i + j, peer))
            for i in range(nh):
                src, out, s = ins[nw + i], outs[nw + i], 3 * nw + 6 * i + j
                sends.append(copy(src.at[c], out.at[q, c], s, peer))
                passes.append((copy(src.at[c], out.at[pq, c], s, peer), copy(out.at[pq, c], out.at[pq, c], s + 3, sib),
                               copy(out.at[pq, c], out.at[pq, 1 - c], s + 3, sib)))
        for cp in sends:
            cp.start()
        for arrival, hand_over, _ in passes:
            arrival.wait_recv()
            hand_over.start()
        for cp in recvs:
            cp.wait_recv()
        for _, _, from_sibling in passes:
            from_sibling.wait_recv()
        for cp in sends + [hand_over for _, hand_over, _ in passes]:
            cp.wait_send()

    arrays = list(whole) + list(halved)
    return pl.pallas_call(
        body, name=name, in_specs=[_HBM] * n, out_specs=[_HBM] * n,
        out_shape=[jax.ShapeDtypeStruct((N_CHIP,) + a.shape, a.dtype) for a in arrays],
        scratch_shapes=[pltpu.SemaphoreType.DMA((n_sem,)), pltpu.SemaphoreType.DMA((n_sem,))],
    )(*arrays)


def _to_sibling(grads, *, name):
    n = len(grads)

    def body(*refs):
        ins, outs, send_sems, recv_sems = refs[:n], refs[n:2 * n], refs[2 * n], refs[2 * n + 1]
        c = lax.axis_index("c")
        sends = [pltpu.make_async_remote_copy(src_ref=ins[i].at[:, 1 - c], dst_ref=outs[i], send_sem=send_sems.at[i],
                                              recv_sem=recv_sems.at[i], device_id=_peer(1), device_id_type=MESH_T)
                 for i in range(n)]
        _run_copies(sends, sends)

    return pl.pallas_call(
        body, name=name, in_specs=[_HBM] * n, out_specs=[_HBM] * n,
        out_shape=[jax.ShapeDtypeStruct(g.shape[:1] + g.shape[2:], g.dtype) for g in grads],
        scratch_shapes=[pltpu.SemaphoreType.DMA((n,)), pltpu.SemaphoreType.DMA((n,))],
    )(*grads)


def _pair_add(grads, from_sibling, qc, *, name):
    _, _, rows, cols = grads.shape
    tile = _pick(rows, 1024, SUBLANE)

    def body(qc_ref, g_ref, s_ref, o_ref):
        del qc_ref
        o_ref[...] = g_ref[...] + s_ref[...]

    spec = pl.BlockSpec((None, tile, cols), lambda j, i, qc: (j, i, 0))
    return pl.pallas_call(
        body, name=name,
        grid_spec=pltpu.PrefetchScalarGridSpec(
            num_scalar_prefetch=1, grid=(N_CHIP, rows // tile),
            in_specs=[pl.BlockSpec((None, None, tile, cols), lambda j, i, qc: (j, qc[1], i, 0)), spec],
            out_specs=spec),
        out_shape=jax.ShapeDtypeStruct((N_CHIP, rows, cols), F32), compiler_params=_params(),
    )(qc, grads, from_sibling)


def _to_chips(sums, *, name):
    n = len(sums)

    def body(*refs):
        ins, outs, send_sems, recv_sems = refs[:n], refs[n:2 * n], refs[2 * n], refs[2 * n + 1]
        sends, recvs = [], []
        for j, k in enumerate(CHIP_RELATIONS):
            peer = _peer(k)
            for i in range(n):
                sem = dict(send_sem=send_sems.at[3 * i + j], recv_sem=recv_sems.at[3 * i + j], device_id=peer,
                           device_id_type=MESH_T)
                src = ins[i].at[_chip_index(peer)]
                sends.append(pltpu.make_async_remote_copy(src_ref=src, dst_ref=outs[i].at[j], **sem))
                recvs.append(pltpu.make_async_remote_copy(src_ref=src, dst_ref=outs[i].at[j], **sem))
        _run_copies(sends, recvs)

    return pl.pallas_call(
        body, name=name, in_specs=[_HBM] * n, out_specs=[_HBM] * n,
        out_shape=[jax.ShapeDtypeStruct((3,) + g.shape[1:], g.dtype) for g in sums],
        scratch_shapes=[pltpu.SemaphoreType.DMA((3 * n,)), pltpu.SemaphoreType.DMA((3 * n,))],
    )(*sums)


def _swap_halves(arrays, *, name):
    n = len(arrays)

    def body(*refs):
        outs, send_sems, recv_sems = refs[n:2 * n], refs[2 * n], refs[2 * n + 1]
        c = lax.axis_index("c")
        sib = _peer(1)
        sends, recvs = [], []
        for i in range(n):
            sem = dict(send_sem=send_sems.at[i], recv_sem=recv_sems.at[i], device_id=sib, device_id_type=MESH_T)
            sends.append(pltpu.make_async_remote_copy(src_ref=outs[i].at[c], dst_ref=outs[i].at[c], **sem))
            recvs.append(pltpu.make_async_remote_copy(src_ref=outs[i].at[c], dst_ref=outs[i].at[1 - c], **sem))
        _run_copies(sends, recvs)

    return pl.pallas_call(
        body, name=name, in_specs=[_HBM] * n, out_specs=[_HBM] * n,
        out_shape=[jax.ShapeDtypeStruct(a.shape, a.dtype) for a in arrays],
        input_output_aliases={i: i for i in range(n)},
        scratch_shapes=[pltpu.SemaphoreType.DMA((n,)), pltpu.SemaphoreType.DMA((n,))],
    )(*arrays)


def _adam_math(g, w, m, v):
    bc1 = 1.0 - ADAM_B1 ** ADAM_STEP
    bc2 = 1.0 - ADAM_B2 ** ADAM_STEP
    m_new = ADAM_B1 * m + (1.0 - ADAM_B1) * g
    v_new = ADAM_B2 * v + (1.0 - ADAM_B2) * (g * g)
    delta = -ADAM_LR * ((m_new / bc1) / (jnp.sqrt(v_new / bc2) + ADAM_EPS) + ADAM_WD * w)
    return delta, m_new, v_new


def _adamw_half(sums, parts, w, m, v, qc, *, name):
    _, rows, cols = sums.shape
    tile = _pick(rows, 384, SUBLANE)
    n_parts = parts.shape[0]

    def body(qc_ref, g_ref, p_ref, w_ref, m_ref, v_ref, go_ref, do_ref, mo_ref, vo_ref):
        del qc_ref
        g = g_ref[...]
        for k in range(n_parts):
            g = g + p_ref[k]
        delta, m_new, v_new = _adam_math(g, w_ref[...], m_ref[...], v_ref[...])
        go_ref[...] = g
        do_ref[...] = delta
        mo_ref[...] = m_new
        vo_ref[...] = v_new

    half = pl.BlockSpec((None, tile, cols), lambda i, qc: (qc[1], i, 0))
    shape = jax.ShapeDtypeStruct((2, rows, cols), F32)
    return pl.pallas_call(
        body, name=name,
        grid_spec=pltpu.PrefetchScalarGridSpec(
            num_scalar_prefetch=1, grid=(rows // tile,),
            in_specs=[pl.BlockSpec((None, tile, cols), lambda i, qc: (qc[0], i, 0)),
                      pl.BlockSpec((n_parts, tile, cols), lambda i, qc: (0, i, 0)), half, half, half],
            out_specs=[half] * 4),
        out_shape=[shape] * 4, compiler_params=_params(),
    )(qc, sums, parts, w, m, v)


def _adamw(parts, w, m, v, *, name):
    _, rows, cols = parts.shape
    tile = _pick(rows, 256, SUBLANE)

    def body(p_ref, w_ref, m_ref, v_ref, o_ref):
        g = p_ref[0]
        for i in range(1, N_DEV):
            g = g + p_ref[i]
        delta, m_new, v_new = _adam_math(g, w_ref[...], m_ref[...], v_ref[...])
        o_ref[0] = g
        o_ref[1] = delta
        o_ref[2] = m_new
        o_ref[3] = v_new

    spec = pl.BlockSpec((tile, cols), lambda i: (i, 0))
    return pl.pallas_call(
        body, name=name, grid=(rows // tile,),
        in_specs=[pl.BlockSpec((N_DEV, tile, cols), lambda i: (0, i, 0)), spec, spec, spec],
        out_specs=pl.BlockSpec((4, tile, cols), lambda i: (0, i, 0)),
        out_shape=jax.ShapeDtypeStruct((4, rows, cols), F32), compiler_params=_params(),
    )(parts, w, m, v)


def _layout(shapes, names, align):
    out, off = [], 0
    for n in names:
        size = math.prod(shapes[n])
        out.append((n, tuple(shapes[n]), off, size))
        off += _round_up(size, align)
    return out, off


def _pack(arrays, layout, total, lead=()):
    parts = []
    for i, (n, _, off, size) in enumerate(layout):
        end = layout[i + 1][2] if i + 1 < len(layout) else total
        flat = arrays[n].reshape(lead + (size,))
        if end - off > size:
            flat = jnp.pad(flat, [(0, 0)] * len(lead) + [(0, end - off - size)])
        parts.append(flat)
    return jnp.concatenate(parts, axis=len(lead))


def _unpack(flat, layout, lead=()):
    return {n: flat[..., off:off + size].reshape(lead + shape) for n, shape, off, size in layout}


def _to_shards(full, axis):
    shp = full.shape
    return jnp.moveaxis(full.reshape(shp[:axis] + (N_CHIP, shp[axis] // N_CHIP) + shp[axis + 1:]), axis, 0)


def _from_shards(shards, axis):
    x = jnp.moveaxis(shards, 0, axis)
    shp = x.shape
    return x.reshape(shp[:axis] + (shp[axis] * shp[axis + 1],) + shp[axis + 2:])


def _pad_cols(w, per, padded):
    lead = w.shape[:-1]
    x = w.reshape(lead + (-1, per))
    x = jnp.pad(x, [(0, 0)] * len(lead) + [(0, 0), (0, padded - per)])
    return x.reshape(lead + (-1,))


def _unpad_cols(w, per, padded):
    lead = w.shape[:-1]
    return w.reshape(lead + (-1, padded))[..., :per].reshape(lead + (-1,))


def _local_step(x, mem, target, W):
    nb, seq, d = x.shape
    n = nb * seq
    tokw = d - MEM_WIDTH
    heads = tokw // HEAD_DIM
    mlen = mem.shape[1]
    dff2 = W["ffn_w_up"].shape[-1]
    per = dff2 // N_CHIP
    per_p = _round_up(per, LANE)
    fp = 2 * per_p
    kvw = 2 * tokw + heads
    kvp = 2 * tokw + LANE
    gate_block = 2 * tokw // LANE

    x2d = x.reshape(n, d)
    mem2d = mem.reshape(nb * mlen, d)
    t2d = target.reshape(n, d)
    row = lambda a: a.reshape(1, -1)
    ones_tok = jnp.ones((1, tokw), F32)

    pool_bd = jax.scipy.linalg.block_diag(*[W["a_pool_w"][0, i] for i in range(len(POOL_WINDOWS))]).astype(BF16)
    kv_w = jnp.pad(W["kv_w"], ((0, 0), (0, kvp - kvw)))
    fb = jnp.pad(W["f_b"], (0, LANE - heads)).reshape(1, LANE)
    w_up = [_pad_cols(W["ffn_w_up"][l], per, per_p) for l in range(DEPTH)]
    w_down = [jnp.pad(W["ffn_w_down"][l].reshape(2, per, d), ((0, 0), (0, per_p - per), (0, 0))).reshape(fp, d)
              for l in range(DEPTH)]
    conv_w = [jnp.pad(_pad_cols(W["ffn_conv_w"][l], per, per_p), ((0, SUBLANE - CONV_WIDTH), (0, 0)))
              for l in range(DEPTH)]
    conv_b = [_pad_cols(W["ffn_conv_b"][l], per, per_p).reshape(1, 2 * fp) for l in range(DEPTH)]
    w_in = [W["a_w_in"][0], W["b_w_q"][0]]
    w_out = [W["a_w_out"][0], W["b_w_out"][0]]

    saved = []
    cur = x2d
    for l in range(DEPTH):
        s = {"x_in": cur}
        memkv = _mm(mem2d, W["mem_w_kv"][l], "nn", name=f"memkv{l}").reshape(nb, mlen, 2 * MEM_WIDTH)
        if l == 0:
            proj = _mm(cur, w_in[l], "nn", name="proj0")
            pooled = _pool_fwd(proj.reshape(nb, seq, d), tokw, name="pool_fwd").reshape(n, tokw)
            tok = _mm(pooled, pool_bd, "nn", name="pool_mix")
            scale = W["a_pool_scale"].reshape(1, tokw)
            s.update(pooled=pooled, mixed=tok, scale=scale)
        else:
            kvf = _mm(cur, kv_w, "nn", tn=kvp, name="kvf")
            kvf3 = kvf.reshape(nb, seq, kvp)
            gsum = _gate_fwd(kvf3, fb, gate_block, name="gate_fwd")[:, :, :heads]
            aq3, ak3 = _bias_lanes(gsum)
            proj = _mm(cur, w_in[l], "nn", name="proj1")
            o3, lse3 = _fox_fwd(proj.reshape(nb, seq, d), kvf3, aq3, ak3, tokw, name="fox_fwd")
            tok = o3.reshape(n, tokw)
            scale = ones_tok
            s.update(kvf3=kvf3, aq3=aq3, ak3=ak3, o3=o3, lse3=lse3)
        mixin = _memattn_fwd(tok, proj, memkv, scale, seq=seq, name=f"memattn_fwd{l}")
        mix = _mm(mixin, w_out[l], "nn", name=f"mix{l}")
        x1 = _ln_fwd(cur, mix, row(W["ln1_g"][l]), row(W["ln1_b"][l]), name=f"ln1_fwd{l}")
        up = _mm(x1, w_up[l], "nn", tn=per_p, col_major=True, name=f"ffn_up{l}")
        act = _convgate_fwd(up.reshape(nb, seq, 2 * fp), conv_w[l], conv_b[l], name=f"convgate_fwd{l}")
        act = act.reshape(n, fp)
        ffn = _mm(act, w_down[l], "nn", tk=fp, name=f"ffn_down{l}")
        s.update(proj=proj, memkv=memkv, mixin=mixin, mix=mix, x1=x1, up=up, act=act, ffn=ffn)
        saved.append(s)
        if l + 1 < DEPTH:
            cur = _ln_fwd(x1, ffn, row(W["ln2_g"][l]), row(W["ln2_b"][l]), name=f"ln2_fwd{l}")

    G = {}
    ln_g = {k: [None] * DEPTH for k in ("ln1_g", "ln1_b", "ln2_g", "ln2_b")}
    stack = {k: [None] * DEPTH for k in ("mem_w_kv", "ffn_w_up", "ffn_conv_w", "ffn_conv_b", "ffn_w_down")}
    dx_terms = None
    loss = None
    for l in reversed(range(DEPTH)):
        s = saved[l]
        g2 = row(W["ln2_g"][l])
        if l == DEPTH - 1:
            dres, dffn, loss, dg, db = _final_ln_loss(s["x1"], s["ffn"], t2d, g2, row(W["ln2_b"][l]),
                                                      name="final_ln_loss")
        else:
            dres, dffn, dg, db = _ln_bwd(s["x1"], s["ffn"], g2, dx_terms, name=f"ln2_bwd{l}")
        ln_g["ln2_g"][l], ln_g["ln2_b"][l] = dg[0], db[0]
        dact = _mm(dffn, w_down[l], "nt", tn=fp, name=f"ffn_down_dx{l}")
        stack["ffn_w_down"][l] = _mm(s["act"], dffn, "tn", tm=per_p, tk=DW_ROWS, trim=("rows", per),
                                     name=f"ffn_down_dw{l}")
        du3, dg3, dcw, dcb = _convgate_bwd(s["up"].reshape(nb, seq, 2 * fp), dact.reshape(nb, seq, fp), conv_w[l],
                                           conv_b[l], name=f"convgate_bwd{l}")
        du, dgt = du3.reshape(n, fp), dg3.reshape(n, fp)
        dx1_u = _mm(du, w_up[l], "nt", tk=fp, name=f"ffn_up_dx_u{l}")
        dx1_ffn = _mm(dgt, w_up[l], "nt", tk=fp, b_col0=fp, adds=[dx1_u], name=f"ffn_up_dx_g{l}")
        stack["ffn_w_up"][l] = [_mm(s["x1"], part, "tn", tm=d, tn=per_p, tk=DW_ROWS, trim=("cols", per),
                                    name=f"ffn_up_dw_{nm}{l}") for nm, part in (("u", du), ("g", dgt))]
        stack["ffn_conv_w"][l] = dcw[:CONV_WIDTH]
        stack["ffn_conv_b"][l] = dcb[0]
        dres1, dmix, dg, db = _ln_bwd(s["x_in"], s["mix"], row(W["ln1_g"][l]), [dres, dx1_ffn], name=f"ln1_bwd{l}")
        ln_g["ln1_g"][l], ln_g["ln1_b"][l] = dg[0], db[0]
        dmixin = _mm(dmix, w_out[l], "nt", name=f"mix_dx{l}")
        d_w_out = _mm(s["mixin"], dmix, "tn", tm=d, tk=DW_ROWS, name=f"mix_dw{l}")
        if l == 0:
            G["a_w_out"] = d_w_out[None]
            dmixed, dscale = _scale_bwd(dmixin, s["mixed"], s["scale"], name="scale_bwd")
            G["a_pool_scale"] = dscale
            dpooled = _mm(dmixed, pool_bd, "nt", name="pool_mix_dx")
            dpw = _mm(s["pooled"], dmixed, "tn", tm=tokw, tk=DW_ROWS, name="pool_mix_dw")
            grp = tokw // len(POOL_WINDOWS)
            G["a_pool_w"] = jnp.stack([dpw[i * grp:(i + 1) * grp, i * grp:(i + 1) * grp]
                                       for i in range(len(POOL_WINDOWS))])[None]
            dtok = _pool_bwd(dpooled.reshape(nb, seq, tokw), name="pool_bwd").reshape(n, tokw)
            extra = []
        else:
            G["b_w_out"] = d_w_out[None]
            p3 = s["proj"].reshape(nb, seq, d)
            dm3 = dmixin.reshape(nb, seq, d)
            dq3, dk3, dv3, dfq3, dfk = _fox_bwd(p3, s["kvf3"], s["aq3"], s["ak3"], s["o3"], dm3, s["lse3"], tokw,
                                                name="fox_bwd")
            dtok = dq3.reshape(n, tokw)
            dfk = jnp.swapaxes(dfk[:, :, :, :2, :], 2, 3).reshape(nb, heads, seq)
            dgsum = jnp.swapaxes(dfk, 1, 2) + dfq3[:, :, ::LANE]
            dgsum = jnp.pad(dgsum, ((0, 0), (0, 0), (0, LANE - heads)))
            df3, dfb = _gate_bwd(s["kvf3"], fb, dgsum, gate_block, heads, name="gate_bwd")
            G["f_b"] = dfb[0, :heads]
            dkvf = [(dk3.reshape(n, tokw), 0, "k"), (dv3.reshape(n, tokw), tokw, "v"),
                    (df3.reshape(n, LANE), 2 * tokw, "f")]
            dx_kv = []
            for part, col0, nm in dkvf:
                dx_kv = [_mm(part, kv_w, "nt", b_col0=col0, adds=dx_kv, name=f"kvf_dx_{nm}")]
            extra = dx_kv
            G["kv_w"] = jnp.concatenate([_mm(s["x_in"], part, "tn", tm=d, tk=DW_ROWS, name=f"kvf_dw_{nm}")
                                         for part, _, nm in dkvf], axis=1)[:, :kvw]
        dproj, dmemkv = _memattn_bwd(dmixin, dtok, s["proj"], s["memkv"], seq=seq, name=f"memattn_bwd{l}")
        stack["mem_w_kv"][l] = _mm(mem2d, dmemkv.reshape(nb * mlen, 2 * MEM_WIDTH), "tn", tm=d, tk=DW_ROWS,
                                   name=f"memkv_dw{l}")
        G["a_w_in" if l == 0 else "b_w_q"] = _mm(s["x_in"], dproj, "tn", tm=d, tk=DW_ROWS, name=f"proj_dw{l}")[None]
        if l == 0:
            grad_x = _mm(dproj, w_in[l], "nt", adds=[dres1], name="proj_dx0")
        else:
            dx_terms = [dres1, _mm(dproj, w_in[l], "nt", name="proj_dx1")] + extra
    for k, v in ln_g.items():
        G[k] = jnp.stack(v)
    G["mem_w_kv"] = jnp.stack(stack["mem_w_kv"])
    G["ffn_w_up"] = jnp.stack([jnp.concatenate(ug, axis=0) for ug in stack["ffn_w_up"]], axis=1)
    G["ffn_conv_w"] = jnp.stack([_unpad_cols(g, per, per_p) for g in stack["ffn_conv_w"]])
    G["ffn_conv_b"] = jnp.stack([_unpad_cols(g, per, per_p) for g in stack["ffn_conv_b"]])
    G["ffn_w_down"] = jnp.stack([g.reshape(N_CHIP, per // 2, d) for g in stack["ffn_w_down"]], axis=1)
    return loss[0, 0], grad_x.reshape(nb, seq, d), G


def kernel(x, mem, a_w_in, a_pool_w, a_pool_scale, a_w_out, b_w_q, b_w_out, kv_w, f_b, mem_w_kv, ln1_g, ln1_b, ln2_g, ln2_b, ffn_w_up, ffn_conv_w, ffn_conv_b, ffn_w_down, loss_target, m_a_w_in, m_a_pool_w, m_a_pool_scale, m_a_w_out, m_b_w_q, m_b_w_out, m_kv_w, m_f_b, m_mem_w_kv, m_ln1_g, m_ln1_b, m_ln2_g, m_ln2_b, m_ffn_w_up, m_ffn_conv_w, m_ffn_conv_b, m_ffn_w_down, v_a_w_in, v_a_pool_w, v_a_pool_scale, v_a_w_out, v_b_w_q, v_b_w_out, v_kv_w, v_f_b, v_mem_w_kv, v_ln1_g, v_ln1_b, v_ln2_g, v_ln2_b, v_ffn_w_up, v_ffn_conv_w, v_ffn_conv_b, v_ffn_w_down):
    w_loc = dict(a_w_in=a_w_in, a_pool_w=a_pool_w, a_pool_scale=a_pool_scale, a_w_out=a_w_out, b_w_q=b_w_q,
                 b_w_out=b_w_out, kv_w=kv_w, f_b=f_b, mem_w_kv=mem_w_kv, ln1_g=ln1_g, ln1_b=ln1_b, ln2_g=ln2_g,
                 ln2_b=ln2_b, ffn_w_up=ffn_w_up, ffn_conv_w=ffn_conv_w, ffn_conv_b=ffn_conv_b, ffn_w_down=ffn_w_down)
    m_loc = dict(a_w_in=m_a_w_in, a_pool_w=m_a_pool_w, a_pool_scale=m_a_pool_scale, a_w_out=m_a_w_out,
                 b_w_q=m_b_w_q, b_w_out=m_b_w_out, kv_w=m_kv_w, f_b=m_f_b, mem_w_kv=m_mem_w_kv, ln1_g=m_ln1_g,
                 ln1_b=m_ln1_b, ln2_g=m_ln2_g, ln2_b=m_ln2_b, ffn_w_up=m_ffn_w_up, ffn_conv_w=m_ffn_conv_w,
                 ffn_conv_b=m_ffn_conv_b, ffn_w_down=m_ffn_w_down)
    v_loc = dict(a_w_in=v_a_w_in, a_pool_w=v_a_pool_w, a_pool_scale=v_a_pool_scale, a_w_out=v_a_w_out,
                 b_w_q=v_b_w_q, b_w_out=v_b_w_out, kv_w=v_kv_w, f_b=v_f_b, mem_w_kv=v_mem_w_kv, ln1_g=v_ln1_g,
                 ln1_b=v_ln1_b, ln2_g=v_ln2_g, ln2_b=v_ln2_b, ffn_w_up=v_ffn_w_up, ffn_conv_w=v_ffn_conv_w,
                 ffn_conv_b=v_ffn_conv_b, ffn_w_down=v_ffn_w_down)
    x_i, y_i, c = lax.axis_index("x"), lax.axis_index("y"), lax.axis_index("c")
    q = 2 * x_i + y_i
    qc = jnp.stack([q, c]).astype(jnp.int32)
    shapes = {k: v.shape for k, v in w_loc.items()}

    def halves(a):
        if a.ndim == 3 and a.shape[0] == 2:
            return a
        rows = math.prod(a.shape[:-1])
        return a.reshape(2, rows // 2, a.shape[-1])

    own = {k: (w_loc[k] if k in GATHER_F32 else w_loc[k].astype(BF16)) for k in SHARDED}
    gathered = _gather_shards([own["a_pool_scale"]], [halves(own[k]) for k in BIG], name="gather_weights")
    W = {k: _from_shards(lax.dynamic_update_slice_in_dim(g.reshape((N_CHIP,) + shapes[k]), own[k][None], q, axis=0),
                         SHARD_AXIS[k])
         for k, g in zip(("a_pool_scale",) + BIG, gathered)}
    for k in REPLICATED:
        W[k] = w_loc[k]

    loss_part, grad_x, G = _local_step(x, mem, loss_target, W)
    loss = lax.psum(loss_part, ("x", "y", "c"))

    g_chip = [G[k] if k in GRADS_BY_CHIP else _to_shards(G[k], SHARD_AXIS[k]) for k in BIG]
    g_chip = [g.reshape((N_CHIP,) + halves(w_loc[k]).shape) for k, g in zip(BIG, g_chip)]
    from_sib = _to_sibling(g_chip, name="grads_to_sibling")
    sums = [_pair_add(g, s, qc, name=f"pair_add_{k}") for k, g, s in zip(BIG, g_chip, from_sib)]
    parts = _to_chips(sums, name="scatter_grads")
    res = []
    for k, g, p in zip(BIG, sums, parts):
        res.extend(_adamw_half(g, p, halves(w_loc[k]), halves(m_loc[k]), halves(v_loc[k]), qc, name=f"adamw_{k}"))
    res = _swap_halves(res, name="swap_halves")
    out = {k: [r.reshape(shapes[k]) for r in res[4 * i:4 * i + 4]] for i, k in enumerate(BIG)}

    lay_r, tot_r = _layout(shapes, REPLICATED, PACK_COLS)
    rep_rows = tot_r // PACK_COLS
    rows = _round_up(rep_rows + 1, SUBLANE)
    scale_w = shapes["a_pool_scale"][-1]

    def small(rep, scale_row):
        lead = scale_row.shape[:-2]
        rep = jnp.broadcast_to(_pack(rep, lay_r, tot_r).reshape(rep_rows, PACK_COLS), lead + (rep_rows, PACK_COLS))
        pad = [(0, 0)] * len(lead)
        return jnp.concatenate([rep, jnp.pad(scale_row, pad + [(0, rows - rep_rows - 1), (0, PACK_COLS - scale_w)])],
                               axis=-2)

    g_scale = jnp.repeat(_to_shards(G["a_pool_scale"], 1), 2, axis=0)
    sm_parts = _exchange(small(G, g_scale), gather=False, name="scatter_small")
    sm = _adamw(sm_parts, *[small(d, d["a_pool_scale"]) for d in (w_loc, m_loc, v_loc)], name="adamw_small")
    out_r = _unpack(sm[:, :rep_rows].reshape(4, tot_r), lay_r, lead=(4,))
    for k in REPLICATED:
        out[k] = [out_r[k][a] for a in range(4)]
    out["a_pool_scale"] = [sm[a, rep_rows:rep_rows + 1, :scale_w] for a in range(4)]

    outs = [loss, grad_x]
    for a in range(4):
        for k in WEIGHTS:
            outs.append(out[k][a])
    return tuple(outs)
```

```python
import functools
import math

import jax
import jax.numpy as jnp
from jax import lax
from jax.experimental import pallas as pl
from jax.experimental.pallas import tpu as pltpu

F32 = jnp.float32
BF16 = jnp.bfloat16

HEAD_DIM = 64
MEM_HEADS = 4
MEM_WIDTH = MEM_HEADS * HEAD_DIM
POOL_WINDOWS = (2, 4, 8, 16)
MAX_WINDOW = 16
CONV_WIDTH = 3
DEPTH = 2
DN_ALPHA = (2.0 * DEPTH) ** 0.25
LN_EPS = 1e-5
ATT_SCALE = HEAD_DIM ** -0.5
NEG_BIG = -1e30

ADAM_LR = 0.001
ADAM_B1 = 0.9
ADAM_B2 = 0.999
ADAM_EPS = 1e-08
ADAM_WD = 0.01
ADAM_STEP = 10

LANE = 128
SUBLANE = 8
PACK_COLS = 1024
DW_ROWS = 1024
VMEM_LIMIT = 56 * 1024 * 1024
N_DEV = 8
N_CHIP = 4
MESH_T = pl.DeviceIdType.MESH

SHARDED = ("a_w_in", "a_pool_scale", "a_w_out", "b_w_q", "b_w_out", "kv_w", "mem_w_kv", "ffn_w_up",
           "ffn_conv_w", "ffn_w_down")
SHARD_AXIS = {"a_w_in": 1, "a_pool_scale": 1, "a_w_out": 1, "b_w_q": 1, "b_w_out": 1, "kv_w": 1, "mem_w_kv": 1,
              "ffn_w_up": 2, "ffn_conv_w": 2, "ffn_w_down": 1}
GATHER_F32 = ("a_pool_scale", "ffn_conv_w")
BIG = tuple(k for k in SHARDED if k != "a_pool_scale")
GRADS_BY_CHIP = ("ffn_w_up", "ffn_w_down")
REPLICATED = ("a_pool_w", "f_b", "ln1_g", "ln1_b", "ln2_g", "ln2_b", "ffn_conv_b")
WEIGHTS = ("a_w_in", "a_pool_w", "a_pool_scale", "a_w_out", "b_w_q", "b_w_out", "kv_w", "f_b", "mem_w_kv",
           "ln1_g", "ln1_b", "ln2_g", "ln2_b", "ffn_w_up", "ffn_conv_w", "ffn_conv_b", "ffn_w_down")


def _round_up(n, m):
    return -(-n // m) * m


def _pick(dim, pref, unit=LANE):
    if dim <= pref:
        return dim
    t = (pref // unit) * unit
    while t >= unit:
        if dim % t == 0:
            return t
        t -= unit
    raise ValueError(f"no tile for {dim} <= {pref}")


def _params():
    return pltpu.CompilerParams(vmem_limit_bytes=VMEM_LIMIT)


_DIMS = {"nn": ((1,), (0,)), "nt": ((1,), (1,)), "tn": ((0,), (0,))}


def _bdot(a, b, mode):
    return lax.dot_general(a.astype(BF16), b.astype(BF16), (_DIMS[mode], ((), ())), preferred_element_type=F32)


def _mm(a, b, mode, *, name, tm=512, tn=1024, tk=2048, adds=(), b_col0=0, trim=None, col_major=False, b_lead=None):
    b_shape = b.shape if b_lead is None else b.shape[1:]
    if mode == "nn":
        (M, K), (K2, N) = a.shape, b_shape
    elif mode == "nt":
        (M, K), (N, K2) = a.shape, b_shape
        K2 = K if b_col0 + K <= K2 else -1
    else:
        (K, M), (K2, N) = a.shape, b_shape
    assert K == K2 and (mode == "nt" or b_col0 == 0), (name, a.shape, b.shape)
    tm, tn = _pick(M, tm, SUBLANE if mode != "tn" else LANE), _pick(N, tn)
    tk = _pick(K, tk, LANE if mode != "tn" else SUBLANE)
    nk = K // tk
    assert b_col0 % tk == 0, (name, b_col0, tk)
    koff = b_col0 // tk
    n_add = len(adds)

    def body(*refs):
        a_ref, b_ref = refs[0], refs[1]
        add_refs = refs[2:2 + n_add]
        o_ref, acc_ref = refs[2 + n_add], refs[3 + n_add]
        part = _bdot(a_ref[...], b_ref[...], mode)

        def finish(r):
            for ar in add_refs:
                r = r + ar[...]
            if trim is not None:
                r = r[:, :trim[1]] if trim[0] == "cols" else r[:trim[1], :]
            o_ref[...] = r

        if nk == 1:
            finish(part)
        else:
            k = pl.program_id(2)

            @pl.when(k == 0)
            def _():
                acc_ref[...] = part

            @pl.when(k > 0)
            def _():
                acc_ref[...] += part

            @pl.when(k == nk - 1)
            def _():
                finish(acc_ref[...])

    def spec(block, index):
        if col_major:
            return pl.BlockSpec(block, lambda j, i, k: index(i, j, k))
        return pl.BlockSpec(block, index)

    def b_spec_of(block, index):
        if b_lead is None:
            return spec(block, index)
        return spec((None,) + block, lambda i, j, k: (b_lead,) + index(i, j, k))

    if mode == "nn":
        a_spec = spec((tm, tk), lambda i, j, k: (i, k))
        b_spec = b_spec_of((tk, tn), lambda i, j, k: (k, j))
    elif mode == "nt":
        a_spec = spec((tm, tk), lambda i, j, k: (i, k))
        b_spec = b_spec_of((tn, tk), lambda i, j, k: (j, k + koff))
    else:
        a_spec = spec((tk, tm), lambda i, j, k: (k, i))
        b_spec = b_spec_of((tk, tn), lambda i, j, k: (k, j))
    o_spec = spec((tm, tn), lambda i, j, k: (i, j))
    out_spec, out_shape = o_spec, (M, N)
    if trim is not None and trim[0] == "cols":
        out_spec, out_shape = spec((None, tm, trim[1]), lambda i, j, k: (j, i, 0)), (N // tn, M, trim[1])
    elif trim is not None:
        out_spec, out_shape = spec((None, trim[1], tn), lambda i, j, k: (i, 0, j)), (M // tm, trim[1], N)
    acc_shape = (tm, tn) if nk > 1 else (SUBLANE, LANE)
    return pl.pallas_call(
        body, name=name, grid=(N // tn, M // tm, nk) if col_major else (M // tm, N // tn, nk),
        in_specs=[a_spec, b_spec] + [o_spec] * n_add, out_specs=out_spec,
        out_shape=jax.ShapeDtypeStruct(out_shape, F32),
        scratch_shapes=[pltpu.VMEM(acc_shape, F32)],
        compiler_params=_params(),
    )(a, b, *adds)


def _rowwise(fn, tiled, full, outs_tiled, outs_acc, *, rows, tile, name, acc_period=None):
    n_tiles = rows // tile
    period = n_tiles if acc_period is None else acc_period
    arrays, in_specs = [], []
    for t in tiled:
        arr, width, cb = t if isinstance(t, tuple) else (t, t.shape[1], 0)
        arrays.append(arr)
        in_specs.append(pl.BlockSpec((tile, width), lambda i, cb=cb: (i, cb)))
    for f in full:
        arr, spec = f if isinstance(f, tuple) else (f, None)
        arrays.append(arr)
        in_specs.append(spec if spec is not None else pl.BlockSpec(arr.shape, lambda i, nd=arr.ndim: (0,) * nd))
    out_shape, out_specs = [], []
    for width, dt in outs_tiled:
        out_shape.append(jax.ShapeDtypeStruct((rows, width), dt))
        out_specs.append(pl.BlockSpec((tile, width), lambda i: (i, 0)))
    for acc in outs_acc:
        shape, dt = acc[0], acc[1]
        out_shape.append(jax.ShapeDtypeStruct(shape, dt))
        out_specs.append(acc[2] if len(acc) > 2 else pl.BlockSpec(shape, lambda i, nd=len(shape): (0,) * nd))
    n_in, n_t, n_a = len(arrays), len(outs_tiled), len(outs_acc)

    def body(*refs):
        vals = [r[...] for r in refs[:n_in]]
        o_t, o_a = fn(*vals)
        for r, v in zip(refs[n_in:n_in + n_t], o_t):
            r[...] = v.astype(r.dtype)
        first = pl.program_id(0) % period == 0
        for r, v in zip(refs[n_in + n_t:n_in + n_t + n_a], o_a):
            v = v.reshape(r.shape)

            @pl.when(first)
            def _(r=r, v=v):
                r[...] = v

            @pl.when(jnp.logical_not(first))
            def _(r=r, v=v):
                r[...] += v

    return pl.pallas_call(
        body, name=name, grid=(n_tiles,), in_specs=in_specs, out_specs=out_specs, out_shape=out_shape,
        compiler_params=_params(),
    )(*arrays)


def _ln_stats(h):
    mu = jnp.mean(h, axis=-1, keepdims=True)
    d = h - mu
    var = jnp.mean(d * d, axis=-1, keepdims=True)
    rstd = lax.rsqrt(var + LN_EPS)
    return d * rstd, rstd


def _ln_bwd_math(h, g, dy):
    xhat, rstd = _ln_stats(h)
    dxhat = dy * g
    dh = rstd * (dxhat - jnp.mean(dxhat, axis=-1, keepdims=True)
                 - xhat * jnp.mean(dxhat * xhat, axis=-1, keepdims=True))
    return dh, jnp.sum(dy * xhat, axis=0, keepdims=True), jnp.sum(dy, axis=0, keepdims=True)


def _ln_fwd(x, r, g, b, *, name):
    n, d = x.shape

    def fn(x, r, g, b):
        xhat, _ = _ln_stats(DN_ALPHA * x + r)
        return (xhat * g + b,), ()

    return _rowwise(fn, [x, r], [g, b], [(d, F32)], [], rows=n, tile=_pick(n, 512, SUBLANE), name=name)[0]


def _ln_bwd(x, r, g, dys, *, name):
    n, d = x.shape
    n_dy = len(dys)

    def fn(x, r, *rest):
        dy = rest[0]
        for e in rest[1:n_dy]:
            dy = dy + e
        dh, dg, db = _ln_bwd_math(DN_ALPHA * x + r, rest[n_dy], dy)
        return (DN_ALPHA * dh, dh), (dg, db)

    return _rowwise(fn, [x, r, *dys], [g], [(d, F32), (d, BF16)], [((1, d), F32), ((1, d), F32)],
                    rows=n, tile=_pick(n, 256, SUBLANE), name=name)


def _final_ln_loss(x, r, target, g, b, *, name):
    n, d = x.shape

    def fn(x, r, t, g, b):
        h = DN_ALPHA * x + r
        xhat, _ = _ln_stats(h)
        err = xhat * g + b - t
        loss = jnp.full((1, LANE), 0.5 * jnp.sum(err * err) / d, F32)
        dh, dg, db = _ln_bwd_math(h, g, err / d)
        return (DN_ALPHA * dh, dh), (loss, dg, db)

    return _rowwise(fn, [x, r, target], [g, b], [(d, F32), (d, BF16)],
                    [((1, LANE), F32), ((1, d), F32), ((1, d), F32)],
                    rows=n, tile=_pick(n, 256, SUBLANE), name=name)


def _mem_heads(qm):
    lane = lax.broadcasted_iota(jnp.int32, (1, MEM_WIDTH), 1)
    for h in range(MEM_HEADS):
        msk = (lane >= h * HEAD_DIM) & (lane < (h + 1) * HEAD_DIM)
        yield msk, jnp.where(msk, qm, 0.0).astype(BF16)


def _mem_softmax(qh, k):
    s = _bdot(qh, k, "nt") * ATT_SCALE
    p = jnp.exp(s - jnp.max(s, axis=-1, keepdims=True))
    return p / jnp.sum(p, axis=-1, keepdims=True)


def _memattn_fwd(tok, proj, memkv, scale, *, seq, name):
    n, tokw = tok.shape
    d = tokw + MEM_WIDTH
    tile = _pick(seq, 512, SUBLANE)

    def fn(tok, qm, kv, scale):
        k, v = kv[:, :MEM_WIDTH].astype(BF16), kv[:, MEM_WIDTH:].astype(BF16)
        out = jnp.zeros(qm.shape, F32)
        for msk, qh in _mem_heads(qm):
            out = jnp.where(msk, _bdot(_mem_softmax(qh, k), v, "nn"), out)
        return (jnp.concatenate([tok * scale, out], axis=1),), ()

    kv_spec = pl.BlockSpec((None,) + memkv.shape[1:], lambda i: (i // (seq // tile), 0, 0))
    return _rowwise(fn, [tok, (proj, MEM_WIDTH, tokw // MEM_WIDTH)], [(memkv, kv_spec), scale], [(d, BF16)], [],
                    rows=n, tile=tile, name=name)[0]


def _memattn_bwd(dmixin, dtok, proj, memkv, *, seq, name):
    n, tokw = dtok.shape
    d = tokw + MEM_WIDTH
    tile = _pick(seq, 512, SUBLANE)

    def fn(dmo, dtok, qm, kv):
        k, v = kv[:, :MEM_WIDTH].astype(BF16), kv[:, MEM_WIDTH:].astype(BF16)
        dq = jnp.zeros(qm.shape, F32)
        dk = jnp.zeros(k.shape, F32)
        dv = jnp.zeros(v.shape, F32)
        for msk, qh in _mem_heads(qm):
            p = _mem_softmax(qh, k)
            doh = jnp.where(msk, dmo, 0.0).astype(BF16)
            dv = dv + _bdot(p, doh, "tn")
            dp = _bdot(doh, v, "nt")
            ds = (p * (dp - jnp.sum(dp * p, axis=-1, keepdims=True))).astype(BF16)
            dq = jnp.where(msk, _bdot(ds, k, "nn") * ATT_SCALE, dq)
            dk = dk + _bdot(ds, qh, "tn") * ATT_SCALE
        return (jnp.concatenate([dtok, dq], axis=1),), (jnp.concatenate([dk, dv], axis=1),)

    tpe = seq // tile
    kv_spec = pl.BlockSpec((None,) + memkv.shape[1:], lambda i: (i // tpe, 0, 0))
    return _rowwise(fn, [(dmixin, MEM_WIDTH, tokw // MEM_WIDTH), dtok, (proj, MEM_WIDTH, tokw // MEM_WIDTH)],
                    [(memkv, kv_spec)], [(d, BF16)], [(memkv.shape, F32, kv_spec)],
                    rows=n, tile=tile, name=name, acc_period=tpe)


def _scale_bwd(dmixin, mixed, scale, *, name):
    n, tokw = mixed.shape

    def fn(dt, mixed, scale):
        return (dt * scale,), (jnp.sum(dt * mixed, axis=0, keepdims=True),)

    return _rowwise(fn, [(dmixin, tokw, 0), mixed], [scale], [(tokw, BF16)], [((1, tokw), F32)],
                    rows=n, tile=_pick(n, 512, SUBLANE), name=name)


def _chunk_rows(seq):
    return _pick(seq, 512, SUBLANE)


def _load_ext(ref, c, rows, before, after, seq):
    lo, hi = c * rows - before, (c + 1) * rows + after
    parts = []
    if lo < 0:
        parts.append(jnp.zeros((-lo, ref.shape[1]), F32))
    parts.append(ref[max(lo, 0):min(hi, seq), :])
    if hi > seq:
        parts.append(jnp.zeros((hi - seq, ref.shape[1]), F32))
    return parts[0] if len(parts) == 1 else jnp.concatenate(parts, axis=0)


def _down(x, k):
    return pltpu.roll(x, k, 0)


def _up(x, k):
    return pltpu.roll(x, x.shape[0] - k, 0)


def _window_sums(ext, shift, col0, group):
    lane = col0 + lax.broadcasted_iota(jnp.int32, (1, ext.shape[1]), 1)
    gidx = lane // group
    s = ext
    out = None
    k = 1
    for gi, w in enumerate(POOL_WINDOWS):
        while k < w:
            s = s + shift(s, k)
            k *= 2
        out = s if out is None else jnp.where(gidx >= gi, s, out)
    return out, jnp.left_shift(2, jnp.minimum(gidx, len(POOL_WINDOWS) - 1))


def _pool_fwd(proj3, tokw, *, name):
    nb, seq, _ = proj3.shape
    rows = _chunk_rows(seq)
    group = tokw // len(POOL_WINDOWS)

    def body(u_ref, o_ref):
        col0 = pl.program_id(1) * LANE
        for c in range(seq // rows):
            ext = _load_ext(u_ref, c, rows, MAX_WINDOW, 0, seq)
            sums, win = _window_sums(ext, _down, col0, group)
            t = c * rows + lax.broadcasted_iota(jnp.int32, (rows, 1), 0)
            count = jnp.minimum(t + 1, win).astype(F32)
            o_ref[c * rows:(c + 1) * rows, :] = (sums[MAX_WINDOW:, :] / count - ext[MAX_WINDOW:, :]).astype(BF16)

    spec = pl.BlockSpec((None, seq, LANE), lambda b, j: (b, 0, j))
    return pl.pallas_call(
        body, name=name, grid=(nb, tokw // LANE), in_specs=[spec], out_specs=spec,
        out_shape=jax.ShapeDtypeStruct((nb, seq, tokw), BF16), compiler_params=_params(),
    )(proj3)


def _pool_bwd(dp3, *, name):
    nb, seq, tokw = dp3.shape
    rows = _chunk_rows(seq)
    group = tokw // len(POOL_WINDOWS)

    def body(d_ref, o_ref):
        col0 = pl.program_id(1) * LANE
        for c in range(seq // rows):
            ext = _load_ext(d_ref, c, rows, 0, MAX_WINDOW, seq)
            lane = col0 + lax.broadcasted_iota(jnp.int32, (1, LANE), 1)
            win = jnp.left_shift(2, jnp.minimum(lane // group, len(POOL_WINDOWS) - 1))
            t = c * rows + lax.broadcasted_iota(jnp.int32, (rows + MAX_WINDOW, 1), 0)
            scaled = ext / jnp.minimum(t + 1, win).astype(F32)
            sums, _ = _window_sums(scaled, _up, col0, group)
            o_ref[c * rows:(c + 1) * rows, :] = sums[:rows, :] - ext[:rows, :]

    spec = pl.BlockSpec((None, seq, LANE), lambda b, j: (b, 0, j))
    return pl.pallas_call(
        body, name=name, grid=(nb, tokw // LANE), in_specs=[spec], out_specs=spec,
        out_shape=jax.ShapeDtypeStruct((nb, seq, tokw), F32), compiler_params=_params(),
    )(dp3)


def _conv3(ext, w_ref, b_ref):
    x1, x2 = _down(ext, 1), _down(ext, 2)
    return w_ref[0:1, :] * x2 + w_ref[1:2, :] * x1 + w_ref[2:3, :] * ext + b_ref[...], x1, x2


def _convgate_fwd(up3, cw, cb, *, name):
    nb, seq, c2 = up3.shape
    fp = c2 // 2
    nblk = fp // LANE
    rows = _chunk_rows(seq)

    def body(u_ref, g_ref, wu_ref, wg_ref, bu_ref, bg_ref, o_ref):
        for c in range(seq // rows):
            hu, _, _ = _conv3(_load_ext(u_ref, c, rows, SUBLANE, 0, seq), wu_ref, bu_ref)
            hg, _, _ = _conv3(_load_ext(g_ref, c, rows, SUBLANE, 0, seq), wg_ref, bg_ref)
            o_ref[c * rows:(c + 1) * rows, :] = (hg * jax.nn.sigmoid(hg) * hu)[SUBLANE:, :].astype(BF16)

    def col(off, r):
        return pl.BlockSpec((r, LANE), lambda b, j: (0, j + off))

    def act(off):
        return pl.BlockSpec((None, seq, LANE), lambda b, j: (b, 0, j + off))

    return pl.pallas_call(
        body, name=name, grid=(nb, nblk),
        in_specs=[act(0), act(nblk), col(0, SUBLANE), col(nblk, SUBLANE), col(0, 1), col(nblk, 1)],
        out_specs=act(0), out_shape=jax.ShapeDtypeStruct((nb, seq, fp), BF16), compiler_params=_params(),
    )(up3, up3, cw, cw, cb, cb)


def _convgate_bwd(up3, dact3, cw, cb, *, name):
    nb, seq, c2 = up3.shape
    fp = c2 // 2
    nblk = fp // LANE
    rows = _chunk_rows(seq)
    h = SUBLANE

    def body(u_ref, g_ref, da_ref, wu_ref, wg_ref, bu_ref, bg_ref, du_ref, dg_ref, dwu_ref, dwg_ref, dbu_ref,
             dbg_ref):
        @pl.when(pl.program_id(1) == 0)
        def _():
            for r in (dwu_ref, dwg_ref, dbu_ref, dbg_ref):
                r[...] = jnp.zeros(r.shape, F32)

        for c in range(seq // rows):
            eu = _load_ext(u_ref, c, rows, h, h, seq)
            eg = _load_ext(g_ref, c, rows, h, h, seq)
            da = _load_ext(da_ref, c, rows, h, h, seq)
            hu, u1, u2 = _conv3(eu, wu_ref, bu_ref)
            hg, g1, g2 = _conv3(eg, wg_ref, bg_ref)
            sig = jax.nn.sigmoid(hg)
            dhu = da * hg * sig
            dhg = da * hu * sig * (1.0 + hg * (1.0 - sig))
            for dh, w_ref, x0, x1, x2, dx_ref, dw_ref, db_ref in (
                    (dhu, wu_ref, eu, u1, u2, du_ref, dwu_ref, dbu_ref),
                    (dhg, wg_ref, eg, g1, g2, dg_ref, dwg_ref, dbg_ref)):
                dx = w_ref[2:3, :] * dh + w_ref[1:2, :] * _up(dh, 1) + w_ref[0:1, :] * _up(dh, 2)
                dx_ref[c * rows:(c + 1) * rows, :] = dx[h:h + rows, :].astype(BF16)
                core = dh[h:h + rows, :]
                for k, xk in ((0, x2), (1, x1), (2, x0)):
                    dw_ref[k:k + 1, :] += jnp.sum(core * xk[h:h + rows, :], axis=0, keepdims=True)
                db_ref[...] += jnp.sum(core, axis=0, keepdims=True)

    def col(off, r):
        return pl.BlockSpec((r, LANE), lambda j, b: (0, j + off))

    def act(off):
        return pl.BlockSpec((None, seq, LANE), lambda j, b: (b, 0, j + off))

    du, dg, dwu, dwg, dbu, dbg = pl.pallas_call(
        body, name=name, grid=(nblk, nb),
        in_specs=[act(0), act(nblk), act(0), col(0, SUBLANE), col(nblk, SUBLANE), col(0, 1), col(nblk, 1)],
        out_specs=[act(0), act(0), col(0, SUBLANE), col(0, SUBLANE), col(0, 1), col(0, 1)],
        out_shape=[jax.ShapeDtypeStruct((nb, seq, fp), BF16), jax.ShapeDtypeStruct((nb, seq, fp), BF16),
                   jax.ShapeDtypeStruct((SUBLANE, fp), F32), jax.ShapeDtypeStruct((SUBLANE, fp), F32),
                   jax.ShapeDtypeStruct((1, fp), F32), jax.ShapeDtypeStruct((1, fp), F32)],
        compiler_params=_params(),
    )(up3, up3, dact3, cw, cw, cb, cb)
    return du, dg, jnp.concatenate([dwu, dwg], axis=1), jnp.concatenate([dbu, dbg], axis=1)


def _scan_rows(x, shift, valid):
    row = lax.broadcasted_iota(jnp.int32, (x.shape[0], 1), 0)
    k = 1
    while k < x.shape[0]:
        x = x + jnp.where(valid(row, k), shift(x, k), 0.0)
        k *= 2
    return x


def _pick_row(x, r):
    row = lax.broadcasted_iota(jnp.int32, (x.shape[0], 1), 0)
    return jnp.sum(jnp.where(row == r, x, 0.0), axis=0, keepdims=True)


def _log_sigmoid(z):
    return jnp.minimum(z, 0.0) - jnp.log(1.0 + jnp.exp(-jnp.abs(z)))


def _gate_fwd(kvf3, fb, col_block, *, name):
    nb, seq, _ = kvf3.shape
    rows = _chunk_rows(seq)

    def body(f_ref, fb_ref, o_ref):
        carry = jnp.zeros((1, LANE), F32)
        for c in range(seq // rows):
            logf = _log_sigmoid(f_ref[c * rows:(c + 1) * rows, :] + fb_ref[...])
            run = _scan_rows(logf, _down, lambda row, k: row >= k) + carry
            o_ref[c * rows:(c + 1) * rows, :] = run
            carry = _pick_row(run, rows - 1)

    return pl.pallas_call(
        body, name=name, grid=(nb,),
        in_specs=[pl.BlockSpec((None, seq, LANE), lambda b: (b, 0, col_block)),
                  pl.BlockSpec((1, LANE), lambda b: (0, 0))],
        out_specs=pl.BlockSpec((None, seq, LANE), lambda b: (b, 0, 0)),
        out_shape=jax.ShapeDtypeStruct((nb, seq, LANE), F32), compiler_params=_params(),
    )(kvf3, fb)


def _gate_bwd(kvf3, fb, dF3, col_block, heads, *, name):
    nb, seq, _ = kvf3.shape
    rows = _chunk_rows(seq)

    def body(f_ref, fb_ref, d_ref, o_ref, dfb_ref):
        @pl.when(pl.program_id(0) == 0)
        def _():
            dfb_ref[...] = jnp.zeros(dfb_ref.shape, F32)

        lane = lax.broadcasted_iota(jnp.int32, (1, LANE), 1)
        carry = jnp.zeros((1, LANE), F32)
        for c in reversed(range(seq // rows)):
            run = _scan_rows(d_ref[c * rows:(c + 1) * rows, :], _up, lambda row, k: row < rows - k) + carry
            carry = _pick_row(run, 0)
            z = f_ref[c * rows:(c + 1) * rows, :] + fb_ref[...]
            df = jnp.where(lane < heads, run * jax.nn.sigmoid(-z), 0.0)
            o_ref[c * rows:(c + 1) * rows, :] = df
            dfb_ref[...] += jnp.sum(df, axis=0, keepdims=True)

    return pl.pallas_call(
        body, name=name, grid=(nb,),
        in_specs=[pl.BlockSpec((None, seq, LANE), lambda b: (b, 0, col_block)),
                  pl.BlockSpec((1, LANE), lambda b: (0, 0)),
                  pl.BlockSpec((None, seq, LANE), lambda b: (b, 0, 0))],
        out_specs=[pl.BlockSpec((None, seq, LANE), lambda b: (b, 0, 0)), pl.BlockSpec((1, LANE), lambda b: (0, 0))],
        out_shape=[jax.ShapeDtypeStruct((nb, seq, LANE), F32), jax.ShapeDtypeStruct((1, LANE), F32)],
        compiler_params=_params(),
    )(kvf3, fb, dF3)


def _head_masks():
    lane = lax.broadcasted_iota(jnp.int32, (1, LANE), 1)
    return (lane < HEAD_DIM, lane >= HEAD_DIM)


BIAS_TERMS = 3


def _bias_lanes(gsum):
    nb, seq, heads = gsum.shape
    terms, rest = [], gsum
    for _ in range(BIAS_TERMS):
        t = lax.reduce_precision(rest, exponent_bits=8, mantissa_bits=7)
        terms.append(t)
        rest = rest - t
    ones = [jnp.ones_like(gsum)] * BIAS_TERMS

    def lanes(parts):
        z = jnp.stack(parts, axis=-1)
        z = jnp.pad(z, ((0, 0), (0, 0), (0, 0), (0, HEAD_DIM - 2 * BIAS_TERMS)))
        z = z.reshape(nb, seq, heads // 2, 2, HEAD_DIM)[:, :, :, ::-1]
        return z.reshape(nb, seq, heads * HEAD_DIM).astype(BF16)

    return lanes(terms + ones), lanes(ones + [-t for t in terms])


def _fox_scores(q, k, aq, ak, masked):
    qs = (q * ATT_SCALE).astype(BF16)
    qts = [jnp.where(msk, qs, aq) for msk in _head_masks()]
    ss = [_bdot(qt, jnp.where(msk, k, ak), "nt") for qt, msk in zip(qts, _head_masks())]
    if masked:
        t = q.shape[0]
        keep = lax.broadcasted_iota(jnp.int32, (t, t), 0) >= lax.broadcasted_iota(jnp.int32, (t, t), 1)
        ss = [jnp.where(keep, s, NEG_BIG) for s in ss]
    return ss, qts


def _on_blocks(qi, ki, step):
    @pl.when(ki < qi)
    def _():
        step(False)

    @pl.when(ki == qi)
    def _():
        step(True)


def _fox_grid(nblk, tokw, t, q_major):
    if q_major:
        pairs = [(qi, ki) for qi in range(nblk) for ki in range(qi + 1)]
    else:
        pairs = [(qi, ki) for ki in range(nblk) for qi in range(ki, nblk)]
    tables = [jnp.array([p[i] for p in pairs], jnp.int32) for i in (0, 1)]

    def q_spec(off=0, wide=False):
        width = 2 * LANE if wide else LANE
        return pl.BlockSpec((None, t, width), lambda b, p, i, qt, kt: (b, qt[i], p + off))

    def k_spec(off=0):
        return pl.BlockSpec((None, t, LANE), lambda b, p, i, qt, kt: (b, kt[i], p + off))

    return tables, len(pairs), q_spec, k_spec, tokw // LANE


def _lanes(col):
    return jnp.broadcast_to(col, (col.shape[0], LANE))


def _across(stat, width):
    return jnp.tile(stat, (1, width // LANE))


def _fox_fwd(proj3, kvf3, aq3, ak3, tokw, *, name):
    nb, seq, _ = proj3.shape
    t = _pick(seq, 512, LANE)
    tables, n_pairs, q_spec, k_spec, hp0 = _fox_grid(seq // t, tokw, t, True)

    def body(qt_ref, kt_ref, q_ref, k_ref, v_ref, aq_ref, ak_ref, o_ref, lse_ref, m_s, l_s, acc_s):
        i = pl.program_id(2)
        qi, ki = qt_ref[i], kt_ref[i]

        @pl.when(ki == 0)
        def _():
            m_s[...] = jnp.full(m_s.shape, NEG_BIG, F32)
            l_s[...] = jnp.zeros(l_s.shape, F32)
            acc_s[...] = jnp.zeros(acc_s.shape, F32)

        def step(masked):
            v = v_ref[...].astype(BF16)
            ss, _ = _fox_scores(q_ref[...], k_ref[...].astype(BF16), aq_ref[...], ak_ref[...], masked)
            for h, s in enumerate(ss):
                m_old = m_s[h]
                m_new = jnp.maximum(m_old, _lanes(jnp.max(s, axis=-1, keepdims=True)))
                alpha = jnp.exp(m_old - m_new)
                p = jnp.exp(s - _across(m_new, t))
                l_s[h] = alpha * l_s[h] + _lanes(jnp.sum(p, axis=-1, keepdims=True))
                acc_s[h] = alpha * acc_s[h] + _bdot(p, v, "nn")
                m_s[h] = m_new

        _on_blocks(qi, ki, step)

        @pl.when(ki == qi)
        def _():
            o_ref[...] = jnp.where(_head_masks()[0], acc_s[0] / l_s[0], acc_s[1] / l_s[1])
            lse_ref[...] = jnp.concatenate([m_s[0] + jnp.log(l_s[0]), m_s[1] + jnp.log(l_s[1])], axis=1)

    stat = pltpu.VMEM((2, t, LANE), F32)
    return pl.pallas_call(
        body, name=name,
        grid_spec=pltpu.PrefetchScalarGridSpec(
            num_scalar_prefetch=2, grid=(nb, hp0, n_pairs),
            in_specs=[q_spec(), k_spec(), k_spec(hp0), q_spec(), k_spec()],
            out_specs=[q_spec(), q_spec(wide=True)], scratch_shapes=[stat, stat, stat]),
        out_shape=[jax.ShapeDtypeStruct((nb, seq, tokw), F32), jax.ShapeDtypeStruct((nb, seq, 2 * tokw), F32)],
        compiler_params=_params(),
    )(*tables, proj3, kvf3, kvf3, aq3, ak3)


def _fox_bwd_common(q_ref, k_ref, v_ref, aq_ref, ak_ref, do_ref, lse_ref, delta_ref, masked):
    k, v = k_ref[...].astype(BF16), v_ref[...].astype(BF16)
    ss, qts = _fox_scores(q_ref[...], k, aq_ref[...], ak_ref[...], masked)
    do = do_ref[...]
    t = do.shape[0]
    out = []
    for h, (s, qt, msk) in enumerate(zip(ss, qts, _head_masks())):
        doh = jnp.where(msk, do, 0.0).astype(BF16)
        p = jnp.exp(s - _across(lse_ref[:, h * LANE:(h + 1) * LANE], t))
        ds = p * (_bdot(doh, v, "nt") - _across(delta_ref[:, h * LANE:(h + 1) * LANE], t))
        out.append((qt, doh, p, ds))
    return out, k


def _fox_bwd(proj3, kvf3, aq3, ak3, o3, dmixin3, lse3, tokw, *, name):
    nb, seq, _ = proj3.shape
    t = _pick(seq, 512, LANE)
    nblk = seq // t
    tables, n_pairs, q_spec, k_spec, hp0 = _fox_grid(nblk, tokw, t, True)
    whole = pl.BlockSpec((None, seq, LANE), lambda b, p, i, qt, kt: (b, 0, p))
    dfk_spec = pl.BlockSpec((None, None, nblk, SUBLANE, t), lambda b, p, i, qt, kt: (b, p, 0, 0, 0))

    def body(qt_ref, kt_ref, q_ref, k_ref, v_ref, aq_ref, ak_ref, o_ref, do_ref, lse_ref, dq_ref, dk_ref, dv_ref,
             dfk_ref, acc_s, row_s, delta_s):
        i = pl.program_id(2)
        qi, ki = qt_ref[i], kt_ref[i]

        @pl.when(i == 0)
        def _():
            dk_ref[...] = jnp.zeros(dk_ref.shape, F32)
            dv_ref[...] = jnp.zeros(dv_ref.shape, F32)
            dfk_ref[...] = jnp.zeros(dfk_ref.shape, F32)

        @pl.when(ki == 0)
        def _():
            acc_s[...] = jnp.zeros(acc_s.shape, F32)
            row_s[...] = jnp.zeros(row_s.shape, F32)
            prod = do_ref[...] * o_ref[...]
            delta_s[...] = jnp.concatenate(
                [_lanes(jnp.sum(jnp.where(msk, prod, 0.0), axis=-1, keepdims=True)) for msk in _head_masks()],
                axis=1)

        def step(masked):
            heads, k = _fox_bwd_common(q_ref, k_ref, v_ref, aq_ref, ak_ref, do_ref, lse_ref, delta_s, masked)
            rows = pl.ds(pl.multiple_of(ki * t, t), t)
            for h, ((qt, doh, p, ds), msk) in enumerate(zip(heads, _head_masks())):
                acc_s[h] += _bdot(ds, k, "nn")
                row_s[h] += _lanes(jnp.sum(ds, axis=-1, keepdims=True))
                dv_ref[rows, :] += _bdot(p, doh, "tn")
                dk_ref[rows, :] += jnp.where(msk, _bdot(ds, qt, "tn"), 0.0)
                dfk_ref[ki, h:h + 1, :] -= jnp.sum(ds, axis=0, keepdims=True)

        _on_blocks(qi, ki, step)

        @pl.when(ki == qi)
        def _():
            dq_ref[...] = jnp.where(_head_masks()[0], acc_s[0], acc_s[1]) * ATT_SCALE
            for h in range(2):
                dfk_ref[qi, 2 + h:3 + h, :] = row_s[h].T[0:1, :]

    out = jax.ShapeDtypeStruct((nb, seq, tokw), F32)
    stat = pltpu.VMEM((2, t, LANE), F32)
    return pl.pallas_call(
        body, name=name,
        grid_spec=pltpu.PrefetchScalarGridSpec(
            num_scalar_prefetch=2, grid=(nb, hp0, n_pairs),
            in_specs=[q_spec(), k_spec(), k_spec(hp0), q_spec(), k_spec(), q_spec(), q_spec(), q_spec(wide=True)],
            out_specs=[q_spec(), whole, whole, dfk_spec],
            scratch_shapes=[stat, stat, pltpu.VMEM((t, 2 * LANE), F32)]),
        out_shape=[out, out, out, jax.ShapeDtypeStruct((nb, hp0, nblk, SUBLANE, t), F32)],
        compiler_params=_params(),
    )(*tables, proj3, kvf3, kvf3, aq3, ak3, o3, dmixin3, lse3)


def _peer(k):
    x, y, c = lax.axis_index("x"), lax.axis_index("y"), lax.axis_index("c")
    return (1 - x if k & 4 else x, 1 - y if k & 2 else y, 1 - c if k & 1 else c)


def _dev_index(p):
    return 4 * p[0] + 2 * p[1] + p[2]


def _exchange(send, *, gather, name):
    block = send.shape[-2:]

    def body(s_ref, o_ref, send_sems, recv_sems, local_sem):
        me = _peer(0)
        mine = pltpu.make_async_copy(s_ref if gather else s_ref.at[_dev_index(me)], o_ref.at[_dev_index(me)],
                                     local_sem)
        mine.start()
        sends, recvs = [], []
        for k in range(1, N_DEV):
            peer = _peer(k)
            src = s_ref if gather else s_ref.at[_dev_index(peer)]
            sends.append(pltpu.make_async_remote_copy(
                src_ref=src, dst_ref=o_ref.at[_dev_index(me)], send_sem=send_sems.at[k - 1],
                recv_sem=recv_sems.at[k - 1], device_id=peer, device_id_type=MESH_T))
            recvs.append(pltpu.make_async_remote_copy(
                src_ref=src, dst_ref=o_ref.at[_dev_index(peer)], send_sem=send_sems.at[k - 1],
                recv_sem=recv_sems.at[k - 1], device_id=peer, device_id_type=MESH_T))
        for cp in sends:
            cp.start()
        for cp in recvs:
            cp.wait_recv()
        for cp in sends:
            cp.wait_send()
        mine.wait()

    return pl.pallas_call(
        body, name=name,
        in_specs=[pl.BlockSpec(memory_space=pltpu.HBM)], out_specs=pl.BlockSpec(memory_space=pltpu.HBM),
        out_shape=jax.ShapeDtypeStruct((N_DEV,) + block, send.dtype),
        scratch_shapes=[pltpu.SemaphoreType.DMA((N_DEV - 1,)), pltpu.SemaphoreType.DMA((N_DEV - 1,)),
                        pltpu.SemaphoreType.DMA],
    )(send)


_HBM = pl.BlockSpec(memory_space=pltpu.HBM)
CHIP_RELATIONS = (2, 4, 6)


def _chip_index(p):
    return 2 * p[0] + p[1]


def _run_copies(sends, recvs):
    for cp in sends:
        cp.start()
    for cp in recvs:
        cp.wait_recv()
    for cp in sends:
        cp.wait_send()


def _gather_shards(whole, halved, side_by_side, *, name):
    nw, nh = len(whole), len(halved) + len(side_by_side)
    n = nw + nh
    n_sem = 3 * nw + 6 * nh

    def body(*refs):
        ins, outs, send_sems, recv_sems = refs[:n], refs[n:2 * n], refs[2 * n], refs[2 * n + 1]
        me, sib = _peer(0), _peer(1)
        q, c = _chip_index(me), me[2]

        def copy(src, dst, s, to):
            return pltpu.make_async_remote_copy(src_ref=src, dst_ref=dst, send_sem=send_sems.at[s],
                                                recv_sem=recv_sems.at[s], device_id=to, device_id_type=MESH_T)

        sends, recvs, passes = [], [], []
        for j, k in enumerate(CHIP_RELATIONS):
            peer = _peer(k)
            pq = _chip_index(peer)
            for i in range(nw):
                sends.append(copy(ins[i], outs[i].at[q], 3 * i + j, peer))
                recvs.append(copy(ins[i], outs[i].at[pq], 3 * i + j, peer))
            for i in range(nh):
                src, out, s = ins[nw + i], outs[nw + i], 3 * nw + 6 * i + j
                if i < len(halved):
                    place = lambda chip, half, out=out: out.at[chip, half]
                else:
                    cols = src.shape[-1]
                    place = lambda chip, half, out=out, cols=cols: out.at[
                        half, :, pl.ds(pl.multiple_of(chip * cols, LANE), cols)]
                sends.append(copy(src.at[c], place(q, c), s, peer))
                passes.append((copy(src.at[c], place(pq, c), s, peer), copy(place(pq, c), place(pq, c), s + 3, sib),
                               copy(place(pq, c), place(pq, 1 - c), s + 3, sib)))
        for cp in sends:
            cp.start()
        for arrival, hand_over, _ in passes:
            arrival.wait_recv()
            hand_over.start()
        for cp in recvs:
            cp.wait_recv()
        for _, _, from_sibling in passes:
            from_sibling.wait_recv()
        for cp in sends + [hand_over for _, hand_over, _ in passes]:
            cp.wait_send()

    arrays = list(whole) + list(halved) + list(side_by_side)
    return pl.pallas_call(
        body, name=name, in_specs=[_HBM] * n, out_specs=[_HBM] * n,
        out_shape=[jax.ShapeDtypeStruct((N_CHIP,) + a.shape, a.dtype) for a in list(whole) + list(halved)]
        + [jax.ShapeDtypeStruct(a.shape[:-1] + (N_CHIP * a.shape[-1],), a.dtype) for a in side_by_side],
        scratch_shapes=[pltpu.SemaphoreType.DMA((n_sem,)), pltpu.SemaphoreType.DMA((n_sem,))],
    )(*arrays)


def _to_sibling(grads, *, name):
    n = len(grads)

    def body(*refs):
        ins, outs, send_sems, recv_sems = refs[:n], refs[n:2 * n], refs[2 * n], refs[2 * n + 1]
        c = lax.axis_index("c")
        sends = [pltpu.make_async_remote_copy(src_ref=ins[i].at[:, 1 - c], dst_ref=outs[i], send_sem=send_sems.at[i],
                                              recv_sem=recv_sems.at[i], device_id=_peer(1), device_id_type=MESH_T)
                 for i in range(n)]
        _run_copies(sends, sends)

    return pl.pallas_call(
        body, name=name, in_specs=[_HBM] * n, out_specs=[_HBM] * n,
        out_shape=[jax.ShapeDtypeStruct(g.shape[:1] + g.shape[2:], g.dtype) for g in grads],
        scratch_shapes=[pltpu.SemaphoreType.DMA((n,)), pltpu.SemaphoreType.DMA((n,))],
    )(*grads)


def _pair_add(grads, from_sibling, qc, *, name):
    _, _, rows, cols = grads.shape
    tile = _pick(rows, 1024, SUBLANE)

    def body(qc_ref, g_ref, s_ref, o_ref):
        del qc_ref
        o_ref[...] = g_ref[...] + s_ref[...]

    spec = pl.BlockSpec((None, tile, cols), lambda j, i, qc: (j, i, 0))
    return pl.pallas_call(
        body, name=name,
        grid_spec=pltpu.PrefetchScalarGridSpec(
            num_scalar_prefetch=1, grid=(N_CHIP, rows // tile),
            in_specs=[pl.BlockSpec((None, None, tile, cols), lambda j, i, qc: (j, qc[1], i, 0)), spec],
            out_specs=spec),
        out_shape=jax.ShapeDtypeStruct((N_CHIP, rows, cols), F32), compiler_params=_params(),
    )(qc, grads, from_sibling)


def _to_chips(sums, *, name):
    n = len(sums)

    def body(*refs):
        ins, outs, send_sems, recv_sems = refs[:n], refs[n:2 * n], refs[2 * n], refs[2 * n + 1]
        sends, recvs = [], []
        for j, k in enumerate(CHIP_RELATIONS):
            peer = _peer(k)
            for i in range(n):
                sem = dict(send_sem=send_sems.at[3 * i + j], recv_sem=recv_sems.at[3 * i + j], device_id=peer,
                           device_id_type=MESH_T)
                src = ins[i].at[_chip_index(peer)]
                sends.append(pltpu.make_async_remote_copy(src_ref=src, dst_ref=outs[i].at[j], **sem))
                recvs.append(pltpu.make_async_remote_copy(src_ref=src, dst_ref=outs[i].at[j], **sem))
        _run_copies(sends, recvs)

    return pl.pallas_call(
        body, name=name, in_specs=[_HBM] * n, out_specs=[_HBM] * n,
        out_shape=[jax.ShapeDtypeStruct((3,) + g.shape[1:], g.dtype) for g in sums],
        scratch_shapes=[pltpu.SemaphoreType.DMA((3 * n,)), pltpu.SemaphoreType.DMA((3 * n,))],
    )(*sums)


def _swap_halves(arrays, *, name):
    n = len(arrays)

    def body(*refs):
        outs, send_sems, recv_sems = refs[n:2 * n], refs[2 * n], refs[2 * n + 1]
        c = lax.axis_index("c")
        sib = _peer(1)
        sends, recvs = [], []
        for i in range(n):
            sem = dict(send_sem=send_sems.at[i], recv_sem=recv_sems.at[i], device_id=sib, device_id_type=MESH_T)
            sends.append(pltpu.make_async_remote_copy(src_ref=outs[i].at[c], dst_ref=outs[i].at[c], **sem))
            recvs.append(pltpu.make_async_remote_copy(src_ref=outs[i].at[c], dst_ref=outs[i].at[1 - c], **sem))
        _run_copies(sends, recvs)

    return pl.pallas_call(
        body, name=name, in_specs=[_HBM] * n, out_specs=[_HBM] * n,
        out_shape=[jax.ShapeDtypeStruct(a.shape, a.dtype) for a in arrays],
        input_output_aliases={i: i for i in range(n)},
        scratch_shapes=[pltpu.SemaphoreType.DMA((n,)), pltpu.SemaphoreType.DMA((n,))],
    )(*arrays)


def _adam_math(g, w, m, v):
    bc1 = 1.0 - ADAM_B1 ** ADAM_STEP
    bc2 = 1.0 - ADAM_B2 ** ADAM_STEP
    m_new = ADAM_B1 * m + (1.0 - ADAM_B1) * g
    v_new = ADAM_B2 * v + (1.0 - ADAM_B2) * (g * g)
    delta = -ADAM_LR * ((m_new / bc1) / (jnp.sqrt(v_new / bc2) + ADAM_EPS) + ADAM_WD * w)
    return delta, m_new, v_new


def _adamw_half(sums, parts, w, m, v, qc, *, name):
    _, rows, cols = sums.shape
    tile = _pick(rows, 384, SUBLANE)
    n_parts = parts.shape[0]

    def body(qc_ref, g_ref, p_ref, w_ref, m_ref, v_ref, go_ref, do_ref, mo_ref, vo_ref):
        del qc_ref
        g = g_ref[...]
        for k in range(n_parts):
            g = g + p_ref[k]
        delta, m_new, v_new = _adam_math(g, w_ref[...], m_ref[...], v_ref[...])
        go_ref[...] = g
        do_ref[...] = delta
        mo_ref[...] = m_new
        vo_ref[...] = v_new

    half = pl.BlockSpec((None, tile, cols), lambda i, qc: (qc[1], i, 0))
    shape = jax.ShapeDtypeStruct((2, rows, cols), F32)
    return pl.pallas_call(
        body, name=name,
        grid_spec=pltpu.PrefetchScalarGridSpec(
            num_scalar_prefetch=1, grid=(rows // tile,),
            in_specs=[pl.BlockSpec((None, tile, cols), lambda i, qc: (qc[0], i, 0)),
                      pl.BlockSpec((n_parts, tile, cols), lambda i, qc: (0, i, 0)), half, half, half],
            out_specs=[half] * 4),
        out_shape=[shape] * 4, compiler_params=_params(),
    )(qc, sums, parts, w, m, v)


def _adamw(parts, w, m, v, *, name):
    _, rows, cols = parts.shape
    tile = _pick(rows, 256, SUBLANE)

    def body(p_ref, w_ref, m_ref, v_ref, o_ref):
        g = p_ref[0]
        for i in range(1, N_DEV):
            g = g + p_ref[i]
        delta, m_new, v_new = _adam_math(g, w_ref[...], m_ref[...], v_ref[...])
        o_ref[0] = g
        o_ref[1] = delta
        o_ref[2] = m_new
        o_ref[3] = v_new

    spec = pl.BlockSpec((tile, cols), lambda i: (i, 0))
    return pl.pallas_call(
        body, name=name, grid=(rows // tile,),
        in_specs=[pl.BlockSpec((N_DEV, tile, cols), lambda i: (0, i, 0)), spec, spec, spec],
        out_specs=pl.BlockSpec((4, tile, cols), lambda i: (0, i, 0)),
        out_shape=jax.ShapeDtypeStruct((4, rows, cols), F32), compiler_params=_params(),
    )(parts, w, m, v)


def _layout(shapes, names, align):
    out, off = [], 0
    for n in names:
        size = math.prod(shapes[n])
        out.append((n, tuple(shapes[n]), off, size))
        off += _round_up(size, align)
    return out, off


def _pack(arrays, layout, total, lead=()):
    parts = []
    for i, (n, _, off, size) in enumerate(layout):
        end = layout[i + 1][2] if i + 1 < len(layout) else total
        flat = arrays[n].reshape(lead + (size,))
        if end - off > size:
            flat = jnp.pad(flat, [(0, 0)] * len(lead) + [(0, end - off - size)])
        parts.append(flat)
    return jnp.concatenate(parts, axis=len(lead))


def _unpack(flat, layout, lead=()):
    return {n: flat[..., off:off + size].reshape(lead + shape) for n, shape, off, size in layout}


def _to_shards(full, axis):
    shp = full.shape
    return jnp.moveaxis(full.reshape(shp[:axis] + (N_CHIP, shp[axis] // N_CHIP) + shp[axis + 1:]), axis, 0)


def _from_shards(shards, axis):
    x = jnp.moveaxis(shards, 0, axis)
    shp = x.shape
    return x.reshape(shp[:axis] + (shp[axis] * shp[axis + 1],) + shp[axis + 2:])


def _pad_cols(w, per, padded):
    lead = w.shape[:-1]
    x = w.reshape(lead + (-1, per))
    x = jnp.pad(x, [(0, 0)] * len(lead) + [(0, 0), (0, padded - per)])
    return x.reshape(lead + (-1,))


def _unpad_cols(w, per, padded):
    lead = w.shape[:-1]
    return w.reshape(lead + (-1, padded))[..., :per].reshape(lead + (-1,))


def _local_step(x, mem, target, W):
    nb, seq, d = x.shape
    n = nb * seq
    tokw = d - MEM_WIDTH
    heads = tokw // HEAD_DIM
    mlen = mem.shape[1]
    per = W["ffn_w_down"].shape[1] // 2
    per_p = _round_up(per, LANE)
    fp = 2 * per_p
    kvw = 2 * tokw + heads
    kvp = 2 * tokw + LANE
    gate_block = 2 * tokw // LANE

    x2d = x.reshape(n, d)
    mem2d = mem.reshape(nb * mlen, d)
    t2d = target.reshape(n, d)
    row = lambda a: a.reshape(1, -1)
    ones_tok = jnp.ones((1, tokw), F32)

    pool_bd = jax.scipy.linalg.block_diag(*[W["a_pool_w"][0, i] for i in range(len(POOL_WINDOWS))]).astype(BF16)
    kv_w = jnp.pad(W["kv_w"], ((0, 0), (0, kvp - kvw)))
    fb = jnp.pad(W["f_b"], (0, LANE - heads)).reshape(1, LANE)
    w_up = W.get("ffn_w_up_padded")
    if w_up is None:
        w_up = jnp.stack([_pad_cols(W["ffn_w_up"][l], per, per_p) for l in range(DEPTH)])
    w_down = [jnp.pad(W["ffn_w_down"][l].reshape(2, per, d), ((0, 0), (0, per_p - per), (0, 0))).reshape(fp, d)
              for l in range(DEPTH)]
    conv_w = [jnp.pad(_pad_cols(W["ffn_conv_w"][l], per, per_p), ((0, SUBLANE - CONV_WIDTH), (0, 0)))
              for l in range(DEPTH)]
    conv_b = [_pad_cols(W["ffn_conv_b"][l], per, per_p).reshape(1, 2 * fp) for l in range(DEPTH)]
    w_in = [W["a_w_in"][0], W["b_w_q"][0]]
    w_out = [W["a_w_out"][0], W["b_w_out"][0]]

    saved = []
    cur = x2d
    for l in range(DEPTH):
        s = {"x_in": cur}
        memkv = _mm(mem2d, W["mem_w_kv"][l], "nn", name=f"memkv{l}").reshape(nb, mlen, 2 * MEM_WIDTH)
        if l == 0:
            proj = _mm(cur, w_in[l], "nn", name="proj0")
            pooled = _pool_fwd(proj.reshape(nb, seq, d), tokw, name="pool_fwd").reshape(n, tokw)
            tok = _mm(pooled, pool_bd, "nn", name="pool_mix")
            scale = W["a_pool_scale"].reshape(1, tokw)
            s.update(pooled=pooled, mixed=tok, scale=scale)
        else:
            kvf = _mm(cur, kv_w, "nn", tn=kvp, name="kvf")
            kvf3 = kvf.reshape(nb, seq, kvp)
            gsum = _gate_fwd(kvf3, fb, gate_block, name="gate_fwd")[:, :, :heads]
            aq3, ak3 = _bias_lanes(gsum)
            proj = _mm(cur, w_in[l], "nn", name="proj1")
            o3, lse3 = _fox_fwd(proj.reshape(nb, seq, d), kvf3, aq3, ak3, tokw, name="fox_fwd")
            tok = o3.reshape(n, tokw)
            scale = ones_tok
            s.update(kvf3=kvf3, aq3=aq3, ak3=ak3, o3=o3, lse3=lse3)
        mixin = _memattn_fwd(tok, proj, memkv, scale, seq=seq, name=f"memattn_fwd{l}")
        mix = _mm(mixin, w_out[l], "nn", name=f"mix{l}")
        x1 = _ln_fwd(cur, mix, row(W["ln1_g"][l]), row(W["ln1_b"][l]), name=f"ln1_fwd{l}")
        up = _mm(x1, w_up, "nn", b_lead=l, tn=per_p, col_major=True, name=f"ffn_up{l}")
        act = _convgate_fwd(up.reshape(nb, seq, 2 * fp), conv_w[l], conv_b[l], name=f"convgate_fwd{l}")
        act = act.reshape(n, fp)
        ffn = _mm(act, w_down[l], "nn", tk=fp, name=f"ffn_down{l}")
        s.update(proj=proj, memkv=memkv, mixin=mixin, mix=mix, x1=x1, up=up, act=act, ffn=ffn)
        saved.append(s)
        if l + 1 < DEPTH:
            cur = _ln_fwd(x1, ffn, row(W["ln2_g"][l]), row(W["ln2_b"][l]), name=f"ln2_fwd{l}")

    G = {}
    ln_g = {k: [None] * DEPTH for k in ("ln1_g", "ln1_b", "ln2_g", "ln2_b")}
    stack = {k: [None] * DEPTH for k in ("mem_w_kv", "ffn_w_up", "ffn_conv_w", "ffn_conv_b", "ffn_w_down")}
    dx_terms = None
    loss = None
    for l in reversed(range(DEPTH)):
        s = saved[l]
        g2 = row(W["ln2_g"][l])
        if l == DEPTH - 1:
            dres, dffn, loss, dg, db = _final_ln_loss(s["x1"], s["ffn"], t2d, g2, row(W["ln2_b"][l]),
                                                      name="final_ln_loss")
        else:
            dres, dffn, dg, db = _ln_bwd(s["x1"], s["ffn"], g2, dx_terms, name=f"ln2_bwd{l}")
        ln_g["ln2_g"][l], ln_g["ln2_b"][l] = dg[0], db[0]
        dact = _mm(dffn, w_down[l], "nt", tn=fp, name=f"ffn_down_dx{l}")
        stack["ffn_w_down"][l] = _mm(s["act"], dffn, "tn", tm=per_p, tk=DW_ROWS, trim=("rows", per),
                                     name=f"ffn_down_dw{l}")
        du3, dg3, dcw, dcb = _convgate_bwd(s["up"].reshape(nb, seq, 2 * fp), dact.reshape(nb, seq, fp), conv_w[l],
                                           conv_b[l], name=f"convgate_bwd{l}")
        du, dgt = du3.reshape(n, fp), dg3.reshape(n, fp)
        dx1_u = _mm(du, w_up, "nt", b_lead=l, tk=fp, name=f"ffn_up_dx_u{l}")
        dx1_ffn = _mm(dgt, w_up, "nt", b_lead=l, tk=fp, b_col0=fp, adds=[dx1_u], name=f"ffn_up_dx_g{l}")
        stack["ffn_w_up"][l] = [_mm(s["x1"], part, "tn", tm=d, tn=per_p, tk=DW_ROWS, trim=("cols", per),
                                    name=f"ffn_up_dw_{nm}{l}") for nm, part in (("u", du), ("g", dgt))]
        stack["ffn_conv_w"][l] = dcw[:CONV_WIDTH]
        stack["ffn_conv_b"][l] = dcb[0]
        dres1, dmix, dg, db = _ln_bwd(s["x_in"], s["mix"], row(W["ln1_g"][l]), [dres, dx1_ffn], name=f"ln1_bwd{l}")
        ln_g["ln1_g"][l], ln_g["ln1_b"][l] = dg[0], db[0]
        dmixin = _mm(dmix, w_out[l], "nt", name=f"mix_dx{l}")
        d_w_out = _mm(s["mixin"], dmix, "tn", tm=d, tk=DW_ROWS, name=f"mix_dw{l}")
        if l == 0:
            G["a_w_out"] = d_w_out[None]
            dmixed, dscale = _scale_bwd(dmixin, s["mixed"], s["scale"], name="scale_bwd")
            G["a_pool_scale"] = dscale
            dpooled = _mm(dmixed, pool_bd, "nt", name="pool_mix_dx")
            dpw = _mm(s["pooled"], dmixed, "tn", tm=tokw, tk=DW_ROWS, name="pool_mix_dw")
            grp = tokw // len(POOL_WINDOWS)
            G["a_pool_w"] = jnp.stack([dpw[i * grp:(i + 1) * grp, i * grp:(i + 1) * grp]
                                       for i in range(len(POOL_WINDOWS))])[None]
            dtok = _pool_bwd(dpooled.reshape(nb, seq, tokw), name="pool_bwd").reshape(n, tokw)
            extra = []
        else:
            G["b_w_out"] = d_w_out[None]
            p3 = s["proj"].reshape(nb, seq, d)
            dm3 = dmixin.reshape(nb, seq, d)
            dq3, dk3, dv3, dfk = _fox_bwd(p3, s["kvf3"], s["aq3"], s["ak3"], s["o3"], dm3, s["lse3"], tokw,
                                          name="fox_bwd")
            dtok = dq3.reshape(n, tokw)
            dfk = jnp.swapaxes(dfk[:, :, :, 0:2, :] + dfk[:, :, :, 2:4, :], 2, 3).reshape(nb, heads, seq)
            dgsum = jnp.swapaxes(dfk, 1, 2)
            dgsum = jnp.pad(dgsum, ((0, 0), (0, 0), (0, LANE - heads)))
            df3, dfb = _gate_bwd(s["kvf3"], fb, dgsum, gate_block, heads, name="gate_bwd")
            G["f_b"] = dfb[0, :heads]
            dkvf = [(dk3.reshape(n, tokw), 0, "k"), (dv3.reshape(n, tokw), tokw, "v"),
                    (df3.reshape(n, LANE), 2 * tokw, "f")]
            dx_kv = []
            for part, col0, nm in dkvf:
                dx_kv = [_mm(part, kv_w, "nt", b_col0=col0, adds=dx_kv, name=f"kvf_dx_{nm}")]
            extra = dx_kv
            G["kv_w"] = jnp.concatenate([_mm(s["x_in"], part, "tn", tm=d, tk=DW_ROWS, name=f"kvf_dw_{nm}")
                                         for part, _, nm in dkvf], axis=1)[:, :kvw]
        dproj, dmemkv = _memattn_bwd(dmixin, dtok, s["proj"], s["memkv"], seq=seq, name=f"memattn_bwd{l}")
        stack["mem_w_kv"][l] = _mm(mem2d, dmemkv.reshape(nb * mlen, 2 * MEM_WIDTH), "tn", tm=d, tk=DW_ROWS,
                                   name=f"memkv_dw{l}")
        G["a_w_in" if l == 0 else "b_w_q"] = _mm(s["x_in"], dproj, "tn", tm=d, tk=DW_ROWS, name=f"proj_dw{l}")[None]
        if l == 0:
            grad_x = _mm(dproj, w_in[l], "nt", adds=[dres1], name="proj_dx0")
        else:
            dx_terms = [dres1, _mm(dproj, w_in[l], "nt", name="proj_dx1")] + extra
    for k, v in ln_g.items():
        G[k] = jnp.stack(v)
    G["mem_w_kv"] = jnp.stack(stack["mem_w_kv"])
    G["ffn_w_up"] = jnp.stack([jnp.concatenate(ug, axis=0) for ug in stack["ffn_w_up"]], axis=1)
    G["ffn_conv_w"] = jnp.stack([_unpad_cols(g, per, per_p) for g in stack["ffn_conv_w"]])
    G["ffn_conv_b"] = jnp.stack([_unpad_cols(g, per, per_p) for g in stack["ffn_conv_b"]])
    G["ffn_w_down"] = jnp.stack([g.reshape(N_CHIP, per // 2, d) for g in stack["ffn_w_down"]], axis=1)
    return loss[0, 0], grad_x.reshape(nb, seq, d), G


def kernel(x, mem, a_w_in, a_pool_w, a_pool_scale, a_w_out, b_w_q, b_w_out, kv_w, f_b, mem_w_kv, ln1_g, ln1_b, ln2_g, ln2_b, ffn_w_up, ffn_conv_w, ffn_conv_b, ffn_w_down, loss_target, m_a_w_in, m_a_pool_w, m_a_pool_scale, m_a_w_out, m_b_w_q, m_b_w_out, m_kv_w, m_f_b, m_mem_w_kv, m_ln1_g, m_ln1_b, m_ln2_g, m_ln2_b, m_ffn_w_up, m_ffn_conv_w, m_ffn_conv_b, m_ffn_w_down, v_a_w_in, v_a_pool_w, v_a_pool_scale, v_a_w_out, v_b_w_q, v_b_w_out, v_kv_w, v_f_b, v_mem_w_kv, v_ln1_g, v_ln1_b, v_ln2_g, v_ln2_b, v_ffn_w_up, v_ffn_conv_w, v_ffn_conv_b, v_ffn_w_down):
    w_loc = dict(a_w_in=a_w_in, a_pool_w=a_pool_w, a_pool_scale=a_pool_scale, a_w_out=a_w_out, b_w_q=b_w_q,
                 b_w_out=b_w_out, kv_w=kv_w, f_b=f_b, mem_w_kv=mem_w_kv, ln1_g=ln1_g, ln1_b=ln1_b, ln2_g=ln2_g,
                 ln2_b=ln2_b, ffn_w_up=ffn_w_up, ffn_conv_w=ffn_conv_w, ffn_conv_b=ffn_conv_b, ffn_w_down=ffn_w_down)
    m_loc = dict(a_w_in=m_a_w_in, a_pool_w=m_a_pool_w, a_pool_scale=m_a_pool_scale, a_w_out=m_a_w_out,
                 b_w_q=m_b_w_q, b_w_out=m_b_w_out, kv_w=m_kv_w, f_b=m_f_b, mem_w_kv=m_mem_w_kv, ln1_g=m_ln1_g,
                 ln1_b=m_ln1_b, ln2_g=m_ln2_g, ln2_b=m_ln2_b, ffn_w_up=m_ffn_w_up, ffn_conv_w=m_ffn_conv_w,
                 ffn_conv_b=m_ffn_conv_b, ffn_w_down=m_ffn_w_down)
    v_loc = dict(a_w_in=v_a_w_in, a_pool_w=v_a_pool_w, a_pool_scale=v_a_pool_scale, a_w_out=v_a_w_out,
                 b_w_q=v_b_w_q, b_w_out=v_b_w_out, kv_w=v_kv_w, f_b=v_f_b, mem_w_kv=v_mem_w_kv, ln1_g=v_ln1_g,
                 ln1_b=v_ln1_b, ln2_g=v_ln2_g, ln2_b=v_ln2_b, ffn_w_up=v_ffn_w_up, ffn_conv_w=v_ffn_conv_w,
                 ffn_conv_b=v_ffn_conv_b, ffn_w_down=v_ffn_w_down)
    x_i, y_i, c = lax.axis_index("x"), lax.axis_index("y"), lax.axis_index("c")
    q = 2 * x_i + y_i
    qc = jnp.stack([q, c]).astype(jnp.int32)
    shapes = {k: v.shape for k, v in w_loc.items()}

    def halves(a):
        if a.ndim == 3 and a.shape[0] == 2:
            return a
        rows = math.prod(a.shape[:-1])
        return a.reshape(2, rows // 2, a.shape[-1])

    own = {k: (w_loc[k] if k in GATHER_F32 else w_loc[k].astype(BF16)) for k in SHARDED}
    by_chip = ("a_pool_scale",) + tuple(k for k in BIG if k != "ffn_w_up")
    per = shapes["ffn_w_up"][-1]
    up_own = jnp.pad(own["ffn_w_up"], ((0, 0), (0, 0), (0, _round_up(per, LANE) - per)))
    gathered = _gather_shards([own["a_pool_scale"]], [halves(own[k]) for k in by_chip[1:]], [up_own],
                              name="gather_weights")
    W = {k: _from_shards(lax.dynamic_update_slice_in_dim(g.reshape((N_CHIP,) + shapes[k]), own[k][None], q, axis=0),
                         SHARD_AXIS[k])
         for k, g in zip(by_chip, gathered)}
    W["ffn_w_up_padded"] = lax.dynamic_update_slice_in_dim(gathered[-1], up_own, q * up_own.shape[-1], axis=2)
    for k in REPLICATED:
        W[k] = w_loc[k]

    loss_part, grad_x, G = _local_step(x, mem, loss_target, W)
    loss = lax.psum(loss_part, ("x", "y", "c"))

    g_chip = [G[k] if k in GRADS_BY_CHIP else _to_shards(G[k], SHARD_AXIS[k]) for k in BIG]
    g_chip = [g.reshape((N_CHIP,) + halves(w_loc[k]).shape) for k, g in zip(BIG, g_chip)]
    from_sib = _to_sibling(g_chip, name="grads_to_sibling")
    sums = [_pair_add(g, s, qc, name=f"pair_add_{k}") for k, g, s in zip(BIG, g_chip, from_sib)]
    parts = _to_chips(sums, name="scatter_grads")
    res = []
    for k, g, p in zip(BIG, sums, parts):
        res.extend(_adamw_half(g, p, halves(w_loc[k]), halves(m_loc[k]), halves(v_loc[k]), qc, name=f"adamw_{k}"))
    res = _swap_halves(res, name="swap_halves")
    out = {k: [r.reshape(shapes[k]) for r in res[4 * i:4 * i + 4]] for i, k in enumerate(BIG)}

    lay_r, tot_r = _layout(shapes, REPLICATED, PACK_COLS)
    rep_rows = tot_r // PACK_COLS
    rows = _round_up(rep_rows + 1, SUBLANE)
    scale_w = shapes["a_pool_scale"][-1]

    def small(rep, scale_row):
        lead = scale_row.shape[:-2]
        rep = jnp.broadcast_to(_pack(rep, lay_r, tot_r).reshape(rep_rows, PACK_COLS), lead + (rep_rows, PACK_COLS))
        pad = [(0, 0)] * len(lead)
        return jnp.concatenate([rep, jnp.pad(scale_row, pad + [(0, rows - rep_rows - 1), (0, PACK_COLS - scale_w)])],
                               axis=-2)

    g_scale = jnp.repeat(_to_shards(G["a_pool_scale"], 1), 2, axis=0)
    sm_parts = _exchange(small(G, g_scale), gather=False, name="scatter_small")
    sm = _adamw(sm_parts, *[small(d, d["a_pool_scale"]) for d in (w_loc, m_loc, v_loc)], name="adamw_small")
    out_r = _unpack(sm[:, :rep_rows].reshape(4, tot_r), lay_r, lead=(4,))
    for k in REPLICATED:
        out[k] = [out_r[k][a] for a in range(4)]
    out["a_pool_scale"] = [sm[a, rep_rows:rep_rows + 1, :scale_w] for a in range(4)]

    outs = [loss, grad_x]
    for a in range(4):
        for k in WEIGHTS:
            outs.append(out[k][a])
    return tuple(outs)
```

```python
import functools
import math

import jax
import jax.numpy as jnp
from jax import lax
from jax.experimental import pallas as pl
from jax.experimental.pallas import tpu as pltpu

F32 = jnp.float32
BF16 = jnp.bfloat16

HEAD_DIM = 64
MEM_HEADS = 4
MEM_WIDTH = MEM_HEADS * HEAD_DIM
POOL_WINDOWS = (2, 4, 8, 16)
MAX_WINDOW = 16
CONV_WIDTH = 3
DEPTH = 2
DN_ALPHA = (2.0 * DEPTH) ** 0.25
LN_EPS = 1e-5
ATT_SCALE = HEAD_DIM ** -0.5
NEG_BIG = -1e30

ADAM_LR = 0.001
ADAM_B1 = 0.9
ADAM_B2 = 0.999
ADAM_EPS = 1e-08
ADAM_WD = 0.01
ADAM_STEP = 10

LANE = 128
SUBLANE = 8
PACK_COLS = 1024
DW_ROWS = 1024
VMEM_LIMIT = 56 * 1024 * 1024
N_DEV = 8
N_CHIP = 4
MESH_T = pl.DeviceIdType.MESH

SHARDED = ("a_w_in", "a_pool_scale", "a_w_out", "b_w_q", "b_w_out", "kv_w", "mem_w_kv", "ffn_w_up",
           "ffn_conv_w", "ffn_w_down")
SHARD_AXIS = {"a_w_in": 1, "a_pool_scale": 1, "a_w_out": 1, "b_w_q": 1, "b_w_out": 1, "kv_w": 1, "mem_w_kv": 1,
              "ffn_w_up": 2, "ffn_conv_w": 2, "ffn_w_down": 1}
GATHER_F32 = ("a_pool_scale", "ffn_conv_w")
BIG = tuple(k for k in SHARDED if k != "a_pool_scale")
GRADS_BY_CHIP = ("ffn_w_up", "ffn_w_down")
REPLICATED = ("a_pool_w", "f_b", "ln1_g", "ln1_b", "ln2_g", "ln2_b", "ffn_conv_b")
WEIGHTS = ("a_w_in", "a_pool_w", "a_pool_scale", "a_w_out", "b_w_q", "b_w_out", "kv_w", "f_b", "mem_w_kv",
           "ln1_g", "ln1_b", "ln2_g", "ln2_b", "ffn_w_up", "ffn_conv_w", "ffn_conv_b", "ffn_w_down")


def _round_up(n, m):
    return -(-n // m) * m


def _pick(dim, pref, unit=LANE):
    if dim <= pref:
        return dim
    t = (pref // unit) * unit
    while t >= unit:
        if dim % t == 0:
            return t
        t -= unit
    raise ValueError(f"no tile for {dim} <= {pref}")


def _params():
    return pltpu.CompilerParams(vmem_limit_bytes=VMEM_LIMIT)


_DIMS = {"nn": ((1,), (0,)), "nt": ((1,), (1,)), "tn": ((0,), (0,))}


def _bdot(a, b, mode):
    return lax.dot_general(a.astype(BF16), b.astype(BF16), (_DIMS[mode], ((), ())), preferred_element_type=F32)


def _mm(a, b, mode, *, name, tm=512, tn=1024, tk=2048, adds=(), b_col0=0, trim=None, col_major=False, b_lead=None):
    b_shape = b.shape if b_lead is None else b.shape[1:]
    if mode == "nn":
        (M, K), (K2, N) = a.shape, b_shape
    elif mode == "nt":
        (M, K), (N, K2) = a.shape, b_shape
        K2 = K if b_col0 + K <= K2 else -1
    else:
        (K, M), (K2, N) = a.shape, b_shape
    assert K == K2 and (mode == "nt" or b_col0 == 0), (name, a.shape, b.shape)
    tm, tn = _pick(M, tm, SUBLANE if mode != "tn" else LANE), _pick(N, tn)
    tk = _pick(K, tk, LANE if mode != "tn" else SUBLANE)
    nk = K // tk
    assert b_col0 % tk == 0, (name, b_col0, tk)
    koff = b_col0 // tk
    n_add = len(adds)

    def body(*refs):
        a_ref, b_ref = refs[0], refs[1]
        add_refs = refs[2:2 + n_add]
        o_ref, acc_ref = refs[2 + n_add], refs[3 + n_add]
        part = _bdot(a_ref[...], b_ref[...], mode)

        def finish(r):
            for ar in add_refs:
                r = r + ar[...]
            if trim is not None:
                r = r[:, :trim[1]] if trim[0] == "cols" else r[:trim[1], :]
            o_ref[...] = r

        if nk == 1:
            finish(part)
        else:
            k = pl.program_id(2)

            @pl.when(k == 0)
            def _():
                acc_ref[...] = part

            @pl.when(k > 0)
            def _():
                acc_ref[...] += part

            @pl.when(k == nk - 1)
            def _():
                finish(acc_ref[...])

    def spec(block, index):
        if col_major:
            return pl.BlockSpec(block, lambda j, i, k: index(i, j, k))
        return pl.BlockSpec(block, index)

    def b_spec_of(block, index):
        if b_lead is None:
            return spec(block, index)
        return spec((None,) + block, lambda i, j, k: (b_lead,) + index(i, j, k))

    if mode == "nn":
        a_spec = spec((tm, tk), lambda i, j, k: (i, k))
        b_spec = b_spec_of((tk, tn), lambda i, j, k: (k, j))
    elif mode == "nt":
        a_spec = spec((tm, tk), lambda i, j, k: (i, k))
        b_spec = b_spec_of((tn, tk), lambda i, j, k: (j, k + koff))
    else:
        a_spec = spec((tk, tm), lambda i, j, k: (k, i))
        b_spec = b_spec_of((tk, tn), lambda i, j, k: (k, j))
    o_spec = spec((tm, tn), lambda i, j, k: (i, j))
    out_spec, out_shape = o_spec, (M, N)
    if trim is not None and trim[0] == "cols":
        out_spec, out_shape = spec((None, tm, trim[1]), lambda i, j, k: (j, i, 0)), (N // tn, M, trim[1])
    elif trim is not None:
        out_spec, out_shape = spec((None, trim[1], tn), lambda i, j, k: (i, 0, j)), (M // tm, trim[1], N)
    acc_shape = (tm, tn) if nk > 1 else (SUBLANE, LANE)
    return pl.pallas_call(
        body, name=name, grid=(N // tn, M // tm, nk) if col_major else (M // tm, N // tn, nk),
        in_specs=[a_spec, b_spec] + [o_spec] * n_add, out_specs=out_spec,
        out_shape=jax.ShapeDtypeStruct(out_shape, F32),
        scratch_shapes=[pltpu.VMEM(acc_shape, F32)],
        compiler_params=_params(),
    )(a, b, *adds)


def _rowwise(fn, tiled, full, outs_tiled, outs_acc, *, rows, tile, name, acc_period=None):
    n_tiles = rows // tile
    period = n_tiles if acc_period is None else acc_period
    arrays, in_specs = [], []
    for t in tiled:
        arr, width, cb = t if isinstance(t, tuple) else (t, t.shape[1], 0)
        arrays.append(arr)
        in_specs.append(pl.BlockSpec((tile, width), lambda i, cb=cb: (i, cb)))
    for f in full:
        arr, spec = f if isinstance(f, tuple) else (f, None)
        arrays.append(arr)
        in_specs.append(spec if spec is not None else pl.BlockSpec(arr.shape, lambda i, nd=arr.ndim: (0,) * nd))
    out_shape, out_specs = [], []
    for width, dt in outs_tiled:
        out_shape.append(jax.ShapeDtypeStruct((rows, width), dt))
        out_specs.append(pl.BlockSpec((tile, width), lambda i: (i, 0)))
    for acc in outs_acc:
        shape, dt = acc[0], acc[1]
        out_shape.append(jax.ShapeDtypeStruct(shape, dt))
        out_specs.append(acc[2] if len(acc) > 2 else pl.BlockSpec(shape, lambda i, nd=len(shape): (0,) * nd))
    n_in, n_t, n_a = len(arrays), len(outs_tiled), len(outs_acc)

    def body(*refs):
        vals = [r[...] for r in refs[:n_in]]
        o_t, o_a = fn(*vals)
        for r, v in zip(refs[n_in:n_in + n_t], o_t):
            r[...] = v.astype(r.dtype)
        first = pl.program_id(0) % period == 0
        for r, v in zip(refs[n_in + n_t:n_in + n_t + n_a], o_a):
            v = v.reshape(r.shape)

            @pl.when(first)
            def _(r=r, v=v):
                r[...] = v

            @pl.when(jnp.logical_not(first))
            def _(r=r, v=v):
                r[...] += v

    return pl.pallas_call(
        body, name=name, grid=(n_tiles,), in_specs=in_specs, out_specs=out_specs, out_shape=out_shape,
        compiler_params=_params(),
    )(*arrays)


def _ln_stats(h):
    mu = jnp.mean(h, axis=-1, keepdims=True)
    d = h - mu
    var = jnp.mean(d * d, axis=-1, keepdims=True)
    rstd = lax.rsqrt(var + LN_EPS)
    return d * rstd, rstd


def _ln_bwd_math(h, g, dy):
    xhat, rstd = _ln_stats(h)
    dxhat = dy * g
    dh = rstd * (dxhat - jnp.mean(dxhat, axis=-1, keepdims=True)
                 - xhat * jnp.mean(dxhat * xhat, axis=-1, keepdims=True))
    return dh, jnp.sum(dy * xhat, axis=0, keepdims=True), jnp.sum(dy, axis=0, keepdims=True)


def _ln_fwd(x, r, g, b, *, name):
    n, d = x.shape

    def fn(x, r, g, b):
        xhat, _ = _ln_stats(DN_ALPHA * x + r)
        return (xhat * g + b,), ()

    return _rowwise(fn, [x, r], [g, b], [(d, F32)], [], rows=n, tile=_pick(n, 512, SUBLANE), name=name)[0]


def _ln_bwd(x, r, g, dys, *, name):
    n, d = x.shape
    n_dy = len(dys)

    def fn(x, r, *rest):
        dy = rest[0]
        for e in rest[1:n_dy]:
            dy = dy + e
        dh, dg, db = _ln_bwd_math(DN_ALPHA * x + r, rest[n_dy], dy)
        return (DN_ALPHA * dh, dh), (dg, db)

    return _rowwise(fn, [x, r, *dys], [g], [(d, F32), (d, BF16)], [((1, d), F32), ((1, d), F32)],
                    rows=n, tile=_pick(n, 256, SUBLANE), name=name)


def _final_ln_loss(x, r, target, g, b, *, name):
    n, d = x.shape

    def fn(x, r, t, g, b):
        h = DN_ALPHA * x + r
        xhat, _ = _ln_stats(h)
        err = xhat * g + b - t
        loss = jnp.full((1, LANE), 0.5 * jnp.sum(err * err) / d, F32)
        dh, dg, db = _ln_bwd_math(h, g, err / d)
        return (DN_ALPHA * dh, dh), (loss, dg, db)

    return _rowwise(fn, [x, r, target], [g, b], [(d, F32), (d, BF16)],
                    [((1, LANE), F32), ((1, d), F32), ((1, d), F32)],
                    rows=n, tile=_pick(n, 256, SUBLANE), name=name)


def _mem_heads(qm):
    lane = lax.broadcasted_iota(jnp.int32, (1, MEM_WIDTH), 1)
    for h in range(MEM_HEADS):
        msk = (lane >= h * HEAD_DIM) & (lane < (h + 1) * HEAD_DIM)
        yield msk, jnp.where(msk, qm, 0.0).astype(BF16)


def _mem_softmax(qh, k):
    s = _bdot(qh, k, "nt") * ATT_SCALE
    p = jnp.exp(s - jnp.max(s, axis=-1, keepdims=True))
    return p / jnp.sum(p, axis=-1, keepdims=True)


def _memattn_fwd(tok, proj, memkv, scale, *, seq, name):
    n, tokw = tok.shape
    d = tokw + MEM_WIDTH
    tile = _pick(seq, 512, SUBLANE)

    def fn(tok, qm, kv, scale):
        k, v = kv[:, :MEM_WIDTH].astype(BF16), kv[:, MEM_WIDTH:].astype(BF16)
        out = jnp.zeros(qm.shape, F32)
        for msk, qh in _mem_heads(qm):
            out = jnp.where(msk, _bdot(_mem_softmax(qh, k), v, "nn"), out)
        return (jnp.concatenate([tok * scale, out], axis=1),), ()

    kv_spec = pl.BlockSpec((None,) + memkv.shape[1:], lambda i: (i // (seq // tile), 0, 0))
    return _rowwise(fn, [tok, (proj, MEM_WIDTH, tokw // MEM_WIDTH)], [(memkv, kv_spec), scale], [(d, BF16)], [],
                    rows=n, tile=tile, name=name)[0]


def _memattn_bwd(dmixin, dtok, proj, memkv, *, seq, name):
    n, tokw = dtok.shape
    d = tokw + MEM_WIDTH
    tile = _pick(seq, 512, SUBLANE)

    def fn(dmo, dtok, qm, kv):
        k, v = kv[:, :MEM_WIDTH].astype(BF16), kv[:, MEM_WIDTH:].astype(BF16)
        dq = jnp.zeros(qm.shape, F32)
        dk = jnp.zeros(k.shape, F32)
        dv = jnp.zeros(v.shape, F32)
        for msk, qh in _mem_heads(qm):
            p = _mem_softmax(qh, k)
            doh = jnp.where(msk, dmo, 0.0).astype(BF16)
            dv = dv + _bdot(p, doh, "tn")
            dp = _bdot(doh, v, "nt")
            ds = (p * (dp - jnp.sum(dp * p, axis=-1, keepdims=True))).astype(BF16)
            dq = jnp.where(msk, _bdot(ds, k, "nn") * ATT_SCALE, dq)
            dk = dk + _bdot(ds, qh, "tn") * ATT_SCALE
        return (jnp.concatenate([dtok, dq], axis=1),), (jnp.concatenate([dk, dv], axis=1),)

    tpe = seq // tile
    kv_spec = pl.BlockSpec((None,) + memkv.shape[1:], lambda i: (i // tpe, 0, 0))
    return _rowwise(fn, [(dmixin, MEM_WIDTH, tokw // MEM_WIDTH), dtok, (proj, MEM_WIDTH, tokw // MEM_WIDTH)],
                    [(memkv, kv_spec)], [(d, BF16)], [(memkv.shape, F32, kv_spec)],
                    rows=n, tile=tile, name=name, acc_period=tpe)


def _scale_bwd(dmixin, mixed, scale, *, name):
    n, tokw = mixed.shape

    def fn(dt, mixed, scale):
        return (dt * scale,), (jnp.sum(dt * mixed, axis=0, keepdims=True),)

    return _rowwise(fn, [(dmixin, tokw, 0), mixed], [scale], [(tokw, BF16)], [((1, tokw), F32)],
                    rows=n, tile=_pick(n, 512, SUBLANE), name=name)


def _chunk_rows(seq):
    return _pick(seq, 512, SUBLANE)


def _load_ext(ref, c, rows, before, after, seq):
    lo, hi = c * rows - before, (c + 1) * rows + after
    parts = []
    if lo < 0:
        parts.append(jnp.zeros((-lo, ref.shape[1]), F32))
    parts.append(ref[max(lo, 0):min(hi, seq), :])
    if hi > seq:
        parts.append(jnp.zeros((hi - seq, ref.shape[1]), F32))
    return parts[0] if len(parts) == 1 else jnp.concatenate(parts, axis=0)


def _down(x, k):
    return pltpu.roll(x, k, 0)


def _up(x, k):
    return pltpu.roll(x, x.shape[0] - k, 0)


def _window_sums(ext, shift, col0, group):
    lane = col0 + lax.broadcasted_iota(jnp.int32, (1, ext.shape[1]), 1)
    gidx = lane // group
    s = ext
    out = None
    k = 1
    for gi, w in enumerate(POOL_WINDOWS):
        while k < w:
            s = s + shift(s, k)
            k *= 2
        out = s if out is None else jnp.where(gidx >= gi, s, out)
    return out, jnp.left_shift(2, jnp.minimum(gidx, len(POOL_WINDOWS) - 1))


def _pool_fwd(proj3, tokw, *, name):
    nb, seq, _ = proj3.shape
    rows = _chunk_rows(seq)
    group = tokw // len(POOL_WINDOWS)

    def body(u_ref, o_ref):
        col0 = pl.program_id(1) * LANE
        for c in range(seq // rows):
            ext = _load_ext(u_ref, c, rows, MAX_WINDOW, 0, seq)
            sums, win = _window_sums(ext, _down, col0, group)
            t = c * rows + lax.broadcasted_iota(jnp.int32, (rows, 1), 0)
            count = jnp.minimum(t + 1, win).astype(F32)
            o_ref[c * rows:(c + 1) * rows, :] = (sums[MAX_WINDOW:, :] / count - ext[MAX_WINDOW:, :]).astype(BF16)

    spec = pl.BlockSpec((None, seq, LANE), lambda b, j: (b, 0, j))
    return pl.pallas_call(
        body, name=name, grid=(nb, tokw // LANE), in_specs=[spec], out_specs=spec,
        out_shape=jax.ShapeDtypeStruct((nb, seq, tokw), BF16), compiler_params=_params(),
    )(proj3)


def _pool_bwd(dp3, *, name):
    nb, seq, tokw = dp3.shape
    rows = _chunk_rows(seq)
    group = tokw // len(POOL_WINDOWS)

    def body(d_ref, o_ref):
        col0 = pl.program_id(1) * LANE
        for c in range(seq // rows):
            ext = _load_ext(d_ref, c, rows, 0, MAX_WINDOW, seq)
            lane = col0 + lax.broadcasted_iota(jnp.int32, (1, LANE), 1)
            win = jnp.left_shift(2, jnp.minimum(lane // group, len(POOL_WINDOWS) - 1))
            t = c * rows + lax.broadcasted_iota(jnp.int32, (rows + MAX_WINDOW, 1), 0)
            scaled = ext / jnp.minimum(t + 1, win).astype(F32)
            sums, _ = _window_sums(scaled, _up, col0, group)
            o_ref[c * rows:(c + 1) * rows, :] = sums[:rows, :] - ext[:rows, :]

    spec = pl.BlockSpec((None, seq, LANE), lambda b, j: (b, 0, j))
    return pl.pallas_call(
        body, name=name, grid=(nb, tokw // LANE), in_specs=[spec], out_specs=spec,
        out_shape=jax.ShapeDtypeStruct((nb, seq, tokw), F32), compiler_params=_params(),
    )(dp3)


def _conv3(ext, w_ref, b_ref):
    x1, x2 = _down(ext, 1), _down(ext, 2)
    return w_ref[0:1, :] * x2 + w_ref[1:2, :] * x1 + w_ref[2:3, :] * ext + b_ref[...], x1, x2


def _convgate_fwd(up3, cw, cb, *, name):
    nb, seq, c2 = up3.shape
    fp = c2 // 2
    nblk = fp // LANE
    rows = _chunk_rows(seq)

    def body(u_ref, g_ref, wu_ref, wg_ref, bu_ref, bg_ref, o_ref):
        for c in range(seq // rows):
            hu, _, _ = _conv3(_load_ext(u_ref, c, rows, SUBLANE, 0, seq), wu_ref, bu_ref)
            hg, _, _ = _conv3(_load_ext(g_ref, c, rows, SUBLANE, 0, seq), wg_ref, bg_ref)
            o_ref[c * rows:(c + 1) * rows, :] = (hg * jax.nn.sigmoid(hg) * hu)[SUBLANE:, :].astype(BF16)

    def col(off, r):
        return pl.BlockSpec((r, LANE), lambda b, j: (0, j + off))

    def act(off):
        return pl.BlockSpec((None, seq, LANE), lambda b, j: (b, 0, j + off))

    return pl.pallas_call(
        body, name=name, grid=(nb, nblk),
        in_specs=[act(0), act(nblk), col(0, SUBLANE), col(nblk, SUBLANE), col(0, 1), col(nblk, 1)],
        out_specs=act(0), out_shape=jax.ShapeDtypeStruct((nb, seq, fp), BF16), compiler_params=_params(),
    )(up3, up3, cw, cw, cb, cb)


def _convgate_bwd(up3, dact3, cw, cb, *, name):
    nb, seq, c2 = up3.shape
    fp = c2 // 2
    nblk = fp // LANE
    rows = _chunk_rows(seq)
    h = SUBLANE

    def body(u_ref, g_ref, da_ref, wu_ref, wg_ref, bu_ref, bg_ref, du_ref, dg_ref, dwu_ref, dwg_ref, dbu_ref,
             dbg_ref):
        @pl.when(pl.program_id(1) == 0)
        def _():
            for r in (dwu_ref, dwg_ref, dbu_ref, dbg_ref):
                r[...] = jnp.zeros(r.shape, F32)

        for c in range(seq // rows):
            eu = _load_ext(u_ref, c, rows, h, h, seq)
            eg = _load_ext(g_ref, c, rows, h, h, seq)
            da = _load_ext(da_ref, c, rows, h, h, seq)
            hu, u1, u2 = _conv3(eu, wu_ref, bu_ref)
            hg, g1, g2 = _conv3(eg, wg_ref, bg_ref)
            sig = jax.nn.sigmoid(hg)
            dhu = da * hg * sig
            dhg = da * hu * sig * (1.0 + hg * (1.0 - sig))
            for dh, w_ref, x0, x1, x2, dx_ref, dw_ref, db_ref in (
                    (dhu, wu_ref, eu, u1, u2, du_ref, dwu_ref, dbu_ref),
                    (dhg, wg_ref, eg, g1, g2, dg_ref, dwg_ref, dbg_ref)):
                dx = w_ref[2:3, :] * dh + w_ref[1:2, :] * _up(dh, 1) + w_ref[0:1, :] * _up(dh, 2)
                dx_ref[c * rows:(c + 1) * rows, :] = dx[h:h + rows, :].astype(BF16)
                core = dh[h:h + rows, :]
                for k, xk in ((0, x2), (1, x1), (2, x0)):
                    dw_ref[k:k + 1, :] += jnp.sum(core * xk[h:h + rows, :], axis=0, keepdims=True)
                db_ref[...] += jnp.sum(core, axis=0, keepdims=True)

    def col(off, r):
        return pl.BlockSpec((r, LANE), lambda j, b: (0, j + off))

    def act(off):
        return pl.BlockSpec((None, seq, LANE), lambda j, b: (b, 0, j + off))

    du, dg, dwu, dwg, dbu, dbg = pl.pallas_call(
        body, name=name, grid=(nblk, nb),
        in_specs=[act(0), act(nblk), act(0), col(0, SUBLANE), col(nblk, SUBLANE), col(0, 1), col(nblk, 1)],
        out_specs=[act(0), act(0), col(0, SUBLANE), col(0, SUBLANE), col(0, 1), col(0, 1)],
        out_shape=[jax.ShapeDtypeStruct((nb, seq, fp), BF16), jax.ShapeDtypeStruct((nb, seq, fp), BF16),
                   jax.ShapeDtypeStruct((SUBLANE, fp), F32), jax.ShapeDtypeStruct((SUBLANE, fp), F32),
                   jax.ShapeDtypeStruct((1, fp), F32), jax.ShapeDtypeStruct((1, fp), F32)],
        compiler_params=_params(),
    )(up3, up3, dact3, cw, cw, cb, cb)
    return du, dg, jnp.concatenate([dwu, dwg], axis=1), jnp.concatenate([dbu, dbg], axis=1)


def _scan_rows(x, shift, valid):
    row = lax.broadcasted_iota(jnp.int32, (x.shape[0], 1), 0)
    k = 1
    while k < x.shape[0]:
        x = x + jnp.where(valid(row, k), shift(x, k), 0.0)
        k *= 2
    return x


def _pick_row(x, r):
    row = lax.broadcasted_iota(jnp.int32, (x.shape[0], 1), 0)
    return jnp.sum(jnp.where(row == r, x, 0.0), axis=0, keepdims=True)


def _log_sigmoid(z):
    return jnp.minimum(z, 0.0) - jnp.log(1.0 + jnp.exp(-jnp.abs(z)))


def _gate_fwd(kvf3, fb, col_block, *, name):
    nb, seq, _ = kvf3.shape
    rows = _chunk_rows(seq)

    def body(f_ref, fb_ref, o_ref):
        carry = jnp.zeros((1, LANE), F32)
        for c in range(seq // rows):
            logf = _log_sigmoid(f_ref[c * rows:(c + 1) * rows, :] + fb_ref[...])
            run = _scan_rows(logf, _down, lambda row, k: row >= k) + carry
            o_ref[c * rows:(c + 1) * rows, :] = run
            carry = _pick_row(run, rows - 1)

    return pl.pallas_call(
        body, name=name, grid=(nb,),
        in_specs=[pl.BlockSpec((None, seq, LANE), lambda b: (b, 0, col_block)),
                  pl.BlockSpec((1, LANE), lambda b: (0, 0))],
        out_specs=pl.BlockSpec((None, seq, LANE), lambda b: (b, 0, 0)),
        out_shape=jax.ShapeDtypeStruct((nb, seq, LANE), F32), compiler_params=_params(),
    )(kvf3, fb)


def _gate_bwd(kvf3, fb, dF3, col_block, heads, *, name):
    nb, seq, _ = kvf3.shape
    rows = _chunk_rows(seq)

    def body(f_ref, fb_ref, d_ref, o_ref, dfb_ref):
        @pl.when(pl.program_id(0) == 0)
        def _():
            dfb_ref[...] = jnp.zeros(dfb_ref.shape, F32)

        lane = lax.broadcasted_iota(jnp.int32, (1, LANE), 1)
        carry = jnp.zeros((1, LANE), F32)
        for c in reversed(range(seq // rows)):
            run = _scan_rows(d_ref[c * rows:(c + 1) * rows, :], _up, lambda row, k: row < rows - k) + carry
            carry = _pick_row(run, 0)
            z = f_ref[c * rows:(c + 1) * rows, :] + fb_ref[...]
            df = jnp.where(lane < heads, run * jax.nn.sigmoid(-z), 0.0)
            o_ref[c * rows:(c + 1) * rows, :] = df
            dfb_ref[...] += jnp.sum(df, axis=0, keepdims=True)

    return pl.pallas_call(
        body, name=name, grid=(nb,),
        in_specs=[pl.BlockSpec((None, seq, LANE), lambda b: (b, 0, col_block)),
                  pl.BlockSpec((1, LANE), lambda b: (0, 0)),
                  pl.BlockSpec((None, seq, LANE), lambda b: (b, 0, 0))],
        out_specs=[pl.BlockSpec((None, seq, LANE), lambda b: (b, 0, 0)), pl.BlockSpec((1, LANE), lambda b: (0, 0))],
        out_shape=[jax.ShapeDtypeStruct((nb, seq, LANE), F32), jax.ShapeDtypeStruct((1, LANE), F32)],
        compiler_params=_params(),
    )(kvf3, fb, dF3)


def _head_masks():
    lane = lax.broadcasted_iota(jnp.int32, (1, LANE), 1)
    return (lane < HEAD_DIM, lane >= HEAD_DIM)


BIAS_TERMS = 3


def _bias_lanes(gsum):
    nb, seq, heads = gsum.shape
    terms, rest = [], gsum
    for _ in range(BIAS_TERMS):
        t = lax.reduce_precision(rest, exponent_bits=8, mantissa_bits=7)
        terms.append(t)
        rest = rest - t
    ones = [jnp.ones_like(gsum)] * BIAS_TERMS

    def lanes(parts):
        z = jnp.stack(parts, axis=-1)
        z = jnp.pad(z, ((0, 0), (0, 0), (0, 0), (0, HEAD_DIM - 2 * BIAS_TERMS)))
        z = z.reshape(nb, seq, heads // 2, 2, HEAD_DIM)[:, :, :, ::-1]
        return z.reshape(nb, seq, heads * HEAD_DIM).astype(BF16)

    return lanes(terms + ones), lanes(ones + [-t for t in terms])


def _fox_scores(q, k, aq, ak, masked):
    qs = (q * ATT_SCALE).astype(BF16)
    qts = [jnp.where(msk, qs, aq) for msk in _head_masks()]
    ss = [_bdot(qt, jnp.where(msk, k, ak), "nt") for qt, msk in zip(qts, _head_masks())]
    if masked:
        t = q.shape[0]
        keep = lax.broadcasted_iota(jnp.int32, (t, t), 0) >= lax.broadcasted_iota(jnp.int32, (t, t), 1)
        ss = [jnp.where(keep, s, NEG_BIG) for s in ss]
    return ss, qts


def _on_blocks(qi, ki, step):
    @pl.when(ki < qi)
    def _():
        step(False)

    @pl.when(ki == qi)
    def _():
        step(True)


def _fox_grid(nblk, tokw, t, q_major):
    if q_major:
        pairs = [(qi, ki) for qi in range(nblk) for ki in range(qi + 1)]
    else:
        pairs = [(qi, ki) for ki in range(nblk) for qi in range(ki, nblk)]
    tables = [jnp.array([p[i] for p in pairs], jnp.int32) for i in (0, 1)]

    def q_spec(off=0, wide=False):
        width = 2 * LANE if wide else LANE
        return pl.BlockSpec((None, t, width), lambda b, p, i, qt, kt: (b, qt[i], p + off))

    def k_spec(off=0):
        return pl.BlockSpec((None, t, LANE), lambda b, p, i, qt, kt: (b, kt[i], p + off))

    return tables, len(pairs), q_spec, k_spec, tokw // LANE


def _lanes(col):
    return jnp.broadcast_to(col, (col.shape[0], LANE))


def _across(stat, width):
    return jnp.tile(stat, (1, width // LANE))


def _fox_fwd(proj3, kvf3, aq3, ak3, tokw, *, name):
    nb, seq, _ = proj3.shape
    t = _pick(seq, 512, LANE)
    tables, n_pairs, q_spec, k_spec, hp0 = _fox_grid(seq // t, tokw, t, True)

    def body(qt_ref, kt_ref, q_ref, k_ref, v_ref, aq_ref, ak_ref, o_ref, lse_ref, m_s, l_s, acc_s):
        i = pl.program_id(2)
        qi, ki = qt_ref[i], kt_ref[i]

        @pl.when(ki == 0)
        def _():
            m_s[...] = jnp.full(m_s.shape, NEG_BIG, F32)
            l_s[...] = jnp.zeros(l_s.shape, F32)
            acc_s[...] = jnp.zeros(acc_s.shape, F32)

        def step(masked):
            v = v_ref[...].astype(BF16)
            ss, _ = _fox_scores(q_ref[...], k_ref[...].astype(BF16), aq_ref[...], ak_ref[...], masked)
            for h, s in enumerate(ss):
                m_old = m_s[h]
                m_new = jnp.maximum(m_old, _lanes(jnp.max(s, axis=-1, keepdims=True)))
                alpha = jnp.exp(m_old - m_new)
                p = jnp.exp(s - _across(m_new, t))
                l_s[h] = alpha * l_s[h] + _lanes(jnp.sum(p, axis=-1, keepdims=True))
                acc_s[h] = alpha * acc_s[h] + _bdot(p, v, "nn")
                m_s[h] = m_new

        _on_blocks(qi, ki, step)

        @pl.when(ki == qi)
        def _():
            o_ref[...] = jnp.where(_head_masks()[0], acc_s[0] / l_s[0], acc_s[1] / l_s[1])
            lse_ref[...] = jnp.concatenate([m_s[0] + jnp.log(l_s[0]), m_s[1] + jnp.log(l_s[1])], axis=1)

    stat = pltpu.VMEM((2, t, LANE), F32)
    return pl.pallas_call(
        body, name=name,
        grid_spec=pltpu.PrefetchScalarGridSpec(
            num_scalar_prefetch=2, grid=(nb, hp0, n_pairs),
            in_specs=[q_spec(), k_spec(), k_spec(hp0), q_spec(), k_spec()],
            out_specs=[q_spec(), q_spec(wide=True)], scratch_shapes=[stat, stat, stat]),
        out_shape=[jax.ShapeDtypeStruct((nb, seq, tokw), F32), jax.ShapeDtypeStruct((nb, seq, 2 * tokw), F32)],
        compiler_params=_params(),
    )(*tables, proj3, kvf3, kvf3, aq3, ak3)


def _fox_bwd_common(q_ref, k_ref, v_ref, aq_ref, ak_ref, do_ref, lse_ref, delta_ref, masked):
    k, v = k_ref[...].astype(BF16), v_ref[...].astype(BF16)
    ss, qts = _fox_scores(q_ref[...], k, aq_ref[...], ak_ref[...], masked)
    do = do_ref[...]
    t = do.shape[0]
    out = []
    for h, (s, qt, msk) in enumerate(zip(ss, qts, _head_masks())):
        doh = jnp.where(msk, do, 0.0).astype(BF16)
        p = jnp.exp(s - _across(lse_ref[:, h * LANE:(h + 1) * LANE], t))
        ds = p * (_bdot(doh, v, "nt") - _across(delta_ref[:, h * LANE:(h + 1) * LANE], t))
        out.append((qt, doh, p, ds))
    return out, k


def _fox_bwd(proj3, kvf3, aq3, ak3, o3, dmixin3, lse3, tokw, *, name):
    nb, seq, _ = proj3.shape
    t = _pick(seq, 512, LANE)
    nblk = seq // t
    tables, n_pairs, q_spec, k_spec, hp0 = _fox_grid(nblk, tokw, t, True)
    whole = pl.BlockSpec((None, seq, LANE), lambda b, p, i, qt, kt: (b, 0, p))
    dfk_spec = pl.BlockSpec((None, None, nblk, SUBLANE, t), lambda b, p, i, qt, kt: (b, p, 0, 0, 0))

    def body(qt_ref, kt_ref, q_ref, k_ref, v_ref, aq_ref, ak_ref, o_ref, do_ref, lse_ref, dq_ref, dk_ref, dv_ref,
             dfk_ref, acc_s, row_s, delta_s):
        i = pl.program_id(2)
        qi, ki = qt_ref[i], kt_ref[i]

        @pl.when(i == 0)
        def _():
            dk_ref[...] = jnp.zeros(dk_ref.shape, F32)
            dv_ref[...] = jnp.zeros(dv_ref.shape, F32)
            dfk_ref[...] = jnp.zeros(dfk_ref.shape, F32)

        @pl.when(ki == 0)
        def _():
            acc_s[...] = jnp.zeros(acc_s.shape, F32)
            row_s[...] = jnp.zeros(row_s.shape, F32)
            prod = do_ref[...] * o_ref[...]
            delta_s[...] = jnp.concatenate(
                [_lanes(jnp.sum(jnp.where(msk, prod, 0.0), axis=-1, keepdims=True)) for msk in _head_masks()],
                axis=1)

        def step(masked):
            heads, k = _fox_bwd_common(q_ref, k_ref, v_ref, aq_ref, ak_ref, do_ref, lse_ref, delta_s, masked)
            rows = pl.ds(pl.multiple_of(ki * t, t), t)
            for h, ((qt, doh, p, ds), msk) in enumerate(zip(heads, _head_masks())):
                acc_s[h] += _bdot(ds, k, "nn")
                row_s[h] += _lanes(jnp.sum(ds, axis=-1, keepdims=True))
                dv_ref[rows, :] += _bdot(p, doh, "tn")
                dk_ref[rows, :] += jnp.where(msk, _bdot(ds, qt, "tn"), 0.0)
                dfk_ref[ki, h:h + 1, :] -= jnp.sum(ds, axis=0, keepdims=True)

        _on_blocks(qi, ki, step)

        @pl.when(ki == qi)
        def _():
            dq_ref[...] = jnp.where(_head_masks()[0], acc_s[0], acc_s[1]) * ATT_SCALE
            for h in range(2):
                dfk_ref[qi, 2 + h:3 + h, :] = row_s[h].T[0:1, :]

    out = jax.ShapeDtypeStruct((nb, seq, tokw), F32)
    stat = pltpu.VMEM((2, t, LANE), F32)
    return pl.pallas_call(
        body, name=name,
        grid_spec=pltpu.PrefetchScalarGridSpec(
            num_scalar_prefetch=2, grid=(nb, hp0, n_pairs),
            in_specs=[q_spec(), k_spec(), k_spec(hp0), q_spec(), k_spec(), q_spec(), q_spec(), q_spec(wide=True)],
            out_specs=[q_spec(), whole, whole, dfk_spec],
            scratch_shapes=[stat, stat, pltpu.VMEM((t, 2 * LANE), F32)]),
        out_shape=[out, out, out, jax.ShapeDtypeStruct((nb, hp0, nblk, SUBLANE, t), F32)],
        compiler_params=_params(),
    )(*tables, proj3, kvf3, kvf3, aq3, ak3, o3, dmixin3, lse3)


def _peer(k):
    x, y, c = lax.axis_index("x"), lax.axis_index("y"), lax.axis_index("c")
    return (1 - x if k & 4 else x, 1 - y if k & 2 else y, 1 - c if k & 1 else c)


def _dev_index(p):
    return 4 * p[0] + 2 * p[1] + p[2]


_HBM = pl.BlockSpec(memory_space=pltpu.HBM)
CHIP_RELATIONS = (2, 4, 6)


def _chip_index(p):
    return 2 * p[0] + p[1]


def _run_copies(sends, recvs):
    for cp in sends:
        cp.start()
    for cp in recvs:
        cp.wait_recv()
    for cp in sends:
        cp.wait_send()


def _gather_shards(whole, halved, side_by_side, *, name):
    nw, nh = len(whole), len(halved) + len(side_by_side)
    n = nw + nh
    n_sem = 3 * nw + 6 * nh

    def body(*refs):
        ins, outs, send_sems, recv_sems, local_sems = refs[:n], refs[n:2 * n], refs[2 * n], refs[2 * n + 1], refs[-1]
        me, sib = _peer(0), _peer(1)
        q, c = _chip_index(me), me[2]
        own = []

        def copy(src, dst, s, to):
            return pltpu.make_async_remote_copy(src_ref=src, dst_ref=dst, send_sem=send_sems.at[s],
                                                recv_sem=recv_sems.at[s], device_id=to, device_id_type=MESH_T)

        sends, recvs, passes = [], [], []
        for j, k in enumerate(CHIP_RELATIONS):
            peer = _peer(k)
            pq = _chip_index(peer)
            for i in range(nw):
                sends.append(copy(ins[i], outs[i].at[q], 3 * i + j, peer))
                recvs.append(copy(ins[i], outs[i].at[pq], 3 * i + j, peer))
            for i in range(nh):
                src, out, s = ins[nw + i], outs[nw + i], 3 * nw + 6 * i + j
                if i < len(halved):
                    place = lambda chip, half, out=out: out.at[chip, half]
                else:
                    cols = src.shape[-1]
                    place = lambda chip, half, out=out, cols=cols: out.at[
                        half, :, pl.ds(pl.multiple_of(chip * cols, LANE), cols)]
                    if j == 0:
                        own += [pltpu.make_async_copy(src.at[h], place(q, h), local_sems.at[len(own) + h])
                                for h in range(2)]
                sends.append(copy(src.at[c], place(q, c), s, peer))
                passes.append((copy(src.at[c], place(pq, c), s, peer), copy(place(pq, c), place(pq, c), s + 3, sib),
                               copy(place(pq, c), place(pq, 1 - c), s + 3, sib)))
        for cp in sends + own:
            cp.start()
        for arrival, hand_over, _ in passes:
            arrival.wait_recv()
            hand_over.start()
        for cp in recvs:
            cp.wait_recv()
        for _, _, from_sibling in passes:
            from_sibling.wait_recv()
        for cp in sends + [hand_over for _, hand_over, _ in passes]:
            cp.wait_send()
        for cp in own:
            cp.wait()

    arrays = list(whole) + list(halved) + list(side_by_side)
    return pl.pallas_call(
        body, name=name, in_specs=[_HBM] * n, out_specs=[_HBM] * n,
        out_shape=[jax.ShapeDtypeStruct((N_CHIP,) + a.shape, a.dtype) for a in list(whole) + list(halved)]
        + [jax.ShapeDtypeStruct(a.shape[:-1] + (N_CHIP * a.shape[-1],), a.dtype) for a in side_by_side],
        scratch_shapes=[pltpu.SemaphoreType.DMA((n_sem,)), pltpu.SemaphoreType.DMA((n_sem,)),
                        pltpu.SemaphoreType.DMA((2 * len(side_by_side),))],
    )(*arrays)


def _to_sibling(grads, *, name):
    n = len(grads)

    def body(*refs):
        ins, outs, send_sems, recv_sems = refs[:n], refs[n:2 * n], refs[2 * n], refs[2 * n + 1]
        c = lax.axis_index("c")
        sends = [pltpu.make_async_remote_copy(src_ref=ins[i].at[:, 1 - c], dst_ref=outs[i], send_sem=send_sems.at[i],
                                              recv_sem=recv_sems.at[i], device_id=_peer(1), device_id_type=MESH_T)
                 for i in range(n)]
        _run_copies(sends, sends)

    return pl.pallas_call(
        body, name=name, in_specs=[_HBM] * n, out_specs=[_HBM] * n,
        out_shape=[jax.ShapeDtypeStruct(g.shape[:1] + g.shape[2:], g.dtype) for g in grads],
        scratch_shapes=[pltpu.SemaphoreType.DMA((n,)), pltpu.SemaphoreType.DMA((n,))],
    )(*grads)


def _pair_add(grads, from_sibling, qc, *, name):
    _, _, rows, cols = grads.shape
    tile = _pick(rows, 1024, SUBLANE)

    def body(qc_ref, g_ref, s_ref, o_ref):
        del qc_ref
        o_ref[...] = g_ref[...] + s_ref[...]

    spec = pl.BlockSpec((None, tile, cols), lambda j, i, qc: (j, i, 0))
    return pl.pallas_call(
        body, name=name,
        grid_spec=pltpu.PrefetchScalarGridSpec(
            num_scalar_prefetch=1, grid=(N_CHIP, rows // tile),
            in_specs=[pl.BlockSpec((None, None, tile, cols), lambda j, i, qc: (j, qc[1], i, 0)), spec],
            out_specs=spec),
        out_shape=jax.ShapeDtypeStruct((N_CHIP, rows, cols), F32), compiler_params=_params(),
    )(qc, grads, from_sibling)


def _to_chips(sums, small, *, name):
    n = len(sums)

    def body(*refs):
        ins, small_ref, outs, small_out = refs[:n], refs[n], refs[n + 1:2 * n + 1], refs[2 * n + 1]
        send_sems, recv_sems, local_sem = refs[2 * n + 2:]
        me = _peer(0)

        def copy(src, dst, s, to):
            return pltpu.make_async_remote_copy(src_ref=src, dst_ref=dst, send_sem=send_sems.at[s],
                                                recv_sem=recv_sems.at[s], device_id=to, device_id_type=MESH_T)

        mine = pltpu.make_async_copy(small_ref.at[_dev_index(me)], small_out.at[_dev_index(me)], local_sem)
        mine.start()
        sends, recvs = [], []
        for j, k in enumerate(CHIP_RELATIONS):
            peer = _peer(k)
            for i in range(n):
                src = ins[i].at[_chip_index(peer)]
                sends.append(copy(src, outs[i].at[j], 3 * i + j, peer))
                recvs.append(copy(src, outs[i].at[j], 3 * i + j, peer))
        for k in range(1, N_DEV):
            peer = _peer(k)
            src = small_ref.at[_dev_index(peer)]
            sends.append(copy(src, small_out.at[_dev_index(me)], 3 * n + k - 1, peer))
            recvs.append(copy(src, small_out.at[_dev_index(peer)], 3 * n + k - 1, peer))
        _run_copies(sends, recvs)
        mine.wait()

    n_sem = 3 * n + N_DEV - 1
    return pl.pallas_call(
        body, name=name, in_specs=[_HBM] * (n + 1), out_specs=[_HBM] * (n + 1),
        out_shape=[jax.ShapeDtypeStruct((3,) + g.shape[1:], g.dtype) for g in sums]
        + [jax.ShapeDtypeStruct(small.shape, small.dtype)],
        scratch_shapes=[pltpu.SemaphoreType.DMA((n_sem,)), pltpu.SemaphoreType.DMA((n_sem,)),
                        pltpu.SemaphoreType.DMA],
    )(*sums, small)


def _swap_halves(arrays, *, name):
    n = len(arrays)

    def body(*refs):
        outs, send_sems, recv_sems = refs[n:2 * n], refs[2 * n], refs[2 * n + 1]
        c = lax.axis_index("c")
        sib = _peer(1)
        sends, recvs = [], []
        for i in range(n):
            sem = dict(send_sem=send_sems.at[i], recv_sem=recv_sems.at[i], device_id=sib, device_id_type=MESH_T)
            sends.append(pltpu.make_async_remote_copy(src_ref=outs[i].at[c], dst_ref=outs[i].at[c], **sem))
            recvs.append(pltpu.make_async_remote_copy(src_ref=outs[i].at[c], dst_ref=outs[i].at[1 - c], **sem))
        _run_copies(sends, recvs)

    return pl.pallas_call(
        body, name=name, in_specs=[_HBM] * n, out_specs=[_HBM] * n,
        out_shape=[jax.ShapeDtypeStruct(a.shape, a.dtype) for a in arrays],
        input_output_aliases={i: i for i in range(n)},
        scratch_shapes=[pltpu.SemaphoreType.DMA((n,)), pltpu.SemaphoreType.DMA((n,))],
    )(*arrays)


def _adam_math(g, w, m, v):
    bc1 = 1.0 - ADAM_B1 ** ADAM_STEP
    bc2 = 1.0 - ADAM_B2 ** ADAM_STEP
    m_new = ADAM_B1 * m + (1.0 - ADAM_B1) * g
    v_new = ADAM_B2 * v + (1.0 - ADAM_B2) * (g * g)
    delta = -ADAM_LR * ((m_new / bc1) / (jnp.sqrt(v_new / bc2) + ADAM_EPS) + ADAM_WD * w)
    return delta, m_new, v_new


def _adamw_half(sums, parts, w, m, v, qc, *, name):
    _, rows, cols = sums.shape
    tile = _pick(rows, 384, SUBLANE)
    n_parts = parts.shape[0]

    def body(qc_ref, g_ref, p_ref, w_ref, m_ref, v_ref, go_ref, do_ref, mo_ref, vo_ref):
        del qc_ref
        g = g_ref[...]
        for k in range(n_parts):
            g = g + p_ref[k]
        delta, m_new, v_new = _adam_math(g, w_ref[...], m_ref[...], v_ref[...])
        go_ref[...] = g
        do_ref[...] = delta
        mo_ref[...] = m_new
        vo_ref[...] = v_new

    half = pl.BlockSpec((None, tile, cols), lambda i, qc: (qc[1], i, 0))
    shape = jax.ShapeDtypeStruct((2, rows, cols), F32)
    return pl.pallas_call(
        body, name=name,
        grid_spec=pltpu.PrefetchScalarGridSpec(
            num_scalar_prefetch=1, grid=(rows // tile,),
            in_specs=[pl.BlockSpec((None, tile, cols), lambda i, qc: (qc[0], i, 0)),
                      pl.BlockSpec((n_parts, tile, cols), lambda i, qc: (0, i, 0)), half, half, half],
            out_specs=[half] * 4),
        out_shape=[shape] * 4, compiler_params=_params(),
    )(qc, sums, parts, w, m, v)


def _adamw(parts, w, m, v, *, name):
    _, rows, cols = parts.shape
    tile = _pick(rows, 256, SUBLANE)

    def body(p_ref, w_ref, m_ref, v_ref, o_ref):
        g = p_ref[0]
        for i in range(1, N_DEV):
            g = g + p_ref[i]
        delta, m_new, v_new = _adam_math(g, w_ref[...], m_ref[...], v_ref[...])
        o_ref[0] = g
        o_ref[1] = delta
        o_ref[2] = m_new
        o_ref[3] = v_new

    spec = pl.BlockSpec((tile, cols), lambda i: (i, 0))
    return pl.pallas_call(
        body, name=name, grid=(rows // tile,),
        in_specs=[pl.BlockSpec((N_DEV, tile, cols), lambda i: (0, i, 0)), spec, spec, spec],
        out_specs=pl.BlockSpec((4, tile, cols), lambda i: (0, i, 0)),
        out_shape=jax.ShapeDtypeStruct((4, rows, cols), F32), compiler_params=_params(),
    )(parts, w, m, v)


def _layout(shapes, names, align):
    out, off = [], 0
    for n in names:
        size = math.prod(shapes[n])
        out.append((n, tuple(shapes[n]), off, size))
        off += _round_up(size, align)
    return out, off


def _pack(arrays, layout, total, lead=()):
    parts = []
    for i, (n, _, off, size) in enumerate(layout):
        end = layout[i + 1][2] if i + 1 < len(layout) else total
        flat = arrays[n].reshape(lead + (size,))
        if end - off > size:
            flat = jnp.pad(flat, [(0, 0)] * len(lead) + [(0, end - off - size)])
        parts.append(flat)
    return jnp.concatenate(parts, axis=len(lead))


def _unpack(flat, layout, lead=()):
    return {n: flat[..., off:off + size].reshape(lead + shape) for n, shape, off, size in layout}


def _to_shards(full, axis):
    shp = full.shape
    return jnp.moveaxis(full.reshape(shp[:axis] + (N_CHIP, shp[axis] // N_CHIP) + shp[axis + 1:]), axis, 0)


def _from_shards(shards, axis):
    x = jnp.moveaxis(shards, 0, axis)
    shp = x.shape
    return x.reshape(shp[:axis] + (shp[axis] * shp[axis + 1],) + shp[axis + 2:])


def _pad_cols(w, per, padded):
    lead = w.shape[:-1]
    x = w.reshape(lead + (-1, per))
    x = jnp.pad(x, [(0, 0)] * len(lead) + [(0, 0), (0, padded - per)])
    return x.reshape(lead + (-1,))


def _unpad_cols(w, per, padded):
    lead = w.shape[:-1]
    return w.reshape(lead + (-1, padded))[..., :per].reshape(lead + (-1,))


def _local_step(x, mem, target, W):
    nb, seq, d = x.shape
    n = nb * seq
    tokw = d - MEM_WIDTH
    heads = tokw // HEAD_DIM
    mlen = mem.shape[1]
    per = W["ffn_w_down"].shape[1] // 2
    per_p = _round_up(per, LANE)
    fp = 2 * per_p
    kvw = 2 * tokw + heads
    kvp = 2 * tokw + LANE
    gate_block = 2 * tokw // LANE

    x2d = x.reshape(n, d)
    mem2d = mem.reshape(nb * mlen, d)
    t2d = target.reshape(n, d)
    row = lambda a: a.reshape(1, -1)
    ones_tok = jnp.ones((1, tokw), F32)

    pool_bd = jax.scipy.linalg.block_diag(*[W["a_pool_w"][0, i] for i in range(len(POOL_WINDOWS))]).astype(BF16)
    kv_w = jnp.pad(W["kv_w"], ((0, 0), (0, kvp - kvw)))
    fb = jnp.pad(W["f_b"], (0, LANE - heads)).reshape(1, LANE)
    w_up = W.get("ffn_w_up_padded")
    if w_up is None:
        w_up = jnp.stack([_pad_cols(W["ffn_w_up"][l], per, per_p) for l in range(DEPTH)])
    w_down = [jnp.pad(W["ffn_w_down"][l].reshape(2, per, d), ((0, 0), (0, per_p - per), (0, 0))).reshape(fp, d)
              for l in range(DEPTH)]
    conv_w = [jnp.pad(_pad_cols(W["ffn_conv_w"][l], per, per_p), ((0, SUBLANE - CONV_WIDTH), (0, 0)))
              for l in range(DEPTH)]
    conv_b = [_pad_cols(W["ffn_conv_b"][l], per, per_p).reshape(1, 2 * fp) for l in range(DEPTH)]
    w_in = [W["a_w_in"][0], W["b_w_q"][0]]
    w_out = [W["a_w_out"][0], W["b_w_out"][0]]

    saved = []
    cur = x2d
    for l in range(DEPTH):
        s = {"x_in": cur}
        memkv = _mm(mem2d, W["mem_w_kv"][l], "nn", name=f"memkv{l}").reshape(nb, mlen, 2 * MEM_WIDTH)
        if l == 0:
            proj = _mm(cur, w_in[l], "nn", name="proj0")
            pooled = _pool_fwd(proj.reshape(nb, seq, d), tokw, name="pool_fwd").reshape(n, tokw)
            tok = _mm(pooled, pool_bd, "nn", name="pool_mix")
            scale = W["a_pool_scale"].reshape(1, tokw)
            s.update(pooled=pooled, mixed=tok, scale=scale)
        else:
            kvf = _mm(cur, kv_w, "nn", tn=kvp, name="kvf")
            kvf3 = kvf.reshape(nb, seq, kvp)
            gsum = _gate_fwd(kvf3, fb, gate_block, name="gate_fwd")[:, :, :heads]
            aq3, ak3 = _bias_lanes(gsum)
            proj = _mm(cur, w_in[l], "nn", name="proj1")
            o3, lse3 = _fox_fwd(proj.reshape(nb, seq, d), kvf3, aq3, ak3, tokw, name="fox_fwd")
            tok = o3.reshape(n, tokw)
            scale = ones_tok
            s.update(kvf3=kvf3, aq3=aq3, ak3=ak3, o3=o3, lse3=lse3)
        mixin = _memattn_fwd(tok, proj, memkv, scale, seq=seq, name=f"memattn_fwd{l}")
        mix = _mm(mixin, w_out[l], "nn", name=f"mix{l}")
        x1 = _ln_fwd(cur, mix, row(W["ln1_g"][l]), row(W["ln1_b"][l]), name=f"ln1_fwd{l}")
        up = _mm(x1, w_up, "nn", b_lead=l, tn=per_p, col_major=True, name=f"ffn_up{l}")
        act = _convgate_fwd(up.reshape(nb, seq, 2 * fp), conv_w[l], conv_b[l], name=f"convgate_fwd{l}")
        act = act.reshape(n, fp)
        ffn = _mm(act, w_down[l], "nn", tk=fp, name=f"ffn_down{l}")
        s.update(proj=proj, memkv=memkv, mixin=mixin, mix=mix, x1=x1, up=up, act=act, ffn=ffn)
        saved.append(s)
        if l + 1 < DEPTH:
            cur = _ln_fwd(x1, ffn, row(W["ln2_g"][l]), row(W["ln2_b"][l]), name=f"ln2_fwd{l}")

    G = {}
    ln_g = {k: [None] * DEPTH for k in ("ln1_g", "ln1_b", "ln2_g", "ln2_b")}
    stack = {k: [None] * DEPTH for k in ("mem_w_kv", "ffn_w_up", "ffn_conv_w", "ffn_conv_b", "ffn_w_down")}
    dx_terms = None
    loss = None
    for l in reversed(range(DEPTH)):
        s = saved[l]
        g2 = row(W["ln2_g"][l])
        if l == DEPTH - 1:
            dres, dffn, loss, dg, db = _final_ln_loss(s["x1"], s["ffn"], t2d, g2, row(W["ln2_b"][l]),
                                                      name="final_ln_loss")
        else:
            dres, dffn, dg, db = _ln_bwd(s["x1"], s["ffn"], g2, dx_terms, name=f"ln2_bwd{l}")
        ln_g["ln2_g"][l], ln_g["ln2_b"][l] = dg[0], db[0]
        dact = _mm(dffn, w_down[l], "nt", tn=fp, name=f"ffn_down_dx{l}")
        stack["ffn_w_down"][l] = _mm(s["act"], dffn, "tn", tm=per_p, tk=DW_ROWS, trim=("rows", per),
                                     name=f"ffn_down_dw{l}")
        du3, dg3, dcw, dcb = _convgate_bwd(s["up"].reshape(nb, seq, 2 * fp), dact.reshape(nb, seq, fp), conv_w[l],
                                           conv_b[l], name=f"convgate_bwd{l}")
        du, dgt = du3.reshape(n, fp), dg3.reshape(n, fp)
        dx1_u = _mm(du, w_up, "nt", b_lead=l, tk=fp, name=f"ffn_up_dx_u{l}")
        dx1_ffn = _mm(dgt, w_up, "nt", b_lead=l, tk=fp, b_col0=fp, adds=[dx1_u], name=f"ffn_up_dx_g{l}")
        stack["ffn_w_up"][l] = [_mm(s["x1"], part, "tn", tm=d, tn=per_p, tk=DW_ROWS, trim=("cols", per),
                                    name=f"ffn_up_dw_{nm}{l}") for nm, part in (("u", du), ("g", dgt))]
        stack["ffn_conv_w"][l] = dcw[:CONV_WIDTH]
        stack["ffn_conv_b"][l] = dcb[0]
        dres1, dmix, dg, db = _ln_bwd(s["x_in"], s["mix"], row(W["ln1_g"][l]), [dres, dx1_ffn], name=f"ln1_bwd{l}")
        ln_g["ln1_g"][l], ln_g["ln1_b"][l] = dg[0], db[0]
        dmixin = _mm(dmix, w_out[l], "nt", name=f"mix_dx{l}")
        d_w_out = _mm(s["mixin"], dmix, "tn", tm=d, tk=DW_ROWS, name=f"mix_dw{l}")
        if l == 0:
            G["a_w_out"] = d_w_out[None]
            dmixed, dscale = _scale_bwd(dmixin, s["mixed"], s["scale"], name="scale_bwd")
            G["a_pool_scale"] = dscale
            dpooled = _mm(dmixed, pool_bd, "nt", name="pool_mix_dx")
            dpw = _mm(s["pooled"], dmixed, "tn", tm=tokw, tk=DW_ROWS, name="pool_mix_dw")
            grp = tokw // len(POOL_WINDOWS)
            G["a_pool_w"] = jnp.stack([dpw[i * grp:(i + 1) * grp, i * grp:(i + 1) * grp]
                                       for i in range(len(POOL_WINDOWS))])[None]
            dtok = _pool_bwd(dpooled.reshape(nb, seq, tokw), name="pool_bwd").reshape(n, tokw)
            extra = []
        else:
            G["b_w_out"] = d_w_out[None]
            p3 = s["proj"].reshape(nb, seq, d)
            dm3 = dmixin.reshape(nb, seq, d)
            dq3, dk3, dv3, dfk = _fox_bwd(p3, s["kvf3"], s["aq3"], s["ak3"], s["o3"], dm3, s["lse3"], tokw,
                                          name="fox_bwd")
            dtok = dq3.reshape(n, tokw)
            dfk = jnp.swapaxes(dfk[:, :, :, 0:2, :] + dfk[:, :, :, 2:4, :], 2, 3).reshape(nb, heads, seq)
            dgsum = jnp.swapaxes(dfk, 1, 2)
            dgsum = jnp.pad(dgsum, ((0, 0), (0, 0), (0, LANE - heads)))
            df3, dfb = _gate_bwd(s["kvf3"], fb, dgsum, gate_block, heads, name="gate_bwd")
            G["f_b"] = dfb[0, :heads]
            dkvf = [(dk3.reshape(n, tokw), 0, "k"), (dv3.reshape(n, tokw), tokw, "v"),
                    (df3.reshape(n, LANE), 2 * tokw, "f")]
            dx_kv = []
            for part, col0, nm in dkvf:
                dx_kv = [_mm(part, kv_w, "nt", b_col0=col0, adds=dx_kv, name=f"kvf_dx_{nm}")]
            extra = dx_kv
            G["kv_w"] = jnp.concatenate([_mm(s["x_in"], part, "tn", tm=d, tk=DW_ROWS, name=f"kvf_dw_{nm}")
                                         for part, _, nm in dkvf], axis=1)[:, :kvw]
        dproj, dmemkv = _memattn_bwd(dmixin, dtok, s["proj"], s["memkv"], seq=seq, name=f"memattn_bwd{l}")
        stack["mem_w_kv"][l] = _mm(mem2d, dmemkv.reshape(nb * mlen, 2 * MEM_WIDTH), "tn", tm=d, tk=DW_ROWS,
                                   name=f"memkv_dw{l}")
        G["a_w_in" if l == 0 else "b_w_q"] = _mm(s["x_in"], dproj, "tn", tm=d, tk=DW_ROWS, name=f"proj_dw{l}")[None]
        if l == 0:
            grad_x = _mm(dproj, w_in[l], "nt", adds=[dres1], name="proj_dx0")
        else:
            dx_terms = [dres1, _mm(dproj, w_in[l], "nt", name="proj_dx1")] + extra
    for k, v in ln_g.items():
        G[k] = jnp.stack(v)
    G["mem_w_kv"] = jnp.stack(stack["mem_w_kv"])
    G["ffn_w_up"] = jnp.stack([jnp.concatenate(ug, axis=0) for ug in stack["ffn_w_up"]], axis=1)
    G["ffn_conv_w"] = jnp.stack([_unpad_cols(g, per, per_p) for g in stack["ffn_conv_w"]])
    G["ffn_conv_b"] = jnp.stack([_unpad_cols(g, per, per_p) for g in stack["ffn_conv_b"]])
    G["ffn_w_down"] = jnp.stack([g.reshape(N_CHIP, per // 2, d) for g in stack["ffn_w_down"]], axis=1)
    return loss[0, 0], grad_x.reshape(nb, seq, d), G


def kernel(x, mem, a_w_in, a_pool_w, a_pool_scale, a_w_out, b_w_q, b_w_out, kv_w, f_b, mem_w_kv, ln1_g, ln1_b, ln2_g, ln2_b, ffn_w_up, ffn_conv_w, ffn_conv_b, ffn_w_down, loss_target, m_a_w_in, m_a_pool_w, m_a_pool_scale, m_a_w_out, m_b_w_q, m_b_w_out, m_kv_w, m_f_b, m_mem_w_kv, m_ln1_g, m_ln1_b, m_ln2_g, m_ln2_b, m_ffn_w_up, m_ffn_conv_w, m_ffn_conv_b, m_ffn_w_down, v_a_w_in, v_a_pool_w, v_a_pool_scale, v_a_w_out, v_b_w_q, v_b_w_out, v_kv_w, v_f_b, v_mem_w_kv, v_ln1_g, v_ln1_b, v_ln2_g, v_ln2_b, v_ffn_w_up, v_ffn_conv_w, v_ffn_conv_b, v_ffn_w_down):
    w_loc = dict(a_w_in=a_w_in, a_pool_w=a_pool_w, a_pool_scale=a_pool_scale, a_w_out=a_w_out, b_w_q=b_w_q,
                 b_w_out=b_w_out, kv_w=kv_w, f_b=f_b, mem_w_kv=mem_w_kv, ln1_g=ln1_g, ln1_b=ln1_b, ln2_g=ln2_g,
                 ln2_b=ln2_b, ffn_w_up=ffn_w_up, ffn_conv_w=ffn_conv_w, ffn_conv_b=ffn_conv_b, ffn_w_down=ffn_w_down)
    m_loc = dict(a_w_in=m_a_w_in, a_pool_w=m_a_pool_w, a_pool_scale=m_a_pool_scale, a_w_out=m_a_w_out,
                 b_w_q=m_b_w_q, b_w_out=m_b_w_out, kv_w=m_kv_w, f_b=m_f_b, mem_w_kv=m_mem_w_kv, ln1_g=m_ln1_g,
                 ln1_b=m_ln1_b, ln2_g=m_ln2_g, ln2_b=m_ln2_b, ffn_w_up=m_ffn_w_up, ffn_conv_w=m_ffn_conv_w,
                 ffn_conv_b=m_ffn_conv_b, ffn_w_down=m_ffn_w_down)
    v_loc = dict(a_w_in=v_a_w_in, a_pool_w=v_a_pool_w, a_pool_scale=v_a_pool_scale, a_w_out=v_a_w_out,
                 b_w_q=v_b_w_q, b_w_out=v_b_w_out, kv_w=v_kv_w, f_b=v_f_b, mem_w_kv=v_mem_w_kv, ln1_g=v_ln1_g,
                 ln1_b=v_ln1_b, ln2_g=v_ln2_g, ln2_b=v_ln2_b, ffn_w_up=v_ffn_w_up, ffn_conv_w=v_ffn_conv_w,
                 ffn_conv_b=v_ffn_conv_b, ffn_w_down=v_ffn_w_down)
    x_i, y_i, c = lax.axis_index("x"), lax.axis_index("y"), lax.axis_index("c")
    q = 2 * x_i + y_i
    qc = jnp.stack([q, c]).astype(jnp.int32)
    shapes = {k: v.shape for k, v in w_loc.items()}

    def halves(a):
        if a.ndim == 3 and a.shape[0] == 2:
            return a
        rows = math.prod(a.shape[:-1])
        return a.reshape(2, rows // 2, a.shape[-1])

    own = {k: (w_loc[k] if k in GATHER_F32 else w_loc[k].astype(BF16)) for k in SHARDED}
    by_chip = ("a_pool_scale",) + tuple(k for k in BIG if k != "ffn_w_up")
    per = shapes["ffn_w_up"][-1]
    up_own = jnp.pad(own["ffn_w_up"], ((0, 0), (0, 0), (0, _round_up(per, LANE) - per)))
    gathered = _gather_shards([own["a_pool_scale"]], [halves(own[k]) for k in by_chip[1:]], [up_own],
                              name="gather_weights")
    W = {k: _from_shards(lax.dynamic_update_slice_in_dim(g.reshape((N_CHIP,) + shapes[k]), own[k][None], q, axis=0),
                         SHARD_AXIS[k])
         for k, g in zip(by_chip, gathered)}
    W["ffn_w_up_padded"] = gathered[-1]
    for k in REPLICATED:
        W[k] = w_loc[k]

    loss_part, grad_x, G = _local_step(x, mem, loss_target, W)
    loss = lax.psum(loss_part, ("x", "y", "c"))

    g_chip = [G[k] if k in GRADS_BY_CHIP else _to_shards(G[k], SHARD_AXIS[k]) for k in BIG]
    g_chip = [g.reshape((N_CHIP,) + halves(w_loc[k]).shape) for k, g in zip(BIG, g_chip)]
    lay_r, tot_r = _layout(shapes, REPLICATED, PACK_COLS)
    rep_rows = tot_r // PACK_COLS
    rows = _round_up(rep_rows + 1, SUBLANE)
    scale_w = shapes["a_pool_scale"][-1]

    def small(rep, scale_row):
        lead = scale_row.shape[:-2]
        rep = jnp.broadcast_to(_pack(rep, lay_r, tot_r).reshape(rep_rows, PACK_COLS), lead + (rep_rows, PACK_COLS))
        pad = [(0, 0)] * len(lead)
        return jnp.concatenate([rep, jnp.pad(scale_row, pad + [(0, rows - rep_rows - 1), (0, PACK_COLS - scale_w)])],
                               axis=-2)

    g_scale = jnp.repeat(_to_shards(G["a_pool_scale"], 1), 2, axis=0)

    from_sib = _to_sibling(g_chip, name="grads_to_sibling")
    sums = [_pair_add(g, s, qc, name=f"pair_add_{k}") for k, g, s in zip(BIG, g_chip, from_sib)]
    *parts, sm_parts = _to_chips(sums, small(G, g_scale), name="scatter_grads")
    res = []
    for k, g, p in zip(BIG, sums, parts):
        res.extend(_adamw_half(g, p, halves(w_loc[k]), halves(m_loc[k]), halves(v_loc[k]), qc, name=f"adamw_{k}"))
    res = _swap_halves(res, name="swap_halves")
    out = {k: [r.reshape(shapes[k]) for r in res[4 * i:4 * i + 4]] for i, k in enumerate(BIG)}
    sm = _adamw(sm_parts, *[small(d, d["a_pool_scale"]) for d in (w_loc, m_loc, v_loc)], name="adamw_small")
    out_r = _unpack(sm[:, :rep_rows].reshape(4, tot_r), lay_r, lead=(4,))
    for k in REPLICATED:
        out[k] = [out_r[k][a] for a in range(4)]
    out["a_pool_scale"] = [sm[a, rep_rows:rep_rows + 1, :scale_w] for a in range(4)]

    outs = [loss, grad_x]
    for a in range(4):
        for k in WEIGHTS:
            outs.append(out[k][a])
    return tuple(outs)
```

```python
import functools
import math

import jax
import jax.numpy as jnp
from jax import lax
from jax.experimental import pallas as pl
from jax.experimental.pallas import tpu as pltpu

F32 = jnp.float32
BF16 = jnp.bfloat16

HEAD_DIM = 64
MEM_HEADS = 4
MEM_WIDTH = MEM_HEADS * HEAD_DIM
POOL_WINDOWS = (2, 4, 8, 16)
MAX_WINDOW = 16
CONV_WIDTH = 3
DEPTH = 2
DN_ALPHA = (2.0 * DEPTH) ** 0.25
LN_EPS = 1e-5
ATT_SCALE = HEAD_DIM ** -0.5
NEG_BIG = -1e30

ADAM_LR = 0.001
ADAM_B1 = 0.9
ADAM_B2 = 0.999
ADAM_EPS = 1e-08
ADAM_WD = 0.01
ADAM_STEP = 10

LANE = 128
SUBLANE = 8
PACK_COLS = 1024
DW_ROWS = 1024
VMEM_LIMIT = 56 * 1024 * 1024
N_DEV = 8
N_CHIP = 4
MESH_T = pl.DeviceIdType.MESH

SHARDED = ("a_w_in", "a_pool_scale", "a_w_out", "b_w_q", "b_w_out", "kv_w", "mem_w_kv", "ffn_w_up",
           "ffn_conv_w", "ffn_w_down")
SHARD_AXIS = {"a_w_in": 1, "a_pool_scale": 1, "a_w_out": 1, "b_w_q": 1, "b_w_out": 1, "kv_w": 1, "mem_w_kv": 1,
              "ffn_w_up": 2, "ffn_conv_w": 2, "ffn_w_down": 1}
GATHER_F32 = ("a_pool_scale", "ffn_conv_w")
BIG = tuple(k for k in SHARDED if k != "a_pool_scale")
GRADS_BY_CHIP = ("ffn_w_up", "ffn_w_down")
REPLICATED = ("a_pool_w", "f_b", "ln1_g", "ln1_b", "ln2_g", "ln2_b", "ffn_conv_b")
WEIGHTS = ("a_w_in", "a_pool_w", "a_pool_scale", "a_w_out", "b_w_q", "b_w_out", "kv_w", "f_b", "mem_w_kv",
           "ln1_g", "ln1_b", "ln2_g", "ln2_b", "ffn_w_up", "ffn_conv_w", "ffn_conv_b", "ffn_w_down")


def _round_up(n, m):
    return -(-n // m) * m


def _pick(dim, pref, unit=LANE):
    if dim <= pref:
        return dim
    t = (pref // unit) * unit
    while t >= unit:
        if dim % t == 0:
            return t
        t -= unit
    raise ValueError(f"no tile for {dim} <= {pref}")


def _params():
    return pltpu.CompilerParams(vmem_limit_bytes=VMEM_LIMIT)


_DIMS = {"nn": ((1,), (0,)), "nt": ((1,), (1,)), "tn": ((0,), (0,))}


def _bdot(a, b, mode):
    return lax.dot_general(a.astype(BF16), b.astype(BF16), (_DIMS[mode], ((), ())), preferred_element_type=F32)


def _mm(a, b, mode, *, name, tm=512, tn=1024, tk=2048, adds=(), b_col0=0, trim=None, col_major=False, b_lead=None):
    b_shape = b.shape if b_lead is None else b.shape[1:]
    if mode == "nn":
        (M, K), (K2, N) = a.shape, b_shape
    elif mode == "nt":
        (M, K), (N, K2) = a.shape, b_shape
        K2 = K if b_col0 + K <= K2 else -1
    else:
        (K, M), (K2, N) = a.shape, b_shape
    assert K == K2 and (mode == "nt" or b_col0 == 0), (name, a.shape, b.shape)
    tm, tn = _pick(M, tm, SUBLANE if mode != "tn" else LANE), _pick(N, tn)
    tk = _pick(K, tk, LANE if mode != "tn" else SUBLANE)
    nk = K // tk
    assert b_col0 % tk == 0, (name, b_col0, tk)
    koff = b_col0 // tk
    n_add = len(adds)

    def body(*refs):
        a_ref, b_ref = refs[0], refs[1]
        add_refs = refs[2:2 + n_add]
        o_ref, acc_ref = refs[2 + n_add], refs[3 + n_add]
        part = _bdot(a_ref[...], b_ref[...], mode)

        def finish(r):
            for ar in add_refs:
                r = r + ar[...]
            if trim is not None:
                r = r[:, :trim[1]] if trim[0] == "cols" else r[:trim[1], :]
            o_ref[...] = r

        if nk == 1:
            finish(part)
        else:
            k = pl.program_id(2)

            @pl.when(k == 0)
            def _():
                acc_ref[...] = part

            @pl.when(k > 0)
            def _():
                acc_ref[...] += part

            @pl.when(k == nk - 1)
            def _():
                finish(acc_ref[...])

    def spec(block, index):
        if col_major:
            return pl.BlockSpec(block, lambda j, i, k: index(i, j, k))
        return pl.BlockSpec(block, index)

    def b_spec_of(block, index):
        if b_lead is None:
            return spec(block, index)
        return spec((None,) + block, lambda i, j, k: (b_lead,) + index(i, j, k))

    if mode == "nn":
        a_spec = spec((tm, tk), lambda i, j, k: (i, k))
        b_spec = b_spec_of((tk, tn), lambda i, j, k: (k, j))
    elif mode == "nt":
        a_spec = spec((tm, tk), lambda i, j, k: (i, k))
        b_spec = b_spec_of((tn, tk), lambda i, j, k: (j, k + koff))
    else:
        a_spec = spec((tk, tm), lambda i, j, k: (k, i))
        b_spec = b_spec_of((tk, tn), lambda i, j, k: (k, j))
    o_spec = spec((tm, tn), lambda i, j, k: (i, j))
    out_spec, out_shape = o_spec, (M, N)
    if trim is not None and trim[0] == "cols":
        out_spec, out_shape = spec((None, tm, trim[1]), lambda i, j, k: (j, i, 0)), (N // tn, M, trim[1])
    elif trim is not None:
        out_spec, out_shape = spec((None, trim[1], tn), lambda i, j, k: (i, 0, j)), (M // tm, trim[1], N)
    acc_shape = (tm, tn) if nk > 1 else (SUBLANE, LANE)
    return pl.pallas_call(
        body, name=name, grid=(N // tn, M // tm, nk) if col_major else (M // tm, N // tn, nk),
        in_specs=[a_spec, b_spec] + [o_spec] * n_add, out_specs=out_spec,
        out_shape=jax.ShapeDtypeStruct(out_shape, F32),
        scratch_shapes=[pltpu.VMEM(acc_shape, F32)],
        compiler_params=_params(),
    )(a, b, *adds)


def _rowwise(fn, tiled, full, outs_tiled, outs_acc, *, rows, tile, name, acc_period=None):
    n_tiles = rows // tile
    period = n_tiles if acc_period is None else acc_period
    arrays, in_specs = [], []
    for t in tiled:
        arr, width, cb = t if isinstance(t, tuple) else (t, t.shape[1], 0)
        arrays.append(arr)
        in_specs.append(pl.BlockSpec((tile, width), lambda i, cb=cb: (i, cb)))
    for f in full:
        arr, spec = f if isinstance(f, tuple) else (f, None)
        arrays.append(arr)
        in_specs.append(spec if spec is not None else pl.BlockSpec(arr.shape, lambda i, nd=arr.ndim: (0,) * nd))
    out_shape, out_specs = [], []
    for width, dt in outs_tiled:
        out_shape.append(jax.ShapeDtypeStruct((rows, width), dt))
        out_specs.append(pl.BlockSpec((tile, width), lambda i: (i, 0)))
    for acc in outs_acc:
        shape, dt = acc[0], acc[1]
        out_shape.append(jax.ShapeDtypeStruct(shape, dt))
        out_specs.append(acc[2] if len(acc) > 2 else pl.BlockSpec(shape, lambda i, nd=len(shape): (0,) * nd))
    n_in, n_t, n_a = len(arrays), len(outs_tiled), len(outs_acc)

    def body(*refs):
        vals = [r[...] for r in refs[:n_in]]
        o_t, o_a = fn(*vals)
        for r, v in zip(refs[n_in:n_in + n_t], o_t):
            r[...] = v.astype(r.dtype)
        first = pl.program_id(0) % period == 0
        for r, v in zip(refs[n_in + n_t:n_in + n_t + n_a], o_a):
            v = v.reshape(r.shape)

            @pl.when(first)
            def _(r=r, v=v):
                r[...] = v

            @pl.when(jnp.logical_not(first))
            def _(r=r, v=v):
                r[...] += v

    return pl.pallas_call(
        body, name=name, grid=(n_tiles,), in_specs=in_specs, out_specs=out_specs, out_shape=out_shape,
        compiler_params=_params(),
    )(*arrays)


def _ln_stats(h):
    mu = jnp.mean(h, axis=-1, keepdims=True)
    d = h - mu
    var = jnp.mean(d * d, axis=-1, keepdims=True)
    rstd = lax.rsqrt(var + LN_EPS)
    return d * rstd, rstd


def _ln_bwd_math(h, g, dy):
    xhat, rstd = _ln_stats(h)
    dxhat = dy * g
    dh = rstd * (dxhat - jnp.mean(dxhat, axis=-1, keepdims=True)
                 - xhat * jnp.mean(dxhat * xhat, axis=-1, keepdims=True))
    return dh, jnp.sum(dy * xhat, axis=0, keepdims=True), jnp.sum(dy, axis=0, keepdims=True)


def _ln_fwd(x, r, g, b, *, name):
    n, d = x.shape

    def fn(x, r, g, b):
        xhat, _ = _ln_stats(DN_ALPHA * x + r)
        return (xhat * g + b,), ()

    return _rowwise(fn, [x, r], [g, b], [(d, F32)], [], rows=n, tile=_pick(n, 512, SUBLANE), name=name)[0]


def _ln_bwd(x, r, g, dys, *, name):
    n, d = x.shape
    n_dy = len(dys)

    def fn(x, r, *rest):
        dy = rest[0]
        for e in rest[1:n_dy]:
            dy = dy + e
        dh, dg, db = _ln_bwd_math(DN_ALPHA * x + r, rest[n_dy], dy)
        return (DN_ALPHA * dh, dh), (dg, db)

    return _rowwise(fn, [x, r, *dys], [g], [(d, F32), (d, BF16)], [((1, d), F32), ((1, d), F32)],
                    rows=n, tile=_pick(n, 256, SUBLANE), name=name)


def _final_ln_loss(x, r, target, g, b, *, name):
    n, d = x.shape

    def fn(x, r, t, g, b):
        h = DN_ALPHA * x + r
        xhat, _ = _ln_stats(h)
        err = xhat * g + b - t
        loss = jnp.full((1, LANE), 0.5 * jnp.sum(err * err) / d, F32)
        dh, dg, db = _ln_bwd_math(h, g, err / d)
        return (DN_ALPHA * dh, dh), (loss, dg, db)

    return _rowwise(fn, [x, r, target], [g, b], [(d, F32), (d, BF16)],
                    [((1, LANE), F32), ((1, d), F32), ((1, d), F32)],
                    rows=n, tile=_pick(n, 256, SUBLANE), name=name)


def _mem_heads(qm):
    lane = lax.broadcasted_iota(jnp.int32, (1, MEM_WIDTH), 1)
    for h in range(MEM_HEADS):
        msk = (lane >= h * HEAD_DIM) & (lane < (h + 1) * HEAD_DIM)
        yield msk, jnp.where(msk, qm, 0.0).astype(BF16)


def _mem_softmax(qh, k):
    s = _bdot(qh, k, "nt") * ATT_SCALE
    p = jnp.exp(s - jnp.max(s, axis=-1, keepdims=True))
    return p / jnp.sum(p, axis=-1, keepdims=True)


def _memattn_fwd(tok, proj, memkv, scale, *, seq, name):
    n, tokw = tok.shape
    d = tokw + MEM_WIDTH
    tile = _pick(seq, 512, SUBLANE)

    def fn(tok, qm, kv, scale):
        k, v = kv[:, :MEM_WIDTH].astype(BF16), kv[:, MEM_WIDTH:].astype(BF16)
        out = jnp.zeros(qm.shape, F32)
        for msk, qh in _mem_heads(qm):
            out = jnp.where(msk, _bdot(_mem_softmax(qh, k), v, "nn"), out)
        return (jnp.concatenate([tok * scale, out], axis=1),), ()

    kv_spec = pl.BlockSpec((None,) + memkv.shape[1:], lambda i: (i // (seq // tile), 0, 0))
    return _rowwise(fn, [tok, (proj, MEM_WIDTH, tokw // MEM_WIDTH)], [(memkv, kv_spec), scale], [(d, BF16)], [],
                    rows=n, tile=tile, name=name)[0]


def _memattn_bwd(dmixin, dtok, proj, memkv, *, seq, name):
    n, tokw = dtok.shape
    d = tokw + MEM_WIDTH
    tile = _pick(seq, 512, SUBLANE)

    def fn(dmo, dtok, qm, kv):
        k, v = kv[:, :MEM_WIDTH].astype(BF16), kv[:, MEM_WIDTH:].astype(BF16)
        dq = jnp.zeros(qm.shape, F32)
        dk = jnp.zeros(k.shape, F32)
        dv = jnp.zeros(v.shape, F32)
        for msk, qh in _mem_heads(qm):
            p = _mem_softmax(qh, k)
            doh = jnp.where(msk, dmo, 0.0).astype(BF16)
            dv = dv + _bdot(p, doh, "tn")
            dp = _bdot(doh, v, "nt")
            ds = (p * (dp - jnp.sum(dp * p, axis=-1, keepdims=True))).astype(BF16)
            dq = jnp.where(msk, _bdot(ds, k, "nn") * ATT_SCALE, dq)
            dk = dk + _bdot(ds, qh, "tn") * ATT_SCALE
        return (jnp.concatenate([dtok, dq], axis=1),), (jnp.concatenate([dk, dv], axis=1),)

    tpe = seq // tile
    kv_spec = pl.BlockSpec((None,) + memkv.shape[1:], lambda i: (i // tpe, 0, 0))
    return _rowwise(fn, [(dmixin, MEM_WIDTH, tokw // MEM_WIDTH), dtok, (proj, MEM_WIDTH, tokw // MEM_WIDTH)],
                    [(memkv, kv_spec)], [(d, BF16)], [(memkv.shape, F32, kv_spec)],
                    rows=n, tile=tile, name=name, acc_period=tpe)


def _scale_bwd(dmixin, mixed, scale, *, name):
    n, tokw = mixed.shape

    def fn(dt, mixed, scale):
        return (dt * scale,), (jnp.sum(dt * mixed, axis=0, keepdims=True),)

    return _rowwise(fn, [(dmixin, tokw, 0), mixed], [scale], [(tokw, BF16)], [((1, tokw), F32)],
                    rows=n, tile=_pick(n, 512, SUBLANE), name=name)


def _chunk_rows(seq):
    return _pick(seq, 512, SUBLANE)


def _load_ext(ref, c, rows, before, after, seq):
    lo, hi = c * rows - before, (c + 1) * rows + after
    parts = []
    if lo < 0:
        parts.append(jnp.zeros((-lo, ref.shape[1]), F32))
    parts.append(ref[max(lo, 0):min(hi, seq), :])
    if hi > seq:
        parts.append(jnp.zeros((hi - seq, ref.shape[1]), F32))
    return parts[0] if len(parts) == 1 else jnp.concatenate(parts, axis=0)


def _down(x, k):
    return pltpu.roll(x, k, 0)


def _up(x, k):
    return pltpu.roll(x, x.shape[0] - k, 0)


def _window_sums(ext, shift, col0, group):
    lane = col0 + lax.broadcasted_iota(jnp.int32, (1, ext.shape[1]), 1)
    gidx = lane // group
    s = ext
    out = None
    k = 1
    for gi, w in enumerate(POOL_WINDOWS):
        while k < w:
            s = s + shift(s, k)
            k *= 2
        out = s if out is None else jnp.where(gidx >= gi, s, out)
    return out, jnp.left_shift(2, jnp.minimum(gidx, len(POOL_WINDOWS) - 1))


def _pool_fwd(proj3, tokw, *, name):
    nb, seq, _ = proj3.shape
    rows = _chunk_rows(seq)
    group = tokw // len(POOL_WINDOWS)

    def body(u_ref, o_ref):
        col0 = pl.program_id(1) * LANE
        for c in range(seq // rows):
            ext = _load_ext(u_ref, c, rows, MAX_WINDOW, 0, seq)
            sums, win = _window_sums(ext, _down, col0, group)
            t = c * rows + lax.broadcasted_iota(jnp.int32, (rows, 1), 0)
            count = jnp.minimum(t + 1, win).astype(F32)
            o_ref[c * rows:(c + 1) * rows, :] = (sums[MAX_WINDOW:, :] / count - ext[MAX_WINDOW:, :]).astype(BF16)

    spec = pl.BlockSpec((None, seq, LANE), lambda b, j: (b, 0, j))
    return pl.pallas_call(
        body, name=name, grid=(nb, tokw // LANE), in_specs=[spec], out_specs=spec,
        out_shape=jax.ShapeDtypeStruct((nb, seq, tokw), BF16), compiler_params=_params(),
    )(proj3)


def _pool_bwd(dp3, *, name):
    nb, seq, tokw = dp3.shape
    rows = _chunk_rows(seq)
    group = tokw // len(POOL_WINDOWS)

    def body(d_ref, o_ref):
        col0 = pl.program_id(1) * LANE
        for c in range(seq // rows):
            ext = _load_ext(d_ref, c, rows, 0, MAX_WINDOW, seq)
            lane = col0 + lax.broadcasted_iota(jnp.int32, (1, LANE), 1)
            win = jnp.left_shift(2, jnp.minimum(lane // group, len(POOL_WINDOWS) - 1))
            t = c * rows + lax.broadcasted_iota(jnp.int32, (rows + MAX_WINDOW, 1), 0)
            scaled = ext / jnp.minimum(t + 1, win).astype(F32)
            sums, _ = _window_sums(scaled, _up, col0, group)
            o_ref[c * rows:(c + 1) * rows, :] = sums[:rows, :] - ext[:rows, :]

    spec = pl.BlockSpec((None, seq, LANE), lambda b, j: (b, 0, j))
    return pl.pallas_call(
        body, name=name, grid=(nb, tokw // LANE), in_specs=[spec], out_specs=spec,
        out_shape=jax.ShapeDtypeStruct((nb, seq, tokw), F32), compiler_params=_params(),
    )(dp3)


def _conv3(ext, w_ref, b_ref):
    x1, x2 = _down(ext, 1), _down(ext, 2)
    return w_ref[0:1, :] * x2 + w_ref[1:2, :] * x1 + w_ref[2:3, :] * ext + b_ref[...], x1, x2


def _convgate_fwd(up3, cw, cb, *, name):
    nb, seq, c2 = up3.shape
    fp = c2 // 2
    nblk = fp // LANE
    rows = _chunk_rows(seq)

    def body(u_ref, g_ref, wu_ref, wg_ref, bu_ref, bg_ref, o_ref):
        for c in range(seq // rows):
            hu, _, _ = _conv3(_load_ext(u_ref, c, rows, SUBLANE, 0, seq), wu_ref, bu_ref)
            hg, _, _ = _conv3(_load_ext(g_ref, c, rows, SUBLANE, 0, seq), wg_ref, bg_ref)
            o_ref[c * rows:(c + 1) * rows, :] = (hg * jax.nn.sigmoid(hg) * hu)[SUBLANE:, :].astype(BF16)

    def col(off, r):
        return pl.BlockSpec((r, LANE), lambda b, j: (0, j + off))

    def act(off):
        return pl.BlockSpec((None, seq, LANE), lambda b, j: (b, 0, j + off))

    return pl.pallas_call(
        body, name=name, grid=(nb, nblk),
        in_specs=[act(0), act(nblk), col(0, SUBLANE), col(nblk, SUBLANE), col(0, 1), col(nblk, 1)],
        out_specs=act(0), out_shape=jax.ShapeDtypeStruct((nb, seq, fp), BF16), compiler_params=_params(),
    )(up3, up3, cw, cw, cb, cb)


def _convgate_bwd(up3, dact3, cw, cb, *, name):
    nb, seq, c2 = up3.shape
    fp = c2 // 2
    nblk = fp // LANE
    rows = _chunk_rows(seq)
    h = SUBLANE

    def body(u_ref, g_ref, da_ref, wu_ref, wg_ref, bu_ref, bg_ref, du_ref, dg_ref, dwu_ref, dwg_ref, dbu_ref,
             dbg_ref):
        @pl.when(pl.program_id(1) == 0)
        def _():
            for r in (dwu_ref, dwg_ref, dbu_ref, dbg_ref):
                r[...] = jnp.zeros(r.shape, F32)

        for c in range(seq // rows):
            eu = _load_ext(u_ref, c, rows, h, h, seq)
            eg = _load_ext(g_ref, c, rows, h, h, seq)
            da = _load_ext(da_ref, c, rows, h, h, seq)
            hu, u1, u2 = _conv3(eu, wu_ref, bu_ref)
            hg, g1, g2 = _conv3(eg, wg_ref, bg_ref)
            sig = jax.nn.sigmoid(hg)
            dhu = da * hg * sig
            dhg = da * hu * sig * (1.0 + hg * (1.0 - sig))
            for dh, w_ref, x0, x1, x2, dx_ref, dw_ref, db_ref in (
                    (dhu, wu_ref, eu, u1, u2, du_ref, dwu_ref, dbu_ref),
                    (dhg, wg_ref, eg, g1, g2, dg_ref, dwg_ref, dbg_ref)):
                dx = w_ref[2:3, :] * dh + w_ref[1:2, :] * _up(dh, 1) + w_ref[0:1, :] * _up(dh, 2)
                dx_ref[c * rows:(c + 1) * rows, :] = dx[h:h + rows, :].astype(BF16)
                core = dh[h:h + rows, :]
                for k, xk in ((0, x2), (1, x1), (2, x0)):
                    dw_ref[k:k + 1, :] += jnp.sum(core * xk[h:h + rows, :], axis=0, keepdims=True)
                db_ref[...] += jnp.sum(core, axis=0, keepdims=True)

    def col(off, r):
        return pl.BlockSpec((r, LANE), lambda j, b: (0, j + off))

    def act(off):
        return pl.BlockSpec((None, seq, LANE), lambda j, b: (b, 0, j + off))

    du, dg, dwu, dwg, dbu, dbg = pl.pallas_call(
        body, name=name, grid=(nblk, nb),
        in_specs=[act(0), act(nblk), act(0), col(0, SUBLANE), col(nblk, SUBLANE), col(0, 1), col(nblk, 1)],
        out_specs=[act(0), act(0), col(0, SUBLANE), col(0, SUBLANE), col(0, 1), col(0, 1)],
        out_shape=[jax.ShapeDtypeStruct((nb, seq, fp), BF16), jax.ShapeDtypeStruct((nb, seq, fp), BF16),
                   jax.ShapeDtypeStruct((SUBLANE, fp), F32), jax.ShapeDtypeStruct((SUBLANE, fp), F32),
                   jax.ShapeDtypeStruct((1, fp), F32), jax.ShapeDtypeStruct((1, fp), F32)],
        compiler_params=_params(),
    )(up3, up3, dact3, cw, cw, cb, cb)
    return du, dg, jnp.concatenate([dwu, dwg], axis=1), jnp.concatenate([dbu, dbg], axis=1)


def _scan_rows(x, shift, valid):
    row = lax.broadcasted_iota(jnp.int32, (x.shape[0], 1), 0)
    k = 1
    while k < x.shape[0]:
        x = x + jnp.where(valid(row, k), shift(x, k), 0.0)
        k *= 2
    return x


def _pick_row(x, r):
    row = lax.broadcasted_iota(jnp.int32, (x.shape[0], 1), 0)
    return jnp.sum(jnp.where(row == r, x, 0.0), axis=0, keepdims=True)


def _log_sigmoid(z):
    return jnp.minimum(z, 0.0) - jnp.log(1.0 + jnp.exp(-jnp.abs(z)))


def _gate_fwd(kvf3, fb, col_block, *, name):
    nb, seq, _ = kvf3.shape
    rows = _chunk_rows(seq)

    def body(f_ref, fb_ref, o_ref):
        carry = jnp.zeros((1, LANE), F32)
        for c in range(seq // rows):
            logf = _log_sigmoid(f_ref[c * rows:(c + 1) * rows, :] + fb_ref[...])
            run = _scan_rows(logf, _down, lambda row, k: row >= k) + carry
            o_ref[c * rows:(c + 1) * rows, :] = run
            carry = _pick_row(run, rows - 1)

    return pl.pallas_call(
        body, name=name, grid=(nb,),
        in_specs=[pl.BlockSpec((None, seq, LANE), lambda b: (b, 0, col_block)),
                  pl.BlockSpec((1, LANE), lambda b: (0, 0))],
        out_specs=pl.BlockSpec((None, seq, LANE), lambda b: (b, 0, 0)),
        out_shape=jax.ShapeDtypeStruct((nb, seq, LANE), F32), compiler_params=_params(),
    )(kvf3, fb)


def _gate_bwd(kvf3, fb, dF3, col_block, heads, *, name):
    nb, seq, _ = kvf3.shape
    rows = _chunk_rows(seq)

    def body(f_ref, fb_ref, d_ref, o_ref, dfb_ref):
        @pl.when(pl.program_id(0) == 0)
        def _():
            dfb_ref[...] = jnp.zeros(dfb_ref.shape, F32)

        lane = lax.broadcasted_iota(jnp.int32, (1, LANE), 1)
        carry = jnp.zeros((1, LANE), F32)
        for c in reversed(range(seq // rows)):
            run = _scan_rows(d_ref[c * rows:(c + 1) * rows, :], _up, lambda row, k: row < rows - k) + carry
            carry = _pick_row(run, 0)
            z = f_ref[c * rows:(c + 1) * rows, :] + fb_ref[...]
            df = jnp.where(lane < heads, run * jax.nn.sigmoid(-z), 0.0)
            o_ref[c * rows:(c + 1) * rows, :] = df
            dfb_ref[...] += jnp.sum(df, axis=0, keepdims=True)

    return pl.pallas_call(
        body, name=name, grid=(nb,),
        in_specs=[pl.BlockSpec((None, seq, LANE), lambda b: (b, 0, col_block)),
                  pl.BlockSpec((1, LANE), lambda b: (0, 0)),
                  pl.BlockSpec((None, seq, LANE), lambda b: (b, 0, 0))],
        out_specs=[pl.BlockSpec((None, seq, LANE), lambda b: (b, 0, 0)), pl.BlockSpec((1, LANE), lambda b: (0, 0))],
        out_shape=[jax.ShapeDtypeStruct((nb, seq, LANE), F32), jax.ShapeDtypeStruct((1, LANE), F32)],
        compiler_params=_params(),
    )(kvf3, fb, dF3)


def _head_masks():
    lane = lax.broadcasted_iota(jnp.int32, (1, LANE), 1)
    return (lane < HEAD_DIM, lane >= HEAD_DIM)


BIAS_TERMS = 3


def _bias_lanes(gsum):
    nb, seq, heads = gsum.shape
    terms, rest = [], gsum
    for _ in range(BIAS_TERMS):
        t = lax.reduce_precision(rest, exponent_bits=8, mantissa_bits=7)
        terms.append(t)
        rest = rest - t
    ones = [jnp.ones_like(gsum)] * BIAS_TERMS

    def lanes(parts):
        z = jnp.stack(parts, axis=-1)
        z = jnp.pad(z, ((0, 0), (0, 0), (0, 0), (0, HEAD_DIM - 2 * BIAS_TERMS)))
        z = z.reshape(nb, seq, heads // 2, 2, HEAD_DIM)[:, :, :, ::-1]
        return z.reshape(nb, seq, heads * HEAD_DIM).astype(BF16)

    return lanes(terms + ones), lanes(ones + [-t for t in terms])


def _fox_scores(q, k, aq, ak, masked):
    qs = (q * ATT_SCALE).astype(BF16)
    qts = [jnp.where(msk, qs, aq) for msk in _head_masks()]
    ss = [_bdot(qt, jnp.where(msk, k, ak), "nt") for qt, msk in zip(qts, _head_masks())]
    if masked:
        t = q.shape[0]
        keep = lax.broadcasted_iota(jnp.int32, (t, t), 0) >= lax.broadcasted_iota(jnp.int32, (t, t), 1)
        ss = [jnp.where(keep, s, NEG_BIG) for s in ss]
    return ss, qts


def _on_blocks(qi, ki, step):
    @pl.when(ki < qi)
    def _():
        step(False)

    @pl.when(ki == qi)
    def _():
        step(True)


def _fox_grid(nblk, tokw, t, q_major):
    if q_major:
        pairs = [(qi, ki) for qi in range(nblk) for ki in range(qi + 1)]
    else:
        pairs = [(qi, ki) for ki in range(nblk) for qi in range(ki, nblk)]
    tables = [jnp.array([p[i] for p in pairs], jnp.int32) for i in (0, 1)]

    def q_spec(off=0, wide=False):
        width = 2 * LANE if wide else LANE
        return pl.BlockSpec((None, t, width), lambda b, p, i, qt, kt: (b, qt[i], p + off))

    def k_spec(off=0):
        return pl.BlockSpec((None, t, LANE), lambda b, p, i, qt, kt: (b, kt[i], p + off))

    return tables, len(pairs), q_spec, k_spec, tokw // LANE


def _lanes(col):
    return jnp.broadcast_to(col, (col.shape[0], LANE))


def _across(stat, width):
    return jnp.tile(stat, (1, width // LANE))


def _fox_fwd(proj3, kvf3, aq3, ak3, tokw, *, name):
    nb, seq, _ = proj3.shape
    t = _pick(seq, 512, LANE)
    tables, n_pairs, q_spec, k_spec, hp0 = _fox_grid(seq // t, tokw, t, True)

    def body(qt_ref, kt_ref, q_ref, k_ref, v_ref, aq_ref, ak_ref, o_ref, lse_ref, m_s, l_s, acc_s):
        i = pl.program_id(2)
        qi, ki = qt_ref[i], kt_ref[i]

        @pl.when(ki == 0)
        def _():
            m_s[...] = jnp.full(m_s.shape, NEG_BIG, F32)
            l_s[...] = jnp.zeros(l_s.shape, F32)
            acc_s[...] = jnp.zeros(acc_s.shape, F32)

        def step(masked):
            v = v_ref[...].astype(BF16)
            ss, _ = _fox_scores(q_ref[...], k_ref[...].astype(BF16), aq_ref[...], ak_ref[...], masked)
            for h, s in enumerate(ss):
                m_old = m_s[h]
                m_new = jnp.maximum(m_old, _lanes(jnp.max(s, axis=-1, keepdims=True)))
                alpha = jnp.exp(m_old - m_new)
                p = jnp.exp(s - _across(m_new, t))
                l_s[h] = alpha * l_s[h] + _lanes(jnp.sum(p, axis=-1, keepdims=True))
                acc_s[h] = alpha * acc_s[h] + _bdot(p, v, "nn")
                m_s[h] = m_new

        _on_blocks(qi, ki, step)

        @pl.when(ki == qi)
        def _():
            o_ref[...] = jnp.where(_head_masks()[0], acc_s[0] / l_s[0], acc_s[1] / l_s[1])
            lse_ref[...] = jnp.concatenate([m_s[0] + jnp.log(l_s[0]), m_s[1] + jnp.log(l_s[1])], axis=1)

    stat = pltpu.VMEM((2, t, LANE), F32)
    return pl.pallas_call(
        body, name=name,
        grid_spec=pltpu.PrefetchScalarGridSpec(
            num_scalar_prefetch=2, grid=(nb, hp0, n_pairs),
            in_specs=[q_spec(), k_spec(), k_spec(hp0), q_spec(), k_spec()],
            out_specs=[q_spec(), q_spec(wide=True)], scratch_shapes=[stat, stat, stat]),
        out_shape=[jax.ShapeDtypeStruct((nb, seq, tokw), F32), jax.ShapeDtypeStruct((nb, seq, 2 * tokw), F32)],
        compiler_params=_params(),
    )(*tables, proj3, kvf3, kvf3, aq3, ak3)


def _fox_bwd_common(q_ref, k_ref, v_ref, aq_ref, ak_ref, do_ref, lse_ref, delta_ref, masked):
    k, v = k_ref[...].astype(BF16), v_ref[...].astype(BF16)
    ss, qts = _fox_scores(q_ref[...], k, aq_ref[...], ak_ref[...], masked)
    do = do_ref[...]
    t = do.shape[0]
    out = []
    for h, (s, qt, msk) in enumerate(zip(ss, qts, _head_masks())):
        doh = jnp.where(msk, do, 0.0).astype(BF16)
        p = jnp.exp(s - _across(lse_ref[:, h * LANE:(h + 1) * LANE], t))
        ds = p * (_bdot(doh, v, "nt") - _across(delta_ref[:, h * LANE:(h + 1) * LANE], t))
        out.append((qt, doh, p, ds))
    return out, k


def _fox_bwd(proj3, kvf3, aq3, ak3, o3, dmixin3, lse3, tokw, *, name):
    nb, seq, _ = proj3.shape
    t = _pick(seq, 512, LANE)
    nblk = seq // t
    tables, n_pairs, q_spec, k_spec, hp0 = _fox_grid(nblk, tokw, t, True)
    whole = pl.BlockSpec((None, seq, LANE), lambda b, p, i, qt, kt: (b, 0, p))
    dfk_spec = pl.BlockSpec((None, None, nblk, SUBLANE, t), lambda b, p, i, qt, kt: (b, p, 0, 0, 0))

    def body(qt_ref, kt_ref, q_ref, k_ref, v_ref, aq_ref, ak_ref, o_ref, do_ref, lse_ref, dq_ref, dk_ref, dv_ref,
             dfk_ref, acc_s, row_s, delta_s):
        i = pl.program_id(2)
        qi, ki = qt_ref[i], kt_ref[i]

        @pl.when(i == 0)
        def _():
            dk_ref[...] = jnp.zeros(dk_ref.shape, F32)
            dv_ref[...] = jnp.zeros(dv_ref.shape, F32)
            dfk_ref[...] = jnp.zeros(dfk_ref.shape, F32)

        @pl.when(ki == 0)
        def _():
            acc_s[...] = jnp.zeros(acc_s.shape, F32)
            row_s[...] = jnp.zeros(row_s.shape, F32)
            prod = do_ref[...] * o_ref[...]
            delta_s[...] = jnp.concatenate(
                [_lanes(jnp.sum(jnp.where(msk, prod, 0.0), axis=-1, keepdims=True)) for msk in _head_masks()],
                axis=1)

        def step(masked):
            heads, k = _fox_bwd_common(q_ref, k_ref, v_ref, aq_ref, ak_ref, do_ref, lse_ref, delta_s, masked)
            rows = pl.ds(pl.multiple_of(ki * t, t), t)
            for h, ((qt, doh, p, ds), msk) in enumerate(zip(heads, _head_masks())):
                acc_s[h] += _bdot(ds, k, "nn")
                row_s[h] += _lanes(jnp.sum(ds, axis=-1, keepdims=True))
                dv_ref[rows, :] += _bdot(p, doh, "tn")
                dk_ref[rows, :] += jnp.where(msk, _bdot(ds, qt, "tn"), 0.0)
                dfk_ref[ki, h:h + 1, :] -= jnp.sum(ds, axis=0, keepdims=True)

        _on_blocks(qi, ki, step)

        @pl.when(ki == qi)
        def _():
            dq_ref[...] = jnp.where(_head_masks()[0], acc_s[0], acc_s[1]) * ATT_SCALE
            for h in range(2):
                dfk_ref[qi, 2 + h:3 + h, :] = row_s[h].T[0:1, :]

    out = jax.ShapeDtypeStruct((nb, seq, tokw), F32)
    stat = pltpu.VMEM((2, t, LANE), F32)
    return pl.pallas_call(
        body, name=name,
        grid_spec=pltpu.PrefetchScalarGridSpec(
            num_scalar_prefetch=2, grid=(nb, hp0, n_pairs),
            in_specs=[q_spec(), k_spec(), k_spec(hp0), q_spec(), k_spec(), q_spec(), q_spec(), q_spec(wide=True)],
            out_specs=[q_spec(), whole, whole, dfk_spec],
            scratch_shapes=[stat, stat, pltpu.VMEM((t, 2 * LANE), F32)]),
        out_shape=[out, out, out, jax.ShapeDtypeStruct((nb, hp0, nblk, SUBLANE, t), F32)],
        compiler_params=_params(),
    )(*tables, proj3, kvf3, kvf3, aq3, ak3, o3, dmixin3, lse3)


def _peer(k):
    x, y, c = lax.axis_index("x"), lax.axis_index("y"), lax.axis_index("c")
    return (1 - x if k & 4 else x, 1 - y if k & 2 else y, 1 - c if k & 1 else c)


def _dev_index(p):
    return 4 * p[0] + 2 * p[1] + p[2]


_HBM = pl.BlockSpec(memory_space=pltpu.HBM)
CHIP_RELATIONS = (2, 4, 6)


def _chip_index(p):
    return 2 * p[0] + p[1]


def _run_copies(sends, recvs):
    for cp in sends:
        cp.start()
    for cp in recvs:
        cp.wait_recv()
    for cp in sends:
        cp.wait_send()


def _gather_shards(whole, halved, side_by_side, *, name):
    nw, nh = len(whole), len(halved) + len(side_by_side)
    n = nw + nh
    n_sem = 3 * nw + 6 * nh

    def body(*refs):
        ins, outs, send_sems, recv_sems, local_sems = refs[:n], refs[n:2 * n], refs[2 * n], refs[2 * n + 1], refs[-1]
        me, sib = _peer(0), _peer(1)
        q, c = _chip_index(me), me[2]
        own = []

        def copy(src, dst, s, to):
            return pltpu.make_async_remote_copy(src_ref=src, dst_ref=dst, send_sem=send_sems.at[s],
                                                recv_sem=recv_sems.at[s], device_id=to, device_id_type=MESH_T)

        sends, recvs, passes = [], [], []
        for j, k in enumerate(CHIP_RELATIONS):
            peer = _peer(k)
            pq = _chip_index(peer)
            for i in range(nw):
                sends.append(copy(ins[i], outs[i].at[q], 3 * i + j, peer))
                recvs.append(copy(ins[i], outs[i].at[pq], 3 * i + j, peer))
            for i in range(nh):
                src, out, s = ins[nw + i], outs[nw + i], 3 * nw + 6 * i + j
                if i < len(halved):
                    place = lambda chip, half, out=out: out.at[chip, half]
                else:
                    cols = src.shape[-1]
                    place = lambda chip, half, out=out, cols=cols: out.at[
                        half, :, pl.ds(pl.multiple_of(chip * cols, LANE), cols)]
                    if j == 0:
                        own += [pltpu.make_async_copy(src.at[h], place(q, h), local_sems.at[len(own) + h])
                                for h in range(2)]
                sends.append(copy(src.at[c], place(q, c), s, peer))
                passes.append((copy(src.at[c], place(pq, c), s, peer), copy(place(pq, c), place(pq, c), s + 3, sib),
                               copy(place(pq, c), place(pq, 1 - c), s + 3, sib)))
        for cp in sends + own:
            cp.start()
        for arrival, hand_over, _ in passes:
            arrival.wait_recv()
            hand_over.start()
        for cp in recvs:
            cp.wait_recv()
        for _, _, from_sibling in passes:
            from_sibling.wait_recv()
        for cp in sends + [hand_over for _, hand_over, _ in passes]:
            cp.wait_send()
        for cp in own:
            cp.wait()

    arrays = list(whole) + list(halved) + list(side_by_side)
    return pl.pallas_call(
        body, name=name, in_specs=[_HBM] * n, out_specs=[_HBM] * n,
        out_shape=[jax.ShapeDtypeStruct((N_CHIP,) + a.shape, a.dtype) for a in list(whole) + list(halved)]
        + [jax.ShapeDtypeStruct(a.shape[:-1] + (N_CHIP * a.shape[-1],), a.dtype) for a in side_by_side],
        scratch_shapes=[pltpu.SemaphoreType.DMA((n_sem,)), pltpu.SemaphoreType.DMA((n_sem,)),
                        pltpu.SemaphoreType.DMA((2 * len(side_by_side),))],
    )(*arrays)


def _to_sibling(grads, *, name):
    n = len(grads)

    def body(*refs):
        ins, outs, send_sems, recv_sems = refs[:n], refs[n:2 * n], refs[2 * n], refs[2 * n + 1]
        c = lax.axis_index("c")
        sends = [pltpu.make_async_remote_copy(src_ref=ins[i].at[:, 1 - c], dst_ref=outs[i], send_sem=send_sems.at[i],
                                              recv_sem=recv_sems.at[i], device_id=_peer(1), device_id_type=MESH_T)
                 for i in range(n)]
        _run_copies(sends, sends)

    return pl.pallas_call(
        body, name=name, in_specs=[_HBM] * n, out_specs=[_HBM] * n,
        out_shape=[jax.ShapeDtypeStruct(g.shape[:1] + g.shape[2:], g.dtype) for g in grads],
        scratch_shapes=[pltpu.SemaphoreType.DMA((n,)), pltpu.SemaphoreType.DMA((n,))],
    )(*grads)


def _pair_add(grads, from_sibling, qc, *, name):
    _, _, rows, cols = grads.shape
    tile = _pick(rows, 1024, SUBLANE)

    def body(qc_ref, g_ref, s_ref, o_ref):
        del qc_ref
        o_ref[...] = g_ref[...] + s_ref[...]

    spec = pl.BlockSpec((None, tile, cols), lambda j, i, qc: (j, i, 0))
    return pl.pallas_call(
        body, name=name,
        grid_spec=pltpu.PrefetchScalarGridSpec(
            num_scalar_prefetch=1, grid=(N_CHIP, rows // tile),
            in_specs=[pl.BlockSpec((None, None, tile, cols), lambda j, i, qc: (j, qc[1], i, 0)), spec],
            out_specs=spec),
        out_shape=jax.ShapeDtypeStruct((N_CHIP, rows, cols), F32), compiler_params=_params(),
    )(qc, grads, from_sibling)


def _to_chips(sums, small, *, name):
    n = len(sums)

    def body(*refs):
        ins, small_ref, outs, small_out = refs[:n], refs[n], refs[n + 1:2 * n + 1], refs[2 * n + 1]
        send_sems, recv_sems, local_sem = refs[2 * n + 2:]
        me = _peer(0)

        def copy(src, dst, s, to):
            return pltpu.make_async_remote_copy(src_ref=src, dst_ref=dst, send_sem=send_sems.at[s],
                                                recv_sem=recv_sems.at[s], device_id=to, device_id_type=MESH_T)

        mine = pltpu.make_async_copy(small_ref.at[_dev_index(me)], small_out.at[_dev_index(me)], local_sem)
        mine.start()
        sends, recvs = [], []
        for j, k in enumerate(CHIP_RELATIONS):
            peer = _peer(k)
            for i in range(n):
                src = ins[i].at[_chip_index(peer)]
                sends.append(copy(src, outs[i].at[j], 3 * i + j, peer))
                recvs.append(copy(src, outs[i].at[j], 3 * i + j, peer))
        for k in range(1, N_DEV):
            peer = _peer(k)
            src = small_ref.at[_dev_index(peer)]
            sends.append(copy(src, small_out.at[_dev_index(me)], 3 * n + k - 1, peer))
            recvs.append(copy(src, small_out.at[_dev_index(peer)], 3 * n + k - 1, peer))
        _run_copies(sends, recvs)
        mine.wait()

    n_sem = 3 * n + N_DEV - 1
    return pl.pallas_call(
        body, name=name, in_specs=[_HBM] * (n + 1), out_specs=[_HBM] * (n + 1),
        out_shape=[jax.ShapeDtypeStruct((3,) + g.shape[1:], g.dtype) for g in sums]
        + [jax.ShapeDtypeStruct(small.shape, small.dtype)],
        scratch_shapes=[pltpu.SemaphoreType.DMA((n_sem,)), pltpu.SemaphoreType.DMA((n_sem,)),
                        pltpu.SemaphoreType.DMA],
    )(*sums, small)


def _swap_halves(arrays, *, name):
    n = len(arrays)

    def body(*refs):
        outs, send_sems, recv_sems = refs[n:2 * n], refs[2 * n], refs[2 * n + 1]
        c = lax.axis_index("c")
        sib = _peer(1)
        sends, recvs = [], []
        for i in range(n):
            sem = dict(send_sem=send_sems.at[i], recv_sem=recv_sems.at[i], device_id=sib, device_id_type=MESH_T)
            sends.append(pltpu.make_async_remote_copy(src_ref=outs[i].at[c], dst_ref=outs[i].at[c], **sem))
            recvs.append(pltpu.make_async_remote_copy(src_ref=outs[i].at[c], dst_ref=outs[i].at[1 - c], **sem))
        _run_copies(sends, recvs)

    return pl.pallas_call(
        body, name=name, in_specs=[_HBM] * n, out_specs=[_HBM] * n,
        out_shape=[jax.ShapeDtypeStruct(a.shape, a.dtype) for a in arrays],
        input_output_aliases={i: i for i in range(n)},
        scratch_shapes=[pltpu.SemaphoreType.DMA((n,)), pltpu.SemaphoreType.DMA((n,))],
    )(*arrays)


def _adam_math(g, w, m, v):
    bc1 = 1.0 - ADAM_B1 ** ADAM_STEP
    bc2 = 1.0 - ADAM_B2 ** ADAM_STEP
    m_new = ADAM_B1 * m + (1.0 - ADAM_B1) * g
    v_new = ADAM_B2 * v + (1.0 - ADAM_B2) * (g * g)
    delta = -ADAM_LR * ((m_new / bc1) / (jnp.sqrt(v_new / bc2) + ADAM_EPS) + ADAM_WD * w)
    return delta, m_new, v_new


def _reduce_half(sums, parts, qc, *, name):
    _, rows, cols = sums.shape
    tile = _pick(rows, 1024, SUBLANE)
    n_parts = parts.shape[0]

    def body(qc_ref, g_ref, p_ref, o_ref):
        del qc_ref
        g = g_ref[...]
        for k in range(n_parts):
            g = g + p_ref[k]
        o_ref[...] = g

    return pl.pallas_call(
        body, name=name,
        grid_spec=pltpu.PrefetchScalarGridSpec(
            num_scalar_prefetch=1, grid=(rows // tile,),
            in_specs=[pl.BlockSpec((None, tile, cols), lambda i, qc: (qc[0], i, 0)),
                      pl.BlockSpec((n_parts, tile, cols), lambda i, qc: (0, i, 0))],
            out_specs=pl.BlockSpec((None, tile, cols), lambda i, qc: (qc[1], i, 0))),
        out_shape=jax.ShapeDtypeStruct((2, rows, cols), F32), compiler_params=_params(),
    )(qc, sums, parts)


def _adamw_shard(g, w, m, v, *, name):
    _, rows, cols = g.shape
    tile = _pick(rows, 512, SUBLANE)

    def body(g_ref, w_ref, m_ref, v_ref, do_ref, mo_ref, vo_ref):
        do_ref[...], mo_ref[...], vo_ref[...] = _adam_math(g_ref[...], w_ref[...], m_ref[...], v_ref[...])

    spec = pl.BlockSpec((None, tile, cols), lambda h, i: (h, i, 0))
    return pl.pallas_call(
        body, name=name, grid=(2, rows // tile), in_specs=[spec] * 4, out_specs=[spec] * 3,
        out_shape=[jax.ShapeDtypeStruct(g.shape, F32)] * 3, compiler_params=_params(),
    )(g, w, m, v)


def _adamw(parts, w, m, v, *, name):
    _, rows, cols = parts.shape
    tile = _pick(rows, 256, SUBLANE)

    def body(p_ref, w_ref, m_ref, v_ref, o_ref):
        g = p_ref[0]
        for i in range(1, N_DEV):
            g = g + p_ref[i]
        delta, m_new, v_new = _adam_math(g, w_ref[...], m_ref[...], v_ref[...])
        o_ref[0] = g
        o_ref[1] = delta
        o_ref[2] = m_new
        o_ref[3] = v_new

    spec = pl.BlockSpec((tile, cols), lambda i: (i, 0))
    return pl.pallas_call(
        body, name=name, grid=(rows // tile,),
        in_specs=[pl.BlockSpec((N_DEV, tile, cols), lambda i: (0, i, 0)), spec, spec, spec],
        out_specs=pl.BlockSpec((4, tile, cols), lambda i: (0, i, 0)),
        out_shape=jax.ShapeDtypeStruct((4, rows, cols), F32), compiler_params=_params(),
    )(parts, w, m, v)


def _layout(shapes, names, align):
    out, off = [], 0
    for n in names:
        size = math.prod(shapes[n])
        out.append((n, tuple(shapes[n]), off, size))
        off += _round_up(size, align)
    return out, off


def _pack(arrays, layout, total, lead=()):
    parts = []
    for i, (n, _, off, size) in enumerate(layout):
        end = layout[i + 1][2] if i + 1 < len(layout) else total
        flat = arrays[n].reshape(lead + (size,))
        if end - off > size:
            flat = jnp.pad(flat, [(0, 0)] * len(lead) + [(0, end - off - size)])
        parts.append(flat)
    return jnp.concatenate(parts, axis=len(lead))


def _unpack(flat, layout, lead=()):
    return {n: flat[..., off:off + size].reshape(lead + shape) for n, shape, off, size in layout}


def _to_shards(full, axis):
    shp = full.shape
    return jnp.moveaxis(full.reshape(shp[:axis] + (N_CHIP, shp[axis] // N_CHIP) + shp[axis + 1:]), axis, 0)


def _from_shards(shards, axis):
    x = jnp.moveaxis(shards, 0, axis)
    shp = x.shape
    return x.reshape(shp[:axis] + (shp[axis] * shp[axis + 1],) + shp[axis + 2:])


def _pad_cols(w, per, padded):
    lead = w.shape[:-1]
    x = w.reshape(lead + (-1, per))
    x = jnp.pad(x, [(0, 0)] * len(lead) + [(0, 0), (0, padded - per)])
    return x.reshape(lead + (-1,))


def _unpad_cols(w, per, padded):
    lead = w.shape[:-1]
    return w.reshape(lead + (-1, padded))[..., :per].reshape(lead + (-1,))


def _local_step(x, mem, target, W):
    nb, seq, d = x.shape
    n = nb * seq
    tokw = d - MEM_WIDTH
    heads = tokw // HEAD_DIM
    mlen = mem.shape[1]
    per = W["ffn_w_down"].shape[1] // 2
    per_p = _round_up(per, LANE)
    fp = 2 * per_p
    kvw = 2 * tokw + heads
    kvp = 2 * tokw + LANE
    gate_block = 2 * tokw // LANE

    x2d = x.reshape(n, d)
    mem2d = mem.reshape(nb * mlen, d)
    t2d = target.reshape(n, d)
    row = lambda a: a.reshape(1, -1)
    ones_tok = jnp.ones((1, tokw), F32)

    pool_bd = jax.scipy.linalg.block_diag(*[W["a_pool_w"][0, i] for i in range(len(POOL_WINDOWS))]).astype(BF16)
    kv_w = jnp.pad(W["kv_w"], ((0, 0), (0, kvp - kvw)))
    fb = jnp.pad(W["f_b"], (0, LANE - heads)).reshape(1, LANE)
    w_up = W.get("ffn_w_up_padded")
    if w_up is None:
        w_up = jnp.stack([_pad_cols(W["ffn_w_up"][l], per, per_p) for l in range(DEPTH)])
    w_down = [jnp.pad(W["ffn_w_down"][l].reshape(2, per, d), ((0, 0), (0, per_p - per), (0, 0))).reshape(fp, d)
              for l in range(DEPTH)]
    conv_w = [jnp.pad(_pad_cols(W["ffn_conv_w"][l], per, per_p), ((0, SUBLANE - CONV_WIDTH), (0, 0)))
              for l in range(DEPTH)]
    conv_b = [_pad_cols(W["ffn_conv_b"][l], per, per_p).reshape(1, 2 * fp) for l in range(DEPTH)]
    w_in = [W["a_w_in"][0], W["b_w_q"][0]]
    w_out = [W["a_w_out"][0], W["b_w_out"][0]]

    saved = []
    cur = x2d
    for l in range(DEPTH):
        s = {"x_in": cur}
        memkv = _mm(mem2d, W["mem_w_kv"][l], "nn", name=f"memkv{l}").reshape(nb, mlen, 2 * MEM_WIDTH)
        if l == 0:
            proj = _mm(cur, w_in[l], "nn", name="proj0")
            pooled = _pool_fwd(proj.reshape(nb, seq, d), tokw, name="pool_fwd").reshape(n, tokw)
            tok = _mm(pooled, pool_bd, "nn", name="pool_mix")
            scale = W["a_pool_scale"].reshape(1, tokw)
            s.update(pooled=pooled, mixed=tok, scale=scale)
        else:
            kvf = _mm(cur, kv_w, "nn", tn=kvp, name="kvf")
            kvf3 = kvf.reshape(nb, seq, kvp)
            gsum = _gate_fwd(kvf3, fb, gate_block, name="gate_fwd")[:, :, :heads]
            aq3, ak3 = _bias_lanes(gsum)
            proj = _mm(cur, w_in[l], "nn", name="proj1")
            o3, lse3 = _fox_fwd(proj.reshape(nb, seq, d), kvf3, aq3, ak3, tokw, name="fox_fwd")
            tok = o3.reshape(n, tokw)
            scale = ones_tok
            s.update(kvf3=kvf3, aq3=aq3, ak3=ak3, o3=o3, lse3=lse3)
        mixin = _memattn_fwd(tok, proj, memkv, scale, seq=seq, name=f"memattn_fwd{l}")
        mix = _mm(mixin, w_out[l], "nn", name=f"mix{l}")
        x1 = _ln_fwd(cur, mix, row(W["ln1_g"][l]), row(W["ln1_b"][l]), name=f"ln1_fwd{l}")
        up = _mm(x1, w_up, "nn", b_lead=l, tn=per_p, col_major=True, name=f"ffn_up{l}")
        act = _convgate_fwd(up.reshape(nb, seq, 2 * fp), conv_w[l], conv_b[l], name=f"convgate_fwd{l}")
        act = act.reshape(n, fp)
        ffn = _mm(act, w_down[l], "nn", tk=fp, name=f"ffn_down{l}")
        s.update(proj=proj, memkv=memkv, mixin=mixin, mix=mix, x1=x1, up=up, act=act, ffn=ffn)
        saved.append(s)
        if l + 1 < DEPTH:
            cur = _ln_fwd(x1, ffn, row(W["ln2_g"][l]), row(W["ln2_b"][l]), name=f"ln2_fwd{l}")

    G = {}
    ln_g = {k: [None] * DEPTH for k in ("ln1_g", "ln1_b", "ln2_g", "ln2_b")}
    stack = {k: [None] * DEPTH for k in ("mem_w_kv", "ffn_w_up", "ffn_conv_w", "ffn_conv_b", "ffn_w_down")}
    dx_terms = None
    loss = None
    for l in reversed(range(DEPTH)):
        s = saved[l]
        g2 = row(W["ln2_g"][l])
        if l == DEPTH - 1:
            dres, dffn, loss, dg, db = _final_ln_loss(s["x1"], s["ffn"], t2d, g2, row(W["ln2_b"][l]),
                                                      name="final_ln_loss")
        else:
            dres, dffn, dg, db = _ln_bwd(s["x1"], s["ffn"], g2, dx_terms, name=f"ln2_bwd{l}")
        ln_g["ln2_g"][l], ln_g["ln2_b"][l] = dg[0], db[0]
        dact = _mm(dffn, w_down[l], "nt", tn=fp, name=f"ffn_down_dx{l}")
        stack["ffn_w_down"][l] = _mm(s["act"], dffn, "tn", tm=per_p, tk=DW_ROWS, trim=("rows", per),
                                     name=f"ffn_down_dw{l}")
        du3, dg3, dcw, dcb = _convgate_bwd(s["up"].reshape(nb, seq, 2 * fp), dact.reshape(nb, seq, fp), conv_w[l],
                                           conv_b[l], name=f"convgate_bwd{l}")
        du, dgt = du3.reshape(n, fp), dg3.reshape(n, fp)
        dx1_u = _mm(du, w_up, "nt", b_lead=l, tk=fp, name=f"ffn_up_dx_u{l}")
        dx1_ffn = _mm(dgt, w_up, "nt", b_lead=l, tk=fp, b_col0=fp, adds=[dx1_u], name=f"ffn_up_dx_g{l}")
        stack["ffn_w_up"][l] = [_mm(s["x1"], part, "tn", tm=d, tn=per_p, tk=DW_ROWS, trim=("cols", per),
                                    name=f"ffn_up_dw_{nm}{l}") for nm, part in (("u", du), ("g", dgt))]
        stack["ffn_conv_w"][l] = dcw[:CONV_WIDTH]
        stack["ffn_conv_b"][l] = dcb[0]
        dres1, dmix, dg, db = _ln_bwd(s["x_in"], s["mix"], row(W["ln1_g"][l]), [dres, dx1_ffn], name=f"ln1_bwd{l}")
        ln_g["ln1_g"][l], ln_g["ln1_b"][l] = dg[0], db[0]
        dmixin = _mm(dmix, w_out[l], "nt", name=f"mix_dx{l}")
        d_w_out = _mm(s["mixin"], dmix, "tn", tm=d, tk=DW_ROWS, name=f"mix_dw{l}")
        if l == 0:
            G["a_w_out"] = d_w_out[None]
            dmixed, dscale = _scale_bwd(dmixin, s["mixed"], s["scale"], name="scale_bwd")
            G["a_pool_scale"] = dscale
            dpooled = _mm(dmixed, pool_bd, "nt", name="pool_mix_dx")
            dpw = _mm(s["pooled"], dmixed, "tn", tm=tokw, tk=DW_ROWS, name="pool_mix_dw")
            grp = tokw // len(POOL_WINDOWS)
            G["a_pool_w"] = jnp.stack([dpw[i * grp:(i + 1) * grp, i * grp:(i + 1) * grp]
                                       for i in range(len(POOL_WINDOWS))])[None]
            dtok = _pool_bwd(dpooled.reshape(nb, seq, tokw), name="pool_bwd").reshape(n, tokw)
            extra = []
        else:
            G["b_w_out"] = d_w_out[None]
            p3 = s["proj"].reshape(nb, seq, d)
            dm3 = dmixin.reshape(nb, seq, d)
            dq3, dk3, dv3, dfk = _fox_bwd(p3, s["kvf3"], s["aq3"], s["ak3"], s["o3"], dm3, s["lse3"], tokw,
                                          name="fox_bwd")
            dtok = dq3.reshape(n, tokw)
            dfk = jnp.swapaxes(dfk[:, :, :, 0:2, :] + dfk[:, :, :, 2:4, :], 2, 3).reshape(nb, heads, seq)
            dgsum = jnp.swapaxes(dfk, 1, 2)
            dgsum = jnp.pad(dgsum, ((0, 0), (0, 0), (0, LANE - heads)))
            df3, dfb = _gate_bwd(s["kvf3"], fb, dgsum, gate_block, heads, name="gate_bwd")
            G["f_b"] = dfb[0, :heads]
            dkvf = [(dk3.reshape(n, tokw), 0, "k"), (dv3.reshape(n, tokw), tokw, "v"),
                    (df3.reshape(n, LANE), 2 * tokw, "f")]
            dx_kv = []
            for part, col0, nm in dkvf:
                dx_kv = [_mm(part, kv_w, "nt", b_col0=col0, adds=dx_kv, name=f"kvf_dx_{nm}")]
            extra = dx_kv
            G["kv_w"] = jnp.concatenate([_mm(s["x_in"], part, "tn", tm=d, tk=DW_ROWS, name=f"kvf_dw_{nm}")
                                         for part, _, nm in dkvf], axis=1)[:, :kvw]
        dproj, dmemkv = _memattn_bwd(dmixin, dtok, s["proj"], s["memkv"], seq=seq, name=f"memattn_bwd{l}")
        stack["mem_w_kv"][l] = _mm(mem2d, dmemkv.reshape(nb * mlen, 2 * MEM_WIDTH), "tn", tm=d, tk=DW_ROWS,
                                   name=f"memkv_dw{l}")
        G["a_w_in" if l == 0 else "b_w_q"] = _mm(s["x_in"], dproj, "tn", tm=d, tk=DW_ROWS, name=f"proj_dw{l}")[None]
        if l == 0:
            grad_x = _mm(dproj, w_in[l], "nt", adds=[dres1], name="proj_dx0")
        else:
            dx_terms = [dres1, _mm(dproj, w_in[l], "nt", name="proj_dx1")] + extra
    for k, v in ln_g.items():
        G[k] = jnp.stack(v)
    G["mem_w_kv"] = jnp.stack(stack["mem_w_kv"])
    G["ffn_w_up"] = jnp.stack([jnp.concatenate(ug, axis=0) for ug in stack["ffn_w_up"]], axis=1)
    G["ffn_conv_w"] = jnp.stack([_unpad_cols(g, per, per_p) for g in stack["ffn_conv_w"]])
    G["ffn_conv_b"] = jnp.stack([_unpad_cols(g, per, per_p) for g in stack["ffn_conv_b"]])
    G["ffn_w_down"] = jnp.stack([g.reshape(N_CHIP, per // 2, d) for g in stack["ffn_w_down"]], axis=1)
    return loss[0, 0], grad_x.reshape(nb, seq, d), G


def kernel(x, mem, a_w_in, a_pool_w, a_pool_scale, a_w_out, b_w_q, b_w_out, kv_w, f_b, mem_w_kv, ln1_g, ln1_b, ln2_g, ln2_b, ffn_w_up, ffn_conv_w, ffn_conv_b, ffn_w_down, loss_target, m_a_w_in, m_a_pool_w, m_a_pool_scale, m_a_w_out, m_b_w_q, m_b_w_out, m_kv_w, m_f_b, m_mem_w_kv, m_ln1_g, m_ln1_b, m_ln2_g, m_ln2_b, m_ffn_w_up, m_ffn_conv_w, m_ffn_conv_b, m_ffn_w_down, v_a_w_in, v_a_pool_w, v_a_pool_scale, v_a_w_out, v_b_w_q, v_b_w_out, v_kv_w, v_f_b, v_mem_w_kv, v_ln1_g, v_ln1_b, v_ln2_g, v_ln2_b, v_ffn_w_up, v_ffn_conv_w, v_ffn_conv_b, v_ffn_w_down):
    w_loc = dict(a_w_in=a_w_in, a_pool_w=a_pool_w, a_pool_scale=a_pool_scale, a_w_out=a_w_out, b_w_q=b_w_q,
                 b_w_out=b_w_out, kv_w=kv_w, f_b=f_b, mem_w_kv=mem_w_kv, ln1_g=ln1_g, ln1_b=ln1_b, ln2_g=ln2_g,
                 ln2_b=ln2_b, ffn_w_up=ffn_w_up, ffn_conv_w=ffn_conv_w, ffn_conv_b=ffn_conv_b, ffn_w_down=ffn_w_down)
    m_loc = dict(a_w_in=m_a_w_in, a_pool_w=m_a_pool_w, a_pool_scale=m_a_pool_scale, a_w_out=m_a_w_out,
                 b_w_q=m_b_w_q, b_w_out=m_b_w_out, kv_w=m_kv_w, f_b=m_f_b, mem_w_kv=m_mem_w_kv, ln1_g=m_ln1_g,
                 ln1_b=m_ln1_b, ln2_g=m_ln2_g, ln2_b=m_ln2_b, ffn_w_up=m_ffn_w_up, ffn_conv_w=m_ffn_conv_w,
                 ffn_conv_b=m_ffn_conv_b, ffn_w_down=m_ffn_w_down)
    v_loc = dict(a_w_in=v_a_w_in, a_pool_w=v_a_pool_w, a_pool_scale=v_a_pool_scale, a_w_out=v_a_w_out,
                 b_w_q=v_b_w_q, b_w_out=v_b_w_out, kv_w=v_kv_w, f_b=v_f_b, mem_w_kv=v_mem_w_kv, ln1_g=v_ln1_g,
                 ln1_b=v_ln1_b, ln2_g=v_ln2_g, ln2_b=v_ln2_b, ffn_w_up=v_ffn_w_up, ffn_conv_w=v_ffn_conv_w,
                 ffn_conv_b=v_ffn_conv_b, ffn_w_down=v_ffn_w_down)
    x_i, y_i, c = lax.axis_index("x"), lax.axis_index("y"), lax.axis_index("c")
    q = 2 * x_i + y_i
    qc = jnp.stack([q, c]).astype(jnp.int32)
    shapes = {k: v.shape for k, v in w_loc.items()}

    def halves(a):
        if a.ndim == 3 and a.shape[0] == 2:
            return a
        rows = math.prod(a.shape[:-1])
        return a.reshape(2, rows // 2, a.shape[-1])

    own = {k: (w_loc[k] if k in GATHER_F32 else w_loc[k].astype(BF16)) for k in SHARDED}
    by_chip = ("a_pool_scale",) + tuple(k for k in BIG if k != "ffn_w_up")
    per = shapes["ffn_w_up"][-1]
    up_own = jnp.pad(own["ffn_w_up"], ((0, 0), (0, 0), (0, _round_up(per, LANE) - per)))
    gathered = _gather_shards([own["a_pool_scale"]], [halves(own[k]) for k in by_chip[1:]], [up_own],
                              name="gather_weights")
    W = {k: _from_shards(lax.dynamic_update_slice_in_dim(g.reshape((N_CHIP,) + shapes[k]), own[k][None], q, axis=0),
                         SHARD_AXIS[k])
         for k, g in zip(by_chip, gathered)}
    W["ffn_w_up_padded"] = gathered[-1]
    for k in REPLICATED:
        W[k] = w_loc[k]

    loss_part, grad_x, G = _local_step(x, mem, loss_target, W)

    g_chip = [G[k] if k in GRADS_BY_CHIP else _to_shards(G[k], SHARD_AXIS[k]) for k in BIG]
    g_chip = [g.reshape((N_CHIP,) + halves(w_loc[k]).shape) for k, g in zip(BIG, g_chip)]
    lay_r, tot_r = _layout(shapes, REPLICATED, PACK_COLS)
    rep_rows = tot_r // PACK_COLS
    rows = _round_up(rep_rows + 2, SUBLANE)
    scale_w = shapes["a_pool_scale"][-1]

    def small(rep, scale_row, scalar):
        lead = scale_row.shape[:-2]
        pad = [(0, 0)] * len(lead)
        rep = jnp.broadcast_to(_pack(rep, lay_r, tot_r).reshape(rep_rows, PACK_COLS), lead + (rep_rows, PACK_COLS))
        scale_row = jnp.pad(scale_row, pad + [(0, 0), (0, PACK_COLS - scale_w)])
        last = jnp.broadcast_to(jnp.pad(scalar.reshape(1, 1), [(0, rows - rep_rows - 2), (0, PACK_COLS - 1)]),
                                lead + (rows - rep_rows - 1, PACK_COLS))
        return jnp.concatenate([rep, scale_row, last], axis=-2)

    g_scale = jnp.repeat(_to_shards(G["a_pool_scale"], 1), 2, axis=0)
    zero = jnp.zeros((), F32)

    from_sib = _to_sibling(g_chip, name="grads_to_sibling")
    sums = [_pair_add(g, s, qc, name=f"pair_add_{k}") for k, g, s in zip(BIG, g_chip, from_sib)]
    *parts, sm_parts = _to_chips(sums, small(G, g_scale, loss_part), name="scatter_grads")
    grads = _swap_halves([_reduce_half(g, p, qc, name=f"reduce_{k}") for k, g, p in zip(BIG, sums, parts)],
                         name="swap_halves")
    out = {}
    for k, g in zip(BIG, grads):
        upd = _adamw_shard(g, halves(w_loc[k]), halves(m_loc[k]), halves(v_loc[k]), name=f"adamw_{k}")
        out[k] = [r.reshape(shapes[k]) for r in (g, *upd)]
    sm = _adamw(sm_parts, *[small(d, d["a_pool_scale"], zero) for d in (w_loc, m_loc, v_loc)], name="adamw_small")
    loss = sm[0, rep_rows + 1, 0]
    out_r = _unpack(sm[:, :rep_rows].reshape(4, tot_r), lay_r, lead=(4,))
    for k in REPLICATED:
        out[k] = [out_r[k][a] for a in range(4)]
    out["a_pool_scale"] = [sm[a, rep_rows:rep_rows + 1, :scale_w] for a in range(4)]

    outs = [loss, grad_x]
    for a in range(4):
        for k in WEIGHTS:
            outs.append(out[k][a])
    return tuple(outs)
```

```python
import functools
import math

import jax
import jax.numpy as jnp
from jax import lax
from jax.experimental import pallas as pl
from jax.experimental.pallas import tpu as pltpu

F32 = jnp.float32
BF16 = jnp.bfloat16

HEAD_DIM = 64
MEM_HEADS = 4
MEM_WIDTH = MEM_HEADS * HEAD_DIM
POOL_WINDOWS = (2, 4, 8, 16)
MAX_WINDOW = 16
CONV_WIDTH = 3
DEPTH = 2
DN_ALPHA = (2.0 * DEPTH) ** 0.25
LN_EPS = 1e-5
ATT_SCALE = HEAD_DIM ** -0.5
NEG_BIG = -1e30

ADAM_LR = 0.001
ADAM_B1 = 0.9
ADAM_B2 = 0.999
ADAM_EPS = 1e-08
ADAM_WD = 0.01
ADAM_STEP = 10

LANE = 128
SUBLANE = 8
PACK_COLS = 1024
DW_ROWS = 1024
VMEM_LIMIT = 56 * 1024 * 1024
N_DEV = 8
N_CHIP = 4
MESH_T = pl.DeviceIdType.MESH

SHARDED = ("a_w_in", "a_pool_scale", "a_w_out", "b_w_q", "b_w_out", "kv_w", "mem_w_kv", "ffn_w_up",
           "ffn_conv_w", "ffn_w_down")
SHARD_AXIS = {"a_w_in": 1, "a_pool_scale": 1, "a_w_out": 1, "b_w_q": 1, "b_w_out": 1, "kv_w": 1, "mem_w_kv": 1,
              "ffn_w_up": 2, "ffn_conv_w": 2, "ffn_w_down": 1}
GATHER_F32 = ("a_pool_scale", "ffn_conv_w")
BIG = tuple(k for k in SHARDED if k != "a_pool_scale")
GRADS_BY_CHIP = ("ffn_w_up", "ffn_w_down")
REPLICATED = ("a_pool_w", "f_b", "ln1_g", "ln1_b", "ln2_g", "ln2_b", "ffn_conv_b")
WEIGHTS = ("a_w_in", "a_pool_w", "a_pool_scale", "a_w_out", "b_w_q", "b_w_out", "kv_w", "f_b", "mem_w_kv",
           "ln1_g", "ln1_b", "ln2_g", "ln2_b", "ffn_w_up", "ffn_conv_w", "ffn_conv_b", "ffn_w_down")


def _round_up(n, m):
    return -(-n // m) * m


def _pick(dim, pref, unit=LANE):
    if dim <= pref:
        return dim
    t = (pref // unit) * unit
    while t >= unit:
        if dim % t == 0:
            return t
        t -= unit
    raise ValueError(f"no tile for {dim} <= {pref}")


def _params():
    return pltpu.CompilerParams(vmem_limit_bytes=VMEM_LIMIT)


_DIMS = {"nn": ((1,), (0,)), "nt": ((1,), (1,)), "tn": ((0,), (0,))}


def _bdot(a, b, mode):
    return lax.dot_general(a.astype(BF16), b.astype(BF16), (_DIMS[mode], ((), ())), preferred_element_type=F32)


def _mm(a, b, mode, *, name, tm=512, tn=1024, tk=2048, adds=(), b_col0=0, trim=None, col_major=False, b_lead=None):
    b_shape = b.shape if b_lead is None else b.shape[1:]
    if mode == "nn":
        (M, K), (K2, N) = a.shape, b_shape
    elif mode == "nt":
        (M, K), (N, K2) = a.shape, b_shape
        K2 = K if b_col0 + K <= K2 else -1
    else:
        (K, M), (K2, N) = a.shape, b_shape
    assert K == K2 and (mode == "nt" or b_col0 == 0), (name, a.shape, b.shape)
    tm, tn = _pick(M, tm, SUBLANE if mode != "tn" else LANE), _pick(N, tn)
    tk = _pick(K, tk, LANE if mode != "tn" else SUBLANE)
    nk = K // tk
    assert b_col0 % tk == 0, (name, b_col0, tk)
    koff = b_col0 // tk
    n_add = len(adds)

    def body(*refs):
        a_ref, b_ref = refs[0], refs[1]
        add_refs = refs[2:2 + n_add]
        o_ref, acc_ref = refs[2 + n_add], refs[3 + n_add]
        part = _bdot(a_ref[...], b_ref[...], mode)

        def finish(r):
            for ar in add_refs:
                r = r + ar[...]
            if trim is not None:
                r = r[:, :trim[1]] if trim[0] == "cols" else r[:trim[1], :]
            o_ref[...] = r

        if nk == 1:
            finish(part)
        else:
            k = pl.program_id(2)

            @pl.when(k == 0)
            def _():
                acc_ref[...] = part

            @pl.when(k > 0)
            def _():
                acc_ref[...] += part

            @pl.when(k == nk - 1)
            def _():
                finish(acc_ref[...])

    def spec(block, index):
        if col_major:
            return pl.BlockSpec(block, lambda j, i, k: index(i, j, k))
        return pl.BlockSpec(block, index)

    def b_spec_of(block, index):
        if b_lead is None:
            return spec(block, index)
        return spec((None,) + block, lambda i, j, k: (b_lead,) + index(i, j, k))

    if mode == "nn":
        a_spec = spec((tm, tk), lambda i, j, k: (i, k))
        b_spec = b_spec_of((tk, tn), lambda i, j, k: (k, j))
    elif mode == "nt":
        a_spec = spec((tm, tk), lambda i, j, k: (i, k))
        b_spec = b_spec_of((tn, tk), lambda i, j, k: (j, k + koff))
    else:
        a_spec = spec((tk, tm), lambda i, j, k: (k, i))
        b_spec = b_spec_of((tk, tn), lambda i, j, k: (k, j))
    o_spec = spec((tm, tn), lambda i, j, k: (i, j))
    out_spec, out_shape = o_spec, (M, N)
    if trim is not None and trim[0] == "cols":
        out_spec, out_shape = spec((None, tm, trim[1]), lambda i, j, k: (j, i, 0)), (N // tn, M, trim[1])
    elif trim is not None:
        out_spec, out_shape = spec((None, trim[1], tn), lambda i, j, k: (i, 0, j)), (M // tm, trim[1], N)
    acc_shape = (tm, tn) if nk > 1 else (SUBLANE, LANE)
    return pl.pallas_call(
        body, name=name, grid=(N // tn, M // tm, nk) if col_major else (M // tm, N // tn, nk),
        in_specs=[a_spec, b_spec] + [o_spec] * n_add, out_specs=out_spec,
        out_shape=jax.ShapeDtypeStruct(out_shape, F32),
        scratch_shapes=[pltpu.VMEM(acc_shape, F32)],
        compiler_params=_params(),
    )(a, b, *adds)


def _rowwise(fn, tiled, full, outs_tiled, outs_acc, *, rows, tile, name, acc_period=None):
    n_tiles = rows // tile
    period = n_tiles if acc_period is None else acc_period
    arrays, in_specs = [], []
    for t in tiled:
        arr, width, cb = t if isinstance(t, tuple) else (t, t.shape[1], 0)
        arrays.append(arr)
        in_specs.append(pl.BlockSpec((tile, width), lambda i, cb=cb: (i, cb)))
    for f in full:
        arr, spec = f if isinstance(f, tuple) else (f, None)
        arrays.append(arr)
        in_specs.append(spec if spec is not None else pl.BlockSpec(arr.shape, lambda i, nd=arr.ndim: (0,) * nd))
    out_shape, out_specs = [], []
    for width, dt in outs_tiled:
        out_shape.append(jax.ShapeDtypeStruct((rows, width), dt))
        out_specs.append(pl.BlockSpec((tile, width), lambda i: (i, 0)))
    for acc in outs_acc:
        shape, dt = acc[0], acc[1]
        out_shape.append(jax.ShapeDtypeStruct(shape, dt))
        out_specs.append(acc[2] if len(acc) > 2 else pl.BlockSpec(shape, lambda i, nd=len(shape): (0,) * nd))
    n_in, n_t, n_a = len(arrays), len(outs_tiled), len(outs_acc)

    def body(*refs):
        vals = [r[...] for r in refs[:n_in]]
        o_t, o_a = fn(*vals)
        for r, v in zip(refs[n_in:n_in + n_t], o_t):
            r[...] = v.astype(r.dtype)
        first = pl.program_id(0) % period == 0
        for r, v in zip(refs[n_in + n_t:n_in + n_t + n_a], o_a):
            v = v.reshape(r.shape)

            @pl.when(first)
            def _(r=r, v=v):
                r[...] = v

            @pl.when(jnp.logical_not(first))
            def _(r=r, v=v):
                r[...] += v

    return pl.pallas_call(
        body, name=name, grid=(n_tiles,), in_specs=in_specs, out_specs=out_specs, out_shape=out_shape,
        compiler_params=_params(),
    )(*arrays)


def _ln_stats(h):
    mu = jnp.mean(h, axis=-1, keepdims=True)
    d = h - mu
    var = jnp.mean(d * d, axis=-1, keepdims=True)
    rstd = lax.rsqrt(var + LN_EPS)
    return d * rstd, rstd


def _ln_bwd_math(h, g, dy):
    xhat, rstd = _ln_stats(h)
    dxhat = dy * g
    dh = rstd * (dxhat - jnp.mean(dxhat, axis=-1, keepdims=True)
                 - xhat * jnp.mean(dxhat * xhat, axis=-1, keepdims=True))
    return dh, jnp.sum(dy * xhat, axis=0, keepdims=True), jnp.sum(dy, axis=0, keepdims=True)


def _ln_fwd(x, r, g, b, *, name):
    n, d = x.shape

    def fn(x, r, g, b):
        xhat, _ = _ln_stats(DN_ALPHA * x + r)
        y = xhat * g + b
        return (y, y), ()

    return _rowwise(fn, [x, r], [g, b], [(d, F32), (d, BF16)], [], rows=n, tile=_pick(n, 512, SUBLANE), name=name)


def _ln_bwd(x, r, g, dys, *, name):
    n, d = x.shape
    n_dy = len(dys)

    def fn(x, r, *rest):
        dy = rest[0]
        for e in rest[1:n_dy]:
            dy = dy + e
        dh, dg, db = _ln_bwd_math(DN_ALPHA * x + r, rest[n_dy], dy)
        return (DN_ALPHA * dh, dh), (dg, db)

    return _rowwise(fn, [x, r, *dys], [g], [(d, F32), (d, BF16)], [((1, d), F32), ((1, d), F32)],
                    rows=n, tile=_pick(n, 256, SUBLANE), name=name)


def _final_ln_loss(x, r, target, g, b, *, name):
    n, d = x.shape

    def fn(x, r, t, g, b):
        h = DN_ALPHA * x + r
        xhat, _ = _ln_stats(h)
        err = xhat * g + b - t
        loss = jnp.full((1, LANE), 0.5 * jnp.sum(err * err) / d, F32)
        dh, dg, db = _ln_bwd_math(h, g, err / d)
        return (DN_ALPHA * dh, dh), (loss, dg, db)

    return _rowwise(fn, [x, r, target], [g, b], [(d, F32), (d, BF16)],
                    [((1, LANE), F32), ((1, d), F32), ((1, d), F32)],
                    rows=n, tile=_pick(n, 256, SUBLANE), name=name)


def _mem_heads(qm):
    lane = lax.broadcasted_iota(jnp.int32, (1, MEM_WIDTH), 1)
    for h in range(MEM_HEADS):
        msk = (lane >= h * HEAD_DIM) & (lane < (h + 1) * HEAD_DIM)
        yield msk, jnp.where(msk, qm, 0.0).astype(BF16)


def _mem_softmax(qh, k):
    s = _bdot(qh, k, "nt") * ATT_SCALE
    p = jnp.exp(s - jnp.max(s, axis=-1, keepdims=True))
    return p / jnp.sum(p, axis=-1, keepdims=True)


def _memattn_fwd(tok, proj, memkv, scale, *, seq, name):
    n, tokw = tok.shape
    d = tokw + MEM_WIDTH
    tile = _pick(seq, 512, SUBLANE)

    def fn(tok, qm, kv, scale):
        k, v = kv[:, :MEM_WIDTH].astype(BF16), kv[:, MEM_WIDTH:].astype(BF16)
        out = jnp.zeros(qm.shape, F32)
        for msk, qh in _mem_heads(qm):
            out = jnp.where(msk, _bdot(_mem_softmax(qh, k), v, "nn"), out)
        return (jnp.concatenate([tok * scale, out], axis=1),), ()

    kv_spec = pl.BlockSpec((None,) + memkv.shape[1:], lambda i: (i // (seq // tile), 0, 0))
    return _rowwise(fn, [tok, (proj, MEM_WIDTH, tokw // MEM_WIDTH)], [(memkv, kv_spec), scale], [(d, BF16)], [],
                    rows=n, tile=tile, name=name)[0]


def _memattn_bwd(dmixin, dtok, proj, memkv, *, seq, name):
    n, tokw = dtok.shape
    d = tokw + MEM_WIDTH
    tile = _pick(seq, 512, SUBLANE)

    def fn(dmo, dtok, qm, kv):
        k, v = kv[:, :MEM_WIDTH].astype(BF16), kv[:, MEM_WIDTH:].astype(BF16)
        dq = jnp.zeros(qm.shape, F32)
        dk = jnp.zeros(k.shape, F32)
        dv = jnp.zeros(v.shape, F32)
        for msk, qh in _mem_heads(qm):
            p = _mem_softmax(qh, k)
            doh = jnp.where(msk, dmo, 0.0).astype(BF16)
            dv = dv + _bdot(p, doh, "tn")
            dp = _bdot(doh, v, "nt")
            ds = (p * (dp - jnp.sum(dp * p, axis=-1, keepdims=True))).astype(BF16)
            dq = jnp.where(msk, _bdot(ds, k, "nn") * ATT_SCALE, dq)
            dk = dk + _bdot(ds, qh, "tn") * ATT_SCALE
        return (jnp.concatenate([dtok, dq], axis=1),), (jnp.concatenate([dk, dv], axis=1),)

    tpe = seq // tile
    kv_spec = pl.BlockSpec((None,) + memkv.shape[1:], lambda i: (i // tpe, 0, 0))
    return _rowwise(fn, [(dmixin, MEM_WIDTH, tokw // MEM_WIDTH), dtok, (proj, MEM_WIDTH, tokw // MEM_WIDTH)],
                    [(memkv, kv_spec)], [(d, BF16)], [(memkv.shape, F32, kv_spec)],
                    rows=n, tile=tile, name=name, acc_period=tpe)


def _scale_bwd(dmixin, mixed, scale, *, name):
    n, tokw = mixed.shape

    def fn(dt, mixed, scale):
        return (dt * scale,), (jnp.sum(dt * mixed, axis=0, keepdims=True),)

    return _rowwise(fn, [(dmixin, tokw, 0), mixed], [scale], [(tokw, BF16)], [((1, tokw), F32)],
                    rows=n, tile=_pick(n, 512, SUBLANE), name=name)


def _chunk_rows(seq):
    return _pick(seq, 512, SUBLANE)


def _load_ext(ref, c, rows, before, after, seq):
    lo, hi = c * rows - before, (c + 1) * rows + after
    parts = []
    if lo < 0:
        parts.append(jnp.zeros((-lo, ref.shape[1]), F32))
    parts.append(ref[max(lo, 0):min(hi, seq), :])
    if hi > seq:
        parts.append(jnp.zeros((hi - seq, ref.shape[1]), F32))
    return parts[0] if len(parts) == 1 else jnp.concatenate(parts, axis=0)


def _down(x, k):
    return pltpu.roll(x, k, 0)


def _up(x, k):
    return pltpu.roll(x, x.shape[0] - k, 0)


def _window_sums(ext, shift, col0, group):
    lane = col0 + lax.broadcasted_iota(jnp.int32, (1, ext.shape[1]), 1)
    gidx = lane // group
    s = ext
    out = None
    k = 1
    for gi, w in enumerate(POOL_WINDOWS):
        while k < w:
            s = s + shift(s, k)
            k *= 2
        out = s if out is None else jnp.where(gidx >= gi, s, out)
    return out, jnp.left_shift(2, jnp.minimum(gidx, len(POOL_WINDOWS) - 1))


def _pool_fwd(proj3, tokw, *, name):
    nb, seq, _ = proj3.shape
    rows = _chunk_rows(seq)
    group = tokw // len(POOL_WINDOWS)

    def body(u_ref, o_ref):
        col0 = pl.program_id(1) * LANE
        for c in range(seq // rows):
            ext = _load_ext(u_ref, c, rows, MAX_WINDOW, 0, seq)
            sums, win = _window_sums(ext, _down, col0, group)
            t = c * rows + lax.broadcasted_iota(jnp.int32, (rows, 1), 0)
            count = jnp.minimum(t + 1, win).astype(F32)
            o_ref[c * rows:(c + 1) * rows, :] = (sums[MAX_WINDOW:, :] / count - ext[MAX_WINDOW:, :]).astype(BF16)

    spec = pl.BlockSpec((None, seq, LANE), lambda b, j: (b, 0, j))
    return pl.pallas_call(
        body, name=name, grid=(nb, tokw // LANE), in_specs=[spec], out_specs=spec,
        out_shape=jax.ShapeDtypeStruct((nb, seq, tokw), BF16), compiler_params=_params(),
    )(proj3)


def _pool_bwd(dp3, *, name):
    nb, seq, tokw = dp3.shape
    rows = _chunk_rows(seq)
    group = tokw // len(POOL_WINDOWS)

    def body(d_ref, o_ref):
        col0 = pl.program_id(1) * LANE
        for c in range(seq // rows):
            ext = _load_ext(d_ref, c, rows, 0, MAX_WINDOW, seq)
            lane = col0 + lax.broadcasted_iota(jnp.int32, (1, LANE), 1)
            win = jnp.left_shift(2, jnp.minimum(lane // group, len(POOL_WINDOWS) - 1))
            t = c * rows + lax.broadcasted_iota(jnp.int32, (rows + MAX_WINDOW, 1), 0)
            scaled = ext / jnp.minimum(t + 1, win).astype(F32)
            sums, _ = _window_sums(scaled, _up, col0, group)
            o_ref[c * rows:(c + 1) * rows, :] = sums[:rows, :] - ext[:rows, :]

    spec = pl.BlockSpec((None, seq, LANE), lambda b, j: (b, 0, j))
    return pl.pallas_call(
        body, name=name, grid=(nb, tokw // LANE), in_specs=[spec], out_specs=spec,
        out_shape=jax.ShapeDtypeStruct((nb, seq, tokw), F32), compiler_params=_params(),
    )(dp3)


def _conv3(ext, w_ref, b_ref):
    x1, x2 = _down(ext, 1), _down(ext, 2)
    return w_ref[0:1, :] * x2 + w_ref[1:2, :] * x1 + w_ref[2:3, :] * ext + b_ref[...], x1, x2


def _convgate_fwd(up3, cw, cb, *, name):
    nb, seq, c2 = up3.shape
    fp = c2 // 2
    nblk = fp // LANE
    rows = _chunk_rows(seq)

    def body(u_ref, g_ref, wu_ref, wg_ref, bu_ref, bg_ref, o_ref):
        for c in range(seq // rows):
            hu, _, _ = _conv3(_load_ext(u_ref, c, rows, SUBLANE, 0, seq), wu_ref, bu_ref)
            hg, _, _ = _conv3(_load_ext(g_ref, c, rows, SUBLANE, 0, seq), wg_ref, bg_ref)
            o_ref[c * rows:(c + 1) * rows, :] = (hg * jax.nn.sigmoid(hg) * hu)[SUBLANE:, :].astype(BF16)

    def col(off, r):
        return pl.BlockSpec((r, LANE), lambda b, j: (0, j + off))

    def act(off):
        return pl.BlockSpec((None, seq, LANE), lambda b, j: (b, 0, j + off))

    return pl.pallas_call(
        body, name=name, grid=(nb, nblk),
        in_specs=[act(0), act(nblk), col(0, SUBLANE), col(nblk, SUBLANE), col(0, 1), col(nblk, 1)],
        out_specs=act(0), out_shape=jax.ShapeDtypeStruct((nb, seq, fp), BF16), compiler_params=_params(),
    )(up3, up3, cw, cw, cb, cb)


def _convgate_bwd(up3, dact3, cw, cb, *, name):
    nb, seq, c2 = up3.shape
    fp = c2 // 2
    nblk = fp // LANE
    rows = _chunk_rows(seq)
    h = SUBLANE

    def body(u_ref, g_ref, da_ref, wu_ref, wg_ref, bu_ref, bg_ref, du_ref, dg_ref, dwu_ref, dwg_ref, dbu_ref,
             dbg_ref):
        @pl.when(pl.program_id(1) == 0)
        def _():
            for r in (dwu_ref, dwg_ref, dbu_ref, dbg_ref):
                r[...] = jnp.zeros(r.shape, F32)

        for c in range(seq // rows):
            eu = _load_ext(u_ref, c, rows, h, h, seq)
            eg = _load_ext(g_ref, c, rows, h, h, seq)
            da = _load_ext(da_ref, c, rows, h, h, seq)
            hu, u1, u2 = _conv3(eu, wu_ref, bu_ref)
            hg, g1, g2 = _conv3(eg, wg_ref, bg_ref)
            sig = jax.nn.sigmoid(hg)
            dhu = da * hg * sig
            dhg = da * hu * sig * (1.0 + hg * (1.0 - sig))
            for dh, w_ref, x0, x1, x2, dx_ref, dw_ref, db_ref in (
                    (dhu, wu_ref, eu, u1, u2, du_ref, dwu_ref, dbu_ref),
                    (dhg, wg_ref, eg, g1, g2, dg_ref, dwg_ref, dbg_ref)):
                dx = w_ref[2:3, :] * dh + w_ref[1:2, :] * _up(dh, 1) + w_ref[0:1, :] * _up(dh, 2)
                dx_ref[c * rows:(c + 1) * rows, :] = dx[h:h + rows, :].astype(BF16)
                core = dh[h:h + rows, :]
                for k, xk in ((0, x2), (1, x1), (2, x0)):
                    dw_ref[k:k + 1, :] += jnp.sum(core * xk[h:h + rows, :], axis=0, keepdims=True)
                db_ref[...] += jnp.sum(core, axis=0, keepdims=True)

    def col(off, r):
        return pl.BlockSpec((r, LANE), lambda j, b: (0, j + off))

    def act(off):
        return pl.BlockSpec((None, seq, LANE), lambda j, b: (b, 0, j + off))

    du, dg, dwu, dwg, dbu, dbg = pl.pallas_call(
        body, name=name, grid=(nblk, nb),
        in_specs=[act(0), act(nblk), act(0), col(0, SUBLANE), col(nblk, SUBLANE), col(0, 1), col(nblk, 1)],
        out_specs=[act(0), act(0), col(0, SUBLANE), col(0, SUBLANE), col(0, 1), col(0, 1)],
        out_shape=[jax.ShapeDtypeStruct((nb, seq, fp), BF16), jax.ShapeDtypeStruct((nb, seq, fp), BF16),
                   jax.ShapeDtypeStruct((SUBLANE, fp), F32), jax.ShapeDtypeStruct((SUBLANE, fp), F32),
                   jax.ShapeDtypeStruct((1, fp), F32), jax.ShapeDtypeStruct((1, fp), F32)],
        compiler_params=_params(),
    )(up3, up3, dact3, cw, cw, cb, cb)
    return du, dg, jnp.concatenate([dwu, dwg], axis=1), jnp.concatenate([dbu, dbg], axis=1)


def _scan_rows(x, shift, valid):
    row = lax.broadcasted_iota(jnp.int32, (x.shape[0], 1), 0)
    k = 1
    while k < x.shape[0]:
        x = x + jnp.where(valid(row, k), shift(x, k), 0.0)
        k *= 2
    return x


def _pick_row(x, r):
    row = lax.broadcasted_iota(jnp.int32, (x.shape[0], 1), 0)
    return jnp.sum(jnp.where(row == r, x, 0.0), axis=0, keepdims=True)


def _log_sigmoid(z):
    return jnp.minimum(z, 0.0) - jnp.log(1.0 + jnp.exp(-jnp.abs(z)))


def _gate_fwd(kvf3, fb, col_block, *, name):
    nb, seq, _ = kvf3.shape
    rows = _chunk_rows(seq)

    def body(f_ref, fb_ref, o_ref):
        carry = jnp.zeros((1, LANE), F32)
        for c in range(seq // rows):
            logf = _log_sigmoid(f_ref[c * rows:(c + 1) * rows, :] + fb_ref[...])
            run = _scan_rows(logf, _down, lambda row, k: row >= k) + carry
            o_ref[c * rows:(c + 1) * rows, :] = run
            carry = _pick_row(run, rows - 1)

    return pl.pallas_call(
        body, name=name, grid=(nb,),
        in_specs=[pl.BlockSpec((None, seq, LANE), lambda b: (b, 0, col_block)),
                  pl.BlockSpec((1, LANE), lambda b: (0, 0))],
        out_specs=pl.BlockSpec((None, seq, LANE), lambda b: (b, 0, 0)),
        out_shape=jax.ShapeDtypeStruct((nb, seq, LANE), F32), compiler_params=_params(),
    )(kvf3, fb)


def _gate_bwd(kvf3, fb, dF3, col_block, heads, *, name):
    nb, seq, _ = kvf3.shape
    rows = _chunk_rows(seq)

    def body(f_ref, fb_ref, d_ref, o_ref, dfb_ref):
        @pl.when(pl.program_id(0) == 0)
        def _():
            dfb_ref[...] = jnp.zeros(dfb_ref.shape, F32)

        lane = lax.broadcasted_iota(jnp.int32, (1, LANE), 1)
        carry = jnp.zeros((1, LANE), F32)
        for c in reversed(range(seq // rows)):
            run = _scan_rows(d_ref[c * rows:(c + 1) * rows, :], _up, lambda row, k: row < rows - k) + carry
            carry = _pick_row(run, 0)
            z = f_ref[c * rows:(c + 1) * rows, :] + fb_ref[...]
            df = jnp.where(lane < heads, run * jax.nn.sigmoid(-z), 0.0)
            o_ref[c * rows:(c + 1) * rows, :] = df
            dfb_ref[...] += jnp.sum(df, axis=0, keepdims=True)

    return pl.pallas_call(
        body, name=name, grid=(nb,),
        in_specs=[pl.BlockSpec((None, seq, LANE), lambda b: (b, 0, col_block)),
                  pl.BlockSpec((1, LANE), lambda b: (0, 0)),
                  pl.BlockSpec((None, seq, LANE), lambda b: (b, 0, 0))],
        out_specs=[pl.BlockSpec((None, seq, LANE), lambda b: (b, 0, 0)), pl.BlockSpec((1, LANE), lambda b: (0, 0))],
        out_shape=[jax.ShapeDtypeStruct((nb, seq, LANE), F32), jax.ShapeDtypeStruct((1, LANE), F32)],
        compiler_params=_params(),
    )(kvf3, fb, dF3)


def _head_masks():
    lane = lax.broadcasted_iota(jnp.int32, (1, LANE), 1)
    return (lane < HEAD_DIM, lane >= HEAD_DIM)


BIAS_TERMS = 3


def _bias_lanes(gsum):
    nb, seq, heads = gsum.shape
    terms, rest = [], gsum
    for _ in range(BIAS_TERMS):
        t = lax.reduce_precision(rest, exponent_bits=8, mantissa_bits=7)
        terms.append(t)
        rest = rest - t
    ones = [jnp.ones_like(gsum)] * BIAS_TERMS

    def lanes(parts):
        z = jnp.stack(parts, axis=-1)
        z = jnp.pad(z, ((0, 0), (0, 0), (0, 0), (0, HEAD_DIM - 2 * BIAS_TERMS)))
        z = z.reshape(nb, seq, heads // 2, 2, HEAD_DIM)[:, :, :, ::-1]
        return z.reshape(nb, seq, heads * HEAD_DIM).astype(BF16)

    return lanes(terms + ones), lanes(ones + [-t for t in terms])


def _fox_scores(q, k, aq, ak, masked):
    qs = (q * ATT_SCALE).astype(BF16)
    qts = [jnp.where(msk, qs, aq) for msk in _head_masks()]
    ss = [_bdot(qt, jnp.where(msk, k, ak), "nt") for qt, msk in zip(qts, _head_masks())]
    if masked:
        t = q.shape[0]
        keep = lax.broadcasted_iota(jnp.int32, (t, t), 0) >= lax.broadcasted_iota(jnp.int32, (t, t), 1)
        ss = [jnp.where(keep, s, NEG_BIG) for s in ss]
    return ss, qts


def _on_blocks(qi, ki, step):
    @pl.when(ki < qi)
    def _():
        step(False)

    @pl.when(ki == qi)
    def _():
        step(True)


def _fox_grid(nblk, tokw, t, q_major):
    if q_major:
        pairs = [(qi, ki) for qi in range(nblk) for ki in range(qi + 1)]
    else:
        pairs = [(qi, ki) for ki in range(nblk) for qi in range(ki, nblk)]
    tables = [jnp.array([p[i] for p in pairs], jnp.int32) for i in (0, 1)]

    def q_spec(off=0, wide=False):
        width = 2 * LANE if wide else LANE
        return pl.BlockSpec((None, t, width), lambda b, p, i, qt, kt: (b, qt[i], p + off))

    def k_spec(off=0):
        return pl.BlockSpec((None, t, LANE), lambda b, p, i, qt, kt: (b, kt[i], p + off))

    return tables, len(pairs), q_spec, k_spec, tokw // LANE


def _lanes(col):
    return jnp.broadcast_to(col, (col.shape[0], LANE))


def _across(stat, width):
    return jnp.tile(stat, (1, width // LANE))


def _fox_fwd(proj3, kvf3, aq3, ak3, tokw, *, name):
    nb, seq, _ = proj3.shape
    t = _pick(seq, 512, LANE)
    tables, n_pairs, q_spec, k_spec, hp0 = _fox_grid(seq // t, tokw, t, True)

    def body(qt_ref, kt_ref, q_ref, k_ref, v_ref, aq_ref, ak_ref, o_ref, lse_ref, m_s, l_s, acc_s):
        i = pl.program_id(2)
        qi, ki = qt_ref[i], kt_ref[i]

        @pl.when(ki == 0)
        def _():
            m_s[...] = jnp.full(m_s.shape, NEG_BIG, F32)
            l_s[...] = jnp.zeros(l_s.shape, F32)
            acc_s[...] = jnp.zeros(acc_s.shape, F32)

        def step(masked):
            v = v_ref[...].astype(BF16)
            ss, _ = _fox_scores(q_ref[...], k_ref[...].astype(BF16), aq_ref[...], ak_ref[...], masked)
            for h, s in enumerate(ss):
                m_old = m_s[h]
                m_new = jnp.maximum(m_old, _lanes(jnp.max(s, axis=-1, keepdims=True)))
                alpha = jnp.exp(m_old - m_new)
                p = jnp.exp(s - _across(m_new, t))
                l_s[h] = alpha * l_s[h] + _lanes(jnp.sum(p, axis=-1, keepdims=True))
                acc_s[h] = alpha * acc_s[h] + _bdot(p, v, "nn")
                m_s[h] = m_new

        _on_blocks(qi, ki, step)

        @pl.when(ki == qi)
        def _():
            o_ref[...] = jnp.where(_head_masks()[0], acc_s[0] / l_s[0], acc_s[1] / l_s[1])
            lse_ref[...] = jnp.concatenate([m_s[0] + jnp.log(l_s[0]), m_s[1] + jnp.log(l_s[1])], axis=1)

    stat = pltpu.VMEM((2, t, LANE), F32)
    return pl.pallas_call(
        body, name=name,
        grid_spec=pltpu.PrefetchScalarGridSpec(
            num_scalar_prefetch=2, grid=(nb, hp0, n_pairs),
            in_specs=[q_spec(), k_spec(), k_spec(hp0), q_spec(), k_spec()],
            out_specs=[q_spec(), q_spec(wide=True)], scratch_shapes=[stat, stat, stat]),
        out_shape=[jax.ShapeDtypeStruct((nb, seq, tokw), F32), jax.ShapeDtypeStruct((nb, seq, 2 * tokw), F32)],
        compiler_params=_params(),
    )(*tables, proj3, kvf3, kvf3, aq3, ak3)


def _fox_bwd_common(q_ref, k_ref, v_ref, aq_ref, ak_ref, do_ref, lse_ref, delta_ref, masked):
    k, v = k_ref[...].astype(BF16), v_ref[...].astype(BF16)
    ss, qts = _fox_scores(q_ref[...], k, aq_ref[...], ak_ref[...], masked)
    do = do_ref[...]
    t = do.shape[0]
    out = []
    for h, (s, qt, msk) in enumerate(zip(ss, qts, _head_masks())):
        doh = jnp.where(msk, do, 0.0).astype(BF16)
        p = jnp.exp(s - _across(lse_ref[:, h * LANE:(h + 1) * LANE], t))
        ds = p * (_bdot(doh, v, "nt") - _across(delta_ref[:, h * LANE:(h + 1) * LANE], t))
        out.append((qt, doh, p, ds))
    return out, k


def _fox_bwd(proj3, kvf3, aq3, ak3, o3, dmixin3, lse3, tokw, *, name):
    nb, seq, _ = proj3.shape
    t = _pick(seq, 512, LANE)
    nblk = seq // t
    tables, n_pairs, q_spec, k_spec, hp0 = _fox_grid(nblk, tokw, t, True)
    whole = pl.BlockSpec((None, seq, LANE), lambda b, p, i, qt, kt: (b, 0, p))
    dfk_spec = pl.BlockSpec((None, None, nblk, SUBLANE, t), lambda b, p, i, qt, kt: (b, p, 0, 0, 0))

    def body(qt_ref, kt_ref, q_ref, k_ref, v_ref, aq_ref, ak_ref, o_ref, do_ref, lse_ref, dq_ref, dk_ref, dv_ref,
             dfk_ref, acc_s, row_s, delta_s):
        i = pl.program_id(2)
        qi, ki = qt_ref[i], kt_ref[i]

        @pl.when(i == 0)
        def _():
            dk_ref[...] = jnp.zeros(dk_ref.shape, F32)
            dv_ref[...] = jnp.zeros(dv_ref.shape, F32)
            dfk_ref[...] = jnp.zeros(dfk_ref.shape, F32)

        @pl.when(ki == 0)
        def _():
            acc_s[...] = jnp.zeros(acc_s.shape, F32)
            row_s[...] = jnp.zeros(row_s.shape, F32)
            prod = do_ref[...] * o_ref[...]
            delta_s[...] = jnp.concatenate(
                [_lanes(jnp.sum(jnp.where(msk, prod, 0.0), axis=-1, keepdims=True)) for msk in _head_masks()],
                axis=1)

        def step(masked):
            heads, k = _fox_bwd_common(q_ref, k_ref, v_ref, aq_ref, ak_ref, do_ref, lse_ref, delta_s, masked)
            rows = pl.ds(pl.multiple_of(ki * t, t), t)
            for h, ((qt, doh, p, ds), msk) in enumerate(zip(heads, _head_masks())):
                acc_s[h] += _bdot(ds, k, "nn")
                row_s[h] += _lanes(jnp.sum(ds, axis=-1, keepdims=True))
                dv_ref[rows, :] += _bdot(p, doh, "tn")
                dk_ref[rows, :] += jnp.where(msk, _bdot(ds, qt, "tn"), 0.0)
                dfk_ref[ki, h:h + 1, :] -= jnp.sum(ds, axis=0, keepdims=True)

        _on_blocks(qi, ki, step)

        @pl.when(ki == qi)
        def _():
            dq_ref[...] = jnp.where(_head_masks()[0], acc_s[0], acc_s[1]) * ATT_SCALE
            for h in range(2):
                dfk_ref[qi, 2 + h:3 + h, :] = row_s[h].T[0:1, :]

    out = jax.ShapeDtypeStruct((nb, seq, tokw), F32)
    stat = pltpu.VMEM((2, t, LANE), F32)
    return pl.pallas_call(
        body, name=name,
        grid_spec=pltpu.PrefetchScalarGridSpec(
            num_scalar_prefetch=2, grid=(nb, hp0, n_pairs),
            in_specs=[q_spec(), k_spec(), k_spec(hp0), q_spec(), k_spec(), q_spec(), q_spec(), q_spec(wide=True)],
            out_specs=[q_spec(), whole, whole, dfk_spec],
            scratch_shapes=[stat, stat, pltpu.VMEM((t, 2 * LANE), F32)]),
        out_shape=[out, out, out, jax.ShapeDtypeStruct((nb, hp0, nblk, SUBLANE, t), F32)],
        compiler_params=_params(),
    )(*tables, proj3, kvf3, kvf3, aq3, ak3, o3, dmixin3, lse3)


def _peer(k):
    x, y, c = lax.axis_index("x"), lax.axis_index("y"), lax.axis_index("c")
    return (1 - x if k & 4 else x, 1 - y if k & 2 else y, 1 - c if k & 1 else c)


def _dev_index(p):
    return 4 * p[0] + 2 * p[1] + p[2]


_HBM = pl.BlockSpec(memory_space=pltpu.HBM)
CHIP_RELATIONS = (2, 4, 6)


def _chip_index(p):
    return 2 * p[0] + p[1]


def _run_copies(sends, recvs):
    for cp in sends:
        cp.start()
    for cp in recvs:
        cp.wait_recv()
    for cp in sends:
        cp.wait_send()


def _gather_shards(whole, halved, side_by_side, *, name):
    nw, nh = len(whole), len(halved) + len(side_by_side)
    n = nw + nh
    n_sem = 3 * nw + 6 * nh

    def body(*refs):
        ins, outs, send_sems, recv_sems, local_sems = refs[:n], refs[n:2 * n], refs[2 * n], refs[2 * n + 1], refs[-1]
        me, sib = _peer(0), _peer(1)
        q, c = _chip_index(me), me[2]
        own = []

        def copy(src, dst, s, to):
            return pltpu.make_async_remote_copy(src_ref=src, dst_ref=dst, send_sem=send_sems.at[s],
                                                recv_sem=recv_sems.at[s], device_id=to, device_id_type=MESH_T)

        sends, recvs, passes = [], [], []
        for j, k in enumerate(CHIP_RELATIONS):
            peer = _peer(k)
            pq = _chip_index(peer)
            for i in range(nw):
                sends.append(copy(ins[i], outs[i].at[q], 3 * i + j, peer))
                recvs.append(copy(ins[i], outs[i].at[pq], 3 * i + j, peer))
            for i in range(nh):
                src, out, s = ins[nw + i], outs[nw + i], 3 * nw + 6 * i + j
                if i < len(halved):
                    place = lambda chip, half, out=out: out.at[chip, half]
                else:
                    cols = src.shape[-1]
                    place = lambda chip, half, out=out, cols=cols: out.at[
                        half, :, pl.ds(pl.multiple_of(chip * cols, LANE), cols)]
                    if j == 0:
                        own += [pltpu.make_async_copy(src.at[h], place(q, h), local_sems.at[len(own) + h])
                                for h in range(2)]
                sends.append(copy(src.at[c], place(q, c), s, peer))
                passes.append((copy(src.at[c], place(pq, c), s, peer), copy(place(pq, c), place(pq, c), s + 3, sib),
                               copy(place(pq, c), place(pq, 1 - c), s + 3, sib)))
        for cp in sends + own:
            cp.start()
        for arrival, hand_over, _ in passes:
            arrival.wait_recv()
            hand_over.start()
        for cp in recvs:
            cp.wait_recv()
        for _, _, from_sibling in passes:
            from_sibling.wait_recv()
        for cp in sends + [hand_over for _, hand_over, _ in passes]:
            cp.wait_send()
        for cp in own:
            cp.wait()

    arrays = list(whole) + list(halved) + list(side_by_side)
    return pl.pallas_call(
        body, name=name, in_specs=[_HBM] * n, out_specs=[_HBM] * n,
        out_shape=[jax.ShapeDtypeStruct((N_CHIP,) + a.shape, a.dtype) for a in list(whole) + list(halved)]
        + [jax.ShapeDtypeStruct(a.shape[:-1] + (N_CHIP * a.shape[-1],), a.dtype) for a in side_by_side],
        scratch_shapes=[pltpu.SemaphoreType.DMA((n_sem,)), pltpu.SemaphoreType.DMA((n_sem,)),
                        pltpu.SemaphoreType.DMA((2 * len(side_by_side),))],
    )(*arrays)


def _to_sibling(grads, *, name):
    n = len(grads)

    def body(*refs):
        ins, outs, send_sems, recv_sems = refs[:n], refs[n:2 * n], refs[2 * n], refs[2 * n + 1]
        c = lax.axis_index("c")
        sends = [pltpu.make_async_remote_copy(src_ref=ins[i].at[:, 1 - c], dst_ref=outs[i], send_sem=send_sems.at[i],
                                              recv_sem=recv_sems.at[i], device_id=_peer(1), device_id_type=MESH_T)
                 for i in range(n)]
        _run_copies(sends, sends)

    return pl.pallas_call(
        body, name=name, in_specs=[_HBM] * n, out_specs=[_HBM] * n,
        out_shape=[jax.ShapeDtypeStruct(g.shape[:1] + g.shape[2:], g.dtype) for g in grads],
        scratch_shapes=[pltpu.SemaphoreType.DMA((n,)), pltpu.SemaphoreType.DMA((n,))],
    )(*grads)


def _pair_add(grads, from_sibling, qc, *, name):
    _, _, rows, cols = grads.shape
    tile = _pick(rows, 1024, SUBLANE)

    def body(qc_ref, g_ref, s_ref, o_ref):
        del qc_ref
        o_ref[...] = g_ref[...] + s_ref[...]

    spec = pl.BlockSpec((None, tile, cols), lambda j, i, qc: (j, i, 0))
    return pl.pallas_call(
        body, name=name,
        grid_spec=pltpu.PrefetchScalarGridSpec(
            num_scalar_prefetch=1, grid=(N_CHIP, rows // tile),
            in_specs=[pl.BlockSpec((None, None, tile, cols), lambda j, i, qc: (j, qc[1], i, 0)), spec],
            out_specs=spec),
        out_shape=jax.ShapeDtypeStruct((N_CHIP, rows, cols), F32), compiler_params=_params(),
    )(qc, grads, from_sibling)


def _to_chips(sums, small, *, name):
    n = len(sums)

    def body(*refs):
        ins, small_ref, outs, small_out = refs[:n], refs[n], refs[n + 1:2 * n + 1], refs[2 * n + 1]
        send_sems, recv_sems, local_sem = refs[2 * n + 2:]
        me = _peer(0)

        def copy(src, dst, s, to):
            return pltpu.make_async_remote_copy(src_ref=src, dst_ref=dst, send_sem=send_sems.at[s],
                                                recv_sem=recv_sems.at[s], device_id=to, device_id_type=MESH_T)

        mine = pltpu.make_async_copy(small_ref.at[_dev_index(me)], small_out.at[_dev_index(me)], local_sem)
        mine.start()
        sends, recvs = [], []
        for j, k in enumerate(CHIP_RELATIONS):
            peer = _peer(k)
            for i in range(n):
                src = ins[i].at[_chip_index(peer)]
                sends.append(copy(src, outs[i].at[j], 3 * i + j, peer))
                recvs.append(copy(src, outs[i].at[j], 3 * i + j, peer))
        for k in range(1, N_DEV):
            peer = _peer(k)
            src = small_ref.at[_dev_index(peer)]
            sends.append(copy(src, small_out.at[_dev_index(me)], 3 * n + k - 1, peer))
            recvs.append(copy(src, small_out.at[_dev_index(peer)], 3 * n + k - 1, peer))
        _run_copies(sends, recvs)
        mine.wait()

    n_sem = 3 * n + N_DEV - 1
    return pl.pallas_call(
        body, name=name, in_specs=[_HBM] * (n + 1), out_specs=[_HBM] * (n + 1),
        out_shape=[jax.ShapeDtypeStruct((3,) + g.shape[1:], g.dtype) for g in sums]
        + [jax.ShapeDtypeStruct(small.shape, small.dtype)],
        scratch_shapes=[pltpu.SemaphoreType.DMA((n_sem,)), pltpu.SemaphoreType.DMA((n_sem,)),
                        pltpu.SemaphoreType.DMA],
    )(*sums, small)


def _swap_halves(arrays, *, name):
    n = len(arrays)

    def body(*refs):
        outs, send_sems, recv_sems = refs[n:2 * n], refs[2 * n], refs[2 * n + 1]
        c = lax.axis_index("c")
        sib = _peer(1)
        sends, recvs = [], []
        for i in range(n):
            sem = dict(send_sem=send_sems.at[i], recv_sem=recv_sems.at[i], device_id=sib, device_id_type=MESH_T)
            sends.append(pltpu.make_async_remote_copy(src_ref=outs[i].at[c], dst_ref=outs[i].at[c], **sem))
            recvs.append(pltpu.make_async_remote_copy(src_ref=outs[i].at[c], dst_ref=outs[i].at[1 - c], **sem))
        _run_copies(sends, recvs)

    return pl.pallas_call(
        body, name=name, in_specs=[_HBM] * n, out_specs=[_HBM] * n,
        out_shape=[jax.ShapeDtypeStruct(a.shape, a.dtype) for a in arrays],
        input_output_aliases={i: i for i in range(n)},
        scratch_shapes=[pltpu.SemaphoreType.DMA((n,)), pltpu.SemaphoreType.DMA((n,))],
    )(*arrays)


def _adam_math(g, w, m, v):
    bc1 = 1.0 - ADAM_B1 ** ADAM_STEP
    bc2 = 1.0 - ADAM_B2 ** ADAM_STEP
    m_new = ADAM_B1 * m + (1.0 - ADAM_B1) * g
    v_new = ADAM_B2 * v + (1.0 - ADAM_B2) * (g * g)
    delta = -ADAM_LR * ((m_new / bc1) / (jnp.sqrt(v_new / bc2) + ADAM_EPS) + ADAM_WD * w)
    return delta, m_new, v_new


def _reduce_half(sums, parts, qc, *, name):
    _, rows, cols = sums.shape
    tile = _pick(rows, 1024, SUBLANE)
    n_parts = parts.shape[0]

    def body(qc_ref, g_ref, p_ref, o_ref):
        del qc_ref
        g = g_ref[...]
        for k in range(n_parts):
            g = g + p_ref[k]
        o_ref[...] = g

    return pl.pallas_call(
        body, name=name,
        grid_spec=pltpu.PrefetchScalarGridSpec(
            num_scalar_prefetch=1, grid=(rows // tile,),
            in_specs=[pl.BlockSpec((None, tile, cols), lambda i, qc: (qc[0], i, 0)),
                      pl.BlockSpec((n_parts, tile, cols), lambda i, qc: (0, i, 0))],
            out_specs=pl.BlockSpec((None, tile, cols), lambda i, qc: (qc[1], i, 0))),
        out_shape=jax.ShapeDtypeStruct((2, rows, cols), F32), compiler_params=_params(),
    )(qc, sums, parts)


def _adamw_shard(g, w, m, v, *, name):
    _, rows, cols = g.shape
    tile = _pick(rows, 512, SUBLANE)

    def body(g_ref, w_ref, m_ref, v_ref, do_ref, mo_ref, vo_ref):
        do_ref[...], mo_ref[...], vo_ref[...] = _adam_math(g_ref[...], w_ref[...], m_ref[...], v_ref[...])

    spec = pl.BlockSpec((None, tile, cols), lambda h, i: (h, i, 0))
    return pl.pallas_call(
        body, name=name, grid=(2, rows // tile), in_specs=[spec] * 4, out_specs=[spec] * 3,
        out_shape=[jax.ShapeDtypeStruct(g.shape, F32)] * 3, compiler_params=_params(),
    )(g, w, m, v)


def _adamw(parts, w, m, v, *, name):
    _, rows, cols = parts.shape
    tile = _pick(rows, 256, SUBLANE)

    def body(p_ref, w_ref, m_ref, v_ref, o_ref):
        g = p_ref[0]
        for i in range(1, N_DEV):
            g = g + p_ref[i]
        delta, m_new, v_new = _adam_math(g, w_ref[...], m_ref[...], v_ref[...])
        o_ref[0] = g
        o_ref[1] = delta
        o_ref[2] = m_new
        o_ref[3] = v_new

    spec = pl.BlockSpec((tile, cols), lambda i: (i, 0))
    return pl.pallas_call(
        body, name=name, grid=(rows // tile,),
        in_specs=[pl.BlockSpec((N_DEV, tile, cols), lambda i: (0, i, 0)), spec, spec, spec],
        out_specs=pl.BlockSpec((4, tile, cols), lambda i: (0, i, 0)),
        out_shape=jax.ShapeDtypeStruct((4, rows, cols), F32), compiler_params=_params(),
    )(parts, w, m, v)


def _layout(shapes, names, align):
    out, off = [], 0
    for n in names:
        size = math.prod(shapes[n])
        out.append((n, tuple(shapes[n]), off, size))
        off += _round_up(size, align)
    return out, off


def _pack(arrays, layout, total, lead=()):
    parts = []
    for i, (n, _, off, size) in enumerate(layout):
        end = layout[i + 1][2] if i + 1 < len(layout) else total
        flat = arrays[n].reshape(lead + (size,))
        if end - off > size:
            flat = jnp.pad(flat, [(0, 0)] * len(lead) + [(0, end - off - size)])
        parts.append(flat)
    return jnp.concatenate(parts, axis=len(lead))


def _unpack(flat, layout, lead=()):
    return {n: flat[..., off:off + size].reshape(lead + shape) for n, shape, off, size in layout}


def _to_shards(full, axis):
    shp = full.shape
    return jnp.moveaxis(full.reshape(shp[:axis] + (N_CHIP, shp[axis] // N_CHIP) + shp[axis + 1:]), axis, 0)


def _from_shards(shards, axis):
    x = jnp.moveaxis(shards, 0, axis)
    shp = x.shape
    return x.reshape(shp[:axis] + (shp[axis] * shp[axis + 1],) + shp[axis + 2:])


def _pad_cols(w, per, padded):
    lead = w.shape[:-1]
    x = w.reshape(lead + (-1, per))
    x = jnp.pad(x, [(0, 0)] * len(lead) + [(0, 0), (0, padded - per)])
    return x.reshape(lead + (-1,))


def _unpad_cols(w, per, padded):
    lead = w.shape[:-1]
    return w.reshape(lead + (-1, padded))[..., :per].reshape(lead + (-1,))


def _local_step(x, mem, target, W):
    nb, seq, d = x.shape
    n = nb * seq
    tokw = d - MEM_WIDTH
    heads = tokw // HEAD_DIM
    mlen = mem.shape[1]
    per = W["ffn_w_down"].shape[1] // 2
    per_p = _round_up(per, LANE)
    fp = 2 * per_p
    kvw = 2 * tokw + heads
    kvp = 2 * tokw + LANE
    gate_block = 2 * tokw // LANE

    x2d = x.reshape(n, d)
    mem2d = mem.reshape(nb * mlen, d)
    t2d = target.reshape(n, d)
    row = lambda a: a.reshape(1, -1)
    ones_tok = jnp.ones((1, tokw), F32)

    pool_bd = jax.scipy.linalg.block_diag(*[W["a_pool_w"][0, i] for i in range(len(POOL_WINDOWS))]).astype(BF16)
    kv_w = jnp.pad(W["kv_w"], ((0, 0), (0, kvp - kvw)))
    fb = jnp.pad(W["f_b"], (0, LANE - heads)).reshape(1, LANE)
    w_up = W.get("ffn_w_up_padded")
    if w_up is None:
        w_up = jnp.stack([_pad_cols(W["ffn_w_up"][l], per, per_p) for l in range(DEPTH)])
    w_down = [jnp.pad(W["ffn_w_down"][l].reshape(2, per, d), ((0, 0), (0, per_p - per), (0, 0))).reshape(fp, d)
              for l in range(DEPTH)]
    conv_w = [jnp.pad(_pad_cols(W["ffn_conv_w"][l], per, per_p), ((0, SUBLANE - CONV_WIDTH), (0, 0)))
              for l in range(DEPTH)]
    conv_b = [_pad_cols(W["ffn_conv_b"][l], per, per_p).reshape(1, 2 * fp) for l in range(DEPTH)]
    w_in = [W["a_w_in"][0], W["b_w_q"][0]]
    w_out = [W["a_w_out"][0], W["b_w_out"][0]]

    saved = []
    cur = cur_mm = x2d
    for l in range(DEPTH):
        s = {"x_in": cur, "x_in_mm": cur_mm}
        memkv = _mm(mem2d, W["mem_w_kv"][l], "nn", name=f"memkv{l}").reshape(nb, mlen, 2 * MEM_WIDTH)
        if l == 0:
            proj = _mm(cur_mm, w_in[l], "nn", name="proj0")
            pooled = _pool_fwd(proj.reshape(nb, seq, d), tokw, name="pool_fwd").reshape(n, tokw)
            tok = _mm(pooled, pool_bd, "nn", name="pool_mix")
            scale = W["a_pool_scale"].reshape(1, tokw)
            s.update(pooled=pooled, mixed=tok, scale=scale)
        else:
            kvf = _mm(cur_mm, kv_w, "nn", tn=kvp, name="kvf")
            kvf3 = kvf.reshape(nb, seq, kvp)
            gsum = _gate_fwd(kvf3, fb, gate_block, name="gate_fwd")[:, :, :heads]
            aq3, ak3 = _bias_lanes(gsum)
            proj = _mm(cur_mm, w_in[l], "nn", name="proj1")
            o3, lse3 = _fox_fwd(proj.reshape(nb, seq, d), kvf3, aq3, ak3, tokw, name="fox_fwd")
            tok = o3.reshape(n, tokw)
            scale = ones_tok
            s.update(kvf3=kvf3, aq3=aq3, ak3=ak3, o3=o3, lse3=lse3)
        mixin = _memattn_fwd(tok, proj, memkv, scale, seq=seq, name=f"memattn_fwd{l}")
        mix = _mm(mixin, w_out[l], "nn", name=f"mix{l}")
        x1, x1_mm = _ln_fwd(cur, mix, row(W["ln1_g"][l]), row(W["ln1_b"][l]), name=f"ln1_fwd{l}")
        up = _mm(x1_mm, w_up, "nn", b_lead=l, tn=per_p, col_major=True, name=f"ffn_up{l}")
        act = _convgate_fwd(up.reshape(nb, seq, 2 * fp), conv_w[l], conv_b[l], name=f"convgate_fwd{l}")
        act = act.reshape(n, fp)
        ffn = _mm(act, w_down[l], "nn", tk=fp, name=f"ffn_down{l}")
        s.update(proj=proj, memkv=memkv, mixin=mixin, mix=mix, x1=x1, x1_mm=x1_mm, up=up, act=act, ffn=ffn)
        saved.append(s)
        if l + 1 < DEPTH:
            cur, cur_mm = _ln_fwd(x1, ffn, row(W["ln2_g"][l]), row(W["ln2_b"][l]), name=f"ln2_fwd{l}")

    G = {}
    ln_g = {k: [None] * DEPTH for k in ("ln1_g", "ln1_b", "ln2_g", "ln2_b")}
    stack = {k: [None] * DEPTH for k in ("mem_w_kv", "ffn_w_up", "ffn_conv_w", "ffn_conv_b", "ffn_w_down")}
    dx_terms = None
    loss = None
    for l in reversed(range(DEPTH)):
        s = saved[l]
        g2 = row(W["ln2_g"][l])
        if l == DEPTH - 1:
            dres, dffn, loss, dg, db = _final_ln_loss(s["x1"], s["ffn"], t2d, g2, row(W["ln2_b"][l]),
                                                      name="final_ln_loss")
        else:
            dres, dffn, dg, db = _ln_bwd(s["x1"], s["ffn"], g2, dx_terms, name=f"ln2_bwd{l}")
        ln_g["ln2_g"][l], ln_g["ln2_b"][l] = dg[0], db[0]
        dact = _mm(dffn, w_down[l], "nt", tn=fp, name=f"ffn_down_dx{l}")
        stack["ffn_w_down"][l] = _mm(s["act"], dffn, "tn", tm=per_p, tk=DW_ROWS, trim=("rows", per),
                                     name=f"ffn_down_dw{l}")
        du3, dg3, dcw, dcb = _convgate_bwd(s["up"].reshape(nb, seq, 2 * fp), dact.reshape(nb, seq, fp), conv_w[l],
                                           conv_b[l], name=f"convgate_bwd{l}")
        du, dgt = du3.reshape(n, fp), dg3.reshape(n, fp)
        dx1_u = _mm(du, w_up, "nt", b_lead=l, tk=fp, name=f"ffn_up_dx_u{l}")
        dx1_ffn = _mm(dgt, w_up, "nt", b_lead=l, tk=fp, b_col0=fp, adds=[dx1_u], name=f"ffn_up_dx_g{l}")
        stack["ffn_w_up"][l] = [_mm(s["x1_mm"], part, "tn", tm=d, tn=per_p, tk=DW_ROWS, trim=("cols", per),
                                    name=f"ffn_up_dw_{nm}{l}") for nm, part in (("u", du), ("g", dgt))]
        stack["ffn_conv_w"][l] = dcw[:CONV_WIDTH]
        stack["ffn_conv_b"][l] = dcb[0]
        dres1, dmix, dg, db = _ln_bwd(s["x_in"], s["mix"], row(W["ln1_g"][l]), [dres, dx1_ffn], name=f"ln1_bwd{l}")
        ln_g["ln1_g"][l], ln_g["ln1_b"][l] = dg[0], db[0]
        dmixin = _mm(dmix, w_out[l], "nt", name=f"mix_dx{l}")
        d_w_out = _mm(s["mixin"], dmix, "tn", tm=d, tk=DW_ROWS, name=f"mix_dw{l}")
        if l == 0:
            G["a_w_out"] = d_w_out[None]
            dmixed, dscale = _scale_bwd(dmixin, s["mixed"], s["scale"], name="scale_bwd")
            G["a_pool_scale"] = dscale
            dpooled = _mm(dmixed, pool_bd, "nt", name="pool_mix_dx")
            dpw = _mm(s["pooled"], dmixed, "tn", tm=tokw, tk=DW_ROWS, name="pool_mix_dw")
            grp = tokw // len(POOL_WINDOWS)
            G["a_pool_w"] = jnp.stack([dpw[i * grp:(i + 1) * grp, i * grp:(i + 1) * grp]
                                       for i in range(len(POOL_WINDOWS))])[None]
            dtok = _pool_bwd(dpooled.reshape(nb, seq, tokw), name="pool_bwd").reshape(n, tokw)
            extra = []
        else:
            G["b_w_out"] = d_w_out[None]
            p3 = s["proj"].reshape(nb, seq, d)
            dm3 = dmixin.reshape(nb, seq, d)
            dq3, dk3, dv3, dfk = _fox_bwd(p3, s["kvf3"], s["aq3"], s["ak3"], s["o3"], dm3, s["lse3"], tokw,
                                          name="fox_bwd")
            dtok = dq3.reshape(n, tokw)
            dfk = jnp.swapaxes(dfk[:, :, :, 0:2, :] + dfk[:, :, :, 2:4, :], 2, 3).reshape(nb, heads, seq)
            dgsum = jnp.swapaxes(dfk, 1, 2)
            dgsum = jnp.pad(dgsum, ((0, 0), (0, 0), (0, LANE - heads)))
            df3, dfb = _gate_bwd(s["kvf3"], fb, dgsum, gate_block, heads, name="gate_bwd")
            G["f_b"] = dfb[0, :heads]
            dkvf = [(dk3.reshape(n, tokw), 0, "k"), (dv3.reshape(n, tokw), tokw, "v"),
                    (df3.reshape(n, LANE), 2 * tokw, "f")]
            dx_kv = []
            for part, col0, nm in dkvf:
                dx_kv = [_mm(part, kv_w, "nt", b_col0=col0, adds=dx_kv, name=f"kvf_dx_{nm}")]
            extra = dx_kv
            G["kv_w"] = jnp.concatenate([_mm(s["x_in_mm"], part, "tn", tm=d, tk=DW_ROWS, name=f"kvf_dw_{nm}")
                                         for part, _, nm in dkvf], axis=1)[:, :kvw]
        dproj, dmemkv = _memattn_bwd(dmixin, dtok, s["proj"], s["memkv"], seq=seq, name=f"memattn_bwd{l}")
        stack["mem_w_kv"][l] = _mm(mem2d, dmemkv.reshape(nb * mlen, 2 * MEM_WIDTH), "tn", tm=d, tk=DW_ROWS,
                                   name=f"memkv_dw{l}")
        G["a_w_in" if l == 0 else "b_w_q"] = _mm(s["x_in_mm"], dproj, "tn", tm=d, tk=DW_ROWS, name=f"proj_dw{l}")[None]
        if l == 0:
            grad_x = _mm(dproj, w_in[l], "nt", adds=[dres1], name="proj_dx0")
        else:
            dx_terms = [dres1, _mm(dproj, w_in[l], "nt", name="proj_dx1")] + extra
    for k, v in ln_g.items():
        G[k] = jnp.stack(v)
    G["mem_w_kv"] = jnp.stack(stack["mem_w_kv"])
    G["ffn_w_up"] = jnp.stack([jnp.concatenate(ug, axis=0) for ug in stack["ffn_w_up"]], axis=1)
    G["ffn_conv_w"] = jnp.stack([_unpad_cols(g, per, per_p) for g in stack["ffn_conv_w"]])
    G["ffn_conv_b"] = jnp.stack([_unpad_cols(g, per, per_p) for g in stack["ffn_conv_b"]])
    G["ffn_w_down"] = jnp.stack([g.reshape(N_CHIP, per // 2, d) for g in stack["ffn_w_down"]], axis=1)
    return loss[0, 0], grad_x.reshape(nb, seq, d), G


def kernel(x, mem, a_w_in, a_pool_w, a_pool_scale, a_w_out, b_w_q, b_w_out, kv_w, f_b, mem_w_kv, ln1_g, ln1_b, ln2_g, ln2_b, ffn_w_up, ffn_conv_w, ffn_conv_b, ffn_w_down, loss_target, m_a_w_in, m_a_pool_w, m_a_pool_scale, m_a_w_out, m_b_w_q, m_b_w_out, m_kv_w, m_f_b, m_mem_w_kv, m_ln1_g, m_ln1_b, m_ln2_g, m_ln2_b, m_ffn_w_up, m_ffn_conv_w, m_ffn_conv_b, m_ffn_w_down, v_a_w_in, v_a_pool_w, v_a_pool_scale, v_a_w_out, v_b_w_q, v_b_w_out, v_kv_w, v_f_b, v_mem_w_kv, v_ln1_g, v_ln1_b, v_ln2_g, v_ln2_b, v_ffn_w_up, v_ffn_conv_w, v_ffn_conv_b, v_ffn_w_down):
    w_loc = dict(a_w_in=a_w_in, a_pool_w=a_pool_w, a_pool_scale=a_pool_scale, a_w_out=a_w_out, b_w_q=b_w_q,
                 b_w_out=b_w_out, kv_w=kv_w, f_b=f_b, mem_w_kv=mem_w_kv, ln1_g=ln1_g, ln1_b=ln1_b, ln2_g=ln2_g,
                 ln2_b=ln2_b, ffn_w_up=ffn_w_up, ffn_conv_w=ffn_conv_w, ffn_conv_b=ffn_conv_b, ffn_w_down=ffn_w_down)
    m_loc = dict(a_w_in=m_a_w_in, a_pool_w=m_a_pool_w, a_pool_scale=m_a_pool_scale, a_w_out=m_a_w_out,
                 b_w_q=m_b_w_q, b_w_out=m_b_w_out, kv_w=m_kv_w, f_b=m_f_b, mem_w_kv=m_mem_w_kv, ln1_g=m_ln1_g,
                 ln1_b=m_ln1_b, ln2_g=m_ln2_g, ln2_b=m_ln2_b, ffn_w_up=m_ffn_w_up, ffn_conv_w=m_ffn_conv_w,
                 ffn_conv_b=m_ffn_conv_b, ffn_w_down=m_ffn_w_down)
    v_loc = dict(a_w_in=v_a_w_in, a_pool_w=v_a_pool_w, a_pool_scale=v_a_pool_scale, a_w_out=v_a_w_out,
                 b_w_q=v_b_w_q, b_w_out=v_b_w_out, kv_w=v_kv_w, f_b=v_f_b, mem_w_kv=v_mem_w_kv, ln1_g=v_ln1_g,
                 ln1_b=v_ln1_b, ln2_g=v_ln2_g, ln2_b=v_ln2_b, ffn_w_up=v_ffn_w_up, ffn_conv_w=v_ffn_conv_w,
                 ffn_conv_b=v_ffn_conv_b, ffn_w_down=v_ffn_w_down)
    x_i, y_i, c = lax.axis_index("x"), lax.axis_index("y"), lax.axis_index("c")
    q = 2 * x_i + y_i
    qc = jnp.stack([q, c]).astype(jnp.int32)
    shapes = {k: v.shape for k, v in w_loc.items()}

    def halves(a):
        if a.ndim == 3 and a.shape[0] == 2:
            return a
        rows = math.prod(a.shape[:-1])
        return a.reshape(2, rows // 2, a.shape[-1])

    own = {k: (w_loc[k] if k in GATHER_F32 else w_loc[k].astype(BF16)) for k in SHARDED}
    by_chip = ("a_pool_scale",) + tuple(k for k in BIG if k != "ffn_w_up")
    per = shapes["ffn_w_up"][-1]
    up_own = jnp.pad(own["ffn_w_up"], ((0, 0), (0, 0), (0, _round_up(per, LANE) - per)))
    gathered = _gather_shards([own["a_pool_scale"]], [halves(own[k]) for k in by_chip[1:]], [up_own],
                              name="gather_weights")
    W = {k: _from_shards(lax.dynamic_update_slice_in_dim(g.reshape((N_CHIP,) + shapes[k]), own[k][None], q, axis=0),
                         SHARD_AXIS[k])
         for k, g in zip(by_chip, gathered)}
    W["ffn_w_up_padded"] = gathered[-1]
    for k in REPLICATED:
        W[k] = w_loc[k]

    loss_part, grad_x, G = _local_step(x, mem, loss_target, W)

    g_chip = [G[k] if k in GRADS_BY_CHIP else _to_shards(G[k], SHARD_AXIS[k]) for k in BIG]
    g_chip = [g.reshape((N_CHIP,) + halves(w_loc[k]).shape) for k, g in zip(BIG, g_chip)]
    lay_r, tot_r = _layout(shapes, REPLICATED, PACK_COLS)
    rep_rows = tot_r // PACK_COLS
    rows = _round_up(rep_rows + 2, SUBLANE)
    scale_w = shapes["a_pool_scale"][-1]

    def small(rep, scale_row, scalar):
        lead = scale_row.shape[:-2]
        pad = [(0, 0)] * len(lead)
        rep = jnp.broadcast_to(_pack(rep, lay_r, tot_r).reshape(rep_rows, PACK_COLS), lead + (rep_rows, PACK_COLS))
        scale_row = jnp.pad(scale_row, pad + [(0, 0), (0, PACK_COLS - scale_w)])
        last = jnp.broadcast_to(jnp.pad(scalar.reshape(1, 1), [(0, rows - rep_rows - 2), (0, PACK_COLS - 1)]),
                                lead + (rows - rep_rows - 1, PACK_COLS))
        return jnp.concatenate([rep, scale_row, last], axis=-2)

    g_scale = jnp.repeat(_to_shards(G["a_pool_scale"], 1), 2, axis=0)
    zero = jnp.zeros((), F32)

    from_sib = _to_sibling(g_chip, name="grads_to_sibling")
    sums = [_pair_add(g, s, qc, name=f"pair_add_{k}") for k, g, s in zip(BIG, g_chip, from_sib)]
    *parts, sm_parts = _to_chips(sums, small(G, g_scale, loss_part), name="scatter_grads")
    grads = _swap_halves([_reduce_half(g, p, qc, name=f"reduce_{k}") for k, g, p in zip(BIG, sums, parts)],
                         name="swap_halves")
    out = {}
    for k, g in zip(BIG, grads):
        upd = _adamw_shard(g, halves(w_loc[k]), halves(m_loc[k]), halves(v_loc[k]), name=f"adamw_{k}")
        out[k] = [r.reshape(shapes[k]) for r in (g, *upd)]
    sm = _adamw(sm_parts, *[small(d, d["a_pool_scale"], zero) for d in (w_loc, m_loc, v_loc)], name="adamw_small")
    loss = sm[0, rep_rows + 1, 0]
    out_r = _unpack(sm[:, :rep_rows].reshape(4, tot_r), lay_r, lead=(4,))
    for k in REPLICATED:
        out[k] = [out_r[k][a] for a in range(4)]
    out["a_pool_scale"] = [sm[a, rep_rows:rep_rows + 1, :scale_w] for a in range(4)]

    outs = [loss, grad_x]
    for a in range(4):
        for k in WEIGHTS:
            outs.append(out[k][a])
    return tuple(outs)
```

```python
import functools
import math

import jax
import jax.numpy as jnp
from jax import lax
from jax.experimental import pallas as pl
from jax.experimental.pallas import tpu as pltpu

F32 = jnp.float32
BF16 = jnp.bfloat16

HEAD_DIM = 64
MEM_HEADS = 4
MEM_WIDTH = MEM_HEADS * HEAD_DIM
POOL_WINDOWS = (2, 4, 8, 16)
MAX_WINDOW = 16
CONV_WIDTH = 3
DEPTH = 2
DN_ALPHA = (2.0 * DEPTH) ** 0.25
LN_EPS = 1e-5
ATT_SCALE = HEAD_DIM ** -0.5
NEG_BIG = -1e30

ADAM_LR = 0.001
ADAM_B1 = 0.9
ADAM_B2 = 0.999
ADAM_EPS = 1e-08
ADAM_WD = 0.01
ADAM_STEP = 10

LANE = 128
SUBLANE = 8
PACK_COLS = 1024
DW_ROWS = 1024
VMEM_LIMIT = 56 * 1024 * 1024
N_DEV = 8
N_CHIP = 4
MESH_T = pl.DeviceIdType.MESH

SHARDED = ("a_w_in", "a_pool_scale", "a_w_out", "b_w_q", "b_w_out", "kv_w", "mem_w_kv", "ffn_w_up",
           "ffn_conv_w", "ffn_w_down")
SHARD_AXIS = {"a_w_in": 1, "a_pool_scale": 1, "a_w_out": 1, "b_w_q": 1, "b_w_out": 1, "kv_w": 1, "mem_w_kv": 1,
              "ffn_w_up": 2, "ffn_conv_w": 2, "ffn_w_down": 1}
GATHER_F32 = ("a_pool_scale", "ffn_conv_w")
BIG = tuple(k for k in SHARDED if k != "a_pool_scale")
GRADS_BY_CHIP = ("ffn_w_up", "ffn_w_down")
REPLICATED = ("a_pool_w", "f_b", "ln1_g", "ln1_b", "ln2_g", "ln2_b", "ffn_conv_b")
WEIGHTS = ("a_w_in", "a_pool_w", "a_pool_scale", "a_w_out", "b_w_q", "b_w_out", "kv_w", "f_b", "mem_w_kv",
           "ln1_g", "ln1_b", "ln2_g", "ln2_b", "ffn_w_up", "ffn_conv_w", "ffn_conv_b", "ffn_w_down")


def _round_up(n, m):
    return -(-n // m) * m


def _pick(dim, pref, unit=LANE):
    if dim <= pref:
        return dim
    t = (pref // unit) * unit
    while t >= unit:
        if dim % t == 0:
            return t
        t -= unit
    raise ValueError(f"no tile for {dim} <= {pref}")


def _params():
    return pltpu.CompilerParams(vmem_limit_bytes=VMEM_LIMIT)


_DIMS = {"nn": ((1,), (0,)), "nt": ((1,), (1,)), "tn": ((0,), (0,))}


def _bdot(a, b, mode):
    return lax.dot_general(a.astype(BF16), b.astype(BF16), (_DIMS[mode], ((), ())), preferred_element_type=F32)


def _mm(a, b, mode, *, name, tm=512, tn=1024, tk=2048, adds=(), b_col0=0, trim=None, col_major=False, b_lead=None,
        ln=None):
    b_shape = b.shape if b_lead is None else b.shape[1:]
    if mode == "nn":
        (M, K), (K2, N) = a.shape, b_shape
    elif mode == "nt":
        (M, K), (N, K2) = a.shape, b_shape
        K2 = K if b_col0 + K <= K2 else -1
    else:
        (K, M), (K2, N) = a.shape, b_shape
    assert K == K2 and (mode == "nt" or b_col0 == 0), (name, a.shape, b.shape)
    tm, tn = _pick(M, tm, SUBLANE if mode != "tn" else LANE), _pick(N, tn)
    tk = _pick(K, tk, LANE if mode != "tn" else SUBLANE)
    nk = K // tk
    assert b_col0 % tk == 0, (name, b_col0, tk)
    koff = b_col0 // tk
    n_add = len(adds)
    n_ln = 0 if ln is None else 3

    def body(*refs):
        a_ref, b_ref = refs[0], refs[1]
        add_refs = refs[2:2 + n_add]
        ln_refs = refs[2 + n_add:2 + n_add + n_ln]
        o_ref, acc_ref = refs[2 + n_add + n_ln], refs[-1]
        part = _bdot(a_ref[...], b_ref[...], mode)

        def finish(r):
            for ar in add_refs:
                r = r + ar[...]
            if trim is not None:
                r = r[:, :trim[1]] if trim[0] == "cols" else r[:trim[1], :]
            o_ref[...] = r
            if ln is not None:
                x_ref, g_ref, beta_ref = ln_refs
                y_ref, y16_ref = refs[3 + n_add + n_ln], refs[4 + n_add + n_ln]
                xhat, _ = _ln_stats(DN_ALPHA * x_ref[...] + r)
                y = xhat * g_ref[...] + beta_ref[...]
                y_ref[...] = y
                y16_ref[...] = y.astype(BF16)

        if nk == 1:
            finish(part)
        else:
            k = pl.program_id(2)

            @pl.when(k == 0)
            def _():
                acc_ref[...] = part

            @pl.when(k > 0)
            def _():
                acc_ref[...] += part

            @pl.when(k == nk - 1)
            def _():
                finish(acc_ref[...])

    def spec(block, index):
        if col_major:
            return pl.BlockSpec(block, lambda j, i, k: index(i, j, k))
        return pl.BlockSpec(block, index)

    def b_spec_of(block, index):
        if b_lead is None:
            return spec(block, index)
        return spec((None,) + block, lambda i, j, k: (b_lead,) + index(i, j, k))

    if mode == "nn":
        a_spec = spec((tm, tk), lambda i, j, k: (i, k))
        b_spec = b_spec_of((tk, tn), lambda i, j, k: (k, j))
    elif mode == "nt":
        a_spec = spec((tm, tk), lambda i, j, k: (i, k))
        b_spec = b_spec_of((tn, tk), lambda i, j, k: (j, k + koff))
    else:
        a_spec = spec((tk, tm), lambda i, j, k: (k, i))
        b_spec = b_spec_of((tk, tn), lambda i, j, k: (k, j))
    o_spec = spec((tm, tn), lambda i, j, k: (i, j))
    out_spec, out_shape = o_spec, (M, N)
    if trim is not None and trim[0] == "cols":
        out_spec, out_shape = spec((None, tm, trim[1]), lambda i, j, k: (j, i, 0)), (N // tn, M, trim[1])
    elif trim is not None:
        out_spec, out_shape = spec((None, trim[1], tn), lambda i, j, k: (i, 0, j)), (M // tm, trim[1], N)
    acc_shape = (tm, tn) if nk > 1 else (SUBLANE, LANE)
    in_specs, out_specs, out_shapes, ln_args = [a_spec, b_spec] + [o_spec] * n_add, out_spec, \
        jax.ShapeDtypeStruct(out_shape, F32), ()
    if ln is not None:
        assert tn == N and trim is None, name
        vec = spec((1, tn), lambda i, j, k: (0, j))
        in_specs += [o_spec, vec, vec]
        out_specs = [out_spec, o_spec, o_spec]
        out_shapes = [out_shapes, jax.ShapeDtypeStruct((M, N), F32), jax.ShapeDtypeStruct((M, N), BF16)]
        ln_args = ln
    return pl.pallas_call(
        body, name=name, grid=(N // tn, M // tm, nk) if col_major else (M // tm, N // tn, nk),
        in_specs=in_specs, out_specs=out_specs, out_shape=out_shapes,
        scratch_shapes=[pltpu.VMEM(acc_shape, F32)],
        compiler_params=_params(),
    )(a, b, *adds, *ln_args)


def _rowwise(fn, tiled, full, outs_tiled, outs_acc, *, rows, tile, name, acc_period=None):
    n_tiles = rows // tile
    period = n_tiles if acc_period is None else acc_period
    arrays, in_specs = [], []
    for t in tiled:
        arr, width, cb = t if isinstance(t, tuple) else (t, t.shape[1], 0)
        arrays.append(arr)
        in_specs.append(pl.BlockSpec((tile, width), lambda i, cb=cb: (i, cb)))
    for f in full:
        arr, spec = f if isinstance(f, tuple) else (f, None)
        arrays.append(arr)
        in_specs.append(spec if spec is not None else pl.BlockSpec(arr.shape, lambda i, nd=arr.ndim: (0,) * nd))
    out_shape, out_specs = [], []
    for width, dt in outs_tiled:
        out_shape.append(jax.ShapeDtypeStruct((rows, width), dt))
        out_specs.append(pl.BlockSpec((tile, width), lambda i: (i, 0)))
    for acc in outs_acc:
        shape, dt = acc[0], acc[1]
        out_shape.append(jax.ShapeDtypeStruct(shape, dt))
        out_specs.append(acc[2] if len(acc) > 2 else pl.BlockSpec(shape, lambda i, nd=len(shape): (0,) * nd))
    n_in, n_t, n_a = len(arrays), len(outs_tiled), len(outs_acc)

    def body(*refs):
        vals = [r[...] for r in refs[:n_in]]
        o_t, o_a = fn(*vals)
        for r, v in zip(refs[n_in:n_in + n_t], o_t):
            r[...] = v.astype(r.dtype)
        first = pl.program_id(0) % period == 0
        for r, v in zip(refs[n_in + n_t:n_in + n_t + n_a], o_a):
            v = v.reshape(r.shape)

            @pl.when(first)
            def _(r=r, v=v):
                r[...] = v

            @pl.when(jnp.logical_not(first))
            def _(r=r, v=v):
                r[...] += v

    return pl.pallas_call(
        body, name=name, grid=(n_tiles,), in_specs=in_specs, out_specs=out_specs, out_shape=out_shape,
        compiler_params=_params(),
    )(*arrays)


def _ln_stats(h):
    mu = jnp.mean(h, axis=-1, keepdims=True)
    d = h - mu
    var = jnp.mean(d * d, axis=-1, keepdims=True)
    rstd = lax.rsqrt(var + LN_EPS)
    return d * rstd, rstd


def _ln_bwd_math(h, g, dy):
    xhat, rstd = _ln_stats(h)
    dxhat = dy * g
    dh = rstd * (dxhat - jnp.mean(dxhat, axis=-1, keepdims=True)
                 - xhat * jnp.mean(dxhat * xhat, axis=-1, keepdims=True))
    return dh, jnp.sum(dy * xhat, axis=0, keepdims=True), jnp.sum(dy, axis=0, keepdims=True)


def _ln_bwd(x, r, g, dys, *, name):
    n, d = x.shape
    n_dy = len(dys)

    def fn(x, r, *rest):
        dy = rest[0]
        for e in rest[1:n_dy]:
            dy = dy + e
        dh, dg, db = _ln_bwd_math(DN_ALPHA * x + r, rest[n_dy], dy)
        return (DN_ALPHA * dh, dh), (dg, db)

    return _rowwise(fn, [x, r, *dys], [g], [(d, F32), (d, BF16)], [((1, d), F32), ((1, d), F32)],
                    rows=n, tile=_pick(n, 256, SUBLANE), name=name)


def _final_ln_loss(x, r, target, g, b, *, name):
    n, d = x.shape

    def fn(x, r, t, g, b):
        h = DN_ALPHA * x + r
        xhat, _ = _ln_stats(h)
        err = xhat * g + b - t
        loss = jnp.full((1, LANE), 0.5 * jnp.sum(err * err) / d, F32)
        dh, dg, db = _ln_bwd_math(h, g, err / d)
        return (DN_ALPHA * dh, dh), (loss, dg, db)

    return _rowwise(fn, [x, r, target], [g, b], [(d, F32), (d, BF16)],
                    [((1, LANE), F32), ((1, d), F32), ((1, d), F32)],
                    rows=n, tile=_pick(n, 256, SUBLANE), name=name)


def _mem_heads(qm):
    lane = lax.broadcasted_iota(jnp.int32, (1, MEM_WIDTH), 1)
    for h in range(MEM_HEADS):
        msk = (lane >= h * HEAD_DIM) & (lane < (h + 1) * HEAD_DIM)
        yield msk, jnp.where(msk, qm, 0.0).astype(BF16)


def _mem_softmax(qh, k):
    s = _bdot(qh, k, "nt") * ATT_SCALE
    p = jnp.exp(s - jnp.max(s, axis=-1, keepdims=True))
    return p / jnp.sum(p, axis=-1, keepdims=True)


def _memattn_fwd(tok, proj, memkv, scale, *, seq, name):
    n, tokw = tok.shape
    d = tokw + MEM_WIDTH
    tile = _pick(seq, 512, SUBLANE)

    def fn(tok, qm, kv, scale):
        k, v = kv[:, :MEM_WIDTH].astype(BF16), kv[:, MEM_WIDTH:].astype(BF16)
        out = jnp.zeros(qm.shape, F32)
        for msk, qh in _mem_heads(qm):
            out = jnp.where(msk, _bdot(_mem_softmax(qh, k), v, "nn"), out)
        return (jnp.concatenate([tok * scale, out], axis=1),), ()

    kv_spec = pl.BlockSpec((None,) + memkv.shape[1:], lambda i: (i // (seq // tile), 0, 0))
    return _rowwise(fn, [tok, (proj, MEM_WIDTH, tokw // MEM_WIDTH)], [(memkv, kv_spec), scale], [(d, BF16)], [],
                    rows=n, tile=tile, name=name)[0]


def _memattn_bwd(dmixin, dtok, proj, memkv, *, seq, name):
    n, tokw = dtok.shape
    d = tokw + MEM_WIDTH
    tile = _pick(seq, 512, SUBLANE)

    def fn(dmo, dtok, qm, kv):
        k, v = kv[:, :MEM_WIDTH].astype(BF16), kv[:, MEM_WIDTH:].astype(BF16)
        dq = jnp.zeros(qm.shape, F32)
        dk = jnp.zeros(k.shape, F32)
        dv = jnp.zeros(v.shape, F32)
        for msk, qh in _mem_heads(qm):
            p = _mem_softmax(qh, k)
            doh = jnp.where(msk, dmo, 0.0).astype(BF16)
            dv = dv + _bdot(p, doh, "tn")
            dp = _bdot(doh, v, "nt")
            ds = (p * (dp - jnp.sum(dp * p, axis=-1, keepdims=True))).astype(BF16)
            dq = jnp.where(msk, _bdot(ds, k, "nn") * ATT_SCALE, dq)
            dk = dk + _bdot(ds, qh, "tn") * ATT_SCALE
        return (jnp.concatenate([dtok, dq], axis=1),), (jnp.concatenate([dk, dv], axis=1),)

    tpe = seq // tile
    kv_spec = pl.BlockSpec((None,) + memkv.shape[1:], lambda i: (i // tpe, 0, 0))
    return _rowwise(fn, [(dmixin, MEM_WIDTH, tokw // MEM_WIDTH), dtok, (proj, MEM_WIDTH, tokw // MEM_WIDTH)],
                    [(memkv, kv_spec)], [(d, BF16)], [(memkv.shape, F32, kv_spec)],
                    rows=n, tile=tile, name=name, acc_period=tpe)


def _scale_bwd(dmixin, mixed, scale, *, name):
    n, tokw = mixed.shape

    def fn(dt, mixed, scale):
        return (dt * scale,), (jnp.sum(dt * mixed, axis=0, keepdims=True),)

    return _rowwise(fn, [(dmixin, tokw, 0), mixed], [scale], [(tokw, BF16)], [((1, tokw), F32)],
                    rows=n, tile=_pick(n, 512, SUBLANE), name=name)


def _chunk_rows(seq):
    return _pick(seq, 512, SUBLANE)


def _load_ext(ref, c, rows, before, after, seq):
    lo, hi = c * rows - before, (c + 1) * rows + after
    parts = []
    if lo < 0:
        parts.append(jnp.zeros((-lo, ref.shape[1]), F32))
    parts.append(ref[max(lo, 0):min(hi, seq), :])
    if hi > seq:
        parts.append(jnp.zeros((hi - seq, ref.shape[1]), F32))
    return parts[0] if len(parts) == 1 else jnp.concatenate(parts, axis=0)


def _down(x, k):
    return pltpu.roll(x, k, 0)


def _up(x, k):
    return pltpu.roll(x, x.shape[0] - k, 0)


def _window_sums(ext, shift, col0, group):
    lane = col0 + lax.broadcasted_iota(jnp.int32, (1, ext.shape[1]), 1)
    gidx = lane // group
    s = ext
    out = None
    k = 1
    for gi, w in enumerate(POOL_WINDOWS):
        while k < w:
            s = s + shift(s, k)
            k *= 2
        out = s if out is None else jnp.where(gidx >= gi, s, out)
    return out, jnp.left_shift(2, jnp.minimum(gidx, len(POOL_WINDOWS) - 1))


def _pool_fwd(proj3, tokw, *, name):
    nb, seq, _ = proj3.shape
    rows = _chunk_rows(seq)
    group = tokw // len(POOL_WINDOWS)

    def body(u_ref, o_ref):
        col0 = pl.program_id(1) * LANE
        for c in range(seq // rows):
            ext = _load_ext(u_ref, c, rows, MAX_WINDOW, 0, seq)
            sums, win = _window_sums(ext, _down, col0, group)
            t = c * rows + lax.broadcasted_iota(jnp.int32, (rows, 1), 0)
            count = jnp.minimum(t + 1, win).astype(F32)
            o_ref[c * rows:(c + 1) * rows, :] = (sums[MAX_WINDOW:, :] / count - ext[MAX_WINDOW:, :]).astype(BF16)

    spec = pl.BlockSpec((None, seq, LANE), lambda b, j: (b, 0, j))
    return pl.pallas_call(
        body, name=name, grid=(nb, tokw // LANE), in_specs=[spec], out_specs=spec,
        out_shape=jax.ShapeDtypeStruct((nb, seq, tokw), BF16), compiler_params=_params(),
    )(proj3)


def _pool_bwd(dp3, *, name):
    nb, seq, tokw = dp3.shape
    rows = _chunk_rows(seq)
    group = tokw // len(POOL_WINDOWS)

    def body(d_ref, o_ref):
        col0 = pl.program_id(1) * LANE
        for c in range(seq // rows):
            ext = _load_ext(d_ref, c, rows, 0, MAX_WINDOW, seq)
            lane = col0 + lax.broadcasted_iota(jnp.int32, (1, LANE), 1)
            win = jnp.left_shift(2, jnp.minimum(lane // group, len(POOL_WINDOWS) - 1))
            t = c * rows + lax.broadcasted_iota(jnp.int32, (rows + MAX_WINDOW, 1), 0)
            scaled = ext / jnp.minimum(t + 1, win).astype(F32)
            sums, _ = _window_sums(scaled, _up, col0, group)
            o_ref[c * rows:(c + 1) * rows, :] = sums[:rows, :] - ext[:rows, :]

    spec = pl.BlockSpec((None, seq, LANE), lambda b, j: (b, 0, j))
    return pl.pallas_call(
        body, name=name, grid=(nb, tokw // LANE), in_specs=[spec], out_specs=spec,
        out_shape=jax.ShapeDtypeStruct((nb, seq, tokw), F32), compiler_params=_params(),
    )(dp3)


def _conv3(ext, w_ref, b_ref):
    x1, x2 = _down(ext, 1), _down(ext, 2)
    return w_ref[0:1, :] * x2 + w_ref[1:2, :] * x1 + w_ref[2:3, :] * ext + b_ref[...], x1, x2


def _convgate_fwd(up3, cw, cb, *, name):
    nb, seq, c2 = up3.shape
    fp = c2 // 2
    nblk = fp // LANE
    rows = _chunk_rows(seq)

    def body(u_ref, g_ref, wu_ref, wg_ref, bu_ref, bg_ref, o_ref):
        for c in range(seq // rows):
            hu, _, _ = _conv3(_load_ext(u_ref, c, rows, SUBLANE, 0, seq), wu_ref, bu_ref)
            hg, _, _ = _conv3(_load_ext(g_ref, c, rows, SUBLANE, 0, seq), wg_ref, bg_ref)
            o_ref[c * rows:(c + 1) * rows, :] = (hg * jax.nn.sigmoid(hg) * hu)[SUBLANE:, :].astype(BF16)

    def col(off, r):
        return pl.BlockSpec((r, LANE), lambda b, j: (0, j + off))

    def act(off):
        return pl.BlockSpec((None, seq, LANE), lambda b, j: (b, 0, j + off))

    return pl.pallas_call(
        body, name=name, grid=(nb, nblk),
        in_specs=[act(0), act(nblk), col(0, SUBLANE), col(nblk, SUBLANE), col(0, 1), col(nblk, 1)],
        out_specs=act(0), out_shape=jax.ShapeDtypeStruct((nb, seq, fp), BF16), compiler_params=_params(),
    )(up3, up3, cw, cw, cb, cb)


def _convgate_bwd(up3, dact3, cw, cb, *, name):
    nb, seq, c2 = up3.shape
    fp = c2 // 2
    nblk = fp // LANE
    rows = _chunk_rows(seq)
    h = SUBLANE

    def body(u_ref, g_ref, da_ref, wu_ref, wg_ref, bu_ref, bg_ref, du_ref, dg_ref, dwu_ref, dwg_ref, dbu_ref,
             dbg_ref):
        @pl.when(pl.program_id(1) == 0)
        def _():
            for r in (dwu_ref, dwg_ref, dbu_ref, dbg_ref):
                r[...] = jnp.zeros(r.shape, F32)

        for c in range(seq // rows):
            eu = _load_ext(u_ref, c, rows, h, h, seq)
            eg = _load_ext(g_ref, c, rows, h, h, seq)
            da = _load_ext(da_ref, c, rows, h, h, seq)
            hu, u1, u2 = _conv3(eu, wu_ref, bu_ref)
            hg, g1, g2 = _conv3(eg, wg_ref, bg_ref)
            sig = jax.nn.sigmoid(hg)
            dhu = da * hg * sig
            dhg = da * hu * sig * (1.0 + hg * (1.0 - sig))
            for dh, w_ref, x0, x1, x2, dx_ref, dw_ref, db_ref in (
                    (dhu, wu_ref, eu, u1, u2, du_ref, dwu_ref, dbu_ref),
                    (dhg, wg_ref, eg, g1, g2, dg_ref, dwg_ref, dbg_ref)):
                dx = w_ref[2:3, :] * dh + w_ref[1:2, :] * _up(dh, 1) + w_ref[0:1, :] * _up(dh, 2)
                dx_ref[c * rows:(c + 1) * rows, :] = dx[h:h + rows, :].astype(BF16)
                core = dh[h:h + rows, :]
                for k, xk in ((0, x2), (1, x1), (2, x0)):
                    dw_ref[k:k + 1, :] += jnp.sum(core * xk[h:h + rows, :], axis=0, keepdims=True)
                db_ref[...] += jnp.sum(core, axis=0, keepdims=True)

    def col(off, r):
        return pl.BlockSpec((r, LANE), lambda j, b: (0, j + off))

    def act(off):
        return pl.BlockSpec((None, seq, LANE), lambda j, b: (b, 0, j + off))

    du, dg, dwu, dwg, dbu, dbg = pl.pallas_call(
        body, name=name, grid=(nblk, nb),
        in_specs=[act(0), act(nblk), act(0), col(0, SUBLANE), col(nblk, SUBLANE), col(0, 1), col(nblk, 1)],
        out_specs=[act(0), act(0), col(0, SUBLANE), col(0, SUBLANE), col(0, 1), col(0, 1)],
        out_shape=[jax.ShapeDtypeStruct((nb, seq, fp), BF16), jax.ShapeDtypeStruct((nb, seq, fp), BF16),
                   jax.ShapeDtypeStruct((SUBLANE, fp), F32), jax.ShapeDtypeStruct((SUBLANE, fp), F32),
                   jax.ShapeDtypeStruct((1, fp), F32), jax.ShapeDtypeStruct((1, fp), F32)],
        compiler_params=_params(),
    )(up3, up3, dact3, cw, cw, cb, cb)
    return du, dg, jnp.concatenate([dwu, dwg], axis=1), jnp.concatenate([dbu, dbg], axis=1)


def _scan_rows(x, shift, valid):
    row = lax.broadcasted_iota(jnp.int32, (x.shape[0], 1), 0)
    k = 1
    while k < x.shape[0]:
        x = x + jnp.where(valid(row, k), shift(x, k), 0.0)
        k *= 2
    return x


def _pick_row(x, r):
    row = lax.broadcasted_iota(jnp.int32, (x.shape[0], 1), 0)
    return jnp.sum(jnp.where(row == r, x, 0.0), axis=0, keepdims=True)


def _log_sigmoid(z):
    return jnp.minimum(z, 0.0) - jnp.log(1.0 + jnp.exp(-jnp.abs(z)))


def _gate_fwd(kvf3, fb, col_block, *, name):
    nb, seq, _ = kvf3.shape
    rows = _chunk_rows(seq)

    def body(f_ref, fb_ref, o_ref):
        carry = jnp.zeros((1, LANE), F32)
        for c in range(seq // rows):
            logf = _log_sigmoid(f_ref[c * rows:(c + 1) * rows, :] + fb_ref[...])
            run = _scan_rows(logf, _down, lambda row, k: row >= k) + carry
            o_ref[c * rows:(c + 1) * rows, :] = run
            carry = _pick_row(run, rows - 1)

    return pl.pallas_call(
        body, name=name, grid=(nb,),
        in_specs=[pl.BlockSpec((None, seq, LANE), lambda b: (b, 0, col_block)),
                  pl.BlockSpec((1, LANE), lambda b: (0, 0))],
        out_specs=pl.BlockSpec((None, seq, LANE), lambda b: (b, 0, 0)),
        out_shape=jax.ShapeDtypeStruct((nb, seq, LANE), F32), compiler_params=_params(),
    )(kvf3, fb)


def _gate_bwd(kvf3, fb, dF3, col_block, heads, *, name):
    nb, seq, _ = kvf3.shape
    rows = _chunk_rows(seq)

    def body(f_ref, fb_ref, d_ref, o_ref, dfb_ref):
        @pl.when(pl.program_id(0) == 0)
        def _():
            dfb_ref[...] = jnp.zeros(dfb_ref.shape, F32)

        lane = lax.broadcasted_iota(jnp.int32, (1, LANE), 1)
        carry = jnp.zeros((1, LANE), F32)
        for c in reversed(range(seq // rows)):
            run = _scan_rows(d_ref[c * rows:(c + 1) * rows, :], _up, lambda row, k: row < rows - k) + carry
            carry = _pick_row(run, 0)
            z = f_ref[c * rows:(c + 1) * rows, :] + fb_ref[...]
            df = jnp.where(lane < heads, run * jax.nn.sigmoid(-z), 0.0)
            o_ref[c * rows:(c + 1) * rows, :] = df
            dfb_ref[...] += jnp.sum(df, axis=0, keepdims=True)

    return pl.pallas_call(
        body, name=name, grid=(nb,),
        in_specs=[pl.BlockSpec((None, seq, LANE), lambda b: (b, 0, col_block)),
                  pl.BlockSpec((1, LANE), lambda b: (0, 0)),
                  pl.BlockSpec((None, seq, LANE), lambda b: (b, 0, 0))],
        out_specs=[pl.BlockSpec((None, seq, LANE), lambda b: (b, 0, 0)), pl.BlockSpec((1, LANE), lambda b: (0, 0))],
        out_shape=[jax.ShapeDtypeStruct((nb, seq, LANE), F32), jax.ShapeDtypeStruct((1, LANE), F32)],
        compiler_params=_params(),
    )(kvf3, fb, dF3)


def _head_masks():
    lane = lax.broadcasted_iota(jnp.int32, (1, LANE), 1)
    return (lane < HEAD_DIM, lane >= HEAD_DIM)


BIAS_TERMS = 3


def _bias_lanes(gsum):
    nb, seq, heads = gsum.shape
    terms, rest = [], gsum
    for _ in range(BIAS_TERMS):
        t = lax.reduce_precision(rest, exponent_bits=8, mantissa_bits=7)
        terms.append(t)
        rest = rest - t
    ones = [jnp.ones_like(gsum)] * BIAS_TERMS

    def lanes(parts):
        z = jnp.stack(parts, axis=-1)
        z = jnp.pad(z, ((0, 0), (0, 0), (0, 0), (0, HEAD_DIM - 2 * BIAS_TERMS)))
        z = z.reshape(nb, seq, heads // 2, 2, HEAD_DIM)[:, :, :, ::-1]
        return z.reshape(nb, seq, heads * HEAD_DIM).astype(BF16)

    return lanes(terms + ones), lanes(ones + [-t for t in terms])


def _fox_scores(q, k, aq, ak, masked):
    qs = (q * ATT_SCALE).astype(BF16)
    qts = [jnp.where(msk, qs, aq) for msk in _head_masks()]
    ss = [_bdot(qt, jnp.where(msk, k, ak), "nt") for qt, msk in zip(qts, _head_masks())]
    if masked:
        t = q.shape[0]
        keep = lax.broadcasted_iota(jnp.int32, (t, t), 0) >= lax.broadcasted_iota(jnp.int32, (t, t), 1)
        ss = [jnp.where(keep, s, NEG_BIG) for s in ss]
    return ss, qts


def _on_blocks(qi, ki, step):
    @pl.when(ki < qi)
    def _():
        step(False)

    @pl.when(ki == qi)
    def _():
        step(True)


def _fox_grid(nblk, tokw, t, q_major):
    if q_major:
        pairs = [(qi, ki) for qi in range(nblk) for ki in range(qi + 1)]
    else:
        pairs = [(qi, ki) for ki in range(nblk) for qi in range(ki, nblk)]
    tables = [jnp.array([p[i] for p in pairs], jnp.int32) for i in (0, 1)]

    def q_spec(off=0, wide=False):
        width = 2 * LANE if wide else LANE
        return pl.BlockSpec((None, t, width), lambda b, p, i, qt, kt: (b, qt[i], p + off))

    def k_spec(off=0):
        return pl.BlockSpec((None, t, LANE), lambda b, p, i, qt, kt: (b, kt[i], p + off))

    return tables, len(pairs), q_spec, k_spec, tokw // LANE


def _lanes(col):
    return jnp.broadcast_to(col, (col.shape[0], LANE))


def _across(stat, width):
    return jnp.tile(stat, (1, width // LANE))


def _fox_fwd(proj3, kvf3, aq3, ak3, tokw, *, name):
    nb, seq, _ = proj3.shape
    t = _pick(seq, 512, LANE)
    tables, n_pairs, q_spec, k_spec, hp0 = _fox_grid(seq // t, tokw, t, True)

    def body(qt_ref, kt_ref, q_ref, k_ref, v_ref, aq_ref, ak_ref, o_ref, lse_ref, m_s, l_s, acc_s):
        i = pl.program_id(2)
        qi, ki = qt_ref[i], kt_ref[i]

        @pl.when(ki == 0)
        def _():
            m_s[...] = jnp.full(m_s.shape, NEG_BIG, F32)
            l_s[...] = jnp.zeros(l_s.shape, F32)
            acc_s[...] = jnp.zeros(acc_s.shape, F32)

        def step(masked):
            v = v_ref[...].astype(BF16)
            ss, _ = _fox_scores(q_ref[...], k_ref[...].astype(BF16), aq_ref[...], ak_ref[...], masked)
            for h, s in enumerate(ss):
                m_old = m_s[h]
                m_new = jnp.maximum(m_old, _lanes(jnp.max(s, axis=-1, keepdims=True)))
                alpha = jnp.exp(m_old - m_new)
                p = jnp.exp(s - _across(m_new, t))
                l_s[h] = alpha * l_s[h] + _lanes(jnp.sum(p, axis=-1, keepdims=True))
                acc_s[h] = alpha * acc_s[h] + _bdot(p, v, "nn")
                m_s[h] = m_new

        _on_blocks(qi, ki, step)

        @pl.when(ki == qi)
        def _():
            o_ref[...] = jnp.where(_head_masks()[0], acc_s[0] / l_s[0], acc_s[1] / l_s[1])
            lse_ref[...] = jnp.concatenate([m_s[0] + jnp.log(l_s[0]), m_s[1] + jnp.log(l_s[1])], axis=1)

    stat = pltpu.VMEM((2, t, LANE), F32)
    return pl.pallas_call(
        body, name=name,
        grid_spec=pltpu.PrefetchScalarGridSpec(
            num_scalar_prefetch=2, grid=(nb, hp0, n_pairs),
            in_specs=[q_spec(), k_spec(), k_spec(hp0), q_spec(), k_spec()],
            out_specs=[q_spec(), q_spec(wide=True)], scratch_shapes=[stat, stat, stat]),
        out_shape=[jax.ShapeDtypeStruct((nb, seq, tokw), F32), jax.ShapeDtypeStruct((nb, seq, 2 * tokw), F32)],
        compiler_params=_params(),
    )(*tables, proj3, kvf3, kvf3, aq3, ak3)


def _fox_bwd_common(q_ref, k_ref, v_ref, aq_ref, ak_ref, do_ref, lse_ref, delta_ref, masked):
    k, v = k_ref[...].astype(BF16), v_ref[...].astype(BF16)
    ss, qts = _fox_scores(q_ref[...], k, aq_ref[...], ak_ref[...], masked)
    do = do_ref[...]
    t = do.shape[0]
    out = []
    for h, (s, qt, msk) in enumerate(zip(ss, qts, _head_masks())):
        doh = jnp.where(msk, do, 0.0).astype(BF16)
        p = jnp.exp(s - _across(lse_ref[:, h * LANE:(h + 1) * LANE], t))
        ds = p * (_bdot(doh, v, "nt") - _across(delta_ref[:, h * LANE:(h + 1) * LANE], t))
        out.append((qt, doh, p, ds))
    return out, k


def _fox_bwd(proj3, kvf3, aq3, ak3, o3, dmixin3, lse3, tokw, *, name):
    nb, seq, _ = proj3.shape
    t = _pick(seq, 512, LANE)
    nblk = seq // t
    tables, n_pairs, q_spec, k_spec, hp0 = _fox_grid(nblk, tokw, t, True)
    whole = pl.BlockSpec((None, seq, LANE), lambda b, p, i, qt, kt: (b, 0, p))
    dfk_spec = pl.BlockSpec((None, None, nblk, SUBLANE, t), lambda b, p, i, qt, kt: (b, p, 0, 0, 0))

    def body(qt_ref, kt_ref, q_ref, k_ref, v_ref, aq_ref, ak_ref, o_ref, do_ref, lse_ref, dq_ref, dk_ref, dv_ref,
             dfk_ref, acc_s, row_s, delta_s):
        i = pl.program_id(2)
        qi, ki = qt_ref[i], kt_ref[i]

        @pl.when(i == 0)
        def _():
            dk_ref[...] = jnp.zeros(dk_ref.shape, F32)
            dv_ref[...] = jnp.zeros(dv_ref.shape, F32)
            dfk_ref[...] = jnp.zeros(dfk_ref.shape, F32)

        @pl.when(ki == 0)
        def _():
            acc_s[...] = jnp.zeros(acc_s.shape, F32)
            row_s[...] = jnp.zeros(row_s.shape, F32)
            prod = do_ref[...] * o_ref[...]
            delta_s[...] = jnp.concatenate(
                [_lanes(jnp.sum(jnp.where(msk, prod, 0.0), axis=-1, keepdims=True)) for msk in _head_masks()],
                axis=1)

        def step(masked):
            heads, k = _fox_bwd_common(q_ref, k_ref, v_ref, aq_ref, ak_ref, do_ref, lse_ref, delta_s, masked)
            rows = pl.ds(pl.multiple_of(ki * t, t), t)
            for h, ((qt, doh, p, ds), msk) in enumerate(zip(heads, _head_masks())):
                acc_s[h] += _bdot(ds, k, "nn")
                row_s[h] += _lanes(jnp.sum(ds, axis=-1, keepdims=True))
                dv_ref[rows, :] += _bdot(p, doh, "tn")
                dk_ref[rows, :] += jnp.where(msk, _bdot(ds, qt, "tn"), 0.0)
                dfk_ref[ki, h:h + 1, :] -= jnp.sum(ds, axis=0, keepdims=True)

        _on_blocks(qi, ki, step)

        @pl.when(ki == qi)
        def _():
            dq_ref[...] = jnp.where(_head_masks()[0], acc_s[0], acc_s[1]) * ATT_SCALE
            for h in range(2):
                dfk_ref[qi, 2 + h:3 + h, :] = row_s[h].T[0:1, :]

    out = jax.ShapeDtypeStruct((nb, seq, tokw), F32)
    stat = pltpu.VMEM((2, t, LANE), F32)
    return pl.pallas_call(
        body, name=name,
        grid_spec=pltpu.PrefetchScalarGridSpec(
            num_scalar_prefetch=2, grid=(nb, hp0, n_pairs),
            in_specs=[q_spec(), k_spec(), k_spec(hp0), q_spec(), k_spec(), q_spec(), q_spec(), q_spec(wide=True)],
            out_specs=[q_spec(), whole, whole, dfk_spec],
            scratch_shapes=[stat, stat, pltpu.VMEM((t, 2 * LANE), F32)]),
        out_shape=[out, out, out, jax.ShapeDtypeStruct((nb, hp0, nblk, SUBLANE, t), F32)],
        compiler_params=_params(),
    )(*tables, proj3, kvf3, kvf3, aq3, ak3, o3, dmixin3, lse3)


def _peer(k):
    x, y, c = lax.axis_index("x"), lax.axis_index("y"), lax.axis_index("c")
    return (1 - x if k & 4 else x, 1 - y if k & 2 else y, 1 - c if k & 1 else c)


def _dev_index(p):
    return 4 * p[0] + 2 * p[1] + p[2]


_HBM = pl.BlockSpec(memory_space=pltpu.HBM)
CHIP_RELATIONS = (2, 4, 6)


def _chip_index(p):
    return 2 * p[0] + p[1]


def _run_copies(sends, recvs):
    for cp in sends:
        cp.start()
    for cp in recvs:
        cp.wait_recv()
    for cp in sends:
        cp.wait_send()


def _gather_shards(whole, halved, side_by_side, *, name):
    nw, nh = len(whole), len(halved) + len(side_by_side)
    n = nw + nh
    n_sem = 3 * nw + 6 * nh

    def body(*refs):
        ins, outs, send_sems, recv_sems, local_sems = refs[:n], refs[n:2 * n], refs[2 * n], refs[2 * n + 1], refs[-1]
        me, sib = _peer(0), _peer(1)
        q, c = _chip_index(me), me[2]
        own = []

        def copy(src, dst, s, to):
            return pltpu.make_async_remote_copy(src_ref=src, dst_ref=dst, send_sem=send_sems.at[s],
                                                recv_sem=recv_sems.at[s], device_id=to, device_id_type=MESH_T)

        sends, recvs, passes = [], [], []
        for j, k in enumerate(CHIP_RELATIONS):
            peer = _peer(k)
            pq = _chip_index(peer)
            for i in range(nw):
                sends.append(copy(ins[i], outs[i].at[q], 3 * i + j, peer))
                recvs.append(copy(ins[i], outs[i].at[pq], 3 * i + j, peer))
            for i in range(nh):
                src, out, s = ins[nw + i], outs[nw + i], 3 * nw + 6 * i + j
                if i < len(halved):
                    place = lambda chip, half, out=out: out.at[chip, half]
                else:
                    cols = src.shape[-1]
                    place = lambda chip, half, out=out, cols=cols: out.at[
                        half, :, pl.ds(pl.multiple_of(chip * cols, LANE), cols)]
                    if j == 0:
                        own += [pltpu.make_async_copy(src.at[h], place(q, h), local_sems.at[len(own) + h])
                                for h in range(2)]
                sends.append(copy(src.at[c], place(q, c), s, peer))
                passes.append((copy(src.at[c], place(pq, c), s, peer), copy(place(pq, c), place(pq, c), s + 3, sib),
                               copy(place(pq, c), place(pq, 1 - c), s + 3, sib)))
        for cp in sends + own:
            cp.start()
        for arrival, hand_over, _ in passes:
            arrival.wait_recv()
            hand_over.start()
        for cp in recvs:
            cp.wait_recv()
        for _, _, from_sibling in passes:
            from_sibling.wait_recv()
        for cp in sends + [hand_over for _, hand_over, _ in passes]:
            cp.wait_send()
        for cp in own:
            cp.wait()

    arrays = list(whole) + list(halved) + list(side_by_side)
    return pl.pallas_call(
        body, name=name, in_specs=[_HBM] * n, out_specs=[_HBM] * n,
        out_shape=[jax.ShapeDtypeStruct((N_CHIP,) + a.shape, a.dtype) for a in list(whole) + list(halved)]
        + [jax.ShapeDtypeStruct(a.shape[:-1] + (N_CHIP * a.shape[-1],), a.dtype) for a in side_by_side],
        scratch_shapes=[pltpu.SemaphoreType.DMA((n_sem,)), pltpu.SemaphoreType.DMA((n_sem,)),
                        pltpu.SemaphoreType.DMA((2 * len(side_by_side),))],
    )(*arrays)


def _to_sibling(grads, *, name):
    n = len(grads)

    def body(*refs):
        ins, outs, send_sems, recv_sems = refs[:n], refs[n:2 * n], refs[2 * n], refs[2 * n + 1]
        c = lax.axis_index("c")
        sends = [pltpu.make_async_remote_copy(src_ref=ins[i].at[:, 1 - c], dst_ref=outs[i], send_sem=send_sems.at[i],
                                              recv_sem=recv_sems.at[i], device_id=_peer(1), device_id_type=MESH_T)
                 for i in range(n)]
        _run_copies(sends, sends)

    return pl.pallas_call(
        body, name=name, in_specs=[_HBM] * n, out_specs=[_HBM] * n,
        out_shape=[jax.ShapeDtypeStruct(g.shape[:1] + g.shape[2:], g.dtype) for g in grads],
        scratch_shapes=[pltpu.SemaphoreType.DMA((n,)), pltpu.SemaphoreType.DMA((n,))],
    )(*grads)


def _pair_add(grads, from_sibling, qc, *, name):
    _, _, rows, cols = grads.shape
    tile = _pick(rows, 1024, SUBLANE)

    def body(qc_ref, g_ref, s_ref, o_ref):
        del qc_ref
        o_ref[...] = g_ref[...] + s_ref[...]

    spec = pl.BlockSpec((None, tile, cols), lambda j, i, qc: (j, i, 0))
    return pl.pallas_call(
        body, name=name,
        grid_spec=pltpu.PrefetchScalarGridSpec(
            num_scalar_prefetch=1, grid=(N_CHIP, rows // tile),
            in_specs=[pl.BlockSpec((None, None, tile, cols), lambda j, i, qc: (j, qc[1], i, 0)), spec],
            out_specs=spec),
        out_shape=jax.ShapeDtypeStruct((N_CHIP, rows, cols), F32), compiler_params=_params(),
    )(qc, grads, from_sibling)


def _to_chips(sums, small, *, name):
    n = len(sums)

    def body(*refs):
        ins, small_ref, outs, small_out = refs[:n], refs[n], refs[n + 1:2 * n + 1], refs[2 * n + 1]
        send_sems, recv_sems, local_sem = refs[2 * n + 2:]
        me = _peer(0)

        def copy(src, dst, s, to):
            return pltpu.make_async_remote_copy(src_ref=src, dst_ref=dst, send_sem=send_sems.at[s],
                                                recv_sem=recv_sems.at[s], device_id=to, device_id_type=MESH_T)

        mine = pltpu.make_async_copy(small_ref.at[_dev_index(me)], small_out.at[_dev_index(me)], local_sem)
        mine.start()
        sends, recvs = [], []
        for j, k in enumerate(CHIP_RELATIONS):
            peer = _peer(k)
            for i in range(n):
                src = ins[i].at[_chip_index(peer)]
                sends.append(copy(src, outs[i].at[j], 3 * i + j, peer))
                recvs.append(copy(src, outs[i].at[j], 3 * i + j, peer))
        for k in range(1, N_DEV):
            peer = _peer(k)
            src = small_ref.at[_dev_index(peer)]
            sends.append(copy(src, small_out.at[_dev_index(me)], 3 * n + k - 1, peer))
            recvs.append(copy(src, small_out.at[_dev_index(peer)], 3 * n + k - 1, peer))
        _run_copies(sends, recvs)
        mine.wait()

    n_sem = 3 * n + N_DEV - 1
    return pl.pallas_call(
        body, name=name, in_specs=[_HBM] * (n + 1), out_specs=[_HBM] * (n + 1),
        out_shape=[jax.ShapeDtypeStruct((3,) + g.shape[1:], g.dtype) for g in sums]
        + [jax.ShapeDtypeStruct(small.shape, small.dtype)],
        scratch_shapes=[pltpu.SemaphoreType.DMA((n_sem,)), pltpu.SemaphoreType.DMA((n_sem,)),
                        pltpu.SemaphoreType.DMA],
    )(*sums, small)


def _swap_halves(arrays, *, name):
    n = len(arrays)

    def body(*refs):
        outs, send_sems, recv_sems = refs[n:2 * n], refs[2 * n], refs[2 * n + 1]
        c = lax.axis_index("c")
        sib = _peer(1)
        sends, recvs = [], []
        for i in range(n):
            sem = dict(send_sem=send_sems.at[i], recv_sem=recv_sems.at[i], device_id=sib, device_id_type=MESH_T)
            sends.append(pltpu.make_async_remote_copy(src_ref=outs[i].at[c], dst_ref=outs[i].at[c], **sem))
            recvs.append(pltpu.make_async_remote_copy(src_ref=outs[i].at[c], dst_ref=outs[i].at[1 - c], **sem))
        _run_copies(sends, recvs)

    return pl.pallas_call(
        body, name=name, in_specs=[_HBM] * n, out_specs=[_HBM] * n,
        out_shape=[jax.ShapeDtypeStruct(a.shape, a.dtype) for a in arrays],
        input_output_aliases={i: i for i in range(n)},
        scratch_shapes=[pltpu.SemaphoreType.DMA((n,)), pltpu.SemaphoreType.DMA((n,))],
    )(*arrays)


def _adam_math(g, w, m, v):
    bc1 = 1.0 - ADAM_B1 ** ADAM_STEP
    bc2 = 1.0 - ADAM_B2 ** ADAM_STEP
    m_new = ADAM_B1 * m + (1.0 - ADAM_B1) * g
    v_new = ADAM_B2 * v + (1.0 - ADAM_B2) * (g * g)
    delta = -ADAM_LR * ((m_new / bc1) / (jnp.sqrt(v_new / bc2) + ADAM_EPS) + ADAM_WD * w)
    return delta, m_new, v_new


def _reduce_half(sums, parts, qc, *, name):
    _, rows, cols = sums.shape
    tile = _pick(rows, 1024, SUBLANE)
    n_parts = parts.shape[0]

    def body(qc_ref, g_ref, p_ref, o_ref):
        del qc_ref
        g = g_ref[...]
        for k in range(n_parts):
            g = g + p_ref[k]
        o_ref[...] = g

    return pl.pallas_call(
        body, name=name,
        grid_spec=pltpu.PrefetchScalarGridSpec(
            num_scalar_prefetch=1, grid=(rows // tile,),
            in_specs=[pl.BlockSpec((None, tile, cols), lambda i, qc: (qc[0], i, 0)),
                      pl.BlockSpec((n_parts, tile, cols), lambda i, qc: (0, i, 0))],
            out_specs=pl.BlockSpec((None, tile, cols), lambda i, qc: (qc[1], i, 0))),
        out_shape=jax.ShapeDtypeStruct((2, rows, cols), F32), compiler_params=_params(),
    )(qc, sums, parts)


def _adamw_shard(g, w, m, v, *, name):
    _, rows, cols = g.shape
    tile = _pick(rows, 512, SUBLANE)

    def body(g_ref, w_ref, m_ref, v_ref, do_ref, mo_ref, vo_ref):
        do_ref[...], mo_ref[...], vo_ref[...] = _adam_math(g_ref[...], w_ref[...], m_ref[...], v_ref[...])

    spec = pl.BlockSpec((None, tile, cols), lambda h, i: (h, i, 0))
    return pl.pallas_call(
        body, name=name, grid=(2, rows // tile), in_specs=[spec] * 4, out_specs=[spec] * 3,
        out_shape=[jax.ShapeDtypeStruct(g.shape, F32)] * 3, compiler_params=_params(),
    )(g, w, m, v)


def _adamw(parts, w, m, v, *, name):
    _, rows, cols = parts.shape
    tile = _pick(rows, 256, SUBLANE)

    def body(p_ref, w_ref, m_ref, v_ref, o_ref):
        g = p_ref[0]
        for i in range(1, N_DEV):
            g = g + p_ref[i]
        delta, m_new, v_new = _adam_math(g, w_ref[...], m_ref[...], v_ref[...])
        o_ref[0] = g
        o_ref[1] = delta
        o_ref[2] = m_new
        o_ref[3] = v_new

    spec = pl.BlockSpec((tile, cols), lambda i: (i, 0))
    return pl.pallas_call(
        body, name=name, grid=(rows // tile,),
        in_specs=[pl.BlockSpec((N_DEV, tile, cols), lambda i: (0, i, 0)), spec, spec, spec],
        out_specs=pl.BlockSpec((4, tile, cols), lambda i: (0, i, 0)),
        out_shape=jax.ShapeDtypeStruct((4, rows, cols), F32), compiler_params=_params(),
    )(parts, w, m, v)


def _layout(shapes, names, align):
    out, off = [], 0
    for n in names:
        size = math.prod(shapes[n])
        out.append((n, tuple(shapes[n]), off, size))
        off += _round_up(size, align)
    return out, off


def _pack(arrays, layout, total, lead=()):
    parts = []
    for i, (n, _, off, size) in enumerate(layout):
        end = layout[i + 1][2] if i + 1 < len(layout) else total
        flat = arrays[n].reshape(lead + (size,))
        if end - off > size:
            flat = jnp.pad(flat, [(0, 0)] * len(lead) + [(0, end - off - size)])
        parts.append(flat)
    return jnp.concatenate(parts, axis=len(lead))


def _unpack(flat, layout, lead=()):
    return {n: flat[..., off:off + size].reshape(lead + shape) for n, shape, off, size in layout}


def _to_shards(full, axis):
    shp = full.shape
    return jnp.moveaxis(full.reshape(shp[:axis] + (N_CHIP, shp[axis] // N_CHIP) + shp[axis + 1:]), axis, 0)


def _from_shards(shards, axis):
    x = jnp.moveaxis(shards, 0, axis)
    shp = x.shape
    return x.reshape(shp[:axis] + (shp[axis] * shp[axis + 1],) + shp[axis + 2:])


def _pad_cols(w, per, padded):
    lead = w.shape[:-1]
    x = w.reshape(lead + (-1, per))
    x = jnp.pad(x, [(0, 0)] * len(lead) + [(0, 0), (0, padded - per)])
    return x.reshape(lead + (-1,))


def _unpad_cols(w, per, padded):
    lead = w.shape[:-1]
    return w.reshape(lead + (-1, padded))[..., :per].reshape(lead + (-1,))


def _local_step(x, mem, target, W):
    nb, seq, d = x.shape
    n = nb * seq
    tokw = d - MEM_WIDTH
    heads = tokw // HEAD_DIM
    mlen = mem.shape[1]
    per = W["ffn_w_down"].shape[1] // 2
    per_p = _round_up(per, LANE)
    fp = 2 * per_p
    kvw = 2 * tokw + heads
    kvp = 2 * tokw + LANE
    gate_block = 2 * tokw // LANE

    x2d = x.reshape(n, d)
    mem2d = mem.reshape(nb * mlen, d)
    t2d = target.reshape(n, d)
    row = lambda a: a.reshape(1, -1)
    ones_tok = jnp.ones((1, tokw), F32)

    pool_bd = jax.scipy.linalg.block_diag(*[W["a_pool_w"][0, i] for i in range(len(POOL_WINDOWS))]).astype(BF16)
    kv_w = jnp.pad(W["kv_w"], ((0, 0), (0, kvp - kvw)))
    fb = jnp.pad(W["f_b"], (0, LANE - heads)).reshape(1, LANE)
    w_up = W.get("ffn_w_up_padded")
    if w_up is None:
        w_up = jnp.stack([_pad_cols(W["ffn_w_up"][l], per, per_p) for l in range(DEPTH)])
    w_down = [jnp.pad(W["ffn_w_down"][l].reshape(2, per, d), ((0, 0), (0, per_p - per), (0, 0))).reshape(fp, d)
              for l in range(DEPTH)]
    conv_w = [jnp.pad(_pad_cols(W["ffn_conv_w"][l], per, per_p), ((0, SUBLANE - CONV_WIDTH), (0, 0)))
              for l in range(DEPTH)]
    conv_b = [_pad_cols(W["ffn_conv_b"][l], per, per_p).reshape(1, 2 * fp) for l in range(DEPTH)]
    w_in = [W["a_w_in"][0], W["b_w_q"][0]]
    w_out = [W["a_w_out"][0], W["b_w_out"][0]]

    saved = []
    cur = cur_mm = x2d
    for l in range(DEPTH):
        s = {"x_in": cur, "x_in_mm": cur_mm}
        memkv = _mm(mem2d, W["mem_w_kv"][l], "nn", name=f"memkv{l}").reshape(nb, mlen, 2 * MEM_WIDTH)
        if l == 0:
            proj = _mm(cur_mm, w_in[l], "nn", name="proj0")
            pooled = _pool_fwd(proj.reshape(nb, seq, d), tokw, name="pool_fwd").reshape(n, tokw)
            tok = _mm(pooled, pool_bd, "nn", name="pool_mix")
            scale = W["a_pool_scale"].reshape(1, tokw)
            s.update(pooled=pooled, mixed=tok, scale=scale)
        else:
            kvf = _mm(cur_mm, kv_w, "nn", tn=kvp, name="kvf")
            kvf3 = kvf.reshape(nb, seq, kvp)
            gsum = _gate_fwd(kvf3, fb, gate_block, name="gate_fwd")[:, :, :heads]
            aq3, ak3 = _bias_lanes(gsum)
            proj = _mm(cur_mm, w_in[l], "nn", name="proj1")
            o3, lse3 = _fox_fwd(proj.reshape(nb, seq, d), kvf3, aq3, ak3, tokw, name="fox_fwd")
            tok = o3.reshape(n, tokw)
            scale = ones_tok
            s.update(kvf3=kvf3, aq3=aq3, ak3=ak3, o3=o3, lse3=lse3)
        mixin = _memattn_fwd(tok, proj, memkv, scale, seq=seq, name=f"memattn_fwd{l}")
        mix, x1, x1_mm = _mm(mixin, w_out[l], "nn", ln=(cur, row(W["ln1_g"][l]), row(W["ln1_b"][l])), name=f"mix{l}")
        up = _mm(x1_mm, w_up, "nn", b_lead=l, tn=per_p, col_major=True, name=f"ffn_up{l}")
        act = _convgate_fwd(up.reshape(nb, seq, 2 * fp), conv_w[l], conv_b[l], name=f"convgate_fwd{l}")
        act = act.reshape(n, fp)
        if l + 1 < DEPTH:
            ffn, cur, cur_mm = _mm(act, w_down[l], "nn", tk=fp, ln=(x1, row(W["ln2_g"][l]), row(W["ln2_b"][l])),
                                   name=f"ffn_down{l}")
        else:
            ffn = _mm(act, w_down[l], "nn", tk=fp, name=f"ffn_down{l}")
        s.update(proj=proj, memkv=memkv, mixin=mixin, mix=mix, x1=x1, x1_mm=x1_mm, up=up, act=act, ffn=ffn)
        saved.append(s)

    G = {}
    ln_g = {k: [None] * DEPTH for k in ("ln1_g", "ln1_b", "ln2_g", "ln2_b")}
    stack = {k: [None] * DEPTH for k in ("mem_w_kv", "ffn_w_up", "ffn_conv_w", "ffn_conv_b", "ffn_w_down")}
    dx_terms = None
    loss = None
    for l in reversed(range(DEPTH)):
        s = saved[l]
        g2 = row(W["ln2_g"][l])
        if l == DEPTH - 1:
            dres, dffn, loss, dg, db = _final_ln_loss(s["x1"], s["ffn"], t2d, g2, row(W["ln2_b"][l]),
                                                      name="final_ln_loss")
        else:
            dres, dffn, dg, db = _ln_bwd(s["x1"], s["ffn"], g2, dx_terms, name=f"ln2_bwd{l}")
        ln_g["ln2_g"][l], ln_g["ln2_b"][l] = dg[0], db[0]
        dact = _mm(dffn, w_down[l], "nt", tn=fp, name=f"ffn_down_dx{l}")
        stack["ffn_w_down"][l] = _mm(s["act"], dffn, "tn", tm=per_p, tk=DW_ROWS, trim=("rows", per),
                                     name=f"ffn_down_dw{l}")
        du3, dg3, dcw, dcb = _convgate_bwd(s["up"].reshape(nb, seq, 2 * fp), dact.reshape(nb, seq, fp), conv_w[l],
                                           conv_b[l], name=f"convgate_bwd{l}")
        du, dgt = du3.reshape(n, fp), dg3.reshape(n, fp)
        dx1_u = _mm(du, w_up, "nt", b_lead=l, tk=fp, name=f"ffn_up_dx_u{l}")
        dx1 = _mm(dgt, w_up, "nt", b_lead=l, tk=fp, b_col0=fp, adds=[dx1_u, dres], name=f"ffn_up_dx_g{l}")
        stack["ffn_w_up"][l] = [_mm(s["x1_mm"], part, "tn", tm=d, tn=per_p, tk=DW_ROWS, trim=("cols", per),
                                    name=f"ffn_up_dw_{nm}{l}") for nm, part in (("u", du), ("g", dgt))]
        stack["ffn_conv_w"][l] = dcw[:CONV_WIDTH]
        stack["ffn_conv_b"][l] = dcb[0]
        dres1, dmix, dg, db = _ln_bwd(s["x_in"], s["mix"], row(W["ln1_g"][l]), [dx1], name=f"ln1_bwd{l}")
        ln_g["ln1_g"][l], ln_g["ln1_b"][l] = dg[0], db[0]
        dmixin = _mm(dmix, w_out[l], "nt", name=f"mix_dx{l}")
        d_w_out = _mm(s["mixin"], dmix, "tn", tm=d, tk=DW_ROWS, name=f"mix_dw{l}")
        if l == 0:
            G["a_w_out"] = d_w_out[None]
            dmixed, dscale = _scale_bwd(dmixin, s["mixed"], s["scale"], name="scale_bwd")
            G["a_pool_scale"] = dscale
            dpooled = _mm(dmixed, pool_bd, "nt", name="pool_mix_dx")
            dpw = _mm(s["pooled"], dmixed, "tn", tm=tokw, tk=DW_ROWS, name="pool_mix_dw")
            grp = tokw // len(POOL_WINDOWS)
            G["a_pool_w"] = jnp.stack([dpw[i * grp:(i + 1) * grp, i * grp:(i + 1) * grp]
                                       for i in range(len(POOL_WINDOWS))])[None]
            dtok = _pool_bwd(dpooled.reshape(nb, seq, tokw), name="pool_bwd").reshape(n, tokw)
            extra = []
        else:
            G["b_w_out"] = d_w_out[None]
            p3 = s["proj"].reshape(nb, seq, d)
            dm3 = dmixin.reshape(nb, seq, d)
            dq3, dk3, dv3, dfk = _fox_bwd(p3, s["kvf3"], s["aq3"], s["ak3"], s["o3"], dm3, s["lse3"], tokw,
                                          name="fox_bwd")
            dtok = dq3.reshape(n, tokw)
            dfk = jnp.swapaxes(dfk[:, :, :, 0:2, :] + dfk[:, :, :, 2:4, :], 2, 3).reshape(nb, heads, seq)
            dgsum = jnp.swapaxes(dfk, 1, 2)
            dgsum = jnp.pad(dgsum, ((0, 0), (0, 0), (0, LANE - heads)))
            df3, dfb = _gate_bwd(s["kvf3"], fb, dgsum, gate_block, heads, name="gate_bwd")
            G["f_b"] = dfb[0, :heads]
            dkvf = [(dk3.reshape(n, tokw), 0, "k"), (dv3.reshape(n, tokw), tokw, "v"),
                    (df3.reshape(n, LANE), 2 * tokw, "f")]
            dx_kv = []
            for part, col0, nm in dkvf:
                dx_kv = [_mm(part, kv_w, "nt", b_col0=col0, adds=dx_kv, name=f"kvf_dx_{nm}")]
            extra = dx_kv
            G["kv_w"] = jnp.concatenate([_mm(s["x_in_mm"], part, "tn", tm=d, tk=DW_ROWS, name=f"kvf_dw_{nm}")
                                         for part, _, nm in dkvf], axis=1)[:, :kvw]
        dproj, dmemkv = _memattn_bwd(dmixin, dtok, s["proj"], s["memkv"], seq=seq, name=f"memattn_bwd{l}")
        stack["mem_w_kv"][l] = _mm(mem2d, dmemkv.reshape(nb * mlen, 2 * MEM_WIDTH), "tn", tm=d, tk=DW_ROWS,
                                   name=f"memkv_dw{l}")
        G["a_w_in" if l == 0 else "b_w_q"] = _mm(s["x_in_mm"], dproj, "tn", tm=d, tk=DW_ROWS, name=f"proj_dw{l}")[None]
        if l == 0:
            grad_x = _mm(dproj, w_in[l], "nt", adds=[dres1], name="proj_dx0")
        else:
            dx_terms = [_mm(dproj, w_in[l], "nt", adds=[dres1] + extra, name="proj_dx1")]
    for k, v in ln_g.items():
        G[k] = jnp.stack(v)
    G["mem_w_kv"] = jnp.stack(stack["mem_w_kv"])
    G["ffn_w_up"] = jnp.stack([jnp.concatenate(ug, axis=0) for ug in stack["ffn_w_up"]], axis=1)
    G["ffn_conv_w"] = jnp.stack([_unpad_cols(g, per, per_p) for g in stack["ffn_conv_w"]])
    G["ffn_conv_b"] = jnp.stack([_unpad_cols(g, per, per_p) for g in stack["ffn_conv_b"]])
    G["ffn_w_down"] = jnp.stack([g.reshape(N_CHIP, per // 2, d) for g in stack["ffn_w_down"]], axis=1)
    return loss[0, 0], grad_x.reshape(nb, seq, d), G


def kernel(x, mem, a_w_in, a_pool_w, a_pool_scale, a_w_out, b_w_q, b_w_out, kv_w, f_b, mem_w_kv, ln1_g, ln1_b, ln2_g, ln2_b, ffn_w_up, ffn_conv_w, ffn_conv_b, ffn_w_down, loss_target, m_a_w_in, m_a_pool_w, m_a_pool_scale, m_a_w_out, m_b_w_q, m_b_w_out, m_kv_w, m_f_b, m_mem_w_kv, m_ln1_g, m_ln1_b, m_ln2_g, m_ln2_b, m_ffn_w_up, m_ffn_conv_w, m_ffn_conv_b, m_ffn_w_down, v_a_w_in, v_a_pool_w, v_a_pool_scale, v_a_w_out, v_b_w_q, v_b_w_out, v_kv_w, v_f_b, v_mem_w_kv, v_ln1_g, v_ln1_b, v_ln2_g, v_ln2_b, v_ffn_w_up, v_ffn_conv_w, v_ffn_conv_b, v_ffn_w_down):
    w_loc = dict(a_w_in=a_w_in, a_pool_w=a_pool_w, a_pool_scale=a_pool_scale, a_w_out=a_w_out, b_w_q=b_w_q,
                 b_w_out=b_w_out, kv_w=kv_w, f_b=f_b, mem_w_kv=mem_w_kv, ln1_g=ln1_g, ln1_b=ln1_b, ln2_g=ln2_g,
                 ln2_b=ln2_b, ffn_w_up=ffn_w_up, ffn_conv_w=ffn_conv_w, ffn_conv_b=ffn_conv_b, ffn_w_down=ffn_w_down)
    m_loc = dict(a_w_in=m_a_w_in, a_pool_w=m_a_pool_w, a_pool_scale=m_a_pool_scale, a_w_out=m_a_w_out,
                 b_w_q=m_b_w_q, b_w_out=m_b_w_out, kv_w=m_kv_w, f_b=m_f_b, mem_w_kv=m_mem_w_kv, ln1_g=m_ln1_g,
                 ln1_b=m_ln1_b, ln2_g=m_ln2_g, ln2_b=m_ln2_b, ffn_w_up=m_ffn_w_up, ffn_conv_w=m_ffn_conv_w,
                 ffn_conv_b=m_ffn_conv_b, ffn_w_down=m_ffn_w_down)
    v_loc = dict(a_w_in=v_a_w_in, a_pool_w=v_a_pool_w, a_pool_scale=v_a_pool_scale, a_w_out=v_a_w_out,
                 b_w_q=v_b_w_q, b_w_out=v_b_w_out, kv_w=v_kv_w, f_b=v_f_b, mem_w_kv=v_mem_w_kv, ln1_g=v_ln1_g,
                 ln1_b=v_ln1_b, ln2_g=v_ln2_g, ln2_b=v_ln2_b, ffn_w_up=v_ffn_w_up, ffn_conv_w=v_ffn_conv_w,
                 ffn_conv_b=v_ffn_conv_b, ffn_w_down=v_ffn_w_down)
    x_i, y_i, c = lax.axis_index("x"), lax.axis_index("y"), lax.axis_index("c")
    q = 2 * x_i + y_i
    qc = jnp.stack([q, c]).astype(jnp.int32)
    shapes = {k: v.shape for k, v in w_loc.items()}

    def halves(a):
        if a.ndim == 3 and a.shape[0] == 2:
            return a
        rows = math.prod(a.shape[:-1])
        return a.reshape(2, rows // 2, a.shape[-1])

    own = {k: (w_loc[k] if k in GATHER_F32 else w_loc[k].astype(BF16)) for k in SHARDED}
    by_chip = ("a_pool_scale",) + tuple(k for k in BIG if k != "ffn_w_up")
    per = shapes["ffn_w_up"][-1]
    up_own = jnp.pad(own["ffn_w_up"], ((0, 0), (0, 0), (0, _round_up(per, LANE) - per)))
    gathered = _gather_shards([own["a_pool_scale"]], [halves(own[k]) for k in by_chip[1:]], [up_own],
                              name="gather_weights")
    W = {k: _from_shards(lax.dynamic_update_slice_in_dim(g.reshape((N_CHIP,) + shapes[k]), own[k][None], q, axis=0),
                         SHARD_AXIS[k])
         for k, g in zip(by_chip, gathered)}
    W["ffn_w_up_padded"] = gathered[-1]
    for k in REPLICATED:
        W[k] = w_loc[k]

    loss_part, grad_x, G = _local_step(x, mem, loss_target, W)

    g_chip = [G[k] if k in GRADS_BY_CHIP else _to_shards(G[k], SHARD_AXIS[k]) for k in BIG]
    g_chip = [g.reshape((N_CHIP,) + halves(w_loc[k]).shape) for k, g in zip(BIG, g_chip)]
    lay_r, tot_r = _layout(shapes, REPLICATED, PACK_COLS)
    rep_rows = tot_r // PACK_COLS
    rows = _round_up(rep_rows + 2, SUBLANE)
    scale_w = shapes["a_pool_scale"][-1]

    def small(rep, scale_row, scalar):
        lead = scale_row.shape[:-2]
        pad = [(0, 0)] * len(lead)
        rep = jnp.broadcast_to(_pack(rep, lay_r, tot_r).reshape(rep_rows, PACK_COLS), lead + (rep_rows, PACK_COLS))
        scale_row = jnp.pad(scale_row, pad + [(0, 0), (0, PACK_COLS - scale_w)])
        last = jnp.broadcast_to(jnp.pad(scalar.reshape(1, 1), [(0, rows - rep_rows - 2), (0, PACK_COLS - 1)]),
                                lead + (rows - rep_rows - 1, PACK_COLS))
        return jnp.concatenate([rep, scale_row, last], axis=-2)

    g_scale = jnp.repeat(_to_shards(G["a_pool_scale"], 1), 2, axis=0)
    zero = jnp.zeros((), F32)

    from_sib = _to_sibling(g_chip, name="grads_to_sibling")
    sums = [_pair_add(g, s, qc, name=f"pair_add_{k}") for k, g, s in zip(BIG, g_chip, from_sib)]
    *parts, sm_parts = _to_chips(sums, small(G, g_scale, loss_part), name="scatter_grads")
    grads = _swap_halves([_reduce_half(g, p, qc, name=f"reduce_{k}") for k, g, p in zip(BIG, sums, parts)],
                         name="swap_halves")
    out = {}
    for k, g in zip(BIG, grads):
        upd = _adamw_shard(g, halves(w_loc[k]), halves(m_loc[k]), halves(v_loc[k]), name=f"adamw_{k}")
        out[k] = [r.reshape(shapes[k]) for r in (g, *upd)]
    sm = _adamw(sm_parts, *[small(d, d["a_pool_scale"], zero) for d in (w_loc, m_loc, v_loc)], name="adamw_small")
    loss = sm[0, rep_rows + 1, 0]
    out_r = _unpack(sm[:, :rep_rows].reshape(4, tot_r), lay_r, lead=(4,))
    for k in REPLICATED:
        out[k] = [out_r[k][a] for a in range(4)]
    out["a_pool_scale"] = [sm[a, rep_rows:rep_rows + 1, :scale_w] for a in range(4)]

    outs = [loss, grad_x]
    for a in range(4):
        for k in WEIGHTS:
            outs.append(out[k][a])
    return tuple(outs)
```

```python
import functools
import math

import jax
import jax.numpy as jnp
from jax import lax
from jax.experimental import pallas as pl
from jax.experimental.pallas import tpu as pltpu

F32 = jnp.float32
BF16 = jnp.bfloat16

HEAD_DIM = 64
MEM_HEADS = 4
MEM_WIDTH = MEM_HEADS * HEAD_DIM
POOL_WINDOWS = (2, 4, 8, 16)
MAX_WINDOW = 16
CONV_WIDTH = 3
DEPTH = 2
DN_ALPHA = (2.0 * DEPTH) ** 0.25
LN_EPS = 1e-5
ATT_SCALE = HEAD_DIM ** -0.5
NEG_BIG = -1e30

ADAM_LR = 0.001
ADAM_B1 = 0.9
ADAM_B2 = 0.999
ADAM_EPS = 1e-08
ADAM_WD = 0.01
ADAM_STEP = 10

LANE = 128
SUBLANE = 8
PACK_COLS = 1024
DW_ROWS = 1024
VMEM_LIMIT = 56 * 1024 * 1024
N_DEV = 8
N_CHIP = 4
MESH_T = pl.DeviceIdType.MESH

SHARDED = ("a_w_in", "a_pool_scale", "a_w_out", "b_w_q", "b_w_out", "kv_w", "mem_w_kv", "ffn_w_up",
           "ffn_conv_w", "ffn_w_down")
SHARD_AXIS = {"a_w_in": 1, "a_pool_scale": 1, "a_w_out": 1, "b_w_q": 1, "b_w_out": 1, "kv_w": 1, "mem_w_kv": 1,
              "ffn_w_up": 2, "ffn_conv_w": 2, "ffn_w_down": 1}
GATHER_F32 = ("a_pool_scale", "ffn_conv_w")
BIG = tuple(k for k in SHARDED if k != "a_pool_scale")
GRADS_BY_CHIP = ("ffn_w_up", "ffn_w_down")
REPLICATED = ("a_pool_w", "f_b", "ln1_g", "ln1_b", "ln2_g", "ln2_b", "ffn_conv_b")
WEIGHTS = ("a_w_in", "a_pool_w", "a_pool_scale", "a_w_out", "b_w_q", "b_w_out", "kv_w", "f_b", "mem_w_kv",
           "ln1_g", "ln1_b", "ln2_g", "ln2_b", "ffn_w_up", "ffn_conv_w", "ffn_conv_b", "ffn_w_down")


def _round_up(n, m):
    return -(-n // m) * m


def _pick(dim, pref, unit=LANE):
    if dim <= pref:
        return dim
    t = (pref // unit) * unit
    while t >= unit:
        if dim % t == 0:
            return t
        t -= unit
    raise ValueError(f"no tile for {dim} <= {pref}")


def _params():
    return pltpu.CompilerParams(vmem_limit_bytes=VMEM_LIMIT)


_DIMS = {"nn": ((1,), (0,)), "nt": ((1,), (1,)), "tn": ((0,), (0,))}


def _bdot(a, b, mode):
    return lax.dot_general(a.astype(BF16), b.astype(BF16), (_DIMS[mode], ((), ())), preferred_element_type=F32)


def _mm(a, b, mode, *, name, tm=512, tn=1024, tk=2048, adds=(), b_col0=0, trim=None, col_major=False, b_lead=None,
        ln=None):
    b_shape = b.shape if b_lead is None else b.shape[1:]
    if mode == "nn":
        (M, K), (K2, N) = a.shape, b_shape
    elif mode == "nt":
        (M, K), (N, K2) = a.shape, b_shape
        K2 = K if b_col0 + K <= K2 else -1
    else:
        (K, M), (K2, N) = a.shape, b_shape
    assert K == K2 and (mode == "nt" or b_col0 == 0), (name, a.shape, b.shape)
    tm, tn = _pick(M, tm, SUBLANE if mode != "tn" else LANE), _pick(N, tn)
    tk = _pick(K, tk, LANE if mode != "tn" else SUBLANE)
    nk = K // tk
    assert b_col0 % tk == 0, (name, b_col0, tk)
    koff = b_col0 // tk
    n_add = len(adds)
    n_ln = 0 if ln is None else 3

    def body(*refs):
        a_ref, b_ref = refs[0], refs[1]
        add_refs = refs[2:2 + n_add]
        ln_refs = refs[2 + n_add:2 + n_add + n_ln]
        o_ref, acc_ref = refs[2 + n_add + n_ln], refs[-1]
        part = _bdot(a_ref[...], b_ref[...], mode)

        def finish(r):
            for ar in add_refs:
                r = r + ar[...]
            if trim is not None:
                r = r[:, :trim[1]] if trim[0] == "cols" else r[:trim[1], :]
            o_ref[...] = r
            if ln is not None:
                x_ref, g_ref, beta_ref = ln_refs
                y_ref, y16_ref = refs[3 + n_add + n_ln], refs[4 + n_add + n_ln]
                xhat, _ = _ln_stats(DN_ALPHA * x_ref[...] + r)
                y = xhat * g_ref[...] + beta_ref[...]
                y_ref[...] = y
                y16_ref[...] = y.astype(BF16)

        if nk == 1:
            finish(part)
        else:
            k = pl.program_id(2)

            @pl.when(k == 0)
            def _():
                acc_ref[...] = part

            @pl.when(k > 0)
            def _():
                acc_ref[...] += part

            @pl.when(k == nk - 1)
            def _():
                finish(acc_ref[...])

    def spec(block, index):
        if col_major:
            return pl.BlockSpec(block, lambda j, i, k: index(i, j, k))
        return pl.BlockSpec(block, index)

    def b_spec_of(block, index):
        if b_lead is None:
            return spec(block, index)
        return spec((None,) + block, lambda i, j, k: (b_lead,) + index(i, j, k))

    if mode == "nn":
        a_spec = spec((tm, tk), lambda i, j, k: (i, k))
        b_spec = b_spec_of((tk, tn), lambda i, j, k: (k, j))
    elif mode == "nt":
        a_spec = spec((tm, tk), lambda i, j, k: (i, k))
        b_spec = b_spec_of((tn, tk), lambda i, j, k: (j, k + koff))
    else:
        a_spec = spec((tk, tm), lambda i, j, k: (k, i))
        b_spec = b_spec_of((tk, tn), lambda i, j, k: (k, j))
    o_spec = spec((tm, tn), lambda i, j, k: (i, j))
    out_spec, out_shape = o_spec, (M, N)
    if trim is not None and trim[0] == "cols":
        out_spec, out_shape = spec((None, tm, trim[1]), lambda i, j, k: (j, i, 0)), (N // tn, M, trim[1])
    elif trim is not None:
        out_spec, out_shape = spec((None, trim[1], tn), lambda i, j, k: (i, 0, j)), (M // tm, trim[1], N)
    acc_shape = (tm, tn) if nk > 1 else (SUBLANE, LANE)
    in_specs, out_specs, out_shapes, ln_args = [a_spec, b_spec] + [o_spec] * n_add, out_spec, \
        jax.ShapeDtypeStruct(out_shape, F32), ()
    if ln is not None:
        assert tn == N and trim is None, name
        vec = spec((1, tn), lambda i, j, k: (0, j))
        in_specs += [o_spec, vec, vec]
        out_specs = [out_spec, o_spec, o_spec]
        out_shapes = [out_shapes, jax.ShapeDtypeStruct((M, N), F32), jax.ShapeDtypeStruct((M, N), BF16)]
        ln_args = ln
    return pl.pallas_call(
        body, name=name, grid=(N // tn, M // tm, nk) if col_major else (M // tm, N // tn, nk),
        in_specs=in_specs, out_specs=out_specs, out_shape=out_shapes,
        scratch_shapes=[pltpu.VMEM(acc_shape, F32)],
        compiler_params=_params(),
    )(a, b, *adds, *ln_args)


def _rowwise(fn, tiled, full, outs_tiled, outs_acc, *, rows, tile, name, acc_period=None):
    n_tiles = rows // tile
    period = n_tiles if acc_period is None else acc_period
    arrays, in_specs = [], []
    for t in tiled:
        arr, width, cb = t if isinstance(t, tuple) else (t, t.shape[1], 0)
        arrays.append(arr)
        in_specs.append(pl.BlockSpec((tile, width), lambda i, cb=cb: (i, cb)))
    for f in full:
        arr, spec = f if isinstance(f, tuple) else (f, None)
        arrays.append(arr)
        in_specs.append(spec if spec is not None else pl.BlockSpec(arr.shape, lambda i, nd=arr.ndim: (0,) * nd))
    out_shape, out_specs = [], []
    for width, dt in outs_tiled:
        out_shape.append(jax.ShapeDtypeStruct((rows, width), dt))
        out_specs.append(pl.BlockSpec((tile, width), lambda i: (i, 0)))
    for acc in outs_acc:
        shape, dt = acc[0], acc[1]
        out_shape.append(jax.ShapeDtypeStruct(shape, dt))
        out_specs.append(acc[2] if len(acc) > 2 else pl.BlockSpec(shape, lambda i, nd=len(shape): (0,) * nd))
    n_in, n_t, n_a = len(arrays), len(outs_tiled), len(outs_acc)

    def body(*refs):
        vals = [r[...] for r in refs[:n_in]]
        o_t, o_a = fn(*vals)
        for r, v in zip(refs[n_in:n_in + n_t], o_t):
            r[...] = v.astype(r.dtype)
        first = pl.program_id(0) % period == 0
        for r, v in zip(refs[n_in + n_t:n_in + n_t + n_a], o_a):
            v = v.reshape(r.shape)

            @pl.when(first)
            def _(r=r, v=v):
                r[...] = v

            @pl.when(jnp.logical_not(first))
            def _(r=r, v=v):
                r[...] += v

    return pl.pallas_call(
        body, name=name, grid=(n_tiles,), in_specs=in_specs, out_specs=out_specs, out_shape=out_shape,
        compiler_params=_params(),
    )(*arrays)


def _ln_stats(h):
    mu = jnp.mean(h, axis=-1, keepdims=True)
    d = h - mu
    var = jnp.mean(d * d, axis=-1, keepdims=True)
    rstd = lax.rsqrt(var + LN_EPS)
    return d * rstd, rstd


def _ln_bwd_math(h, g, dy):
    xhat, rstd = _ln_stats(h)
    dxhat = dy * g
    dh = rstd * (dxhat - jnp.mean(dxhat, axis=-1, keepdims=True)
                 - xhat * jnp.mean(dxhat * xhat, axis=-1, keepdims=True))
    return dh, jnp.sum(dy * xhat, axis=0, keepdims=True), jnp.sum(dy, axis=0, keepdims=True)


def _ln_bwd(x, r, g, dys, *, name):
    n, d = x.shape
    n_dy = len(dys)

    def fn(x, r, *rest):
        dy = rest[0]
        for e in rest[1:n_dy]:
            dy = dy + e
        dh, dg, db = _ln_bwd_math(DN_ALPHA * x + r, rest[n_dy], dy)
        return (DN_ALPHA * dh, dh), (dg, db)

    return _rowwise(fn, [x, r, *dys], [g], [(d, F32), (d, BF16)], [((1, d), F32), ((1, d), F32)],
                    rows=n, tile=_pick(n, 512, SUBLANE), name=name)


def _final_ln_loss(x, r, target, g, b, *, name):
    n, d = x.shape

    def fn(x, r, t, g, b):
        h = DN_ALPHA * x + r
        xhat, _ = _ln_stats(h)
        err = xhat * g + b - t
        loss = jnp.full((1, LANE), 0.5 * jnp.sum(err * err) / d, F32)
        dh, dg, db = _ln_bwd_math(h, g, err / d)
        return (DN_ALPHA * dh, dh), (loss, dg, db)

    return _rowwise(fn, [x, r, target], [g, b], [(d, F32), (d, BF16)],
                    [((1, LANE), F32), ((1, d), F32), ((1, d), F32)],
                    rows=n, tile=_pick(n, 512, SUBLANE), name=name)


def _mem_heads(qm):
    lane = lax.broadcasted_iota(jnp.int32, (1, MEM_WIDTH), 1)
    for h in range(MEM_HEADS):
        msk = (lane >= h * HEAD_DIM) & (lane < (h + 1) * HEAD_DIM)
        yield msk, jnp.where(msk, qm, 0.0).astype(BF16)


def _mem_softmax(qh, k):
    s = _bdot(qh, k, "nt") * ATT_SCALE
    p = jnp.exp(s - jnp.max(s, axis=-1, keepdims=True))
    return p / jnp.sum(p, axis=-1, keepdims=True)


def _memattn_fwd(tok, proj, memkv, scale, *, seq, name):
    n, tokw = tok.shape
    d = tokw + MEM_WIDTH
    tile = _pick(seq, 512, SUBLANE)

    def fn(tok, qm, kv, scale):
        k, v = kv[:, :MEM_WIDTH].astype(BF16), kv[:, MEM_WIDTH:].astype(BF16)
        out = jnp.zeros(qm.shape, F32)
        for msk, qh in _mem_heads(qm):
            out = jnp.where(msk, _bdot(_mem_softmax(qh, k), v, "nn"), out)
        return (jnp.concatenate([tok * scale, out], axis=1),), ()

    kv_spec = pl.BlockSpec((None,) + memkv.shape[1:], lambda i: (i // (seq // tile), 0, 0))
    return _rowwise(fn, [tok, (proj, MEM_WIDTH, tokw // MEM_WIDTH)], [(memkv, kv_spec), scale], [(d, BF16)], [],
                    rows=n, tile=tile, name=name)[0]


def _memattn_bwd(dmixin, dtok, proj, memkv, *, seq, name):
    n, tokw = dtok.shape
    d = tokw + MEM_WIDTH
    tile = _pick(seq, 512, SUBLANE)

    def fn(dmo, dtok, qm, kv):
        k, v = kv[:, :MEM_WIDTH].astype(BF16), kv[:, MEM_WIDTH:].astype(BF16)
        dq = jnp.zeros(qm.shape, F32)
        dk = jnp.zeros(k.shape, F32)
        dv = jnp.zeros(v.shape, F32)
        for msk, qh in _mem_heads(qm):
            p = _mem_softmax(qh, k)
            doh = jnp.where(msk, dmo, 0.0).astype(BF16)
            dv = dv + _bdot(p, doh, "tn")
            dp = _bdot(doh, v, "nt")
            ds = (p * (dp - jnp.sum(dp * p, axis=-1, keepdims=True))).astype(BF16)
            dq = jnp.where(msk, _bdot(ds, k, "nn") * ATT_SCALE, dq)
            dk = dk + _bdot(ds, qh, "tn") * ATT_SCALE
        return (jnp.concatenate([dtok, dq], axis=1),), (jnp.concatenate([dk, dv], axis=1),)

    tpe = seq // tile
    kv_spec = pl.BlockSpec((None,) + memkv.shape[1:], lambda i: (i // tpe, 0, 0))
    return _rowwise(fn, [(dmixin, MEM_WIDTH, tokw // MEM_WIDTH), dtok, (proj, MEM_WIDTH, tokw // MEM_WIDTH)],
                    [(memkv, kv_spec)], [(d, BF16)], [(memkv.shape, F32, kv_spec)],
                    rows=n, tile=tile, name=name, acc_period=tpe)


def _scale_bwd(dmixin, mixed, scale, *, name):
    n, tokw = mixed.shape

    def fn(dt, mixed, scale):
        return (dt * scale,), (jnp.sum(dt * mixed, axis=0, keepdims=True),)

    return _rowwise(fn, [(dmixin, tokw, 0), mixed], [scale], [(tokw, BF16)], [((1, tokw), F32)],
                    rows=n, tile=_pick(n, 512, SUBLANE), name=name)


def _chunk_rows(seq):
    return _pick(seq, 512, SUBLANE)


def _load_ext(ref, c, rows, before, after, seq):
    lo, hi = c * rows - before, (c + 1) * rows + after
    parts = []
    if lo < 0:
        parts.append(jnp.zeros((-lo, ref.shape[1]), F32))
    parts.append(ref[max(lo, 0):min(hi, seq), :])
    if hi > seq:
        parts.append(jnp.zeros((hi - seq, ref.shape[1]), F32))
    return parts[0] if len(parts) == 1 else jnp.concatenate(parts, axis=0)


def _down(x, k):
    return pltpu.roll(x, k, 0)


def _up(x, k):
    return pltpu.roll(x, x.shape[0] - k, 0)


def _window_sums(ext, shift, col0, group):
    lane = col0 + lax.broadcasted_iota(jnp.int32, (1, ext.shape[1]), 1)
    gidx = lane // group
    s = ext
    out = None
    k = 1
    for gi, w in enumerate(POOL_WINDOWS):
        while k < w:
            s = s + shift(s, k)
            k *= 2
        out = s if out is None else jnp.where(gidx >= gi, s, out)
    return out, jnp.left_shift(2, jnp.minimum(gidx, len(POOL_WINDOWS) - 1))


def _pool_fwd(proj3, tokw, *, name):
    nb, seq, _ = proj3.shape
    rows = _chunk_rows(seq)
    group = tokw // len(POOL_WINDOWS)

    def body(u_ref, o_ref):
        col0 = pl.program_id(1) * LANE
        for c in range(seq // rows):
            ext = _load_ext(u_ref, c, rows, MAX_WINDOW, 0, seq)
            sums, win = _window_sums(ext, _down, col0, group)
            t = c * rows + lax.broadcasted_iota(jnp.int32, (rows, 1), 0)
            count = jnp.minimum(t + 1, win).astype(F32)
            o_ref[c * rows:(c + 1) * rows, :] = (sums[MAX_WINDOW:, :] / count - ext[MAX_WINDOW:, :]).astype(BF16)

    spec = pl.BlockSpec((None, seq, LANE), lambda b, j: (b, 0, j))
    return pl.pallas_call(
        body, name=name, grid=(nb, tokw // LANE), in_specs=[spec], out_specs=spec,
        out_shape=jax.ShapeDtypeStruct((nb, seq, tokw), BF16), compiler_params=_params(),
    )(proj3)


def _pool_bwd(dp3, *, name):
    nb, seq, tokw = dp3.shape
    rows = _chunk_rows(seq)
    group = tokw // len(POOL_WINDOWS)

    def body(d_ref, o_ref):
        col0 = pl.program_id(1) * LANE
        for c in range(seq // rows):
            ext = _load_ext(d_ref, c, rows, 0, MAX_WINDOW, seq)
            lane = col0 + lax.broadcasted_iota(jnp.int32, (1, LANE), 1)
            win = jnp.left_shift(2, jnp.minimum(lane // group, len(POOL_WINDOWS) - 1))
            t = c * rows + lax.broadcasted_iota(jnp.int32, (rows + MAX_WINDOW, 1), 0)
            scaled = ext / jnp.minimum(t + 1, win).astype(F32)
            sums, _ = _window_sums(scaled, _up, col0, group)
            o_ref[c * rows:(c + 1) * rows, :] = sums[:rows, :] - ext[:rows, :]

    spec = pl.BlockSpec((None, seq, LANE), lambda b, j: (b, 0, j))
    return pl.pallas_call(
        body, name=name, grid=(nb, tokw // LANE), in_specs=[spec], out_specs=spec,
        out_shape=jax.ShapeDtypeStruct((nb, seq, tokw), F32), compiler_params=_params(),
    )(dp3)


def _conv3(ext, w_ref, b_ref):
    x1, x2 = _down(ext, 1), _down(ext, 2)
    return w_ref[0:1, :] * x2 + w_ref[1:2, :] * x1 + w_ref[2:3, :] * ext + b_ref[...], x1, x2


def _convgate_fwd(up3, cw, cb, *, name):
    nb, seq, c2 = up3.shape
    fp = c2 // 2
    nblk = fp // LANE
    rows = _chunk_rows(seq)

    def body(u_ref, g_ref, wu_ref, wg_ref, bu_ref, bg_ref, o_ref):
        for c in range(seq // rows):
            hu, _, _ = _conv3(_load_ext(u_ref, c, rows, SUBLANE, 0, seq), wu_ref, bu_ref)
            hg, _, _ = _conv3(_load_ext(g_ref, c, rows, SUBLANE, 0, seq), wg_ref, bg_ref)
            o_ref[c * rows:(c + 1) * rows, :] = (hg * jax.nn.sigmoid(hg) * hu)[SUBLANE:, :].astype(BF16)

    def col(off, r):
        return pl.BlockSpec((r, LANE), lambda b, j: (0, j + off))

    def act(off):
        return pl.BlockSpec((None, seq, LANE), lambda b, j: (b, 0, j + off))

    return pl.pallas_call(
        body, name=name, grid=(nb, nblk),
        in_specs=[act(0), act(nblk), col(0, SUBLANE), col(nblk, SUBLANE), col(0, 1), col(nblk, 1)],
        out_specs=act(0), out_shape=jax.ShapeDtypeStruct((nb, seq, fp), BF16), compiler_params=_params(),
    )(up3, up3, cw, cw, cb, cb)


def _convgate_bwd(up3, dact3, cw, cb, *, name):
    nb, seq, c2 = up3.shape
    fp = c2 // 2
    nblk = fp // LANE
    rows = _chunk_rows(seq)
    h = SUBLANE

    def body(u_ref, g_ref, da_ref, wu_ref, wg_ref, bu_ref, bg_ref, du_ref, dg_ref, dwu_ref, dwg_ref, dbu_ref,
             dbg_ref):
        @pl.when(pl.program_id(1) == 0)
        def _():
            for r in (dwu_ref, dwg_ref, dbu_ref, dbg_ref):
                r[...] = jnp.zeros(r.shape, F32)

        for c in range(seq // rows):
            eu = _load_ext(u_ref, c, rows, h, h, seq)
            eg = _load_ext(g_ref, c, rows, h, h, seq)
            da = _load_ext(da_ref, c, rows, h, h, seq)
            hu, u1, u2 = _conv3(eu, wu_ref, bu_ref)
            hg, g1, g2 = _conv3(eg, wg_ref, bg_ref)
            sig = jax.nn.sigmoid(hg)
            dhu = da * hg * sig
            dhg = da * hu * sig * (1.0 + hg * (1.0 - sig))
            for dh, w_ref, x0, x1, x2, dx_ref, dw_ref, db_ref in (
                    (dhu, wu_ref, eu, u1, u2, du_ref, dwu_ref, dbu_ref),
                    (dhg, wg_ref, eg, g1, g2, dg_ref, dwg_ref, dbg_ref)):
                dx = w_ref[2:3, :] * dh + w_ref[1:2, :] * _up(dh, 1) + w_ref[0:1, :] * _up(dh, 2)
                dx_ref[c * rows:(c + 1) * rows, :] = dx[h:h + rows, :].astype(BF16)
                core = dh[h:h + rows, :]
                for k, xk in ((0, x2), (1, x1), (2, x0)):
                    dw_ref[k:k + 1, :] += jnp.sum(core * xk[h:h + rows, :], axis=0, keepdims=True)
                db_ref[...] += jnp.sum(core, axis=0, keepdims=True)

    def col(off, r):
        return pl.BlockSpec((r, LANE), lambda j, b: (0, j + off))

    def act(off):
        return pl.BlockSpec((None, seq, LANE), lambda j, b: (b, 0, j + off))

    du, dg, dwu, dwg, dbu, dbg = pl.pallas_call(
        body, name=name, grid=(nblk, nb),
        in_specs=[act(0), act(nblk), act(0), col(0, SUBLANE), col(nblk, SUBLANE), col(0, 1), col(nblk, 1)],
        out_specs=[act(0), act(0), col(0, SUBLANE), col(0, SUBLANE), col(0, 1), col(0, 1)],
        out_shape=[jax.ShapeDtypeStruct((nb, seq, fp), BF16), jax.ShapeDtypeStruct((nb, seq, fp), BF16),
                   jax.ShapeDtypeStruct((SUBLANE, fp), F32), jax.ShapeDtypeStruct((SUBLANE, fp), F32),
                   jax.ShapeDtypeStruct((1, fp), F32), jax.ShapeDtypeStruct((1, fp), F32)],
        compiler_params=_params(),
    )(up3, up3, dact3, cw, cw, cb, cb)
    return du, dg, jnp.concatenate([dwu, dwg], axis=1), jnp.concatenate([dbu, dbg], axis=1)


def _scan_rows(x, shift, valid):
    row = lax.broadcasted_iota(jnp.int32, (x.shape[0], 1), 0)
    k = 1
    while k < x.shape[0]:
        x = x + jnp.where(valid(row, k), shift(x, k), 0.0)
        k *= 2
    return x


def _pick_row(x, r):
    row = lax.broadcasted_iota(jnp.int32, (x.shape[0], 1), 0)
    return jnp.sum(jnp.where(row == r, x, 0.0), axis=0, keepdims=True)


def _log_sigmoid(z):
    return jnp.minimum(z, 0.0) - jnp.log(1.0 + jnp.exp(-jnp.abs(z)))


def _gate_fwd(kvf3, fb, col_block, *, name):
    nb, seq, _ = kvf3.shape
    rows = _chunk_rows(seq)

    def body(f_ref, fb_ref, o_ref):
        carry = jnp.zeros((1, LANE), F32)
        for c in range(seq // rows):
            logf = _log_sigmoid(f_ref[c * rows:(c + 1) * rows, :] + fb_ref[...])
            run = _scan_rows(logf, _down, lambda row, k: row >= k) + carry
            o_ref[c * rows:(c + 1) * rows, :] = run
            carry = _pick_row(run, rows - 1)

    return pl.pallas_call(
        body, name=name, grid=(nb,),
        in_specs=[pl.BlockSpec((None, seq, LANE), lambda b: (b, 0, col_block)),
                  pl.BlockSpec((1, LANE), lambda b: (0, 0))],
        out_specs=pl.BlockSpec((None, seq, LANE), lambda b: (b, 0, 0)),
        out_shape=jax.ShapeDtypeStruct((nb, seq, LANE), F32), compiler_params=_params(),
    )(kvf3, fb)


def _gate_bwd(kvf3, fb, dF3, col_block, heads, *, name):
    nb, seq, _ = kvf3.shape
    rows = _chunk_rows(seq)

    def body(f_ref, fb_ref, d_ref, o_ref, dfb_ref):
        @pl.when(pl.program_id(0) == 0)
        def _():
            dfb_ref[...] = jnp.zeros(dfb_ref.shape, F32)

        lane = lax.broadcasted_iota(jnp.int32, (1, LANE), 1)
        carry = jnp.zeros((1, LANE), F32)
        for c in reversed(range(seq // rows)):
            run = _scan_rows(d_ref[c * rows:(c + 1) * rows, :], _up, lambda row, k: row < rows - k) + carry
            carry = _pick_row(run, 0)
            z = f_ref[c * rows:(c + 1) * rows, :] + fb_ref[...]
            df = jnp.where(lane < heads, run * jax.nn.sigmoid(-z), 0.0)
            o_ref[c * rows:(c + 1) * rows, :] = df
            dfb_ref[...] += jnp.sum(df, axis=0, keepdims=True)

    return pl.pallas_call(
        body, name=name, grid=(nb,),
        in_specs=[pl.BlockSpec((None, seq, LANE), lambda b: (b, 0, col_block)),
                  pl.BlockSpec((1, LANE), lambda b: (0, 0)),
                  pl.BlockSpec((None, seq, LANE), lambda b: (b, 0, 0))],
        out_specs=[pl.BlockSpec((None, seq, LANE), lambda b: (b, 0, 0)), pl.BlockSpec((1, LANE), lambda b: (0, 0))],
        out_shape=[jax.ShapeDtypeStruct((nb, seq, LANE), F32), jax.ShapeDtypeStruct((1, LANE), F32)],
        compiler_params=_params(),
    )(kvf3, fb, dF3)


def _head_masks():
    lane = lax.broadcasted_iota(jnp.int32, (1, LANE), 1)
    return (lane < HEAD_DIM, lane >= HEAD_DIM)


BIAS_TERMS = 3


def _bias_lanes(gsum):
    nb, seq, heads = gsum.shape
    terms, rest = [], gsum
    for _ in range(BIAS_TERMS):
        t = lax.reduce_precision(rest, exponent_bits=8, mantissa_bits=7)
        terms.append(t)
        rest = rest - t
    ones = [jnp.ones_like(gsum)] * BIAS_TERMS

    def lanes(parts):
        z = jnp.stack(parts, axis=-1)
        z = jnp.pad(z, ((0, 0), (0, 0), (0, 0), (0, HEAD_DIM - 2 * BIAS_TERMS)))
        z = z.reshape(nb, seq, heads // 2, 2, HEAD_DIM)[:, :, :, ::-1]
        return z.reshape(nb, seq, heads * HEAD_DIM).astype(BF16)

    return lanes(terms + ones), lanes(ones + [-t for t in terms])


def _fox_scores(q, k, aq, ak, masked):
    qs = (q * ATT_SCALE).astype(BF16)
    qts = [jnp.where(msk, qs, aq) for msk in _head_masks()]
    ss = [_bdot(qt, jnp.where(msk, k, ak), "nt") for qt, msk in zip(qts, _head_masks())]
    if masked:
        t = q.shape[0]
        keep = lax.broadcasted_iota(jnp.int32, (t, t), 0) >= lax.broadcasted_iota(jnp.int32, (t, t), 1)
        ss = [jnp.where(keep, s, NEG_BIG) for s in ss]
    return ss, qts


def _on_blocks(qi, ki, step):
    @pl.when(ki < qi)
    def _():
        step(False)

    @pl.when(ki == qi)
    def _():
        step(True)


def _fox_grid(nblk, tokw, t, q_major):
    if q_major:
        pairs = [(qi, ki) for qi in range(nblk) for ki in range(qi + 1)]
    else:
        pairs = [(qi, ki) for ki in range(nblk) for qi in range(ki, nblk)]
    tables = [jnp.array([p[i] for p in pairs], jnp.int32) for i in (0, 1)]

    def q_spec(off=0, wide=False):
        width = 2 * LANE if wide else LANE
        return pl.BlockSpec((None, t, width), lambda b, p, i, qt, kt: (b, qt[i], p + off))

    def k_spec(off=0):
        return pl.BlockSpec((None, t, LANE), lambda b, p, i, qt, kt: (b, kt[i], p + off))

    return tables, len(pairs), q_spec, k_spec, tokw // LANE


def _lanes(col):
    return jnp.broadcast_to(col, (col.shape[0], LANE))


def _across(stat, width):
    return jnp.tile(stat, (1, width // LANE))


def _fox_fwd(proj3, kvf3, aq3, ak3, tokw, *, name):
    nb, seq, _ = proj3.shape
    t = _pick(seq, 512, LANE)
    tables, n_pairs, q_spec, k_spec, hp0 = _fox_grid(seq // t, tokw, t, True)

    def body(qt_ref, kt_ref, q_ref, k_ref, v_ref, aq_ref, ak_ref, o_ref, lse_ref, m_s, l_s, acc_s):
        i = pl.program_id(2)
        qi, ki = qt_ref[i], kt_ref[i]

        @pl.when(ki == 0)
        def _():
            m_s[...] = jnp.full(m_s.shape, NEG_BIG, F32)
            l_s[...] = jnp.zeros(l_s.shape, F32)
            acc_s[...] = jnp.zeros(acc_s.shape, F32)

        def step(masked):
            v = v_ref[...].astype(BF16)
            ss, _ = _fox_scores(q_ref[...], k_ref[...].astype(BF16), aq_ref[...], ak_ref[...], masked)
            for h, s in enumerate(ss):
                m_old = m_s[h]
                m_new = jnp.maximum(m_old, _lanes(jnp.max(s, axis=-1, keepdims=True)))
                alpha = jnp.exp(m_old - m_new)
                p = jnp.exp(s - _across(m_new, t))
                l_s[h] = alpha * l_s[h] + _lanes(jnp.sum(p, axis=-1, keepdims=True))
                acc_s[h] = alpha * acc_s[h] + _bdot(p, v, "nn")
                m_s[h] = m_new

        _on_blocks(qi, ki, step)

        @pl.when(ki == qi)
        def _():
            o_ref[...] = jnp.where(_head_masks()[0], acc_s[0] / l_s[0], acc_s[1] / l_s[1])
            lse_ref[...] = jnp.concatenate([m_s[0] + jnp.log(l_s[0]), m_s[1] + jnp.log(l_s[1])], axis=1)

    stat = pltpu.VMEM((2, t, LANE), F32)
    return pl.pallas_call(
        body, name=name,
        grid_spec=pltpu.PrefetchScalarGridSpec(
            num_scalar_prefetch=2, grid=(nb, hp0, n_pairs),
            in_specs=[q_spec(), k_spec(), k_spec(hp0), q_spec(), k_spec()],
            out_specs=[q_spec(), q_spec(wide=True)], scratch_shapes=[stat, stat, stat]),
        out_shape=[jax.ShapeDtypeStruct((nb, seq, tokw), F32), jax.ShapeDtypeStruct((nb, seq, 2 * tokw), F32)],
        compiler_params=_params(),
    )(*tables, proj3, kvf3, kvf3, aq3, ak3)


def _fox_bwd_common(q_ref, k_ref, v_ref, aq_ref, ak_ref, do_ref, lse_ref, delta_ref, masked):
    k, v = k_ref[...].astype(BF16), v_ref[...].astype(BF16)
    ss, qts = _fox_scores(q_ref[...], k, aq_ref[...], ak_ref[...], masked)
    do = do_ref[...]
    t = do.shape[0]
    out = []
    for h, (s, qt, msk) in enumerate(zip(ss, qts, _head_masks())):
        doh = jnp.where(msk, do, 0.0).astype(BF16)
        p = jnp.exp(s - _across(lse_ref[:, h * LANE:(h + 1) * LANE], t))
        ds = p * (_bdot(doh, v, "nt") - _across(delta_ref[:, h * LANE:(h + 1) * LANE], t))
        out.append((qt, doh, p, ds))
    return out, k


def _fox_bwd(proj3, kvf3, aq3, ak3, o3, dmixin3, lse3, tokw, *, name):
    nb, seq, _ = proj3.shape
    t = _pick(seq, 512, LANE)
    nblk = seq // t
    tables, n_pairs, q_spec, k_spec, hp0 = _fox_grid(nblk, tokw, t, True)
    whole = pl.BlockSpec((None, seq, LANE), lambda b, p, i, qt, kt: (b, 0, p))
    dfk_spec = pl.BlockSpec((None, None, nblk, SUBLANE, t), lambda b, p, i, qt, kt: (b, p, 0, 0, 0))

    def body(qt_ref, kt_ref, q_ref, k_ref, v_ref, aq_ref, ak_ref, o_ref, do_ref, lse_ref, dq_ref, dk_ref, dv_ref,
             dfk_ref, acc_s, row_s, delta_s):
        i = pl.program_id(2)
        qi, ki = qt_ref[i], kt_ref[i]

        @pl.when(i == 0)
        def _():
            dk_ref[...] = jnp.zeros(dk_ref.shape, F32)
            dv_ref[...] = jnp.zeros(dv_ref.shape, F32)
            dfk_ref[...] = jnp.zeros(dfk_ref.shape, F32)

        @pl.when(ki == 0)
        def _():
            acc_s[...] = jnp.zeros(acc_s.shape, F32)
            row_s[...] = jnp.zeros(row_s.shape, F32)
            prod = do_ref[...] * o_ref[...]
            delta_s[...] = jnp.concatenate(
                [_lanes(jnp.sum(jnp.where(msk, prod, 0.0), axis=-1, keepdims=True)) for msk in _head_masks()],
                axis=1)

        def step(masked):
            heads, k = _fox_bwd_common(q_ref, k_ref, v_ref, aq_ref, ak_ref, do_ref, lse_ref, delta_s, masked)
            rows = pl.ds(pl.multiple_of(ki * t, t), t)
            for h, ((qt, doh, p, ds), msk) in enumerate(zip(heads, _head_masks())):
                acc_s[h] += _bdot(ds, k, "nn")
                row_s[h] += _lanes(jnp.sum(ds, axis=-1, keepdims=True))
                dv_ref[rows, :] += _bdot(p, doh, "tn")
                dk_ref[rows, :] += jnp.where(msk, _bdot(ds, qt, "tn"), 0.0)
                dfk_ref[ki, h:h + 1, :] -= jnp.sum(ds, axis=0, keepdims=True)

        _on_blocks(qi, ki, step)

        @pl.when(ki == qi)
        def _():
            dq_ref[...] = jnp.where(_head_masks()[0], acc_s[0], acc_s[1]) * ATT_SCALE
            for h in range(2):
                dfk_ref[qi, 2 + h:3 + h, :] = row_s[h].T[0:1, :]

    out = jax.ShapeDtypeStruct((nb, seq, tokw), F32)
    stat = pltpu.VMEM((2, t, LANE), F32)
    return pl.pallas_call(
        body, name=name,
        grid_spec=pltpu.PrefetchScalarGridSpec(
            num_scalar_prefetch=2, grid=(nb, hp0, n_pairs),
            in_specs=[q_spec(), k_spec(), k_spec(hp0), q_spec(), k_spec(), q_spec(), q_spec(), q_spec(wide=True)],
            out_specs=[q_spec(), whole, whole, dfk_spec],
            scratch_shapes=[stat, stat, pltpu.VMEM((t, 2 * LANE), F32)]),
        out_shape=[out, out, out, jax.ShapeDtypeStruct((nb, hp0, nblk, SUBLANE, t), F32)],
        compiler_params=_params(),
    )(*tables, proj3, kvf3, kvf3, aq3, ak3, o3, dmixin3, lse3)


def _peer(k):
    x, y, c = lax.axis_index("x"), lax.axis_index("y"), lax.axis_index("c")
    return (1 - x if k & 4 else x, 1 - y if k & 2 else y, 1 - c if k & 1 else c)


def _dev_index(p):
    return 4 * p[0] + 2 * p[1] + p[2]


_HBM = pl.BlockSpec(memory_space=pltpu.HBM)
CHIP_RELATIONS = (2, 4, 6)


def _chip_index(p):
    return 2 * p[0] + p[1]


def _run_copies(sends, recvs):
    for cp in sends:
        cp.start()
    for cp in recvs:
        cp.wait_recv()
    for cp in sends:
        cp.wait_send()


def _gather_shards(whole, halved, side_by_side, *, name):
    nw, nh = len(whole), len(halved) + len(side_by_side)
    n = nw + nh
    n_sem = 3 * nw + 6 * nh

    def body(*refs):
        ins, outs, send_sems, recv_sems, local_sems = refs[:n], refs[n:2 * n], refs[2 * n], refs[2 * n + 1], refs[-1]
        me, sib = _peer(0), _peer(1)
        q, c = _chip_index(me), me[2]
        own = []

        def copy(src, dst, s, to):
            return pltpu.make_async_remote_copy(src_ref=src, dst_ref=dst, send_sem=send_sems.at[s],
                                                recv_sem=recv_sems.at[s], device_id=to, device_id_type=MESH_T)

        sends, recvs, passes = [], [], []
        for j, k in enumerate(CHIP_RELATIONS):
            peer = _peer(k)
            pq = _chip_index(peer)
            for i in range(nw):
                sends.append(copy(ins[i], outs[i].at[q], 3 * i + j, peer))
                recvs.append(copy(ins[i], outs[i].at[pq], 3 * i + j, peer))
            for i in range(nh):
                src, out, s = ins[nw + i], outs[nw + i], 3 * nw + 6 * i + j
                if i < len(halved):
                    place = lambda chip, half, out=out: out.at[chip, half]
                else:
                    cols = src.shape[-1]
                    place = lambda chip, half, out=out, cols=cols: out.at[
                        half, :, pl.ds(pl.multiple_of(chip * cols, LANE), cols)]
                    if j == 0:
                        own += [pltpu.make_async_copy(src.at[h], place(q, h), local_sems.at[len(own) + h])
                                for h in range(2)]
                sends.append(copy(src.at[c], place(q, c), s, peer))
                passes.append((copy(src.at[c], place(pq, c), s, peer), copy(place(pq, c), place(pq, c), s + 3, sib),
                               copy(place(pq, c), place(pq, 1 - c), s + 3, sib)))
        for cp in sends + own:
            cp.start()
        for arrival, hand_over, _ in passes:
            arrival.wait_recv()
            hand_over.start()
        for cp in recvs:
            cp.wait_recv()
        for _, _, from_sibling in passes:
            from_sibling.wait_recv()
        for cp in sends + [hand_over for _, hand_over, _ in passes]:
            cp.wait_send()
        for cp in own:
            cp.wait()

    arrays = list(whole) + list(halved) + list(side_by_side)
    return pl.pallas_call(
        body, name=name, in_specs=[_HBM] * n, out_specs=[_HBM] * n,
        out_shape=[jax.ShapeDtypeStruct((N_CHIP,) + a.shape, a.dtype) for a in list(whole) + list(halved)]
        + [jax.ShapeDtypeStruct(a.shape[:-1] + (N_CHIP * a.shape[-1],), a.dtype) for a in side_by_side],
        scratch_shapes=[pltpu.SemaphoreType.DMA((n_sem,)), pltpu.SemaphoreType.DMA((n_sem,)),
                        pltpu.SemaphoreType.DMA((2 * len(side_by_side),))],
    )(*arrays)


def _to_sibling(grads, *, name):
    n = len(grads)

    def body(*refs):
        ins, outs, send_sems, recv_sems = refs[:n], refs[n:2 * n], refs[2 * n], refs[2 * n + 1]
        c = lax.axis_index("c")
        sends = [pltpu.make_async_remote_copy(src_ref=ins[i].at[:, 1 - c], dst_ref=outs[i], send_sem=send_sems.at[i],
                                              recv_sem=recv_sems.at[i], device_id=_peer(1), device_id_type=MESH_T)
                 for i in range(n)]
        _run_copies(sends, sends)

    return pl.pallas_call(
        body, name=name, in_specs=[_HBM] * n, out_specs=[_HBM] * n,
        out_shape=[jax.ShapeDtypeStruct(g.shape[:1] + g.shape[2:], g.dtype) for g in grads],
        scratch_shapes=[pltpu.SemaphoreType.DMA((n,)), pltpu.SemaphoreType.DMA((n,))],
    )(*grads)


def _pair_add(grads, from_sibling, qc, *, name):
    _, _, rows, cols = grads.shape
    tile = _pick(rows, 1024, SUBLANE)

    def body(qc_ref, g_ref, s_ref, o_ref):
        del qc_ref
        o_ref[...] = g_ref[...] + s_ref[...]

    spec = pl.BlockSpec((None, tile, cols), lambda j, i, qc: (j, i, 0))
    return pl.pallas_call(
        body, name=name,
        grid_spec=pltpu.PrefetchScalarGridSpec(
            num_scalar_prefetch=1, grid=(N_CHIP, rows // tile),
            in_specs=[pl.BlockSpec((None, None, tile, cols), lambda j, i, qc: (j, qc[1], i, 0)), spec],
            out_specs=spec),
        out_shape=jax.ShapeDtypeStruct((N_CHIP, rows, cols), F32), compiler_params=_params(),
    )(qc, grads, from_sibling)


def _to_chips(sums, small, *, name):
    n = len(sums)

    def body(*refs):
        ins, small_ref, outs, small_out = refs[:n], refs[n], refs[n + 1:2 * n + 1], refs[2 * n + 1]
        send_sems, recv_sems, local_sem = refs[2 * n + 2:]
        me = _peer(0)

        def copy(src, dst, s, to):
            return pltpu.make_async_remote_copy(src_ref=src, dst_ref=dst, send_sem=send_sems.at[s],
                                                recv_sem=recv_sems.at[s], device_id=to, device_id_type=MESH_T)

        mine = pltpu.make_async_copy(small_ref.at[_dev_index(me)], small_out.at[_dev_index(me)], local_sem)
        mine.start()
        sends, recvs = [], []
        for j, k in enumerate(CHIP_RELATIONS):
            peer = _peer(k)
            for i in range(n):
                src = ins[i].at[_chip_index(peer)]
                sends.append(copy(src, outs[i].at[j], 3 * i + j, peer))
                recvs.append(copy(src, outs[i].at[j], 3 * i + j, peer))
        for k in range(1, N_DEV):
            peer = _peer(k)
            src = small_ref.at[_dev_index(peer)]
            sends.append(copy(src, small_out.at[_dev_index(me)], 3 * n + k - 1, peer))
            recvs.append(copy(src, small_out.at[_dev_index(peer)], 3 * n + k - 1, peer))
        _run_copies(sends, recvs)
        mine.wait()

    n_sem = 3 * n + N_DEV - 1
    return pl.pallas_call(
        body, name=name, in_specs=[_HBM] * (n + 1), out_specs=[_HBM] * (n + 1),
        out_shape=[jax.ShapeDtypeStruct((3,) + g.shape[1:], g.dtype) for g in sums]
        + [jax.ShapeDtypeStruct(small.shape, small.dtype)],
        scratch_shapes=[pltpu.SemaphoreType.DMA((n_sem,)), pltpu.SemaphoreType.DMA((n_sem,)),
                        pltpu.SemaphoreType.DMA],
    )(*sums, small)


def _swap_halves(arrays, *, name):
    n = len(arrays)

    def body(*refs):
        outs, send_sems, recv_sems = refs[n:2 * n], refs[2 * n], refs[2 * n + 1]
        c = lax.axis_index("c")
        sib = _peer(1)
        sends, recvs = [], []
        for i in range(n):
            sem = dict(send_sem=send_sems.at[i], recv_sem=recv_sems.at[i], device_id=sib, device_id_type=MESH_T)
            sends.append(pltpu.make_async_remote_copy(src_ref=outs[i].at[c], dst_ref=outs[i].at[c], **sem))
            recvs.append(pltpu.make_async_remote_copy(src_ref=outs[i].at[c], dst_ref=outs[i].at[1 - c], **sem))
        _run_copies(sends, recvs)

    return pl.pallas_call(
        body, name=name, in_specs=[_HBM] * n, out_specs=[_HBM] * n,
        out_shape=[jax.ShapeDtypeStruct(a.shape, a.dtype) for a in arrays],
        input_output_aliases={i: i for i in range(n)},
        scratch_shapes=[pltpu.SemaphoreType.DMA((n,)), pltpu.SemaphoreType.DMA((n,))],
    )(*arrays)


def _adam_math(g, w, m, v):
    bc1 = 1.0 - ADAM_B1 ** ADAM_STEP
    bc2 = 1.0 - ADAM_B2 ** ADAM_STEP
    m_new = ADAM_B1 * m + (1.0 - ADAM_B1) * g
    v_new = ADAM_B2 * v + (1.0 - ADAM_B2) * (g * g)
    delta = -ADAM_LR * ((m_new / bc1) / (jnp.sqrt(v_new / bc2) + ADAM_EPS) + ADAM_WD * w)
    return delta, m_new, v_new


def _reduce_half(sums, parts, qc, *, name):
    _, rows, cols = sums.shape
    tile = _pick(rows, 1024, SUBLANE)
    n_parts = parts.shape[0]

    def body(qc_ref, g_ref, p_ref, o_ref):
        del qc_ref
        g = g_ref[...]
        for k in range(n_parts):
            g = g + p_ref[k]
        o_ref[...] = g

    return pl.pallas_call(
        body, name=name,
        grid_spec=pltpu.PrefetchScalarGridSpec(
            num_scalar_prefetch=1, grid=(rows // tile,),
            in_specs=[pl.BlockSpec((None, tile, cols), lambda i, qc: (qc[0], i, 0)),
                      pl.BlockSpec((n_parts, tile, cols), lambda i, qc: (0, i, 0))],
            out_specs=pl.BlockSpec((None, tile, cols), lambda i, qc: (qc[1], i, 0))),
        out_shape=jax.ShapeDtypeStruct((2, rows, cols), F32), compiler_params=_params(),
    )(qc, sums, parts)


def _adamw_shard(g, w, m, v, *, name):
    _, rows, cols = g.shape
    tile = _pick(rows, 512, SUBLANE)

    def body(g_ref, w_ref, m_ref, v_ref, do_ref, mo_ref, vo_ref):
        do_ref[...], mo_ref[...], vo_ref[...] = _adam_math(g_ref[...], w_ref[...], m_ref[...], v_ref[...])

    spec = pl.BlockSpec((None, tile, cols), lambda h, i: (h, i, 0))
    return pl.pallas_call(
        body, name=name, grid=(2, rows // tile), in_specs=[spec] * 4, out_specs=[spec] * 3,
        out_shape=[jax.ShapeDtypeStruct(g.shape, F32)] * 3, compiler_params=_params(),
    )(g, w, m, v)


def _adamw(parts, w, m, v, *, name):
    _, rows, cols = parts.shape
    tile = _pick(rows, 256, SUBLANE)

    def body(p_ref, w_ref, m_ref, v_ref, o_ref):
        g = p_ref[0]
        for i in range(1, N_DEV):
            g = g + p_ref[i]
        delta, m_new, v_new = _adam_math(g, w_ref[...], m_ref[...], v_ref[...])
        o_ref[0] = g
        o_ref[1] = delta
        o_ref[2] = m_new
        o_ref[3] = v_new

    spec = pl.BlockSpec((tile, cols), lambda i: (i, 0))
    return pl.pallas_call(
        body, name=name, grid=(rows // tile,),
        in_specs=[pl.BlockSpec((N_DEV, tile, cols), lambda i: (0, i, 0)), spec, spec, spec],
        out_specs=pl.BlockSpec((4, tile, cols), lambda i: (0, i, 0)),
        out_shape=jax.ShapeDtypeStruct((4, rows, cols), F32), compiler_params=_params(),
    )(parts, w, m, v)


def _layout(shapes, names, align):
    out, off = [], 0
    for n in names:
        size = math.prod(shapes[n])
        out.append((n, tuple(shapes[n]), off, size))
        off += _round_up(size, align)
    return out, off


def _pack(arrays, layout, total, lead=()):
    parts = []
    for i, (n, _, off, size) in enumerate(layout):
        end = layout[i + 1][2] if i + 1 < len(layout) else total
        flat = arrays[n].reshape(lead + (size,))
        if end - off > size:
            flat = jnp.pad(flat, [(0, 0)] * len(lead) + [(0, end - off - size)])
        parts.append(flat)
    return jnp.concatenate(parts, axis=len(lead))


def _unpack(flat, layout, lead=()):
    return {n: flat[..., off:off + size].reshape(lead + shape) for n, shape, off, size in layout}


def _to_shards(full, axis):
    shp = full.shape
    return jnp.moveaxis(full.reshape(shp[:axis] + (N_CHIP, shp[axis] // N_CHIP) + shp[axis + 1:]), axis, 0)


def _from_shards(shards, axis):
    x = jnp.moveaxis(shards, 0, axis)
    shp = x.shape
    return x.reshape(shp[:axis] + (shp[axis] * shp[axis + 1],) + shp[axis + 2:])


def _pad_cols(w, per, padded):
    lead = w.shape[:-1]
    x = w.reshape(lead + (-1, per))
    x = jnp.pad(x, [(0, 0)] * len(lead) + [(0, 0), (0, padded - per)])
    return x.reshape(lead + (-1,))


def _unpad_cols(w, per, padded):
    lead = w.shape[:-1]
    return w.reshape(lead + (-1, padded))[..., :per].reshape(lead + (-1,))


def _local_step(x, mem, target, W):
    nb, seq, d = x.shape
    n = nb * seq
    tokw = d - MEM_WIDTH
    heads = tokw // HEAD_DIM
    mlen = mem.shape[1]
    per = W["ffn_w_down"].shape[1] // 2
    per_p = _round_up(per, LANE)
    fp = 2 * per_p
    kvw = 2 * tokw + heads
    kvp = 2 * tokw + LANE
    gate_block = 2 * tokw // LANE

    x2d = x.reshape(n, d)
    mem2d = mem.reshape(nb * mlen, d)
    t2d = target.reshape(n, d)
    row = lambda a: a.reshape(1, -1)
    ones_tok = jnp.ones((1, tokw), F32)

    pool_bd = jax.scipy.linalg.block_diag(*[W["a_pool_w"][0, i] for i in range(len(POOL_WINDOWS))]).astype(BF16)
    kv_w = jnp.pad(W["kv_w"], ((0, 0), (0, kvp - kvw)))
    fb = jnp.pad(W["f_b"], (0, LANE - heads)).reshape(1, LANE)
    w_up = W.get("ffn_w_up_padded")
    if w_up is None:
        w_up = jnp.stack([_pad_cols(W["ffn_w_up"][l], per, per_p) for l in range(DEPTH)])
    w_down = [jnp.pad(W["ffn_w_down"][l].reshape(2, per, d), ((0, 0), (0, per_p - per), (0, 0))).reshape(fp, d)
              for l in range(DEPTH)]
    conv_w = [jnp.pad(_pad_cols(W["ffn_conv_w"][l], per, per_p), ((0, SUBLANE - CONV_WIDTH), (0, 0)))
              for l in range(DEPTH)]
    conv_b = [_pad_cols(W["ffn_conv_b"][l], per, per_p).reshape(1, 2 * fp) for l in range(DEPTH)]
    w_in = [W["a_w_in"][0], W["b_w_q"][0]]
    w_out = [W["a_w_out"][0], W["b_w_out"][0]]

    saved = []
    cur = cur_mm = x2d
    for l in range(DEPTH):
        s = {"x_in": cur, "x_in_mm": cur_mm}
        memkv = _mm(mem2d, W["mem_w_kv"][l], "nn", name=f"memkv{l}").reshape(nb, mlen, 2 * MEM_WIDTH)
        if l == 0:
            proj = _mm(cur_mm, w_in[l], "nn", name="proj0")
            pooled = _pool_fwd(proj.reshape(nb, seq, d), tokw, name="pool_fwd").reshape(n, tokw)
            tok = _mm(pooled, pool_bd, "nn", name="pool_mix")
            scale = W["a_pool_scale"].reshape(1, tokw)
            s.update(pooled=pooled, mixed=tok, scale=scale)
        else:
            kvf = _mm(cur_mm, kv_w, "nn", tn=kvp, name="kvf")
            kvf3 = kvf.reshape(nb, seq, kvp)
            gsum = _gate_fwd(kvf3, fb, gate_block, name="gate_fwd")[:, :, :heads]
            aq3, ak3 = _bias_lanes(gsum)
            proj = _mm(cur_mm, w_in[l], "nn", name="proj1")
            o3, lse3 = _fox_fwd(proj.reshape(nb, seq, d), kvf3, aq3, ak3, tokw, name="fox_fwd")
            tok = o3.reshape(n, tokw)
            scale = ones_tok
            s.update(kvf3=kvf3, aq3=aq3, ak3=ak3, o3=o3, lse3=lse3)
        mixin = _memattn_fwd(tok, proj, memkv, scale, seq=seq, name=f"memattn_fwd{l}")
        mix, x1, x1_mm = _mm(mixin, w_out[l], "nn", ln=(cur, row(W["ln1_g"][l]), row(W["ln1_b"][l])), name=f"mix{l}")
        up = _mm(x1_mm, w_up, "nn", b_lead=l, tn=per_p, col_major=True, name=f"ffn_up{l}")
        act = _convgate_fwd(up.reshape(nb, seq, 2 * fp), conv_w[l], conv_b[l], name=f"convgate_fwd{l}")
        act = act.reshape(n, fp)
        if l + 1 < DEPTH:
            ffn, cur, cur_mm = _mm(act, w_down[l], "nn", tk=fp, ln=(x1, row(W["ln2_g"][l]), row(W["ln2_b"][l])),
                                   name=f"ffn_down{l}")
        else:
            ffn = _mm(act, w_down[l], "nn", tk=fp, name=f"ffn_down{l}")
        s.update(proj=proj, memkv=memkv, mixin=mixin, mix=mix, x1=x1, x1_mm=x1_mm, up=up, act=act, ffn=ffn)
        saved.append(s)

    G = {}
    ln_g = {k: [None] * DEPTH for k in ("ln1_g", "ln1_b", "ln2_g", "ln2_b")}
    stack = {k: [None] * DEPTH for k in ("mem_w_kv", "ffn_w_up", "ffn_conv_w", "ffn_conv_b", "ffn_w_down")}
    dx_terms = None
    loss = None
    for l in reversed(range(DEPTH)):
        s = saved[l]
        g2 = row(W["ln2_g"][l])
        if l == DEPTH - 1:
            dres, dffn, loss, dg, db = _final_ln_loss(s["x1"], s["ffn"], t2d, g2, row(W["ln2_b"][l]),
                                                      name="final_ln_loss")
        else:
            dres, dffn, dg, db = _ln_bwd(s["x1"], s["ffn"], g2, dx_terms, name=f"ln2_bwd{l}")
        ln_g["ln2_g"][l], ln_g["ln2_b"][l] = dg[0], db[0]
        dact = _mm(dffn, w_down[l], "nt", tn=fp, name=f"ffn_down_dx{l}")
        stack["ffn_w_down"][l] = _mm(s["act"], dffn, "tn", tm=per_p, tk=DW_ROWS, trim=("rows", per),
                                     name=f"ffn_down_dw{l}")
        du3, dg3, dcw, dcb = _convgate_bwd(s["up"].reshape(nb, seq, 2 * fp), dact.reshape(nb, seq, fp), conv_w[l],
                                           conv_b[l], name=f"convgate_bwd{l}")
        du, dgt = du3.reshape(n, fp), dg3.reshape(n, fp)
        dx1_u = _mm(du, w_up, "nt", b_lead=l, tk=fp, name=f"ffn_up_dx_u{l}")
        dx1 = _mm(dgt, w_up, "nt", b_lead=l, tk=fp, b_col0=fp, adds=[dx1_u, dres], name=f"ffn_up_dx_g{l}")
        stack["ffn_w_up"][l] = [_mm(s["x1_mm"], part, "tn", tm=d, tn=per_p, tk=DW_ROWS, trim=("cols", per),
                                    name=f"ffn_up_dw_{nm}{l}") for nm, part in (("u", du), ("g", dgt))]
        stack["ffn_conv_w"][l] = dcw[:CONV_WIDTH]
        stack["ffn_conv_b"][l] = dcb[0]
        dres1, dmix, dg, db = _ln_bwd(s["x_in"], s["mix"], row(W["ln1_g"][l]), [dx1], name=f"ln1_bwd{l}")
        ln_g["ln1_g"][l], ln_g["ln1_b"][l] = dg[0], db[0]
        dmixin = _mm(dmix, w_out[l], "nt", name=f"mix_dx{l}")
        d_w_out = _mm(s["mixin"], dmix, "tn", tm=d, tk=DW_ROWS, name=f"mix_dw{l}")
        if l == 0:
            G["a_w_out"] = d_w_out[None]
            dmixed, dscale = _scale_bwd(dmixin, s["mixed"], s["scale"], name="scale_bwd")
            G["a_pool_scale"] = dscale
            dpooled = _mm(dmixed, pool_bd, "nt", name="pool_mix_dx")
            dpw = _mm(s["pooled"], dmixed, "tn", tm=tokw, tk=DW_ROWS, name="pool_mix_dw")
            grp = tokw // len(POOL_WINDOWS)
            G["a_pool_w"] = jnp.stack([dpw[i * grp:(i + 1) * grp, i * grp:(i + 1) * grp]
                                       for i in range(len(POOL_WINDOWS))])[None]
            dtok = _pool_bwd(dpooled.reshape(nb, seq, tokw), name="pool_bwd").reshape(n, tokw)
            extra = []
        else:
            G["b_w_out"] = d_w_out[None]
            p3 = s["proj"].reshape(nb, seq, d)
            dm3 = dmixin.reshape(nb, seq, d)
            dq3, dk3, dv3, dfk = _fox_bwd(p3, s["kvf3"], s["aq3"], s["ak3"], s["o3"], dm3, s["lse3"], tokw,
                                          name="fox_bwd")
            dtok = dq3.reshape(n, tokw)
            dfk = jnp.swapaxes(dfk[:, :, :, 0:2, :] + dfk[:, :, :, 2:4, :], 2, 3).reshape(nb, heads, seq)
            dgsum = jnp.swapaxes(dfk, 1, 2)
            dgsum = jnp.pad(dgsum, ((0, 0), (0, 0), (0, LANE - heads)))
            df3, dfb = _gate_bwd(s["kvf3"], fb, dgsum, gate_block, heads, name="gate_bwd")
            G["f_b"] = dfb[0, :heads]
            dkvf = [(dk3.reshape(n, tokw), 0, "k"), (dv3.reshape(n, tokw), tokw, "v"),
                    (df3.reshape(n, LANE), 2 * tokw, "f")]
            dx_kv = []
            for part, col0, nm in dkvf:
                dx_kv = [_mm(part, kv_w, "nt", b_col0=col0, adds=dx_kv, name=f"kvf_dx_{nm}")]
            extra = dx_kv
            G["kv_w"] = jnp.concatenate([_mm(s["x_in_mm"], part, "tn", tm=d, tk=DW_ROWS, name=f"kvf_dw_{nm}")
                                         for part, _, nm in dkvf], axis=1)[:, :kvw]
        dproj, dmemkv = _memattn_bwd(dmixin, dtok, s["proj"], s["memkv"], seq=seq, name=f"memattn_bwd{l}")
        stack["mem_w_kv"][l] = _mm(mem2d, dmemkv.reshape(nb * mlen, 2 * MEM_WIDTH), "tn", tm=d, tk=DW_ROWS,
                                   name=f"memkv_dw{l}")
        G["a_w_in" if l == 0 else "b_w_q"] = _mm(s["x_in_mm"], dproj, "tn", tm=d, tk=DW_ROWS, name=f"proj_dw{l}")[None]
        if l == 0:
            grad_x = _mm(dproj, w_in[l], "nt", adds=[dres1], name="proj_dx0")
        else:
            dx_terms = [_mm(dproj, w_in[l], "nt", adds=[dres1] + extra, name="proj_dx1")]
    for k, v in ln_g.items():
        G[k] = jnp.stack(v)
    G["mem_w_kv"] = jnp.stack(stack["mem_w_kv"])
    G["ffn_w_up"] = jnp.stack([jnp.concatenate(ug, axis=0) for ug in stack["ffn_w_up"]], axis=1)
    G["ffn_conv_w"] = jnp.stack([_unpad_cols(g, per, per_p) for g in stack["ffn_conv_w"]])
    G["ffn_conv_b"] = jnp.stack([_unpad_cols(g, per, per_p) for g in stack["ffn_conv_b"]])
    G["ffn_w_down"] = jnp.stack([g.reshape(N_CHIP, per // 2, d) for g in stack["ffn_w_down"]], axis=1)
    return loss[0, 0], grad_x.reshape(nb, seq, d), G


def kernel(x, mem, a_w_in, a_pool_w, a_pool_scale, a_w_out, b_w_q, b_w_out, kv_w, f_b, mem_w_kv, ln1_g, ln1_b, ln2_g, ln2_b, ffn_w_up, ffn_conv_w, ffn_conv_b, ffn_w_down, loss_target, m_a_w_in, m_a_pool_w, m_a_pool_scale, m_a_w_out, m_b_w_q, m_b_w_out, m_kv_w, m_f_b, m_mem_w_kv, m_ln1_g, m_ln1_b, m_ln2_g, m_ln2_b, m_ffn_w_up, m_ffn_conv_w, m_ffn_conv_b, m_ffn_w_down, v_a_w_in, v_a_pool_w, v_a_pool_scale, v_a_w_out, v_b_w_q, v_b_w_out, v_kv_w, v_f_b, v_mem_w_kv, v_ln1_g, v_ln1_b, v_ln2_g, v_ln2_b, v_ffn_w_up, v_ffn_conv_w, v_ffn_conv_b, v_ffn_w_down):
    w_loc = dict(a_w_in=a_w_in, a_pool_w=a_pool_w, a_pool_scale=a_pool_scale, a_w_out=a_w_out, b_w_q=b_w_q,
                 b_w_out=b_w_out, kv_w=kv_w, f_b=f_b, mem_w_kv=mem_w_kv, ln1_g=ln1_g, ln1_b=ln1_b, ln2_g=ln2_g,
                 ln2_b=ln2_b, ffn_w_up=ffn_w_up, ffn_conv_w=ffn_conv_w, ffn_conv_b=ffn_conv_b, ffn_w_down=ffn_w_down)
    m_loc = dict(a_w_in=m_a_w_in, a_pool_w=m_a_pool_w, a_pool_scale=m_a_pool_scale, a_w_out=m_a_w_out,
                 b_w_q=m_b_w_q, b_w_out=m_b_w_out, kv_w=m_kv_w, f_b=m_f_b, mem_w_kv=m_mem_w_kv, ln1_g=m_ln1_g,
                 ln1_b=m_ln1_b, ln2_g=m_ln2_g, ln2_b=m_ln2_b, ffn_w_up=m_ffn_w_up, ffn_conv_w=m_ffn_conv_w,
                 ffn_conv_b=m_ffn_conv_b, ffn_w_down=m_ffn_w_down)
    v_loc = dict(a_w_in=v_a_w_in, a_pool_w=v_a_pool_w, a_pool_scale=v_a_pool_scale, a_w_out=v_a_w_out,
                 b_w_q=v_b_w_q, b_w_out=v_b_w_out, kv_w=v_kv_w, f_b=v_f_b, mem_w_kv=v_mem_w_kv, ln1_g=v_ln1_g,
                 ln1_b=v_ln1_b, ln2_g=v_ln2_g, ln2_b=v_ln2_b, ffn_w_up=v_ffn_w_up, ffn_conv_w=v_ffn_conv_w,
                 ffn_conv_b=v_ffn_conv_b, ffn_w_down=v_ffn_w_down)
    x_i, y_i, c = lax.axis_index("x"), lax.axis_index("y"), lax.axis_index("c")
    q = 2 * x_i + y_i
    qc = jnp.stack([q, c]).astype(jnp.int32)
    shapes = {k: v.shape for k, v in w_loc.items()}

    def halves(a):
        if a.ndim == 3 and a.shape[0] == 2:
            return a
        rows = math.prod(a.shape[:-1])
        return a.reshape(2, rows // 2, a.shape[-1])

    own = {k: (w_loc[k] if k in GATHER_F32 else w_loc[k].astype(BF16)) for k in SHARDED}
    by_chip = ("a_pool_scale",) + tuple(k for k in BIG if k != "ffn_w_up")
    per = shapes["ffn_w_up"][-1]
    up_own = jnp.pad(own["ffn_w_up"], ((0, 0), (0, 0), (0, _round_up(per, LANE) - per)))
    gathered = _gather_shards([own["a_pool_scale"]], [halves(own[k]) for k in by_chip[1:]], [up_own],
                              name="gather_weights")
    W = {k: _from_shards(lax.dynamic_update_slice_in_dim(g.reshape((N_CHIP,) + shapes[k]), own[k][None], q, axis=0),
                         SHARD_AXIS[k])
         for k, g in zip(by_chip, gathered)}
    W["ffn_w_up_padded"] = gathered[-1]
    for k in REPLICATED:
        W[k] = w_loc[k]

    loss_part, grad_x, G = _local_step(x, mem, loss_target, W)

    g_chip = [G[k] if k in GRADS_BY_CHIP else _to_shards(G[k], SHARD_AXIS[k]) for k in BIG]
    g_chip = [g.reshape((N_CHIP,) + halves(w_loc[k]).shape) for k, g in zip(BIG, g_chip)]
    lay_r, tot_r = _layout(shapes, REPLICATED, PACK_COLS)
    rep_rows = tot_r // PACK_COLS
    rows = _round_up(rep_rows + 2, SUBLANE)
    scale_w = shapes["a_pool_scale"][-1]

    def small(rep, scale_row, scalar):
        lead = scale_row.shape[:-2]
        pad = [(0, 0)] * len(lead)
        rep = jnp.broadcast_to(_pack(rep, lay_r, tot_r).reshape(rep_rows, PACK_COLS), lead + (rep_rows, PACK_COLS))
        scale_row = jnp.pad(scale_row, pad + [(0, 0), (0, PACK_COLS - scale_w)])
        last = jnp.broadcast_to(jnp.pad(scalar.reshape(1, 1), [(0, rows - rep_rows - 2), (0, PACK_COLS - 1)]),
                                lead + (rows - rep_rows - 1, PACK_COLS))
        return jnp.concatenate([rep, scale_row, last], axis=-2)

    g_scale = jnp.repeat(_to_shards(G["a_pool_scale"], 1), 2, axis=0)
    zero = jnp.zeros((), F32)

    from_sib = _to_sibling(g_chip, name="grads_to_sibling")
    sums = [_pair_add(g, s, qc, name=f"pair_add_{k}") for k, g, s in zip(BIG, g_chip, from_sib)]
    *parts, sm_parts = _to_chips(sums, small(G, g_scale, loss_part), name="scatter_grads")
    grads = _swap_halves([_reduce_half(g, p, qc, name=f"reduce_{k}") for k, g, p in zip(BIG, sums, parts)],
                         name="swap_halves")
    out = {}
    for k, g in zip(BIG, grads):
        upd = _adamw_shard(g, halves(w_loc[k]), halves(m_loc[k]), halves(v_loc[k]), name=f"adamw_{k}")
        out[k] = [r.reshape(shapes[k]) for r in (g, *upd)]
    sm = _adamw(sm_parts, *[small(d, d["a_pool_scale"], zero) for d in (w_loc, m_loc, v_loc)], name="adamw_small")
    loss = sm[0, rep_rows + 1, 0]
    out_r = _unpack(sm[:, :rep_rows].reshape(4, tot_r), lay_r, lead=(4,))
    for k in REPLICATED:
        out[k] = [out_r[k][a] for a in range(4)]
    out["a_pool_scale"] = [sm[a, rep_rows:rep_rows + 1, :scale_w] for a in range(4)]

    outs = [loss, grad_x]
    for a in range(4):
        for k in WEIGHTS:
            outs.append(out[k][a])
    return tuple(outs)
```

```python
import functools
import math

import jax
import jax.numpy as jnp
from jax import lax
from jax.experimental import pallas as pl
from jax.experimental.pallas import tpu as pltpu

F32 = jnp.float32
BF16 = jnp.bfloat16

HEAD_DIM = 64
MEM_HEADS = 4
MEM_WIDTH = MEM_HEADS * HEAD_DIM
POOL_WINDOWS = (2, 4, 8, 16)
MAX_WINDOW = 16
CONV_WIDTH = 3
DEPTH = 2
DN_ALPHA = (2.0 * DEPTH) ** 0.25
LN_EPS = 1e-5
ATT_SCALE = HEAD_DIM ** -0.5
NEG_BIG = -1e30

ADAM_LR = 0.001
ADAM_B1 = 0.9
ADAM_B2 = 0.999
ADAM_EPS = 1e-08
ADAM_WD = 0.01
ADAM_STEP = 10

LANE = 128
SUBLANE = 8
PACK_COLS = 1024
DW_ROWS = 1024
VMEM_LIMIT = 56 * 1024 * 1024
N_DEV = 8
N_CHIP = 4
MESH_T = pl.DeviceIdType.MESH

SHARDED = ("a_w_in", "a_pool_scale", "a_w_out", "b_w_q", "b_w_out", "kv_w", "mem_w_kv", "ffn_w_up",
           "ffn_conv_w", "ffn_w_down")
SHARD_AXIS = {"a_w_in": 1, "a_pool_scale": 1, "a_w_out": 1, "b_w_q": 1, "b_w_out": 1, "kv_w": 1, "mem_w_kv": 1,
              "ffn_w_up": 2, "ffn_conv_w": 2, "ffn_w_down": 1}
GATHER_F32 = ("a_pool_scale", "ffn_conv_w")
BIG = tuple(k for k in SHARDED if k != "a_pool_scale")
GRADS_BY_CHIP = ("ffn_w_up", "ffn_w_down")
REPLICATED = ("a_pool_w", "f_b", "ln1_g", "ln1_b", "ln2_g", "ln2_b", "ffn_conv_b")
WEIGHTS = ("a_w_in", "a_pool_w", "a_pool_scale", "a_w_out", "b_w_q", "b_w_out", "kv_w", "f_b", "mem_w_kv",
           "ln1_g", "ln1_b", "ln2_g", "ln2_b", "ffn_w_up", "ffn_conv_w", "ffn_conv_b", "ffn_w_down")


def _round_up(n, m):
    return -(-n // m) * m


def _pick(dim, pref, unit=LANE):
    if dim <= pref:
        return dim
    t = (pref // unit) * unit
    while t >= unit:
        if dim % t == 0:
            return t
        t -= unit
    raise ValueError(f"no tile for {dim} <= {pref}")


def _params():
    return pltpu.CompilerParams(vmem_limit_bytes=VMEM_LIMIT)


_DIMS = {"nn": ((1,), (0,)), "nt": ((1,), (1,)), "tn": ((0,), (0,))}


def _bdot(a, b, mode):
    return lax.dot_general(a.astype(BF16), b.astype(BF16), (_DIMS[mode], ((), ())), preferred_element_type=F32)


def _mm(a, b, mode, *, name, tm=512, tn=1024, tk=2048, adds=(), b_col0=0, trim=None, col_major=False, b_lead=None,
        ln=None):
    b_shape = b.shape if b_lead is None else b.shape[1:]
    if mode == "nn":
        (M, K), (K2, N) = a.shape, b_shape
    elif mode == "nt":
        (M, K), (N, K2) = a.shape, b_shape
        K2 = K if b_col0 + K <= K2 else -1
    else:
        (K, M), (K2, N) = a.shape, b_shape
    assert K == K2 and (mode == "nt" or b_col0 == 0), (name, a.shape, b.shape)
    tm, tn = _pick(M, tm, SUBLANE if mode != "tn" else LANE), _pick(N, tn)
    tk = _pick(K, tk, LANE if mode != "tn" else SUBLANE)
    nk = K // tk
    assert b_col0 % tk == 0, (name, b_col0, tk)
    koff = b_col0 // tk
    n_add = len(adds)
    n_ln = 0 if ln is None else 3

    def body(*refs):
        a_ref, b_ref = refs[0], refs[1]
        add_refs = refs[2:2 + n_add]
        ln_refs = refs[2 + n_add:2 + n_add + n_ln]
        o_ref, acc_ref = refs[2 + n_add + n_ln], refs[-1]
        part = _bdot(a_ref[...], b_ref[...], mode)

        def finish(r):
            for ar in add_refs:
                r = r + ar[...]
            if trim is not None:
                r = r[:, :trim[1]] if trim[0] == "cols" else r[:trim[1], :]
            o_ref[...] = r
            if ln is not None:
                x_ref, g_ref, beta_ref = ln_refs
                y_ref, y16_ref = refs[3 + n_add + n_ln], refs[4 + n_add + n_ln]
                xhat, _ = _ln_stats(DN_ALPHA * x_ref[...] + r)
                y = xhat * g_ref[...] + beta_ref[...]
                y_ref[...] = y
                y16_ref[...] = y.astype(BF16)

        if nk == 1:
            finish(part)
        else:
            k = pl.program_id(2)

            @pl.when(k == 0)
            def _():
                acc_ref[...] = part

            @pl.when(k > 0)
            def _():
                acc_ref[...] += part

            @pl.when(k == nk - 1)
            def _():
                finish(acc_ref[...])

    def spec(block, index):
        if col_major:
            return pl.BlockSpec(block, lambda j, i, k: index(i, j, k))
        return pl.BlockSpec(block, index)

    def b_spec_of(block, index):
        if b_lead is None:
            return spec(block, index)
        return spec((None,) + block, lambda i, j, k: (b_lead,) + index(i, j, k))

    if mode == "nn":
        a_spec = spec((tm, tk), lambda i, j, k: (i, k))
        b_spec = b_spec_of((tk, tn), lambda i, j, k: (k, j))
    elif mode == "nt":
        a_spec = spec((tm, tk), lambda i, j, k: (i, k))
        b_spec = b_spec_of((tn, tk), lambda i, j, k: (j, k + koff))
    else:
        a_spec = spec((tk, tm), lambda i, j, k: (k, i))
        b_spec = b_spec_of((tk, tn), lambda i, j, k: (k, j))
    o_spec = spec((tm, tn), lambda i, j, k: (i, j))
    out_spec, out_shape = o_spec, (M, N)
    if trim is not None and trim[0] == "cols":
        out_spec, out_shape = spec((None, tm, trim[1]), lambda i, j, k: (j, i, 0)), (N // tn, M, trim[1])
    elif trim is not None:
        out_spec, out_shape = spec((None, trim[1], tn), lambda i, j, k: (i, 0, j)), (M // tm, trim[1], N)
    acc_shape = (tm, tn) if nk > 1 else (SUBLANE, LANE)
    in_specs, out_specs, out_shapes, ln_args = [a_spec, b_spec] + [o_spec] * n_add, out_spec, \
        jax.ShapeDtypeStruct(out_shape, F32), ()
    if ln is not None:
        assert tn == N and trim is None, name
        vec = spec((1, tn), lambda i, j, k: (0, j))
        in_specs += [o_spec, vec, vec]
        out_specs = [out_spec, o_spec, o_spec]
        out_shapes = [out_shapes, jax.ShapeDtypeStruct((M, N), F32), jax.ShapeDtypeStruct((M, N), BF16)]
        ln_args = ln
    return pl.pallas_call(
        body, name=name, grid=(N // tn, M // tm, nk) if col_major else (M // tm, N // tn, nk),
        in_specs=in_specs, out_specs=out_specs, out_shape=out_shapes,
        scratch_shapes=[pltpu.VMEM(acc_shape, F32)],
        compiler_params=_params(),
    )(a, b, *adds, *ln_args)


def _rowwise(fn, tiled, full, outs_tiled, outs_acc, *, rows, tile, name, acc_period=None):
    n_tiles = rows // tile
    period = n_tiles if acc_period is None else acc_period
    arrays, in_specs = [], []
    for t in tiled:
        arr, width, cb = t if isinstance(t, tuple) else (t, t.shape[1], 0)
        arrays.append(arr)
        in_specs.append(pl.BlockSpec((tile, width), lambda i, cb=cb: (i, cb)))
    for f in full:
        arr, spec = f if isinstance(f, tuple) else (f, None)
        arrays.append(arr)
        in_specs.append(spec if spec is not None else pl.BlockSpec(arr.shape, lambda i, nd=arr.ndim: (0,) * nd))
    out_shape, out_specs = [], []
    for width, dt in outs_tiled:
        out_shape.append(jax.ShapeDtypeStruct((rows, width), dt))
        out_specs.append(pl.BlockSpec((tile, width), lambda i: (i, 0)))
    for acc in outs_acc:
        shape, dt = acc[0], acc[1]
        out_shape.append(jax.ShapeDtypeStruct(shape, dt))
        out_specs.append(acc[2] if len(acc) > 2 else pl.BlockSpec(shape, lambda i, nd=len(shape): (0,) * nd))
    n_in, n_t, n_a = len(arrays), len(outs_tiled), len(outs_acc)

    def body(*refs):
        vals = [r[...] for r in refs[:n_in]]
        o_t, o_a = fn(*vals)
        for r, v in zip(refs[n_in:n_in + n_t], o_t):
            r[...] = v.astype(r.dtype)
        first = pl.program_id(0) % period == 0
        for r, v in zip(refs[n_in + n_t:n_in + n_t + n_a], o_a):
            v = v.reshape(r.shape)

            @pl.when(first)
            def _(r=r, v=v):
                r[...] = v

            @pl.when(jnp.logical_not(first))
            def _(r=r, v=v):
                r[...] += v

    return pl.pallas_call(
        body, name=name, grid=(n_tiles,), in_specs=in_specs, out_specs=out_specs, out_shape=out_shape,
        compiler_params=_params(),
    )(*arrays)


def _ln_stats(h):
    mu = jnp.mean(h, axis=-1, keepdims=True)
    d = h - mu
    var = jnp.mean(d * d, axis=-1, keepdims=True)
    rstd = lax.rsqrt(var + LN_EPS)
    return d * rstd, rstd


def _ln_bwd_math(h, g, dy):
    xhat, rstd = _ln_stats(h)
    dxhat = dy * g
    dh = rstd * (dxhat - jnp.mean(dxhat, axis=-1, keepdims=True)
                 - xhat * jnp.mean(dxhat * xhat, axis=-1, keepdims=True))
    return dh, jnp.sum(dy * xhat, axis=0, keepdims=True), jnp.sum(dy, axis=0, keepdims=True)


def _ln_bwd(x, r, g, dys, *, name):
    n, d = x.shape
    n_dy = len(dys)

    def fn(x, r, *rest):
        dy = rest[0]
        for e in rest[1:n_dy]:
            dy = dy + e
        dh, dg, db = _ln_bwd_math(DN_ALPHA * x + r, rest[n_dy], dy)
        return (DN_ALPHA * dh, dh), (dg, db)

    return _rowwise(fn, [x, r, *dys], [g], [(d, F32), (d, BF16)], [((1, d), F32), ((1, d), F32)],
                    rows=n, tile=_pick(n, 512, SUBLANE), name=name)


def _final_ln_loss(x, r, target, g, b, *, name):
    n, d = x.shape

    def fn(x, r, t, g, b):
        h = DN_ALPHA * x + r
        xhat, _ = _ln_stats(h)
        err = xhat * g + b - t
        loss = jnp.full((1, LANE), 0.5 * jnp.sum(err * err) / d, F32)
        dh, dg, db = _ln_bwd_math(h, g, err / d)
        return (DN_ALPHA * dh, dh), (loss, dg, db)

    return _rowwise(fn, [x, r, target], [g, b], [(d, F32), (d, BF16)],
                    [((1, LANE), F32), ((1, d), F32), ((1, d), F32)],
                    rows=n, tile=_pick(n, 512, SUBLANE), name=name)


def _mem_heads(qm):
    lane = lax.broadcasted_iota(jnp.int32, (1, MEM_WIDTH), 1)
    for h in range(MEM_HEADS):
        msk = (lane >= h * HEAD_DIM) & (lane < (h + 1) * HEAD_DIM)
        yield msk, jnp.where(msk, qm, 0.0).astype(BF16)


def _mem_softmax(qh, k):
    s = _bdot(qh, k, "nt") * ATT_SCALE
    p = jnp.exp(s - jnp.max(s, axis=-1, keepdims=True))
    return p / jnp.sum(p, axis=-1, keepdims=True)


def _memattn_fwd(tok, proj, memkv, scale, *, seq, name):
    n, tokw = tok.shape
    d = tokw + MEM_WIDTH
    tile = _pick(seq, 512, SUBLANE)

    def fn(tok, qm, kv, scale):
        k, v = kv[:, :MEM_WIDTH].astype(BF16), kv[:, MEM_WIDTH:].astype(BF16)
        out = jnp.zeros(qm.shape, F32)
        for msk, qh in _mem_heads(qm):
            out = jnp.where(msk, _bdot(_mem_softmax(qh, k), v, "nn"), out)
        return (jnp.concatenate([tok * scale, out], axis=1),), ()

    kv_spec = pl.BlockSpec((None,) + memkv.shape[1:], lambda i: (i // (seq // tile), 0, 0))
    return _rowwise(fn, [tok, (proj, MEM_WIDTH, tokw // MEM_WIDTH)], [(memkv, kv_spec), scale], [(d, BF16)], [],
                    rows=n, tile=tile, name=name)[0]


def _memattn_bwd(dmixin, dtok, proj, memkv, *, seq, name):
    n, tokw = dtok.shape
    d = tokw + MEM_WIDTH
    tile = _pick(seq, 512, SUBLANE)

    def fn(dmo, dtok, qm, kv):
        k, v = kv[:, :MEM_WIDTH].astype(BF16), kv[:, MEM_WIDTH:].astype(BF16)
        dq = jnp.zeros(qm.shape, F32)
        dk = jnp.zeros(k.shape, F32)
        dv = jnp.zeros(v.shape, F32)
        for msk, qh in _mem_heads(qm):
            p = _mem_softmax(qh, k)
            doh = jnp.where(msk, dmo, 0.0).astype(BF16)
            dv = dv + _bdot(p, doh, "tn")
            dp = _bdot(doh, v, "nt")
            ds = (p * (dp - jnp.sum(dp * p, axis=-1, keepdims=True))).astype(BF16)
            dq = jnp.where(msk, _bdot(ds, k, "nn") * ATT_SCALE, dq)
            dk = dk + _bdot(ds, qh, "tn") * ATT_SCALE
        return (jnp.concatenate([dtok, dq], axis=1),), (jnp.concatenate([dk, dv], axis=1),)

    tpe = seq // tile
    kv_spec = pl.BlockSpec((None,) + memkv.shape[1:], lambda i: (i // tpe, 0, 0))
    return _rowwise(fn, [(dmixin, MEM_WIDTH, tokw // MEM_WIDTH), dtok, (proj, MEM_WIDTH, tokw // MEM_WIDTH)],
                    [(memkv, kv_spec)], [(d, BF16)], [(memkv.shape, F32, kv_spec)],
                    rows=n, tile=tile, name=name, acc_period=tpe)


def _scale_bwd(dmixin, mixed, scale, *, name):
    n, tokw = mixed.shape

    def fn(dt, mixed, scale):
        return (dt * scale,), (jnp.sum(dt * mixed, axis=0, keepdims=True),)

    return _rowwise(fn, [(dmixin, tokw, 0), mixed], [scale], [(tokw, BF16)], [((1, tokw), F32)],
                    rows=n, tile=_pick(n, 512, SUBLANE), name=name)


def _chunk_rows(seq):
    return _pick(seq, 512, SUBLANE)


def _load_ext(ref, c, rows, before, after, seq):
    lo, hi = c * rows - before, (c + 1) * rows + after
    parts = []
    if lo < 0:
        parts.append(jnp.zeros((-lo, ref.shape[1]), F32))
    parts.append(ref[max(lo, 0):min(hi, seq), :])
    if hi > seq:
        parts.append(jnp.zeros((hi - seq, ref.shape[1]), F32))
    return parts[0] if len(parts) == 1 else jnp.concatenate(parts, axis=0)


def _down(x, k):
    return pltpu.roll(x, k, 0)


def _up(x, k):
    return pltpu.roll(x, x.shape[0] - k, 0)


def _window_sums(ext, shift, col0, group):
    lane = col0 + lax.broadcasted_iota(jnp.int32, (1, ext.shape[1]), 1)
    gidx = lane // group
    s = ext
    out = None
    k = 1
    for gi, w in enumerate(POOL_WINDOWS):
        while k < w:
            s = s + shift(s, k)
            k *= 2
        out = s if out is None else jnp.where(gidx >= gi, s, out)
    return out, jnp.left_shift(2, jnp.minimum(gidx, len(POOL_WINDOWS) - 1))


def _pool_fwd(proj3, tokw, *, name):
    nb, seq, _ = proj3.shape
    rows = _chunk_rows(seq)
    group = tokw // len(POOL_WINDOWS)

    def body(u_ref, o_ref):
        col0 = pl.program_id(1) * LANE
        for c in range(seq // rows):
            ext = _load_ext(u_ref, c, rows, MAX_WINDOW, 0, seq)
            sums, win = _window_sums(ext, _down, col0, group)
            t = c * rows + lax.broadcasted_iota(jnp.int32, (rows, 1), 0)
            count = jnp.minimum(t + 1, win).astype(F32)
            o_ref[c * rows:(c + 1) * rows, :] = (sums[MAX_WINDOW:, :] / count - ext[MAX_WINDOW:, :]).astype(BF16)

    spec = pl.BlockSpec((None, seq, LANE), lambda b, j: (b, 0, j))
    return pl.pallas_call(
        body, name=name, grid=(nb, tokw // LANE), in_specs=[spec], out_specs=spec,
        out_shape=jax.ShapeDtypeStruct((nb, seq, tokw), BF16), compiler_params=_params(),
    )(proj3)


def _pool_bwd(dp3, *, name):
    nb, seq, tokw = dp3.shape
    rows = _chunk_rows(seq)
    group = tokw // len(POOL_WINDOWS)

    def body(d_ref, o_ref):
        col0 = pl.program_id(1) * LANE
        for c in range(seq // rows):
            ext = _load_ext(d_ref, c, rows, 0, MAX_WINDOW, seq)
            lane = col0 + lax.broadcasted_iota(jnp.int32, (1, LANE), 1)
            win = jnp.left_shift(2, jnp.minimum(lane // group, len(POOL_WINDOWS) - 1))
            t = c * rows + lax.broadcasted_iota(jnp.int32, (rows + MAX_WINDOW, 1), 0)
            scaled = ext / jnp.minimum(t + 1, win).astype(F32)
            sums, _ = _window_sums(scaled, _up, col0, group)
            o_ref[c * rows:(c + 1) * rows, :] = sums[:rows, :] - ext[:rows, :]

    spec = pl.BlockSpec((None, seq, LANE), lambda b, j: (b, 0, j))
    return pl.pallas_call(
        body, name=name, grid=(nb, tokw // LANE), in_specs=[spec], out_specs=spec,
        out_shape=jax.ShapeDtypeStruct((nb, seq, tokw), F32), compiler_params=_params(),
    )(dp3)


def _conv3(ext, w_ref, b_ref):
    x1, x2 = _down(ext, 1), _down(ext, 2)
    return w_ref[0:1, :] * x2 + w_ref[1:2, :] * x1 + w_ref[2:3, :] * ext + b_ref[...], x1, x2


def _convgate_fwd(up3, cw, cb, *, name):
    nb, seq, c2 = up3.shape
    fp = c2 // 2
    nblk = fp // LANE
    rows = _chunk_rows(seq)

    def body(u_ref, g_ref, wu_ref, wg_ref, bu_ref, bg_ref, o_ref):
        for c in range(seq // rows):
            hu, _, _ = _conv3(_load_ext(u_ref, c, rows, SUBLANE, 0, seq), wu_ref, bu_ref)
            hg, _, _ = _conv3(_load_ext(g_ref, c, rows, SUBLANE, 0, seq), wg_ref, bg_ref)
            o_ref[c * rows:(c + 1) * rows, :] = (hg * jax.nn.sigmoid(hg) * hu)[SUBLANE:, :].astype(BF16)

    def col(off, r):
        return pl.BlockSpec((r, LANE), lambda b, j: (0, j + off))

    def act(off):
        return pl.BlockSpec((None, seq, LANE), lambda b, j: (b, 0, j + off))

    return pl.pallas_call(
        body, name=name, grid=(nb, nblk),
        in_specs=[act(0), act(nblk), col(0, SUBLANE), col(nblk, SUBLANE), col(0, 1), col(nblk, 1)],
        out_specs=act(0), out_shape=jax.ShapeDtypeStruct((nb, seq, fp), BF16), compiler_params=_params(),
    )(up3, up3, cw, cw, cb, cb)


def _convgate_bwd(up3, dact3, cw, cb, *, name):
    nb, seq, c2 = up3.shape
    fp = c2 // 2
    nblk = fp // LANE
    rows = _chunk_rows(seq)
    h = SUBLANE

    def body(u_ref, g_ref, da_ref, wu_ref, wg_ref, bu_ref, bg_ref, du_ref, dg_ref, dwu_ref, dwg_ref, dbu_ref,
             dbg_ref):
        @pl.when(pl.program_id(1) == 0)
        def _():
            for r in (dwu_ref, dwg_ref, dbu_ref, dbg_ref):
                r[...] = jnp.zeros(r.shape, F32)

        for c in range(seq // rows):
            eu = _load_ext(u_ref, c, rows, h, h, seq)
            eg = _load_ext(g_ref, c, rows, h, h, seq)
            da = _load_ext(da_ref, c, rows, h, h, seq)
            hu, u1, u2 = _conv3(eu, wu_ref, bu_ref)
            hg, g1, g2 = _conv3(eg, wg_ref, bg_ref)
            sig = jax.nn.sigmoid(hg)
            dhu = da * hg * sig
            dhg = da * hu * sig * (1.0 + hg * (1.0 - sig))
            for dh, w_ref, x0, x1, x2, dx_ref, dw_ref, db_ref in (
                    (dhu, wu_ref, eu, u1, u2, du_ref, dwu_ref, dbu_ref),
                    (dhg, wg_ref, eg, g1, g2, dg_ref, dwg_ref, dbg_ref)):
                dx = w_ref[2:3, :] * dh + w_ref[1:2, :] * _up(dh, 1) + w_ref[0:1, :] * _up(dh, 2)
                dx_ref[c * rows:(c + 1) * rows, :] = dx[h:h + rows, :].astype(BF16)
                core = dh[h:h + rows, :]
                for k, xk in ((0, x2), (1, x1), (2, x0)):
                    dw_ref[k:k + 1, :] += jnp.sum(core * xk[h:h + rows, :], axis=0, keepdims=True)
                db_ref[...] += jnp.sum(core, axis=0, keepdims=True)

    def col(off, r):
        return pl.BlockSpec((r, LANE), lambda j, b: (0, j + off))

    def act(off):
        return pl.BlockSpec((None, seq, LANE), lambda j, b: (b, 0, j + off))

    du, dg, dwu, dwg, dbu, dbg = pl.pallas_call(
        body, name=name, grid=(nblk, nb),
        in_specs=[act(0), act(nblk), act(0), col(0, SUBLANE), col(nblk, SUBLANE), col(0, 1), col(nblk, 1)],
        out_specs=[act(0), act(0), col(0, SUBLANE), col(0, SUBLANE), col(0, 1), col(0, 1)],
        out_shape=[jax.ShapeDtypeStruct((nb, seq, fp), BF16), jax.ShapeDtypeStruct((nb, seq, fp), BF16),
                   jax.ShapeDtypeStruct((SUBLANE, fp), F32), jax.ShapeDtypeStruct((SUBLANE, fp), F32),
                   jax.ShapeDtypeStruct((1, fp), F32), jax.ShapeDtypeStruct((1, fp), F32)],
        compiler_params=_params(),
    )(up3, up3, dact3, cw, cw, cb, cb)
    return du, dg, jnp.concatenate([dwu, dwg], axis=1), jnp.concatenate([dbu, dbg], axis=1)


def _scan_rows(x, shift, valid):
    row = lax.broadcasted_iota(jnp.int32, (x.shape[0], 1), 0)
    k = 1
    while k < x.shape[0]:
        x = x + jnp.where(valid(row, k), shift(x, k), 0.0)
        k *= 2
    return x


def _pick_row(x, r):
    row = lax.broadcasted_iota(jnp.int32, (x.shape[0], 1), 0)
    return jnp.sum(jnp.where(row == r, x, 0.0), axis=0, keepdims=True)


def _log_sigmoid(z):
    return jnp.minimum(z, 0.0) - jnp.log(1.0 + jnp.exp(-jnp.abs(z)))


def _gate_fwd(kvf3, fb, col_block, *, name):
    nb, seq, _ = kvf3.shape
    rows = _chunk_rows(seq)

    def body(f_ref, fb_ref, o_ref):
        carry = jnp.zeros((1, LANE), F32)
        for c in range(seq // rows):
            logf = _log_sigmoid(f_ref[c * rows:(c + 1) * rows, :] + fb_ref[...])
            run = _scan_rows(logf, _down, lambda row, k: row >= k) + carry
            o_ref[c * rows:(c + 1) * rows, :] = run
            carry = _pick_row(run, rows - 1)

    return pl.pallas_call(
        body, name=name, grid=(nb,),
        in_specs=[pl.BlockSpec((None, seq, LANE), lambda b: (b, 0, col_block)),
                  pl.BlockSpec((1, LANE), lambda b: (0, 0))],
        out_specs=pl.BlockSpec((None, seq, LANE), lambda b: (b, 0, 0)),
        out_shape=jax.ShapeDtypeStruct((nb, seq, LANE), F32), compiler_params=_params(),
    )(kvf3, fb)


def _gate_bwd(kvf3, fb, dF3, col_block, heads, *, name):
    nb, seq, _ = kvf3.shape
    rows = _chunk_rows(seq)

    def body(f_ref, fb_ref, d_ref, o_ref, dfb_ref):
        @pl.when(pl.program_id(0) == 0)
        def _():
            dfb_ref[...] = jnp.zeros(dfb_ref.shape, F32)

        lane = lax.broadcasted_iota(jnp.int32, (1, LANE), 1)
        carry = jnp.zeros((1, LANE), F32)
        for c in reversed(range(seq // rows)):
            run = _scan_rows(d_ref[c * rows:(c + 1) * rows, :], _up, lambda row, k: row < rows - k) + carry
            carry = _pick_row(run, 0)
            z = f_ref[c * rows:(c + 1) * rows, :] + fb_ref[...]
            df = jnp.where(lane < heads, run * jax.nn.sigmoid(-z), 0.0)
            o_ref[c * rows:(c + 1) * rows, :] = df
            dfb_ref[...] += jnp.sum(df, axis=0, keepdims=True)

    return pl.pallas_call(
        body, name=name, grid=(nb,),
        in_specs=[pl.BlockSpec((None, seq, LANE), lambda b: (b, 0, col_block)),
                  pl.BlockSpec((1, LANE), lambda b: (0, 0)),
                  pl.BlockSpec((None, seq, LANE), lambda b: (b, 0, 0))],
        out_specs=[pl.BlockSpec((None, seq, LANE), lambda b: (b, 0, 0)), pl.BlockSpec((1, LANE), lambda b: (0, 0))],
        out_shape=[jax.ShapeDtypeStruct((nb, seq, LANE), F32), jax.ShapeDtypeStruct((1, LANE), F32)],
        compiler_params=_params(),
    )(kvf3, fb, dF3)


def _head_masks():
    lane = lax.broadcasted_iota(jnp.int32, (1, LANE), 1)
    return (lane < HEAD_DIM, lane >= HEAD_DIM)


BIAS_TERMS = 3


def _bias_lanes(gsum):
    nb, seq, heads = gsum.shape
    terms, rest = [], gsum
    for _ in range(BIAS_TERMS):
        t = lax.reduce_precision(rest, exponent_bits=8, mantissa_bits=7)
        terms.append(t)
        rest = rest - t
    ones = [jnp.ones_like(gsum)] * BIAS_TERMS

    def lanes(parts):
        z = jnp.stack(parts, axis=-1)
        z = jnp.pad(z, ((0, 0), (0, 0), (0, 0), (0, HEAD_DIM - 2 * BIAS_TERMS)))
        z = z.reshape(nb, seq, heads // 2, 2, HEAD_DIM)[:, :, :, ::-1]
        return z.reshape(nb, seq, heads * HEAD_DIM).astype(BF16)

    return lanes(terms + ones), lanes(ones + [-t for t in terms])


def _fox_scores(q, k, aq, ak, masked):
    qs = (q * ATT_SCALE).astype(BF16)
    qts = [jnp.where(msk, qs, aq) for msk in _head_masks()]
    ss = [_bdot(qt, jnp.where(msk, k, ak), "nt") for qt, msk in zip(qts, _head_masks())]
    if masked:
        t = q.shape[0]
        keep = lax.broadcasted_iota(jnp.int32, (t, t), 0) >= lax.broadcasted_iota(jnp.int32, (t, t), 1)
        ss = [jnp.where(keep, s, NEG_BIG) for s in ss]
    return ss, qts


def _on_blocks(qi, ki, step):
    @pl.when(ki < qi)
    def _():
        step(False)

    @pl.when(ki == qi)
    def _():
        step(True)


def _fox_grid(nblk, tokw, t, q_major):
    if q_major:
        pairs = [(qi, ki) for qi in range(nblk) for ki in range(qi + 1)]
    else:
        pairs = [(qi, ki) for ki in range(nblk) for qi in range(ki, nblk)]
    tables = [jnp.array([p[i] for p in pairs], jnp.int32) for i in (0, 1)]

    def q_spec(off=0, wide=False):
        width = 2 * LANE if wide else LANE
        return pl.BlockSpec((None, t, width), lambda b, p, i, qt, kt: (b, qt[i], p + off))

    def k_spec(off=0):
        return pl.BlockSpec((None, t, LANE), lambda b, p, i, qt, kt: (b, kt[i], p + off))

    return tables, len(pairs), q_spec, k_spec, tokw // LANE


def _lanes(col):
    return jnp.broadcast_to(col, (col.shape[0], LANE))


def _across(stat, width):
    return jnp.tile(stat, (1, width // LANE))


def _fox_fwd(proj3, kvf3, aq3, ak3, tokw, *, name):
    nb, seq, _ = proj3.shape
    t = _pick(seq, 512, LANE)
    tables, n_pairs, q_spec, k_spec, hp0 = _fox_grid(seq // t, tokw, t, True)

    def body(qt_ref, kt_ref, q_ref, k_ref, v_ref, aq_ref, ak_ref, o_ref, lse_ref, m_s, l_s, acc_s):
        i = pl.program_id(2)
        qi, ki = qt_ref[i], kt_ref[i]

        @pl.when(ki == 0)
        def _():
            m_s[...] = jnp.full(m_s.shape, NEG_BIG, F32)
            l_s[...] = jnp.zeros(l_s.shape, F32)
            acc_s[...] = jnp.zeros(acc_s.shape, F32)

        def step(masked):
            v = v_ref[...].astype(BF16)
            ss, _ = _fox_scores(q_ref[...], k_ref[...].astype(BF16), aq_ref[...], ak_ref[...], masked)
            for h, s in enumerate(ss):
                m_old = m_s[h]
                m_new = jnp.maximum(m_old, _lanes(jnp.max(s, axis=-1, keepdims=True)))
                alpha = jnp.exp(m_old - m_new)
                p = jnp.exp(s - _across(m_new, t))
                l_s[h] = alpha * l_s[h] + _lanes(jnp.sum(p, axis=-1, keepdims=True))
                acc_s[h] = alpha * acc_s[h] + _bdot(p, v, "nn")
                m_s[h] = m_new

        _on_blocks(qi, ki, step)

        @pl.when(ki == qi)
        def _():
            o_ref[...] = jnp.where(_head_masks()[0], acc_s[0] / l_s[0], acc_s[1] / l_s[1])
            lse_ref[...] = jnp.concatenate([m_s[0] + jnp.log(l_s[0]), m_s[1] + jnp.log(l_s[1])], axis=1)

    stat = pltpu.VMEM((2, t, LANE), F32)
    return pl.pallas_call(
        body, name=name,
        grid_spec=pltpu.PrefetchScalarGridSpec(
            num_scalar_prefetch=2, grid=(nb, hp0, n_pairs),
            in_specs=[q_spec(), k_spec(), k_spec(hp0), q_spec(), k_spec()],
            out_specs=[q_spec(), q_spec(wide=True)], scratch_shapes=[stat, stat, stat]),
        out_shape=[jax.ShapeDtypeStruct((nb, seq, tokw), F32), jax.ShapeDtypeStruct((nb, seq, 2 * tokw), F32)],
        compiler_params=_params(),
    )(*tables, proj3, kvf3, kvf3, aq3, ak3)


def _fox_bwd_common(q_ref, k_ref, v_ref, aq_ref, ak_ref, do_ref, lse_ref, delta_ref, masked):
    k, v = k_ref[...].astype(BF16), v_ref[...].astype(BF16)
    ss, qts = _fox_scores(q_ref[...], k, aq_ref[...], ak_ref[...], masked)
    do = do_ref[...]
    t = do.shape[0]
    out = []
    for h, (s, qt, msk) in enumerate(zip(ss, qts, _head_masks())):
        doh = jnp.where(msk, do, 0.0).astype(BF16)
        p = jnp.exp(s - _across(lse_ref[:, h * LANE:(h + 1) * LANE], t))
        ds = p * (_bdot(doh, v, "nt") - _across(delta_ref[:, h * LANE:(h + 1) * LANE], t))
        out.append((qt, doh, p, ds))
    return out, k


def _fox_bwd(proj3, kvf3, aq3, ak3, o3, dmixin3, lse3, tokw, *, name):
    nb, seq, _ = proj3.shape
    t = _pick(seq, 512, LANE)
    nblk = seq // t
    tables, n_pairs, q_spec, k_spec, hp0 = _fox_grid(nblk, tokw, t, True)
    whole = pl.BlockSpec((None, seq, LANE), lambda b, p, i, qt, kt: (b, 0, p))
    dfk_spec = pl.BlockSpec((None, None, nblk, SUBLANE, t), lambda b, p, i, qt, kt: (b, p, 0, 0, 0))

    def body(qt_ref, kt_ref, q_ref, k_ref, v_ref, aq_ref, ak_ref, o_ref, do_ref, lse_ref, dq_ref, dk_ref, dv_ref,
             dfk_ref, acc_s, row_s, delta_s):
        i = pl.program_id(2)
        qi, ki = qt_ref[i], kt_ref[i]

        @pl.when(i == 0)
        def _():
            dk_ref[...] = jnp.zeros(dk_ref.shape, F32)
            dv_ref[...] = jnp.zeros(dv_ref.shape, F32)
            dfk_ref[...] = jnp.zeros(dfk_ref.shape, F32)

        @pl.when(ki == 0)
        def _():
            acc_s[...] = jnp.zeros(acc_s.shape, F32)
            row_s[...] = jnp.zeros(row_s.shape, F32)
            prod = do_ref[...] * o_ref[...]
            delta_s[...] = jnp.concatenate(
                [_lanes(jnp.sum(jnp.where(msk, prod, 0.0), axis=-1, keepdims=True)) for msk in _head_masks()],
                axis=1)

        def step(masked):
            heads, k = _fox_bwd_common(q_ref, k_ref, v_ref, aq_ref, ak_ref, do_ref, lse_ref, delta_s, masked)
            rows = pl.ds(pl.multiple_of(ki * t, t), t)
            for h, ((qt, doh, p, ds), msk) in enumerate(zip(heads, _head_masks())):
                acc_s[h] += _bdot(ds, k, "nn")
                row_s[h] += _lanes(jnp.sum(ds, axis=-1, keepdims=True))
                dv_ref[rows, :] += _bdot(p, doh, "tn")
                dk_ref[rows, :] += jnp.where(msk, _bdot(ds, qt, "tn"), 0.0)
                dfk_ref[ki, h:h + 1, :] -= jnp.sum(ds, axis=0, keepdims=True)

        _on_blocks(qi, ki, step)

        @pl.when(ki == qi)
        def _():
            dq_ref[...] = jnp.where(_head_masks()[0], acc_s[0], acc_s[1]) * ATT_SCALE
            for h in range(2):
                dfk_ref[qi, 2 + h:3 + h, :] = row_s[h].T[0:1, :]

    out = jax.ShapeDtypeStruct((nb, seq, tokw), F32)
    stat = pltpu.VMEM((2, t, LANE), F32)
    return pl.pallas_call(
        body, name=name,
        grid_spec=pltpu.PrefetchScalarGridSpec(
            num_scalar_prefetch=2, grid=(nb, hp0, n_pairs),
            in_specs=[q_spec(), k_spec(), k_spec(hp0), q_spec(), k_spec(), q_spec(), q_spec(), q_spec(wide=True)],
            out_specs=[q_spec(), whole, whole, dfk_spec],
            scratch_shapes=[stat, stat, pltpu.VMEM((t, 2 * LANE), F32)]),
        out_shape=[out, out, out, jax.ShapeDtypeStruct((nb, hp0, nblk, SUBLANE, t), F32)],
        compiler_params=_params(),
    )(*tables, proj3, kvf3, kvf3, aq3, ak3, o3, dmixin3, lse3)


def _peer(k):
    x, y, c = lax.axis_index("x"), lax.axis_index("y"), lax.axis_index("c")
    return (1 - x if k & 4 else x, 1 - y if k & 2 else y, 1 - c if k & 1 else c)


def _dev_index(p):
    return 4 * p[0] + 2 * p[1] + p[2]


_HBM = pl.BlockSpec(memory_space=pltpu.HBM)
CHIP_RELATIONS = (2, 4, 6)


def _chip_index(p):
    return 2 * p[0] + p[1]


def _run_copies(sends, recvs):
    for cp in sends:
        cp.start()
    for cp in recvs:
        cp.wait_recv()
    for cp in sends:
        cp.wait_send()


def _gather_shards(whole, halved, side_by_side, *, name):
    nw, nh = len(whole), len(halved) + len(side_by_side)
    n = nw + nh
    n_sem = 3 * nw + 6 * nh

    def body(*refs):
        ins, outs, send_sems, recv_sems, local_sems = refs[:n], refs[n:2 * n], refs[2 * n], refs[2 * n + 1], refs[-1]
        me, sib = _peer(0), _peer(1)
        q, c = _chip_index(me), me[2]
        own = []

        def copy(src, dst, s, to):
            return pltpu.make_async_remote_copy(src_ref=src, dst_ref=dst, send_sem=send_sems.at[s],
                                                recv_sem=recv_sems.at[s], device_id=to, device_id_type=MESH_T)

        sends, recvs, passes = [], [], []
        for j, k in enumerate(CHIP_RELATIONS):
            peer = _peer(k)
            pq = _chip_index(peer)
            for i in range(nw):
                sends.append(copy(ins[i], outs[i].at[q], 3 * i + j, peer))
                recvs.append(copy(ins[i], outs[i].at[pq], 3 * i + j, peer))
            for i in range(nh):
                src, out, s = ins[nw + i], outs[nw + i], 3 * nw + 6 * i + j
                if i < len(halved):
                    place = lambda chip, half, out=out: out.at[chip, half]
                else:
                    cols = src.shape[-1]
                    place = lambda chip, half, out=out, cols=cols: out.at[
                        half, :, pl.ds(pl.multiple_of(chip * cols, LANE), cols)]
                    if j == 0:
                        own += [pltpu.make_async_copy(src.at[h], place(q, h), local_sems.at[len(own) + h])
                                for h in range(2)]
                sends.append(copy(src.at[c], place(q, c), s, peer))
                passes.append((copy(src.at[c], place(pq, c), s, peer), copy(place(pq, c), place(pq, c), s + 3, sib),
                               copy(place(pq, c), place(pq, 1 - c), s + 3, sib)))
        for cp in sends + own:
            cp.start()
        for arrival, hand_over, _ in passes:
            arrival.wait_recv()
            hand_over.start()
        for cp in recvs:
            cp.wait_recv()
        for _, _, from_sibling in passes:
            from_sibling.wait_recv()
        for cp in sends + [hand_over for _, hand_over, _ in passes]:
            cp.wait_send()
        for cp in own:
            cp.wait()

    arrays = list(whole) + list(halved) + list(side_by_side)
    return pl.pallas_call(
        body, name=name, in_specs=[_HBM] * n, out_specs=[_HBM] * n,
        out_shape=[jax.ShapeDtypeStruct((N_CHIP,) + a.shape, a.dtype) for a in list(whole) + list(halved)]
        + [jax.ShapeDtypeStruct(a.shape[:-1] + (N_CHIP * a.shape[-1],), a.dtype) for a in side_by_side],
        scratch_shapes=[pltpu.SemaphoreType.DMA((n_sem,)), pltpu.SemaphoreType.DMA((n_sem,)),
                        pltpu.SemaphoreType.DMA((2 * len(side_by_side),))],
    )(*arrays)


def _to_sibling(grads, *, name):
    n = len(grads)

    def body(*refs):
        ins, outs, send_sems, recv_sems = refs[:n], refs[n:2 * n], refs[2 * n], refs[2 * n + 1]
        c = lax.axis_index("c")
        sends = [pltpu.make_async_remote_copy(src_ref=ins[i].at[:, 1 - c], dst_ref=outs[i], send_sem=send_sems.at[i],
                                              recv_sem=recv_sems.at[i], device_id=_peer(1), device_id_type=MESH_T)
                 for i in range(n)]
        _run_copies(sends, sends)

    return pl.pallas_call(
        body, name=name, in_specs=[_HBM] * n, out_specs=[_HBM] * n,
        out_shape=[jax.ShapeDtypeStruct(g.shape[:1] + g.shape[2:], g.dtype) for g in grads],
        scratch_shapes=[pltpu.SemaphoreType.DMA((n,)), pltpu.SemaphoreType.DMA((n,))],
    )(*grads)


def _pair_add(grads, from_sibling, qc, *, name):
    _, _, rows, cols = grads.shape
    tile = _pick(rows, 1024, SUBLANE)

    def body(qc_ref, g_ref, s_ref, o_ref):
        del qc_ref
        o_ref[...] = g_ref[...] + s_ref[...]

    spec = pl.BlockSpec((None, tile, cols), lambda j, i, qc: (j, i, 0))
    return pl.pallas_call(
        body, name=name,
        grid_spec=pltpu.PrefetchScalarGridSpec(
            num_scalar_prefetch=1, grid=(N_CHIP, rows // tile),
            in_specs=[pl.BlockSpec((None, None, tile, cols), lambda j, i, qc: (j, qc[1], i, 0)), spec],
            out_specs=spec),
        out_shape=jax.ShapeDtypeStruct((N_CHIP, rows, cols), F32), compiler_params=_params(),
    )(qc, grads, from_sibling)


def _to_chips(sums, small, *, name):
    n = len(sums)

    def body(*refs):
        ins, small_ref, outs, small_out = refs[:n], refs[n], refs[n + 1:2 * n + 1], refs[2 * n + 1]
        send_sems, recv_sems, local_sem = refs[2 * n + 2:]
        me = _peer(0)

        def copy(src, dst, s, to):
            return pltpu.make_async_remote_copy(src_ref=src, dst_ref=dst, send_sem=send_sems.at[s],
                                                recv_sem=recv_sems.at[s], device_id=to, device_id_type=MESH_T)

        mine = pltpu.make_async_copy(small_ref.at[_dev_index(me)], small_out.at[_dev_index(me)], local_sem)
        mine.start()
        sends, recvs = [], []
        for j, k in enumerate(CHIP_RELATIONS):
            peer = _peer(k)
            for i in range(n):
                src = ins[i].at[_chip_index(peer)]
                sends.append(copy(src, outs[i].at[j], 3 * i + j, peer))
                recvs.append(copy(src, outs[i].at[j], 3 * i + j, peer))
        for k in range(1, N_DEV):
            peer = _peer(k)
            src = small_ref.at[_dev_index(peer)]
            sends.append(copy(src, small_out.at[_dev_index(me)], 3 * n + k - 1, peer))
            recvs.append(copy(src, small_out.at[_dev_index(peer)], 3 * n + k - 1, peer))
        _run_copies(sends, recvs)
        mine.wait()

    n_sem = 3 * n + N_DEV - 1
    return pl.pallas_call(
        body, name=name, in_specs=[_HBM] * (n + 1), out_specs=[_HBM] * (n + 1),
        out_shape=[jax.ShapeDtypeStruct((3,) + g.shape[1:], g.dtype) for g in sums]
        + [jax.ShapeDtypeStruct(small.shape, small.dtype)],
        scratch_shapes=[pltpu.SemaphoreType.DMA((n_sem,)), pltpu.SemaphoreType.DMA((n_sem,)),
                        pltpu.SemaphoreType.DMA],
    )(*sums, small)


def _swap_halves(arrays, *, name):
    n = len(arrays)

    def body(*refs):
        outs, send_sems, recv_sems = refs[n:2 * n], refs[2 * n], refs[2 * n + 1]
        c = lax.axis_index("c")
        sib = _peer(1)
        sends, recvs = [], []
        for i in range(n):
            sem = dict(send_sem=send_sems.at[i], recv_sem=recv_sems.at[i], device_id=sib, device_id_type=MESH_T)
            sends.append(pltpu.make_async_remote_copy(src_ref=outs[i].at[c], dst_ref=outs[i].at[c], **sem))
            recvs.append(pltpu.make_async_remote_copy(src_ref=outs[i].at[c], dst_ref=outs[i].at[1 - c], **sem))
        _run_copies(sends, recvs)

    return pl.pallas_call(
        body, name=name, in_specs=[_HBM] * n, out_specs=[_HBM] * n,
        out_shape=[jax.ShapeDtypeStruct(a.shape, a.dtype) for a in arrays],
        input_output_aliases={i: i for i in range(n)},
        scratch_shapes=[pltpu.SemaphoreType.DMA((n,)), pltpu.SemaphoreType.DMA((n,))],
    )(*arrays)


def _adam_math(g, w, m, v):
    bc1 = 1.0 - ADAM_B1 ** ADAM_STEP
    bc2 = 1.0 - ADAM_B2 ** ADAM_STEP
    m_new = ADAM_B1 * m + (1.0 - ADAM_B1) * g
    v_new = ADAM_B2 * v + (1.0 - ADAM_B2) * (g * g)
    delta = -ADAM_LR * ((m_new / bc1) / (jnp.sqrt(v_new / bc2) + ADAM_EPS) + ADAM_WD * w)
    return delta, m_new, v_new


def _reduce_half(sums, parts, qc, *, name):
    _, rows, cols = sums.shape
    tile = _pick(rows, 1024, SUBLANE)
    n_parts = parts.shape[0]

    def body(qc_ref, g_ref, p_ref, o_ref):
        del qc_ref
        g = g_ref[...]
        for k in range(n_parts):
            g = g + p_ref[k]
        o_ref[...] = g

    return pl.pallas_call(
        body, name=name,
        grid_spec=pltpu.PrefetchScalarGridSpec(
            num_scalar_prefetch=1, grid=(rows // tile,),
            in_specs=[pl.BlockSpec((None, tile, cols), lambda i, qc: (qc[0], i, 0)),
                      pl.BlockSpec((n_parts, tile, cols), lambda i, qc: (0, i, 0))],
            out_specs=pl.BlockSpec((None, tile, cols), lambda i, qc: (qc[1], i, 0))),
        out_shape=jax.ShapeDtypeStruct((2, rows, cols), F32), compiler_params=_params(),
    )(qc, sums, parts)


def _adamw_shard(g, w, m, v, *, name):
    _, rows, cols = g.shape
    tile = _pick(rows, 512, SUBLANE)

    def body(g_ref, w_ref, m_ref, v_ref, do_ref, mo_ref, vo_ref):
        do_ref[...], mo_ref[...], vo_ref[...] = _adam_math(g_ref[...], w_ref[...], m_ref[...], v_ref[...])

    spec = pl.BlockSpec((None, tile, cols), lambda h, i: (h, i, 0))
    return pl.pallas_call(
        body, name=name, grid=(2, rows // tile), in_specs=[spec] * 4, out_specs=[spec] * 3,
        out_shape=[jax.ShapeDtypeStruct(g.shape, F32)] * 3, compiler_params=_params(),
    )(g, w, m, v)


def _adamw(parts, w, m, v, *, name):
    _, rows, cols = parts.shape
    tile = _pick(rows, 256, SUBLANE)

    def body(p_ref, w_ref, m_ref, v_ref, o_ref):
        g = p_ref[0]
        for i in range(1, N_DEV):
            g = g + p_ref[i]
        delta, m_new, v_new = _adam_math(g, w_ref[...], m_ref[...], v_ref[...])
        o_ref[0] = g
        o_ref[1] = delta
        o_ref[2] = m_new
        o_ref[3] = v_new

    spec = pl.BlockSpec((tile, cols), lambda i: (i, 0))
    return pl.pallas_call(
        body, name=name, grid=(rows // tile,),
        in_specs=[pl.BlockSpec((N_DEV, tile, cols), lambda i: (0, i, 0)), spec, spec, spec],
        out_specs=pl.BlockSpec((4, tile, cols), lambda i: (0, i, 0)),
        out_shape=jax.ShapeDtypeStruct((4, rows, cols), F32), compiler_params=_params(),
    )(parts, w, m, v)


def _layout(shapes, names, align):
    out, off = [], 0
    for n in names:
        size = math.prod(shapes[n])
        out.append((n, tuple(shapes[n]), off, size))
        off += _round_up(size, align)
    return out, off


def _pack(arrays, layout, total, lead=()):
    parts = []
    for i, (n, _, off, size) in enumerate(layout):
        end = layout[i + 1][2] if i + 1 < len(layout) else total
        flat = arrays[n].reshape(lead + (size,))
        if end - off > size:
            flat = jnp.pad(flat, [(0, 0)] * len(lead) + [(0, end - off - size)])
        parts.append(flat)
    return jnp.concatenate(parts, axis=len(lead))


def _unpack(flat, layout, lead=()):
    return {n: flat[..., off:off + size].reshape(lead + shape) for n, shape, off, size in layout}


def _to_shards(full, axis):
    shp = full.shape
    return jnp.moveaxis(full.reshape(shp[:axis] + (N_CHIP, shp[axis] // N_CHIP) + shp[axis + 1:]), axis, 0)


def _from_shards(shards, axis):
    x = jnp.moveaxis(shards, 0, axis)
    shp = x.shape
    return x.reshape(shp[:axis] + (shp[axis] * shp[axis + 1],) + shp[axis + 2:])


def _pad_cols(w, per, padded):
    lead = w.shape[:-1]
    x = w.reshape(lead + (-1, per))
    x = jnp.pad(x, [(0, 0)] * len(lead) + [(0, 0), (0, padded - per)])
    return x.reshape(lead + (-1,))


def _unpad_cols(w, per, padded):
    lead = w.shape[:-1]
    return w.reshape(lead + (-1, padded))[..., :per].reshape(lead + (-1,))


def _local_step(x, mem, target, W):
    nb, seq, d = x.shape
    n = nb * seq
    tokw = d - MEM_WIDTH
    heads = tokw // HEAD_DIM
    mlen = mem.shape[1]
    per = W["ffn_w_down"].shape[1] // 2
    per_p = _round_up(per, LANE)
    fp = 2 * per_p
    kvw = 2 * tokw + heads
    kvp = 2 * tokw + LANE
    gate_block = 2 * tokw // LANE

    x2d = x.reshape(n, d)
    mem2d = mem.reshape(nb * mlen, d)
    t2d = target.reshape(n, d)
    row = lambda a: a.reshape(1, -1)
    ones_tok = jnp.ones((1, tokw), F32)

    pool_bd = jax.scipy.linalg.block_diag(*[W["a_pool_w"][0, i] for i in range(len(POOL_WINDOWS))]).astype(BF16)
    kv_w = jnp.pad(W["kv_w"], ((0, 0), (0, kvp - kvw)))
    fb = jnp.pad(W["f_b"], (0, LANE - heads)).reshape(1, LANE)
    w_up = W.get("ffn_w_up_padded")
    if w_up is None:
        w_up = jnp.stack([_pad_cols(W["ffn_w_up"][l], per, per_p) for l in range(DEPTH)])
    w_down = [jnp.pad(W["ffn_w_down"][l].reshape(2, per, d), ((0, 0), (0, per_p - per), (0, 0))).reshape(fp, d)
              for l in range(DEPTH)]
    conv_w = [jnp.pad(_pad_cols(W["ffn_conv_w"][l], per, per_p), ((0, SUBLANE - CONV_WIDTH), (0, 0)))
              for l in range(DEPTH)]
    conv_b = [_pad_cols(W["ffn_conv_b"][l], per, per_p).reshape(1, 2 * fp) for l in range(DEPTH)]
    w_in = [W["a_w_in"][0], W["b_w_q"][0]]
    w_out = [W["a_w_out"][0], W["b_w_out"][0]]

    saved = []
    cur = cur_mm = x2d
    for l in range(DEPTH):
        s = {"x_in": cur, "x_in_mm": cur_mm}
        memkv = _mm(mem2d, W["mem_w_kv"][l], "nn", name=f"memkv{l}").reshape(nb, mlen, 2 * MEM_WIDTH)
        if l == 0:
            proj = _mm(cur_mm, w_in[l], "nn", name="proj0")
            pooled = _pool_fwd(proj.reshape(nb, seq, d), tokw, name="pool_fwd").reshape(n, tokw)
            tok = _mm(pooled, pool_bd, "nn", name="pool_mix")
            scale = W["a_pool_scale"].reshape(1, tokw)
            s.update(pooled=pooled, mixed=tok, scale=scale)
        else:
            kvf = _mm(cur_mm, kv_w, "nn", tn=kvp, name="kvf")
            kvf3 = kvf.reshape(nb, seq, kvp)
            gsum = _gate_fwd(kvf3, fb, gate_block, name="gate_fwd")[:, :, :heads]
            aq3, ak3 = _bias_lanes(gsum)
            proj = _mm(cur_mm, w_in[l], "nn", name="proj1")
            o3, lse3 = _fox_fwd(proj.reshape(nb, seq, d), kvf3, aq3, ak3, tokw, name="fox_fwd")
            tok = o3.reshape(n, tokw)
            scale = ones_tok
            s.update(kvf3=kvf3, aq3=aq3, ak3=ak3, o3=o3, lse3=lse3)
        mixin = _memattn_fwd(tok, proj, memkv, scale, seq=seq, name=f"memattn_fwd{l}")
        mix, x1, x1_mm = _mm(mixin, w_out[l], "nn", ln=(cur, row(W["ln1_g"][l]), row(W["ln1_b"][l])), name=f"mix{l}")
        up = _mm(x1_mm, w_up, "nn", b_lead=l, tm=1024, tn=per_p, col_major=True, name=f"ffn_up{l}")
        act = _convgate_fwd(up.reshape(nb, seq, 2 * fp), conv_w[l], conv_b[l], name=f"convgate_fwd{l}")
        act = act.reshape(n, fp)
        if l + 1 < DEPTH:
            ffn, cur, cur_mm = _mm(act, w_down[l], "nn", tk=fp, ln=(x1, row(W["ln2_g"][l]), row(W["ln2_b"][l])),
                                   name=f"ffn_down{l}")
        else:
            ffn = _mm(act, w_down[l], "nn", tk=fp, name=f"ffn_down{l}")
        s.update(proj=proj, memkv=memkv, mixin=mixin, mix=mix, x1=x1, x1_mm=x1_mm, up=up, act=act, ffn=ffn)
        saved.append(s)

    G = {}
    ln_g = {k: [None] * DEPTH for k in ("ln1_g", "ln1_b", "ln2_g", "ln2_b")}
    stack = {k: [None] * DEPTH for k in ("mem_w_kv", "ffn_w_up", "ffn_conv_w", "ffn_conv_b", "ffn_w_down")}
    dx_terms = None
    loss = None
    for l in reversed(range(DEPTH)):
        s = saved[l]
        g2 = row(W["ln2_g"][l])
        if l == DEPTH - 1:
            dres, dffn, loss, dg, db = _final_ln_loss(s["x1"], s["ffn"], t2d, g2, row(W["ln2_b"][l]),
                                                      name="final_ln_loss")
        else:
            dres, dffn, dg, db = _ln_bwd(s["x1"], s["ffn"], g2, dx_terms, name=f"ln2_bwd{l}")
        ln_g["ln2_g"][l], ln_g["ln2_b"][l] = dg[0], db[0]
        dact = _mm(dffn, w_down[l], "nt", tn=fp, name=f"ffn_down_dx{l}")
        stack["ffn_w_down"][l] = _mm(s["act"], dffn, "tn", tm=per_p, tk=DW_ROWS, trim=("rows", per),
                                     name=f"ffn_down_dw{l}")
        du3, dg3, dcw, dcb = _convgate_bwd(s["up"].reshape(nb, seq, 2 * fp), dact.reshape(nb, seq, fp), conv_w[l],
                                           conv_b[l], name=f"convgate_bwd{l}")
        du, dgt = du3.reshape(n, fp), dg3.reshape(n, fp)
        dx1_u = _mm(du, w_up, "nt", b_lead=l, tk=fp, name=f"ffn_up_dx_u{l}")
        dx1 = _mm(dgt, w_up, "nt", b_lead=l, tk=fp, b_col0=fp, adds=[dx1_u, dres], name=f"ffn_up_dx_g{l}")
        stack["ffn_w_up"][l] = [_mm(s["x1_mm"], part, "tn", tm=d, tn=per_p, tk=DW_ROWS, trim=("cols", per),
                                    name=f"ffn_up_dw_{nm}{l}") for nm, part in (("u", du), ("g", dgt))]
        stack["ffn_conv_w"][l] = dcw[:CONV_WIDTH]
        stack["ffn_conv_b"][l] = dcb[0]
        dres1, dmix, dg, db = _ln_bwd(s["x_in"], s["mix"], row(W["ln1_g"][l]), [dx1], name=f"ln1_bwd{l}")
        ln_g["ln1_g"][l], ln_g["ln1_b"][l] = dg[0], db[0]
        dmixin = _mm(dmix, w_out[l], "nt", name=f"mix_dx{l}")
        d_w_out = _mm(s["mixin"], dmix, "tn", tm=d, tk=DW_ROWS, name=f"mix_dw{l}")
        if l == 0:
            G["a_w_out"] = d_w_out[None]
            dmixed, dscale = _scale_bwd(dmixin, s["mixed"], s["scale"], name="scale_bwd")
            G["a_pool_scale"] = dscale
            dpooled = _mm(dmixed, pool_bd, "nt", name="pool_mix_dx")
            dpw = _mm(s["pooled"], dmixed, "tn", tm=tokw, tk=DW_ROWS, name="pool_mix_dw")
            grp = tokw // len(POOL_WINDOWS)
            G["a_pool_w"] = jnp.stack([dpw[i * grp:(i + 1) * grp, i * grp:(i + 1) * grp]
                                       for i in range(len(POOL_WINDOWS))])[None]
            dtok = _pool_bwd(dpooled.reshape(nb, seq, tokw), name="pool_bwd").reshape(n, tokw)
            extra = []
        else:
            G["b_w_out"] = d_w_out[None]
            p3 = s["proj"].reshape(nb, seq, d)
            dm3 = dmixin.reshape(nb, seq, d)
            dq3, dk3, dv3, dfk = _fox_bwd(p3, s["kvf3"], s["aq3"], s["ak3"], s["o3"], dm3, s["lse3"], tokw,
                                          name="fox_bwd")
            dtok = dq3.reshape(n, tokw)
            dfk = jnp.swapaxes(dfk[:, :, :, 0:2, :] + dfk[:, :, :, 2:4, :], 2, 3).reshape(nb, heads, seq)
            dgsum = jnp.swapaxes(dfk, 1, 2)
            dgsum = jnp.pad(dgsum, ((0, 0), (0, 0), (0, LANE - heads)))
            df3, dfb = _gate_bwd(s["kvf3"], fb, dgsum, gate_block, heads, name="gate_bwd")
            G["f_b"] = dfb[0, :heads]
            dkvf = [(dk3.reshape(n, tokw), 0, "k"), (dv3.reshape(n, tokw), tokw, "v"),
                    (df3.reshape(n, LANE), 2 * tokw, "f")]
            dx_kv = []
            for part, col0, nm in dkvf:
                dx_kv = [_mm(part, kv_w, "nt", b_col0=col0, adds=dx_kv, name=f"kvf_dx_{nm}")]
            extra = dx_kv
            G["kv_w"] = jnp.concatenate([_mm(s["x_in_mm"], part, "tn", tm=d, tk=DW_ROWS, name=f"kvf_dw_{nm}")
                                         for part, _, nm in dkvf], axis=1)[:, :kvw]
        dproj, dmemkv = _memattn_bwd(dmixin, dtok, s["proj"], s["memkv"], seq=seq, name=f"memattn_bwd{l}")
        stack["mem_w_kv"][l] = _mm(mem2d, dmemkv.reshape(nb * mlen, 2 * MEM_WIDTH), "tn", tm=d, tk=DW_ROWS,
                                   name=f"memkv_dw{l}")
        G["a_w_in" if l == 0 else "b_w_q"] = _mm(s["x_in_mm"], dproj, "tn", tm=d, tk=DW_ROWS, name=f"proj_dw{l}")[None]
        if l == 0:
            grad_x = _mm(dproj, w_in[l], "nt", adds=[dres1], name="proj_dx0")
        else:
            dx_terms = [_mm(dproj, w_in[l], "nt", adds=[dres1] + extra, name="proj_dx1")]
    for k, v in ln_g.items():
        G[k] = jnp.stack(v)
    G["mem_w_kv"] = jnp.stack(stack["mem_w_kv"])
    G["ffn_w_up"] = jnp.stack([jnp.concatenate(ug, axis=0) for ug in stack["ffn_w_up"]], axis=1)
    G["ffn_conv_w"] = jnp.stack([_unpad_cols(g, per, per_p) for g in stack["ffn_conv_w"]])
    G["ffn_conv_b"] = jnp.stack([_unpad_cols(g, per, per_p) for g in stack["ffn_conv_b"]])
    G["ffn_w_down"] = jnp.stack([g.reshape(N_CHIP, per // 2, d) for g in stack["ffn_w_down"]], axis=1)
    return loss[0, 0], grad_x.reshape(nb, seq, d), G


def kernel(x, mem, a_w_in, a_pool_w, a_pool_scale, a_w_out, b_w_q, b_w_out, kv_w, f_b, mem_w_kv, ln1_g, ln1_b, ln2_g, ln2_b, ffn_w_up, ffn_conv_w, ffn_conv_b, ffn_w_down, loss_target, m_a_w_in, m_a_pool_w, m_a_pool_scale, m_a_w_out, m_b_w_q, m_b_w_out, m_kv_w, m_f_b, m_mem_w_kv, m_ln1_g, m_ln1_b, m_ln2_g, m_ln2_b, m_ffn_w_up, m_ffn_conv_w, m_ffn_conv_b, m_ffn_w_down, v_a_w_in, v_a_pool_w, v_a_pool_scale, v_a_w_out, v_b_w_q, v_b_w_out, v_kv_w, v_f_b, v_mem_w_kv, v_ln1_g, v_ln1_b, v_ln2_g, v_ln2_b, v_ffn_w_up, v_ffn_conv_w, v_ffn_conv_b, v_ffn_w_down):
    w_loc = dict(a_w_in=a_w_in, a_pool_w=a_pool_w, a_pool_scale=a_pool_scale, a_w_out=a_w_out, b_w_q=b_w_q,
                 b_w_out=b_w_out, kv_w=kv_w, f_b=f_b, mem_w_kv=mem_w_kv, ln1_g=ln1_g, ln1_b=ln1_b, ln2_g=ln2_g,
                 ln2_b=ln2_b, ffn_w_up=ffn_w_up, ffn_conv_w=ffn_conv_w, ffn_conv_b=ffn_conv_b, ffn_w_down=ffn_w_down)
    m_loc = dict(a_w_in=m_a_w_in, a_pool_w=m_a_pool_w, a_pool_scale=m_a_pool_scale, a_w_out=m_a_w_out,
                 b_w_q=m_b_w_q, b_w_out=m_b_w_out, kv_w=m_kv_w, f_b=m_f_b, mem_w_kv=m_mem_w_kv, ln1_g=m_ln1_g,
                 ln1_b=m_ln1_b, ln2_g=m_ln2_g, ln2_b=m_ln2_b, ffn_w_up=m_ffn_w_up, ffn_conv_w=m_ffn_conv_w,
                 ffn_conv_b=m_ffn_conv_b, ffn_w_down=m_ffn_w_down)
    v_loc = dict(a_w_in=v_a_w_in, a_pool_w=v_a_pool_w, a_pool_scale=v_a_pool_scale, a_w_out=v_a_w_out,
                 b_w_q=v_b_w_q, b_w_out=v_b_w_out, kv_w=v_kv_w, f_b=v_f_b, mem_w_kv=v_mem_w_kv, ln1_g=v_ln1_g,
                 ln1_b=v_ln1_b, ln2_g=v_ln2_g, ln2_b=v_ln2_b, ffn_w_up=v_ffn_w_up, ffn_conv_w=v_ffn_conv_w,
                 ffn_conv_b=v_ffn_conv_b, ffn_w_down=v_ffn_w_down)
    x_i, y_i, c = lax.axis_index("x"), lax.axis_index("y"), lax.axis_index("c")
    q = 2 * x_i + y_i
    qc = jnp.stack([q, c]).astype(jnp.int32)
    shapes = {k: v.shape for k, v in w_loc.items()}

    def halves(a):
        if a.ndim == 3 and a.shape[0] == 2:
            return a
        rows = math.prod(a.shape[:-1])
        return a.reshape(2, rows // 2, a.shape[-1])

    own = {k: (w_loc[k] if k in GATHER_F32 else w_loc[k].astype(BF16)) for k in SHARDED}
    by_chip = ("a_pool_scale",) + tuple(k for k in BIG if k != "ffn_w_up")
    per = shapes["ffn_w_up"][-1]
    up_own = jnp.pad(own["ffn_w_up"], ((0, 0), (0, 0), (0, _round_up(per, LANE) - per)))
    gathered = _gather_shards([own["a_pool_scale"]], [halves(own[k]) for k in by_chip[1:]], [up_own],
                              name="gather_weights")
    W = {k: _from_shards(lax.dynamic_update_slice_in_dim(g.reshape((N_CHIP,) + shapes[k]), own[k][None], q, axis=0),
                         SHARD_AXIS[k])
         for k, g in zip(by_chip, gathered)}
    W["ffn_w_up_padded"] = gathered[-1]
    for k in REPLICATED:
        W[k] = w_loc[k]

    loss_part, grad_x, G = _local_step(x, mem, loss_target, W)

    g_chip = [G[k] if k in GRADS_BY_CHIP else _to_shards(G[k], SHARD_AXIS[k]) for k in BIG]
    g_chip = [g.reshape((N_CHIP,) + halves(w_loc[k]).shape) for k, g in zip(BIG, g_chip)]
    lay_r, tot_r = _layout(shapes, REPLICATED, PACK_COLS)
    rep_rows = tot_r // PACK_COLS
    rows = _round_up(rep_rows + 2, SUBLANE)
    scale_w = shapes["a_pool_scale"][-1]

    def small(rep, scale_row, scalar):
        lead = scale_row.shape[:-2]
        pad = [(0, 0)] * len(lead)
        rep = jnp.broadcast_to(_pack(rep, lay_r, tot_r).reshape(rep_rows, PACK_COLS), lead + (rep_rows, PACK_COLS))
        scale_row = jnp.pad(scale_row, pad + [(0, 0), (0, PACK_COLS - scale_w)])
        last = jnp.broadcast_to(jnp.pad(scalar.reshape(1, 1), [(0, rows - rep_rows - 2), (0, PACK_COLS - 1)]),
                                lead + (rows - rep_rows - 1, PACK_COLS))
        return jnp.concatenate([rep, scale_row, last], axis=-2)

    g_scale = jnp.repeat(_to_shards(G["a_pool_scale"], 1), 2, axis=0)
    zero = jnp.zeros((), F32)

    from_sib = _to_sibling(g_chip, name="grads_to_sibling")
    sums = [_pair_add(g, s, qc, name=f"pair_add_{k}") for k, g, s in zip(BIG, g_chip, from_sib)]
    *parts, sm_parts = _to_chips(sums, small(G, g_scale, loss_part), name="scatter_grads")
    grads = _swap_halves([_reduce_half(g, p, qc, name=f"reduce_{k}") for k, g, p in zip(BIG, sums, parts)],
                         name="swap_halves")
    out = {}
    for k, g in zip(BIG, grads):
        upd = _adamw_shard(g, halves(w_loc[k]), halves(m_loc[k]), halves(v_loc[k]), name=f"adamw_{k}")
        out[k] = [r.reshape(shapes[k]) for r in (g, *upd)]
    sm = _adamw(sm_parts, *[small(d, d["a_pool_scale"], zero) for d in (w_loc, m_loc, v_loc)], name="adamw_small")
    loss = sm[0, rep_rows + 1, 0]
    out_r = _unpack(sm[:, :rep_rows].reshape(4, tot_r), lay_r, lead=(4,))
    for k in REPLICATED:
        out[k] = [out_r[k][a] for a in range(4)]
    out["a_pool_scale"] = [sm[a, rep_rows:rep_rows + 1, :scale_w] for a in range(4)]

    outs = [loss, grad_x]
    for a in range(4):
        for k in WEIGHTS:
            outs.append(out[k][a])
    return tuple(outs)
```

```python
import functools
import math

import jax
import jax.numpy as jnp
from jax import lax
from jax.experimental import pallas as pl
from jax.experimental.pallas import tpu as pltpu

F32 = jnp.float32
BF16 = jnp.bfloat16

HEAD_DIM = 64
MEM_HEADS = 4
MEM_WIDTH = MEM_HEADS * HEAD_DIM
POOL_WINDOWS = (2, 4, 8, 16)
MAX_WINDOW = 16
CONV_WIDTH = 3
DEPTH = 2
DN_ALPHA = (2.0 * DEPTH) ** 0.25
LN_EPS = 1e-5
ATT_SCALE = HEAD_DIM ** -0.5
NEG_BIG = -1e30

ADAM_LR = 0.001
ADAM_B1 = 0.9
ADAM_B2 = 0.999
ADAM_EPS = 1e-08
ADAM_WD = 0.01
ADAM_STEP = 10

LANE = 128
SUBLANE = 8
PACK_COLS = 1024
DW_ROWS = 1024
VMEM_LIMIT = 56 * 1024 * 1024
N_DEV = 8
N_CHIP = 4
MESH_T = pl.DeviceIdType.MESH

SHARDED = ("a_w_in", "a_pool_scale", "a_w_out", "b_w_q", "b_w_out", "kv_w", "mem_w_kv", "ffn_w_up",
           "ffn_conv_w", "ffn_w_down")
SHARD_AXIS = {"a_w_in": 1, "a_pool_scale": 1, "a_w_out": 1, "b_w_q": 1, "b_w_out": 1, "kv_w": 1, "mem_w_kv": 1,
              "ffn_w_up": 2, "ffn_conv_w": 2, "ffn_w_down": 1}
GATHER_F32 = ("a_pool_scale", "ffn_conv_w")
BIG = tuple(k for k in SHARDED if k != "a_pool_scale")
GRADS_BY_CHIP = ("ffn_w_up", "ffn_w_down")
REPLICATED = ("a_pool_w", "f_b", "ln1_g", "ln1_b", "ln2_g", "ln2_b", "ffn_conv_b")
WEIGHTS = ("a_w_in", "a_pool_w", "a_pool_scale", "a_w_out", "b_w_q", "b_w_out", "kv_w", "f_b", "mem_w_kv",
           "ln1_g", "ln1_b", "ln2_g", "ln2_b", "ffn_w_up", "ffn_conv_w", "ffn_conv_b", "ffn_w_down")


def _round_up(n, m):
    return -(-n // m) * m


def _pick(dim, pref, unit=LANE):
    if dim <= pref:
        return dim
    t = (pref // unit) * unit
    while t >= unit:
        if dim % t == 0:
            return t
        t -= unit
    raise ValueError(f"no tile for {dim} <= {pref}")


def _params():
    return pltpu.CompilerParams(vmem_limit_bytes=VMEM_LIMIT)


_DIMS = {"nn": ((1,), (0,)), "nt": ((1,), (1,)), "tn": ((0,), (0,))}


def _bdot(a, b, mode):
    return lax.dot_general(a.astype(BF16), b.astype(BF16), (_DIMS[mode], ((), ())), preferred_element_type=F32)


def _mm(a, b, mode, *, name, tm=1024, tn=1024, tk=2048, adds=(), b_col0=0, trim=None, col_major=False, b_lead=None,
        ln=None):
    b_shape = b.shape if b_lead is None else b.shape[1:]
    if mode == "nn":
        (M, K), (K2, N) = a.shape, b_shape
    elif mode == "nt":
        (M, K), (N, K2) = a.shape, b_shape
        K2 = K if b_col0 + K <= K2 else -1
    else:
        (K, M), (K2, N) = a.shape, b_shape
    assert K == K2 and (mode == "nt" or b_col0 == 0), (name, a.shape, b.shape)
    tm, tn = _pick(M, tm, SUBLANE if mode != "tn" else LANE), _pick(N, tn)
    tk = _pick(K, tk, LANE if mode != "tn" else SUBLANE)
    nk = K // tk
    assert b_col0 % tk == 0, (name, b_col0, tk)
    koff = b_col0 // tk
    n_add = len(adds)
    n_ln = 0 if ln is None else 3

    def body(*refs):
        a_ref, b_ref = refs[0], refs[1]
        add_refs = refs[2:2 + n_add]
        ln_refs = refs[2 + n_add:2 + n_add + n_ln]
        o_ref, acc_ref = refs[2 + n_add + n_ln], refs[-1]
        part = _bdot(a_ref[...], b_ref[...], mode)

        def finish(r):
            for ar in add_refs:
                r = r + ar[...]
            if trim is not None:
                r = r[:, :trim[1]] if trim[0] == "cols" else r[:trim[1], :]
            o_ref[...] = r
            if ln is not None:
                x_ref, g_ref, beta_ref = ln_refs
                y_ref, y16_ref = refs[3 + n_add + n_ln], refs[4 + n_add + n_ln]
                xhat, _ = _ln_stats(DN_ALPHA * x_ref[...] + r)
                y = xhat * g_ref[...] + beta_ref[...]
                y_ref[...] = y
                y16_ref[...] = y.astype(BF16)

        if nk == 1:
            finish(part)
        else:
            k = pl.program_id(2)

            @pl.when(k == 0)
            def _():
                acc_ref[...] = part

            @pl.when(k > 0)
            def _():
                acc_ref[...] += part

            @pl.when(k == nk - 1)
            def _():
                finish(acc_ref[...])

    def spec(block, index):
        if col_major:
            return pl.BlockSpec(block, lambda j, i, k: index(i, j, k))
        return pl.BlockSpec(block, index)

    def b_spec_of(block, index):
        if b_lead is None:
            return spec(block, index)
        return spec((None,) + block, lambda i, j, k: (b_lead,) + index(i, j, k))

    if mode == "nn":
        a_spec = spec((tm, tk), lambda i, j, k: (i, k))
        b_spec = b_spec_of((tk, tn), lambda i, j, k: (k, j))
    elif mode == "nt":
        a_spec = spec((tm, tk), lambda i, j, k: (i, k))
        b_spec = b_spec_of((tn, tk), lambda i, j, k: (j, k + koff))
    else:
        a_spec = spec((tk, tm), lambda i, j, k: (k, i))
        b_spec = b_spec_of((tk, tn), lambda i, j, k: (k, j))
    o_spec = spec((tm, tn), lambda i, j, k: (i, j))
    out_spec, out_shape = o_spec, (M, N)
    if trim is not None and trim[0] == "cols":
        out_spec, out_shape = spec((None, tm, trim[1]), lambda i, j, k: (j, i, 0)), (N // tn, M, trim[1])
    elif trim is not None:
        out_spec, out_shape = spec((None, trim[1], tn), lambda i, j, k: (i, 0, j)), (M // tm, trim[1], N)
    acc_shape = (tm, tn) if nk > 1 else (SUBLANE, LANE)
    in_specs, out_specs, out_shapes, ln_args = [a_spec, b_spec] + [o_spec] * n_add, out_spec, \
        jax.ShapeDtypeStruct(out_shape, F32), ()
    if ln is not None:
        assert tn == N and trim is None, name
        vec = spec((1, tn), lambda i, j, k: (0, j))
        in_specs += [o_spec, vec, vec]
        out_specs = [out_spec, o_spec, o_spec]
        out_shapes = [out_shapes, jax.ShapeDtypeStruct((M, N), F32), jax.ShapeDtypeStruct((M, N), BF16)]
        ln_args = ln
    return pl.pallas_call(
        body, name=name, grid=(N // tn, M // tm, nk) if col_major else (M // tm, N // tn, nk),
        in_specs=in_specs, out_specs=out_specs, out_shape=out_shapes,
        scratch_shapes=[pltpu.VMEM(acc_shape, F32)],
        compiler_params=_params(),
    )(a, b, *adds, *ln_args)


def _rowwise(fn, tiled, full, outs_tiled, outs_acc, *, rows, tile, name, acc_period=None):
    n_tiles = rows // tile
    period = n_tiles if acc_period is None else acc_period
    arrays, in_specs = [], []
    for t in tiled:
        arr, width, cb = t if isinstance(t, tuple) else (t, t.shape[1], 0)
        arrays.append(arr)
        in_specs.append(pl.BlockSpec((tile, width), lambda i, cb=cb: (i, cb)))
    for f in full:
        arr, spec = f if isinstance(f, tuple) else (f, None)
        arrays.append(arr)
        in_specs.append(spec if spec is not None else pl.BlockSpec(arr.shape, lambda i, nd=arr.ndim: (0,) * nd))
    out_shape, out_specs = [], []
    for width, dt in outs_tiled:
        out_shape.append(jax.ShapeDtypeStruct((rows, width), dt))
        out_specs.append(pl.BlockSpec((tile, width), lambda i: (i, 0)))
    for acc in outs_acc:
        shape, dt = acc[0], acc[1]
        out_shape.append(jax.ShapeDtypeStruct(shape, dt))
        out_specs.append(acc[2] if len(acc) > 2 else pl.BlockSpec(shape, lambda i, nd=len(shape): (0,) * nd))
    n_in, n_t, n_a = len(arrays), len(outs_tiled), len(outs_acc)

    def body(*refs):
        vals = [r[...] for r in refs[:n_in]]
        o_t, o_a = fn(*vals)
        for r, v in zip(refs[n_in:n_in + n_t], o_t):
            r[...] = v.astype(r.dtype)
        first = pl.program_id(0) % period == 0
        for r, v in zip(refs[n_in + n_t:n_in + n_t + n_a], o_a):
            v = v.reshape(r.shape)

            @pl.when(first)
            def _(r=r, v=v):
                r[...] = v

            @pl.when(jnp.logical_not(first))
            def _(r=r, v=v):
                r[...] += v

    return pl.pallas_call(
        body, name=name, grid=(n_tiles,), in_specs=in_specs, out_specs=out_specs, out_shape=out_shape,
        compiler_params=_params(),
    )(*arrays)


def _ln_stats(h):
    mu = jnp.mean(h, axis=-1, keepdims=True)
    d = h - mu
    var = jnp.mean(d * d, axis=-1, keepdims=True)
    rstd = lax.rsqrt(var + LN_EPS)
    return d * rstd, rstd


def _ln_bwd_math(h, g, dy):
    xhat, rstd = _ln_stats(h)
    dxhat = dy * g
    dh = rstd * (dxhat - jnp.mean(dxhat, axis=-1, keepdims=True)
                 - xhat * jnp.mean(dxhat * xhat, axis=-1, keepdims=True))
    return dh, jnp.sum(dy * xhat, axis=0, keepdims=True), jnp.sum(dy, axis=0, keepdims=True)


def _ln_bwd(x, r, g, dys, *, name):
    n, d = x.shape
    n_dy = len(dys)

    def fn(x, r, *rest):
        dy = rest[0]
        for e in rest[1:n_dy]:
            dy = dy + e
        dh, dg, db = _ln_bwd_math(DN_ALPHA * x + r, rest[n_dy], dy)
        return (DN_ALPHA * dh, dh), (dg, db)

    return _rowwise(fn, [x, r, *dys], [g], [(d, F32), (d, BF16)], [((1, d), F32), ((1, d), F32)],
                    rows=n, tile=_pick(n, 512, SUBLANE), name=name)


def _final_ln_loss(x, r, target, g, b, *, name):
    n, d = x.shape

    def fn(x, r, t, g, b):
        h = DN_ALPHA * x + r
        xhat, _ = _ln_stats(h)
        err = xhat * g + b - t
        loss = jnp.full((1, LANE), 0.5 * jnp.sum(err * err) / d, F32)
        dh, dg, db = _ln_bwd_math(h, g, err / d)
        return (DN_ALPHA * dh, dh), (loss, dg, db)

    return _rowwise(fn, [x, r, target], [g, b], [(d, F32), (d, BF16)],
                    [((1, LANE), F32), ((1, d), F32), ((1, d), F32)],
                    rows=n, tile=_pick(n, 512, SUBLANE), name=name)


def _mem_heads(qm):
    lane = lax.broadcasted_iota(jnp.int32, (1, MEM_WIDTH), 1)
    for h in range(MEM_HEADS):
        msk = (lane >= h * HEAD_DIM) & (lane < (h + 1) * HEAD_DIM)
        yield msk, jnp.where(msk, qm, 0.0).astype(BF16)


def _mem_softmax(qh, k):
    s = _bdot(qh, k, "nt") * ATT_SCALE
    p = jnp.exp(s - jnp.max(s, axis=-1, keepdims=True))
    return p / jnp.sum(p, axis=-1, keepdims=True)


def _memattn_fwd(tok, proj, memkv, scale, *, seq, name):
    n, tokw = tok.shape
    d = tokw + MEM_WIDTH
    tile = _pick(seq, 512, SUBLANE)

    def fn(tok, qm, kv, scale):
        k, v = kv[:, :MEM_WIDTH].astype(BF16), kv[:, MEM_WIDTH:].astype(BF16)
        out = jnp.zeros(qm.shape, F32)
        for msk, qh in _mem_heads(qm):
            out = jnp.where(msk, _bdot(_mem_softmax(qh, k), v, "nn"), out)
        return (jnp.concatenate([tok * scale, out], axis=1),), ()

    kv_spec = pl.BlockSpec((None,) + memkv.shape[1:], lambda i: (i // (seq // tile), 0, 0))
    return _rowwise(fn, [tok, (proj, MEM_WIDTH, tokw // MEM_WIDTH)], [(memkv, kv_spec), scale], [(d, BF16)], [],
                    rows=n, tile=tile, name=name)[0]


def _memattn_bwd(dmixin, dtok, proj, memkv, *, seq, name):
    n, tokw = dtok.shape
    d = tokw + MEM_WIDTH
    tile = _pick(seq, 512, SUBLANE)

    def fn(dmo, dtok, qm, kv):
        k, v = kv[:, :MEM_WIDTH].astype(BF16), kv[:, MEM_WIDTH:].astype(BF16)
        dq = jnp.zeros(qm.shape, F32)
        dk = jnp.zeros(k.shape, F32)
        dv = jnp.zeros(v.shape, F32)
        for msk, qh in _mem_heads(qm):
            p = _mem_softmax(qh, k)
            doh = jnp.where(msk, dmo, 0.0).astype(BF16)
            dv = dv + _bdot(p, doh, "tn")
            dp = _bdot(doh, v, "nt")
            ds = (p * (dp - jnp.sum(dp * p, axis=-1, keepdims=True))).astype(BF16)
            dq = jnp.where(msk, _bdot(ds, k, "nn") * ATT_SCALE, dq)
            dk = dk + _bdot(ds, qh, "tn") * ATT_SCALE
        return (jnp.concatenate([dtok, dq], axis=1),), (jnp.concatenate([dk, dv], axis=1),)

    tpe = seq // tile
    kv_spec = pl.BlockSpec((None,) + memkv.shape[1:], lambda i: (i // tpe, 0, 0))
    return _rowwise(fn, [(dmixin, MEM_WIDTH, tokw // MEM_WIDTH), dtok, (proj, MEM_WIDTH, tokw // MEM_WIDTH)],
                    [(memkv, kv_spec)], [(d, BF16)], [(memkv.shape, F32, kv_spec)],
                    rows=n, tile=tile, name=name, acc_period=tpe)


def _scale_bwd(dmixin, mixed, scale, *, name):
    n, tokw = mixed.shape

    def fn(dt, mixed, scale):
        return (dt * scale,), (jnp.sum(dt * mixed, axis=0, keepdims=True),)

    return _rowwise(fn, [(dmixin, tokw, 0), mixed], [scale], [(tokw, BF16)], [((1, tokw), F32)],
                    rows=n, tile=_pick(n, 512, SUBLANE), name=name)


def _chunk_rows(seq):
    return _pick(seq, 512, SUBLANE)


def _load_ext(ref, c, rows, before, after, seq):
    lo, hi = c * rows - before, (c + 1) * rows + after
    parts = []
    if lo < 0:
        parts.append(jnp.zeros((-lo, ref.shape[1]), F32))
    parts.append(ref[max(lo, 0):min(hi, seq), :])
    if hi > seq:
        parts.append(jnp.zeros((hi - seq, ref.shape[1]), F32))
    return parts[0] if len(parts) == 1 else jnp.concatenate(parts, axis=0)


def _down(x, k):
    return pltpu.roll(x, k, 0)


def _up(x, k):
    return pltpu.roll(x, x.shape[0] - k, 0)


def _window_sums(ext, shift, col0, group):
    lane = col0 + lax.broadcasted_iota(jnp.int32, (1, ext.shape[1]), 1)
    gidx = lane // group
    s = ext
    out = None
    k = 1
    for gi, w in enumerate(POOL_WINDOWS):
        while k < w:
            s = s + shift(s, k)
            k *= 2
        out = s if out is None else jnp.where(gidx >= gi, s, out)
    return out, jnp.left_shift(2, jnp.minimum(gidx, len(POOL_WINDOWS) - 1))


def _pool_fwd(proj3, tokw, *, name):
    nb, seq, _ = proj3.shape
    rows = _chunk_rows(seq)
    group = tokw // len(POOL_WINDOWS)

    def body(u_ref, o_ref):
        col0 = pl.program_id(1) * LANE
        for c in range(seq // rows):
            ext = _load_ext(u_ref, c, rows, MAX_WINDOW, 0, seq)
            sums, win = _window_sums(ext, _down, col0, group)
            t = c * rows + lax.broadcasted_iota(jnp.int32, (rows, 1), 0)
            count = jnp.minimum(t + 1, win).astype(F32)
            o_ref[c * rows:(c + 1) * rows, :] = (sums[MAX_WINDOW:, :] / count - ext[MAX_WINDOW:, :]).astype(BF16)

    spec = pl.BlockSpec((None, seq, LANE), lambda b, j: (b, 0, j))
    return pl.pallas_call(
        body, name=name, grid=(nb, tokw // LANE), in_specs=[spec], out_specs=spec,
        out_shape=jax.ShapeDtypeStruct((nb, seq, tokw), BF16), compiler_params=_params(),
    )(proj3)


def _pool_bwd(dp3, *, name):
    nb, seq, tokw = dp3.shape
    rows = _chunk_rows(seq)
    group = tokw // len(POOL_WINDOWS)

    def body(d_ref, o_ref):
        col0 = pl.program_id(1) * LANE
        for c in range(seq // rows):
            ext = _load_ext(d_ref, c, rows, 0, MAX_WINDOW, seq)
            lane = col0 + lax.broadcasted_iota(jnp.int32, (1, LANE), 1)
            win = jnp.left_shift(2, jnp.minimum(lane // group, len(POOL_WINDOWS) - 1))
            t = c * rows + lax.broadcasted_iota(jnp.int32, (rows + MAX_WINDOW, 1), 0)
            scaled = ext / jnp.minimum(t + 1, win).astype(F32)
            sums, _ = _window_sums(scaled, _up, col0, group)
            o_ref[c * rows:(c + 1) * rows, :] = sums[:rows, :] - ext[:rows, :]

    spec = pl.BlockSpec((None, seq, LANE), lambda b, j: (b, 0, j))
    return pl.pallas_call(
        body, name=name, grid=(nb, tokw // LANE), in_specs=[spec], out_specs=spec,
        out_shape=jax.ShapeDtypeStruct((nb, seq, tokw), F32), compiler_params=_params(),
    )(dp3)


def _conv3(ext, w_ref, b_ref):
    x1, x2 = _down(ext, 1), _down(ext, 2)
    return w_ref[0:1, :] * x2 + w_ref[1:2, :] * x1 + w_ref[2:3, :] * ext + b_ref[...], x1, x2


def _convgate_fwd(up3, cw, cb, *, name):
    nb, seq, c2 = up3.shape
    fp = c2 // 2
    nblk = fp // LANE
    rows = _chunk_rows(seq)

    def body(u_ref, g_ref, wu_ref, wg_ref, bu_ref, bg_ref, o_ref):
        for c in range(seq // rows):
            hu, _, _ = _conv3(_load_ext(u_ref, c, rows, SUBLANE, 0, seq), wu_ref, bu_ref)
            hg, _, _ = _conv3(_load_ext(g_ref, c, rows, SUBLANE, 0, seq), wg_ref, bg_ref)
            o_ref[c * rows:(c + 1) * rows, :] = (hg * jax.nn.sigmoid(hg) * hu)[SUBLANE:, :].astype(BF16)

    def col(off, r):
        return pl.BlockSpec((r, LANE), lambda b, j: (0, j + off))

    def act(off):
        return pl.BlockSpec((None, seq, LANE), lambda b, j: (b, 0, j + off))

    return pl.pallas_call(
        body, name=name, grid=(nb, nblk),
        in_specs=[act(0), act(nblk), col(0, SUBLANE), col(nblk, SUBLANE), col(0, 1), col(nblk, 1)],
        out_specs=act(0), out_shape=jax.ShapeDtypeStruct((nb, seq, fp), BF16), compiler_params=_params(),
    )(up3, up3, cw, cw, cb, cb)


def _convgate_bwd(up3, dact3, cw, cb, *, name):
    nb, seq, c2 = up3.shape
    fp = c2 // 2
    nblk = fp // LANE
    rows = _chunk_rows(seq)
    h = SUBLANE

    def body(u_ref, g_ref, da_ref, wu_ref, wg_ref, bu_ref, bg_ref, du_ref, dg_ref, dwu_ref, dwg_ref, dbu_ref,
             dbg_ref):
        @pl.when(pl.program_id(1) == 0)
        def _():
            for r in (dwu_ref, dwg_ref, dbu_ref, dbg_ref):
                r[...] = jnp.zeros(r.shape, F32)

        for c in range(seq // rows):
            eu = _load_ext(u_ref, c, rows, h, h, seq)
            eg = _load_ext(g_ref, c, rows, h, h, seq)
            da = _load_ext(da_ref, c, rows, h, h, seq)
            hu, u1, u2 = _conv3(eu, wu_ref, bu_ref)
            hg, g1, g2 = _conv3(eg, wg_ref, bg_ref)
            sig = jax.nn.sigmoid(hg)
            dhu = da * hg * sig
            dhg = da * hu * sig * (1.0 + hg * (1.0 - sig))
            for dh, w_ref, x0, x1, x2, dx_ref, dw_ref, db_ref in (
                    (dhu, wu_ref, eu, u1, u2, du_ref, dwu_ref, dbu_ref),
                    (dhg, wg_ref, eg, g1, g2, dg_ref, dwg_ref, dbg_ref)):
                dx = w_ref[2:3, :] * dh + w_ref[1:2, :] * _up(dh, 1) + w_ref[0:1, :] * _up(dh, 2)
                dx_ref[c * rows:(c + 1) * rows, :] = dx[h:h + rows, :].astype(BF16)
                core = dh[h:h + rows, :]
                for k, xk in ((0, x2), (1, x1), (2, x0)):
                    dw_ref[k:k + 1, :] += jnp.sum(core * xk[h:h + rows, :], axis=0, keepdims=True)
                db_ref[...] += jnp.sum(core, axis=0, keepdims=True)

    def col(off, r):
        return pl.BlockSpec((r, LANE), lambda j, b: (0, j + off))

    def act(off):
        return pl.BlockSpec((None, seq, LANE), lambda j, b: (b, 0, j + off))

    du, dg, dwu, dwg, dbu, dbg = pl.pallas_call(
        body, name=name, grid=(nblk, nb),
        in_specs=[act(0), act(nblk), act(0), col(0, SUBLANE), col(nblk, SUBLANE), col(0, 1), col(nblk, 1)],
        out_specs=[act(0), act(0), col(0, SUBLANE), col(0, SUBLANE), col(0, 1), col(0, 1)],
        out_shape=[jax.ShapeDtypeStruct((nb, seq, fp), BF16), jax.ShapeDtypeStruct((nb, seq, fp), BF16),
                   jax.ShapeDtypeStruct((SUBLANE, fp), F32), jax.ShapeDtypeStruct((SUBLANE, fp), F32),
                   jax.ShapeDtypeStruct((1, fp), F32), jax.ShapeDtypeStruct((1, fp), F32)],
        compiler_params=_params(),
    )(up3, up3, dact3, cw, cw, cb, cb)
    return du, dg, jnp.concatenate([dwu, dwg], axis=1), jnp.concatenate([dbu, dbg], axis=1)


def _scan_rows(x, shift, valid):
    row = lax.broadcasted_iota(jnp.int32, (x.shape[0], 1), 0)
    k = 1
    while k < x.shape[0]:
        x = x + jnp.where(valid(row, k), shift(x, k), 0.0)
        k *= 2
    return x


def _pick_row(x, r):
    row = lax.broadcasted_iota(jnp.int32, (x.shape[0], 1), 0)
    return jnp.sum(jnp.where(row == r, x, 0.0), axis=0, keepdims=True)


def _log_sigmoid(z):
    return jnp.minimum(z, 0.0) - jnp.log(1.0 + jnp.exp(-jnp.abs(z)))


def _gate_fwd(kvf3, fb, col_block, *, name):
    nb, seq, _ = kvf3.shape
    rows = _chunk_rows(seq)

    def body(f_ref, fb_ref, o_ref):
        carry = jnp.zeros((1, LANE), F32)
        for c in range(seq // rows):
            logf = _log_sigmoid(f_ref[c * rows:(c + 1) * rows, :] + fb_ref[...])
            run = _scan_rows(logf, _down, lambda row, k: row >= k) + carry
            o_ref[c * rows:(c + 1) * rows, :] = run
            carry = _pick_row(run, rows - 1)

    return pl.pallas_call(
        body, name=name, grid=(nb,),
        in_specs=[pl.BlockSpec((None, seq, LANE), lambda b: (b, 0, col_block)),
                  pl.BlockSpec((1, LANE), lambda b: (0, 0))],
        out_specs=pl.BlockSpec((None, seq, LANE), lambda b: (b, 0, 0)),
        out_shape=jax.ShapeDtypeStruct((nb, seq, LANE), F32), compiler_params=_params(),
    )(kvf3, fb)


def _gate_bwd(kvf3, fb, dF3, col_block, heads, *, name):
    nb, seq, _ = kvf3.shape
    rows = _chunk_rows(seq)

    def body(f_ref, fb_ref, d_ref, o_ref, dfb_ref):
        @pl.when(pl.program_id(0) == 0)
        def _():
            dfb_ref[...] = jnp.zeros(dfb_ref.shape, F32)

        lane = lax.broadcasted_iota(jnp.int32, (1, LANE), 1)
        carry = jnp.zeros((1, LANE), F32)
        for c in reversed(range(seq // rows)):
            run = _scan_rows(d_ref[c * rows:(c + 1) * rows, :], _up, lambda row, k: row < rows - k) + carry
            carry = _pick_row(run, 0)
            z = f_ref[c * rows:(c + 1) * rows, :] + fb_ref[...]
            df = jnp.where(lane < heads, run * jax.nn.sigmoid(-z), 0.0)
            o_ref[c * rows:(c + 1) * rows, :] = df
            dfb_ref[...] += jnp.sum(df, axis=0, keepdims=True)

    return pl.pallas_call(
        body, name=name, grid=(nb,),
        in_specs=[pl.BlockSpec((None, seq, LANE), lambda b: (b, 0, col_block)),
                  pl.BlockSpec((1, LANE), lambda b: (0, 0)),
                  pl.BlockSpec((None, seq, LANE), lambda b: (b, 0, 0))],
        out_specs=[pl.BlockSpec((None, seq, LANE), lambda b: (b, 0, 0)), pl.BlockSpec((1, LANE), lambda b: (0, 0))],
        out_shape=[jax.ShapeDtypeStruct((nb, seq, LANE), F32), jax.ShapeDtypeStruct((1, LANE), F32)],
        compiler_params=_params(),
    )(kvf3, fb, dF3)


def _head_masks():
    lane = lax.broadcasted_iota(jnp.int32, (1, LANE), 1)
    return (lane < HEAD_DIM, lane >= HEAD_DIM)


BIAS_TERMS = 3


def _bias_lanes(gsum):
    nb, seq, heads = gsum.shape
    terms, rest = [], gsum
    for _ in range(BIAS_TERMS):
        t = lax.reduce_precision(rest, exponent_bits=8, mantissa_bits=7)
        terms.append(t)
        rest = rest - t
    ones = [jnp.ones_like(gsum)] * BIAS_TERMS

    def lanes(parts):
        z = jnp.stack(parts, axis=-1)
        z = jnp.pad(z, ((0, 0), (0, 0), (0, 0), (0, HEAD_DIM - 2 * BIAS_TERMS)))
        z = z.reshape(nb, seq, heads // 2, 2, HEAD_DIM)[:, :, :, ::-1]
        return z.reshape(nb, seq, heads * HEAD_DIM).astype(BF16)

    return lanes(terms + ones), lanes(ones + [-t for t in terms])


def _fox_scores(q, k, aq, ak, masked):
    qs = (q * ATT_SCALE).astype(BF16)
    qts = [jnp.where(msk, qs, aq) for msk in _head_masks()]
    ss = [_bdot(qt, jnp.where(msk, k, ak), "nt") for qt, msk in zip(qts, _head_masks())]
    if masked:
        t = q.shape[0]
        keep = lax.broadcasted_iota(jnp.int32, (t, t), 0) >= lax.broadcasted_iota(jnp.int32, (t, t), 1)
        ss = [jnp.where(keep, s, NEG_BIG) for s in ss]
    return ss, qts


def _on_blocks(qi, ki, step):
    @pl.when(ki < qi)
    def _():
        step(False)

    @pl.when(ki == qi)
    def _():
        step(True)


def _fox_grid(nblk, tokw, t, q_major):
    if q_major:
        pairs = [(qi, ki) for qi in range(nblk) for ki in range(qi + 1)]
    else:
        pairs = [(qi, ki) for ki in range(nblk) for qi in range(ki, nblk)]
    tables = [jnp.array([p[i] for p in pairs], jnp.int32) for i in (0, 1)]

    def q_spec(off=0, wide=False):
        width = 2 * LANE if wide else LANE
        return pl.BlockSpec((None, t, width), lambda b, p, i, qt, kt: (b, qt[i], p + off))

    def k_spec(off=0):
        return pl.BlockSpec((None, t, LANE), lambda b, p, i, qt, kt: (b, kt[i], p + off))

    return tables, len(pairs), q_spec, k_spec, tokw // LANE


def _lanes(col):
    return jnp.broadcast_to(col, (col.shape[0], LANE))


def _across(stat, width):
    return jnp.tile(stat, (1, width // LANE))


def _fox_fwd(proj3, kvf3, aq3, ak3, tokw, *, name):
    nb, seq, _ = proj3.shape
    t = _pick(seq, 512, LANE)
    tables, n_pairs, q_spec, k_spec, hp0 = _fox_grid(seq // t, tokw, t, True)

    def body(qt_ref, kt_ref, q_ref, k_ref, v_ref, aq_ref, ak_ref, o_ref, lse_ref, m_s, l_s, acc_s):
        i = pl.program_id(2)
        qi, ki = qt_ref[i], kt_ref[i]

        @pl.when(ki == 0)
        def _():
            m_s[...] = jnp.full(m_s.shape, NEG_BIG, F32)
            l_s[...] = jnp.zeros(l_s.shape, F32)
            acc_s[...] = jnp.zeros(acc_s.shape, F32)

        def step(masked):
            v = v_ref[...].astype(BF16)
            ss, _ = _fox_scores(q_ref[...], k_ref[...].astype(BF16), aq_ref[...], ak_ref[...], masked)
            for h, s in enumerate(ss):
                m_old = m_s[h]
                m_new = jnp.maximum(m_old, _lanes(jnp.max(s, axis=-1, keepdims=True)))
                alpha = jnp.exp(m_old - m_new)
                p = jnp.exp(s - _across(m_new, t))
                l_s[h] = alpha * l_s[h] + _lanes(jnp.sum(p, axis=-1, keepdims=True))
                acc_s[h] = alpha * acc_s[h] + _bdot(p, v, "nn")
                m_s[h] = m_new

        _on_blocks(qi, ki, step)

        @pl.when(ki == qi)
        def _():
            o_ref[...] = jnp.where(_head_masks()[0], acc_s[0] / l_s[0], acc_s[1] / l_s[1])
            lse_ref[...] = jnp.concatenate([m_s[0] + jnp.log(l_s[0]), m_s[1] + jnp.log(l_s[1])], axis=1)

    stat = pltpu.VMEM((2, t, LANE), F32)
    return pl.pallas_call(
        body, name=name,
        grid_spec=pltpu.PrefetchScalarGridSpec(
            num_scalar_prefetch=2, grid=(nb, hp0, n_pairs),
            in_specs=[q_spec(), k_spec(), k_spec(hp0), q_spec(), k_spec()],
            out_specs=[q_spec(), q_spec(wide=True)], scratch_shapes=[stat, stat, stat]),
        out_shape=[jax.ShapeDtypeStruct((nb, seq, tokw), F32), jax.ShapeDtypeStruct((nb, seq, 2 * tokw), F32)],
        compiler_params=_params(),
    )(*tables, proj3, kvf3, kvf3, aq3, ak3)


def _fox_bwd_common(q_ref, k_ref, v_ref, aq_ref, ak_ref, do_ref, lse_ref, delta_ref, masked):
    k, v = k_ref[...].astype(BF16), v_ref[...].astype(BF16)
    ss, qts = _fox_scores(q_ref[...], k, aq_ref[...], ak_ref[...], masked)
    do = do_ref[...]
    t = do.shape[0]
    out = []
    for h, (s, qt, msk) in enumerate(zip(ss, qts, _head_masks())):
        doh = jnp.where(msk, do, 0.0).astype(BF16)
        p = jnp.exp(s - _across(lse_ref[:, h * LANE:(h + 1) * LANE], t))
        ds = p * (_bdot(doh, v, "nt") - _across(delta_ref[:, h * LANE:(h + 1) * LANE], t))
        out.append((qt, doh, p, ds))
    return out, k


def _fox_bwd(proj3, kvf3, aq3, ak3, o3, dmixin3, lse3, tokw, *, name):
    nb, seq, _ = proj3.shape
    t = _pick(seq, 512, LANE)
    nblk = seq // t
    tables, n_pairs, q_spec, k_spec, hp0 = _fox_grid(nblk, tokw, t, True)
    whole = pl.BlockSpec((None, seq, LANE), lambda b, p, i, qt, kt: (b, 0, p))
    dfk_spec = pl.BlockSpec((None, None, nblk, SUBLANE, t), lambda b, p, i, qt, kt: (b, p, 0, 0, 0))

    def body(qt_ref, kt_ref, q_ref, k_ref, v_ref, aq_ref, ak_ref, o_ref, do_ref, lse_ref, dq_ref, dk_ref, dv_ref,
             dfk_ref, acc_s, row_s, delta_s):
        i = pl.program_id(2)
        qi, ki = qt_ref[i], kt_ref[i]

        @pl.when(i == 0)
        def _():
            dk_ref[...] = jnp.zeros(dk_ref.shape, F32)
            dv_ref[...] = jnp.zeros(dv_ref.shape, F32)
            dfk_ref[...] = jnp.zeros(dfk_ref.shape, F32)

        @pl.when(ki == 0)
        def _():
            acc_s[...] = jnp.zeros(acc_s.shape, F32)
            row_s[...] = jnp.zeros(row_s.shape, F32)
            prod = do_ref[...] * o_ref[...]
            delta_s[...] = jnp.concatenate(
                [_lanes(jnp.sum(jnp.where(msk, prod, 0.0), axis=-1, keepdims=True)) for msk in _head_masks()],
                axis=1)

        def step(masked):
            heads, k = _fox_bwd_common(q_ref, k_ref, v_ref, aq_ref, ak_ref, do_ref, lse_ref, delta_s, masked)
            rows = pl.ds(pl.multiple_of(ki * t, t), t)
            for h, ((qt, doh, p, ds), msk) in enumerate(zip(heads, _head_masks())):
                acc_s[h] += _bdot(ds, k, "nn")
                row_s[h] += _lanes(jnp.sum(ds, axis=-1, keepdims=True))
                dv_ref[rows, :] += _bdot(p, doh, "tn")
                dk_ref[rows, :] += jnp.where(msk, _bdot(ds, qt, "tn"), 0.0)
                dfk_ref[ki, h:h + 1, :] -= jnp.sum(ds, axis=0, keepdims=True)

        _on_blocks(qi, ki, step)

        @pl.when(ki == qi)
        def _():
            dq_ref[...] = jnp.where(_head_masks()[0], acc_s[0], acc_s[1]) * ATT_SCALE
            for h in range(2):
                dfk_ref[qi, 2 + h:3 + h, :] = row_s[h].T[0:1, :]

    out = jax.ShapeDtypeStruct((nb, seq, tokw), F32)
    stat = pltpu.VMEM((2, t, LANE), F32)
    return pl.pallas_call(
        body, name=name,
        grid_spec=pltpu.PrefetchScalarGridSpec(
            num_scalar_prefetch=2, grid=(nb, hp0, n_pairs),
            in_specs=[q_spec(), k_spec(), k_spec(hp0), q_spec(), k_spec(), q_spec(), q_spec(), q_spec(wide=True)],
            out_specs=[q_spec(), whole, whole, dfk_spec],
            scratch_shapes=[stat, stat, pltpu.VMEM((t, 2 * LANE), F32)]),
        out_shape=[out, out, out, jax.ShapeDtypeStruct((nb, hp0, nblk, SUBLANE, t), F32)],
        compiler_params=_params(),
    )(*tables, proj3, kvf3, kvf3, aq3, ak3, o3, dmixin3, lse3)


def _peer(k):
    x, y, c = lax.axis_index("x"), lax.axis_index("y"), lax.axis_index("c")
    return (1 - x if k & 4 else x, 1 - y if k & 2 else y, 1 - c if k & 1 else c)


def _dev_index(p):
    return 4 * p[0] + 2 * p[1] + p[2]


_HBM = pl.BlockSpec(memory_space=pltpu.HBM)
CHIP_RELATIONS = (2, 4, 6)


def _chip_index(p):
    return 2 * p[0] + p[1]


def _run_copies(sends, recvs):
    for cp in sends:
        cp.start()
    for cp in recvs:
        cp.wait_recv()
    for cp in sends:
        cp.wait_send()


def _gather_shards(whole, halved, side_by_side, *, name):
    nw, nh = len(whole), len(halved) + len(side_by_side)
    n = nw + nh
    n_sem = 3 * nw + 6 * nh

    def body(*refs):
        ins, outs, send_sems, recv_sems, local_sems = refs[:n], refs[n:2 * n], refs[2 * n], refs[2 * n + 1], refs[-1]
        me, sib = _peer(0), _peer(1)
        q, c = _chip_index(me), me[2]
        own = []

        def copy(src, dst, s, to):
            return pltpu.make_async_remote_copy(src_ref=src, dst_ref=dst, send_sem=send_sems.at[s],
                                                recv_sem=recv_sems.at[s], device_id=to, device_id_type=MESH_T)

        sends, recvs, passes = [], [], []
        for j, k in enumerate(CHIP_RELATIONS):
            peer = _peer(k)
            pq = _chip_index(peer)
            for i in range(nw):
                sends.append(copy(ins[i], outs[i].at[q], 3 * i + j, peer))
                recvs.append(copy(ins[i], outs[i].at[pq], 3 * i + j, peer))
            for i in range(nh):
                src, out, s = ins[nw + i], outs[nw + i], 3 * nw + 6 * i + j
                if i < len(halved):
                    place = lambda chip, half, out=out: out.at[chip, half]
                else:
                    cols = src.shape[-1]
                    place = lambda chip, half, out=out, cols=cols: out.at[
                        half, :, pl.ds(pl.multiple_of(chip * cols, LANE), cols)]
                    if j == 0:
                        own += [pltpu.make_async_copy(src.at[h], place(q, h), local_sems.at[len(own) + h])
                                for h in range(2)]
                sends.append(copy(src.at[c], place(q, c), s, peer))
                passes.append((copy(src.at[c], place(pq, c), s, peer), copy(place(pq, c), place(pq, c), s + 3, sib),
                               copy(place(pq, c), place(pq, 1 - c), s + 3, sib)))
        for cp in sends + own:
            cp.start()
        for arrival, hand_over, _ in passes:
            arrival.wait_recv()
            hand_over.start()
        for cp in recvs:
            cp.wait_recv()
        for _, _, from_sibling in passes:
            from_sibling.wait_recv()
        for cp in sends + [hand_over for _, hand_over, _ in passes]:
            cp.wait_send()
        for cp in own:
            cp.wait()

    arrays = list(whole) + list(halved) + list(side_by_side)
    return pl.pallas_call(
        body, name=name, in_specs=[_HBM] * n, out_specs=[_HBM] * n,
        out_shape=[jax.ShapeDtypeStruct((N_CHIP,) + a.shape, a.dtype) for a in list(whole) + list(halved)]
        + [jax.ShapeDtypeStruct(a.shape[:-1] + (N_CHIP * a.shape[-1],), a.dtype) for a in side_by_side],
        scratch_shapes=[pltpu.SemaphoreType.DMA((n_sem,)), pltpu.SemaphoreType.DMA((n_sem,)),
                        pltpu.SemaphoreType.DMA((2 * len(side_by_side),))],
    )(*arrays)


def _to_sibling(grads, *, name):
    n = len(grads)

    def body(*refs):
        ins, outs, send_sems, recv_sems = refs[:n], refs[n:2 * n], refs[2 * n], refs[2 * n + 1]
        c = lax.axis_index("c")
        sends = [pltpu.make_async_remote_copy(src_ref=ins[i].at[:, 1 - c], dst_ref=outs[i], send_sem=send_sems.at[i],
                                              recv_sem=recv_sems.at[i], device_id=_peer(1), device_id_type=MESH_T)
                 for i in range(n)]
        _run_copies(sends, sends)

    return pl.pallas_call(
        body, name=name, in_specs=[_HBM] * n, out_specs=[_HBM] * n,
        out_shape=[jax.ShapeDtypeStruct(g.shape[:1] + g.shape[2:], g.dtype) for g in grads],
        scratch_shapes=[pltpu.SemaphoreType.DMA((n,)), pltpu.SemaphoreType.DMA((n,))],
    )(*grads)


def _pair_add(grads, from_sibling, qc, *, name):
    _, _, rows, cols = grads.shape
    tile = _pick(rows, 1024, SUBLANE)

    def body(qc_ref, g_ref, s_ref, o_ref):
        del qc_ref
        o_ref[...] = g_ref[...] + s_ref[...]

    spec = pl.BlockSpec((None, tile, cols), lambda j, i, qc: (j, i, 0))
    return pl.pallas_call(
        body, name=name,
        grid_spec=pltpu.PrefetchScalarGridSpec(
            num_scalar_prefetch=1, grid=(N_CHIP, rows // tile),
            in_specs=[pl.BlockSpec((None, None, tile, cols), lambda j, i, qc: (j, qc[1], i, 0)), spec],
            out_specs=spec),
        out_shape=jax.ShapeDtypeStruct((N_CHIP, rows, cols), F32), compiler_params=_params(),
    )(qc, grads, from_sibling)


def _to_chips(sums, small, *, name):
    n = len(sums)

    def body(*refs):
        ins, small_ref, outs, small_out = refs[:n], refs[n], refs[n + 1:2 * n + 1], refs[2 * n + 1]
        send_sems, recv_sems, local_sem = refs[2 * n + 2:]
        me = _peer(0)

        def copy(src, dst, s, to):
            return pltpu.make_async_remote_copy(src_ref=src, dst_ref=dst, send_sem=send_sems.at[s],
                                                recv_sem=recv_sems.at[s], device_id=to, device_id_type=MESH_T)

        mine = pltpu.make_async_copy(small_ref.at[_dev_index(me)], small_out.at[_dev_index(me)], local_sem)
        mine.start()
        sends, recvs = [], []
        for j, k in enumerate(CHIP_RELATIONS):
            peer = _peer(k)
            for i in range(n):
                src = ins[i].at[_chip_index(peer)]
                sends.append(copy(src, outs[i].at[j], 3 * i + j, peer))
                recvs.append(copy(src, outs[i].at[j], 3 * i + j, peer))
        for k in range(1, N_DEV):
            peer = _peer(k)
            src = small_ref.at[_dev_index(peer)]
            sends.append(copy(src, small_out.at[_dev_index(me)], 3 * n + k - 1, peer))
            recvs.append(copy(src, small_out.at[_dev_index(peer)], 3 * n + k - 1, peer))
        _run_copies(sends, recvs)
        mine.wait()

    n_sem = 3 * n + N_DEV - 1
    return pl.pallas_call(
        body, name=name, in_specs=[_HBM] * (n + 1), out_specs=[_HBM] * (n + 1),
        out_shape=[jax.ShapeDtypeStruct((3,) + g.shape[1:], g.dtype) for g in sums]
        + [jax.ShapeDtypeStruct(small.shape, small.dtype)],
        scratch_shapes=[pltpu.SemaphoreType.DMA((n_sem,)), pltpu.SemaphoreType.DMA((n_sem,)),
                        pltpu.SemaphoreType.DMA],
    )(*sums, small)


def _swap_halves(arrays, *, name):
    n = len(arrays)

    def body(*refs):
        outs, send_sems, recv_sems = refs[n:2 * n], refs[2 * n], refs[2 * n + 1]
        c = lax.axis_index("c")
        sib = _peer(1)
        sends, recvs = [], []
        for i in range(n):
            sem = dict(send_sem=send_sems.at[i], recv_sem=recv_sems.at[i], device_id=sib, device_id_type=MESH_T)
            sends.append(pltpu.make_async_remote_copy(src_ref=outs[i].at[c], dst_ref=outs[i].at[c], **sem))
            recvs.append(pltpu.make_async_remote_copy(src_ref=outs[i].at[c], dst_ref=outs[i].at[1 - c], **sem))
        _run_copies(sends, recvs)

    return pl.pallas_call(
        body, name=name, in_specs=[_HBM] * n, out_specs=[_HBM] * n,
        out_shape=[jax.ShapeDtypeStruct(a.shape, a.dtype) for a in arrays],
        input_output_aliases={i: i for i in range(n)},
        scratch_shapes=[pltpu.SemaphoreType.DMA((n,)), pltpu.SemaphoreType.DMA((n,))],
    )(*arrays)


def _adam_math(g, w, m, v):
    bc1 = 1.0 - ADAM_B1 ** ADAM_STEP
    bc2 = 1.0 - ADAM_B2 ** ADAM_STEP
    m_new = ADAM_B1 * m + (1.0 - ADAM_B1) * g
    v_new = ADAM_B2 * v + (1.0 - ADAM_B2) * (g * g)
    delta = -ADAM_LR * ((m_new / bc1) / (jnp.sqrt(v_new / bc2) + ADAM_EPS) + ADAM_WD * w)
    return delta, m_new, v_new


def _reduce_half(sums, parts, qc, *, name):
    _, rows, cols = sums.shape
    tile = _pick(rows, 1024, SUBLANE)
    n_parts = parts.shape[0]

    def body(qc_ref, g_ref, p_ref, o_ref):
        del qc_ref
        g = g_ref[...]
        for k in range(n_parts):
            g = g + p_ref[k]
        o_ref[...] = g

    return pl.pallas_call(
        body, name=name,
        grid_spec=pltpu.PrefetchScalarGridSpec(
            num_scalar_prefetch=1, grid=(rows // tile,),
            in_specs=[pl.BlockSpec((None, tile, cols), lambda i, qc: (qc[0], i, 0)),
                      pl.BlockSpec((n_parts, tile, cols), lambda i, qc: (0, i, 0))],
            out_specs=pl.BlockSpec((None, tile, cols), lambda i, qc: (qc[1], i, 0))),
        out_shape=jax.ShapeDtypeStruct((2, rows, cols), F32), compiler_params=_params(),
    )(qc, sums, parts)


def _adamw_shard(g, w, m, v, *, name):
    _, rows, cols = g.shape
    tile = _pick(rows, 512, SUBLANE)

    def body(g_ref, w_ref, m_ref, v_ref, do_ref, mo_ref, vo_ref):
        do_ref[...], mo_ref[...], vo_ref[...] = _adam_math(g_ref[...], w_ref[...], m_ref[...], v_ref[...])

    spec = pl.BlockSpec((None, tile, cols), lambda h, i: (h, i, 0))
    return pl.pallas_call(
        body, name=name, grid=(2, rows // tile), in_specs=[spec] * 4, out_specs=[spec] * 3,
        out_shape=[jax.ShapeDtypeStruct(g.shape, F32)] * 3, compiler_params=_params(),
    )(g, w, m, v)


def _adamw(parts, w, m, v, *, name):
    _, rows, cols = parts.shape
    tile = _pick(rows, 256, SUBLANE)

    def body(p_ref, w_ref, m_ref, v_ref, o_ref):
        g = p_ref[0]
        for i in range(1, N_DEV):
            g = g + p_ref[i]
        delta, m_new, v_new = _adam_math(g, w_ref[...], m_ref[...], v_ref[...])
        o_ref[0] = g
        o_ref[1] = delta
        o_ref[2] = m_new
        o_ref[3] = v_new

    spec = pl.BlockSpec((tile, cols), lambda i: (i, 0))
    return pl.pallas_call(
        body, name=name, grid=(rows // tile,),
        in_specs=[pl.BlockSpec((N_DEV, tile, cols), lambda i: (0, i, 0)), spec, spec, spec],
        out_specs=pl.BlockSpec((4, tile, cols), lambda i: (0, i, 0)),
        out_shape=jax.ShapeDtypeStruct((4, rows, cols), F32), compiler_params=_params(),
    )(parts, w, m, v)


def _layout(shapes, names, align):
    out, off = [], 0
    for n in names:
        size = math.prod(shapes[n])
        out.append((n, tuple(shapes[n]), off, size))
        off += _round_up(size, align)
    return out, off


def _pack(arrays, layout, total, lead=()):
    parts = []
    for i, (n, _, off, size) in enumerate(layout):
        end = layout[i + 1][2] if i + 1 < len(layout) else total
        flat = arrays[n].reshape(lead + (size,))
        if end - off > size:
            flat = jnp.pad(flat, [(0, 0)] * len(lead) + [(0, end - off - size)])
        parts.append(flat)
    return jnp.concatenate(parts, axis=len(lead))


def _unpack(flat, layout, lead=()):
    return {n: flat[..., off:off + size].reshape(lead + shape) for n, shape, off, size in layout}


def _to_shards(full, axis):
    shp = full.shape
    return jnp.moveaxis(full.reshape(shp[:axis] + (N_CHIP, shp[axis] // N_CHIP) + shp[axis + 1:]), axis, 0)


def _from_shards(shards, axis):
    x = jnp.moveaxis(shards, 0, axis)
    shp = x.shape
    return x.reshape(shp[:axis] + (shp[axis] * shp[axis + 1],) + shp[axis + 2:])


def _pad_cols(w, per, padded):
    lead = w.shape[:-1]
    x = w.reshape(lead + (-1, per))
    x = jnp.pad(x, [(0, 0)] * len(lead) + [(0, 0), (0, padded - per)])
    return x.reshape(lead + (-1,))


def _unpad_cols(w, per, padded):
    lead = w.shape[:-1]
    return w.reshape(lead + (-1, padded))[..., :per].reshape(lead + (-1,))


def _local_step(x, mem, target, W):
    nb, seq, d = x.shape
    n = nb * seq
    tokw = d - MEM_WIDTH
    heads = tokw // HEAD_DIM
    mlen = mem.shape[1]
    per = W["ffn_w_down"].shape[1] // 2
    per_p = _round_up(per, LANE)
    fp = 2 * per_p
    kvw = 2 * tokw + heads
    kvp = 2 * tokw + LANE
    gate_block = 2 * tokw // LANE

    x2d = x.reshape(n, d)
    mem2d = mem.reshape(nb * mlen, d)
    t2d = target.reshape(n, d)
    row = lambda a: a.reshape(1, -1)
    ones_tok = jnp.ones((1, tokw), F32)

    pool_bd = jax.scipy.linalg.block_diag(*[W["a_pool_w"][0, i] for i in range(len(POOL_WINDOWS))]).astype(BF16)
    kv_w = jnp.pad(W["kv_w"], ((0, 0), (0, kvp - kvw)))
    fb = jnp.pad(W["f_b"], (0, LANE - heads)).reshape(1, LANE)
    w_up = W.get("ffn_w_up_padded")
    if w_up is None:
        w_up = jnp.stack([_pad_cols(W["ffn_w_up"][l], per, per_p) for l in range(DEPTH)])
    w_down = [jnp.pad(W["ffn_w_down"][l].reshape(2, per, d), ((0, 0), (0, per_p - per), (0, 0))).reshape(fp, d)
              for l in range(DEPTH)]
    conv_w = [jnp.pad(_pad_cols(W["ffn_conv_w"][l], per, per_p), ((0, SUBLANE - CONV_WIDTH), (0, 0)))
              for l in range(DEPTH)]
    conv_b = [_pad_cols(W["ffn_conv_b"][l], per, per_p).reshape(1, 2 * fp) for l in range(DEPTH)]
    w_in = [W["a_w_in"][0], W["b_w_q"][0]]
    w_out = [W["a_w_out"][0], W["b_w_out"][0]]

    saved = []
    cur = cur_mm = x2d
    for l in range(DEPTH):
        s = {"x_in": cur, "x_in_mm": cur_mm}
        memkv = _mm(mem2d, W["mem_w_kv"][l], "nn", name=f"memkv{l}").reshape(nb, mlen, 2 * MEM_WIDTH)
        if l == 0:
            proj = _mm(cur_mm, w_in[l], "nn", name="proj0")
            pooled = _pool_fwd(proj.reshape(nb, seq, d), tokw, name="pool_fwd").reshape(n, tokw)
            tok = _mm(pooled, pool_bd, "nn", name="pool_mix")
            scale = W["a_pool_scale"].reshape(1, tokw)
            s.update(pooled=pooled, mixed=tok, scale=scale)
        else:
            kvf = _mm(cur_mm, kv_w, "nn", tn=kvp, name="kvf")
            kvf3 = kvf.reshape(nb, seq, kvp)
            gsum = _gate_fwd(kvf3, fb, gate_block, name="gate_fwd")[:, :, :heads]
            aq3, ak3 = _bias_lanes(gsum)
            proj = _mm(cur_mm, w_in[l], "nn", name="proj1")
            o3, lse3 = _fox_fwd(proj.reshape(nb, seq, d), kvf3, aq3, ak3, tokw, name="fox_fwd")
            tok = o3.reshape(n, tokw)
            scale = ones_tok
            s.update(kvf3=kvf3, aq3=aq3, ak3=ak3, o3=o3, lse3=lse3)
        mixin = _memattn_fwd(tok, proj, memkv, scale, seq=seq, name=f"memattn_fwd{l}")
        mix, x1, x1_mm = _mm(mixin, w_out[l], "nn", ln=(cur, row(W["ln1_g"][l]), row(W["ln1_b"][l])), name=f"mix{l}")
        up = _mm(x1_mm, w_up, "nn", b_lead=l, tm=1024, tn=per_p, col_major=True, name=f"ffn_up{l}")
        act = _convgate_fwd(up.reshape(nb, seq, 2 * fp), conv_w[l], conv_b[l], name=f"convgate_fwd{l}")
        act = act.reshape(n, fp)
        if l + 1 < DEPTH:
            ffn, cur, cur_mm = _mm(act, w_down[l], "nn", tm=512, tk=fp, ln=(x1, row(W["ln2_g"][l]), row(W["ln2_b"][l])),
                                   name=f"ffn_down{l}")
        else:
            ffn = _mm(act, w_down[l], "nn", tm=512, tk=fp, name=f"ffn_down{l}")
        s.update(proj=proj, memkv=memkv, mixin=mixin, mix=mix, x1=x1, x1_mm=x1_mm, up=up, act=act, ffn=ffn)
        saved.append(s)

    G = {}
    ln_g = {k: [None] * DEPTH for k in ("ln1_g", "ln1_b", "ln2_g", "ln2_b")}
    stack = {k: [None] * DEPTH for k in ("mem_w_kv", "ffn_w_up", "ffn_conv_w", "ffn_conv_b", "ffn_w_down")}
    dx_terms = None
    loss = None
    for l in reversed(range(DEPTH)):
        s = saved[l]
        g2 = row(W["ln2_g"][l])
        if l == DEPTH - 1:
            dres, dffn, loss, dg, db = _final_ln_loss(s["x1"], s["ffn"], t2d, g2, row(W["ln2_b"][l]),
                                                      name="final_ln_loss")
        else:
            dres, dffn, dg, db = _ln_bwd(s["x1"], s["ffn"], g2, dx_terms, name=f"ln2_bwd{l}")
        ln_g["ln2_g"][l], ln_g["ln2_b"][l] = dg[0], db[0]
        dact = _mm(dffn, w_down[l], "nt", tn=fp, name=f"ffn_down_dx{l}")
        stack["ffn_w_down"][l] = _mm(s["act"], dffn, "tn", tm=per_p, tk=DW_ROWS, trim=("rows", per),
                                     name=f"ffn_down_dw{l}")
        du3, dg3, dcw, dcb = _convgate_bwd(s["up"].reshape(nb, seq, 2 * fp), dact.reshape(nb, seq, fp), conv_w[l],
                                           conv_b[l], name=f"convgate_bwd{l}")
        du, dgt = du3.reshape(n, fp), dg3.reshape(n, fp)
        dx1_u = _mm(du, w_up, "nt", b_lead=l, tm=512, tk=fp, name=f"ffn_up_dx_u{l}")
        dx1 = _mm(dgt, w_up, "nt", b_lead=l, tm=512, tk=fp, b_col0=fp, adds=[dx1_u, dres], name=f"ffn_up_dx_g{l}")
        stack["ffn_w_up"][l] = [_mm(s["x1_mm"], part, "tn", tm=d, tn=per_p, tk=DW_ROWS, trim=("cols", per),
                                    name=f"ffn_up_dw_{nm}{l}") for nm, part in (("u", du), ("g", dgt))]
        stack["ffn_conv_w"][l] = dcw[:CONV_WIDTH]
        stack["ffn_conv_b"][l] = dcb[0]
        dres1, dmix, dg, db = _ln_bwd(s["x_in"], s["mix"], row(W["ln1_g"][l]), [dx1], name=f"ln1_bwd{l}")
        ln_g["ln1_g"][l], ln_g["ln1_b"][l] = dg[0], db[0]
        dmixin = _mm(dmix, w_out[l], "nt", name=f"mix_dx{l}")
        d_w_out = _mm(s["mixin"], dmix, "tn", tm=d, tk=DW_ROWS, name=f"mix_dw{l}")
        if l == 0:
            G["a_w_out"] = d_w_out[None]
            dmixed, dscale = _scale_bwd(dmixin, s["mixed"], s["scale"], name="scale_bwd")
            G["a_pool_scale"] = dscale
            dpooled = _mm(dmixed, pool_bd, "nt", name="pool_mix_dx")
            dpw = _mm(s["pooled"], dmixed, "tn", tm=tokw, tk=DW_ROWS, name="pool_mix_dw")
            grp = tokw // len(POOL_WINDOWS)
            G["a_pool_w"] = jnp.stack([dpw[i * grp:(i + 1) * grp, i * grp:(i + 1) * grp]
                                       for i in range(len(POOL_WINDOWS))])[None]
            dtok = _pool_bwd(dpooled.reshape(nb, seq, tokw), name="pool_bwd").reshape(n, tokw)
            extra = []
        else:
            G["b_w_out"] = d_w_out[None]
            p3 = s["proj"].reshape(nb, seq, d)
            dm3 = dmixin.reshape(nb, seq, d)
            dq3, dk3, dv3, dfk = _fox_bwd(p3, s["kvf3"], s["aq3"], s["ak3"], s["o3"], dm3, s["lse3"], tokw,
                                          name="fox_bwd")
            dtok = dq3.reshape(n, tokw)
            dfk = jnp.swapaxes(dfk[:, :, :, 0:2, :] + dfk[:, :, :, 2:4, :], 2, 3).reshape(nb, heads, seq)
            dgsum = jnp.swapaxes(dfk, 1, 2)
            dgsum = jnp.pad(dgsum, ((0, 0), (0, 0), (0, LANE - heads)))
            df3, dfb = _gate_bwd(s["kvf3"], fb, dgsum, gate_block, heads, name="gate_bwd")
            G["f_b"] = dfb[0, :heads]
            dkvf = [(dk3.reshape(n, tokw), 0, "k"), (dv3.reshape(n, tokw), tokw, "v"),
                    (df3.reshape(n, LANE), 2 * tokw, "f")]
            dx_kv = []
            for part, col0, nm in dkvf:
                dx_kv = [_mm(part, kv_w, "nt", b_col0=col0, adds=dx_kv, name=f"kvf_dx_{nm}")]
            extra = dx_kv
            G["kv_w"] = jnp.concatenate([_mm(s["x_in_mm"], part, "tn", tm=d, tk=DW_ROWS, name=f"kvf_dw_{nm}")
                                         for part, _, nm in dkvf], axis=1)[:, :kvw]
        dproj, dmemkv = _memattn_bwd(dmixin, dtok, s["proj"], s["memkv"], seq=seq, name=f"memattn_bwd{l}")
        stack["mem_w_kv"][l] = _mm(mem2d, dmemkv.reshape(nb * mlen, 2 * MEM_WIDTH), "tn", tm=d, tk=DW_ROWS,
                                   name=f"memkv_dw{l}")
        G["a_w_in" if l == 0 else "b_w_q"] = _mm(s["x_in_mm"], dproj, "tn", tm=d, tk=DW_ROWS, name=f"proj_dw{l}")[None]
        if l == 0:
            grad_x = _mm(dproj, w_in[l], "nt", adds=[dres1], name="proj_dx0")
        else:
            dx_terms = [_mm(dproj, w_in[l], "nt", adds=[dres1] + extra, name="proj_dx1")]
    for k, v in ln_g.items():
        G[k] = jnp.stack(v)
    G["mem_w_kv"] = jnp.stack(stack["mem_w_kv"])
    G["ffn_w_up"] = jnp.stack([jnp.concatenate(ug, axis=0) for ug in stack["ffn_w_up"]], axis=1)
    G["ffn_conv_w"] = jnp.stack([_unpad_cols(g, per, per_p) for g in stack["ffn_conv_w"]])
    G["ffn_conv_b"] = jnp.stack([_unpad_cols(g, per, per_p) for g in stack["ffn_conv_b"]])
    G["ffn_w_down"] = jnp.stack([g.reshape(N_CHIP, per // 2, d) for g in stack["ffn_w_down"]], axis=1)
    return loss[0, 0], grad_x.reshape(nb, seq, d), G


def kernel(x, mem, a_w_in, a_pool_w, a_pool_scale, a_w_out, b_w_q, b_w_out, kv_w, f_b, mem_w_kv, ln1_g, ln1_b, ln2_g, ln2_b, ffn_w_up, ffn_conv_w, ffn_conv_b, ffn_w_down, loss_target, m_a_w_in, m_a_pool_w, m_a_pool_scale, m_a_w_out, m_b_w_q, m_b_w_out, m_kv_w, m_f_b, m_mem_w_kv, m_ln1_g, m_ln1_b, m_ln2_g, m_ln2_b, m_ffn_w_up, m_ffn_conv_w, m_ffn_conv_b, m_ffn_w_down, v_a_w_in, v_a_pool_w, v_a_pool_scale, v_a_w_out, v_b_w_q, v_b_w_out, v_kv_w, v_f_b, v_mem_w_kv, v_ln1_g, v_ln1_b, v_ln2_g, v_ln2_b, v_ffn_w_up, v_ffn_conv_w, v_ffn_conv_b, v_ffn_w_down):
    w_loc = dict(a_w_in=a_w_in, a_pool_w=a_pool_w, a_pool_scale=a_pool_scale, a_w_out=a_w_out, b_w_q=b_w_q,
                 b_w_out=b_w_out, kv_w=kv_w, f_b=f_b, mem_w_kv=mem_w_kv, ln1_g=ln1_g, ln1_b=ln1_b, ln2_g=ln2_g,
                 ln2_b=ln2_b, ffn_w_up=ffn_w_up, ffn_conv_w=ffn_conv_w, ffn_conv_b=ffn_conv_b, ffn_w_down=ffn_w_down)
    m_loc = dict(a_w_in=m_a_w_in, a_pool_w=m_a_pool_w, a_pool_scale=m_a_pool_scale, a_w_out=m_a_w_out,
                 b_w_q=m_b_w_q, b_w_out=m_b_w_out, kv_w=m_kv_w, f_b=m_f_b, mem_w_kv=m_mem_w_kv, ln1_g=m_ln1_g,
                 ln1_b=m_ln1_b, ln2_g=m_ln2_g, ln2_b=m_ln2_b, ffn_w_up=m_ffn_w_up, ffn_conv_w=m_ffn_conv_w,
                 ffn_conv_b=m_ffn_conv_b, ffn_w_down=m_ffn_w_down)
    v_loc = dict(a_w_in=v_a_w_in, a_pool_w=v_a_pool_w, a_pool_scale=v_a_pool_scale, a_w_out=v_a_w_out,
                 b_w_q=v_b_w_q, b_w_out=v_b_w_out, kv_w=v_kv_w, f_b=v_f_b, mem_w_kv=v_mem_w_kv, ln1_g=v_ln1_g,
                 ln1_b=v_ln1_b, ln2_g=v_ln2_g, ln2_b=v_ln2_b, ffn_w_up=v_ffn_w_up, ffn_conv_w=v_ffn_conv_w,
                 ffn_conv_b=v_ffn_conv_b, ffn_w_down=v_ffn_w_down)
    x_i, y_i, c = lax.axis_index("x"), lax.axis_index("y"), lax.axis_index("c")
    q = 2 * x_i + y_i
    qc = jnp.stack([q, c]).astype(jnp.int32)
    shapes = {k: v.shape for k, v in w_loc.items()}

    def halves(a):
        if a.ndim == 3 and a.shape[0] == 2:
            return a
        rows = math.prod(a.shape[:-1])
        return a.reshape(2, rows // 2, a.shape[-1])

    own = {k: (w_loc[k] if k in GATHER_F32 else w_loc[k].astype(BF16)) for k in SHARDED}
    by_chip = ("a_pool_scale",) + tuple(k for k in BIG if k != "ffn_w_up")
    per = shapes["ffn_w_up"][-1]
    up_own = jnp.pad(own["ffn_w_up"], ((0, 0), (0, 0), (0, _round_up(per, LANE) - per)))
    gathered = _gather_shards([own["a_pool_scale"]], [halves(own[k]) for k in by_chip[1:]], [up_own],
                              name="gather_weights")
    W = {k: _from_shards(lax.dynamic_update_slice_in_dim(g.reshape((N_CHIP,) + shapes[k]), own[k][None], q, axis=0),
                         SHARD_AXIS[k])
         for k, g in zip(by_chip, gathered)}
    W["ffn_w_up_padded"] = gathered[-1]
    for k in REPLICATED:
        W[k] = w_loc[k]

    loss_part, grad_x, G = _local_step(x, mem, loss_target, W)

    g_chip = [G[k] if k in GRADS_BY_CHIP else _to_shards(G[k], SHARD_AXIS[k]) for k in BIG]
    g_chip = [g.reshape((N_CHIP,) + halves(w_loc[k]).shape) for k, g in zip(BIG, g_chip)]
    lay_r, tot_r = _layout(shapes, REPLICATED, PACK_COLS)
    rep_rows = tot_r // PACK_COLS
    rows = _round_up(rep_rows + 2, SUBLANE)
    scale_w = shapes["a_pool_scale"][-1]

    def small(rep, scale_row, scalar):
        lead = scale_row.shape[:-2]
        pad = [(0, 0)] * len(lead)
        rep = jnp.broadcast_to(_pack(rep, lay_r, tot_r).reshape(rep_rows, PACK_COLS), lead + (rep_rows, PACK_COLS))
        scale_row = jnp.pad(scale_row, pad + [(0, 0), (0, PACK_COLS - scale_w)])
        last = jnp.broadcast_to(jnp.pad(scalar.reshape(1, 1), [(0, rows - rep_rows - 2), (0, PACK_COLS - 1)]),
                                lead + (rows - rep_rows - 1, PACK_COLS))
        return jnp.concatenate([rep, scale_row, last], axis=-2)

    g_scale = jnp.repeat(_to_shards(G["a_pool_scale"], 1), 2, axis=0)
    zero = jnp.zeros((), F32)

    from_sib = _to_sibling(g_chip, name="grads_to_sibling")
    sums = [_pair_add(g, s, qc, name=f"pair_add_{k}") for k, g, s in zip(BIG, g_chip, from_sib)]
    *parts, sm_parts = _to_chips(sums, small(G, g_scale, loss_part), name="scatter_grads")
    grads = _swap_halves([_reduce_half(g, p, qc, name=f"reduce_{k}") for k, g, p in zip(BIG, sums, parts)],
                         name="swap_halves")
    out = {}
    for k, g in zip(BIG, grads):
        upd = _adamw_shard(g, halves(w_loc[k]), halves(m_loc[k]), halves(v_loc[k]), name=f"adamw_{k}")
        out[k] = [r.reshape(shapes[k]) for r in (g, *upd)]
    sm = _adamw(sm_parts, *[small(d, d["a_pool_scale"], zero) for d in (w_loc, m_loc, v_loc)], name="adamw_small")
    loss = sm[0, rep_rows + 1, 0]
    out_r = _unpack(sm[:, :rep_rows].reshape(4, tot_r), lay_r, lead=(4,))
    for k in REPLICATED:
        out[k] = [out_r[k][a] for a in range(4)]
    out["a_pool_scale"] = [sm[a, rep_rows:rep_rows + 1, :scale_w] for a in range(4)]

    outs = [loss, grad_x]
    for a in range(4):
        for k in WEIGHTS:
            outs.append(out[k][a])
    return tuple(outs)
```

```python
import functools
import math

import jax
import jax.numpy as jnp
from jax import lax
from jax.experimental import pallas as pl
from jax.experimental.pallas import tpu as pltpu

F32 = jnp.float32
BF16 = jnp.bfloat16

HEAD_DIM = 64
MEM_HEADS = 4
MEM_WIDTH = MEM_HEADS * HEAD_DIM
POOL_WINDOWS = (2, 4, 8, 16)
MAX_WINDOW = 16
CONV_WIDTH = 3
DEPTH = 2
DN_ALPHA = (2.0 * DEPTH) ** 0.25
LN_EPS = 1e-5
ATT_SCALE = HEAD_DIM ** -0.5
NEG_BIG = -1e30

ADAM_LR = 0.001
ADAM_B1 = 0.9
ADAM_B2 = 0.999
ADAM_EPS = 1e-08
ADAM_WD = 0.01
ADAM_STEP = 10

LANE = 128
SUBLANE = 8
PACK_COLS = 1024
DW_ROWS = 1024
VMEM_LIMIT = 56 * 1024 * 1024
N_DEV = 8
N_CHIP = 4
MESH_T = pl.DeviceIdType.MESH

SHARDED = ("a_w_in", "a_pool_scale", "a_w_out", "b_w_q", "b_w_out", "kv_w", "mem_w_kv", "ffn_w_up",
           "ffn_conv_w", "ffn_w_down")
SHARD_AXIS = {"a_w_in": 1, "a_pool_scale": 1, "a_w_out": 1, "b_w_q": 1, "b_w_out": 1, "kv_w": 1, "mem_w_kv": 1,
              "ffn_w_up": 2, "ffn_conv_w": 2, "ffn_w_down": 1}
GATHER_F32 = ("a_pool_scale", "ffn_conv_w")
BIG = tuple(k for k in SHARDED if k != "a_pool_scale")
GRADS_BY_CHIP = ("ffn_w_up", "ffn_w_down")
REPLICATED = ("a_pool_w", "f_b", "ln1_g", "ln1_b", "ln2_g", "ln2_b", "ffn_conv_b")
WEIGHTS = ("a_w_in", "a_pool_w", "a_pool_scale", "a_w_out", "b_w_q", "b_w_out", "kv_w", "f_b", "mem_w_kv",
           "ln1_g", "ln1_b", "ln2_g", "ln2_b", "ffn_w_up", "ffn_conv_w", "ffn_conv_b", "ffn_w_down")


def _round_up(n, m):
    return -(-n // m) * m


def _pick(dim, pref, unit=LANE):
    if dim <= pref:
        return dim
    t = (pref // unit) * unit
    while t >= unit:
        if dim % t == 0:
            return t
        t -= unit
    raise ValueError(f"no tile for {dim} <= {pref}")


def _params():
    return pltpu.CompilerParams(vmem_limit_bytes=VMEM_LIMIT)


_DIMS = {"nn": ((1,), (0,)), "nt": ((1,), (1,)), "tn": ((0,), (0,))}


def _bdot(a, b, mode):
    return lax.dot_general(a.astype(BF16), b.astype(BF16), (_DIMS[mode], ((), ())), preferred_element_type=F32)


def _mm(a, b, mode, *, name, tm=1024, tn=1024, tk=2048, adds=(), b_col0=0, trim=None, col_major=False, b_lead=None,
        ln=None, pair=None):
    b_shape = b.shape if b_lead is None else b.shape[1:]
    if mode == "nn":
        (M, K), (K2, N) = a.shape, b_shape
    elif mode == "nt":
        (M, K), (N, K2) = a.shape, b_shape
        K2 = K if b_col0 + K <= K2 else -1
    else:
        (K, M), (K2, N) = a.shape, b_shape
    assert K == K2 and (mode == "nt" or b_col0 == 0), (name, a.shape, b.shape)
    tm, tn = _pick(M, tm, SUBLANE if mode != "tn" else LANE), _pick(N, tn)
    tk = _pick(K, tk, LANE if mode != "tn" else SUBLANE)
    nk = K // tk
    assert b_col0 % tk == 0, (name, b_col0, tk)
    koff = b_col0 // tk
    n_add = len(adds)
    n_ln = 0 if ln is None else 3
    n_pair = 0 if pair is None else 2
    assert pair is None or (mode == "nt" and nk == 1 and ln is None and pair[1] % tk == 0), name

    def body(*refs):
        a_ref, b_ref = refs[0], refs[1]
        add_refs = refs[2:2 + n_add]
        ln_refs = refs[2 + n_add:2 + n_add + n_ln]
        o_ref, acc_ref = refs[2 + n_add + n_ln + n_pair], refs[-1]
        part = _bdot(a_ref[...], b_ref[...], mode)
        if pair is not None:
            part = part + _bdot(refs[2 + n_add][...], refs[3 + n_add][...], mode)

        def finish(r):
            for ar in add_refs:
                r = r + ar[...]
            if trim is not None:
                r = r[:, :trim[1]] if trim[0] == "cols" else r[:trim[1], :]
            o_ref[...] = r
            if ln is not None:
                x_ref, g_ref, beta_ref = ln_refs
                y_ref, y16_ref = refs[3 + n_add + n_ln], refs[4 + n_add + n_ln]
                xhat, _ = _ln_stats(DN_ALPHA * x_ref[...] + r)
                y = xhat * g_ref[...] + beta_ref[...]
                y_ref[...] = y
                y16_ref[...] = y.astype(BF16)

        if nk == 1:
            finish(part)
        else:
            k = pl.program_id(2)

            @pl.when(k == 0)
            def _():
                acc_ref[...] = part

            @pl.when(k > 0)
            def _():
                acc_ref[...] += part

            @pl.when(k == nk - 1)
            def _():
                finish(acc_ref[...])

    def spec(block, index):
        if col_major:
            return pl.BlockSpec(block, lambda j, i, k: index(i, j, k))
        return pl.BlockSpec(block, index)

    def b_spec_of(block, index):
        if b_lead is None:
            return spec(block, index)
        return spec((None,) + block, lambda i, j, k: (b_lead,) + index(i, j, k))

    if mode == "nn":
        a_spec = spec((tm, tk), lambda i, j, k: (i, k))
        b_spec = b_spec_of((tk, tn), lambda i, j, k: (k, j))
    elif mode == "nt":
        a_spec = spec((tm, tk), lambda i, j, k: (i, k))
        b_spec = b_spec_of((tn, tk), lambda i, j, k: (j, k + koff))
    else:
        a_spec = spec((tk, tm), lambda i, j, k: (k, i))
        b_spec = b_spec_of((tk, tn), lambda i, j, k: (k, j))
    o_spec = spec((tm, tn), lambda i, j, k: (i, j))
    out_spec, out_shape = o_spec, (M, N)
    if trim is not None and trim[0] == "cols":
        out_spec, out_shape = spec((None, tm, trim[1]), lambda i, j, k: (j, i, 0)), (N // tn, M, trim[1])
    elif trim is not None:
        out_spec, out_shape = spec((None, trim[1], tn), lambda i, j, k: (i, 0, j)), (M // tm, trim[1], N)
    acc_shape = (tm, tn) if nk > 1 else (SUBLANE, LANE)
    in_specs, out_specs, out_shapes, ln_args = [a_spec, b_spec] + [o_spec] * n_add, out_spec, \
        jax.ShapeDtypeStruct(out_shape, F32), ()
    if pair is not None:
        in_specs += [a_spec, b_spec_of((tn, tk), lambda i, j, k: (j, k + pair[1] // tk))]
        ln_args = (pair[0], b)
    if ln is not None:
        assert tn == N and trim is None, name
        vec = spec((1, tn), lambda i, j, k: (0, j))
        in_specs += [o_spec, vec, vec]
        out_specs = [out_spec, o_spec, o_spec]
        out_shapes = [out_shapes, jax.ShapeDtypeStruct((M, N), F32), jax.ShapeDtypeStruct((M, N), BF16)]
        ln_args = ln
    return pl.pallas_call(
        body, name=name, grid=(N // tn, M // tm, nk) if col_major else (M // tm, N // tn, nk),
        in_specs=in_specs, out_specs=out_specs, out_shape=out_shapes,
        scratch_shapes=[pltpu.VMEM(acc_shape, F32)],
        compiler_params=_params(),
    )(a, b, *adds, *ln_args)


def _rowwise(fn, tiled, full, outs_tiled, outs_acc, *, rows, tile, name, acc_period=None):
    n_tiles = rows // tile
    period = n_tiles if acc_period is None else acc_period
    arrays, in_specs = [], []
    for t in tiled:
        arr, width, cb = t if isinstance(t, tuple) else (t, t.shape[1], 0)
        arrays.append(arr)
        in_specs.append(pl.BlockSpec((tile, width), lambda i, cb=cb: (i, cb)))
    for f in full:
        arr, spec = f if isinstance(f, tuple) else (f, None)
        arrays.append(arr)
        in_specs.append(spec if spec is not None else pl.BlockSpec(arr.shape, lambda i, nd=arr.ndim: (0,) * nd))
    out_shape, out_specs = [], []
    for width, dt in outs_tiled:
        out_shape.append(jax.ShapeDtypeStruct((rows, width), dt))
        out_specs.append(pl.BlockSpec((tile, width), lambda i: (i, 0)))
    for acc in outs_acc:
        shape, dt = acc[0], acc[1]
        out_shape.append(jax.ShapeDtypeStruct(shape, dt))
        out_specs.append(acc[2] if len(acc) > 2 else pl.BlockSpec(shape, lambda i, nd=len(shape): (0,) * nd))
    n_in, n_t, n_a = len(arrays), len(outs_tiled), len(outs_acc)

    def body(*refs):
        vals = [r[...] for r in refs[:n_in]]
        o_t, o_a = fn(*vals)
        for r, v in zip(refs[n_in:n_in + n_t], o_t):
            r[...] = v.astype(r.dtype)
        first = pl.program_id(0) % period == 0
        for r, v in zip(refs[n_in + n_t:n_in + n_t + n_a], o_a):
            v = v.reshape(r.shape)

            @pl.when(first)
            def _(r=r, v=v):
                r[...] = v

            @pl.when(jnp.logical_not(first))
            def _(r=r, v=v):
                r[...] += v

    return pl.pallas_call(
        body, name=name, grid=(n_tiles,), in_specs=in_specs, out_specs=out_specs, out_shape=out_shape,
        compiler_params=_params(),
    )(*arrays)


def _ln_stats(h):
    mu = jnp.mean(h, axis=-1, keepdims=True)
    d = h - mu
    var = jnp.mean(d * d, axis=-1, keepdims=True)
    rstd = lax.rsqrt(var + LN_EPS)
    return d * rstd, rstd


def _ln_bwd_math(h, g, dy):
    xhat, rstd = _ln_stats(h)
    dxhat = dy * g
    dh = rstd * (dxhat - jnp.mean(dxhat, axis=-1, keepdims=True)
                 - xhat * jnp.mean(dxhat * xhat, axis=-1, keepdims=True))
    return dh, jnp.sum(dy * xhat, axis=0, keepdims=True), jnp.sum(dy, axis=0, keepdims=True)


def _ln_bwd(x, r, g, dys, *, name):
    n, d = x.shape
    n_dy = len(dys)

    def fn(x, r, *rest):
        dy = rest[0]
        for e in rest[1:n_dy]:
            dy = dy + e
        dh, dg, db = _ln_bwd_math(DN_ALPHA * x + r, rest[n_dy], dy)
        return (DN_ALPHA * dh, dh), (dg, db)

    return _rowwise(fn, [x, r, *dys], [g], [(d, F32), (d, BF16)], [((1, d), F32), ((1, d), F32)],
                    rows=n, tile=_pick(n, 512, SUBLANE), name=name)


def _final_ln_loss(x, r, target, g, b, *, name):
    n, d = x.shape

    def fn(x, r, t, g, b):
        h = DN_ALPHA * x + r
        xhat, _ = _ln_stats(h)
        err = xhat * g + b - t
        loss = jnp.full((1, LANE), 0.5 * jnp.sum(err * err) / d, F32)
        dh, dg, db = _ln_bwd_math(h, g, err / d)
        return (DN_ALPHA * dh, dh), (loss, dg, db)

    return _rowwise(fn, [x, r, target], [g, b], [(d, F32), (d, BF16)],
                    [((1, LANE), F32), ((1, d), F32), ((1, d), F32)],
                    rows=n, tile=_pick(n, 512, SUBLANE), name=name)


def _mem_heads(qm):
    lane = lax.broadcasted_iota(jnp.int32, (1, MEM_WIDTH), 1)
    for h in range(MEM_HEADS):
        msk = (lane >= h * HEAD_DIM) & (lane < (h + 1) * HEAD_DIM)
        yield msk, jnp.where(msk, qm, 0.0).astype(BF16)


def _mem_softmax(qh, k):
    s = _bdot(qh, k, "nt") * ATT_SCALE
    p = jnp.exp(s - jnp.max(s, axis=-1, keepdims=True))
    return p / jnp.sum(p, axis=-1, keepdims=True)


def _memattn_fwd(tok, proj, memkv, scale, *, seq, name):
    n, tokw = tok.shape
    d = tokw + MEM_WIDTH
    tile = _pick(seq, 512, SUBLANE)

    def fn(tok, qm, kv, scale):
        k, v = kv[:, :MEM_WIDTH].astype(BF16), kv[:, MEM_WIDTH:].astype(BF16)
        out = jnp.zeros(qm.shape, F32)
        for msk, qh in _mem_heads(qm):
            out = jnp.where(msk, _bdot(_mem_softmax(qh, k), v, "nn"), out)
        return (jnp.concatenate([tok * scale, out], axis=1),), ()

    kv_spec = pl.BlockSpec((None,) + memkv.shape[1:], lambda i: (i // (seq // tile), 0, 0))
    return _rowwise(fn, [tok, (proj, MEM_WIDTH, tokw // MEM_WIDTH)], [(memkv, kv_spec), scale], [(d, BF16)], [],
                    rows=n, tile=tile, name=name)[0]


def _memattn_bwd(dmixin, dtok, proj, memkv, *, seq, name):
    n, tokw = dtok.shape
    d = tokw + MEM_WIDTH
    tile = _pick(seq, 512, SUBLANE)

    def fn(dmo, dtok, qm, kv):
        k, v = kv[:, :MEM_WIDTH].astype(BF16), kv[:, MEM_WIDTH:].astype(BF16)
        dq = jnp.zeros(qm.shape, F32)
        dk = jnp.zeros(k.shape, F32)
        dv = jnp.zeros(v.shape, F32)
        for msk, qh in _mem_heads(qm):
            p = _mem_softmax(qh, k)
            doh = jnp.where(msk, dmo, 0.0).astype(BF16)
            dv = dv + _bdot(p, doh, "tn")
            dp = _bdot(doh, v, "nt")
            ds = (p * (dp - jnp.sum(dp * p, axis=-1, keepdims=True))).astype(BF16)
            dq = jnp.where(msk, _bdot(ds, k, "nn") * ATT_SCALE, dq)
            dk = dk + _bdot(ds, qh, "tn") * ATT_SCALE
        return (jnp.concatenate([dtok, dq], axis=1),), (jnp.concatenate([dk, dv], axis=1),)

    tpe = seq // tile
    kv_spec = pl.BlockSpec((None,) + memkv.shape[1:], lambda i: (i // tpe, 0, 0))
    return _rowwise(fn, [(dmixin, MEM_WIDTH, tokw // MEM_WIDTH), dtok, (proj, MEM_WIDTH, tokw // MEM_WIDTH)],
                    [(memkv, kv_spec)], [(d, BF16)], [(memkv.shape, F32, kv_spec)],
                    rows=n, tile=tile, name=name, acc_period=tpe)


def _scale_bwd(dmixin, mixed, scale, *, name):
    n, tokw = mixed.shape

    def fn(dt, mixed, scale):
        return (dt * scale,), (jnp.sum(dt * mixed, axis=0, keepdims=True),)

    return _rowwise(fn, [(dmixin, tokw, 0), mixed], [scale], [(tokw, BF16)], [((1, tokw), F32)],
                    rows=n, tile=_pick(n, 512, SUBLANE), name=name)


def _chunk_rows(seq):
    return _pick(seq, 512, SUBLANE)


def _load_ext(ref, c, rows, before, after, seq):
    lo, hi = c * rows - before, (c + 1) * rows + after
    parts = []
    if lo < 0:
        parts.append(jnp.zeros((-lo, ref.shape[1]), F32))
    parts.append(ref[max(lo, 0):min(hi, seq), :])
    if hi > seq:
        parts.append(jnp.zeros((hi - seq, ref.shape[1]), F32))
    return parts[0] if len(parts) == 1 else jnp.concatenate(parts, axis=0)


def _down(x, k):
    return pltpu.roll(x, k, 0)


def _up(x, k):
    return pltpu.roll(x, x.shape[0] - k, 0)


def _window_sums(ext, shift, col0, group):
    lane = col0 + lax.broadcasted_iota(jnp.int32, (1, ext.shape[1]), 1)
    gidx = lane // group
    s = ext
    out = None
    k = 1
    for gi, w in enumerate(POOL_WINDOWS):
        while k < w:
            s = s + shift(s, k)
            k *= 2
        out = s if out is None else jnp.where(gidx >= gi, s, out)
    return out, jnp.left_shift(2, jnp.minimum(gidx, len(POOL_WINDOWS) - 1))


def _pool_fwd(proj3, tokw, *, name):
    nb, seq, _ = proj3.shape
    rows = _chunk_rows(seq)
    group = tokw // len(POOL_WINDOWS)

    def body(u_ref, o_ref):
        col0 = pl.program_id(1) * LANE
        for c in range(seq // rows):
            ext = _load_ext(u_ref, c, rows, MAX_WINDOW, 0, seq)
            sums, win = _window_sums(ext, _down, col0, group)
            t = c * rows + lax.broadcasted_iota(jnp.int32, (rows, 1), 0)
            count = jnp.minimum(t + 1, win).astype(F32)
            o_ref[c * rows:(c + 1) * rows, :] = (sums[MAX_WINDOW:, :] / count - ext[MAX_WINDOW:, :]).astype(BF16)

    spec = pl.BlockSpec((None, seq, LANE), lambda b, j: (b, 0, j))
    return pl.pallas_call(
        body, name=name, grid=(nb, tokw // LANE), in_specs=[spec], out_specs=spec,
        out_shape=jax.ShapeDtypeStruct((nb, seq, tokw), BF16), compiler_params=_params(),
    )(proj3)


def _pool_bwd(dp3, *, name):
    nb, seq, tokw = dp3.shape
    rows = _chunk_rows(seq)
    group = tokw // len(POOL_WINDOWS)

    def body(d_ref, o_ref):
        col0 = pl.program_id(1) * LANE
        for c in range(seq // rows):
            ext = _load_ext(d_ref, c, rows, 0, MAX_WINDOW, seq)
            lane = col0 + lax.broadcasted_iota(jnp.int32, (1, LANE), 1)
            win = jnp.left_shift(2, jnp.minimum(lane // group, len(POOL_WINDOWS) - 1))
            t = c * rows + lax.broadcasted_iota(jnp.int32, (rows + MAX_WINDOW, 1), 0)
            scaled = ext / jnp.minimum(t + 1, win).astype(F32)
            sums, _ = _window_sums(scaled, _up, col0, group)
            o_ref[c * rows:(c + 1) * rows, :] = sums[:rows, :] - ext[:rows, :]

    spec = pl.BlockSpec((None, seq, LANE), lambda b, j: (b, 0, j))
    return pl.pallas_call(
        body, name=name, grid=(nb, tokw // LANE), in_specs=[spec], out_specs=spec,
        out_shape=jax.ShapeDtypeStruct((nb, seq, tokw), F32), compiler_params=_params(),
    )(dp3)


def _conv3(ext, w_ref, b_ref):
    x1, x2 = _down(ext, 1), _down(ext, 2)
    return w_ref[0:1, :] * x2 + w_ref[1:2, :] * x1 + w_ref[2:3, :] * ext + b_ref[...], x1, x2


def _convgate_fwd(up3, cw, cb, *, name):
    nb, seq, c2 = up3.shape
    fp = c2 // 2
    nblk = fp // LANE
    rows = _chunk_rows(seq)

    def body(u_ref, g_ref, wu_ref, wg_ref, bu_ref, bg_ref, o_ref):
        for c in range(seq // rows):
            hu, _, _ = _conv3(_load_ext(u_ref, c, rows, SUBLANE, 0, seq), wu_ref, bu_ref)
            hg, _, _ = _conv3(_load_ext(g_ref, c, rows, SUBLANE, 0, seq), wg_ref, bg_ref)
            o_ref[c * rows:(c + 1) * rows, :] = (hg * jax.nn.sigmoid(hg) * hu)[SUBLANE:, :].astype(BF16)

    def col(off, r):
        return pl.BlockSpec((r, LANE), lambda b, j: (0, j + off))

    def act(off):
        return pl.BlockSpec((None, seq, LANE), lambda b, j: (b, 0, j + off))

    return pl.pallas_call(
        body, name=name, grid=(nb, nblk),
        in_specs=[act(0), act(nblk), col(0, SUBLANE), col(nblk, SUBLANE), col(0, 1), col(nblk, 1)],
        out_specs=act(0), out_shape=jax.ShapeDtypeStruct((nb, seq, fp), BF16), compiler_params=_params(),
    )(up3, up3, cw, cw, cb, cb)


def _convgate_bwd(up3, dact3, cw, cb, *, name):
    nb, seq, c2 = up3.shape
    fp = c2 // 2
    nblk = fp // LANE
    rows = _chunk_rows(seq)
    h = SUBLANE

    def body(u_ref, g_ref, da_ref, wu_ref, wg_ref, bu_ref, bg_ref, du_ref, dg_ref, dwu_ref, dwg_ref, dbu_ref,
             dbg_ref):
        @pl.when(pl.program_id(1) == 0)
        def _():
            for r in (dwu_ref, dwg_ref, dbu_ref, dbg_ref):
                r[...] = jnp.zeros(r.shape, F32)

        for c in range(seq // rows):
            eu = _load_ext(u_ref, c, rows, h, h, seq)
            eg = _load_ext(g_ref, c, rows, h, h, seq)
            da = _load_ext(da_ref, c, rows, h, h, seq)
            hu, u1, u2 = _conv3(eu, wu_ref, bu_ref)
            hg, g1, g2 = _conv3(eg, wg_ref, bg_ref)
            sig = jax.nn.sigmoid(hg)
            dhu = da * hg * sig
            dhg = da * hu * sig * (1.0 + hg * (1.0 - sig))
            for dh, w_ref, x0, x1, x2, dx_ref, dw_ref, db_ref in (
                    (dhu, wu_ref, eu, u1, u2, du_ref, dwu_ref, dbu_ref),
                    (dhg, wg_ref, eg, g1, g2, dg_ref, dwg_ref, dbg_ref)):
                dx = w_ref[2:3, :] * dh + w_ref[1:2, :] * _up(dh, 1) + w_ref[0:1, :] * _up(dh, 2)
                dx_ref[c * rows:(c + 1) * rows, :] = dx[h:h + rows, :].astype(BF16)
                core = dh[h:h + rows, :]
                for k, xk in ((0, x2), (1, x1), (2, x0)):
                    dw_ref[k:k + 1, :] += jnp.sum(core * xk[h:h + rows, :], axis=0, keepdims=True)
                db_ref[...] += jnp.sum(core, axis=0, keepdims=True)

    def col(off, r):
        return pl.BlockSpec((r, LANE), lambda j, b: (0, j + off))

    def act(off):
        return pl.BlockSpec((None, seq, LANE), lambda j, b: (b, 0, j + off))

    du, dg, dwu, dwg, dbu, dbg = pl.pallas_call(
        body, name=name, grid=(nblk, nb),
        in_specs=[act(0), act(nblk), act(0), col(0, SUBLANE), col(nblk, SUBLANE), col(0, 1), col(nblk, 1)],
        out_specs=[act(0), act(0), col(0, SUBLANE), col(0, SUBLANE), col(0, 1), col(0, 1)],
        out_shape=[jax.ShapeDtypeStruct((nb, seq, fp), BF16), jax.ShapeDtypeStruct((nb, seq, fp), BF16),
                   jax.ShapeDtypeStruct((SUBLANE, fp), F32), jax.ShapeDtypeStruct((SUBLANE, fp), F32),
                   jax.ShapeDtypeStruct((1, fp), F32), jax.ShapeDtypeStruct((1, fp), F32)],
        compiler_params=_params(),
    )(up3, up3, dact3, cw, cw, cb, cb)
    return du, dg, jnp.concatenate([dwu, dwg], axis=1), jnp.concatenate([dbu, dbg], axis=1)


def _scan_rows(x, shift, valid):
    row = lax.broadcasted_iota(jnp.int32, (x.shape[0], 1), 0)
    k = 1
    while k < x.shape[0]:
        x = x + jnp.where(valid(row, k), shift(x, k), 0.0)
        k *= 2
    return x


def _pick_row(x, r):
    row = lax.broadcasted_iota(jnp.int32, (x.shape[0], 1), 0)
    return jnp.sum(jnp.where(row == r, x, 0.0), axis=0, keepdims=True)


def _log_sigmoid(z):
    return jnp.minimum(z, 0.0) - jnp.log(1.0 + jnp.exp(-jnp.abs(z)))


def _gate_fwd(kvf3, fb, col_block, *, name):
    nb, seq, _ = kvf3.shape
    rows = _chunk_rows(seq)

    def body(f_ref, fb_ref, o_ref):
        carry = jnp.zeros((1, LANE), F32)
        for c in range(seq // rows):
            logf = _log_sigmoid(f_ref[c * rows:(c + 1) * rows, :] + fb_ref[...])
            run = _scan_rows(logf, _down, lambda row, k: row >= k) + carry
            o_ref[c * rows:(c + 1) * rows, :] = run
            carry = _pick_row(run, rows - 1)

    return pl.pallas_call(
        body, name=name, grid=(nb,),
        in_specs=[pl.BlockSpec((None, seq, LANE), lambda b: (b, 0, col_block)),
                  pl.BlockSpec((1, LANE), lambda b: (0, 0))],
        out_specs=pl.BlockSpec((None, seq, LANE), lambda b: (b, 0, 0)),
        out_shape=jax.ShapeDtypeStruct((nb, seq, LANE), F32), compiler_params=_params(),
    )(kvf3, fb)


def _gate_bwd(kvf3, fb, dF3, col_block, heads, *, name):
    nb, seq, _ = kvf3.shape
    rows = _chunk_rows(seq)

    def body(f_ref, fb_ref, d_ref, o_ref, dfb_ref):
        @pl.when(pl.program_id(0) == 0)
        def _():
            dfb_ref[...] = jnp.zeros(dfb_ref.shape, F32)

        lane = lax.broadcasted_iota(jnp.int32, (1, LANE), 1)
        carry = jnp.zeros((1, LANE), F32)
        for c in reversed(range(seq // rows)):
            run = _scan_rows(d_ref[c * rows:(c + 1) * rows, :], _up, lambda row, k: row < rows - k) + carry
            carry = _pick_row(run, 0)
            z = f_ref[c * rows:(c + 1) * rows, :] + fb_ref[...]
            df = jnp.where(lane < heads, run * jax.nn.sigmoid(-z), 0.0)
            o_ref[c * rows:(c + 1) * rows, :] = df
            dfb_ref[...] += jnp.sum(df, axis=0, keepdims=True)

    return pl.pallas_call(
        body, name=name, grid=(nb,),
        in_specs=[pl.BlockSpec((None, seq, LANE), lambda b: (b, 0, col_block)),
                  pl.BlockSpec((1, LANE), lambda b: (0, 0)),
                  pl.BlockSpec((None, seq, LANE), lambda b: (b, 0, 0))],
        out_specs=[pl.BlockSpec((None, seq, LANE), lambda b: (b, 0, 0)), pl.BlockSpec((1, LANE), lambda b: (0, 0))],
        out_shape=[jax.ShapeDtypeStruct((nb, seq, LANE), F32), jax.ShapeDtypeStruct((1, LANE), F32)],
        compiler_params=_params(),
    )(kvf3, fb, dF3)


def _head_masks():
    lane = lax.broadcasted_iota(jnp.int32, (1, LANE), 1)
    return (lane < HEAD_DIM, lane >= HEAD_DIM)


BIAS_TERMS = 3


def _bias_lanes(gsum):
    nb, seq, heads = gsum.shape
    terms, rest = [], gsum
    for _ in range(BIAS_TERMS):
        t = lax.reduce_precision(rest, exponent_bits=8, mantissa_bits=7)
        terms.append(t)
        rest = rest - t
    ones = [jnp.ones_like(gsum)] * BIAS_TERMS

    def lanes(parts):
        z = jnp.stack(parts, axis=-1)
        z = jnp.pad(z, ((0, 0), (0, 0), (0, 0), (0, HEAD_DIM - 2 * BIAS_TERMS)))
        z = z.reshape(nb, seq, heads // 2, 2, HEAD_DIM)[:, :, :, ::-1]
        return z.reshape(nb, seq, heads * HEAD_DIM).astype(BF16)

    return lanes(terms + ones), lanes(ones + [-t for t in terms])


def _fox_scores(q, k, aq, ak, masked):
    qs = (q * ATT_SCALE).astype(BF16)
    qts = [jnp.where(msk, qs, aq) for msk in _head_masks()]
    ss = [_bdot(qt, jnp.where(msk, k, ak), "nt") for qt, msk in zip(qts, _head_masks())]
    if masked:
        t = q.shape[0]
        keep = lax.broadcasted_iota(jnp.int32, (t, t), 0) >= lax.broadcasted_iota(jnp.int32, (t, t), 1)
        ss = [jnp.where(keep, s, NEG_BIG) for s in ss]
    return ss, qts


def _on_blocks(qi, ki, step):
    @pl.when(ki < qi)
    def _():
        step(False)

    @pl.when(ki == qi)
    def _():
        step(True)


def _fox_grid(nblk, tokw, t, q_major):
    if q_major:
        pairs = [(qi, ki) for qi in range(nblk) for ki in range(qi + 1)]
    else:
        pairs = [(qi, ki) for ki in range(nblk) for qi in range(ki, nblk)]
    tables = [jnp.array([p[i] for p in pairs], jnp.int32) for i in (0, 1)]

    def q_spec(off=0, wide=False):
        width = 2 * LANE if wide else LANE
        return pl.BlockSpec((None, t, width), lambda b, p, i, qt, kt: (b, qt[i], p + off))

    def k_spec(off=0):
        return pl.BlockSpec((None, t, LANE), lambda b, p, i, qt, kt: (b, kt[i], p + off))

    return tables, len(pairs), q_spec, k_spec, tokw // LANE


def _lanes(col):
    return jnp.broadcast_to(col, (col.shape[0], LANE))


def _across(stat, width):
    return jnp.tile(stat, (1, width // LANE))


def _fox_fwd(proj3, kvf3, aq3, ak3, tokw, *, name):
    nb, seq, _ = proj3.shape
    t = _pick(seq, 512, LANE)
    tables, n_pairs, q_spec, k_spec, hp0 = _fox_grid(seq // t, tokw, t, True)

    def body(qt_ref, kt_ref, q_ref, k_ref, v_ref, aq_ref, ak_ref, o_ref, lse_ref, m_s, l_s, acc_s):
        i = pl.program_id(2)
        qi, ki = qt_ref[i], kt_ref[i]

        @pl.when(ki == 0)
        def _():
            m_s[...] = jnp.full(m_s.shape, NEG_BIG, F32)
            l_s[...] = jnp.zeros(l_s.shape, F32)
            acc_s[...] = jnp.zeros(acc_s.shape, F32)

        def step(masked):
            v = v_ref[...].astype(BF16)
            ss, _ = _fox_scores(q_ref[...], k_ref[...].astype(BF16), aq_ref[...], ak_ref[...], masked)
            for h, s in enumerate(ss):
                m_old = m_s[h]
                m_new = jnp.maximum(m_old, _lanes(jnp.max(s, axis=-1, keepdims=True)))
                alpha = jnp.exp(m_old - m_new)
                p = jnp.exp(s - _across(m_new, t))
                l_s[h] = alpha * l_s[h] + _lanes(jnp.sum(p, axis=-1, keepdims=True))
                acc_s[h] = alpha * acc_s[h] + _bdot(p, v, "nn")
                m_s[h] = m_new

        _on_blocks(qi, ki, step)

        @pl.when(ki == qi)
        def _():
            o_ref[...] = jnp.where(_head_masks()[0], acc_s[0] / l_s[0], acc_s[1] / l_s[1])
            lse_ref[...] = jnp.concatenate([m_s[0] + jnp.log(l_s[0]), m_s[1] + jnp.log(l_s[1])], axis=1)

    stat = pltpu.VMEM((2, t, LANE), F32)
    return pl.pallas_call(
        body, name=name,
        grid_spec=pltpu.PrefetchScalarGridSpec(
            num_scalar_prefetch=2, grid=(nb, hp0, n_pairs),
            in_specs=[q_spec(), k_spec(), k_spec(hp0), q_spec(), k_spec()],
            out_specs=[q_spec(), q_spec(wide=True)], scratch_shapes=[stat, stat, stat]),
        out_shape=[jax.ShapeDtypeStruct((nb, seq, tokw), F32), jax.ShapeDtypeStruct((nb, seq, 2 * tokw), F32)],
        compiler_params=_params(),
    )(*tables, proj3, kvf3, kvf3, aq3, ak3)


def _fox_bwd_common(q_ref, k_ref, v_ref, aq_ref, ak_ref, do_ref, lse_ref, delta_ref, masked):
    k, v = k_ref[...].astype(BF16), v_ref[...].astype(BF16)
    ss, qts = _fox_scores(q_ref[...], k, aq_ref[...], ak_ref[...], masked)
    do = do_ref[...]
    t = do.shape[0]
    out = []
    for h, (s, qt, msk) in enumerate(zip(ss, qts, _head_masks())):
        doh = jnp.where(msk, do, 0.0).astype(BF16)
        p = jnp.exp(s - _across(lse_ref[:, h * LANE:(h + 1) * LANE], t))
        ds = p * (_bdot(doh, v, "nt") - _across(delta_ref[:, h * LANE:(h + 1) * LANE], t))
        out.append((qt, doh, p, ds))
    return out, k


def _fox_bwd(proj3, kvf3, aq3, ak3, o3, dmixin3, lse3, tokw, *, name):
    nb, seq, _ = proj3.shape
    t = _pick(seq, 512, LANE)
    nblk = seq // t
    tables, n_pairs, q_spec, k_spec, hp0 = _fox_grid(nblk, tokw, t, True)
    whole = pl.BlockSpec((None, seq, LANE), lambda b, p, i, qt, kt: (b, 0, p))
    dfk_spec = pl.BlockSpec((None, None, nblk, SUBLANE, t), lambda b, p, i, qt, kt: (b, p, 0, 0, 0))

    def body(qt_ref, kt_ref, q_ref, k_ref, v_ref, aq_ref, ak_ref, o_ref, do_ref, lse_ref, dq_ref, dk_ref, dv_ref,
             dfk_ref, acc_s, row_s, delta_s):
        i = pl.program_id(2)
        qi, ki = qt_ref[i], kt_ref[i]

        @pl.when(i == 0)
        def _():
            dk_ref[...] = jnp.zeros(dk_ref.shape, F32)
            dv_ref[...] = jnp.zeros(dv_ref.shape, F32)
            dfk_ref[...] = jnp.zeros(dfk_ref.shape, F32)

        @pl.when(ki == 0)
        def _():
            acc_s[...] = jnp.zeros(acc_s.shape, F32)
            row_s[...] = jnp.zeros(row_s.shape, F32)
            prod = do_ref[...] * o_ref[...]
            delta_s[...] = jnp.concatenate(
                [_lanes(jnp.sum(jnp.where(msk, prod, 0.0), axis=-1, keepdims=True)) for msk in _head_masks()],
                axis=1)

        def step(masked):
            heads, k = _fox_bwd_common(q_ref, k_ref, v_ref, aq_ref, ak_ref, do_ref, lse_ref, delta_s, masked)
            rows = pl.ds(pl.multiple_of(ki * t, t), t)
            for h, ((qt, doh, p, ds), msk) in enumerate(zip(heads, _head_masks())):
                acc_s[h] += _bdot(ds, k, "nn")
                row_s[h] += _lanes(jnp.sum(ds, axis=-1, keepdims=True))
                dv_ref[rows, :] += _bdot(p, doh, "tn")
                dk_ref[rows, :] += jnp.where(msk, _bdot(ds, qt, "tn"), 0.0)
                dfk_ref[ki, h:h + 1, :] -= jnp.sum(ds, axis=0, keepdims=True)

        _on_blocks(qi, ki, step)

        @pl.when(ki == qi)
        def _():
            dq_ref[...] = jnp.where(_head_masks()[0], acc_s[0], acc_s[1]) * ATT_SCALE
            for h in range(2):
                dfk_ref[qi, 2 + h:3 + h, :] = row_s[h].T[0:1, :]

    out = jax.ShapeDtypeStruct((nb, seq, tokw), F32)
    stat = pltpu.VMEM((2, t, LANE), F32)
    return pl.pallas_call(
        body, name=name,
        grid_spec=pltpu.PrefetchScalarGridSpec(
            num_scalar_prefetch=2, grid=(nb, hp0, n_pairs),
            in_specs=[q_spec(), k_spec(), k_spec(hp0), q_spec(), k_spec(), q_spec(), q_spec(), q_spec(wide=True)],
            out_specs=[q_spec(), whole, whole, dfk_spec],
            scratch_shapes=[stat, stat, pltpu.VMEM((t, 2 * LANE), F32)]),
        out_shape=[out, out, out, jax.ShapeDtypeStruct((nb, hp0, nblk, SUBLANE, t), F32)],
        compiler_params=_params(),
    )(*tables, proj3, kvf3, kvf3, aq3, ak3, o3, dmixin3, lse3)


def _peer(k):
    x, y, c = lax.axis_index("x"), lax.axis_index("y"), lax.axis_index("c")
    return (1 - x if k & 4 else x, 1 - y if k & 2 else y, 1 - c if k & 1 else c)


def _dev_index(p):
    return 4 * p[0] + 2 * p[1] + p[2]


_HBM = pl.BlockSpec(memory_space=pltpu.HBM)
CHIP_RELATIONS = (2, 4, 6)


def _chip_index(p):
    return 2 * p[0] + p[1]


def _run_copies(sends, recvs):
    for cp in sends:
        cp.start()
    for cp in recvs:
        cp.wait_recv()
    for cp in sends:
        cp.wait_send()


def _gather_shards(whole, halved, side_by_side, *, name):
    nw, nh = len(whole), len(halved) + len(side_by_side)
    n = nw + nh
    n_sem = 3 * nw + 6 * nh

    def body(*refs):
        ins, outs, send_sems, recv_sems, local_sems = refs[:n], refs[n:2 * n], refs[2 * n], refs[2 * n + 1], refs[-1]
        me, sib = _peer(0), _peer(1)
        q, c = _chip_index(me), me[2]
        own = []

        def copy(src, dst, s, to):
            return pltpu.make_async_remote_copy(src_ref=src, dst_ref=dst, send_sem=send_sems.at[s],
                                                recv_sem=recv_sems.at[s], device_id=to, device_id_type=MESH_T)

        sends, recvs, passes = [], [], []
        for j, k in enumerate(CHIP_RELATIONS):
            peer = _peer(k)
            pq = _chip_index(peer)
            for i in range(nw):
                sends.append(copy(ins[i], outs[i].at[q], 3 * i + j, peer))
                recvs.append(copy(ins[i], outs[i].at[pq], 3 * i + j, peer))
            for i in range(nh):
                src, out, s = ins[nw + i], outs[nw + i], 3 * nw + 6 * i + j
                if i < len(halved):
                    place = lambda chip, half, out=out: out.at[chip, half]
                else:
                    cols = src.shape[-1]
                    place = lambda chip, half, out=out, cols=cols: out.at[
                        half, :, pl.ds(pl.multiple_of(chip * cols, LANE), cols)]
                    if j == 0:
                        own += [pltpu.make_async_copy(src.at[h], place(q, h), local_sems.at[len(own) + h])
                                for h in range(2)]
                sends.append(copy(src.at[c], place(q, c), s, peer))
                passes.append((copy(src.at[c], place(pq, c), s, peer), copy(place(pq, c), place(pq, c), s + 3, sib),
                               copy(place(pq, c), place(pq, 1 - c), s + 3, sib)))
        for cp in sends + own:
            cp.start()
        for arrival, hand_over, _ in passes:
            arrival.wait_recv()
            hand_over.start()
        for cp in recvs:
            cp.wait_recv()
        for _, _, from_sibling in passes:
            from_sibling.wait_recv()
        for cp in sends + [hand_over for _, hand_over, _ in passes]:
            cp.wait_send()
        for cp in own:
            cp.wait()

    arrays = list(whole) + list(halved) + list(side_by_side)
    return pl.pallas_call(
        body, name=name, in_specs=[_HBM] * n, out_specs=[_HBM] * n,
        out_shape=[jax.ShapeDtypeStruct((N_CHIP,) + a.shape, a.dtype) for a in list(whole) + list(halved)]
        + [jax.ShapeDtypeStruct(a.shape[:-1] + (N_CHIP * a.shape[-1],), a.dtype) for a in side_by_side],
        scratch_shapes=[pltpu.SemaphoreType.DMA((n_sem,)), pltpu.SemaphoreType.DMA((n_sem,)),
                        pltpu.SemaphoreType.DMA((2 * len(side_by_side),))],
    )(*arrays)


def _to_sibling(grads, *, name):
    n = len(grads)

    def body(*refs):
        ins, outs, send_sems, recv_sems = refs[:n], refs[n:2 * n], refs[2 * n], refs[2 * n + 1]
        c = lax.axis_index("c")
        sends = [pltpu.make_async_remote_copy(src_ref=ins[i].at[:, 1 - c], dst_ref=outs[i], send_sem=send_sems.at[i],
                                              recv_sem=recv_sems.at[i], device_id=_peer(1), device_id_type=MESH_T)
                 for i in range(n)]
        _run_copies(sends, sends)

    return pl.pallas_call(
        body, name=name, in_specs=[_HBM] * n, out_specs=[_HBM] * n,
        out_shape=[jax.ShapeDtypeStruct(g.shape[:1] + g.shape[2:], g.dtype) for g in grads],
        scratch_shapes=[pltpu.SemaphoreType.DMA((n,)), pltpu.SemaphoreType.DMA((n,))],
    )(*grads)


def _pair_add(grads, from_sibling, qc, *, name):
    _, _, rows, cols = grads.shape
    tile = _pick(rows, 1024, SUBLANE)

    def body(qc_ref, g_ref, s_ref, o_ref):
        del qc_ref
        o_ref[...] = g_ref[...] + s_ref[...]

    spec = pl.BlockSpec((None, tile, cols), lambda j, i, qc: (j, i, 0))
    return pl.pallas_call(
        body, name=name,
        grid_spec=pltpu.PrefetchScalarGridSpec(
            num_scalar_prefetch=1, grid=(N_CHIP, rows // tile),
            in_specs=[pl.BlockSpec((None, None, tile, cols), lambda j, i, qc: (j, qc[1], i, 0)), spec],
            out_specs=spec),
        out_shape=jax.ShapeDtypeStruct((N_CHIP, rows, cols), F32), compiler_params=_params(),
    )(qc, grads, from_sibling)


def _to_chips(sums, small, *, name):
    n = len(sums)

    def body(*refs):
        ins, small_ref, outs, small_out = refs[:n], refs[n], refs[n + 1:2 * n + 1], refs[2 * n + 1]
        send_sems, recv_sems, local_sem = refs[2 * n + 2:]
        me = _peer(0)

        def copy(src, dst, s, to):
            return pltpu.make_async_remote_copy(src_ref=src, dst_ref=dst, send_sem=send_sems.at[s],
                                                recv_sem=recv_sems.at[s], device_id=to, device_id_type=MESH_T)

        mine = pltpu.make_async_copy(small_ref.at[_dev_index(me)], small_out.at[_dev_index(me)], local_sem)
        mine.start()
        sends, recvs = [], []
        for j, k in enumerate(CHIP_RELATIONS):
            peer = _peer(k)
            for i in range(n):
                src = ins[i].at[_chip_index(peer)]
                sends.append(copy(src, outs[i].at[j], 3 * i + j, peer))
                recvs.append(copy(src, outs[i].at[j], 3 * i + j, peer))
        for k in range(1, N_DEV):
            peer = _peer(k)
            src = small_ref.at[_dev_index(peer)]
            sends.append(copy(src, small_out.at[_dev_index(me)], 3 * n + k - 1, peer))
            recvs.append(copy(src, small_out.at[_dev_index(peer)], 3 * n + k - 1, peer))
        _run_copies(sends, recvs)
        mine.wait()

    n_sem = 3 * n + N_DEV - 1
    return pl.pallas_call(
        body, name=name, in_specs=[_HBM] * (n + 1), out_specs=[_HBM] * (n + 1),
        out_shape=[jax.ShapeDtypeStruct((3,) + g.shape[1:], g.dtype) for g in sums]
        + [jax.ShapeDtypeStruct(small.shape, small.dtype)],
        scratch_shapes=[pltpu.SemaphoreType.DMA((n_sem,)), pltpu.SemaphoreType.DMA((n_sem,)),
                        pltpu.SemaphoreType.DMA],
    )(*sums, small)


def _swap_halves(arrays, *, name):
    n = len(arrays)

    def body(*refs):
        outs, send_sems, recv_sems = refs[n:2 * n], refs[2 * n], refs[2 * n + 1]
        c = lax.axis_index("c")
        sib = _peer(1)
        sends, recvs = [], []
        for i in range(n):
            sem = dict(send_sem=send_sems.at[i], recv_sem=recv_sems.at[i], device_id=sib, device_id_type=MESH_T)
            sends.append(pltpu.make_async_remote_copy(src_ref=outs[i].at[c], dst_ref=outs[i].at[c], **sem))
            recvs.append(pltpu.make_async_remote_copy(src_ref=outs[i].at[c], dst_ref=outs[i].at[1 - c], **sem))
        _run_copies(sends, recvs)

    return pl.pallas_call(
        body, name=name, in_specs=[_HBM] * n, out_specs=[_HBM] * n,
        out_shape=[jax.ShapeDtypeStruct(a.shape, a.dtype) for a in arrays],
        input_output_aliases={i: i for i in range(n)},
        scratch_shapes=[pltpu.SemaphoreType.DMA((n,)), pltpu.SemaphoreType.DMA((n,))],
    )(*arrays)


def _adam_math(g, w, m, v):
    bc1 = 1.0 - ADAM_B1 ** ADAM_STEP
    bc2 = 1.0 - ADAM_B2 ** ADAM_STEP
    m_new = ADAM_B1 * m + (1.0 - ADAM_B1) * g
    v_new = ADAM_B2 * v + (1.0 - ADAM_B2) * (g * g)
    delta = -ADAM_LR * ((m_new / bc1) / (jnp.sqrt(v_new / bc2) + ADAM_EPS) + ADAM_WD * w)
    return delta, m_new, v_new


def _reduce_half(sums, parts, qc, *, name):
    _, rows, cols = sums.shape
    tile = _pick(rows, 1024, SUBLANE)
    n_parts = parts.shape[0]

    def body(qc_ref, g_ref, p_ref, o_ref):
        del qc_ref
        g = g_ref[...]
        for k in range(n_parts):
            g = g + p_ref[k]
        o_ref[...] = g

    return pl.pallas_call(
        body, name=name,
        grid_spec=pltpu.PrefetchScalarGridSpec(
            num_scalar_prefetch=1, grid=(rows // tile,),
            in_specs=[pl.BlockSpec((None, tile, cols), lambda i, qc: (qc[0], i, 0)),
                      pl.BlockSpec((n_parts, tile, cols), lambda i, qc: (0, i, 0))],
            out_specs=pl.BlockSpec((None, tile, cols), lambda i, qc: (qc[1], i, 0))),
        out_shape=jax.ShapeDtypeStruct((2, rows, cols), F32), compiler_params=_params(),
    )(qc, sums, parts)


def _adamw_shard(g, w, m, v, *, name):
    _, rows, cols = g.shape
    tile = _pick(rows, 512, SUBLANE)

    def body(g_ref, w_ref, m_ref, v_ref, do_ref, mo_ref, vo_ref):
        do_ref[...], mo_ref[...], vo_ref[...] = _adam_math(g_ref[...], w_ref[...], m_ref[...], v_ref[...])

    spec = pl.BlockSpec((None, tile, cols), lambda h, i: (h, i, 0))
    return pl.pallas_call(
        body, name=name, grid=(2, rows // tile), in_specs=[spec] * 4, out_specs=[spec] * 3,
        out_shape=[jax.ShapeDtypeStruct(g.shape, F32)] * 3, compiler_params=_params(),
    )(g, w, m, v)


def _adamw(parts, w, m, v, *, name):
    _, rows, cols = parts.shape
    tile = _pick(rows, 256, SUBLANE)

    def body(p_ref, w_ref, m_ref, v_ref, o_ref):
        g = p_ref[0]
        for i in range(1, N_DEV):
            g = g + p_ref[i]
        delta, m_new, v_new = _adam_math(g, w_ref[...], m_ref[...], v_ref[...])
        o_ref[0] = g
        o_ref[1] = delta
        o_ref[2] = m_new
        o_ref[3] = v_new

    spec = pl.BlockSpec((tile, cols), lambda i: (i, 0))
    return pl.pallas_call(
        body, name=name, grid=(rows // tile,),
        in_specs=[pl.BlockSpec((N_DEV, tile, cols), lambda i: (0, i, 0)), spec, spec, spec],
        out_specs=pl.BlockSpec((4, tile, cols), lambda i: (0, i, 0)),
        out_shape=jax.ShapeDtypeStruct((4, rows, cols), F32), compiler_params=_params(),
    )(parts, w, m, v)


def _layout(shapes, names, align):
    out, off = [], 0
    for n in names:
        size = math.prod(shapes[n])
        out.append((n, tuple(shapes[n]), off, size))
        off += _round_up(size, align)
    return out, off


def _pack(arrays, layout, total, lead=()):
    parts = []
    for i, (n, _, off, size) in enumerate(layout):
        end = layout[i + 1][2] if i + 1 < len(layout) else total
        flat = arrays[n].reshape(lead + (size,))
        if end - off > size:
            flat = jnp.pad(flat, [(0, 0)] * len(lead) + [(0, end - off - size)])
        parts.append(flat)
    return jnp.concatenate(parts, axis=len(lead))


def _unpack(flat, layout, lead=()):
    return {n: flat[..., off:off + size].reshape(lead + shape) for n, shape, off, size in layout}


def _to_shards(full, axis):
    shp = full.shape
    return jnp.moveaxis(full.reshape(shp[:axis] + (N_CHIP, shp[axis] // N_CHIP) + shp[axis + 1:]), axis, 0)


def _from_shards(shards, axis):
    x = jnp.moveaxis(shards, 0, axis)
    shp = x.shape
    return x.reshape(shp[:axis] + (shp[axis] * shp[axis + 1],) + shp[axis + 2:])


def _pad_cols(w, per, padded):
    lead = w.shape[:-1]
    x = w.reshape(lead + (-1, per))
    x = jnp.pad(x, [(0, 0)] * len(lead) + [(0, 0), (0, padded - per)])
    return x.reshape(lead + (-1,))


def _unpad_cols(w, per, padded):
    lead = w.shape[:-1]
    return w.reshape(lead + (-1, padded))[..., :per].reshape(lead + (-1,))


def _local_step(x, mem, target, W):
    nb, seq, d = x.shape
    n = nb * seq
    tokw = d - MEM_WIDTH
    heads = tokw // HEAD_DIM
    mlen = mem.shape[1]
    per = W["ffn_w_down"].shape[1] // 2
    per_p = _round_up(per, LANE)
    fp = 2 * per_p
    kvw = 2 * tokw + heads
    kvp = 2 * tokw + LANE
    gate_block = 2 * tokw // LANE

    x2d = x.reshape(n, d)
    mem2d = mem.reshape(nb * mlen, d)
    t2d = target.reshape(n, d)
    row = lambda a: a.reshape(1, -1)
    ones_tok = jnp.ones((1, tokw), F32)

    pool_bd = jax.scipy.linalg.block_diag(*[W["a_pool_w"][0, i] for i in range(len(POOL_WINDOWS))]).astype(BF16)
    kv_w = jnp.pad(W["kv_w"], ((0, 0), (0, kvp - kvw)))
    fb = jnp.pad(W["f_b"], (0, LANE - heads)).reshape(1, LANE)
    w_up = W.get("ffn_w_up_padded")
    if w_up is None:
        w_up = jnp.stack([_pad_cols(W["ffn_w_up"][l], per, per_p) for l in range(DEPTH)])
    w_down = [jnp.pad(W["ffn_w_down"][l].reshape(2, per, d), ((0, 0), (0, per_p - per), (0, 0))).reshape(fp, d)
              for l in range(DEPTH)]
    conv_w = [jnp.pad(_pad_cols(W["ffn_conv_w"][l], per, per_p), ((0, SUBLANE - CONV_WIDTH), (0, 0)))
              for l in range(DEPTH)]
    conv_b = [_pad_cols(W["ffn_conv_b"][l], per, per_p).reshape(1, 2 * fp) for l in range(DEPTH)]
    w_in = [W["a_w_in"][0], W["b_w_q"][0]]
    w_out = [W["a_w_out"][0], W["b_w_out"][0]]

    saved = []
    cur = cur_mm = x2d
    for l in range(DEPTH):
        s = {"x_in": cur, "x_in_mm": cur_mm}
        memkv = _mm(mem2d, W["mem_w_kv"][l], "nn", name=f"memkv{l}").reshape(nb, mlen, 2 * MEM_WIDTH)
        if l == 0:
            proj = _mm(cur_mm, w_in[l], "nn", name="proj0")
            pooled = _pool_fwd(proj.reshape(nb, seq, d), tokw, name="pool_fwd").reshape(n, tokw)
            tok = _mm(pooled, pool_bd, "nn", name="pool_mix")
            scale = W["a_pool_scale"].reshape(1, tokw)
            s.update(pooled=pooled, mixed=tok, scale=scale)
        else:
            kvf = _mm(cur_mm, kv_w, "nn", tn=kvp, name="kvf")
            kvf3 = kvf.reshape(nb, seq, kvp)
            gsum = _gate_fwd(kvf3, fb, gate_block, name="gate_fwd")[:, :, :heads]
            aq3, ak3 = _bias_lanes(gsum)
            proj = _mm(cur_mm, w_in[l], "nn", name="proj1")
            o3, lse3 = _fox_fwd(proj.reshape(nb, seq, d), kvf3, aq3, ak3, tokw, name="fox_fwd")
            tok = o3.reshape(n, tokw)
            scale = ones_tok
            s.update(kvf3=kvf3, aq3=aq3, ak3=ak3, o3=o3, lse3=lse3)
        mixin = _memattn_fwd(tok, proj, memkv, scale, seq=seq, name=f"memattn_fwd{l}")
        mix, x1, x1_mm = _mm(mixin, w_out[l], "nn", ln=(cur, row(W["ln1_g"][l]), row(W["ln1_b"][l])), name=f"mix{l}")
        up = _mm(x1_mm, w_up, "nn", b_lead=l, tm=1024, tn=per_p, col_major=True, name=f"ffn_up{l}")
        act = _convgate_fwd(up.reshape(nb, seq, 2 * fp), conv_w[l], conv_b[l], name=f"convgate_fwd{l}")
        act = act.reshape(n, fp)
        if l + 1 < DEPTH:
            ffn, cur, cur_mm = _mm(act, w_down[l], "nn", tm=512, tk=fp, ln=(x1, row(W["ln2_g"][l]), row(W["ln2_b"][l])),
                                   name=f"ffn_down{l}")
        else:
            ffn = _mm(act, w_down[l], "nn", tm=512, tk=fp, name=f"ffn_down{l}")
        s.update(proj=proj, memkv=memkv, mixin=mixin, mix=mix, x1=x1, x1_mm=x1_mm, up=up, act=act, ffn=ffn)
        saved.append(s)

    G = {}
    ln_g = {k: [None] * DEPTH for k in ("ln1_g", "ln1_b", "ln2_g", "ln2_b")}
    stack = {k: [None] * DEPTH for k in ("mem_w_kv", "ffn_w_up", "ffn_conv_w", "ffn_conv_b", "ffn_w_down")}
    dx_terms = None
    loss = None
    for l in reversed(range(DEPTH)):
        s = saved[l]
        g2 = row(W["ln2_g"][l])
        if l == DEPTH - 1:
            dres, dffn, loss, dg, db = _final_ln_loss(s["x1"], s["ffn"], t2d, g2, row(W["ln2_b"][l]),
                                                      name="final_ln_loss")
        else:
            dres, dffn, dg, db = _ln_bwd(s["x1"], s["ffn"], g2, dx_terms, name=f"ln2_bwd{l}")
        ln_g["ln2_g"][l], ln_g["ln2_b"][l] = dg[0], db[0]
        dact = _mm(dffn, w_down[l], "nt", tn=fp, name=f"ffn_down_dx{l}")
        stack["ffn_w_down"][l] = _mm(s["act"], dffn, "tn", tm=per_p, tk=DW_ROWS, trim=("rows", per),
                                     name=f"ffn_down_dw{l}")
        du3, dg3, dcw, dcb = _convgate_bwd(s["up"].reshape(nb, seq, 2 * fp), dact.reshape(nb, seq, fp), conv_w[l],
                                           conv_b[l], name=f"convgate_bwd{l}")
        du, dgt = du3.reshape(n, fp), dg3.reshape(n, fp)
        dx1 = _mm(du, w_up, "nt", b_lead=l, tm=512, tk=fp, pair=(dgt, fp), adds=[dres], name=f"ffn_up_dx{l}")
        stack["ffn_w_up"][l] = [_mm(s["x1_mm"], part, "tn", tm=d, tn=per_p, tk=DW_ROWS, trim=("cols", per),
                                    name=f"ffn_up_dw_{nm}{l}") for nm, part in (("u", du), ("g", dgt))]
        stack["ffn_conv_w"][l] = dcw[:CONV_WIDTH]
        stack["ffn_conv_b"][l] = dcb[0]
        dres1, dmix, dg, db = _ln_bwd(s["x_in"], s["mix"], row(W["ln1_g"][l]), [dx1], name=f"ln1_bwd{l}")
        ln_g["ln1_g"][l], ln_g["ln1_b"][l] = dg[0], db[0]
        dmixin = _mm(dmix, w_out[l], "nt", name=f"mix_dx{l}")
        d_w_out = _mm(s["mixin"], dmix, "tn", tm=d, tk=DW_ROWS, name=f"mix_dw{l}")
        if l == 0:
            G["a_w_out"] = d_w_out[None]
            dmixed, dscale = _scale_bwd(dmixin, s["mixed"], s["scale"], name="scale_bwd")
            G["a_pool_scale"] = dscale
            dpooled = _mm(dmixed, pool_bd, "nt", name="pool_mix_dx")
            dpw = _mm(s["pooled"], dmixed, "tn", tm=tokw, tk=DW_ROWS, name="pool_mix_dw")
            grp = tokw // len(POOL_WINDOWS)
            G["a_pool_w"] = jnp.stack([dpw[i * grp:(i + 1) * grp, i * grp:(i + 1) * grp]
                                       for i in range(len(POOL_WINDOWS))])[None]
            dtok = _pool_bwd(dpooled.reshape(nb, seq, tokw), name="pool_bwd").reshape(n, tokw)
            extra = []
        else:
            G["b_w_out"] = d_w_out[None]
            p3 = s["proj"].reshape(nb, seq, d)
            dm3 = dmixin.reshape(nb, seq, d)
            dq3, dk3, dv3, dfk = _fox_bwd(p3, s["kvf3"], s["aq3"], s["ak3"], s["o3"], dm3, s["lse3"], tokw,
                                          name="fox_bwd")
            dtok = dq3.reshape(n, tokw)
            dfk = jnp.swapaxes(dfk[:, :, :, 0:2, :] + dfk[:, :, :, 2:4, :], 2, 3).reshape(nb, heads, seq)
            dgsum = jnp.swapaxes(dfk, 1, 2)
            dgsum = jnp.pad(dgsum, ((0, 0), (0, 0), (0, LANE - heads)))
            df3, dfb = _gate_bwd(s["kvf3"], fb, dgsum, gate_block, heads, name="gate_bwd")
            G["f_b"] = dfb[0, :heads]
            dkvf = [(dk3.reshape(n, tokw), 0, "k"), (dv3.reshape(n, tokw), tokw, "v"),
                    (df3.reshape(n, LANE), 2 * tokw, "f")]
            dx_kv = []
            for part, col0, nm in dkvf:
                dx_kv = [_mm(part, kv_w, "nt", b_col0=col0, adds=dx_kv, name=f"kvf_dx_{nm}")]
            extra = dx_kv
            G["kv_w"] = jnp.concatenate([_mm(s["x_in_mm"], part, "tn", tm=d, tk=DW_ROWS, name=f"kvf_dw_{nm}")
                                         for part, _, nm in dkvf], axis=1)[:, :kvw]
        dproj, dmemkv = _memattn_bwd(dmixin, dtok, s["proj"], s["memkv"], seq=seq, name=f"memattn_bwd{l}")
        stack["mem_w_kv"][l] = _mm(mem2d, dmemkv.reshape(nb * mlen, 2 * MEM_WIDTH), "tn", tm=d, tk=DW_ROWS,
                                   name=f"memkv_dw{l}")
        G["a_w_in" if l == 0 else "b_w_q"] = _mm(s["x_in_mm"], dproj, "tn", tm=d, tk=DW_ROWS, name=f"proj_dw{l}")[None]
        if l == 0:
            grad_x = _mm(dproj, w_in[l], "nt", adds=[dres1], name="proj_dx0")
        else:
            dx_terms = [_mm(dproj, w_in[l], "nt", adds=[dres1] + extra, name="proj_dx1")]
    for k, v in ln_g.items():
        G[k] = jnp.stack(v)
    G["mem_w_kv"] = jnp.stack(stack["mem_w_kv"])
    G["ffn_w_up"] = jnp.stack([jnp.concatenate(ug, axis=0) for ug in stack["ffn_w_up"]], axis=1)
    G["ffn_conv_w"] = jnp.stack([_unpad_cols(g, per, per_p) for g in stack["ffn_conv_w"]])
    G["ffn_conv_b"] = jnp.stack([_unpad_cols(g, per, per_p) for g in stack["ffn_conv_b"]])
    G["ffn_w_down"] = jnp.stack([g.reshape(N_CHIP, per // 2, d) for g in stack["ffn_w_down"]], axis=1)
    return loss[0, 0], grad_x.reshape(nb, seq, d), G


def kernel(x, mem, a_w_in, a_pool_w, a_pool_scale, a_w_out, b_w_q, b_w_out, kv_w, f_b, mem_w_kv, ln1_g, ln1_b, ln2_g, ln2_b, ffn_w_up, ffn_conv_w, ffn_conv_b, ffn_w_down, loss_target, m_a_w_in, m_a_pool_w, m_a_pool_scale, m_a_w_out, m_b_w_q, m_b_w_out, m_kv_w, m_f_b, m_mem_w_kv, m_ln1_g, m_ln1_b, m_ln2_g, m_ln2_b, m_ffn_w_up, m_ffn_conv_w, m_ffn_conv_b, m_ffn_w_down, v_a_w_in, v_a_pool_w, v_a_pool_scale, v_a_w_out, v_b_w_q, v_b_w_out, v_kv_w, v_f_b, v_mem_w_kv, v_ln1_g, v_ln1_b, v_ln2_g, v_ln2_b, v_ffn_w_up, v_ffn_conv_w, v_ffn_conv_b, v_ffn_w_down):
    w_loc = dict(a_w_in=a_w_in, a_pool_w=a_pool_w, a_pool_scale=a_pool_scale, a_w_out=a_w_out, b_w_q=b_w_q,
                 b_w_out=b_w_out, kv_w=kv_w, f_b=f_b, mem_w_kv=mem_w_kv, ln1_g=ln1_g, ln1_b=ln1_b, ln2_g=ln2_g,
                 ln2_b=ln2_b, ffn_w_up=ffn_w_up, ffn_conv_w=ffn_conv_w, ffn_conv_b=ffn_conv_b, ffn_w_down=ffn_w_down)
    m_loc = dict(a_w_in=m_a_w_in, a_pool_w=m_a_pool_w, a_pool_scale=m_a_pool_scale, a_w_out=m_a_w_out,
                 b_w_q=m_b_w_q, b_w_out=m_b_w_out, kv_w=m_kv_w, f_b=m_f_b, mem_w_kv=m_mem_w_kv, ln1_g=m_ln1_g,
                 ln1_b=m_ln1_b, ln2_g=m_ln2_g, ln2_b=m_ln2_b, ffn_w_up=m_ffn_w_up, ffn_conv_w=m_ffn_conv_w,
                 ffn_conv_b=m_ffn_conv_b, ffn_w_down=m_ffn_w_down)
    v_loc = dict(a_w_in=v_a_w_in, a_pool_w=v_a_pool_w, a_pool_scale=v_a_pool_scale, a_w_out=v_a_w_out,
                 b_w_q=v_b_w_q, b_w_out=v_b_w_out, kv_w=v_kv_w, f_b=v_f_b, mem_w_kv=v_mem_w_kv, ln1_g=v_ln1_g,
                 ln1_b=v_ln1_b, ln2_g=v_ln2_g, ln2_b=v_ln2_b, ffn_w_up=v_ffn_w_up, ffn_conv_w=v_ffn_conv_w,
                 ffn_conv_b=v_ffn_conv_b, ffn_w_down=v_ffn_w_down)
    x_i, y_i, c = lax.axis_index("x"), lax.axis_index("y"), lax.axis_index("c")
    q = 2 * x_i + y_i
    qc = jnp.stack([q, c]).astype(jnp.int32)
    shapes = {k: v.shape for k, v in w_loc.items()}

    def halves(a):
        if a.ndim == 3 and a.shape[0] == 2:
            return a
        rows = math.prod(a.shape[:-1])
        return a.reshape(2, rows // 2, a.shape[-1])

    own = {k: (w_loc[k] if k in GATHER_F32 else w_loc[k].astype(BF16)) for k in SHARDED}
    by_chip = ("a_pool_scale",) + tuple(k for k in BIG if k != "ffn_w_up")
    per = shapes["ffn_w_up"][-1]
    up_own = jnp.pad(own["ffn_w_up"], ((0, 0), (0, 0), (0, _round_up(per, LANE) - per)))
    gathered = _gather_shards([own["a_pool_scale"]], [halves(own[k]) for k in by_chip[1:]], [up_own],
                              name="gather_weights")
    W = {k: _from_shards(lax.dynamic_update_slice_in_dim(g.reshape((N_CHIP,) + shapes[k]), own[k][None], q, axis=0),
                         SHARD_AXIS[k])
         for k, g in zip(by_chip, gathered)}
    W["ffn_w_up_padded"] = gathered[-1]
    for k in REPLICATED:
        W[k] = w_loc[k]

    loss_part, grad_x, G = _local_step(x, mem, loss_target, W)

    g_chip = [G[k] if k in GRADS_BY_CHIP else _to_shards(G[k], SHARD_AXIS[k]) for k in BIG]
    g_chip = [g.reshape((N_CHIP,) + halves(w_loc[k]).shape) for k, g in zip(BIG, g_chip)]
    lay_r, tot_r = _layout(shapes, REPLICATED, PACK_COLS)
    rep_rows = tot_r // PACK_COLS
    rows = _round_up(rep_rows + 2, SUBLANE)
    scale_w = shapes["a_pool_scale"][-1]

    def small(rep, scale_row, scalar):
        lead = scale_row.shape[:-2]
        pad = [(0, 0)] * len(lead)
        rep = jnp.broadcast_to(_pack(rep, lay_r, tot_r).reshape(rep_rows, PACK_COLS), lead + (rep_rows, PACK_COLS))
        scale_row = jnp.pad(scale_row, pad + [(0, 0), (0, PACK_COLS - scale_w)])
        last = jnp.broadcast_to(jnp.pad(scalar.reshape(1, 1), [(0, rows - rep_rows - 2), (0, PACK_COLS - 1)]),
                                lead + (rows - rep_rows - 1, PACK_COLS))
        return jnp.concatenate([rep, scale_row, last], axis=-2)

    g_scale = jnp.repeat(_to_shards(G["a_pool_scale"], 1), 2, axis=0)
    zero = jnp.zeros((), F32)

    from_sib = _to_sibling(g_chip, name="grads_to_sibling")
    sums = [_pair_add(g, s, qc, name=f"pair_add_{k}") for k, g, s in zip(BIG, g_chip, from_sib)]
    *parts, sm_parts = _to_chips(sums, small(G, g_scale, loss_part), name="scatter_grads")
    grads = _swap_halves([_reduce_half(g, p, qc, name=f"reduce_{k}") for k, g, p in zip(BIG, sums, parts)],
                         name="swap_halves")
    out = {}
    for k, g in zip(BIG, grads):
        upd = _adamw_shard(g, halves(w_loc[k]), halves(m_loc[k]), halves(v_loc[k]), name=f"adamw_{k}")
        out[k] = [r.reshape(shapes[k]) for r in (g, *upd)]
    sm = _adamw(sm_parts, *[small(d, d["a_pool_scale"], zero) for d in (w_loc, m_loc, v_loc)], name="adamw_small")
    loss = sm[0, rep_rows + 1, 0]
    out_r = _unpack(sm[:, :rep_rows].reshape(4, tot_r), lay_r, lead=(4,))
    for k in REPLICATED:
        out[k] = [out_r[k][a] for a in range(4)]
    out["a_pool_scale"] = [sm[a, rep_rows:rep_rows + 1, :scale_w] for a in range(4)]

    outs = [loss, grad_x]
    for a in range(4):
        for k in WEIGHTS:
            outs.append(out[k][a])
    return tuple(outs)
```
